```python
import math
import jax, jax.numpy as jnp
from jax import lax
import numpy as np

D_MODEL = 1024
BATCH = 8
SEQ = 2048
DEPTH = 1

N_HEADS = 8
N_KV_HEADS = 2
HEAD_DIM = 64
Q_PER_KV = N_HEADS // N_KV_HEADS
ATTN_WIDTH = N_HEADS * HEAD_DIM
KV_WIDTH = N_KV_HEADS * HEAD_DIM
WINDOW = 128
BLOCK = 128
NUM_BUCKETS = 32
MAX_DISTANCE = 128
LRU_WIDTH = D_MODEL - ATTN_WIDTH
LRU_BLOCKS = 8
LRU_BLOCK_DIM = LRU_WIDTH // LRU_BLOCKS
LRU_C = 8.0
CONV_W = 4
CONV_LEFT = 2
MIX_WIDTH = ATTN_WIDTH + LRU_WIDTH
IN_WIDTH = ATTN_WIDTH + 2 * KV_WIDTH + 2 * LRU_WIDTH
N_GROUPS = 4
EXPERTS_PER_GROUP = 8
N_EXPERTS = N_GROUPS * EXPERTS_PER_GROUP
TOP_K = 2
D_EXPERT = 512
MOE_BLOCK = 256
EPS = 1e-6
NEG_INF = -1e30

kernel_name = "hymba_swa_rglru_hmoe_encoder"


def rmsnorm(x, g):
    xf = x.astype(jnp.float32)
    y = xf * lax.rsqrt(jnp.mean(xf * xf, axis=-1, keepdims=True) + EPS)
    return (y * g.astype(jnp.float32)).astype(x.dtype)


def t5_bucket(rel):
    half = NUM_BUCKETS // 2
    max_exact = half // 2
    base = jnp.where(rel > 0, half, 0)
    n = jnp.abs(rel)
    nf = jnp.maximum(n, 1).astype(jnp.float32)
    large = max_exact + (jnp.log(nf / max_exact) / math.log(MAX_DISTANCE / max_exact)
                         * (half - max_exact)).astype(jnp.int32)
    large = jnp.minimum(large, half - 1)
    return base + jnp.where(n < max_exact, n, large)


def windowed_attention(q, k, v, q_gain, k_gain, sink, rel_bias):
    B, S = q.shape[0], q.shape[1]
    nb = S // BLOCK
    q = rmsnorm(q, q_gain) * (HEAD_DIM ** -0.5)
    k = rmsnorm(k, k_gain)
    pad = ((0, 0), (BLOCK, BLOCK), (0, 0), (0, 0))
    kp = jnp.pad(k, pad).reshape(B, nb + 2, BLOCK, N_KV_HEADS, HEAD_DIM)
    vp = jnp.pad(v, pad).reshape(B, nb + 2, BLOCK, N_KV_HEADS, HEAD_DIM)
    kw = jnp.concatenate([kp[:, :-2], kp[:, 1:-1], kp[:, 2:]], axis=2)
    vw = jnp.concatenate([vp[:, :-2], vp[:, 1:-1], vp[:, 2:]], axis=2)
    qb = q.reshape(B, nb, BLOCK, N_KV_HEADS, Q_PER_KV, HEAD_DIM)
    s = jnp.einsum('bnqkgd,bnskd->bnkgqs', qb, kw).astype(jnp.float32)
    qi = jnp.arange(BLOCK, dtype=jnp.int32)
    kj = jnp.arange(3 * BLOCK, dtype=jnp.int32)
    rel = kj[None, :] - BLOCK - qi[:, None]
    bias = rel_bias.astype(jnp.float32)[t5_bucket(rel)]
    bias = jnp.transpose(bias, (2, 0, 1)).reshape(N_KV_HEADS, Q_PER_KV, BLOCK, 3 * BLOCK)
    key_pos = (jnp.arange(nb, dtype=jnp.int32)[:, None] - 1) * BLOCK + kj[None, :]
    valid = (jnp.abs(rel) <= WINDOW)[None] & ((key_pos >= 0) & (key_pos < S))[:, None, :]
    s = jnp.where(valid[None, :, None, None], s + bias[None, None], NEG_INF)
    sink_l = sink.astype(jnp.float32).reshape(N_KV_HEADS, Q_PER_KV)[None, None, :, :, None, None]
    m = jnp.maximum(jnp.max(s, axis=-1, keepdims=True), sink_l)
    p = jnp.exp(s - m)
    denom = jnp.sum(p, axis=-1, keepdims=True) + jnp.exp(sink_l - m)
    p = (p / denom).astype(v.dtype)
    o = jnp.einsum('bnkgqs,bnskd->bnqkgd', p, vw)
    return o.reshape(B, S, ATTN_WIDTH)


def centred_depthwise_conv(x, w, b):
    y = lax.conv_general_dilated(
        x, w[:, None, :].astype(x.dtype), window_strides=(1,),
        padding=[(CONV_LEFT, CONV_W - 1 - CONV_LEFT)],
        dimension_numbers=('NWC', 'WIO', 'NWC'), feature_group_count=x.shape[-1])
    return y + b.astype(x.dtype)


def _lin_combine(c1, c2):
    a1, b1 = c1
    a2, b2 = c2
    return a1 * a2, a2 * b1 + b2


def bidirectional_rglru(xc, w_a, b_a, w_i, b_i, lam):
    B, S, W = xc.shape
    xb = xc.reshape(B, S, LRU_BLOCKS, LRU_BLOCK_DIM)
    r = jax.nn.sigmoid((jnp.einsum('bshi,rhij->rbshj', xb, w_a).reshape(2, B, S, W)
                        + b_a[:, None, None]).astype(jnp.float32))
    i = jax.nn.sigmoid((jnp.einsum('bshi,rhij->rbshj', xb, w_i).reshape(2, B, S, W)
                        + b_i[:, None, None]).astype(jnp.float32))
    log_a = -LRU_C * r * jax.nn.softplus(-lam.astype(jnp.float32))[:, None, None]
    a = jnp.exp(log_a)
    u = jnp.sqrt(-jnp.expm1(2.0 * log_a)) * i * xc.astype(jnp.float32)[None]
    h_f = lax.associative_scan(_lin_combine, (a[0], u[0]), axis=1)[1]
    h_b = lax.associative_scan(_lin_combine, (a[1], u[1]), axis=1, reverse=True)[1]
    return h_f + h_b


def hierarchical_moe(xf, w_group, b_group, w_er, b_er, w_gate, w_up, w_down):
    T, D = xf.shape
    g_prob = jax.nn.softmax((xf @ w_group + b_group).astype(jnp.float32), axis=-1)
    g_p, g_idx = lax.top_k(g_prob, 1)
    e_logits = jnp.einsum('td,gde->tge', xf, w_er) + b_er
    e_logits = jnp.take_along_axis(e_logits, g_idx[:, :, None], axis=1)[:, 0]
    e_prob = jax.nn.softmax(e_logits.astype(jnp.float32), axis=-1)
    e_p, e_idx = lax.top_k(e_prob, TOP_K)
    gate = g_p * e_p / jnp.sum(e_p, axis=-1, keepdims=True)
    expert_id = g_idx * EXPERTS_PER_GROUP + e_idx
    A = T * TOP_K
    flat_e = expert_id.reshape(-1)
    flat_tok = jnp.repeat(jnp.arange(T, dtype=jnp.int32), TOP_K)
    flat_w = gate.reshape(-1)
    order = jnp.argsort(flat_e)
    sorted_e = flat_e[order]
    counts = jnp.bincount(flat_e, length=N_EXPERTS)
    padded = ((counts + MOE_BLOCK - 1) // MOE_BLOCK) * MOE_BLOCK
    padded_end = jnp.cumsum(padded)
    padded_start = padded_end - padded
    start = jnp.cumsum(counts) - counts
    dest = padded_start[sorted_e] + jnp.arange(A, dtype=jnp.int32) - start[sorted_e]
    cap = ((A + MOE_BLOCK - 1) // MOE_BLOCK) * MOE_BLOCK + N_EXPERTS * MOE_BLOCK
    n_blocks = cap // MOE_BLOCK
    buf_tok = jnp.zeros((cap,), jnp.int32).at[dest].set(flat_tok[order])
    buf_w = jnp.zeros((cap,), jnp.float32).at[dest].set(flat_w[order])
    block_pos = jnp.arange(n_blocks, dtype=jnp.int32) * MOE_BLOCK
    block_expert = jnp.minimum(jnp.searchsorted(padded_end, block_pos, side='right'), N_EXPERTS - 1)
    xb = xf[buf_tok].reshape(n_blocks, MOE_BLOCK, D)

    def expert_block(args):
        xblk, e = args
        h = jax.nn.silu(xblk @ w_gate[e]) * (xblk @ w_up[e])
        return h @ w_down[e]

    yb = lax.map(expert_block, (xb, block_expert)).reshape(cap, D)
    y = jnp.zeros((T, D), jnp.float32).at[buf_tok].add(yb.astype(jnp.float32) * buf_w[:, None])
    return y.astype(xf.dtype)


def setup_inputs(seed: int = 0) -> dict:
    key = jax.random.key(seed)
    ks = jax.random.split(key, 26)
    f32 = jnp.float32
    L = DEPTH
    BD = LRU_BLOCK_DIM

    def nrm(k, shape, scale):
        return scale * jax.random.normal(k, shape, f32)

    def gain(k, shape):
        return 1.0 + 0.05 * jax.random.normal(k, shape, f32)

    a0 = jax.random.uniform(ks[13], (L, 2, LRU_WIDTH), f32, 0.9, 0.999)
    s0 = a0 ** (1.0 / LRU_C)
    return {
        "x": jax.random.normal(ks[0], (BATCH, SEQ, D_MODEL), f32),
        "rel_bias": nrm(ks[1], (NUM_BUCKETS, N_HEADS), 0.5),
        "ln1": gain(ks[2], (L, D_MODEL)),
        "w_in": nrm(ks[3], (L, D_MODEL, IN_WIDTH), D_MODEL ** -0.5),
        "q_norm": gain(ks[4], (L, HEAD_DIM)),
        "k_norm": gain(ks[5], (L, HEAD_DIM)),
        "attn_sink": nrm(ks[6], (L, N_HEADS), 0.5),
        "conv_w": nrm(ks[7], (L, CONV_W, LRU_WIDTH), CONV_W ** -0.5),
        "conv_b": nrm(ks[8], (L, LRU_WIDTH), 0.02),
        "lru_wa": nrm(ks[9], (L, 2, LRU_BLOCKS, BD, BD), BD ** -0.5),
        "lru_ba": nrm(ks[10], (L, 2, LRU_WIDTH), 0.1),
        "lru_wi": nrm(ks[11], (L, 2, LRU_BLOCKS, BD, BD), BD ** -0.5),
        "lru_bi": nrm(ks[12], (L, 2, LRU_WIDTH), 0.1),
        "lru_lambda": jnp.log(s0) - jnp.log1p(-s0),
        "out_norm_attn": gain(ks[14], (L, ATTN_WIDTH)),
        "out_norm_lru": gain(ks[15], (L, LRU_WIDTH)),
        "w_out": nrm(ks[16], (L, MIX_WIDTH, D_MODEL), MIX_WIDTH ** -0.5),
        "ln2": gain(ks[17], (L, D_MODEL)),
        "w_group": nrm(ks[18], (L, D_MODEL, N_GROUPS), D_MODEL ** -0.5),
        "b_group": nrm(ks[19], (L, N_GROUPS), 0.01),
        "w_expert_router": nrm(ks[20], (L, N_GROUPS, D_MODEL, EXPERTS_PER_GROUP), D_MODEL ** -0.5),
        "b_expert_router": nrm(ks[21], (L, N_GROUPS, EXPERTS_PER_GROUP), 0.01),
        "w_gate": nrm(ks[22], (L, N_EXPERTS, D_MODEL, D_EXPERT), D_MODEL ** -0.5),
        "w_up": nrm(ks[23], (L, N_EXPERTS, D_MODEL, D_EXPERT), D_MODEL ** -0.5),
        "w_down": nrm(ks[24], (L, N_EXPERTS, D_EXPERT, D_MODEL), D_EXPERT ** -0.5),
    }


def reference(x, rel_bias, ln1, w_in, q_norm, k_norm, attn_sink, conv_w, conv_b,
              lru_wa, lru_ba, lru_wi, lru_bi, lru_lambda, out_norm_attn, out_norm_lru,
              w_out, ln2, w_group, b_group, w_expert_router, b_expert_router,
              w_gate, w_up, w_down):
    B, S, D = x.shape
    c_q = ATTN_WIDTH
    c_k = c_q + KV_WIDTH
    c_v = c_k + KV_WIDTH
    c_x = c_v + LRU_WIDTH
    for l in range(DEPTH):
        h = rmsnorm(x, ln1[l])
        proj = h @ w_in[l]
        q = proj[..., :c_q].reshape(B, S, N_HEADS, HEAD_DIM)
        k = proj[..., c_q:c_k].reshape(B, S, N_KV_HEADS, HEAD_DIM)
        v = proj[..., c_k:c_v].reshape(B, S, N_KV_HEADS, HEAD_DIM)
        xr = proj[..., c_v:c_x]
        gr = proj[..., c_x:]
        attn = windowed_attention(q, k, v, q_norm[l], k_norm[l], attn_sink[l], rel_bias)
        xc = centred_depthwise_conv(xr, conv_w[l], conv_b[l])
        lru = bidirectional_rglru(xc, lru_wa[l], lru_ba[l], lru_wi[l], lru_bi[l], lru_lambda[l])
        lru = (lru * jax.nn.gelu(gr.astype(jnp.float32))).astype(x.dtype)
        mixed = jnp.concatenate([rmsnorm(attn, out_norm_attn[l]),
                                 rmsnorm(lru, out_norm_lru[l])], axis=-1)
        x = x + mixed @ w_out[l]
        h2 = rmsnorm(x, ln2[l]).reshape(B * S, D)
        y = hierarchical_moe(h2, w_group[l], b_group[l], w_expert_router[l], b_expert_router[l],
                             w_gate[l], w_up[l], w_down[l])
        x = x + y.reshape(B, S, D)
    return x
```

```python
import functools
import math

import jax
import jax.numpy as jnp
import numpy as np
from jax import lax
from jax.experimental import pallas as pl
from jax.experimental.pallas import tpu as pltpu

D_MODEL = 1024
N_HEADS = 8
N_KV_HEADS = 2
HEAD_DIM = 64
Q_PER_KV = N_HEADS // N_KV_HEADS
ATTN_WIDTH = N_HEADS * HEAD_DIM
KV_WIDTH = N_KV_HEADS * HEAD_DIM
WINDOW = 128
BLOCK = 128
NUM_BUCKETS = 32
MAX_DISTANCE = 128
LRU_WIDTH = D_MODEL - ATTN_WIDTH
LRU_BLOCKS = 8
LRU_BLOCK_DIM = LRU_WIDTH // LRU_BLOCKS
LRU_C = 8.0
CONV_W = 4
CONV_LEFT = 2
N_GROUPS = 4
EXPERTS_PER_GROUP = 8
N_EXPERTS = N_GROUPS * EXPERTS_PER_GROUP
TOP_K = 2
D_EXPERT = 512
MOE_BLOCK = 256
EPS = 1e-6
NEG_INF = -1e30

LANES = 128
SUBLANES = 8
VMEM_LIMIT = 56 * 1024 * 1024

F32 = jnp.float32
BF16 = jnp.bfloat16


def _cparams(n_axes, vmem=VMEM_LIMIT):
    return pltpu.CompilerParams(dimension_semantics=("arbitrary",) * n_axes, vmem_limit_bytes=vmem)


def _rms(x, gain):
    return x * lax.rsqrt(jnp.mean(x * x, axis=-1, keepdims=True) + EPS) * gain


IN_TM = 512


def _in_proj_kernel(x_ref, g_ref, w_ref, q_ref, kv_ref, xr_ref, gr_ref, wb_ref):
    @pl.when(pl.program_id(0) == 0)
    def _():
        wb_ref[...] = w_ref[...].astype(BF16)

    h = _rms(x_ref[...], g_ref[...]).astype(BF16)
    c_k = ATTN_WIDTH
    c_x = c_k + 2 * KV_WIDTH
    c_g = c_x + LRU_WIDTH
    q_ref[...] = jnp.dot(h, wb_ref[:, :c_k], preferred_element_type=F32).astype(BF16)
    kv_ref[...] = jnp.dot(h, wb_ref[:, c_k:c_x], preferred_element_type=F32).astype(BF16)
    xr_ref[...] = jnp.dot(h, wb_ref[:, c_x:c_g], preferred_element_type=F32)
    gr_ref[...] = jnp.dot(h, wb_ref[:, c_g:], preferred_element_type=F32)


def _in_proj(x2, ln1, w_in):
    T = x2.shape[0]
    n_in = w_in.shape[1]
    row = lambda w: pl.BlockSpec((IN_TM, w), lambda i: (i, 0))
    return pl.pallas_call(
        _in_proj_kernel,
        grid=(T // IN_TM,),
        in_specs=[row(D_MODEL),
                  pl.BlockSpec((1, D_MODEL), lambda i: (0, 0)),
                  pl.BlockSpec((D_MODEL, n_in), lambda i: (0, 0))],
        out_specs=[row(ATTN_WIDTH), row(2 * KV_WIDTH), row(LRU_WIDTH), row(LRU_WIDTH)],
        out_shape=[jax.ShapeDtypeStruct((T, ATTN_WIDTH), BF16),
                   jax.ShapeDtypeStruct((T, 2 * KV_WIDTH), BF16),
                   jax.ShapeDtypeStruct((T, LRU_WIDTH), F32),
                   jax.ShapeDtypeStruct((T, LRU_WIDTH), F32)],
        scratch_shapes=[pltpu.VMEM((D_MODEL, n_in), BF16)],
        compiler_params=_cparams(1),
        name="in_proj",
    )(x2, ln1.reshape(1, D_MODEL), w_in)


def _t5_bucket(rel):
    half = NUM_BUCKETS // 2
    max_exact = half // 2
    base = jnp.where(rel > 0, half, 0)
    n = jnp.abs(rel)
    nf = jnp.maximum(n, 1).astype(jnp.float32)
    large = max_exact + (jnp.log(nf / max_exact) / math.log(MAX_DISTANCE / max_exact)
                         * (half - max_exact)).astype(jnp.int32)
    large = jnp.minimum(large, half - 1)
    return base + jnp.where(n < max_exact, n, large)


def _bias_kernel(rb_ref, bucket_ref, band_ref, o_ref):
    bucket = bucket_ref[...]
    band = band_ref[...] > 0
    for h in range(N_HEADS):
        acc = jnp.zeros(bucket.shape, F32)
        for b in range(NUM_BUCKETS):
            acc = jnp.where(bucket == b, rb_ref[b, h], acc)
        kv, g = divmod(h, Q_PER_KV)
        o_ref[kv, g * BLOCK:(g + 1) * BLOCK, :] = jnp.where(band, acc, NEG_INF)


def _bias_table(rel_bias):
    qi = jnp.arange(BLOCK, dtype=jnp.int32)
    kj = jnp.arange(3 * BLOCK, dtype=jnp.int32)
    rel = kj[None, :] - BLOCK - qi[:, None]
    bucket = _t5_bucket(rel).astype(jnp.int32)
    band = (jnp.abs(rel) <= WINDOW).astype(jnp.int32)
    return pl.pallas_call(
        _bias_kernel,
        in_specs=[pl.BlockSpec(memory_space=pltpu.SMEM),
                  pl.BlockSpec(memory_space=pltpu.VMEM),
                  pl.BlockSpec(memory_space=pltpu.VMEM)],
        out_specs=pl.BlockSpec(memory_space=pltpu.VMEM),
        out_shape=jax.ShapeDtypeStruct((N_KV_HEADS, Q_PER_KV * BLOCK, 3 * BLOCK), F32),
        name="bias_table",
    )(rel_bias.astype(F32), bucket, band)


def _attn_kernel(sink_ref, q_ref, kp_ref, kc_ref, kn_ref, bias_ref, qg_ref, kg_ref, og_ref, o_ref, acc_ref):
    n = pl.program_id(1)
    nb = pl.num_programs(1)
    qf = q_ref[0].astype(F32)
    kvw = jnp.concatenate([kp_ref[0], kc_ref[0], kn_ref[0]], axis=0).astype(F32)
    col = lax.broadcasted_iota(jnp.int32, (1, 3 * BLOCK), 1)
    edge_ok = ((col >= BLOCK) | (n > 0)) & ((col < 2 * BLOCK) | (n < nb - 1))
    rowi = lax.broadcasted_iota(jnp.int32, (Q_PER_KV * BLOCK, 1), 0)
    qgain = qg_ref[...] * (HEAD_DIM ** -0.5)
    for kv in range(N_KV_HEADS):
        k = _rms(kvw[:, kv * HEAD_DIM:(kv + 1) * HEAD_DIM], kg_ref[...]).astype(BF16)
        v = kvw[:, KV_WIDTH + kv * HEAD_DIM:KV_WIDTH + (kv + 1) * HEAD_DIM].astype(BF16)
        qs = []
        sink = jnp.zeros((Q_PER_KV * BLOCK, 1), F32)
        for g in range(Q_PER_KV):
            h = kv * Q_PER_KV + g
            qh = qf[:, h * HEAD_DIM:(h + 1) * HEAD_DIM]
            qs.append(_rms(qh, 1.0) * qgain)
            sink = jnp.where(rowi // BLOCK == g, sink_ref[h], sink)
        qst = jnp.concatenate(qs, axis=0).astype(BF16)
        s = lax.dot_general(qst, k, (((1,), (1,)), ((), ())), preferred_element_type=F32)
        s = jnp.where(edge_ok, s + bias_ref[kv], NEG_INF)
        m = jnp.maximum(jnp.max(s, axis=-1, keepdims=True), sink)
        p = jnp.exp(s - m)
        denom = jnp.sum(p, axis=-1, keepdims=True) + jnp.exp(sink - m)
        o = jnp.dot(p.astype(BF16), v, preferred_element_type=F32) / denom
        for g in range(Q_PER_KV):
            h = kv * Q_PER_KV + g
            acc_ref[:, h * HEAD_DIM:(h + 1) * HEAD_DIM] = o[g * BLOCK:(g + 1) * BLOCK, :]
    o_ref[0] = _rms(acc_ref[...], og_ref[...]).astype(o_ref.dtype)


def _attention(q, kv, bias_tab, sink, q_gain, k_gain, out_gain):
    B, S, _ = q.shape
    nb = S // BLOCK
    kvspec = lambda f: pl.BlockSpec((1, BLOCK, 2 * KV_WIDTH), f)
    gs = pltpu.PrefetchScalarGridSpec(
        num_scalar_prefetch=0,
        grid=(B, nb),
        in_specs=[pl.BlockSpec(memory_space=pltpu.SMEM),
                  pl.BlockSpec((1, BLOCK, ATTN_WIDTH), lambda b, n: (b, n, 0)),
                  kvspec(lambda b, n: (b, jnp.maximum(n - 1, 0), 0)),
                  kvspec(lambda b, n: (b, n, 0)),
                  kvspec(lambda b, n: (b, jnp.minimum(n + 1, nb - 1), 0)),
                  pl.BlockSpec((N_KV_HEADS, Q_PER_KV * BLOCK, 3 * BLOCK), lambda b, n: (0, 0, 0)),
                  pl.BlockSpec((1, HEAD_DIM), lambda b, n: (0, 0)),
                  pl.BlockSpec((1, HEAD_DIM), lambda b, n: (0, 0)),
                  pl.BlockSpec((1, ATTN_WIDTH), lambda b, n: (0, 0))],
        out_specs=pl.BlockSpec((1, BLOCK, ATTN_WIDTH), lambda b, n: (b, n, 0)),
        scratch_shapes=[pltpu.VMEM((BLOCK, ATTN_WIDTH), F32)],
    )
    return pl.pallas_call(
        _attn_kernel,
        grid_spec=gs,
        out_shape=jax.ShapeDtypeStruct((B, S, ATTN_WIDTH), BF16),
        compiler_params=_cparams(2),
        name="attention",
    )(sink.astype(F32), q, kv, kv, kv, bias_tab, q_gain.reshape(1, HEAD_DIM), k_gain.reshape(1, HEAD_DIM),
      out_gain.reshape(1, ATTN_WIDTH))


LRU_TC = 128
LRU_PITCH = LRU_TC + SUBLANES
LRU_SLABS = LRU_WIDTH // LANES
HALO = SUBLANES


def _softplus(x):
    return jnp.maximum(x, 0.0) + jnp.log(1.0 + jnp.exp(-jnp.abs(x)))


def _gelu_tanh(x):
    return 0.5 * x * (1.0 + jnp.tanh(math.sqrt(2.0 / math.pi) * (x + 0.044715 * (x * x * x))))


def _rglru_kernel(xr_ref, xp_ref, xn_ref, gr_ref, cw_ref, cb_ref, wg_ref, bg_ref, lam_ref, og_ref,
                  o_ref, sx_ref, a_ref, u_ref, h_ref, carry_ref, hf_ref):
    p = pl.program_id(0)
    i = pl.program_id(1)
    nc = pl.num_programs(1)
    c = i + p * (nc - 1 - 2 * i)
    B = xr_ref.shape[0]
    TC = LRU_TC

    sx_ref[:, HALO:HALO + TC, :] = xr_ref[...]
    sx_ref[:, 0:HALO, :] = jnp.where(c > 0, xp_ref[...], 0.0)
    sx_ref[:, HALO + TC:, :] = jnp.where(c < nc - 1, xn_ref[...], 0.0)
    xc = cb_ref[...][None]
    for j in range(CONV_W):
        off = HALO + j - CONV_LEFT
        xc = xc + cw_ref[j:j + 1, :][None] * sx_ref[:, off:off + TC, :]
    xc2 = xc.reshape(B * TC, LRU_WIDTH)

    g = jnp.dot(xc2.astype(BF16), wg_ref[0], preferred_element_type=F32) + bg_ref[0]
    r = jax.nn.sigmoid(g[:, :LRU_WIDTH])
    ig = jax.nn.sigmoid(g[:, LRU_WIDTH:])
    log_a = -LRU_C * r * _softplus(-lam_ref[0])
    a = jnp.exp(log_a)
    th = jnp.tanh(log_a)
    u = jnp.sqrt(-2.0 * th / (1.0 - th)) * ig * xc2
    for b in range(B):
        for s in range(LRU_SLABS):
            a_ref[s, b * LRU_PITCH:b * LRU_PITCH + TC, :] = a[b * TC:(b + 1) * TC, s * LANES:(s + 1) * LANES]
            u_ref[s, b * LRU_PITCH:b * LRU_PITCH + TC, :] = u[b * TC:(b + 1) * TC, s * LANES:(s + 1) * LANES]

    @pl.when(i == 0)
    def _():
        carry_ref[...] = jnp.zeros_like(carry_ref)

    def step(k, hs):
        t = k + p * (TC - 1 - 2 * k)
        out = []
        for s in range(LRU_SLABS):
            idx = pl.ds(t, B, stride=LRU_PITCH)
            hn = a_ref[s, idx, :] * hs[s] + u_ref[s, idx, :]
            h_ref[s, idx, :] = hn
            out.append(hn)
        return tuple(out)

    hs = lax.fori_loop(0, TC, step, tuple(carry_ref[s] for s in range(LRU_SLABS)), unroll=8)
    for s in range(LRU_SLABS):
        carry_ref[s] = hs[s]

    @pl.when(p == 0)
    def _():
        for b in range(B):
            for s in range(LRU_SLABS):
                hf_ref[c, s, b * TC:(b + 1) * TC, :] = h_ref[s, b * LRU_PITCH:b * LRU_PITCH + TC, :].astype(hf_ref.dtype)

    @pl.when(p == 1)
    def _():
        for b in range(B):
            hsum = jnp.concatenate(
                [h_ref[s, b * LRU_PITCH:b * LRU_PITCH + TC, :] + hf_ref[c, s, b * TC:(b + 1) * TC, :].astype(F32)
                 for s in range(LRU_SLABS)], axis=1)
            y = hsum * _gelu_tanh(gr_ref[b])
            o_ref[b] = _rms(y, og_ref[...]).astype(o_ref.dtype)


def _block_diag(w):
    eye = jnp.eye(LRU_BLOCKS, dtype=w.dtype)
    return jnp.einsum('hij,hg->higj', w, eye).reshape(LRU_WIDTH, LRU_WIDTH)


def _rglru(xr, gr, conv_w, conv_b, w_a, b_a, w_i, b_i, lam, out_gain):
    B, S, W = xr.shape
    nc = S // LRU_TC
    hb = LRU_TC // HALO
    wg = jnp.stack([jnp.concatenate([_block_diag(w_a[d]), _block_diag(w_i[d])], axis=1) for d in range(2)]).astype(BF16)
    bg = jnp.concatenate([b_a, b_i], axis=-1).reshape(2, 1, 2 * W).astype(F32)
    chunk = lambda p, i: i + p * (nc - 1 - 2 * i)
    full2 = lambda shape: pl.BlockSpec(shape, lambda p, i: (0,) * len(shape))
    return pl.pallas_call(
        _rglru_kernel,
        grid=(2, nc),
        in_specs=[pl.BlockSpec((B, LRU_TC, W), lambda p, i: (0, chunk(p, i), 0)),
                  pl.BlockSpec((B, HALO, W), lambda p, i: (0, jnp.maximum(chunk(p, i) * hb - 1, 0), 0)),
                  pl.BlockSpec((B, HALO, W), lambda p, i: (0, jnp.minimum((chunk(p, i) + 1) * hb, S // HALO - 1), 0)),
                  pl.BlockSpec((B, LRU_TC, W), lambda p, i: (0, chunk(p, i), 0)),
                  full2((CONV_W, W)),
                  full2((1, W)),
                  pl.BlockSpec((1, W, 2 * W), lambda p, i: (p, 0, 0)),
                  pl.BlockSpec((1, 1, 2 * W), lambda p, i: (p, 0, 0)),
                  pl.BlockSpec((1, 1, W), lambda p, i: (p, 0, 0)),
                  full2((1, W))],
        out_specs=pl.BlockSpec((B, LRU_TC, W), lambda p, i: (0, nc - 1 - p * i, 0)),
        out_shape=jax.ShapeDtypeStruct((B, S, W), BF16),
        scratch_shapes=[pltpu.VMEM((B, LRU_TC + 2 * HALO, W), F32),
                        pltpu.VMEM((LRU_SLABS, B * LRU_PITCH, LANES), F32),
                        pltpu.VMEM((LRU_SLABS, B * LRU_PITCH, LANES), F32),
                        pltpu.VMEM((LRU_SLABS, B * LRU_PITCH, LANES), F32),
                        pltpu.VMEM((LRU_SLABS, B, LANES), F32),
                        pltpu.VMEM((nc, LRU_SLABS, B * LRU_TC, LANES), BF16)],
        compiler_params=_cparams(2),
        name="rglru",
    )(xr, xr, xr, gr, conv_w.astype(F32), conv_b.reshape(1, W).astype(F32), wg, bg,
      lam.reshape(2, 1, W).astype(F32), out_gain.reshape(1, W).astype(F32))


RT_TM = 512
RT_COLS = LANES
RINFO = 8


def _split_bf16(x):
    hi = x.astype(BF16)
    lo = (x - hi.astype(F32)).astype(BF16)
    return hi, lo


def _route_kernel(an_ref, ln_ref, x_ref, wo_ref, g2_ref, wr_ref, br_ref,
                  x1_ref, h2_ref, ri_ref, cnt_ref, wob_ref, wrb_ref, tri_ref, run_ref):
    @pl.when(pl.program_id(0) == 0)
    def _():
        wob_ref[...] = wo_ref[...].astype(BF16)
        hi, lo = _split_bf16(wr_ref[...])
        wrb_ref[:, :RT_COLS] = hi
        wrb_ref[:, RT_COLS:] = lo
        r = lax.broadcasted_iota(jnp.int32, (RT_TM, RT_TM), 0)
        cidx = lax.broadcasted_iota(jnp.int32, (RT_TM, RT_TM), 1)
        tri_ref[...] = (cidx < r).astype(BF16)
        run_ref[...] = jnp.zeros_like(run_ref)

    x1 = (x_ref[...]
          + jnp.dot(an_ref[...], wob_ref[:ATTN_WIDTH, :], preferred_element_type=F32)
          + jnp.dot(ln_ref[...], wob_ref[ATTN_WIDTH:, :], preferred_element_type=F32))
    x1_ref[...] = x1
    h2 = _rms(x1, g2_ref[...])
    h2_ref[...] = h2

    hi, lo = _split_bf16(h2)
    t1 = jnp.dot(hi, wrb_ref[...], preferred_element_type=F32)
    t2 = jnp.dot(lo, wrb_ref[:, :RT_COLS], preferred_element_type=F32)
    logit = t1[:, :RT_COLS] + t1[:, RT_COLS:] + t2 + br_ref[...]

    lane = lax.broadcasted_iota(jnp.int32, logit.shape, 1)
    big = jnp.int32(4 * RT_COLS)
    is_g = lane < N_GROUPS
    gl = jnp.where(is_g, logit, -jnp.inf)
    gm = jnp.max(gl, axis=-1, keepdims=True)
    gidx = jnp.min(jnp.where(gl == gm, lane, big), axis=-1, keepdims=True)
    g_p = 1.0 / jnp.sum(jnp.where(is_g, jnp.exp(logit - gm), 0.0), axis=-1, keepdims=True)
    lo_lane = N_GROUPS + gidx * EXPERTS_PER_GROUP
    el = jnp.where((lane >= lo_lane) & (lane < lo_lane + EXPERTS_PER_GROUP), logit, -jnp.inf)
    m1 = jnp.max(el, axis=-1, keepdims=True)
    i1 = jnp.min(jnp.where(el == m1, lane, big), axis=-1, keepdims=True)
    el2 = jnp.where(lane == i1, -jnp.inf, el)
    m2 = jnp.max(el2, axis=-1, keepdims=True)
    i2 = jnp.min(jnp.where(el2 == m2, lane, big), axis=-1, keepdims=True)
    t = jnp.exp(m2 - m1)
    gate1 = g_p / (1.0 + t)
    gate2 = g_p * t / (1.0 + t)
    e1 = i1 - N_GROUPS
    e2 = i2 - N_GROUPS

    oh1 = lane == e1
    oh2 = lane == e2
    oh = (oh1 | oh2).astype(F32)
    cum = jnp.dot(tri_ref[...], oh.astype(BF16), preferred_element_type=F32) + run_ref[...]
    rank1 = jnp.sum(jnp.where(oh1, cum, 0.0), axis=-1, keepdims=True)
    rank2 = jnp.sum(jnp.where(oh2, cum, 0.0), axis=-1, keepdims=True)
    run_ref[...] = run_ref[...] + jnp.sum(oh, axis=0, keepdims=True)
    cnt_ref[...] = run_ref[...]

    lane8 = lax.broadcasted_iota(jnp.int32, (RT_TM, RINFO), 1)
    vals = [e1.astype(F32), e2.astype(F32), rank1, rank2, gate1, gate2]
    ri = jnp.zeros((RT_TM, RINFO), F32)
    for k, v in enumerate(vals):
        ri = jnp.where(lane8 == k, v, ri)
    ri_ref[...] = ri


def _out_route(attn_n, lru_n, x2, w_out, ln2, w_group, b_group, w_er, b_er):
    T = x2.shape[0]
    wr = jnp.concatenate([w_group, jnp.transpose(w_er, (1, 0, 2)).reshape(D_MODEL, N_EXPERTS)], axis=1)
    wr = jnp.pad(wr, ((0, 0), (0, RT_COLS - wr.shape[1]))).astype(F32)
    br = jnp.pad(jnp.concatenate([b_group, b_er.reshape(-1)]), (0, RT_COLS - N_GROUPS - N_EXPERTS)).reshape(1, RT_COLS)
    row = lambda w: pl.BlockSpec((RT_TM, w), lambda i: (i, 0))
    const = lambda shape: pl.BlockSpec(shape, lambda i: (0, 0))
    return pl.pallas_call(
        _route_kernel,
        grid=(T // RT_TM,),
        in_specs=[row(ATTN_WIDTH), row(LRU_WIDTH), row(D_MODEL), const((D_MODEL, D_MODEL)), const((1, D_MODEL)),
                  const((D_MODEL, RT_COLS)), const((1, RT_COLS))],
        out_specs=[row(D_MODEL), row(D_MODEL), row(RINFO), const((1, RT_COLS))],
        out_shape=[jax.ShapeDtypeStruct((T, D_MODEL), F32),
                   jax.ShapeDtypeStruct((T, D_MODEL), F32),
                   jax.ShapeDtypeStruct((T, RINFO), F32),
                   jax.ShapeDtypeStruct((1, RT_COLS), F32)],
        scratch_shapes=[pltpu.VMEM((D_MODEL, D_MODEL), BF16),
                        pltpu.VMEM((D_MODEL, 2 * RT_COLS), BF16),
                        pltpu.VMEM((RT_TM, RT_TM), BF16),
                        pltpu.VMEM((1, RT_COLS), F32)],
        compiler_params=_cparams(1),
        name="out_route",
    )(attn_n, lru_n, x2, w_out, ln2.reshape(1, D_MODEL).astype(F32), wr, br.astype(F32))


def _expert_kernel(be_ref, nu_ref, tok_ref, h2_ref, w_ref, wg_ref, wu_ref, wd_ref, o_ref,
                   xbuf, wgb, wub, wdb, sem):
    j = pl.program_id(0)
    n_used = nu_ref[0]

    @pl.when(j < n_used)
    def _():
        def issue(r, carry):
            pltpu.make_async_copy(h2_ref.at[pl.ds(tok_ref[0, 0, r], 1), :], xbuf.at[pl.ds(r, 1), :], sem).start()
            return carry
        lax.fori_loop(0, MOE_BLOCK, issue, 0, unroll=8)

        changed = jnp.logical_or(j == 0, be_ref[j] != be_ref[jnp.maximum(j - 1, 0)])

        @pl.when(changed)
        def _():
            wgb[...] = wg_ref[0].astype(BF16)
            wub[...] = wu_ref[0].astype(BF16)
            wdb[...] = wd_ref[0].astype(BF16)

        pltpu.make_async_copy(h2_ref.at[pl.ds(0, MOE_BLOCK), :], xbuf, sem).wait()
        x = xbuf[...].astype(BF16)
        g = jnp.dot(x, wgb[...], preferred_element_type=F32)
        u = jnp.dot(x, wub[...], preferred_element_type=F32)
        h = (g * jax.nn.sigmoid(g) * u).astype(BF16)
        o_ref[...] = jnp.dot(h, wdb[...], preferred_element_type=F32) * w_ref[...]


def _experts(h2, buf_tok, buf_w, block_expert, n_used, w_gate, w_up, w_down):
    cap = buf_tok.shape[0]
    n_blocks = cap // MOE_BLOCK
    last = lambda j, be, nu: jnp.minimum(j, nu[0] - 1)
    gs = pltpu.PrefetchScalarGridSpec(
        num_scalar_prefetch=2,
        grid=(n_blocks,),
        in_specs=[pl.BlockSpec((1, 1, MOE_BLOCK), lambda j, be, nu: (last(j, be, nu), 0, 0), memory_space=pltpu.SMEM),
                  pl.BlockSpec(memory_space=pl.ANY),
                  pl.BlockSpec((MOE_BLOCK, 1), lambda j, be, nu: (last(j, be, nu), 0)),
                  pl.BlockSpec((1, D_MODEL, D_EXPERT), lambda j, be, nu: (be[last(j, be, nu)], 0, 0)),
                  pl.BlockSpec((1, D_MODEL, D_EXPERT), lambda j, be, nu: (be[last(j, be, nu)], 0, 0)),
                  pl.BlockSpec((1, D_EXPERT, D_MODEL), lambda j, be, nu: (be[last(j, be, nu)], 0, 0))],
        out_specs=pl.BlockSpec((MOE_BLOCK, D_MODEL), lambda j, be, nu: (last(j, be, nu), 0)),
        scratch_shapes=[pltpu.VMEM((MOE_BLOCK, D_MODEL), F32),
                        pltpu.VMEM((D_MODEL, D_EXPERT), BF16),
                        pltpu.VMEM((D_MODEL, D_EXPERT), BF16),
                        pltpu.VMEM((D_EXPERT, D_MODEL), BF16),
                        pltpu.SemaphoreType.DMA(())],
    )
    return pl.pallas_call(
        _expert_kernel,
        grid_spec=gs,
        out_shape=jax.ShapeDtypeStruct((cap, D_MODEL), F32),
        compiler_params=_cparams(1),
        name="experts",
    )(block_expert, n_used, buf_tok.reshape(n_blocks, 1, MOE_BLOCK), h2, buf_w.reshape(cap, 1), w_gate, w_up, w_down)


CB_TM = 256


def _combine_kernel(dest_ref, x1_ref, yb_ref, o_ref, ybuf, sem):
    def issue(r, carry):
        for k in range(TOP_K):
            pltpu.make_async_copy(yb_ref.at[pl.ds(dest_ref[0, 0, TOP_K * r + k], 1), :],
                                  ybuf.at[k, pl.ds(r, 1), :], sem).start()
        return carry
    lax.fori_loop(0, CB_TM, issue, 0, unroll=8)
    for k in range(TOP_K):
        pltpu.make_async_copy(yb_ref.at[pl.ds(0, CB_TM), :], ybuf.at[k], sem).wait()
    o_ref[...] = x1_ref[...] + ybuf[0] + ybuf[1]


def _combine(x1, yb, dest):
    T = x1.shape[0]
    nt = T // CB_TM
    return pl.pallas_call(
        _combine_kernel,
        grid=(nt,),
        in_specs=[pl.BlockSpec((1, 1, TOP_K * CB_TM), lambda i: (i, 0, 0), memory_space=pltpu.SMEM),
                  pl.BlockSpec((CB_TM, D_MODEL), lambda i: (i, 0)),
                  pl.BlockSpec(memory_space=pl.ANY)],
        out_specs=pl.BlockSpec((CB_TM, D_MODEL), lambda i: (i, 0)),
        out_shape=jax.ShapeDtypeStruct((T, D_MODEL), F32),
        scratch_shapes=[pltpu.VMEM((TOP_K, CB_TM, D_MODEL), F32), pltpu.SemaphoreType.DMA(())],
        compiler_params=_cparams(1),
        name="combine",
    )(dest.reshape(nt, 1, TOP_K * CB_TM), x1, yb)


def _layer(x, rel_bias, ln1, w_in, q_norm, k_norm, attn_sink, conv_w, conv_b, lru_wa, lru_ba, lru_wi, lru_bi,
           lru_lambda, out_norm_attn, out_norm_lru, w_out, ln2, w_group, b_group, w_er, b_er, w_gate, w_up, w_down):
    B, S, D = x.shape
    T = B * S
    x2 = x.reshape(T, D)
    q, kv, xr, gr = _in_proj(x2, ln1, w_in)
    bias_tab = _bias_table(rel_bias)
    attn_n = _attention(q.reshape(B, S, ATTN_WIDTH), kv.reshape(B, S, 2 * KV_WIDTH), bias_tab, attn_sink,
                        q_norm, k_norm, out_norm_attn)
    lru_n = _rglru(xr.reshape(B, S, LRU_WIDTH), gr.reshape(B, S, LRU_WIDTH), conv_w, conv_b,
                   lru_wa, lru_ba, lru_wi, lru_bi, lru_lambda, out_norm_lru)
    x1, h2, rinfo, cnt = _out_route(attn_n.reshape(T, ATTN_WIDTH), lru_n.reshape(T, LRU_WIDTH), x2, w_out, ln2,
                                    w_group, b_group, w_er, b_er)

    e = rinfo[:, 0:2].astype(jnp.int32)
    rank = rinfo[:, 2:4].astype(jnp.int32)
    gate = rinfo[:, 4:6]
    counts = cnt[0, :N_EXPERTS].astype(jnp.int32)
    padded = ((counts + MOE_BLOCK - 1) // MOE_BLOCK) * MOE_BLOCK
    padded_end = jnp.cumsum(padded)
    padded_start = padded_end - padded
    dest = padded_start[e] + rank
    A = T * TOP_K
    cap = ((A + MOE_BLOCK - 1) // MOE_BLOCK) * MOE_BLOCK + N_EXPERTS * MOE_BLOCK
    n_blocks = cap // MOE_BLOCK
    flat_tok = jnp.repeat(jnp.arange(T, dtype=jnp.int32), TOP_K)
    buf_tok = jnp.zeros((cap,), jnp.int32).at[dest.reshape(-1)].set(flat_tok)
    buf_w = jnp.zeros((cap,), F32).at[dest.reshape(-1)].set(gate.reshape(-1))
    block_pos = jnp.arange(n_blocks, dtype=jnp.int32) * MOE_BLOCK
    block_expert = jnp.minimum(jnp.searchsorted(padded_end, block_pos, side='right'), N_EXPERTS - 1).astype(jnp.int32)
    n_used = (padded_end[-1:] // MOE_BLOCK).astype(jnp.int32)

    yb = _experts(h2, buf_tok, buf_w, block_expert, n_used, w_gate, w_up, w_down)
    out = _combine(x1, yb, dest)
    return out.reshape(B, S, D)


def kernel(x, rel_bias, ln1, w_in, q_norm, k_norm, attn_sink, conv_w, conv_b, lru_wa, lru_ba, lru_wi, lru_bi,
           lru_lambda, out_norm_attn, out_norm_lru, w_out, ln2, w_group, b_group, w_expert_router, b_expert_router,
           w_gate, w_up, w_down):
    depth = ln1.shape[0]
    for l in range(depth):
        x = _layer(x, rel_bias, ln1[l], w_in[l], q_norm[l], k_norm[l], attn_sink[l], conv_w[l], conv_b[l],
                   lru_wa[l], lru_ba[l], lru_wi[l], lru_bi[l], lru_lambda[l], out_norm_attn[l], out_norm_lru[l],
                   w_out[l], ln2[l], w_group[l], b_group[l], w_expert_router[l], b_expert_router[l],
                   w_gate[l], w_up[l], w_down[l])
    return x
```

```python
import functools
import math

import jax
import jax.numpy as jnp
import numpy as np
from jax import lax
from jax.experimental import pallas as pl
from jax.experimental.pallas import tpu as pltpu

D_MODEL = 1024
N_HEADS = 8
N_KV_HEADS = 2
HEAD_DIM = 64
Q_PER_KV = N_HEADS // N_KV_HEADS
ATTN_WIDTH = N_HEADS * HEAD_DIM
KV_WIDTH = N_KV_HEADS * HEAD_DIM
WINDOW = 128
BLOCK = 128
NUM_BUCKETS = 32
MAX_DISTANCE = 128
LRU_WIDTH = D_MODEL - ATTN_WIDTH
LRU_BLOCKS = 8
LRU_BLOCK_DIM = LRU_WIDTH // LRU_BLOCKS
LRU_C = 8.0
CONV_W = 4
CONV_LEFT = 2
N_GROUPS = 4
EXPERTS_PER_GROUP = 8
N_EXPERTS = N_GROUPS * EXPERTS_PER_GROUP
TOP_K = 2
D_EXPERT = 512
MOE_BLOCK = 256
EPS = 1e-6
NEG_INF = -1e30

LANES = 128
SUBLANES = 8
VMEM_LIMIT = 56 * 1024 * 1024

F32 = jnp.float32
BF16 = jnp.bfloat16


def _cparams(n_axes, vmem=VMEM_LIMIT):
    return pltpu.CompilerParams(dimension_semantics=("arbitrary",) * n_axes, vmem_limit_bytes=vmem)


def _rms(x, gain):
    return x * lax.rsqrt(jnp.mean(x * x, axis=-1, keepdims=True) + EPS) * gain


IN_TM = 512


def _in_proj_kernel(x_ref, g_ref, w_ref, q_ref, kv_ref, xr_ref, gr_ref, wb_ref):
    @pl.when(pl.program_id(0) == 0)
    def _():
        wb_ref[...] = w_ref[...].astype(BF16)

    h = _rms(x_ref[...], g_ref[...]).astype(BF16)
    c_k = ATTN_WIDTH
    c_x = c_k + 2 * KV_WIDTH
    c_g = c_x + LRU_WIDTH
    q_ref[...] = jnp.dot(h, wb_ref[:, :c_k], preferred_element_type=F32).astype(BF16)
    kv_ref[...] = jnp.dot(h, wb_ref[:, c_k:c_x], preferred_element_type=F32).astype(BF16)
    xr_ref[...] = jnp.dot(h, wb_ref[:, c_x:c_g], preferred_element_type=F32)
    gr_ref[...] = jnp.dot(h, wb_ref[:, c_g:], preferred_element_type=F32)


def _in_proj(x2, ln1, w_in):
    T = x2.shape[0]
    n_in = w_in.shape[1]
    row = lambda w: pl.BlockSpec((IN_TM, w), lambda i: (i, 0))
    return pl.pallas_call(
        _in_proj_kernel,
        grid=(T // IN_TM,),
        in_specs=[row(D_MODEL),
                  pl.BlockSpec((1, D_MODEL), lambda i: (0, 0)),
                  pl.BlockSpec((D_MODEL, n_in), lambda i: (0, 0))],
        out_specs=[row(ATTN_WIDTH), row(2 * KV_WIDTH), row(LRU_WIDTH), row(LRU_WIDTH)],
        out_shape=[jax.ShapeDtypeStruct((T, ATTN_WIDTH), BF16),
                   jax.ShapeDtypeStruct((T, 2 * KV_WIDTH), BF16),
                   jax.ShapeDtypeStruct((T, LRU_WIDTH), F32),
                   jax.ShapeDtypeStruct((T, LRU_WIDTH), F32)],
        scratch_shapes=[pltpu.VMEM((D_MODEL, n_in), BF16)],
        compiler_params=_cparams(1),
        name="in_proj",
    )(x2, ln1.reshape(1, D_MODEL), w_in)


def _t5_bucket(rel):
    half = NUM_BUCKETS // 2
    max_exact = half // 2
    base = jnp.where(rel > 0, half, 0)
    n = jnp.abs(rel)
    nf = jnp.maximum(n, 1).astype(jnp.float32)
    large = max_exact + (jnp.log(nf / max_exact) / math.log(MAX_DISTANCE / max_exact)
                         * (half - max_exact)).astype(jnp.int32)
    large = jnp.minimum(large, half - 1)
    return base + jnp.where(n < max_exact, n, large)


def _bias_kernel(rb_ref, bucket_ref, band_ref, o_ref):
    bucket = bucket_ref[...]
    band = band_ref[...] > 0
    for h in range(N_HEADS):
        acc = jnp.zeros(bucket.shape, F32)
        for b in range(NUM_BUCKETS):
            acc = jnp.where(bucket == b, rb_ref[b, h], acc)
        kv, g = divmod(h, Q_PER_KV)
        o_ref[kv, g * BLOCK:(g + 1) * BLOCK, :] = jnp.where(band, acc, NEG_INF)


def _bias_table(rel_bias):
    qi = jnp.arange(BLOCK, dtype=jnp.int32)
    kj = jnp.arange(3 * BLOCK, dtype=jnp.int32)
    rel = kj[None, :] - BLOCK - qi[:, None]
    bucket = _t5_bucket(rel).astype(jnp.int32)
    band = (jnp.abs(rel) <= WINDOW).astype(jnp.int32)
    return pl.pallas_call(
        _bias_kernel,
        in_specs=[pl.BlockSpec(memory_space=pltpu.SMEM),
                  pl.BlockSpec(memory_space=pltpu.VMEM),
                  pl.BlockSpec(memory_space=pltpu.VMEM)],
        out_specs=pl.BlockSpec(memory_space=pltpu.VMEM),
        out_shape=jax.ShapeDtypeStruct((N_KV_HEADS, Q_PER_KV * BLOCK, 3 * BLOCK), F32),
        name="bias_table",
    )(rel_bias.astype(F32), bucket, band)


def _attn_kernel(sink_ref, q_ref, kp_ref, kc_ref, kn_ref, bias_ref, qg_ref, kg_ref, og_ref, o_ref, acc_ref):
    n = pl.program_id(1)
    nb = pl.num_programs(1)
    qf = q_ref[0].astype(F32)
    kvw = jnp.concatenate([kp_ref[0], kc_ref[0], kn_ref[0]], axis=0).astype(F32)
    col = lax.broadcasted_iota(jnp.int32, (1, 3 * BLOCK), 1)
    edge_ok = ((col >= BLOCK) | (n > 0)) & ((col < 2 * BLOCK) | (n < nb - 1))
    rowi = lax.broadcasted_iota(jnp.int32, (Q_PER_KV * BLOCK, 1), 0)
    qgain = qg_ref[...] * (HEAD_DIM ** -0.5)
    for kv in range(N_KV_HEADS):
        k = _rms(kvw[:, kv * HEAD_DIM:(kv + 1) * HEAD_DIM], kg_ref[...]).astype(BF16)
        v = kvw[:, KV_WIDTH + kv * HEAD_DIM:KV_WIDTH + (kv + 1) * HEAD_DIM].astype(BF16)
        qs = []
        sink = jnp.zeros((Q_PER_KV * BLOCK, 1), F32)
        for g in range(Q_PER_KV):
            h = kv * Q_PER_KV + g
            qh = qf[:, h * HEAD_DIM:(h + 1) * HEAD_DIM]
            qs.append(_rms(qh, 1.0) * qgain)
            sink = jnp.where(rowi // BLOCK == g, sink_ref[h], sink)
        qst = jnp.concatenate(qs, axis=0).astype(BF16)
        s = lax.dot_general(qst, k, (((1,), (1,)), ((), ())), preferred_element_type=F32)
        s = jnp.where(edge_ok, s + bias_ref[kv], NEG_INF)
        m = jnp.maximum(jnp.max(s, axis=-1, keepdims=True), sink)
        p = jnp.exp(s - m)
        denom = jnp.sum(p, axis=-1, keepdims=True) + jnp.exp(sink - m)
        o = jnp.dot(p.astype(BF16), v, preferred_element_type=F32) / denom
        for g in range(Q_PER_KV):
            h = kv * Q_PER_KV + g
            acc_ref[:, h * HEAD_DIM:(h + 1) * HEAD_DIM] = o[g * BLOCK:(g + 1) * BLOCK, :]
    o_ref[0] = _rms(acc_ref[...], og_ref[...]).astype(o_ref.dtype)


def _attention(q, kv, bias_tab, sink, q_gain, k_gain, out_gain):
    B, S, _ = q.shape
    nb = S // BLOCK
    kvspec = lambda f: pl.BlockSpec((1, BLOCK, 2 * KV_WIDTH), f)
    gs = pltpu.PrefetchScalarGridSpec(
        num_scalar_prefetch=0,
        grid=(B, nb),
        in_specs=[pl.BlockSpec(memory_space=pltpu.SMEM),
                  pl.BlockSpec((1, BLOCK, ATTN_WIDTH), lambda b, n: (b, n, 0)),
                  kvspec(lambda b, n: (b, jnp.maximum(n - 1, 0), 0)),
                  kvspec(lambda b, n: (b, n, 0)),
                  kvspec(lambda b, n: (b, jnp.minimum(n + 1, nb - 1), 0)),
                  pl.BlockSpec((N_KV_HEADS, Q_PER_KV * BLOCK, 3 * BLOCK), lambda b, n: (0, 0, 0)),
                  pl.BlockSpec((1, HEAD_DIM), lambda b, n: (0, 0)),
                  pl.BlockSpec((1, HEAD_DIM), lambda b, n: (0, 0)),
                  pl.BlockSpec((1, ATTN_WIDTH), lambda b, n: (0, 0))],
        out_specs=pl.BlockSpec((1, BLOCK, ATTN_WIDTH), lambda b, n: (b, n, 0)),
        scratch_shapes=[pltpu.VMEM((BLOCK, ATTN_WIDTH), F32)],
    )
    return pl.pallas_call(
        _attn_kernel,
        grid_spec=gs,
        out_shape=jax.ShapeDtypeStruct((B, S, ATTN_WIDTH), BF16),
        compiler_params=_cparams(2),
        name="attention",
    )(sink.astype(F32), q, kv, kv, kv, bias_tab, q_gain.reshape(1, HEAD_DIM), k_gain.reshape(1, HEAD_DIM),
      out_gain.reshape(1, ATTN_WIDTH))


LRU_TC = 128
LRU_PITCH = LRU_TC + SUBLANES
LRU_SLABS = LRU_WIDTH // LANES
HALO = SUBLANES


def _softplus(x):
    return jnp.maximum(x, 0.0) + jnp.log(1.0 + jnp.exp(-jnp.abs(x)))


def _gelu_tanh(x):
    return 0.5 * x * (1.0 + jnp.tanh(math.sqrt(2.0 / math.pi) * (x + 0.044715 * (x * x * x))))


def _rglru_kernel(xr_ref, xp_ref, xn_ref, gr_ref, cw_ref, cb_ref, wg_ref, bg_ref, lam_ref, og_ref,
                  o_ref, sx_ref, a_ref, u_ref, h_ref, carry_ref, hf_ref):
    p = pl.program_id(0)
    i = pl.program_id(1)
    nc = pl.num_programs(1)
    c = i + p * (nc - 1 - 2 * i)
    B = xr_ref.shape[0]
    TC = LRU_TC

    sx_ref[:, HALO:HALO + TC, :] = xr_ref[...]
    sx_ref[:, 0:HALO, :] = jnp.where(c > 0, xp_ref[...], 0.0)
    sx_ref[:, HALO + TC:, :] = jnp.where(c < nc - 1, xn_ref[...], 0.0)
    xc = cb_ref[...][None]
    for j in range(CONV_W):
        off = HALO + j - CONV_LEFT
        xc = xc + cw_ref[j:j + 1, :][None] * sx_ref[:, off:off + TC, :]
    xc2 = xc.reshape(B * TC, LRU_WIDTH)

    g = jnp.dot(xc2.astype(BF16), wg_ref[0], preferred_element_type=F32) + bg_ref[0]
    r = jax.nn.sigmoid(g[:, :LRU_WIDTH])
    ig = jax.nn.sigmoid(g[:, LRU_WIDTH:])
    log_a = -LRU_C * r * _softplus(-lam_ref[0])
    a = jnp.exp(log_a)
    th = jnp.tanh(log_a)
    u = jnp.sqrt(-2.0 * th / (1.0 - th)) * ig * xc2
    for b in range(B):
        for s in range(LRU_SLABS):
            a_ref[s, b * LRU_PITCH:b * LRU_PITCH + TC, :] = a[b * TC:(b + 1) * TC, s * LANES:(s + 1) * LANES]
            u_ref[s, b * LRU_PITCH:b * LRU_PITCH + TC, :] = u[b * TC:(b + 1) * TC, s * LANES:(s + 1) * LANES]

    @pl.when(i == 0)
    def _():
        carry_ref[...] = jnp.zeros_like(carry_ref)

    def step(k, hs):
        t = k + p * (TC - 1 - 2 * k)
        out = []
        for s in range(LRU_SLABS):
            idx = pl.ds(t, B, stride=LRU_PITCH)
            hn = a_ref[s, idx, :] * hs[s] + u_ref[s, idx, :]
            h_ref[s, idx, :] = hn
            out.append(hn)
        return tuple(out)

    hs = lax.fori_loop(0, TC, step, tuple(carry_ref[s] for s in range(LRU_SLABS)), unroll=8)
    for s in range(LRU_SLABS):
        carry_ref[s] = hs[s]

    @pl.when(p == 0)
    def _():
        for b in range(B):
            for s in range(LRU_SLABS):
                hf_ref[c, s, b * TC:(b + 1) * TC, :] = h_ref[s, b * LRU_PITCH:b * LRU_PITCH + TC, :].astype(hf_ref.dtype)

    @pl.when(p == 1)
    def _():
        for b in range(B):
            hsum = jnp.concatenate(
                [h_ref[s, b * LRU_PITCH:b * LRU_PITCH + TC, :] + hf_ref[c, s, b * TC:(b + 1) * TC, :].astype(F32)
                 for s in range(LRU_SLABS)], axis=1)
            y = hsum * _gelu_tanh(gr_ref[b])
            o_ref[b] = _rms(y, og_ref[...]).astype(o_ref.dtype)


def _block_diag(w):
    eye = jnp.eye(LRU_BLOCKS, dtype=w.dtype)
    return jnp.einsum('hij,hg->higj', w, eye).reshape(LRU_WIDTH, LRU_WIDTH)


def _rglru(xr, gr, conv_w, conv_b, w_a, b_a, w_i, b_i, lam, out_gain):
    B, S, W = xr.shape
    nc = S // LRU_TC
    hb = LRU_TC // HALO
    wg = jnp.stack([jnp.concatenate([_block_diag(w_a[d]), _block_diag(w_i[d])], axis=1) for d in range(2)]).astype(BF16)
    bg = jnp.concatenate([b_a, b_i], axis=-1).reshape(2, 1, 2 * W).astype(F32)
    chunk = lambda p, i: i + p * (nc - 1 - 2 * i)
    full2 = lambda shape: pl.BlockSpec(shape, lambda p, i: (0,) * len(shape))
    return pl.pallas_call(
        _rglru_kernel,
        grid=(2, nc),
        in_specs=[pl.BlockSpec((B, LRU_TC, W), lambda p, i: (0, chunk(p, i), 0)),
                  pl.BlockSpec((B, HALO, W), lambda p, i: (0, jnp.maximum(chunk(p, i) * hb - 1, 0), 0)),
                  pl.BlockSpec((B, HALO, W), lambda p, i: (0, jnp.minimum((chunk(p, i) + 1) * hb, S // HALO - 1), 0)),
                  pl.BlockSpec((B, LRU_TC, W), lambda p, i: (0, chunk(p, i), 0)),
                  full2((CONV_W, W)),
                  full2((1, W)),
                  pl.BlockSpec((1, W, 2 * W), lambda p, i: (p, 0, 0)),
                  pl.BlockSpec((1, 1, 2 * W), lambda p, i: (p, 0, 0)),
                  pl.BlockSpec((1, 1, W), lambda p, i: (p, 0, 0)),
                  full2((1, W))],
        out_specs=pl.BlockSpec((B, LRU_TC, W), lambda p, i: (0, nc - 1 - p * i, 0)),
        out_shape=jax.ShapeDtypeStruct((B, S, W), BF16),
        scratch_shapes=[pltpu.VMEM((B, LRU_TC + 2 * HALO, W), F32),
                        pltpu.VMEM((LRU_SLABS, B * LRU_PITCH, LANES), F32),
                        pltpu.VMEM((LRU_SLABS, B * LRU_PITCH, LANES), F32),
                        pltpu.VMEM((LRU_SLABS, B * LRU_PITCH, LANES), F32),
                        pltpu.VMEM((LRU_SLABS, B, LANES), F32),
                        pltpu.VMEM((nc, LRU_SLABS, B * LRU_TC, LANES), BF16)],
        compiler_params=_cparams(2),
        name="rglru",
    )(xr, xr, xr, gr, conv_w.astype(F32), conv_b.reshape(1, W).astype(F32), wg, bg,
      lam.reshape(2, 1, W).astype(F32), out_gain.reshape(1, W).astype(F32))


RT_TM = 512
RT_COLS = LANES
RINFO = 8


def _split_bf16(x):
    hi = x.astype(BF16)
    lo = (x - hi.astype(F32)).astype(BF16)
    return hi, lo


def _route_kernel(an_ref, ln_ref, x_ref, wo_ref, g2_ref, wr_ref, br_ref,
                  x1_ref, h2_ref, gt_ref, ei_ref, cnt_ref, wob_ref, wrb_ref, tri_ref, run_ref):
    @pl.when(pl.program_id(0) == 0)
    def _():
        wob_ref[...] = wo_ref[...].astype(BF16)
        hi, lo = _split_bf16(wr_ref[...])
        wrb_ref[:, :RT_COLS] = hi
        wrb_ref[:, RT_COLS:] = lo
        r = lax.broadcasted_iota(jnp.int32, (RT_TM, RT_TM), 0)
        cidx = lax.broadcasted_iota(jnp.int32, (RT_TM, RT_TM), 1)
        tri_ref[...] = (cidx < r).astype(BF16)
        run_ref[...] = jnp.zeros_like(run_ref)

    x1 = (x_ref[...]
          + jnp.dot(an_ref[...], wob_ref[:ATTN_WIDTH, :], preferred_element_type=F32)
          + jnp.dot(ln_ref[...], wob_ref[ATTN_WIDTH:, :], preferred_element_type=F32))
    x1_ref[...] = x1
    h2 = _rms(x1, g2_ref[...])
    h2_ref[...] = h2

    hi, lo = _split_bf16(h2)
    t1 = jnp.dot(hi, wrb_ref[...], preferred_element_type=F32)
    t2 = jnp.dot(lo, wrb_ref[:, :RT_COLS], preferred_element_type=F32)
    logit = t1[:, :RT_COLS] + t1[:, RT_COLS:] + t2 + br_ref[...]

    lane = lax.broadcasted_iota(jnp.int32, logit.shape, 1)
    big = jnp.int32(4 * RT_COLS)
    is_g = lane < N_GROUPS
    gl = jnp.where(is_g, logit, -jnp.inf)
    gm = jnp.max(gl, axis=-1, keepdims=True)
    gidx = jnp.min(jnp.where(gl == gm, lane, big), axis=-1, keepdims=True)
    g_p = 1.0 / jnp.sum(jnp.where(is_g, jnp.exp(logit - gm), 0.0), axis=-1, keepdims=True)
    lo_lane = N_GROUPS + gidx * EXPERTS_PER_GROUP
    el = jnp.where((lane >= lo_lane) & (lane < lo_lane + EXPERTS_PER_GROUP), logit, -jnp.inf)
    m1 = jnp.max(el, axis=-1, keepdims=True)
    i1 = jnp.min(jnp.where(el == m1, lane, big), axis=-1, keepdims=True)
    el2 = jnp.where(lane == i1, -jnp.inf, el)
    m2 = jnp.max(el2, axis=-1, keepdims=True)
    i2 = jnp.min(jnp.where(el2 == m2, lane, big), axis=-1, keepdims=True)
    t = jnp.exp(m2 - m1)
    gate1 = g_p / (1.0 + t)
    gate2 = g_p * t / (1.0 + t)
    e1 = i1 - N_GROUPS
    e2 = i2 - N_GROUPS

    oh1 = lane == e1
    oh2 = lane == e2
    oh = (oh1 | oh2).astype(F32)
    cum = jnp.dot(tri_ref[...], oh.astype(BF16), preferred_element_type=F32) + run_ref[...]
    rank1 = jnp.sum(jnp.where(oh1, cum, 0.0), axis=-1, keepdims=True)
    rank2 = jnp.sum(jnp.where(oh2, cum, 0.0), axis=-1, keepdims=True)
    run_ref[...] = run_ref[...] + jnp.sum(oh, axis=0, keepdims=True)
    cnt_ref[...] = run_ref[...].astype(jnp.int32)

    gt_ref[...] = jnp.where(lax.broadcasted_iota(jnp.int32, (RT_TM, RINFO), 1) == 0, gate1, gate2)
    vals = [e1.astype(F32), e2.astype(F32), rank1, rank2]
    ri = jnp.zeros(logit.shape, F32)
    for k, v in enumerate(vals):
        ri = jnp.where(lane == k, v, ri)
    ei_ref[0] = ri.T[:RINFO, :].astype(jnp.int32)


def _out_route(attn_n, lru_n, x2, w_out, ln2, w_group, b_group, w_er, b_er):
    T = x2.shape[0]
    wr = jnp.concatenate([w_group, jnp.transpose(w_er, (1, 0, 2)).reshape(D_MODEL, N_EXPERTS)], axis=1)
    wr = jnp.pad(wr, ((0, 0), (0, RT_COLS - wr.shape[1]))).astype(F32)
    br = jnp.pad(jnp.concatenate([b_group, b_er.reshape(-1)]), (0, RT_COLS - N_GROUPS - N_EXPERTS)).reshape(1, RT_COLS)
    row = lambda w: pl.BlockSpec((RT_TM, w), lambda i: (i, 0))
    const = lambda shape: pl.BlockSpec(shape, lambda i: (0, 0))
    return pl.pallas_call(
        _route_kernel,
        grid=(T // RT_TM,),
        in_specs=[row(ATTN_WIDTH), row(LRU_WIDTH), row(D_MODEL), const((D_MODEL, D_MODEL)), const((1, D_MODEL)),
                  const((D_MODEL, RT_COLS)), const((1, RT_COLS))],
        out_specs=[row(D_MODEL), row(D_MODEL), row(RINFO),
                   pl.BlockSpec((1, RINFO, RT_TM), lambda i: (i, 0, 0)), const((1, RT_COLS))],
        out_shape=[jax.ShapeDtypeStruct((T, D_MODEL), F32),
                   jax.ShapeDtypeStruct((T, D_MODEL), F32),
                   jax.ShapeDtypeStruct((T, RINFO), F32),
                   jax.ShapeDtypeStruct((T // RT_TM, RINFO, RT_TM), jnp.int32),
                   jax.ShapeDtypeStruct((1, RT_COLS), jnp.int32)],
        scratch_shapes=[pltpu.VMEM((D_MODEL, D_MODEL), BF16),
                        pltpu.VMEM((D_MODEL, 2 * RT_COLS), BF16),
                        pltpu.VMEM((RT_TM, RT_TM), BF16),
                        pltpu.VMEM((1, RT_COLS), F32)],
        compiler_params=_cparams(1),
        name="out_route",
    )(attn_n, lru_n, x2, w_out, ln2.reshape(1, D_MODEL).astype(F32), wr, br.astype(F32))


def _moe_cap(T):
    A = T * TOP_K
    return ((A + MOE_BLOCK - 1) // MOE_BLOCK) * MOE_BLOCK + N_EXPERTS * MOE_BLOCK


PAD_BITS = tuple(1 << b for b in reversed(range(3, MOE_BLOCK.bit_length() - 1)))


def _dispatch_kernel(cnt_ref, ei_ref, h2_ref, xs_ref, dest_ref, be_ref, nu_ref, pstart, zeros, zsem, sem):
    i = pl.program_id(0)
    n_blocks = be_ref.shape[0]

    def pad_copies(fn):
        for e in range(N_EXPERTS):
            cnt = cnt_ref[0, e]
            head = (-cnt) & (SUBLANES - 1)
            rest = ((-cnt) & (MOE_BLOCK - 1)) - head
            off = pstart[e] + cnt
            for k in range(SUBLANES - 1):
                @pl.when(k < head)
                def _(off=off, k=k):
                    fn(pltpu.make_async_copy(zeros.at[pl.ds(0, 1), :], xs_ref.at[pl.ds(off + k, 1), :], zsem))
            off = off + head
            for bit in PAD_BITS:
                @pl.when((rest & bit) != 0)
                def _(off=off, bit=bit):
                    fn(pltpu.make_async_copy(zeros.at[pl.ds(0, bit), :],
                                             xs_ref.at[pl.ds(pl.multiple_of(off, SUBLANES), bit), :], zsem))
                off = off + (rest & bit)

    @pl.when(i == 0)
    def _():
        zeros[...] = jnp.zeros_like(zeros)

        def lay(e, carry):
            start, blk = carry
            pstart[e] = start
            nb = (cnt_ref[0, e] + MOE_BLOCK - 1) // MOE_BLOCK

            def fill(k, c):
                be_ref[blk + k] = e
                return c
            lax.fori_loop(0, nb, fill, 0)
            return start + nb * MOE_BLOCK, blk + nb
        _, used = lax.fori_loop(0, N_EXPERTS, lay, (jnp.int32(0), jnp.int32(0)))
        nu_ref[0] = used

        def tail(k, c):
            be_ref[k] = N_EXPERTS - 1
            return c
        lax.fori_loop(used, n_blocks, tail, 0)
        pad_copies(lambda cp: cp.start())
        pad_copies(lambda cp: cp.wait())

    def issue(r, carry):
        for k in range(TOP_K):
            d = pstart[ei_ref[0, k, r]] + ei_ref[0, TOP_K + k, r]
            dest_ref[0, k, r] = d
            pltpu.make_async_copy(h2_ref.at[pl.ds(r, 1), :], xs_ref.at[pl.ds(d, 1), :], sem).start()
        return carry
    lax.fori_loop(0, RT_TM, issue, 0, unroll=8)
    for k in range(TOP_K):
        pltpu.make_async_copy(h2_ref, xs_ref.at[pl.ds(0, RT_TM), :], sem).wait()


def _dispatch(h2, ei, cnt):
    T = h2.shape[0]
    cap = _moe_cap(T)
    n_blocks = cap // MOE_BLOCK
    nt = T // RT_TM
    smem = pl.BlockSpec(memory_space=pltpu.SMEM)
    return pl.pallas_call(
        _dispatch_kernel,
        grid=(nt,),
        in_specs=[smem,
                  pl.BlockSpec((1, RINFO, RT_TM), lambda i: (i, 0, 0), memory_space=pltpu.SMEM),
                  pl.BlockSpec((RT_TM, D_MODEL), lambda i: (i, 0))],
        out_specs=[pl.BlockSpec(memory_space=pl.ANY),
                   pl.BlockSpec((1, TOP_K, RT_TM), lambda i: (i, 0, 0), memory_space=pltpu.SMEM),
                   smem, smem],
        out_shape=[jax.ShapeDtypeStruct((cap, D_MODEL), F32),
                   jax.ShapeDtypeStruct((nt, TOP_K, RT_TM), jnp.int32),
                   jax.ShapeDtypeStruct((n_blocks,), jnp.int32),
                   jax.ShapeDtypeStruct((1,), jnp.int32)],
        scratch_shapes=[pltpu.SMEM((N_EXPERTS,), jnp.int32),
                        pltpu.VMEM((MOE_BLOCK // 2, D_MODEL), F32),
                        pltpu.SemaphoreType.DMA(()),
                        pltpu.SemaphoreType.DMA(())],
        compiler_params=_cparams(1),
        name="dispatch",
    )(cnt, ei, h2)


def _expert_kernel(be_ref, nu_ref, x_ref, wg_ref, wu_ref, wd_ref, o_ref, wgb, wub, wdb):
    j = pl.program_id(0)

    @pl.when(j < nu_ref[0])
    def _():
        changed = jnp.logical_or(j == 0, be_ref[j] != be_ref[jnp.maximum(j - 1, 0)])

        @pl.when(changed)
        def _():
            wgb[...] = wg_ref[0].astype(BF16)
            wub[...] = wu_ref[0].astype(BF16)
            wdb[...] = wd_ref[0].astype(BF16)

        x = x_ref[...].astype(BF16)
        g = jnp.dot(x, wgb[...], preferred_element_type=F32)
        u = jnp.dot(x, wub[...], preferred_element_type=F32)
        h = (g * jax.nn.sigmoid(g) * u).astype(BF16)
        o_ref[...] = jnp.dot(h, wdb[...], preferred_element_type=F32)


def _experts(xs, block_expert, n_used, w_gate, w_up, w_down):
    cap = xs.shape[0]
    n_blocks = cap // MOE_BLOCK
    last = lambda j, be, nu: jnp.minimum(j, nu[0] - 1)
    wspec = lambda shape: pl.BlockSpec((1,) + shape, lambda j, be, nu: (be[last(j, be, nu)], 0, 0))
    gs = pltpu.PrefetchScalarGridSpec(
        num_scalar_prefetch=2,
        grid=(n_blocks,),
        in_specs=[pl.BlockSpec((MOE_BLOCK, D_MODEL), lambda j, be, nu: (last(j, be, nu), 0)),
                  wspec((D_MODEL, D_EXPERT)), wspec((D_MODEL, D_EXPERT)), wspec((D_EXPERT, D_MODEL))],
        out_specs=pl.BlockSpec((MOE_BLOCK, D_MODEL), lambda j, be, nu: (last(j, be, nu), 0)),
        scratch_shapes=[pltpu.VMEM((D_MODEL, D_EXPERT), BF16),
                        pltpu.VMEM((D_MODEL, D_EXPERT), BF16),
                        pltpu.VMEM((D_EXPERT, D_MODEL), BF16)],
    )
    return pl.pallas_call(
        _expert_kernel,
        grid_spec=gs,
        out_shape=jax.ShapeDtypeStruct((cap, D_MODEL), F32),
        compiler_params=_cparams(1),
        name="experts",
    )(block_expert, n_used, xs, w_gate, w_up, w_down)


CB_TM = RT_TM
CB_SLOTS = 2


def _combine_kernel(dcur_ref, dnxt_ref, x1_ref, gt_ref, yb_ref, o_ref, ybuf, sems):
    i = pl.program_id(0)
    nt = pl.num_programs(0)
    slot = i % CB_SLOTS

    def gather(dref, s):
        def issue(r, carry):
            for k in range(TOP_K):
                pltpu.make_async_copy(yb_ref.at[pl.ds(dref[0, k, r], 1), :],
                                      ybuf.at[s, k, pl.ds(r, 1), :], sems.at[s]).start()
            return carry
        lax.fori_loop(0, CB_TM, issue, 0, unroll=8)

    @pl.when(i == 0)
    def _():
        gather(dcur_ref, 0)

    @pl.when(i + 1 < nt)
    def _():
        gather(dnxt_ref, 1 - slot)

    for k in range(TOP_K):
        pltpu.make_async_copy(yb_ref.at[pl.ds(0, CB_TM), :], ybuf.at[slot, k], sems.at[slot]).wait()
    g = gt_ref[...]
    o_ref[...] = x1_ref[...] + g[:, 0:1] * ybuf[slot, 0] + g[:, 1:2] * ybuf[slot, 1]


def _combine(x1, gates, yb, dest):
    T = x1.shape[0]
    nt = T // CB_TM
    dspec = lambda f: pl.BlockSpec((1, TOP_K, CB_TM), f, memory_space=pltpu.SMEM)
    return pl.pallas_call(
        _combine_kernel,
        grid=(nt,),
        in_specs=[dspec(lambda i: (i, 0, 0)),
                  dspec(lambda i: (jnp.minimum(i + 1, nt - 1), 0, 0)),
                  pl.BlockSpec((CB_TM, D_MODEL), lambda i: (i, 0)),
                  pl.BlockSpec((CB_TM, RINFO), lambda i: (i, 0)),
                  pl.BlockSpec(memory_space=pl.ANY)],
        out_specs=pl.BlockSpec((CB_TM, D_MODEL), lambda i: (i, 0)),
        out_shape=jax.ShapeDtypeStruct((T, D_MODEL), F32),
        scratch_shapes=[pltpu.VMEM((CB_SLOTS, TOP_K, CB_TM, D_MODEL), F32), pltpu.SemaphoreType.DMA((CB_SLOTS,))],
        compiler_params=_cparams(1),
        name="combine",
    )(dest, dest, x1, gates, yb)


def _layer(x, rel_bias, ln1, w_in, q_norm, k_norm, attn_sink, conv_w, conv_b, lru_wa, lru_ba, lru_wi, lru_bi,
           lru_lambda, out_norm_attn, out_norm_lru, w_out, ln2, w_group, b_group, w_er, b_er, w_gate, w_up, w_down):
    B, S, D = x.shape
    T = B * S
    x2 = x.reshape(T, D)
    q, kv, xr, gr = _in_proj(x2, ln1, w_in)
    bias_tab = _bias_table(rel_bias)
    attn_n = _attention(q.reshape(B, S, ATTN_WIDTH), kv.reshape(B, S, 2 * KV_WIDTH), bias_tab, attn_sink,
                        q_norm, k_norm, out_norm_attn)
    lru_n = _rglru(xr.reshape(B, S, LRU_WIDTH), gr.reshape(B, S, LRU_WIDTH), conv_w, conv_b,
                   lru_wa, lru_ba, lru_wi, lru_bi, lru_lambda, out_norm_lru)
    x1, h2, gates, ei, cnt = _out_route(attn_n.reshape(T, ATTN_WIDTH), lru_n.reshape(T, LRU_WIDTH), x2, w_out, ln2,
                                        w_group, b_group, w_er, b_er)
    xs, dest, block_expert, n_used = _dispatch(h2, ei, cnt)
    yb = _experts(xs, block_expert, n_used, w_gate, w_up, w_down)
    out = _combine(x1, gates, yb, dest)
    return out.reshape(B, S, D)


def kernel(x, rel_bias, ln1, w_in, q_norm, k_norm, attn_sink, conv_w, conv_b, lru_wa, lru_ba, lru_wi, lru_bi,
           lru_lambda, out_norm_attn, out_norm_lru, w_out, ln2, w_group, b_group, w_expert_router, b_expert_router,
           w_gate, w_up, w_down):
    depth = ln1.shape[0]
    for l in range(depth):
        x = _layer(x, rel_bias, ln1[l], w_in[l], q_norm[l], k_norm[l], attn_sink[l], conv_w[l], conv_b[l],
                   lru_wa[l], lru_ba[l], lru_wi[l], lru_bi[l], lru_lambda[l], out_norm_attn[l], out_norm_lru[l],
                   w_out[l], ln2[l], w_group[l], b_group[l], w_expert_router[l], b_expert_router[l],
                   w_gate[l], w_up[l], w_down[l])
    return x
```

```python
import functools
import math

import jax
import jax.numpy as jnp
import numpy as np
from jax import lax
from jax.experimental import pallas as pl
from jax.experimental.pallas import tpu as pltpu

D_MODEL = 1024
N_HEADS = 8
N_KV_HEADS = 2
HEAD_DIM = 64
Q_PER_KV = N_HEADS // N_KV_HEADS
ATTN_WIDTH = N_HEADS * HEAD_DIM
KV_WIDTH = N_KV_HEADS * HEAD_DIM
WINDOW = 128
BLOCK = 128
NUM_BUCKETS = 32
MAX_DISTANCE = 128
LRU_WIDTH = D_MODEL - ATTN_WIDTH
LRU_BLOCKS = 8
LRU_BLOCK_DIM = LRU_WIDTH // LRU_BLOCKS
LRU_C = 8.0
CONV_W = 4
CONV_LEFT = 2
N_GROUPS = 4
EXPERTS_PER_GROUP = 8
N_EXPERTS = N_GROUPS * EXPERTS_PER_GROUP
TOP_K = 2
D_EXPERT = 512
MOE_BLOCK = 256
EPS = 1e-6
NEG_INF = -1e30

LANES = 128
SUBLANES = 8
VMEM_LIMIT = 56 * 1024 * 1024

F32 = jnp.float32
BF16 = jnp.bfloat16


def _cparams(n_axes, vmem=VMEM_LIMIT):
    return pltpu.CompilerParams(dimension_semantics=("arbitrary",) * n_axes, vmem_limit_bytes=vmem)


def _rms(x, gain):
    return x * lax.rsqrt(jnp.mean(x * x, axis=-1, keepdims=True) + EPS) * gain


U32 = jnp.uint32
HI_MASK = 0xFFFF0000
PACKED = D_MODEL // 2


def _pack_rows(x):
    h = x.shape[1] // 2
    lo = lax.bitcast_convert_type(x[:, :h].astype(BF16).astype(F32), U32) >> 16
    hi = lax.bitcast_convert_type(x[:, h:].astype(BF16).astype(F32), U32) & jnp.uint32(HI_MASK)
    return lo | hi


def _unpack_rows(p):
    lo = lax.bitcast_convert_type(p << 16, F32)
    hi = lax.bitcast_convert_type(p & jnp.uint32(HI_MASK), F32)
    return lo, hi


IN_TM = 512


def _in_proj_kernel(x_ref, g_ref, w_ref, q_ref, kv_ref, xr_ref, gr_ref, wb_ref):
    @pl.when(pl.program_id(0) == 0)
    def _():
        wb_ref[...] = w_ref[...].astype(BF16)

    h = _rms(x_ref[...], g_ref[...]).astype(BF16)
    c_k = ATTN_WIDTH
    c_x = c_k + 2 * KV_WIDTH
    c_g = c_x + LRU_WIDTH
    q_ref[...] = jnp.dot(h, wb_ref[:, :c_k], preferred_element_type=F32).astype(BF16)
    kv_ref[...] = jnp.dot(h, wb_ref[:, c_k:c_x], preferred_element_type=F32).astype(BF16)
    xr_ref[...] = jnp.dot(h, wb_ref[:, c_x:c_g], preferred_element_type=F32)
    gr_ref[...] = jnp.dot(h, wb_ref[:, c_g:], preferred_element_type=F32)


def _in_proj(x2, ln1, w_in):
    T = x2.shape[0]
    n_in = w_in.shape[1]
    row = lambda w: pl.BlockSpec((IN_TM, w), lambda i: (i, 0))
    return pl.pallas_call(
        _in_proj_kernel,
        grid=(T // IN_TM,),
        in_specs=[row(D_MODEL),
                  pl.BlockSpec((1, D_MODEL), lambda i: (0, 0)),
                  pl.BlockSpec((D_MODEL, n_in), lambda i: (0, 0))],
        out_specs=[row(ATTN_WIDTH), row(2 * KV_WIDTH), row(LRU_WIDTH), row(LRU_WIDTH)],
        out_shape=[jax.ShapeDtypeStruct((T, ATTN_WIDTH), BF16),
                   jax.ShapeDtypeStruct((T, 2 * KV_WIDTH), BF16),
                   jax.ShapeDtypeStruct((T, LRU_WIDTH), F32),
                   jax.ShapeDtypeStruct((T, LRU_WIDTH), F32)],
        scratch_shapes=[pltpu.VMEM((D_MODEL, n_in), BF16)],
        compiler_params=_cparams(1),
        name="in_proj",
    )(x2, ln1.reshape(1, D_MODEL), w_in)


def _t5_bucket(rel):
    half = NUM_BUCKETS // 2
    max_exact = half // 2
    base = jnp.where(rel > 0, half, 0)
    n = jnp.abs(rel)
    nf = jnp.maximum(n, 1).astype(jnp.float32)
    large = max_exact + (jnp.log(nf / max_exact) / math.log(MAX_DISTANCE / max_exact)
                         * (half - max_exact)).astype(jnp.int32)
    large = jnp.minimum(large, half - 1)
    return base + jnp.where(n < max_exact, n, large)


def _bias_kernel(rb_ref, bucket_ref, band_ref, o_ref):
    bucket = bucket_ref[...]
    band = band_ref[...] > 0
    for h in range(N_HEADS):
        acc = jnp.zeros(bucket.shape, F32)
        for b in range(NUM_BUCKETS):
            acc = jnp.where(bucket == b, rb_ref[b, h], acc)
        kv, g = divmod(h, Q_PER_KV)
        o_ref[kv, g * BLOCK:(g + 1) * BLOCK, :] = jnp.where(band, acc, NEG_INF)


def _bias_table(rel_bias):
    qi = jnp.arange(BLOCK, dtype=jnp.int32)
    kj = jnp.arange(3 * BLOCK, dtype=jnp.int32)
    rel = kj[None, :] - BLOCK - qi[:, None]
    bucket = _t5_bucket(rel).astype(jnp.int32)
    band = (jnp.abs(rel) <= WINDOW).astype(jnp.int32)
    return pl.pallas_call(
        _bias_kernel,
        in_specs=[pl.BlockSpec(memory_space=pltpu.SMEM),
                  pl.BlockSpec(memory_space=pltpu.VMEM),
                  pl.BlockSpec(memory_space=pltpu.VMEM)],
        out_specs=pl.BlockSpec(memory_space=pltpu.VMEM),
        out_shape=jax.ShapeDtypeStruct((N_KV_HEADS, Q_PER_KV * BLOCK, 3 * BLOCK), F32),
        name="bias_table",
    )(rel_bias.astype(F32), bucket, band)


def _attn_kernel(sink_ref, q_ref, kp_ref, kc_ref, kn_ref, bias_ref, qg_ref, kg_ref, og_ref, o_ref, acc_ref):
    n = pl.program_id(1)
    nb = pl.num_programs(1)
    qf = q_ref[0].astype(F32)
    kvw = jnp.concatenate([kp_ref[0], kc_ref[0], kn_ref[0]], axis=0).astype(F32)
    col = lax.broadcasted_iota(jnp.int32, (1, 3 * BLOCK), 1)
    edge_ok = ((col >= BLOCK) | (n > 0)) & ((col < 2 * BLOCK) | (n < nb - 1))
    rowi = lax.broadcasted_iota(jnp.int32, (Q_PER_KV * BLOCK, 1), 0)
    qgain = qg_ref[...] * (HEAD_DIM ** -0.5)
    for kv in range(N_KV_HEADS):
        k = _rms(kvw[:, kv * HEAD_DIM:(kv + 1) * HEAD_DIM], kg_ref[...]).astype(BF16)
        v = kvw[:, KV_WIDTH + kv * HEAD_DIM:KV_WIDTH + (kv + 1) * HEAD_DIM].astype(BF16)
        qs = []
        sink = jnp.zeros((Q_PER_KV * BLOCK, 1), F32)
        for g in range(Q_PER_KV):
            h = kv * Q_PER_KV + g
            qh = qf[:, h * HEAD_DIM:(h + 1) * HEAD_DIM]
            qs.append(_rms(qh, 1.0) * qgain)
            sink = jnp.where(rowi // BLOCK == g, sink_ref[h], sink)
        qst = jnp.concatenate(qs, axis=0).astype(BF16)
        s = lax.dot_general(qst, k, (((1,), (1,)), ((), ())), preferred_element_type=F32)
        s = jnp.where(edge_ok, s + bias_ref[kv], NEG_INF)
        m = jnp.maximum(jnp.max(s, axis=-1, keepdims=True), sink)
        p = jnp.exp(s - m)
        denom = jnp.sum(p, axis=-1, keepdims=True) + jnp.exp(sink - m)
        o = jnp.dot(p.astype(BF16), v, preferred_element_type=F32) / denom
        for g in range(Q_PER_KV):
            h = kv * Q_PER_KV + g
            acc_ref[:, h * HEAD_DIM:(h + 1) * HEAD_DIM] = o[g * BLOCK:(g + 1) * BLOCK, :]
    o_ref[0] = _rms(acc_ref[...], og_ref[...]).astype(o_ref.dtype)


def _attention(q, kv, bias_tab, sink, q_gain, k_gain, out_gain):
    B, S, _ = q.shape
    nb = S // BLOCK
    kvspec = lambda f: pl.BlockSpec((1, BLOCK, 2 * KV_WIDTH), f)
    gs = pltpu.PrefetchScalarGridSpec(
        num_scalar_prefetch=0,
        grid=(B, nb),
        in_specs=[pl.BlockSpec(memory_space=pltpu.SMEM),
                  pl.BlockSpec((1, BLOCK, ATTN_WIDTH), lambda b, n: (b, n, 0)),
                  kvspec(lambda b, n: (b, jnp.maximum(n - 1, 0), 0)),
                  kvspec(lambda b, n: (b, n, 0)),
                  kvspec(lambda b, n: (b, jnp.minimum(n + 1, nb - 1), 0)),
                  pl.BlockSpec((N_KV_HEADS, Q_PER_KV * BLOCK, 3 * BLOCK), lambda b, n: (0, 0, 0)),
                  pl.BlockSpec((1, HEAD_DIM), lambda b, n: (0, 0)),
                  pl.BlockSpec((1, HEAD_DIM), lambda b, n: (0, 0)),
                  pl.BlockSpec((1, ATTN_WIDTH), lambda b, n: (0, 0))],
        out_specs=pl.BlockSpec((1, BLOCK, ATTN_WIDTH), lambda b, n: (b, n, 0)),
        scratch_shapes=[pltpu.VMEM((BLOCK, ATTN_WIDTH), F32)],
    )
    return pl.pallas_call(
        _attn_kernel,
        grid_spec=gs,
        out_shape=jax.ShapeDtypeStruct((B, S, ATTN_WIDTH), BF16),
        compiler_params=_cparams(2),
        name="attention",
    )(sink.astype(F32), q, kv, kv, kv, bias_tab, q_gain.reshape(1, HEAD_DIM), k_gain.reshape(1, HEAD_DIM),
      out_gain.reshape(1, ATTN_WIDTH))


LRU_TC = 128
LRU_PITCH = LRU_TC + SUBLANES
LRU_SLABS = LRU_WIDTH // LANES
HALO = SUBLANES


def _softplus(x):
    return jnp.maximum(x, 0.0) + jnp.log(1.0 + jnp.exp(-jnp.abs(x)))


def _gelu_tanh(x):
    return 0.5 * x * (1.0 + jnp.tanh(math.sqrt(2.0 / math.pi) * (x + 0.044715 * (x * x * x))))


def _rglru_kernel(xr_ref, xp_ref, xn_ref, gr_ref, cw_ref, cb_ref, wg_ref, bg_ref, lam_ref, og_ref,
                  o_ref, sx_ref, a_ref, u_ref, h_ref, carry_ref, hf_ref):
    p = pl.program_id(0)
    i = pl.program_id(1)
    nc = pl.num_programs(1)
    c = i + p * (nc - 1 - 2 * i)
    B = xr_ref.shape[0]
    TC = LRU_TC

    sx_ref[:, HALO:HALO + TC, :] = xr_ref[...]
    sx_ref[:, 0:HALO, :] = jnp.where(c > 0, xp_ref[...], 0.0)
    sx_ref[:, HALO + TC:, :] = jnp.where(c < nc - 1, xn_ref[...], 0.0)
    xc = cb_ref[...][None]
    for j in range(CONV_W):
        off = HALO + j - CONV_LEFT
        xc = xc + cw_ref[j:j + 1, :][None] * sx_ref[:, off:off + TC, :]
    xc2 = xc.reshape(B * TC, LRU_WIDTH)

    g = jnp.dot(xc2.astype(BF16), wg_ref[0], preferred_element_type=F32) + bg_ref[0]
    r = jax.nn.sigmoid(g[:, :LRU_WIDTH])
    ig = jax.nn.sigmoid(g[:, LRU_WIDTH:])
    log_a = -LRU_C * r * _softplus(-lam_ref[0])
    a = jnp.exp(log_a)
    th = jnp.tanh(log_a)
    u = jnp.sqrt(-2.0 * th / (1.0 - th)) * ig * xc2
    for b in range(B):
        for s in range(LRU_SLABS):
            a_ref[s, b * LRU_PITCH:b * LRU_PITCH + TC, :] = a[b * TC:(b + 1) * TC, s * LANES:(s + 1) * LANES]
            u_ref[s, b * LRU_PITCH:b * LRU_PITCH + TC, :] = u[b * TC:(b + 1) * TC, s * LANES:(s + 1) * LANES]

    @pl.when(i == 0)
    def _():
        carry_ref[...] = jnp.zeros_like(carry_ref)

    def step(k, hs):
        t = k + p * (TC - 1 - 2 * k)
        out = []
        for s in range(LRU_SLABS):
            idx = pl.ds(t, B, stride=LRU_PITCH)
            hn = a_ref[s, idx, :] * hs[s] + u_ref[s, idx, :]
            h_ref[s, idx, :] = hn
            out.append(hn)
        return tuple(out)

    hs = lax.fori_loop(0, TC, step, tuple(carry_ref[s] for s in range(LRU_SLABS)), unroll=8)
    for s in range(LRU_SLABS):
        carry_ref[s] = hs[s]

    @pl.when(p == 0)
    def _():
        for b in range(B):
            for s in range(LRU_SLABS):
                hf_ref[c, s, b * TC:(b + 1) * TC, :] = h_ref[s, b * LRU_PITCH:b * LRU_PITCH + TC, :].astype(hf_ref.dtype)

    @pl.when(p == 1)
    def _():
        for b in range(B):
            hsum = jnp.concatenate(
                [h_ref[s, b * LRU_PITCH:b * LRU_PITCH + TC, :] + hf_ref[c, s, b * TC:(b + 1) * TC, :].astype(F32)
                 for s in range(LRU_SLABS)], axis=1)
            y = hsum * _gelu_tanh(gr_ref[b])
            o_ref[b] = _rms(y, og_ref[...]).astype(o_ref.dtype)


def _block_diag(w):
    eye = jnp.eye(LRU_BLOCKS, dtype=w.dtype)
    return jnp.einsum('hij,hg->higj', w, eye).reshape(LRU_WIDTH, LRU_WIDTH)


def _rglru(xr, gr, conv_w, conv_b, w_a, b_a, w_i, b_i, lam, out_gain):
    B, S, W = xr.shape
    nc = S // LRU_TC
    hb = LRU_TC // HALO
    wg = jnp.stack([jnp.concatenate([_block_diag(w_a[d]), _block_diag(w_i[d])], axis=1) for d in range(2)]).astype(BF16)
    bg = jnp.concatenate([b_a, b_i], axis=-1).reshape(2, 1, 2 * W).astype(F32)
    chunk = lambda p, i: i + p * (nc - 1 - 2 * i)
    full2 = lambda shape: pl.BlockSpec(shape, lambda p, i: (0,) * len(shape))
    return pl.pallas_call(
        _rglru_kernel,
        grid=(2, nc),
        in_specs=[pl.BlockSpec((B, LRU_TC, W), lambda p, i: (0, chunk(p, i), 0)),
                  pl.BlockSpec((B, HALO, W), lambda p, i: (0, jnp.maximum(chunk(p, i) * hb - 1, 0), 0)),
                  pl.BlockSpec((B, HALO, W), lambda p, i: (0, jnp.minimum((chunk(p, i) + 1) * hb, S // HALO - 1), 0)),
                  pl.BlockSpec((B, LRU_TC, W), lambda p, i: (0, chunk(p, i), 0)),
                  full2((CONV_W, W)),
                  full2((1, W)),
                  pl.BlockSpec((1, W, 2 * W), lambda p, i: (p, 0, 0)),
                  pl.BlockSpec((1, 1, 2 * W), lambda p, i: (p, 0, 0)),
                  pl.BlockSpec((1, 1, W), lambda p, i: (p, 0, 0)),
                  full2((1, W))],
        out_specs=pl.BlockSpec((B, LRU_TC, W), lambda p, i: (0, nc - 1 - p * i, 0)),
        out_shape=jax.ShapeDtypeStruct((B, S, W), BF16),
        scratch_shapes=[pltpu.VMEM((B, LRU_TC + 2 * HALO, W), F32),
                        pltpu.VMEM((LRU_SLABS, B * LRU_PITCH, LANES), F32),
                        pltpu.VMEM((LRU_SLABS, B * LRU_PITCH, LANES), F32),
                        pltpu.VMEM((LRU_SLABS, B * LRU_PITCH, LANES), F32),
                        pltpu.VMEM((LRU_SLABS, B, LANES), F32),
                        pltpu.VMEM((nc, LRU_SLABS, B * LRU_TC, LANES), BF16)],
        compiler_params=_cparams(2),
        name="rglru",
    )(xr, xr, xr, gr, conv_w.astype(F32), conv_b.reshape(1, W).astype(F32), wg, bg,
      lam.reshape(2, 1, W).astype(F32), out_gain.reshape(1, W).astype(F32))


RT_TM = 512
RT_COLS = LANES
RINFO = 8


def _split_bf16(x):
    hi = x.astype(BF16)
    lo = (x - hi.astype(F32)).astype(BF16)
    return hi, lo


def _route_kernel(an_ref, ln_ref, x_ref, wo_ref, g2_ref, wr_ref, br_ref,
                  x1_ref, h2_ref, gt_ref, ei_ref, cnt_ref, wob_ref, wrb_ref, tri_ref, run_ref):
    @pl.when(pl.program_id(0) == 0)
    def _():
        wob_ref[...] = wo_ref[...].astype(BF16)
        hi, lo = _split_bf16(wr_ref[...])
        wrb_ref[:, :RT_COLS] = hi
        wrb_ref[:, RT_COLS:] = lo
        r = lax.broadcasted_iota(jnp.int32, (RT_TM, RT_TM), 0)
        cidx = lax.broadcasted_iota(jnp.int32, (RT_TM, RT_TM), 1)
        tri_ref[...] = (cidx < r).astype(BF16)
        run_ref[...] = jnp.zeros_like(run_ref)

    x1 = (x_ref[...]
          + jnp.dot(an_ref[...], wob_ref[:ATTN_WIDTH, :], preferred_element_type=F32)
          + jnp.dot(ln_ref[...], wob_ref[ATTN_WIDTH:, :], preferred_element_type=F32))
    x1_ref[...] = x1
    h2 = _rms(x1, g2_ref[...])
    h2_ref[...] = _pack_rows(h2)

    hi, lo = _split_bf16(h2)
    t1 = jnp.dot(hi, wrb_ref[...], preferred_element_type=F32)
    t2 = jnp.dot(lo, wrb_ref[:, :RT_COLS], preferred_element_type=F32)
    logit = t1[:, :RT_COLS] + t1[:, RT_COLS:] + t2 + br_ref[...]

    lane = lax.broadcasted_iota(jnp.int32, logit.shape, 1)
    big = jnp.int32(4 * RT_COLS)
    is_g = lane < N_GROUPS
    gl = jnp.where(is_g, logit, -jnp.inf)
    gm = jnp.max(gl, axis=-1, keepdims=True)
    gidx = jnp.min(jnp.where(gl == gm, lane, big), axis=-1, keepdims=True)
    g_p = 1.0 / jnp.sum(jnp.where(is_g, jnp.exp(logit - gm), 0.0), axis=-1, keepdims=True)
    lo_lane = N_GROUPS + gidx * EXPERTS_PER_GROUP
    el = jnp.where((lane >= lo_lane) & (lane < lo_lane + EXPERTS_PER_GROUP), logit, -jnp.inf)
    m1 = jnp.max(el, axis=-1, keepdims=True)
    i1 = jnp.min(jnp.where(el == m1, lane, big), axis=-1, keepdims=True)
    el2 = jnp.where(lane == i1, -jnp.inf, el)
    m2 = jnp.max(el2, axis=-1, keepdims=True)
    i2 = jnp.min(jnp.where(el2 == m2, lane, big), axis=-1, keepdims=True)
    t = jnp.exp(m2 - m1)
    gate1 = g_p / (1.0 + t)
    gate2 = g_p * t / (1.0 + t)
    e1 = i1 - N_GROUPS
    e2 = i2 - N_GROUPS

    oh1 = lane == e1
    oh2 = lane == e2
    oh = (oh1 | oh2).astype(F32)
    cum = jnp.dot(tri_ref[...], oh.astype(BF16), preferred_element_type=F32) + run_ref[...]
    rank1 = jnp.sum(jnp.where(oh1, cum, 0.0), axis=-1, keepdims=True)
    rank2 = jnp.sum(jnp.where(oh2, cum, 0.0), axis=-1, keepdims=True)
    run_ref[...] = run_ref[...] + jnp.sum(oh, axis=0, keepdims=True)
    cnt_ref[...] = run_ref[...].astype(jnp.int32)

    gt_ref[...] = jnp.where(lax.broadcasted_iota(jnp.int32, (RT_TM, RINFO), 1) == 0, gate1, gate2)
    vals = [e1.astype(F32), e2.astype(F32), rank1, rank2]
    ri = jnp.zeros(logit.shape, F32)
    for k, v in enumerate(vals):
        ri = jnp.where(lane == k, v, ri)
    ei_ref[0] = ri.T[:RINFO, :].astype(jnp.int32)


def _out_route(attn_n, lru_n, x2, w_out, ln2, w_group, b_group, w_er, b_er):
    T = x2.shape[0]
    wr = jnp.concatenate([w_group, jnp.transpose(w_er, (1, 0, 2)).reshape(D_MODEL, N_EXPERTS)], axis=1)
    wr = jnp.pad(wr, ((0, 0), (0, RT_COLS - wr.shape[1]))).astype(F32)
    br = jnp.pad(jnp.concatenate([b_group, b_er.reshape(-1)]), (0, RT_COLS - N_GROUPS - N_EXPERTS)).reshape(1, RT_COLS)
    row = lambda w: pl.BlockSpec((RT_TM, w), lambda i: (i, 0))
    const = lambda shape: pl.BlockSpec(shape, lambda i: (0, 0))
    return pl.pallas_call(
        _route_kernel,
        grid=(T // RT_TM,),
        in_specs=[row(ATTN_WIDTH), row(LRU_WIDTH), row(D_MODEL), const((D_MODEL, D_MODEL)), const((1, D_MODEL)),
                  const((D_MODEL, RT_COLS)), const((1, RT_COLS))],
        out_specs=[row(D_MODEL), row(PACKED), row(RINFO),
                   pl.BlockSpec((1, RINFO, RT_TM), lambda i: (i, 0, 0)), const((1, RT_COLS))],
        out_shape=[jax.ShapeDtypeStruct((T, D_MODEL), F32),
                   jax.ShapeDtypeStruct((T, PACKED), U32),
                   jax.ShapeDtypeStruct((T, RINFO), F32),
                   jax.ShapeDtypeStruct((T // RT_TM, RINFO, RT_TM), jnp.int32),
                   jax.ShapeDtypeStruct((1, RT_COLS), jnp.int32)],
        scratch_shapes=[pltpu.VMEM((D_MODEL, D_MODEL), BF16),
                        pltpu.VMEM((D_MODEL, 2 * RT_COLS), BF16),
                        pltpu.VMEM((RT_TM, RT_TM), BF16),
                        pltpu.VMEM((1, RT_COLS), F32)],
        compiler_params=_cparams(1),
        name="out_route",
    )(attn_n, lru_n, x2, w_out, ln2.reshape(1, D_MODEL).astype(F32), wr, br.astype(F32))


def _moe_cap(T):
    A = T * TOP_K
    return ((A + MOE_BLOCK - 1) // MOE_BLOCK) * MOE_BLOCK + N_EXPERTS * MOE_BLOCK


PAD_BITS = tuple(1 << b for b in reversed(range(3, MOE_BLOCK.bit_length() - 1)))


def _layout_kernel(cnt_ref, ei_ref, dest_ref, pstart, be_ref, nu_ref):
    n_blocks = be_ref.shape[0]

    def lay(e, carry):
        start, blk = carry
        pstart[e] = start
        nb = (cnt_ref[0, e] + MOE_BLOCK - 1) // MOE_BLOCK

        def fill(k, c):
            be_ref[blk + k] = e
            return c
        lax.fori_loop(0, nb, fill, 0)
        return start + nb * MOE_BLOCK, blk + nb
    _, used = lax.fori_loop(0, N_EXPERTS, lay, (jnp.int32(0), jnp.int32(0)))
    nu_ref[0] = used

    def tail(k, c):
        be_ref[k] = N_EXPERTS - 1
        return c
    lax.fori_loop(used, n_blocks, tail, 0)

    expert = ei_ref[:, 0:TOP_K, :]
    dest = ei_ref[:, TOP_K:2 * TOP_K, :]
    for e in range(N_EXPERTS):
        dest = dest + jnp.where(expert == e, pstart[e], 0)
    dest_ref[...] = dest


def _layout(ei, cnt, n_blocks):
    nt = ei.shape[0]
    smem = pl.BlockSpec(memory_space=pltpu.SMEM)
    vmem = pl.BlockSpec(memory_space=pltpu.VMEM)
    return pl.pallas_call(
        _layout_kernel,
        in_specs=[smem, vmem],
        out_specs=[vmem, smem, smem, smem],
        out_shape=[jax.ShapeDtypeStruct((nt, TOP_K, RT_TM), jnp.int32),
                   jax.ShapeDtypeStruct((N_EXPERTS,), jnp.int32),
                   jax.ShapeDtypeStruct((n_blocks,), jnp.int32),
                   jax.ShapeDtypeStruct((1,), jnp.int32)],
        name="layout",
    )(cnt, ei)


def _dispatch_kernel(cnt_ref, pstart, dest_ref, h2_ref, xs_ref, zeros, zsem, sem):
    i = pl.program_id(0)

    def pad_copies(fn):
        for e in range(N_EXPERTS):
            cnt = cnt_ref[0, e]
            head = (-cnt) & (SUBLANES - 1)
            rest = ((-cnt) & (MOE_BLOCK - 1)) - head
            off = pstart[e] + cnt
            for k in range(SUBLANES - 1):
                @pl.when(k < head)
                def _(off=off, k=k):
                    fn(pltpu.make_async_copy(zeros.at[pl.ds(0, 1), :], xs_ref.at[pl.ds(off + k, 1), :], zsem))
            off = off + head
            for bit in PAD_BITS:
                @pl.when((rest & bit) != 0)
                def _(off=off, bit=bit):
                    fn(pltpu.make_async_copy(zeros.at[pl.ds(0, bit), :],
                                             xs_ref.at[pl.ds(pl.multiple_of(off, SUBLANES), bit), :], zsem))
                off = off + (rest & bit)

    @pl.when(i == 0)
    def _():
        zeros[...] = jnp.zeros_like(zeros)
        pad_copies(lambda cp: cp.start())
        pad_copies(lambda cp: cp.wait())

    def issue(r, carry):
        for k in range(TOP_K):
            pltpu.make_async_copy(h2_ref.at[pl.ds(r, 1), :], xs_ref.at[pl.ds(dest_ref[0, k, r], 1), :], sem).start()
        return carry
    lax.fori_loop(0, RT_TM, issue, 0, unroll=8)
    for k in range(TOP_K):
        pltpu.make_async_copy(h2_ref, xs_ref.at[pl.ds(0, RT_TM), :], sem).wait()


def _dispatch(h2p, dest, pstart, cnt, cap):
    T = h2p.shape[0]
    nt = T // RT_TM
    smem = pl.BlockSpec(memory_space=pltpu.SMEM)
    return pl.pallas_call(
        _dispatch_kernel,
        grid=(nt,),
        in_specs=[smem, smem,
                  pl.BlockSpec((1, TOP_K, RT_TM), lambda i: (i, 0, 0), memory_space=pltpu.SMEM),
                  pl.BlockSpec((RT_TM, PACKED), lambda i: (i, 0))],
        out_specs=pl.BlockSpec(memory_space=pl.ANY),
        out_shape=jax.ShapeDtypeStruct((cap, PACKED), U32),
        scratch_shapes=[pltpu.VMEM((MOE_BLOCK // 2, PACKED), U32),
                        pltpu.SemaphoreType.DMA(()),
                        pltpu.SemaphoreType.DMA(())],
        compiler_params=_cparams(1),
        name="dispatch",
    )(cnt, pstart, dest, h2p)


def _expert_kernel(be_ref, nu_ref, x_ref, wg_ref, wu_ref, wd_ref, o_ref, wgb, wub, wdb):
    j = pl.program_id(0)

    @pl.when(j < nu_ref[0])
    def _():
        changed = jnp.logical_or(j == 0, be_ref[j] != be_ref[jnp.maximum(j - 1, 0)])

        @pl.when(changed)
        def _():
            wgb[...] = wg_ref[0].astype(BF16)
            wub[...] = wu_ref[0].astype(BF16)
            wdb[...] = wd_ref[0].astype(BF16)

        lo, hi = _unpack_rows(x_ref[...])
        lo = lo.astype(BF16)
        hi = hi.astype(BF16)
        g = (jnp.dot(lo, wgb[:PACKED, :], preferred_element_type=F32)
             + jnp.dot(hi, wgb[PACKED:, :], preferred_element_type=F32))
        u = (jnp.dot(lo, wub[:PACKED, :], preferred_element_type=F32)
             + jnp.dot(hi, wub[PACKED:, :], preferred_element_type=F32))
        h = (g * jax.nn.sigmoid(g) * u).astype(BF16)
        o_ref[...] = _pack_rows(jnp.dot(h, wdb[...], preferred_element_type=F32))


def _experts(xs, block_expert, n_used, w_gate, w_up, w_down):
    cap = xs.shape[0]
    n_blocks = cap // MOE_BLOCK
    last = lambda j, be, nu: jnp.minimum(j, nu[0] - 1)
    wspec = lambda shape: pl.BlockSpec((1,) + shape, lambda j, be, nu: (be[last(j, be, nu)], 0, 0))
    gs = pltpu.PrefetchScalarGridSpec(
        num_scalar_prefetch=2,
        grid=(n_blocks,),
        in_specs=[pl.BlockSpec((MOE_BLOCK, PACKED), lambda j, be, nu: (last(j, be, nu), 0)),
                  wspec((D_MODEL, D_EXPERT)), wspec((D_MODEL, D_EXPERT)), wspec((D_EXPERT, D_MODEL))],
        out_specs=pl.BlockSpec((MOE_BLOCK, PACKED), lambda j, be, nu: (last(j, be, nu), 0)),
        scratch_shapes=[pltpu.VMEM((D_MODEL, D_EXPERT), BF16),
                        pltpu.VMEM((D_MODEL, D_EXPERT), BF16),
                        pltpu.VMEM((D_EXPERT, D_MODEL), BF16)],
    )
    return pl.pallas_call(
        _expert_kernel,
        grid_spec=gs,
        out_shape=jax.ShapeDtypeStruct((cap, PACKED), U32),
        compiler_params=_cparams(1),
        name="experts",
    )(block_expert, n_used, xs, w_gate, w_up, w_down)


CB_TM = RT_TM
CB_SLOTS = 2


def _combine_kernel(dcur_ref, dnxt_ref, x1_ref, gt_ref, yb_ref, o_ref, ybuf, sems):
    i = pl.program_id(0)
    nt = pl.num_programs(0)
    slot = i % CB_SLOTS

    def gather(dref, s):
        def issue(r, carry):
            for k in range(TOP_K):
                pltpu.make_async_copy(yb_ref.at[pl.ds(dref[0, k, r], 1), :],
                                      ybuf.at[s, k, pl.ds(r, 1), :], sems.at[s]).start()
            return carry
        lax.fori_loop(0, CB_TM, issue, 0, unroll=8)

    @pl.when(i == 0)
    def _():
        gather(dcur_ref, 0)

    @pl.when(i + 1 < nt)
    def _():
        gather(dnxt_ref, 1 - slot)

    for k in range(TOP_K):
        pltpu.make_async_copy(yb_ref.at[pl.ds(0, CB_TM), :], ybuf.at[slot, k], sems.at[slot]).wait()
    g = gt_ref[...]
    lo1, hi1 = _unpack_rows(ybuf[slot, 0])
    lo2, hi2 = _unpack_rows(ybuf[slot, 1])
    o_ref[:, :PACKED] = x1_ref[:, :PACKED] + g[:, 0:1] * lo1 + g[:, 1:2] * lo2
    o_ref[:, PACKED:] = x1_ref[:, PACKED:] + g[:, 0:1] * hi1 + g[:, 1:2] * hi2


def _combine(x1, gates, yb, dest):
    T = x1.shape[0]
    nt = T // CB_TM
    dspec = lambda f: pl.BlockSpec((1, TOP_K, CB_TM), f, memory_space=pltpu.SMEM)
    return pl.pallas_call(
        _combine_kernel,
        grid=(nt,),
        in_specs=[dspec(lambda i: (i, 0, 0)),
                  dspec(lambda i: (jnp.minimum(i + 1, nt - 1), 0, 0)),
                  pl.BlockSpec((CB_TM, D_MODEL), lambda i: (i, 0)),
                  pl.BlockSpec((CB_TM, RINFO), lambda i: (i, 0)),
                  pl.BlockSpec(memory_space=pl.ANY)],
        out_specs=pl.BlockSpec((CB_TM, D_MODEL), lambda i: (i, 0)),
        out_shape=jax.ShapeDtypeStruct((T, D_MODEL), F32),
        scratch_shapes=[pltpu.VMEM((CB_SLOTS, TOP_K, CB_TM, PACKED), U32), pltpu.SemaphoreType.DMA((CB_SLOTS,))],
        compiler_params=_cparams(1),
        name="combine",
    )(dest, dest, x1, gates, yb)


def _layer(x, rel_bias, ln1, w_in, q_norm, k_norm, attn_sink, conv_w, conv_b, lru_wa, lru_ba, lru_wi, lru_bi,
           lru_lambda, out_norm_attn, out_norm_lru, w_out, ln2, w_group, b_group, w_er, b_er, w_gate, w_up, w_down):
    B, S, D = x.shape
    T = B * S
    x2 = x.reshape(T, D)
    q, kv, xr, gr = _in_proj(x2, ln1, w_in)
    bias_tab = _bias_table(rel_bias)
    attn_n = _attention(q.reshape(B, S, ATTN_WIDTH), kv.reshape(B, S, 2 * KV_WIDTH), bias_tab, attn_sink,
                        q_norm, k_norm, out_norm_attn)
    lru_n = _rglru(xr.reshape(B, S, LRU_WIDTH), gr.reshape(B, S, LRU_WIDTH), conv_w, conv_b,
                   lru_wa, lru_ba, lru_wi, lru_bi, lru_lambda, out_norm_lru)
    x1, h2, gates, ei, cnt = _out_route(attn_n.reshape(T, ATTN_WIDTH), lru_n.reshape(T, LRU_WIDTH), x2, w_out, ln2,
                                        w_group, b_group, w_er, b_er)
    cap = _moe_cap(T)
    dest, pstart, block_expert, n_used = _layout(ei, cnt, cap // MOE_BLOCK)
    xs = _dispatch(h2, dest, pstart, cnt, cap)
    yb = _experts(xs, block_expert, n_used, w_gate, w_up, w_down)
    out = _combine(x1, gates, yb, dest)
    return out.reshape(B, S, D)


def kernel(x, rel_bias, ln1, w_in, q_norm, k_norm, attn_sink, conv_w, conv_b, lru_wa, lru_ba, lru_wi, lru_bi,
           lru_lambda, out_norm_attn, out_norm_lru, w_out, ln2, w_group, b_group, w_expert_router, b_expert_router,
           w_gate, w_up, w_down):
    depth = ln1.shape[0]
    for l in range(depth):
        x = _layer(x, rel_bias, ln1[l], w_in[l], q_norm[l], k_norm[l], attn_sink[l], conv_w[l], conv_b[l],
                   lru_wa[l], lru_ba[l], lru_wi[l], lru_bi[l], lru_lambda[l], out_norm_attn[l], out_norm_lru[l],
                   w_out[l], ln2[l], w_group[l], b_group[l], w_expert_router[l], b_expert_router[l],
                   w_gate[l], w_up[l], w_down[l])
    return x
```

```python
import functools
import math

import jax
import jax.numpy as jnp
import numpy as np
from jax import lax
from jax.experimental import pallas as pl
from jax.experimental.pallas import tpu as pltpu

D_MODEL = 1024
N_HEADS = 8
N_KV_HEADS = 2
HEAD_DIM = 64
Q_PER_KV = N_HEADS // N_KV_HEADS
ATTN_WIDTH = N_HEADS * HEAD_DIM
KV_WIDTH = N_KV_HEADS * HEAD_DIM
WINDOW = 128
BLOCK = 128
NUM_BUCKETS = 32
MAX_DISTANCE = 128
LRU_WIDTH = D_MODEL - ATTN_WIDTH
LRU_BLOCKS = 8
LRU_BLOCK_DIM = LRU_WIDTH // LRU_BLOCKS
LRU_C = 8.0
CONV_W = 4
CONV_LEFT = 2
N_GROUPS = 4
EXPERTS_PER_GROUP = 8
N_EXPERTS = N_GROUPS * EXPERTS_PER_GROUP
TOP_K = 2
D_EXPERT = 512
MOE_BLOCK = 256
EPS = 1e-6
NEG_INF = -1e30

LANES = 128
SUBLANES = 8
VMEM_LIMIT = 56 * 1024 * 1024

F32 = jnp.float32
BF16 = jnp.bfloat16


def _cparams(n_axes, vmem=VMEM_LIMIT):
    return pltpu.CompilerParams(dimension_semantics=("arbitrary",) * n_axes, vmem_limit_bytes=vmem)


def _rms(x, gain):
    return x * lax.rsqrt(jnp.mean(x * x, axis=-1, keepdims=True) + EPS) * gain


U32 = jnp.uint32
HI_MASK = 0xFFFF0000
PACKED = D_MODEL // 2


def _pack_rows(x):
    h = x.shape[1] // 2
    lo = lax.bitcast_convert_type(x[:, :h].astype(BF16).astype(F32), U32) >> 16
    hi = lax.bitcast_convert_type(x[:, h:].astype(BF16).astype(F32), U32) & jnp.uint32(HI_MASK)
    return lo | hi


def _unpack_rows(p):
    lo = lax.bitcast_convert_type(p << 16, F32)
    hi = lax.bitcast_convert_type(p & jnp.uint32(HI_MASK), F32)
    return lo, hi


IN_TM = 512


def _head_rms(x, n_heads, gain):
    head = lax.broadcasted_iota(jnp.int32, (1, n_heads * HEAD_DIM), 1) // HEAD_DIM
    x2 = x * x
    scale = jnp.zeros_like(x)
    for h in range(n_heads):
        ms = jnp.sum(jnp.where(head == h, x2, 0.0), axis=-1, keepdims=True) * (1.0 / HEAD_DIM)
        scale = jnp.where(head == h, lax.rsqrt(ms + EPS), scale)
    return x * scale * gain


def _in_proj_kernel(x_ref, g_ref, w_ref, qg_ref, kg_ref, q_ref, kv_ref, xr_ref, gr_ref, wb_ref):
    @pl.when(pl.program_id(0) == 0)
    def _():
        wb_ref[...] = w_ref[...].astype(BF16)

    h = _rms(x_ref[...], g_ref[...]).astype(BF16)
    c_k = ATTN_WIDTH
    c_v = c_k + KV_WIDTH
    c_x = c_v + KV_WIDTH
    c_g = c_x + LRU_WIDTH
    q = jnp.dot(h, wb_ref[:, :c_k], preferred_element_type=F32)
    q_ref[...] = _head_rms(q, N_HEADS, qg_ref[...]).astype(BF16)
    k = jnp.dot(h, wb_ref[:, c_k:c_v], preferred_element_type=F32)
    kv_ref[:, :KV_WIDTH] = _head_rms(k, N_KV_HEADS, kg_ref[...]).astype(BF16)
    kv_ref[:, KV_WIDTH:] = jnp.dot(h, wb_ref[:, c_v:c_x], preferred_element_type=F32).astype(BF16)
    xr_ref[...] = jnp.dot(h, wb_ref[:, c_x:c_g], preferred_element_type=F32)
    gr_ref[...] = jnp.dot(h, wb_ref[:, c_g:], preferred_element_type=F32)


def _in_proj(x2, ln1, w_in, q_gain, k_gain):
    T = x2.shape[0]
    n_in = w_in.shape[1]
    row = lambda w: pl.BlockSpec((IN_TM, w), lambda i: (i, 0))
    qg = (jnp.tile(q_gain.astype(F32), N_HEADS) * (HEAD_DIM ** -0.5)).reshape(1, ATTN_WIDTH)
    kg = jnp.tile(k_gain.astype(F32), N_KV_HEADS).reshape(1, KV_WIDTH)
    return pl.pallas_call(
        _in_proj_kernel,
        grid=(T // IN_TM,),
        in_specs=[row(D_MODEL),
                  pl.BlockSpec((1, D_MODEL), lambda i: (0, 0)),
                  pl.BlockSpec((D_MODEL, n_in), lambda i: (0, 0)),
                  pl.BlockSpec((1, ATTN_WIDTH), lambda i: (0, 0)),
                  pl.BlockSpec((1, KV_WIDTH), lambda i: (0, 0))],
        out_specs=[row(ATTN_WIDTH), row(2 * KV_WIDTH), row(LRU_WIDTH), row(LRU_WIDTH)],
        out_shape=[jax.ShapeDtypeStruct((T, ATTN_WIDTH), BF16),
                   jax.ShapeDtypeStruct((T, 2 * KV_WIDTH), BF16),
                   jax.ShapeDtypeStruct((T, LRU_WIDTH), F32),
                   jax.ShapeDtypeStruct((T, LRU_WIDTH), F32)],
        scratch_shapes=[pltpu.VMEM((D_MODEL, n_in), BF16)],
        compiler_params=_cparams(1),
        name="in_proj",
    )(x2, ln1.reshape(1, D_MODEL), w_in, qg, kg)


def _t5_bucket(rel):
    half = NUM_BUCKETS // 2
    max_exact = half // 2
    base = jnp.where(rel > 0, half, 0)
    n = jnp.abs(rel)
    nf = jnp.maximum(n, 1).astype(jnp.float32)
    large = max_exact + (jnp.log(nf / max_exact) / math.log(MAX_DISTANCE / max_exact)
                         * (half - max_exact)).astype(jnp.int32)
    large = jnp.minimum(large, half - 1)
    return base + jnp.where(n < max_exact, n, large)


HEAD_PAIRS = Q_PER_KV // 2
EDGE_VARIANTS = 3


def _bias_kernel(rb_ref, bucket_ref, band_ref, o_ref):
    bucket = bucket_ref[...]
    band = band_ref[...] > 0
    col = lax.broadcasted_iota(jnp.int32, bucket.shape, 1)
    valid = (band & (col >= BLOCK), band, band & (col < 2 * BLOCK))
    for h in range(N_HEADS):
        acc = jnp.zeros(bucket.shape, F32)
        for b in range(NUM_BUCKETS):
            acc = jnp.where(bucket == b, rb_ref[b, h], acc)
        kv, g = divmod(h, Q_PER_KV)
        pair, parity = divmod(g, 2)
        for var in range(EDGE_VARIANTS):
            o_ref[var, kv, parity, pair * BLOCK:(pair + 1) * BLOCK, :] = jnp.where(valid[var], acc, NEG_INF)


def _bias_table(rel_bias):
    qi = jnp.arange(BLOCK, dtype=jnp.int32)
    kj = jnp.arange(3 * BLOCK, dtype=jnp.int32)
    rel = kj[None, :] - BLOCK - qi[:, None]
    bucket = _t5_bucket(rel).astype(jnp.int32)
    band = (jnp.abs(rel) <= WINDOW).astype(jnp.int32)
    return pl.pallas_call(
        _bias_kernel,
        in_specs=[pl.BlockSpec(memory_space=pltpu.SMEM),
                  pl.BlockSpec(memory_space=pltpu.VMEM),
                  pl.BlockSpec(memory_space=pltpu.VMEM)],
        out_specs=pl.BlockSpec(memory_space=pltpu.VMEM),
        out_shape=jax.ShapeDtypeStruct((EDGE_VARIANTS, N_KV_HEADS, 2, HEAD_PAIRS * BLOCK, 3 * BLOCK), F32),
        name="bias_table",
    )(rel_bias.astype(F32), bucket, band)


def _attn_kernel(sink_ref, q_ref, kp_ref, kc_ref, kn_ref, bias_ref, og_ref, o_ref):
    q = q_ref[0]
    kvw = jnp.concatenate([kp_ref[0], kc_ref[0], kn_ref[0]], axis=0)
    low = lax.broadcasted_iota(jnp.int32, (1, LANES), 1) < HEAD_DIM
    rowi = lax.broadcasted_iota(jnp.int32, (HEAD_PAIRS * BLOCK, 1), 0)
    swap = lambda slab: pltpu.roll(slab.astype(F32), HEAD_DIM, 1).astype(BF16)
    kslab, vslab = kvw[:, :KV_WIDTH], kvw[:, KV_WIDTH:]
    kslab_sw, vslab_sw = swap(kslab), swap(vslab)
    outs = []
    for kv in range(N_KV_HEADS):
        src = ((kslab, vslab), (kslab_sw, vslab_sw)) if kv == 0 else ((kslab_sw, vslab_sw), (kslab, vslab))
        base = kv * Q_PER_KV * HEAD_DIM
        qpair = jnp.concatenate([q[:, base + j * LANES:base + (j + 1) * LANES] for j in range(HEAD_PAIRS)], axis=0)
        acc = None
        for parity in range(2):
            keep = low if parity == 0 else jnp.logical_not(low)
            kz = jnp.where(keep, src[parity][0], jnp.zeros_like(kslab))
            vz = jnp.where(keep, src[parity][1], jnp.zeros_like(vslab))
            sink = jnp.zeros((HEAD_PAIRS * BLOCK, 1), F32)
            for j in range(HEAD_PAIRS):
                sink = jnp.where(rowi // BLOCK == j, sink_ref[kv * Q_PER_KV + 2 * j + parity], sink)
            s = lax.dot_general(qpair, kz, (((1,), (1,)), ((), ())), preferred_element_type=F32)
            s = s + bias_ref[0, kv, parity]
            m = jnp.maximum(jnp.max(s, axis=-1, keepdims=True), sink)
            p = jnp.exp(s - m)
            denom = jnp.sum(p, axis=-1, keepdims=True) + jnp.exp(sink - m)
            o = jnp.dot(p.astype(BF16), vz, preferred_element_type=F32) * (1.0 / denom)
            acc = o if acc is None else acc + o
        outs += [acc[j * BLOCK:(j + 1) * BLOCK, :] for j in range(HEAD_PAIRS)]
    o_ref[0] = _rms(jnp.concatenate(outs, axis=1), og_ref[...]).astype(o_ref.dtype)


def _attention(q, kv, bias_tab, sink, out_gain):
    B, S, _ = q.shape
    nb = S // BLOCK
    assert nb >= 2, "edge variants assume distinct first and last blocks"
    kvspec = lambda f: pl.BlockSpec((1, BLOCK, 2 * KV_WIDTH), f)
    variant = lambda n: jnp.where(n == 0, 0, jnp.where(n == nb - 1, 2, 1))
    return pl.pallas_call(
        _attn_kernel,
        grid=(B, nb),
        in_specs=[pl.BlockSpec(memory_space=pltpu.SMEM),
                  pl.BlockSpec((1, BLOCK, ATTN_WIDTH), lambda b, n: (b, n, 0)),
                  kvspec(lambda b, n: (b, jnp.maximum(n - 1, 0), 0)),
                  kvspec(lambda b, n: (b, n, 0)),
                  kvspec(lambda b, n: (b, jnp.minimum(n + 1, nb - 1), 0)),
                  pl.BlockSpec((1, N_KV_HEADS, 2, HEAD_PAIRS * BLOCK, 3 * BLOCK),
                               lambda b, n: (variant(n), 0, 0, 0, 0)),
                  pl.BlockSpec((1, ATTN_WIDTH), lambda b, n: (0, 0))],
        out_specs=pl.BlockSpec((1, BLOCK, ATTN_WIDTH), lambda b, n: (b, n, 0)),
        out_shape=jax.ShapeDtypeStruct((B, S, ATTN_WIDTH), BF16),
        compiler_params=_cparams(2),
        name="attention",
    )(sink.astype(F32), q, kv, kv, kv, bias_tab, out_gain.reshape(1, ATTN_WIDTH))


LRU_TC = 128
LRU_PITCH = LRU_TC + SUBLANES
LRU_SLABS = LRU_WIDTH // LANES
HALO = SUBLANES


def _softplus(x):
    return jnp.maximum(x, 0.0) + jnp.log(1.0 + jnp.exp(-jnp.abs(x)))


def _gelu_tanh(x):
    return 0.5 * x * (1.0 + jnp.tanh(math.sqrt(2.0 / math.pi) * (x + 0.044715 * (x * x * x))))


def _rglru_kernel(xr_ref, xp_ref, xn_ref, gr_ref, cw_ref, cb_ref, wg_ref, bg_ref, lam_ref, og_ref,
                  o_ref, sx_ref, a_ref, u_ref, h_ref, carry_ref, hf_ref):
    p = pl.program_id(0)
    i = pl.program_id(1)
    nc = pl.num_programs(1)
    c = i + p * (nc - 1 - 2 * i)
    B = xr_ref.shape[0]
    TC = LRU_TC

    sx_ref[:, HALO:HALO + TC, :] = xr_ref[...]
    sx_ref[:, 0:HALO, :] = jnp.where(c > 0, xp_ref[...], 0.0)
    sx_ref[:, HALO + TC:, :] = jnp.where(c < nc - 1, xn_ref[...], 0.0)
    xc = cb_ref[...][None]
    for j in range(CONV_W):
        off = HALO + j - CONV_LEFT
        xc = xc + cw_ref[j:j + 1, :][None] * sx_ref[:, off:off + TC, :]
    xc2 = xc.reshape(B * TC, LRU_WIDTH)

    g = jnp.dot(xc2.astype(BF16), wg_ref[0], preferred_element_type=F32) + bg_ref[0]
    r = jax.nn.sigmoid(g[:, :LRU_WIDTH])
    ig = jax.nn.sigmoid(g[:, LRU_WIDTH:])
    log_a = -LRU_C * r * _softplus(-lam_ref[0])
    a = jnp.exp(log_a)
    th = jnp.tanh(log_a)
    u = jnp.sqrt(-2.0 * th / (1.0 - th)) * ig * xc2
    for b in range(B):
        for s in range(LRU_SLABS):
            a_ref[s, b * LRU_PITCH:b * LRU_PITCH + TC, :] = a[b * TC:(b + 1) * TC, s * LANES:(s + 1) * LANES]
            u_ref[s, b * LRU_PITCH:b * LRU_PITCH + TC, :] = u[b * TC:(b + 1) * TC, s * LANES:(s + 1) * LANES]

    @pl.when(i == 0)
    def _():
        carry_ref[...] = jnp.zeros_like(carry_ref)

    def step(k, hs):
        t = k + p * (TC - 1 - 2 * k)
        out = []
        for s in range(LRU_SLABS):
            idx = pl.ds(t, B, stride=LRU_PITCH)
            hn = a_ref[s, idx, :] * hs[s] + u_ref[s, idx, :]
            h_ref[s, idx, :] = hn
            out.append(hn)
        return tuple(out)

    hs = lax.fori_loop(0, TC, step, tuple(carry_ref[s] for s in range(LRU_SLABS)), unroll=8)
    for s in range(LRU_SLABS):
        carry_ref[s] = hs[s]

    @pl.when(p == 0)
    def _():
        for b in range(B):
            for s in range(LRU_SLABS):
                hf_ref[c, s, b * TC:(b + 1) * TC, :] = h_ref[s, b * LRU_PITCH:b * LRU_PITCH + TC, :].astype(hf_ref.dtype)

    @pl.when(p == 1)
    def _():
        for b in range(B):
            hsum = jnp.concatenate(
                [h_ref[s, b * LRU_PITCH:b * LRU_PITCH + TC, :] + hf_ref[c, s, b * TC:(b + 1) * TC, :].astype(F32)
                 for s in range(LRU_SLABS)], axis=1)
            y = hsum * _gelu_tanh(gr_ref[b])
            o_ref[b] = _rms(y, og_ref[...]).astype(o_ref.dtype)


def _block_diag(w):
    eye = jnp.eye(LRU_BLOCKS, dtype=w.dtype)
    return jnp.einsum('hij,hg->higj', w, eye).reshape(LRU_WIDTH, LRU_WIDTH)


def _rglru(xr, gr, conv_w, conv_b, w_a, b_a, w_i, b_i, lam, out_gain):
    B, S, W = xr.shape
    nc = S // LRU_TC
    hb = LRU_TC // HALO
    wg = jnp.stack([jnp.concatenate([_block_diag(w_a[d]), _block_diag(w_i[d])], axis=1) for d in range(2)]).astype(BF16)
    bg = jnp.concatenate([b_a, b_i], axis=-1).reshape(2, 1, 2 * W).astype(F32)
    chunk = lambda p, i: i + p * (nc - 1 - 2 * i)
    full2 = lambda shape: pl.BlockSpec(shape, lambda p, i: (0,) * len(shape))
    return pl.pallas_call(
        _rglru_kernel,
        grid=(2, nc),
        in_specs=[pl.BlockSpec((B, LRU_TC, W), lambda p, i: (0, chunk(p, i), 0)),
                  pl.BlockSpec((B, HALO, W), lambda p, i: (0, jnp.maximum(chunk(p, i) * hb - 1, 0), 0)),
                  pl.BlockSpec((B, HALO, W), lambda p, i: (0, jnp.minimum((chunk(p, i) + 1) * hb, S // HALO - 1), 0)),
                  pl.BlockSpec((B, LRU_TC, W), lambda p, i: (0, chunk(p, i), 0)),
                  full2((CONV_W, W)),
                  full2((1, W)),
                  pl.BlockSpec((1, W, 2 * W), lambda p, i: (p, 0, 0)),
                  pl.BlockSpec((1, 1, 2 * W), lambda p, i: (p, 0, 0)),
                  pl.BlockSpec((1, 1, W), lambda p, i: (p, 0, 0)),
                  full2((1, W))],
        out_specs=pl.BlockSpec((B, LRU_TC, W), lambda p, i: (0, nc - 1 - p * i, 0)),
        out_shape=jax.ShapeDtypeStruct((B, S, W), BF16),
        scratch_shapes=[pltpu.VMEM((B, LRU_TC + 2 * HALO, W), F32),
                        pltpu.VMEM((LRU_SLABS, B * LRU_PITCH, LANES), F32),
                        pltpu.VMEM((LRU_SLABS, B * LRU_PITCH, LANES), F32),
                        pltpu.VMEM((LRU_SLABS, B * LRU_PITCH, LANES), F32),
                        pltpu.VMEM((LRU_SLABS, B, LANES), F32),
                        pltpu.VMEM((nc, LRU_SLABS, B * LRU_TC, LANES), BF16)],
        compiler_params=_cparams(2),
        name="rglru",
    )(xr, xr, xr, gr, conv_w.astype(F32), conv_b.reshape(1, W).astype(F32), wg, bg,
      lam.reshape(2, 1, W).astype(F32), out_gain.reshape(1, W).astype(F32))


RT_TM = 512
RT_COLS = LANES
RINFO = 8


def _split_bf16(x):
    hi = x.astype(BF16)
    lo = (x - hi.astype(F32)).astype(BF16)
    return hi, lo


def _route_kernel(an_ref, ln_ref, x_ref, wo_ref, g2_ref, wr_ref, br_ref,
                  x1_ref, h2_ref, gt_ref, ei_ref, cnt_ref, wob_ref, wrb_ref, tri_ref, run_ref):
    @pl.when(pl.program_id(0) == 0)
    def _():
        wob_ref[...] = wo_ref[...].astype(BF16)
        hi, lo = _split_bf16(wr_ref[...])
        wrb_ref[:, :RT_COLS] = hi
        wrb_ref[:, RT_COLS:] = lo
        r = lax.broadcasted_iota(jnp.int32, (RT_TM, RT_TM), 0)
        cidx = lax.broadcasted_iota(jnp.int32, (RT_TM, RT_TM), 1)
        tri_ref[...] = (cidx < r).astype(BF16)
        run_ref[...] = jnp.zeros_like(run_ref)

    x1 = (x_ref[...]
          + jnp.dot(an_ref[...], wob_ref[:ATTN_WIDTH, :], preferred_element_type=F32)
          + jnp.dot(ln_ref[...], wob_ref[ATTN_WIDTH:, :], preferred_element_type=F32))
    x1_ref[...] = x1
    h2 = _rms(x1, g2_ref[...])
    h2_ref[...] = _pack_rows(h2)

    hi, lo = _split_bf16(h2)
    t1 = jnp.dot(hi, wrb_ref[...], preferred_element_type=F32)
    t2 = jnp.dot(lo, wrb_ref[:, :RT_COLS], preferred_element_type=F32)
    logit = t1[:, :RT_COLS] + t1[:, RT_COLS:] + t2 + br_ref[...]

    lane = lax.broadcasted_iota(jnp.int32, logit.shape, 1)
    big = jnp.int32(4 * RT_COLS)
    is_g = lane < N_GROUPS
    gl = jnp.where(is_g, logit, -jnp.inf)
    gm = jnp.max(gl, axis=-1, keepdims=True)
    gidx = jnp.min(jnp.where(gl == gm, lane, big), axis=-1, keepdims=True)
    g_p = 1.0 / jnp.sum(jnp.where(is_g, jnp.exp(logit - gm), 0.0), axis=-1, keepdims=True)
    lo_lane = N_GROUPS + gidx * EXPERTS_PER_GROUP
    el = jnp.where((lane >= lo_lane) & (lane < lo_lane + EXPERTS_PER_GROUP), logit, -jnp.inf)
    m1 = jnp.max(el, axis=-1, keepdims=True)
    i1 = jnp.min(jnp.where(el == m1, lane, big), axis=-1, keepdims=True)
    el2 = jnp.where(lane == i1, -jnp.inf, el)
    m2 = jnp.max(el2, axis=-1, keepdims=True)
    i2 = jnp.min(jnp.where(el2 == m2, lane, big), axis=-1, keepdims=True)
    t = jnp.exp(m2 - m1)
    gate1 = g_p / (1.0 + t)
    gate2 = g_p * t / (1.0 + t)
    e1 = i1 - N_GROUPS
    e2 = i2 - N_GROUPS

    oh1 = lane == e1
    oh2 = lane == e2
    oh = (oh1 | oh2).astype(F32)
    cum = jnp.dot(tri_ref[...], oh.astype(BF16), preferred_element_type=F32) + run_ref[...]
    rank1 = jnp.sum(jnp.where(oh1, cum, 0.0), axis=-1, keepdims=True)
    rank2 = jnp.sum(jnp.where(oh2, cum, 0.0), axis=-1, keepdims=True)
    run_ref[...] = run_ref[...] + jnp.sum(oh, axis=0, keepdims=True)
    cnt_ref[...] = run_ref[...].astype(jnp.int32)

    gt_ref[...] = jnp.where(lax.broadcasted_iota(jnp.int32, (RT_TM, RINFO), 1) == 0, gate1, gate2)
    vals = [e1.astype(F32), e2.astype(F32), rank1, rank2]
    ri = jnp.zeros(logit.shape, F32)
    for k, v in enumerate(vals):
        ri = jnp.where(lane == k, v, ri)
    ei_ref[0] = ri.T[:RINFO, :].astype(jnp.int32)


def _out_route(attn_n, lru_n, x2, w_out, ln2, w_group, b_group, w_er, b_er):
    T = x2.shape[0]
    wr = jnp.concatenate([w_group, jnp.transpose(w_er, (1, 0, 2)).reshape(D_MODEL, N_EXPERTS)], axis=1)
    wr = jnp.pad(wr, ((0, 0), (0, RT_COLS - wr.shape[1]))).astype(F32)
    br = jnp.pad(jnp.concatenate([b_group, b_er.reshape(-1)]), (0, RT_COLS - N_GROUPS - N_EXPERTS)).reshape(1, RT_COLS)
    row = lambda w: pl.BlockSpec((RT_TM, w), lambda i: (i, 0))
    const = lambda shape: pl.BlockSpec(shape, lambda i: (0, 0))
    return pl.pallas_call(
        _route_kernel,
        grid=(T // RT_TM,),
        in_specs=[row(ATTN_WIDTH), row(LRU_WIDTH), row(D_MODEL), const((D_MODEL, D_MODEL)), const((1, D_MODEL)),
                  const((D_MODEL, RT_COLS)), const((1, RT_COLS))],
        out_specs=[row(D_MODEL), row(PACKED), row(RINFO),
                   pl.BlockSpec((1, RINFO, RT_TM), lambda i: (i, 0, 0)), const((1, RT_COLS))],
        out_shape=[jax.ShapeDtypeStruct((T, D_MODEL), F32),
                   jax.ShapeDtypeStruct((T, PACKED), U32),
                   jax.ShapeDtypeStruct((T, RINFO), F32),
                   jax.ShapeDtypeStruct((T // RT_TM, RINFO, RT_TM), jnp.int32),
                   jax.ShapeDtypeStruct((1, RT_COLS), jnp.int32)],
        scratch_shapes=[pltpu.VMEM((D_MODEL, D_MODEL), BF16),
                        pltpu.VMEM((D_MODEL, 2 * RT_COLS), BF16),
                        pltpu.VMEM((RT_TM, RT_TM), BF16),
                        pltpu.VMEM((1, RT_COLS), F32)],
        compiler_params=_cparams(1),
        name="out_route",
    )(attn_n, lru_n, x2, w_out, ln2.reshape(1, D_MODEL).astype(F32), wr, br.astype(F32))


def _moe_cap(T):
    A = T * TOP_K
    return ((A + MOE_BLOCK - 1) // MOE_BLOCK) * MOE_BLOCK + N_EXPERTS * MOE_BLOCK


PAD_BITS = tuple(1 << b for b in reversed(range(3, MOE_BLOCK.bit_length() - 1)))


def _layout_kernel(cnt_ref, ei_ref, dest_ref, pstart, be_ref, nu_ref):
    n_blocks = be_ref.shape[0]

    def lay(e, carry):
        start, blk = carry
        pstart[e] = start
        nb = (cnt_ref[0, e] + MOE_BLOCK - 1) // MOE_BLOCK

        def fill(k, c):
            be_ref[blk + k] = e
            return c
        lax.fori_loop(0, nb, fill, 0)
        return start + nb * MOE_BLOCK, blk + nb
    _, used = lax.fori_loop(0, N_EXPERTS, lay, (jnp.int32(0), jnp.int32(0)))
    nu_ref[0] = used

    def tail(k, c):
        be_ref[k] = N_EXPERTS - 1
        return c
    lax.fori_loop(used, n_blocks, tail, 0)

    expert = ei_ref[:, 0:TOP_K, :]
    dest = ei_ref[:, TOP_K:2 * TOP_K, :]
    for e in range(N_EXPERTS):
        dest = dest + jnp.where(expert == e, pstart[e], 0)
    dest_ref[...] = dest


def _layout(ei, cnt, n_blocks):
    nt = ei.shape[0]
    smem = pl.BlockSpec(memory_space=pltpu.SMEM)
    vmem = pl.BlockSpec(memory_space=pltpu.VMEM)
    return pl.pallas_call(
        _layout_kernel,
        in_specs=[smem, vmem],
        out_specs=[vmem, smem, smem, smem],
        out_shape=[jax.ShapeDtypeStruct((nt, TOP_K, RT_TM), jnp.int32),
                   jax.ShapeDtypeStruct((N_EXPERTS,), jnp.int32),
                   jax.ShapeDtypeStruct((n_blocks,), jnp.int32),
                   jax.ShapeDtypeStruct((1,), jnp.int32)],
        name="layout",
    )(cnt, ei)


def _dispatch_kernel(cnt_ref, pstart, dest_ref, h2_ref, xs_ref, zeros, zsem, sem):
    i = pl.program_id(0)

    def pad_copies(fn):
        for e in range(N_EXPERTS):
            cnt = cnt_ref[0, e]
            head = (-cnt) & (SUBLANES - 1)
            rest = ((-cnt) & (MOE_BLOCK - 1)) - head
            off = pstart[e] + cnt
            for k in range(SUBLANES - 1):
                @pl.when(k < head)
                def _(off=off, k=k):
                    fn(pltpu.make_async_copy(zeros.at[pl.ds(0, 1), :], xs_ref.at[pl.ds(off + k, 1), :], zsem))
            off = off + head
            for bit in PAD_BITS:
                @pl.when((rest & bit) != 0)
                def _(off=off, bit=bit):
                    fn(pltpu.make_async_copy(zeros.at[pl.ds(0, bit), :],
                                             xs_ref.at[pl.ds(pl.multiple_of(off, SUBLANES), bit), :], zsem))
                off = off + (rest & bit)

    @pl.when(i == 0)
    def _():
        zeros[...] = jnp.zeros_like(zeros)
        pad_copies(lambda cp: cp.start())
        pad_copies(lambda cp: cp.wait())

    def issue(r, carry):
        for k in range(TOP_K):
            pltpu.make_async_copy(h2_ref.at[pl.ds(r, 1), :], xs_ref.at[pl.ds(dest_ref[0, k, r], 1), :], sem).start()
        return carry
    lax.fori_loop(0, RT_TM, issue, 0, unroll=8)
    for k in range(TOP_K):
        pltpu.make_async_copy(h2_ref, xs_ref.at[pl.ds(0, RT_TM), :], sem).wait()


def _dispatch(h2p, dest, pstart, cnt, cap):
    T = h2p.shape[0]
    nt = T // RT_TM
    smem = pl.BlockSpec(memory_space=pltpu.SMEM)
    return pl.pallas_call(
        _dispatch_kernel,
        grid=(nt,),
        in_specs=[smem, smem,
                  pl.BlockSpec((1, TOP_K, RT_TM), lambda i: (i, 0, 0), memory_space=pltpu.SMEM),
                  pl.BlockSpec((RT_TM, PACKED), lambda i: (i, 0))],
        out_specs=pl.BlockSpec(memory_space=pl.ANY),
        out_shape=jax.ShapeDtypeStruct((cap, PACKED), U32),
        scratch_shapes=[pltpu.VMEM((MOE_BLOCK // 2, PACKED), U32),
                        pltpu.SemaphoreType.DMA(()),
                        pltpu.SemaphoreType.DMA(())],
        compiler_params=_cparams(1),
        name="dispatch",
    )(cnt, pstart, dest, h2p)


def _expert_kernel(be_ref, nu_ref, x_ref, wg_ref, wu_ref, wd_ref, o_ref, wgb, wub, wdb):
    j = pl.program_id(0)

    @pl.when(j < nu_ref[0])
    def _():
        changed = jnp.logical_or(j == 0, be_ref[j] != be_ref[jnp.maximum(j - 1, 0)])

        @pl.when(changed)
        def _():
            wgb[...] = wg_ref[0].astype(BF16)
            wub[...] = wu_ref[0].astype(BF16)
            wdb[...] = wd_ref[0].astype(BF16)

        lo, hi = _unpack_rows(x_ref[...])
        lo = lo.astype(BF16)
        hi = hi.astype(BF16)
        g = (jnp.dot(lo, wgb[:PACKED, :], preferred_element_type=F32)
             + jnp.dot(hi, wgb[PACKED:, :], preferred_element_type=F32))
        u = (jnp.dot(lo, wub[:PACKED, :], preferred_element_type=F32)
             + jnp.dot(hi, wub[PACKED:, :], preferred_element_type=F32))
        h = (g * jax.nn.sigmoid(g) * u).astype(BF16)
        o_ref[...] = _pack_rows(jnp.dot(h, wdb[...], preferred_element_type=F32))


def _experts(xs, block_expert, n_used, w_gate, w_up, w_down):
    cap = xs.shape[0]
    n_blocks = cap // MOE_BLOCK
    last = lambda j, be, nu: jnp.minimum(j, nu[0] - 1)
    wspec = lambda shape: pl.BlockSpec((1,) + shape, lambda j, be, nu: (be[last(j, be, nu)], 0, 0))
    gs = pltpu.PrefetchScalarGridSpec(
        num_scalar_prefetch=2,
        grid=(n_blocks,),
        in_specs=[pl.BlockSpec((MOE_BLOCK, PACKED), lambda j, be, nu: (last(j, be, nu), 0)),
                  wspec((D_MODEL, D_EXPERT)), wspec((D_MODEL, D_EXPERT)), wspec((D_EXPERT, D_MODEL))],
        out_specs=pl.BlockSpec((MOE_BLOCK, PACKED), lambda j, be, nu: (last(j, be, nu), 0)),
        scratch_shapes=[pltpu.VMEM((D_MODEL, D_EXPERT), BF16),
                        pltpu.VMEM((D_MODEL, D_EXPERT), BF16),
                        pltpu.VMEM((D_EXPERT, D_MODEL), BF16)],
    )
    return pl.pallas_call(
        _expert_kernel,
        grid_spec=gs,
        out_shape=jax.ShapeDtypeStruct((cap, PACKED), U32),
        compiler_params=_cparams(1),
        name="experts",
    )(block_expert, n_used, xs, w_gate, w_up, w_down)


CB_TM = RT_TM
CB_SLOTS = 2


def _combine_kernel(dcur_ref, dnxt_ref, x1_ref, gt_ref, yb_ref, o_ref, ybuf, sems):
    i = pl.program_id(0)
    nt = pl.num_programs(0)
    slot = i % CB_SLOTS

    def gather(dref, s):
        def issue(r, carry):
            for k in range(TOP_K):
                pltpu.make_async_copy(yb_ref.at[pl.ds(dref[0, k, r], 1), :],
                                      ybuf.at[s, k, pl.ds(r, 1), :], sems.at[s]).start()
            return carry
        lax.fori_loop(0, CB_TM, issue, 0, unroll=8)

    @pl.when(i == 0)
    def _():
        gather(dcur_ref, 0)

    @pl.when(i + 1 < nt)
    def _():
        gather(dnxt_ref, 1 - slot)

    for k in range(TOP_K):
        pltpu.make_async_copy(yb_ref.at[pl.ds(0, CB_TM), :], ybuf.at[slot, k], sems.at[slot]).wait()
    g = gt_ref[...]
    lo1, hi1 = _unpack_rows(ybuf[slot, 0])
    lo2, hi2 = _unpack_rows(ybuf[slot, 1])
    o_ref[:, :PACKED] = x1_ref[:, :PACKED] + g[:, 0:1] * lo1 + g[:, 1:2] * lo2
    o_ref[:, PACKED:] = x1_ref[:, PACKED:] + g[:, 0:1] * hi1 + g[:, 1:2] * hi2


def _combine(x1, gates, yb, dest):
    T = x1.shape[0]
    nt = T // CB_TM
    dspec = lambda f: pl.BlockSpec((1, TOP_K, CB_TM), f, memory_space=pltpu.SMEM)
    return pl.pallas_call(
        _combine_kernel,
        grid=(nt,),
        in_specs=[dspec(lambda i: (i, 0, 0)),
                  dspec(lambda i: (jnp.minimum(i + 1, nt - 1), 0, 0)),
                  pl.BlockSpec((CB_TM, D_MODEL), lambda i: (i, 0)),
                  pl.BlockSpec((CB_TM, RINFO), lambda i: (i, 0)),
                  pl.BlockSpec(memory_space=pl.ANY)],
        out_specs=pl.BlockSpec((CB_TM, D_MODEL), lambda i: (i, 0)),
        out_shape=jax.ShapeDtypeStruct((T, D_MODEL), F32),
        scratch_shapes=[pltpu.VMEM((CB_SLOTS, TOP_K, CB_TM, PACKED), U32), pltpu.SemaphoreType.DMA((CB_SLOTS,))],
        compiler_params=_cparams(1),
        name="combine",
    )(dest, dest, x1, gates, yb)


def _layer(x, rel_bias, ln1, w_in, q_norm, k_norm, attn_sink, conv_w, conv_b, lru_wa, lru_ba, lru_wi, lru_bi,
           lru_lambda, out_norm_attn, out_norm_lru, w_out, ln2, w_group, b_group, w_er, b_er, w_gate, w_up, w_down):
    B, S, D = x.shape
    T = B * S
    x2 = x.reshape(T, D)
    q, kv, xr, gr = _in_proj(x2, ln1, w_in, q_norm, k_norm)
    bias_tab = _bias_table(rel_bias)
    attn_n = _attention(q.reshape(B, S, ATTN_WIDTH), kv.reshape(B, S, 2 * KV_WIDTH), bias_tab, attn_sink,
                        out_norm_attn)
    lru_n = _rglru(xr.reshape(B, S, LRU_WIDTH), gr.reshape(B, S, LRU_WIDTH), conv_w, conv_b,
                   lru_wa, lru_ba, lru_wi, lru_bi, lru_lambda, out_norm_lru)
    x1, h2, gates, ei, cnt = _out_route(attn_n.reshape(T, ATTN_WIDTH), lru_n.reshape(T, LRU_WIDTH), x2, w_out, ln2,
                                        w_group, b_group, w_er, b_er)
    cap = _moe_cap(T)
    dest, pstart, block_expert, n_used = _layout(ei, cnt, cap // MOE_BLOCK)
    xs = _dispatch(h2, dest, pstart, cnt, cap)
    yb = _experts(xs, block_expert, n_used, w_gate, w_up, w_down)
    out = _combine(x1, gates, yb, dest)
    return out.reshape(B, S, D)


def kernel(x, rel_bias, ln1, w_in, q_norm, k_norm, attn_sink, conv_w, conv_b, lru_wa, lru_ba, lru_wi, lru_bi,
           lru_lambda, out_norm_attn, out_norm_lru, w_out, ln2, w_group, b_group, w_expert_router, b_expert_router,
           w_gate, w_up, w_down):
    depth = ln1.shape[0]
    for l in range(depth):
        x = _layer(x, rel_bias, ln1[l], w_in[l], q_norm[l], k_norm[l], attn_sink[l], conv_w[l], conv_b[l],
                   lru_wa[l], lru_ba[l], lru_wi[l], lru_bi[l], lru_lambda[l], out_norm_attn[l], out_norm_lru[l],
                   w_out[l], ln2[l], w_group[l], b_group[l], w_expert_router[l], b_expert_router[l],
                   w_gate[l], w_up[l], w_down[l])
    return x
```

```python
import functools
import math

import jax
import jax.numpy as jnp
import numpy as np
from jax import lax
from jax.experimental import pallas as pl
from jax.experimental.pallas import tpu as pltpu

D_MODEL = 1024
N_HEADS = 8
N_KV_HEADS = 2
HEAD_DIM = 64
Q_PER_KV = N_HEADS // N_KV_HEADS
ATTN_WIDTH = N_HEADS * HEAD_DIM
KV_WIDTH = N_KV_HEADS * HEAD_DIM
WINDOW = 128
BLOCK = 128
NUM_BUCKETS = 32
MAX_DISTANCE = 128
LRU_WIDTH = D_MODEL - ATTN_WIDTH
LRU_BLOCKS = 8
LRU_BLOCK_DIM = LRU_WIDTH // LRU_BLOCKS
LRU_C = 8.0
CONV_W = 4
CONV_LEFT = 2
N_GROUPS = 4
EXPERTS_PER_GROUP = 8
N_EXPERTS = N_GROUPS * EXPERTS_PER_GROUP
TOP_K = 2
D_EXPERT = 512
MOE_BLOCK = 256
EPS = 1e-6
NEG_INF = -1e30

LANES = 128
SUBLANES = 8
VMEM_LIMIT = 56 * 1024 * 1024

F32 = jnp.float32
BF16 = jnp.bfloat16


def _cparams(n_axes, vmem=VMEM_LIMIT):
    return pltpu.CompilerParams(dimension_semantics=("arbitrary",) * n_axes, vmem_limit_bytes=vmem)


def _rms(x, gain):
    return x * lax.rsqrt(jnp.mean(x * x, axis=-1, keepdims=True) + EPS) * gain


U32 = jnp.uint32
HI_MASK = 0xFFFF0000
PACKED = D_MODEL // 2


def _pack_rows(x):
    h = x.shape[1] // 2
    lo = lax.bitcast_convert_type(x[:, :h].astype(BF16).astype(F32), U32) >> 16
    hi = lax.bitcast_convert_type(x[:, h:].astype(BF16).astype(F32), U32) & jnp.uint32(HI_MASK)
    return lo | hi


def _unpack_rows(p):
    lo = lax.bitcast_convert_type(p << 16, F32)
    hi = lax.bitcast_convert_type(p & jnp.uint32(HI_MASK), F32)
    return lo, hi


IN_TM = 512


def _head_rms(x, n_heads, gain):
    head = lax.broadcasted_iota(jnp.int32, (1, n_heads * HEAD_DIM), 1) // HEAD_DIM
    x2 = x * x
    scale = jnp.zeros_like(x)
    for h in range(n_heads):
        ms = jnp.sum(jnp.where(head == h, x2, 0.0), axis=-1, keepdims=True) * (1.0 / HEAD_DIM)
        scale = jnp.where(head == h, lax.rsqrt(ms + EPS), scale)
    return x * scale * gain


def _in_proj_kernel(x_ref, g_ref, w_ref, qg_ref, kg_ref, q_ref, kv_ref, xr_ref, gr_ref, wb_ref):
    @pl.when(pl.program_id(0) == 0)
    def _():
        wb_ref[...] = w_ref[...].astype(BF16)

    h = _rms(x_ref[...], g_ref[...]).astype(BF16)
    c_k = ATTN_WIDTH
    c_v = c_k + KV_WIDTH
    c_x = c_v + KV_WIDTH
    c_g = c_x + LRU_WIDTH
    q = jnp.dot(h, wb_ref[:, :c_k], preferred_element_type=F32)
    q_ref[...] = _head_rms(q, N_HEADS, qg_ref[...]).astype(BF16)
    k = jnp.dot(h, wb_ref[:, c_k:c_v], preferred_element_type=F32)
    kv_ref[:, :KV_WIDTH] = _head_rms(k, N_KV_HEADS, kg_ref[...]).astype(BF16)
    kv_ref[:, KV_WIDTH:] = jnp.dot(h, wb_ref[:, c_v:c_x], preferred_element_type=F32).astype(BF16)
    xr_ref[...] = jnp.dot(h, wb_ref[:, c_x:c_g], preferred_element_type=F32)
    gr_ref[...] = jnp.dot(h, wb_ref[:, c_g:], preferred_element_type=F32)


def _in_proj(x2, ln1, w_in, q_gain, k_gain):
    T = x2.shape[0]
    n_in = w_in.shape[1]
    row = lambda w: pl.BlockSpec((IN_TM, w), lambda i: (i, 0))
    qg = (jnp.tile(q_gain.astype(F32), N_HEADS) * (HEAD_DIM ** -0.5)).reshape(1, ATTN_WIDTH)
    kg = jnp.tile(k_gain.astype(F32), N_KV_HEADS).reshape(1, KV_WIDTH)
    return pl.pallas_call(
        _in_proj_kernel,
        grid=(T // IN_TM,),
        in_specs=[row(D_MODEL),
                  pl.BlockSpec((1, D_MODEL), lambda i: (0, 0)),
                  pl.BlockSpec((D_MODEL, n_in), lambda i: (0, 0)),
                  pl.BlockSpec((1, ATTN_WIDTH), lambda i: (0, 0)),
                  pl.BlockSpec((1, KV_WIDTH), lambda i: (0, 0))],
        out_specs=[row(ATTN_WIDTH), row(2 * KV_WIDTH), row(LRU_WIDTH), row(LRU_WIDTH)],
        out_shape=[jax.ShapeDtypeStruct((T, ATTN_WIDTH), BF16),
                   jax.ShapeDtypeStruct((T, 2 * KV_WIDTH), BF16),
                   jax.ShapeDtypeStruct((T, LRU_WIDTH), F32),
                   jax.ShapeDtypeStruct((T, LRU_WIDTH), F32)],
        scratch_shapes=[pltpu.VMEM((D_MODEL, n_in), BF16)],
        compiler_params=_cparams(1),
        name="in_proj",
    )(x2, ln1.reshape(1, D_MODEL), w_in, qg, kg)


def _t5_bucket(rel):
    half = NUM_BUCKETS // 2
    max_exact = half // 2
    base = jnp.where(rel > 0, half, 0)
    n = jnp.abs(rel)
    nf = jnp.maximum(n, 1).astype(jnp.float32)
    large = max_exact + (jnp.log(nf / max_exact) / math.log(MAX_DISTANCE / max_exact)
                         * (half - max_exact)).astype(jnp.int32)
    large = jnp.minimum(large, half - 1)
    return base + jnp.where(n < max_exact, n, large)


HEAD_PAIRS = Q_PER_KV // 2
EDGE_VARIANTS = 3


def _bias_kernel(rb_ref, bucket_ref, band_ref, o_ref):
    bucket = bucket_ref[...]
    band = band_ref[...] > 0
    col = lax.broadcasted_iota(jnp.int32, bucket.shape, 1)
    valid = (band & (col >= BLOCK), band, band & (col < 2 * BLOCK))
    for h in range(N_HEADS):
        acc = jnp.zeros(bucket.shape, F32)
        for b in range(NUM_BUCKETS):
            acc = jnp.where(bucket == b, rb_ref[b, h], acc)
        kv, g = divmod(h, Q_PER_KV)
        pair, parity = divmod(g, 2)
        for var in range(EDGE_VARIANTS):
            o_ref[var, kv, parity, pair * BLOCK:(pair + 1) * BLOCK, :] = jnp.where(valid[var], acc, NEG_INF)


def _bias_table(rel_bias):
    qi = jnp.arange(BLOCK, dtype=jnp.int32)
    kj = jnp.arange(3 * BLOCK, dtype=jnp.int32)
    rel = kj[None, :] - BLOCK - qi[:, None]
    bucket = _t5_bucket(rel).astype(jnp.int32)
    band = (jnp.abs(rel) <= WINDOW).astype(jnp.int32)
    return pl.pallas_call(
        _bias_kernel,
        in_specs=[pl.BlockSpec(memory_space=pltpu.SMEM),
                  pl.BlockSpec(memory_space=pltpu.VMEM),
                  pl.BlockSpec(memory_space=pltpu.VMEM)],
        out_specs=pl.BlockSpec(memory_space=pltpu.VMEM),
        out_shape=jax.ShapeDtypeStruct((EDGE_VARIANTS, N_KV_HEADS, 2, HEAD_PAIRS * BLOCK, 3 * BLOCK), F32),
        name="bias_table",
    )(rel_bias.astype(F32), bucket, band)


def _attn_kernel(sink_ref, q_ref, kp_ref, kc_ref, kn_ref, bias_ref, og_ref, o_ref):
    q = q_ref[0]
    kvw = jnp.concatenate([kp_ref[0], kc_ref[0], kn_ref[0]], axis=0)
    low = lax.broadcasted_iota(jnp.int32, (1, LANES), 1) < HEAD_DIM
    rowi = lax.broadcasted_iota(jnp.int32, (HEAD_PAIRS * BLOCK, 1), 0)
    swap = lambda slab: pltpu.roll(slab.astype(F32), HEAD_DIM, 1).astype(BF16)
    kslab, vslab = kvw[:, :KV_WIDTH], kvw[:, KV_WIDTH:]
    kslab_sw, vslab_sw = swap(kslab), swap(vslab)
    outs = []
    for kv in range(N_KV_HEADS):
        src = ((kslab, vslab), (kslab_sw, vslab_sw)) if kv == 0 else ((kslab_sw, vslab_sw), (kslab, vslab))
        base = kv * Q_PER_KV * HEAD_DIM
        qpair = jnp.concatenate([q[:, base + j * LANES:base + (j + 1) * LANES] for j in range(HEAD_PAIRS)], axis=0)
        acc = None
        for parity in range(2):
            keep = low if parity == 0 else jnp.logical_not(low)
            kz = jnp.where(keep, src[parity][0], jnp.zeros_like(kslab))
            vz = jnp.where(keep, src[parity][1], jnp.zeros_like(vslab))
            sink = jnp.zeros((HEAD_PAIRS * BLOCK, 1), F32)
            for j in range(HEAD_PAIRS):
                sink = jnp.where(rowi // BLOCK == j, sink_ref[kv * Q_PER_KV + 2 * j + parity], sink)
            s = lax.dot_general(qpair, kz, (((1,), (1,)), ((), ())), preferred_element_type=F32)
            s = s + bias_ref[0, kv, parity]
            m = jnp.maximum(jnp.max(s, axis=-1, keepdims=True), sink)
            p = jnp.exp(s - m)
            denom = jnp.sum(p, axis=-1, keepdims=True) + jnp.exp(sink - m)
            o = jnp.dot(p.astype(BF16), vz, preferred_element_type=F32) * (1.0 / denom)
            acc = o if acc is None else acc + o
        outs += [acc[j * BLOCK:(j + 1) * BLOCK, :] for j in range(HEAD_PAIRS)]
    o_ref[0] = _rms(jnp.concatenate(outs, axis=1), og_ref[...]).astype(o_ref.dtype)


def _attention(q, kv, bias_tab, sink, out_gain):
    B, S, _ = q.shape
    nb = S // BLOCK
    assert nb >= 2, "edge variants assume distinct first and last blocks"
    kvspec = lambda f: pl.BlockSpec((1, BLOCK, 2 * KV_WIDTH), f)
    variant = lambda n: jnp.where(n == 0, 0, jnp.where(n == nb - 1, 2, 1))
    return pl.pallas_call(
        _attn_kernel,
        grid=(B, nb),
        in_specs=[pl.BlockSpec(memory_space=pltpu.SMEM),
                  pl.BlockSpec((1, BLOCK, ATTN_WIDTH), lambda b, n: (b, n, 0)),
                  kvspec(lambda b, n: (b, jnp.maximum(n - 1, 0), 0)),
                  kvspec(lambda b, n: (b, n, 0)),
                  kvspec(lambda b, n: (b, jnp.minimum(n + 1, nb - 1), 0)),
                  pl.BlockSpec((1, N_KV_HEADS, 2, HEAD_PAIRS * BLOCK, 3 * BLOCK),
                               lambda b, n: (variant(n), 0, 0, 0, 0)),
                  pl.BlockSpec((1, ATTN_WIDTH), lambda b, n: (0, 0))],
        out_specs=pl.BlockSpec((1, BLOCK, ATTN_WIDTH), lambda b, n: (b, n, 0)),
        out_shape=jax.ShapeDtypeStruct((B, S, ATTN_WIDTH), BF16),
        compiler_params=_cparams(2),
        name="attention",
    )(sink.astype(F32), q, kv, kv, kv, bias_tab, out_gain.reshape(1, ATTN_WIDTH))


LRU_TC = 128
LRU_PITCH = LRU_TC + SUBLANES
LRU_SLABS = LRU_WIDTH // LANES
HALO = SUBLANES


def _softplus(x):
    return jnp.maximum(x, 0.0) + jnp.log(1.0 + jnp.exp(-jnp.abs(x)))


def _gelu_tanh(x):
    k = math.sqrt(2.0 / math.pi)
    hx = 0.5 * x
    return hx + hx * jnp.tanh(x * (k + (k * 0.044715) * (x * x)))


def _sigmoid(x):
    return 0.5 + 0.5 * jnp.tanh(0.5 * x)


def _rglru_kernel(xr_ref, xp_ref, xn_ref, gr_ref, cw_ref, cb_ref, wg_ref, bg_ref, lam_ref, og_ref,
                  o_ref, sx_ref, a_ref, u_ref, h_ref, carry_ref, hf_ref):
    p = pl.program_id(0)
    i = pl.program_id(1)
    nc = pl.num_programs(1)
    c = i + p * (nc - 1 - 2 * i)
    B = xr_ref.shape[0]
    TC = LRU_TC

    sx_ref[:, HALO:HALO + TC, :] = xr_ref[...]
    sx_ref[:, 0:HALO, :] = jnp.where(c > 0, xp_ref[...], 0.0)
    sx_ref[:, HALO + TC:, :] = jnp.where(c < nc - 1, xn_ref[...], 0.0)
    xc = cb_ref[...][None]
    for j in range(CONV_W):
        off = HALO + j - CONV_LEFT
        xc = xc + cw_ref[j:j + 1, :][None] * sx_ref[:, off:off + TC, :]
    xc2 = xc.reshape(B * TC, LRU_WIDTH)

    g = jnp.dot(xc2.astype(BF16), wg_ref[0], preferred_element_type=F32) + bg_ref[0]
    r = _sigmoid(g[:, :LRU_WIDTH])
    ig = _sigmoid(g[:, LRU_WIDTH:])
    a = jnp.exp(r * (-LRU_C * _softplus(-lam_ref[0])))
    z = 1.0 - a * a
    u = z * lax.rsqrt(jnp.maximum(z, 1e-30)) * ig * xc2
    for b in range(B):
        for s in range(LRU_SLABS):
            a_ref[s, b * LRU_PITCH:b * LRU_PITCH + TC, :] = a[b * TC:(b + 1) * TC, s * LANES:(s + 1) * LANES]
            u_ref[s, b * LRU_PITCH:b * LRU_PITCH + TC, :] = u[b * TC:(b + 1) * TC, s * LANES:(s + 1) * LANES]

    @pl.when(i == 0)
    def _():
        carry_ref[...] = jnp.zeros_like(carry_ref)

    def step(k, hs):
        t = k + p * (TC - 1 - 2 * k)
        out = []
        for s in range(LRU_SLABS):
            idx = pl.ds(t, B, stride=LRU_PITCH)
            hn = a_ref[s, idx, :] * hs[s] + u_ref[s, idx, :]
            h_ref[s, idx, :] = hn
            out.append(hn)
        return tuple(out)

    hs = lax.fori_loop(0, TC, step, tuple(carry_ref[s] for s in range(LRU_SLABS)), unroll=8)
    for s in range(LRU_SLABS):
        carry_ref[s] = hs[s]

    @pl.when(p == 0)
    def _():
        for b in range(B):
            for s in range(LRU_SLABS):
                hf_ref[c, s, b * TC:(b + 1) * TC, :] = h_ref[s, b * LRU_PITCH:b * LRU_PITCH + TC, :].astype(hf_ref.dtype)

    @pl.when(p == 1)
    def _():
        for b in range(B):
            hsum = jnp.concatenate(
                [h_ref[s, b * LRU_PITCH:b * LRU_PITCH + TC, :] + hf_ref[c, s, b * TC:(b + 1) * TC, :].astype(F32)
                 for s in range(LRU_SLABS)], axis=1)
            y = hsum * _gelu_tanh(gr_ref[b])
            o_ref[b] = _rms(y, og_ref[...]).astype(o_ref.dtype)


def _block_diag(w):
    eye = jnp.eye(LRU_BLOCKS, dtype=w.dtype)
    return jnp.einsum('hij,hg->higj', w, eye).reshape(LRU_WIDTH, LRU_WIDTH)


def _rglru(xr, gr, conv_w, conv_b, w_a, b_a, w_i, b_i, lam, out_gain):
    B, S, W = xr.shape
    nc = S // LRU_TC
    hb = LRU_TC // HALO
    wg = jnp.stack([jnp.concatenate([_block_diag(w_a[d]), _block_diag(w_i[d])], axis=1) for d in range(2)]).astype(BF16)
    bg = jnp.concatenate([b_a, b_i], axis=-1).reshape(2, 1, 2 * W).astype(F32)
    chunk = lambda p, i: i + p * (nc - 1 - 2 * i)
    full2 = lambda shape: pl.BlockSpec(shape, lambda p, i: (0,) * len(shape))
    return pl.pallas_call(
        _rglru_kernel,
        grid=(2, nc),
        in_specs=[pl.BlockSpec((B, LRU_TC, W), lambda p, i: (0, chunk(p, i), 0)),
                  pl.BlockSpec((B, HALO, W), lambda p, i: (0, jnp.maximum(chunk(p, i) * hb - 1, 0), 0)),
                  pl.BlockSpec((B, HALO, W), lambda p, i: (0, jnp.minimum((chunk(p, i) + 1) * hb, S // HALO - 1), 0)),
                  pl.BlockSpec((B, LRU_TC, W), lambda p, i: (0, chunk(p, i), 0)),
                  full2((CONV_W, W)),
                  full2((1, W)),
                  pl.BlockSpec((1, W, 2 * W), lambda p, i: (p, 0, 0)),
                  pl.BlockSpec((1, 1, 2 * W), lambda p, i: (p, 0, 0)),
                  pl.BlockSpec((1, 1, W), lambda p, i: (p, 0, 0)),
                  full2((1, W))],
        out_specs=pl.BlockSpec((B, LRU_TC, W), lambda p, i: (0, nc - 1 - p * i, 0)),
        out_shape=jax.ShapeDtypeStruct((B, S, W), BF16),
        scratch_shapes=[pltpu.VMEM((B, LRU_TC + 2 * HALO, W), F32),
                        pltpu.VMEM((LRU_SLABS, B * LRU_PITCH, LANES), F32),
                        pltpu.VMEM((LRU_SLABS, B * LRU_PITCH, LANES), F32),
                        pltpu.VMEM((LRU_SLABS, B * LRU_PITCH, LANES), F32),
                        pltpu.VMEM((LRU_SLABS, B, LANES), F32),
                        pltpu.VMEM((nc, LRU_SLABS, B * LRU_TC, LANES), BF16)],
        compiler_params=_cparams(2),
        name="rglru",
    )(xr, xr, xr, gr, conv_w.astype(F32), conv_b.reshape(1, W).astype(F32), wg, bg,
      lam.reshape(2, 1, W).astype(F32), out_gain.reshape(1, W).astype(F32))


RT_TM = 512
RT_COLS = LANES
RINFO = 8


def _split_bf16(x):
    hi = x.astype(BF16)
    lo = (x - hi.astype(F32)).astype(BF16)
    return hi, lo


def _route_kernel(an_ref, ln_ref, x_ref, wo_ref, g2_ref, wr_ref, br_ref,
                  x1_ref, h2_ref, gt_ref, ei_ref, cnt_ref, wob_ref, wrb_ref, tri_ref, run_ref):
    @pl.when(pl.program_id(0) == 0)
    def _():
        wob_ref[...] = wo_ref[...].astype(BF16)
        hi, lo = _split_bf16(wr_ref[...])
        wrb_ref[:, :RT_COLS] = hi
        wrb_ref[:, RT_COLS:] = lo
        r = lax.broadcasted_iota(jnp.int32, (RT_TM, RT_TM), 0)
        cidx = lax.broadcasted_iota(jnp.int32, (RT_TM, RT_TM), 1)
        tri_ref[...] = (cidx < r).astype(BF16)
        run_ref[...] = jnp.zeros_like(run_ref)

    x1 = (x_ref[...]
          + jnp.dot(an_ref[...], wob_ref[:ATTN_WIDTH, :], preferred_element_type=F32)
          + jnp.dot(ln_ref[...], wob_ref[ATTN_WIDTH:, :], preferred_element_type=F32))
    x1_ref[...] = x1
    h2 = _rms(x1, g2_ref[...])
    h2_ref[...] = _pack_rows(h2)

    hi, lo = _split_bf16(h2)
    t1 = jnp.dot(hi, wrb_ref[...], preferred_element_type=F32)
    t2 = jnp.dot(lo, wrb_ref[:, :RT_COLS], preferred_element_type=F32)
    logit = t1[:, :RT_COLS] + t1[:, RT_COLS:] + t2 + br_ref[...]

    lane = lax.broadcasted_iota(jnp.int32, logit.shape, 1)
    big = jnp.int32(4 * RT_COLS)
    is_g = lane < N_GROUPS
    gl = jnp.where(is_g, logit, -jnp.inf)
    gm = jnp.max(gl, axis=-1, keepdims=True)
    gidx = jnp.min(jnp.where(gl == gm, lane, big), axis=-1, keepdims=True)
    g_p = 1.0 / jnp.sum(jnp.where(is_g, jnp.exp(logit - gm), 0.0), axis=-1, keepdims=True)
    lo_lane = N_GROUPS + gidx * EXPERTS_PER_GROUP
    el = jnp.where((lane >= lo_lane) & (lane < lo_lane + EXPERTS_PER_GROUP), logit, -jnp.inf)
    m1 = jnp.max(el, axis=-1, keepdims=True)
    i1 = jnp.min(jnp.where(el == m1, lane, big), axis=-1, keepdims=True)
    el2 = jnp.where(lane == i1, -jnp.inf, el)
    m2 = jnp.max(el2, axis=-1, keepdims=True)
    i2 = jnp.min(jnp.where(el2 == m2, lane, big), axis=-1, keepdims=True)
    t = jnp.exp(m2 - m1)
    gate1 = g_p / (1.0 + t)
    gate2 = g_p * t / (1.0 + t)
    e1 = i1 - N_GROUPS
    e2 = i2 - N_GROUPS

    oh1 = lane == e1
    oh2 = lane == e2
    oh = (oh1 | oh2).astype(F32)
    cum = jnp.dot(tri_ref[...], oh.astype(BF16), preferred_element_type=F32) + run_ref[...]
    rank1 = jnp.sum(jnp.where(oh1, cum, 0.0), axis=-1, keepdims=True)
    rank2 = jnp.sum(jnp.where(oh2, cum, 0.0), axis=-1, keepdims=True)
    run_ref[...] = run_ref[...] + jnp.sum(oh, axis=0, keepdims=True)
    cnt_ref[...] = run_ref[...].astype(jnp.int32)

    gt_ref[...] = jnp.where(lax.broadcasted_iota(jnp.int32, (RT_TM, RINFO), 1) == 0, gate1, gate2)
    vals = [e1.astype(F32), e2.astype(F32), rank1, rank2]
    ri = jnp.zeros(logit.shape, F32)
    for k, v in enumerate(vals):
        ri = jnp.where(lane == k, v, ri)
    ei_ref[0] = ri.T[:RINFO, :].astype(jnp.int32)


def _out_route(attn_n, lru_n, x2, w_out, ln2, w_group, b_group, w_er, b_er):
    T = x2.shape[0]
    wr = jnp.concatenate([w_group, jnp.transpose(w_er, (1, 0, 2)).reshape(D_MODEL, N_EXPERTS)], axis=1)
    wr = jnp.pad(wr, ((0, 0), (0, RT_COLS - wr.shape[1]))).astype(F32)
    br = jnp.pad(jnp.concatenate([b_group, b_er.reshape(-1)]), (0, RT_COLS - N_GROUPS - N_EXPERTS)).reshape(1, RT_COLS)
    row = lambda w: pl.BlockSpec((RT_TM, w), lambda i: (i, 0))
    const = lambda shape: pl.BlockSpec(shape, lambda i: (0, 0))
    return pl.pallas_call(
        _route_kernel,
        grid=(T // RT_TM,),
        in_specs=[row(ATTN_WIDTH), row(LRU_WIDTH), row(D_MODEL), const((D_MODEL, D_MODEL)), const((1, D_MODEL)),
                  const((D_MODEL, RT_COLS)), const((1, RT_COLS))],
        out_specs=[row(D_MODEL), row(PACKED), row(RINFO),
                   pl.BlockSpec((1, RINFO, RT_TM), lambda i: (i, 0, 0)), const((1, RT_COLS))],
        out_shape=[jax.ShapeDtypeStruct((T, D_MODEL), F32),
                   jax.ShapeDtypeStruct((T, PACKED), U32),
                   jax.ShapeDtypeStruct((T, RINFO), F32),
                   jax.ShapeDtypeStruct((T // RT_TM, RINFO, RT_TM), jnp.int32),
                   jax.ShapeDtypeStruct((1, RT_COLS), jnp.int32)],
        scratch_shapes=[pltpu.VMEM((D_MODEL, D_MODEL), BF16),
                        pltpu.VMEM((D_MODEL, 2 * RT_COLS), BF16),
                        pltpu.VMEM((RT_TM, RT_TM), BF16),
                        pltpu.VMEM((1, RT_COLS), F32)],
        compiler_params=_cparams(1),
        name="out_route",
    )(attn_n, lru_n, x2, w_out, ln2.reshape(1, D_MODEL).astype(F32), wr, br.astype(F32))


def _moe_cap(T):
    A = T * TOP_K
    return ((A + MOE_BLOCK - 1) // MOE_BLOCK) * MOE_BLOCK + N_EXPERTS * MOE_BLOCK


PAD_BITS = tuple(1 << b for b in reversed(range(3, MOE_BLOCK.bit_length() - 1)))


def _layout_kernel(cnt_ref, ei_ref, dest_ref, pstart, be_ref, nu_ref, ge_ref):
    n_blocks = be_ref.shape[0]

    def lay(e, carry):
        start, blk, grp = carry
        pstart[e] = start
        nb = (cnt_ref[0, e] + MOE_BLOCK - 1) // MOE_BLOCK
        ge_ref[grp] = e

        def fill(k, c):
            be_ref[blk + k] = e
            return c
        lax.fori_loop(0, nb, fill, 0)
        return start + nb * MOE_BLOCK, blk + nb, grp + (nb > 0).astype(jnp.int32)
    _, used, groups = lax.fori_loop(0, N_EXPERTS, lay, (jnp.int32(0), jnp.int32(0), jnp.int32(0)))
    nu_ref[0] = used

    def tail(k, c):
        be_ref[k] = N_EXPERTS - 1
        return c
    lax.fori_loop(used, n_blocks, tail, 0)

    def no_group(k, c):
        ge_ref[k] = -1
        return c
    lax.fori_loop(groups, N_EXPERTS + 1, no_group, 0)

    expert = ei_ref[:, 0:TOP_K, :]
    dest = ei_ref[:, TOP_K:2 * TOP_K, :]
    for e in range(N_EXPERTS):
        dest = dest + jnp.where(expert == e, pstart[e], 0)
    dest_ref[...] = dest


def _layout(ei, cnt, n_blocks):
    nt = ei.shape[0]
    smem = pl.BlockSpec(memory_space=pltpu.SMEM)
    vmem = pl.BlockSpec(memory_space=pltpu.VMEM)
    return pl.pallas_call(
        _layout_kernel,
        in_specs=[smem, vmem],
        out_specs=[vmem, smem, smem, smem, smem],
        out_shape=[jax.ShapeDtypeStruct((nt, TOP_K, RT_TM), jnp.int32),
                   jax.ShapeDtypeStruct((N_EXPERTS,), jnp.int32),
                   jax.ShapeDtypeStruct((n_blocks,), jnp.int32),
                   jax.ShapeDtypeStruct((1,), jnp.int32),
                   jax.ShapeDtypeStruct((N_EXPERTS + 1,), jnp.int32)],
        name="layout",
    )(cnt, ei)


def _dispatch_kernel(cnt_ref, pstart, dest_ref, h2_ref, xs_ref, zeros, zsem, sem):
    i = pl.program_id(0)

    def pad_copies(fn):
        for e in range(N_EXPERTS):
            cnt = cnt_ref[0, e]
            head = (-cnt) & (SUBLANES - 1)
            rest = ((-cnt) & (MOE_BLOCK - 1)) - head
            off = pstart[e] + cnt
            for k in range(SUBLANES - 1):
                @pl.when(k < head)
                def _(off=off, k=k):
                    fn(pltpu.make_async_copy(zeros.at[pl.ds(0, 1), :], xs_ref.at[pl.ds(off + k, 1), :], zsem))
            off = off + head
            for bit in PAD_BITS:
                @pl.when((rest & bit) != 0)
                def _(off=off, bit=bit):
                    fn(pltpu.make_async_copy(zeros.at[pl.ds(0, bit), :],
                                             xs_ref.at[pl.ds(pl.multiple_of(off, SUBLANES), bit), :], zsem))
                off = off + (rest & bit)

    @pl.when(i == 0)
    def _():
        zeros[...] = jnp.zeros_like(zeros)
        pad_copies(lambda cp: cp.start())
        pad_copies(lambda cp: cp.wait())

    def issue(r, carry):
        for k in range(TOP_K):
            pltpu.make_async_copy(h2_ref.at[pl.ds(r, 1), :], xs_ref.at[pl.ds(dest_ref[0, k, r], 1), :], sem).start()
        return carry
    lax.fori_loop(0, RT_TM, issue, 0, unroll=8)
    for k in range(TOP_K):
        pltpu.make_async_copy(h2_ref, xs_ref.at[pl.ds(0, RT_TM), :], sem).wait()


def _dispatch(h2p, dest, pstart, cnt, cap):
    T = h2p.shape[0]
    nt = T // RT_TM
    smem = pl.BlockSpec(memory_space=pltpu.SMEM)
    return pl.pallas_call(
        _dispatch_kernel,
        grid=(nt,),
        in_specs=[smem, smem,
                  pl.BlockSpec((1, TOP_K, RT_TM), lambda i: (i, 0, 0), memory_space=pltpu.SMEM),
                  pl.BlockSpec((RT_TM, PACKED), lambda i: (i, 0))],
        out_specs=pl.BlockSpec(memory_space=pl.ANY),
        out_shape=jax.ShapeDtypeStruct((cap, PACKED), U32),
        scratch_shapes=[pltpu.VMEM((MOE_BLOCK // 2, PACKED), U32),
                        pltpu.SemaphoreType.DMA(()),
                        pltpu.SemaphoreType.DMA(())],
        compiler_params=_cparams(1),
        name="dispatch",
    )(cnt, pstart, dest, h2p)


W_SLOTS = 2


def _expert_kernel(be_ref, nu_ref, ge_ref, x_ref, wg_hbm, wu_hbm, wd_hbm, o_ref,
                   wgf, wuf, wdf, wgb, wub, wdb, grp_ref, sems):
    j = pl.program_id(0)

    def weight_copies(e, slot):
        return (pltpu.make_async_copy(wg_hbm.at[e], wgf.at[slot], sems.at[slot, 0]),
                pltpu.make_async_copy(wu_hbm.at[e], wuf.at[slot], sems.at[slot, 1]),
                pltpu.make_async_copy(wd_hbm.at[e], wdf.at[slot], sems.at[slot, 2]))

    @pl.when(j == 0)
    def _():
        grp_ref[0] = 0
        for cp in weight_copies(ge_ref[0], 0):
            cp.start()

    @pl.when(j < nu_ref[0])
    def _():
        first = jnp.logical_or(j == 0, be_ref[j] != be_ref[jnp.maximum(j - 1, 0)])

        @pl.when(first)
        def _():
            grp = grp_ref[0]
            slot = grp % W_SLOTS
            for cp in weight_copies(be_ref[j], slot):
                cp.wait()
            wgb[...] = wgf[slot].astype(BF16)
            wub[...] = wuf[slot].astype(BF16)
            wdb[...] = wdf[slot].astype(BF16)
            nxt = ge_ref[grp + 1]

            @pl.when(nxt >= 0)
            def _():
                for cp in weight_copies(nxt, 1 - slot):
                    cp.start()
            grp_ref[0] = grp + 1

        lo, hi = _unpack_rows(x_ref[...])
        lo = lo.astype(BF16)
        hi = hi.astype(BF16)
        g = (jnp.dot(lo, wgb[:PACKED, :], preferred_element_type=F32)
             + jnp.dot(hi, wgb[PACKED:, :], preferred_element_type=F32))
        u = (jnp.dot(lo, wub[:PACKED, :], preferred_element_type=F32)
             + jnp.dot(hi, wub[PACKED:, :], preferred_element_type=F32))
        h = (g * _sigmoid(g) * u).astype(BF16)
        o_ref[...] = _pack_rows(jnp.dot(h, wdb[...], preferred_element_type=F32))


def _experts(xs, block_expert, n_used, group_expert, w_gate, w_up, w_down):
    cap = xs.shape[0]
    n_blocks = cap // MOE_BLOCK
    last = lambda j, be, nu, ge: jnp.minimum(j, nu[0] - 1)
    hbm = pl.BlockSpec(memory_space=pl.ANY)
    gs = pltpu.PrefetchScalarGridSpec(
        num_scalar_prefetch=3,
        grid=(n_blocks,),
        in_specs=[pl.BlockSpec((MOE_BLOCK, PACKED), lambda j, be, nu, ge: (last(j, be, nu, ge), 0)), hbm, hbm, hbm],
        out_specs=pl.BlockSpec((MOE_BLOCK, PACKED), lambda j, be, nu, ge: (last(j, be, nu, ge), 0)),
        scratch_shapes=[pltpu.VMEM((W_SLOTS, D_MODEL, D_EXPERT), F32),
                        pltpu.VMEM((W_SLOTS, D_MODEL, D_EXPERT), F32),
                        pltpu.VMEM((W_SLOTS, D_EXPERT, D_MODEL), F32),
                        pltpu.VMEM((D_MODEL, D_EXPERT), BF16),
                        pltpu.VMEM((D_MODEL, D_EXPERT), BF16),
                        pltpu.VMEM((D_EXPERT, D_MODEL), BF16),
                        pltpu.SMEM((1,), jnp.int32),
                        pltpu.SemaphoreType.DMA((W_SLOTS, 3))],
    )
    return pl.pallas_call(
        _expert_kernel,
        grid_spec=gs,
        out_shape=jax.ShapeDtypeStruct((cap, PACKED), U32),
        compiler_params=_cparams(1),
        name="experts",
    )(block_expert, n_used, group_expert, xs, w_gate, w_up, w_down)


CB_TM = RT_TM
CB_SLOTS = 2


def _combine_kernel(dcur_ref, dnxt_ref, x1_ref, gt_ref, yb_ref, o_ref, ybuf, sems):
    i = pl.program_id(0)
    nt = pl.num_programs(0)
    slot = i % CB_SLOTS

    def gather(dref, s):
        def issue(r, carry):
            for k in range(TOP_K):
                pltpu.make_async_copy(yb_ref.at[pl.ds(dref[0, k, r], 1), :],
                                      ybuf.at[s, k, pl.ds(r, 1), :], sems.at[s]).start()
            return carry
        lax.fori_loop(0, CB_TM, issue, 0, unroll=8)

    @pl.when(i == 0)
    def _():
        gather(dcur_ref, 0)

    @pl.when(i + 1 < nt)
    def _():
        gather(dnxt_ref, 1 - slot)

    for k in range(TOP_K):
        pltpu.make_async_copy(yb_ref.at[pl.ds(0, CB_TM), :], ybuf.at[slot, k], sems.at[slot]).wait()
    g = gt_ref[...]
    lo1, hi1 = _unpack_rows(ybuf[slot, 0])
    lo2, hi2 = _unpack_rows(ybuf[slot, 1])
    o_ref[:, :PACKED] = x1_ref[:, :PACKED] + g[:, 0:1] * lo1 + g[:, 1:2] * lo2
    o_ref[:, PACKED:] = x1_ref[:, PACKED:] + g[:, 0:1] * hi1 + g[:, 1:2] * hi2


def _combine(x1, gates, yb, dest):
    T = x1.shape[0]
    nt = T // CB_TM
    dspec = lambda f: pl.BlockSpec((1, TOP_K, CB_TM), f, memory_space=pltpu.SMEM)
    return pl.pallas_call(
        _combine_kernel,
        grid=(nt,),
        in_specs=[dspec(lambda i: (i, 0, 0)),
                  dspec(lambda i: (jnp.minimum(i + 1, nt - 1), 0, 0)),
                  pl.BlockSpec((CB_TM, D_MODEL), lambda i: (i, 0)),
                  pl.BlockSpec((CB_TM, RINFO), lambda i: (i, 0)),
                  pl.BlockSpec(memory_space=pl.ANY)],
        out_specs=pl.BlockSpec((CB_TM, D_MODEL), lambda i: (i, 0)),
        out_shape=jax.ShapeDtypeStruct((T, D_MODEL), F32),
        scratch_shapes=[pltpu.VMEM((CB_SLOTS, TOP_K, CB_TM, PACKED), U32), pltpu.SemaphoreType.DMA((CB_SLOTS,))],
        compiler_params=_cparams(1),
        name="combine",
    )(dest, dest, x1, gates, yb)


def _layer(x, rel_bias, ln1, w_in, q_norm, k_norm, attn_sink, conv_w, conv_b, lru_wa, lru_ba, lru_wi, lru_bi,
           lru_lambda, out_norm_attn, out_norm_lru, w_out, ln2, w_group, b_group, w_er, b_er, w_gate, w_up, w_down):
    B, S, D = x.shape
    T = B * S
    x2 = x.reshape(T, D)
    q, kv, xr, gr = _in_proj(x2, ln1, w_in, q_norm, k_norm)
    bias_tab = _bias_table(rel_bias)
    attn_n = _attention(q.reshape(B, S, ATTN_WIDTH), kv.reshape(B, S, 2 * KV_WIDTH), bias_tab, attn_sink,
                        out_norm_attn)
    lru_n = _rglru(xr.reshape(B, S, LRU_WIDTH), gr.reshape(B, S, LRU_WIDTH), conv_w, conv_b,
                   lru_wa, lru_ba, lru_wi, lru_bi, lru_lambda, out_norm_lru)
    x1, h2, gates, ei, cnt = _out_route(attn_n.reshape(T, ATTN_WIDTH), lru_n.reshape(T, LRU_WIDTH), x2, w_out, ln2,
                                        w_group, b_group, w_er, b_er)
    cap = _moe_cap(T)
    dest, pstart, block_expert, n_used, group_expert = _layout(ei, cnt, cap // MOE_BLOCK)
    xs = _dispatch(h2, dest, pstart, cnt, cap)
    yb = _experts(xs, block_expert, n_used, group_expert, w_gate, w_up, w_down)
    out = _combine(x1, gates, yb, dest)
    return out.reshape(B, S, D)


def kernel(x, rel_bias, ln1, w_in, q_norm, k_norm, attn_sink, conv_w, conv_b, lru_wa, lru_ba, lru_wi, lru_bi,
           lru_lambda, out_norm_attn, out_norm_lru, w_out, ln2, w_group, b_group, w_expert_router, b_expert_router,
           w_gate, w_up, w_down):
    depth = ln1.shape[0]
    for l in range(depth):
        x = _layer(x, rel_bias, ln1[l], w_in[l], q_norm[l], k_norm[l], attn_sink[l], conv_w[l], conv_b[l],
                   lru_wa[l], lru_ba[l], lru_wi[l], lru_bi[l], lru_lambda[l], out_norm_attn[l], out_norm_lru[l],
                   w_out[l], ln2[l], w_group[l], b_group[l], w_expert_router[l], b_expert_router[l],
                   w_gate[l], w_up[l], w_down[l])
    return x
```

```python
import functools
import math

import jax
import jax.numpy as jnp
import numpy as np
from jax import lax
from jax.experimental import pallas as pl
from jax.experimental.pallas import tpu as pltpu
from jax.experimental.pallas import tpu_sc as plsc

D_MODEL = 1024
N_HEADS = 8
N_KV_HEADS = 2
HEAD_DIM = 64
Q_PER_KV = N_HEADS // N_KV_HEADS
ATTN_WIDTH = N_HEADS * HEAD_DIM
KV_WIDTH = N_KV_HEADS * HEAD_DIM
WINDOW = 128
BLOCK = 128
NUM_BUCKETS = 32
MAX_DISTANCE = 128
LRU_WIDTH = D_MODEL - ATTN_WIDTH
LRU_BLOCKS = 8
LRU_BLOCK_DIM = LRU_WIDTH // LRU_BLOCKS
LRU_C = 8.0
CONV_W = 4
CONV_LEFT = 2
N_GROUPS = 4
EXPERTS_PER_GROUP = 8
N_EXPERTS = N_GROUPS * EXPERTS_PER_GROUP
TOP_K = 2
D_EXPERT = 512
MOE_BLOCK = 256
EPS = 1e-6
NEG_INF = -1e30

LANES = 128
SUBLANES = 8
VMEM_LIMIT = 56 * 1024 * 1024

F32 = jnp.float32
BF16 = jnp.bfloat16


def _cparams(n_axes, vmem=VMEM_LIMIT):
    return pltpu.CompilerParams(dimension_semantics=("arbitrary",) * n_axes, vmem_limit_bytes=vmem)


def _rms(x, gain):
    return x * lax.rsqrt(jnp.mean(x * x, axis=-1, keepdims=True) + EPS) * gain


U32 = jnp.uint32
HI_MASK = 0xFFFF0000
PACKED = D_MODEL // 2


def _pack_rows(x):
    h = x.shape[1] // 2
    lo = lax.bitcast_convert_type(x[:, :h].astype(BF16).astype(F32), U32) >> 16
    hi = lax.bitcast_convert_type(x[:, h:].astype(BF16).astype(F32), U32) & jnp.uint32(HI_MASK)
    return lo | hi


def _unpack_rows(p):
    lo = lax.bitcast_convert_type(p << 16, F32)
    hi = lax.bitcast_convert_type(p & jnp.uint32(HI_MASK), F32)
    return lo, hi


IN_TM = 512


def _head_rms(x, n_heads, gain):
    head = lax.broadcasted_iota(jnp.int32, (1, n_heads * HEAD_DIM), 1) // HEAD_DIM
    x2 = x * x
    scale = jnp.zeros_like(x)
    for h in range(n_heads):
        ms = jnp.sum(jnp.where(head == h, x2, 0.0), axis=-1, keepdims=True) * (1.0 / HEAD_DIM)
        scale = jnp.where(head == h, lax.rsqrt(ms + EPS), scale)
    return x * scale * gain


def _in_proj_kernel(x_ref, g_ref, w_ref, qg_ref, kg_ref, q_ref, kv_ref, xr_ref, gr_ref, wb_ref):
    @pl.when(pl.program_id(0) == 0)
    def _():
        wb_ref[...] = w_ref[...].astype(BF16)

    h = _rms(x_ref[...], g_ref[...]).astype(BF16)
    c_k = ATTN_WIDTH
    c_v = c_k + KV_WIDTH
    c_x = c_v + KV_WIDTH
    c_g = c_x + LRU_WIDTH
    q = jnp.dot(h, wb_ref[:, :c_k], preferred_element_type=F32)
    q_ref[...] = _head_rms(q, N_HEADS, qg_ref[...]).astype(BF16)
    k = jnp.dot(h, wb_ref[:, c_k:c_v], preferred_element_type=F32)
    kv_ref[:, :KV_WIDTH] = _head_rms(k, N_KV_HEADS, kg_ref[...]).astype(BF16)
    kv_ref[:, KV_WIDTH:] = jnp.dot(h, wb_ref[:, c_v:c_x], preferred_element_type=F32).astype(BF16)
    xr_ref[...] = jnp.dot(h, wb_ref[:, c_x:c_g], preferred_element_type=F32)
    gr_ref[...] = jnp.dot(h, wb_ref[:, c_g:], preferred_element_type=F32)


def _in_proj(x2, ln1, w_in, q_gain, k_gain):
    T = x2.shape[0]
    n_in = w_in.shape[1]
    row = lambda w: pl.BlockSpec((IN_TM, w), lambda i: (i, 0))
    qg = (jnp.tile(q_gain.astype(F32), N_HEADS) * (HEAD_DIM ** -0.5)).reshape(1, ATTN_WIDTH)
    kg = jnp.tile(k_gain.astype(F32), N_KV_HEADS).reshape(1, KV_WIDTH)
    return pl.pallas_call(
        _in_proj_kernel,
        grid=(T // IN_TM,),
        in_specs=[row(D_MODEL),
                  pl.BlockSpec((1, D_MODEL), lambda i: (0, 0)),
                  pl.BlockSpec((D_MODEL, n_in), lambda i: (0, 0)),
                  pl.BlockSpec((1, ATTN_WIDTH), lambda i: (0, 0)),
                  pl.BlockSpec((1, KV_WIDTH), lambda i: (0, 0))],
        out_specs=[row(ATTN_WIDTH), row(2 * KV_WIDTH), row(LRU_WIDTH), row(LRU_WIDTH)],
        out_shape=[jax.ShapeDtypeStruct((T, ATTN_WIDTH), BF16),
                   jax.ShapeDtypeStruct((T, 2 * KV_WIDTH), BF16),
                   jax.ShapeDtypeStruct((T, LRU_WIDTH), F32),
                   jax.ShapeDtypeStruct((T, LRU_WIDTH), F32)],
        scratch_shapes=[pltpu.VMEM((D_MODEL, n_in), BF16)],
        compiler_params=_cparams(1),
        name="in_proj",
    )(x2, ln1.reshape(1, D_MODEL), w_in, qg, kg)


def _t5_bucket(rel):
    half = NUM_BUCKETS // 2
    max_exact = half // 2
    base = jnp.where(rel > 0, half, 0)
    n = jnp.abs(rel)
    nf = jnp.maximum(n, 1).astype(jnp.float32)
    large = max_exact + (jnp.log(nf / max_exact) / math.log(MAX_DISTANCE / max_exact)
                         * (half - max_exact)).astype(jnp.int32)
    large = jnp.minimum(large, half - 1)
    return base + jnp.where(n < max_exact, n, large)


HEAD_PAIRS = Q_PER_KV // 2
EDGE_VARIANTS = 3


def _bias_kernel(rb_ref, bucket_ref, band_ref, o_ref):
    bucket = bucket_ref[...]
    band = band_ref[...] > 0
    col = lax.broadcasted_iota(jnp.int32, bucket.shape, 1)
    valid = (band & (col >= BLOCK), band, band & (col < 2 * BLOCK))
    for h in range(N_HEADS):
        acc = jnp.zeros(bucket.shape, F32)
        for b in range(NUM_BUCKETS):
            acc = jnp.where(bucket == b, rb_ref[b, h], acc)
        kv, g = divmod(h, Q_PER_KV)
        pair, parity = divmod(g, 2)
        for var in range(EDGE_VARIANTS):
            o_ref[var, kv, parity, pair * BLOCK:(pair + 1) * BLOCK, :] = jnp.where(valid[var], acc, NEG_INF)


def _bias_table(rel_bias):
    qi = jnp.arange(BLOCK, dtype=jnp.int32)
    kj = jnp.arange(3 * BLOCK, dtype=jnp.int32)
    rel = kj[None, :] - BLOCK - qi[:, None]
    bucket = _t5_bucket(rel).astype(jnp.int32)
    band = (jnp.abs(rel) <= WINDOW).astype(jnp.int32)
    return pl.pallas_call(
        _bias_kernel,
        in_specs=[pl.BlockSpec(memory_space=pltpu.SMEM),
                  pl.BlockSpec(memory_space=pltpu.VMEM),
                  pl.BlockSpec(memory_space=pltpu.VMEM)],
        out_specs=pl.BlockSpec(memory_space=pltpu.VMEM),
        out_shape=jax.ShapeDtypeStruct((EDGE_VARIANTS, N_KV_HEADS, 2, HEAD_PAIRS * BLOCK, 3 * BLOCK), F32),
        name="bias_table",
    )(rel_bias.astype(F32), bucket, band)


def _attn_kernel(sink_ref, q_ref, kp_ref, kc_ref, kn_ref, bias_ref, og_ref, o_ref):
    q = q_ref[0]
    kvw = jnp.concatenate([kp_ref[0], kc_ref[0], kn_ref[0]], axis=0)
    low = lax.broadcasted_iota(jnp.int32, (1, LANES), 1) < HEAD_DIM
    rowi = lax.broadcasted_iota(jnp.int32, (HEAD_PAIRS * BLOCK, 1), 0)
    swap = lambda slab: pltpu.roll(slab.astype(F32), HEAD_DIM, 1).astype(BF16)
    kslab, vslab = kvw[:, :KV_WIDTH], kvw[:, KV_WIDTH:]
    kslab_sw, vslab_sw = swap(kslab), swap(vslab)
    outs = []
    for kv in range(N_KV_HEADS):
        src = ((kslab, vslab), (kslab_sw, vslab_sw)) if kv == 0 else ((kslab_sw, vslab_sw), (kslab, vslab))
        base = kv * Q_PER_KV * HEAD_DIM
        qpair = jnp.concatenate([q[:, base + j * LANES:base + (j + 1) * LANES] for j in range(HEAD_PAIRS)], axis=0)
        acc = None
        for parity in range(2):
            keep = low if parity == 0 else jnp.logical_not(low)
            kz = jnp.where(keep, src[parity][0], jnp.zeros_like(kslab))
            vz = jnp.where(keep, src[parity][1], jnp.zeros_like(vslab))
            sink = jnp.zeros((HEAD_PAIRS * BLOCK, 1), F32)
            for j in range(HEAD_PAIRS):
                sink = jnp.where(rowi // BLOCK == j, sink_ref[kv * Q_PER_KV + 2 * j + parity], sink)
            s = lax.dot_general(qpair, kz, (((1,), (1,)), ((), ())), preferred_element_type=F32)
            s = s + bias_ref[0, kv, parity]
            m = jnp.maximum(jnp.max(s, axis=-1, keepdims=True), sink)
            p = jnp.exp(s - m)
            denom = jnp.sum(p, axis=-1, keepdims=True) + jnp.exp(sink - m)
            o = jnp.dot(p.astype(BF16), vz, preferred_element_type=F32) * (1.0 / denom)
            acc = o if acc is None else acc + o
        outs += [acc[j * BLOCK:(j + 1) * BLOCK, :] for j in range(HEAD_PAIRS)]
    o_ref[0] = _rms(jnp.concatenate(outs, axis=1), og_ref[...]).astype(o_ref.dtype)


def _attention(q, kv, bias_tab, sink, out_gain):
    B, S, _ = q.shape
    nb = S // BLOCK
    assert nb >= 2, "edge variants assume distinct first and last blocks"
    kvspec = lambda f: pl.BlockSpec((1, BLOCK, 2 * KV_WIDTH), f)
    variant = lambda n: jnp.where(n == 0, 0, jnp.where(n == nb - 1, 2, 1))
    return pl.pallas_call(
        _attn_kernel,
        grid=(B, nb),
        in_specs=[pl.BlockSpec(memory_space=pltpu.SMEM),
                  pl.BlockSpec((1, BLOCK, ATTN_WIDTH), lambda b, n: (b, n, 0)),
                  kvspec(lambda b, n: (b, jnp.maximum(n - 1, 0), 0)),
                  kvspec(lambda b, n: (b, n, 0)),
                  kvspec(lambda b, n: (b, jnp.minimum(n + 1, nb - 1), 0)),
                  pl.BlockSpec((1, N_KV_HEADS, 2, HEAD_PAIRS * BLOCK, 3 * BLOCK),
                               lambda b, n: (variant(n), 0, 0, 0, 0)),
                  pl.BlockSpec((1, ATTN_WIDTH), lambda b, n: (0, 0))],
        out_specs=pl.BlockSpec((1, BLOCK, ATTN_WIDTH), lambda b, n: (b, n, 0)),
        out_shape=jax.ShapeDtypeStruct((B, S, ATTN_WIDTH), BF16),
        compiler_params=_cparams(2),
        name="attention",
    )(sink.astype(F32), q, kv, kv, kv, bias_tab, out_gain.reshape(1, ATTN_WIDTH))


LRU_TC = 128
LRU_PITCH = LRU_TC + SUBLANES
LRU_SLABS = LRU_WIDTH // LANES
HALO = SUBLANES


def _softplus(x):
    return jnp.maximum(x, 0.0) + jnp.log(1.0 + jnp.exp(-jnp.abs(x)))


def _gelu_tanh(x):
    k = math.sqrt(2.0 / math.pi)
    hx = 0.5 * x
    return hx + hx * jnp.tanh(x * (k + (k * 0.044715) * (x * x)))


def _sigmoid(x):
    return 0.5 + 0.5 * jnp.tanh(0.5 * x)


def _rglru_kernel(xr_ref, xp_ref, xn_ref, gr_ref, cw_ref, cb_ref, wg_ref, bg_ref, lam_ref, og_ref,
                  o_ref, sx_ref, a_ref, u_ref, h_ref, carry_ref, hf_ref):
    p = pl.program_id(0)
    i = pl.program_id(1)
    nc = pl.num_programs(1)
    c = i + p * (nc - 1 - 2 * i)
    B = xr_ref.shape[0]
    TC = LRU_TC

    sx_ref[:, HALO:HALO + TC, :] = xr_ref[...]
    sx_ref[:, 0:HALO, :] = jnp.where(c > 0, xp_ref[...], 0.0)
    sx_ref[:, HALO + TC:, :] = jnp.where(c < nc - 1, xn_ref[...], 0.0)
    xc = cb_ref[...][None]
    for j in range(CONV_W):
        off = HALO + j - CONV_LEFT
        xc = xc + cw_ref[j:j + 1, :][None] * sx_ref[:, off:off + TC, :]
    xc2 = xc.reshape(B * TC, LRU_WIDTH)

    g = jnp.dot(xc2.astype(BF16), wg_ref[0], preferred_element_type=F32) + bg_ref[0]
    r = _sigmoid(g[:, :LRU_WIDTH])
    ig = _sigmoid(g[:, LRU_WIDTH:])
    a = jnp.exp(r * (-LRU_C * _softplus(-lam_ref[0])))
    z = 1.0 - a * a
    u = z * lax.rsqrt(jnp.maximum(z, 1e-30)) * ig * xc2
    for b in range(B):
        for s in range(LRU_SLABS):
            a_ref[s, b * LRU_PITCH:b * LRU_PITCH + TC, :] = a[b * TC:(b + 1) * TC, s * LANES:(s + 1) * LANES]
            u_ref[s, b * LRU_PITCH:b * LRU_PITCH + TC, :] = u[b * TC:(b + 1) * TC, s * LANES:(s + 1) * LANES]

    @pl.when(i == 0)
    def _():
        carry_ref[...] = jnp.zeros_like(carry_ref)

    def step(k, hs):
        t = k + p * (TC - 1 - 2 * k)
        out = []
        for s in range(LRU_SLABS):
            idx = pl.ds(t, B, stride=LRU_PITCH)
            hn = a_ref[s, idx, :] * hs[s] + u_ref[s, idx, :]
            h_ref[s, idx, :] = hn
            out.append(hn)
        return tuple(out)

    hs = lax.fori_loop(0, TC, step, tuple(carry_ref[s] for s in range(LRU_SLABS)), unroll=8)
    for s in range(LRU_SLABS):
        carry_ref[s] = hs[s]

    @pl.when(p == 0)
    def _():
        for b in range(B):
            for s in range(LRU_SLABS):
                hf_ref[c, s, b * TC:(b + 1) * TC, :] = h_ref[s, b * LRU_PITCH:b * LRU_PITCH + TC, :].astype(hf_ref.dtype)

    @pl.when(p == 1)
    def _():
        for b in range(B):
            hsum = jnp.concatenate(
                [h_ref[s, b * LRU_PITCH:b * LRU_PITCH + TC, :] + hf_ref[c, s, b * TC:(b + 1) * TC, :].astype(F32)
                 for s in range(LRU_SLABS)], axis=1)
            y = hsum * _gelu_tanh(gr_ref[b])
            o_ref[b] = _rms(y, og_ref[...]).astype(o_ref.dtype)


def _block_diag(w):
    eye = jnp.eye(LRU_BLOCKS, dtype=w.dtype)
    return jnp.einsum('hij,hg->higj', w, eye).reshape(LRU_WIDTH, LRU_WIDTH)


def _rglru(xr, gr, conv_w, conv_b, w_a, b_a, w_i, b_i, lam, out_gain):
    B, S, W = xr.shape
    nc = S // LRU_TC
    hb = LRU_TC // HALO
    wg = jnp.stack([jnp.concatenate([_block_diag(w_a[d]), _block_diag(w_i[d])], axis=1) for d in range(2)]).astype(BF16)
    bg = jnp.concatenate([b_a, b_i], axis=-1).reshape(2, 1, 2 * W).astype(F32)
    chunk = lambda p, i: i + p * (nc - 1 - 2 * i)
    full2 = lambda shape: pl.BlockSpec(shape, lambda p, i: (0,) * len(shape))
    return pl.pallas_call(
        _rglru_kernel,
        grid=(2, nc),
        in_specs=[pl.BlockSpec((B, LRU_TC, W), lambda p, i: (0, chunk(p, i), 0)),
                  pl.BlockSpec((B, HALO, W), lambda p, i: (0, jnp.maximum(chunk(p, i) * hb - 1, 0), 0)),
                  pl.BlockSpec((B, HALO, W), lambda p, i: (0, jnp.minimum((chunk(p, i) + 1) * hb, S // HALO - 1), 0)),
                  pl.BlockSpec((B, LRU_TC, W), lambda p, i: (0, chunk(p, i), 0)),
                  full2((CONV_W, W)),
                  full2((1, W)),
                  pl.BlockSpec((1, W, 2 * W), lambda p, i: (p, 0, 0)),
                  pl.BlockSpec((1, 1, 2 * W), lambda p, i: (p, 0, 0)),
                  pl.BlockSpec((1, 1, W), lambda p, i: (p, 0, 0)),
                  full2((1, W))],
        out_specs=pl.BlockSpec((B, LRU_TC, W), lambda p, i: (0, nc - 1 - p * i, 0)),
        out_shape=jax.ShapeDtypeStruct((B, S, W), BF16),
        scratch_shapes=[pltpu.VMEM((B, LRU_TC + 2 * HALO, W), F32),
                        pltpu.VMEM((LRU_SLABS, B * LRU_PITCH, LANES), F32),
                        pltpu.VMEM((LRU_SLABS, B * LRU_PITCH, LANES), F32),
                        pltpu.VMEM((LRU_SLABS, B * LRU_PITCH, LANES), F32),
                        pltpu.VMEM((LRU_SLABS, B, LANES), F32),
                        pltpu.VMEM((nc, LRU_SLABS, B * LRU_TC, LANES), BF16)],
        compiler_params=_cparams(2),
        name="rglru",
    )(xr, xr, xr, gr, conv_w.astype(F32), conv_b.reshape(1, W).astype(F32), wg, bg,
      lam.reshape(2, 1, W).astype(F32), out_gain.reshape(1, W).astype(F32))


RT_TM = 512
RT_COLS = LANES
RINFO = 8


def _split_bf16(x):
    hi = x.astype(BF16)
    lo = (x - hi.astype(F32)).astype(BF16)
    return hi, lo


def _route_kernel(an_ref, ln_ref, x_ref, wo_ref, g2_ref, wr_ref, br_ref,
                  x1_ref, h2_ref, gt_ref, ei_ref, cnt_ref, wob_ref, wrb_ref, tri_ref, run_ref):
    @pl.when(pl.program_id(0) == 0)
    def _():
        wob_ref[...] = wo_ref[...].astype(BF16)
        hi, lo = _split_bf16(wr_ref[...])
        wrb_ref[:, :RT_COLS] = hi
        wrb_ref[:, RT_COLS:] = lo
        r = lax.broadcasted_iota(jnp.int32, (RT_TM, RT_TM), 0)
        cidx = lax.broadcasted_iota(jnp.int32, (RT_TM, RT_TM), 1)
        tri_ref[...] = (cidx < r).astype(BF16)
        run_ref[...] = jnp.zeros_like(run_ref)

    x1 = (x_ref[...]
          + jnp.dot(an_ref[...], wob_ref[:ATTN_WIDTH, :], preferred_element_type=F32)
          + jnp.dot(ln_ref[...], wob_ref[ATTN_WIDTH:, :], preferred_element_type=F32))
    x1_ref[...] = x1
    h2 = _rms(x1, g2_ref[...])
    h2_ref[...] = _pack_rows(h2)

    hi, lo = _split_bf16(h2)
    t1 = jnp.dot(hi, wrb_ref[...], preferred_element_type=F32)
    t2 = jnp.dot(lo, wrb_ref[:, :RT_COLS], preferred_element_type=F32)
    logit = t1[:, :RT_COLS] + t1[:, RT_COLS:] + t2 + br_ref[...]

    lane = lax.broadcasted_iota(jnp.int32, logit.shape, 1)
    big = jnp.int32(4 * RT_COLS)
    is_g = lane < N_GROUPS
    gl = jnp.where(is_g, logit, -jnp.inf)
    gm = jnp.max(gl, axis=-1, keepdims=True)
    gidx = jnp.min(jnp.where(gl == gm, lane, big), axis=-1, keepdims=True)
    g_p = 1.0 / jnp.sum(jnp.where(is_g, jnp.exp(logit - gm), 0.0), axis=-1, keepdims=True)
    lo_lane = N_GROUPS + gidx * EXPERTS_PER_GROUP
    el = jnp.where((lane >= lo_lane) & (lane < lo_lane + EXPERTS_PER_GROUP), logit, -jnp.inf)
    m1 = jnp.max(el, axis=-1, keepdims=True)
    i1 = jnp.min(jnp.where(el == m1, lane, big), axis=-1, keepdims=True)
    el2 = jnp.where(lane == i1, -jnp.inf, el)
    m2 = jnp.max(el2, axis=-1, keepdims=True)
    i2 = jnp.min(jnp.where(el2 == m2, lane, big), axis=-1, keepdims=True)
    t = jnp.exp(m2 - m1)
    gate1 = g_p / (1.0 + t)
    gate2 = g_p * t / (1.0 + t)
    e1 = i1 - N_GROUPS
    e2 = i2 - N_GROUPS

    oh1 = lane == e1
    oh2 = lane == e2
    oh = (oh1 | oh2).astype(F32)
    cum = jnp.dot(tri_ref[...], oh.astype(BF16), preferred_element_type=F32) + run_ref[...]
    rank1 = jnp.sum(jnp.where(oh1, cum, 0.0), axis=-1, keepdims=True)
    rank2 = jnp.sum(jnp.where(oh2, cum, 0.0), axis=-1, keepdims=True)
    run_ref[...] = run_ref[...] + jnp.sum(oh, axis=0, keepdims=True)
    cnt_ref[...] = run_ref[...].astype(jnp.int32)

    gt_ref[...] = jnp.where(lax.broadcasted_iota(jnp.int32, (RT_TM, RINFO), 1) == 0, gate1, gate2)
    vals = [e1.astype(F32), e2.astype(F32), rank1, rank2]
    ri = jnp.zeros(logit.shape, F32)
    for k, v in enumerate(vals):
        ri = jnp.where(lane == k, v, ri)
    ei_ref[0] = ri.T[:RINFO, :].astype(jnp.int32)


def _out_route(attn_n, lru_n, x2, w_out, ln2, w_group, b_group, w_er, b_er):
    T = x2.shape[0]
    wr = jnp.concatenate([w_group, jnp.transpose(w_er, (1, 0, 2)).reshape(D_MODEL, N_EXPERTS)], axis=1)
    wr = jnp.pad(wr, ((0, 0), (0, RT_COLS - wr.shape[1]))).astype(F32)
    br = jnp.pad(jnp.concatenate([b_group, b_er.reshape(-1)]), (0, RT_COLS - N_GROUPS - N_EXPERTS)).reshape(1, RT_COLS)
    row = lambda w: pl.BlockSpec((RT_TM, w), lambda i: (i, 0))
    const = lambda shape: pl.BlockSpec(shape, lambda i: (0, 0))
    return pl.pallas_call(
        _route_kernel,
        grid=(T // RT_TM,),
        in_specs=[row(ATTN_WIDTH), row(LRU_WIDTH), row(D_MODEL), const((D_MODEL, D_MODEL)), const((1, D_MODEL)),
                  const((D_MODEL, RT_COLS)), const((1, RT_COLS))],
        out_specs=[row(D_MODEL), row(PACKED), row(RINFO),
                   pl.BlockSpec((1, RINFO, RT_TM), lambda i: (i, 0, 0)), const((1, RT_COLS))],
        out_shape=[jax.ShapeDtypeStruct((T, D_MODEL), F32),
                   jax.ShapeDtypeStruct((T, PACKED), U32),
                   jax.ShapeDtypeStruct((T, RINFO), F32),
                   jax.ShapeDtypeStruct((T // RT_TM, RINFO, RT_TM), jnp.int32),
                   jax.ShapeDtypeStruct((1, RT_COLS), jnp.int32)],
        scratch_shapes=[pltpu.VMEM((D_MODEL, D_MODEL), BF16),
                        pltpu.VMEM((D_MODEL, 2 * RT_COLS), BF16),
                        pltpu.VMEM((RT_TM, RT_TM), BF16),
                        pltpu.VMEM((1, RT_COLS), F32)],
        compiler_params=_cparams(1),
        name="out_route",
    )(attn_n, lru_n, x2, w_out, ln2.reshape(1, D_MODEL).astype(F32), wr, br.astype(F32))


def _moe_cap(T):
    A = T * TOP_K
    return ((A + MOE_BLOCK - 1) // MOE_BLOCK) * MOE_BLOCK + N_EXPERTS * MOE_BLOCK


PAD_BITS = tuple(1 << b for b in reversed(range(3, MOE_BLOCK.bit_length() - 1)))


def _layout_kernel(cnt_ref, ei_ref, dest_ref, pstart, be_ref, nu_ref, ge_ref):
    n_blocks = be_ref.shape[0]

    def lay(e, carry):
        start, blk, grp = carry
        pstart[e] = start
        nb = (cnt_ref[0, e] + MOE_BLOCK - 1) // MOE_BLOCK
        ge_ref[grp] = e

        def fill(k, c):
            be_ref[blk + k] = e
            return c
        lax.fori_loop(0, nb, fill, 0)
        return start + nb * MOE_BLOCK, blk + nb, grp + (nb > 0).astype(jnp.int32)
    _, used, groups = lax.fori_loop(0, N_EXPERTS, lay, (jnp.int32(0), jnp.int32(0), jnp.int32(0)))
    nu_ref[0] = used

    def tail(k, c):
        be_ref[k] = N_EXPERTS - 1
        return c
    lax.fori_loop(used, n_blocks, tail, 0)

    def no_group(k, c):
        ge_ref[k] = -1
        return c
    lax.fori_loop(groups, N_EXPERTS + 1, no_group, 0)

    expert = ei_ref[:, 0:TOP_K, :]
    dest = ei_ref[:, TOP_K:2 * TOP_K, :]
    for e in range(N_EXPERTS):
        dest = dest + jnp.where(expert == e, pstart[e], 0)
    dest_ref[...] = dest


def _layout(ei, cnt, n_blocks):
    nt = ei.shape[0]
    smem = pl.BlockSpec(memory_space=pltpu.SMEM)
    vmem = pl.BlockSpec(memory_space=pltpu.VMEM)
    return pl.pallas_call(
        _layout_kernel,
        in_specs=[smem, vmem],
        out_specs=[vmem, smem, smem, smem, smem],
        out_shape=[jax.ShapeDtypeStruct((nt, TOP_K, RT_TM), jnp.int32),
                   jax.ShapeDtypeStruct((N_EXPERTS,), jnp.int32),
                   jax.ShapeDtypeStruct((n_blocks,), jnp.int32),
                   jax.ShapeDtypeStruct((1,), jnp.int32),
                   jax.ShapeDtypeStruct((N_EXPERTS + 1,), jnp.int32)],
        name="layout",
    )(cnt, ei)


SC_CHUNK = 64
SC_BUFS = 3
SC_LEAD = SC_BUFS - 1


def _sc_workers():
    info = plsc.get_sparse_core_info()
    return info.num_cores, info.num_subcores


def _sc_ring(n_chunks, read, write):
    for c in range(min(SC_LEAD, n_chunks)):
        for cp in read(c):
            cp.start()
    reclaimed = set()
    for c in range(n_chunks):
        for cp in read(c):
            cp.wait()
        for cp in write(c):
            cp.start()
        nxt = c + SC_LEAD
        if nxt < n_chunks:
            if nxt - SC_BUFS >= 0:
                for cp in write(nxt - SC_BUFS):
                    cp.wait()
                reclaimed.add(nxt - SC_BUFS)
            for cp in read(nxt):
                cp.start()
    for c in range(n_chunks):
        if c not in reclaimed:
            for cp in write(c):
                cp.wait()


def _sc_dispatch(h2p, dest, cap):
    T = h2p.shape[0]
    nc, ns = _sc_workers()
    per_w = T // (nc * ns)
    n_ch = per_w // SC_CHUNK
    assert dest.shape == (nc * ns, TOP_K, per_w) and per_w % SC_CHUNK == 0
    idx = dest.reshape(nc * ns, TOP_K * n_ch, SC_CHUNK)
    mesh = plsc.VectorSubcoreMesh(core_axis_name="c", subcore_axis_name="s")

    @functools.partial(
        pl.kernel, mesh=mesh,
        out_type=jax.ShapeDtypeStruct((cap, PACKED), U32),
        scratch_types=[pltpu.VMEM((TOP_K * n_ch, SC_CHUNK), jnp.int32),
                       pltpu.VMEM((SC_BUFS, SC_CHUNK, PACKED), U32),
                       pltpu.SemaphoreType.DMA((SC_BUFS,)),
                       pltpu.SemaphoreType.DMA((SC_BUFS,))])
    def scatter(src_hbm, idx_hbm, out_hbm, idx_v, rows_v, rsem, wsem):
        wid = lax.axis_index("s") * nc + lax.axis_index("c")
        base = pl.multiple_of(wid * per_w, per_w)
        pltpu.sync_copy(idx_hbm.at[wid], idx_v)

        def read(c):
            b = c % SC_BUFS
            return [pltpu.make_async_copy(src_hbm.at[pl.ds(base + c * SC_CHUNK, SC_CHUNK)], rows_v.at[b], rsem.at[b])]

        def write(c):
            b = c % SC_BUFS
            return [pltpu.make_async_copy(rows_v.at[b], out_hbm.at[idx_v.at[k * n_ch + c]], wsem.at[b])
                    for k in range(TOP_K)]
        _sc_ring(n_ch, read, write)

    return scatter(h2p, idx)


def _sc_gather(yb, dest):
    nt, _, tm = dest.shape
    nc, ns = _sc_workers()
    n_rows = nt * TOP_K * tm
    per_w = n_rows // (nc * ns)
    n_ch = per_w // SC_CHUNK
    assert per_w * nc * ns == n_rows and per_w % SC_CHUNK == 0
    mesh = plsc.VectorSubcoreMesh(core_axis_name="c", subcore_axis_name="s")

    @functools.partial(
        pl.kernel, mesh=mesh,
        out_type=jax.ShapeDtypeStruct((n_rows, PACKED), U32),
        scratch_types=[pltpu.VMEM((per_w,), jnp.int32),
                       pltpu.VMEM((SC_BUFS, SC_CHUNK, PACKED), U32),
                       pltpu.SemaphoreType.DMA((SC_BUFS,)),
                       pltpu.SemaphoreType.DMA((SC_BUFS,))])
    def gather(table_hbm, idx_hbm, out_hbm, idx_v, rows_v, gsem, wsem):
        wid = lax.axis_index("s") * nc + lax.axis_index("c")
        base = pl.multiple_of(wid * per_w, per_w)
        pltpu.sync_copy(idx_hbm.at[pl.ds(base, per_w)], idx_v)

        def read(c):
            b = c % SC_BUFS
            return [pltpu.make_async_copy(table_hbm.at[idx_v.at[pl.ds(c * SC_CHUNK, SC_CHUNK)]], rows_v.at[b], gsem.at[b])]

        def write(c):
            b = c % SC_BUFS
            return [pltpu.make_async_copy(rows_v.at[b], out_hbm.at[pl.ds(base + c * SC_CHUNK, SC_CHUNK)], wsem.at[b])]
        _sc_ring(n_ch, read, write)

    return gather(yb, dest.reshape(n_rows)).reshape(nt, TOP_K, tm, PACKED)


def _padfill_kernel(cnt_ref, pstart, xs_in, xs_ref, zeros, zsem):
    del xs_in

    def pad_copies(fn):
        for e in range(N_EXPERTS):
            cnt = cnt_ref[0, e]
            head = (-cnt) & (SUBLANES - 1)
            rest = ((-cnt) & (MOE_BLOCK - 1)) - head
            off = pstart[e] + cnt
            for k in range(SUBLANES - 1):
                @pl.when(k < head)
                def _(off=off, k=k):
                    fn(pltpu.make_async_copy(zeros.at[pl.ds(0, 1), :], xs_ref.at[pl.ds(off + k, 1), :], zsem))
            off = off + head
            for bit in PAD_BITS:
                @pl.when((rest & bit) != 0)
                def _(off=off, bit=bit):
                    fn(pltpu.make_async_copy(zeros.at[pl.ds(0, bit), :],
                                             xs_ref.at[pl.ds(pl.multiple_of(off, SUBLANES), bit), :], zsem))
                off = off + (rest & bit)

    zeros[...] = jnp.zeros_like(zeros)
    pad_copies(lambda cp: cp.start())
    pad_copies(lambda cp: cp.wait())


def _padfill(xs, pstart, cnt):
    smem = pl.BlockSpec(memory_space=pltpu.SMEM)
    hbm = pl.BlockSpec(memory_space=pl.ANY)
    return pl.pallas_call(
        _padfill_kernel,
        in_specs=[smem, smem, hbm],
        out_specs=hbm,
        out_shape=jax.ShapeDtypeStruct(xs.shape, xs.dtype),
        input_output_aliases={2: 0},
        scratch_shapes=[pltpu.VMEM((MOE_BLOCK // 2, PACKED), U32), pltpu.SemaphoreType.DMA(())],
        name="padfill",
    )(cnt, pstart, xs)


W_SLOTS = 2


def _expert_kernel(be_ref, nu_ref, ge_ref, x_ref, wg_hbm, wu_hbm, wd_hbm, o_ref,
                   wgf, wuf, wdf, wgb, wub, wdb, grp_ref, sems):
    j = pl.program_id(0)

    def weight_copies(e, slot):
        return (pltpu.make_async_copy(wg_hbm.at[e], wgf.at[slot], sems.at[slot, 0]),
                pltpu.make_async_copy(wu_hbm.at[e], wuf.at[slot], sems.at[slot, 1]),
                pltpu.make_async_copy(wd_hbm.at[e], wdf.at[slot], sems.at[slot, 2]))

    @pl.when(j == 0)
    def _():
        grp_ref[0] = 0
        for cp in weight_copies(ge_ref[0], 0):
            cp.start()

    @pl.when(j < nu_ref[0])
    def _():
        first = jnp.logical_or(j == 0, be_ref[j] != be_ref[jnp.maximum(j - 1, 0)])

        @pl.when(first)
        def _():
            grp = grp_ref[0]
            slot = grp % W_SLOTS
            for cp in weight_copies(be_ref[j], slot):
                cp.wait()
            wgb[...] = wgf[slot].astype(BF16)
            wub[...] = wuf[slot].astype(BF16)
            wdb[...] = wdf[slot].astype(BF16)
            nxt = ge_ref[grp + 1]

            @pl.when(nxt >= 0)
            def _():
                for cp in weight_copies(nxt, 1 - slot):
                    cp.start()
            grp_ref[0] = grp + 1

        lo, hi = _unpack_rows(x_ref[...])
        lo = lo.astype(BF16)
        hi = hi.astype(BF16)
        g = (jnp.dot(lo, wgb[:PACKED, :], preferred_element_type=F32)
             + jnp.dot(hi, wgb[PACKED:, :], preferred_element_type=F32))
        u = (jnp.dot(lo, wub[:PACKED, :], preferred_element_type=F32)
             + jnp.dot(hi, wub[PACKED:, :], preferred_element_type=F32))
        h = (g * _sigmoid(g) * u).astype(BF16)
        o_ref[...] = _pack_rows(jnp.dot(h, wdb[...], preferred_element_type=F32))


def _experts(xs, block_expert, n_used, group_expert, w_gate, w_up, w_down):
    cap = xs.shape[0]
    n_blocks = cap // MOE_BLOCK
    last = lambda j, be, nu, ge: jnp.minimum(j, nu[0] - 1)
    hbm = pl.BlockSpec(memory_space=pl.ANY)
    gs = pltpu.PrefetchScalarGridSpec(
        num_scalar_prefetch=3,
        grid=(n_blocks,),
        in_specs=[pl.BlockSpec((MOE_BLOCK, PACKED), lambda j, be, nu, ge: (last(j, be, nu, ge), 0)), hbm, hbm, hbm],
        out_specs=pl.BlockSpec((MOE_BLOCK, PACKED), lambda j, be, nu, ge: (last(j, be, nu, ge), 0)),
        scratch_shapes=[pltpu.VMEM((W_SLOTS, D_MODEL, D_EXPERT), F32),
                        pltpu.VMEM((W_SLOTS, D_MODEL, D_EXPERT), F32),
                        pltpu.VMEM((W_SLOTS, D_EXPERT, D_MODEL), F32),
                        pltpu.VMEM((D_MODEL, D_EXPERT), BF16),
                        pltpu.VMEM((D_MODEL, D_EXPERT), BF16),
                        pltpu.VMEM((D_EXPERT, D_MODEL), BF16),
                        pltpu.SMEM((1,), jnp.int32),
                        pltpu.SemaphoreType.DMA((W_SLOTS, 3))],
    )
    return pl.pallas_call(
        _expert_kernel,
        grid_spec=gs,
        out_shape=jax.ShapeDtypeStruct((cap, PACKED), U32),
        compiler_params=_cparams(1),
        name="experts",
    )(block_expert, n_used, group_expert, xs, w_gate, w_up, w_down)


CB_TM = RT_TM


def _combine_kernel(x1_ref, gt_ref, y2_ref, o_ref):
    g = gt_ref[...]
    lo1, hi1 = _unpack_rows(y2_ref[0, 0])
    lo2, hi2 = _unpack_rows(y2_ref[0, 1])
    o_ref[:, :PACKED] = x1_ref[:, :PACKED] + g[:, 0:1] * lo1 + g[:, 1:2] * lo2
    o_ref[:, PACKED:] = x1_ref[:, PACKED:] + g[:, 0:1] * hi1 + g[:, 1:2] * hi2


def _combine(x1, gates, y2):
    T = x1.shape[0]
    nt = T // CB_TM
    return pl.pallas_call(
        _combine_kernel,
        grid=(nt,),
        in_specs=[pl.BlockSpec((CB_TM, D_MODEL), lambda i: (i, 0)),
                  pl.BlockSpec((CB_TM, RINFO), lambda i: (i, 0)),
                  pl.BlockSpec((1, TOP_K, CB_TM, PACKED), lambda i: (i, 0, 0, 0))],
        out_specs=pl.BlockSpec((CB_TM, D_MODEL), lambda i: (i, 0)),
        out_shape=jax.ShapeDtypeStruct((T, D_MODEL), F32),
        compiler_params=_cparams(1),
        name="combine",
    )(x1, gates, y2)


def _layer(x, rel_bias, ln1, w_in, q_norm, k_norm, attn_sink, conv_w, conv_b, lru_wa, lru_ba, lru_wi, lru_bi,
           lru_lambda, out_norm_attn, out_norm_lru, w_out, ln2, w_group, b_group, w_er, b_er, w_gate, w_up, w_down):
    B, S, D = x.shape
    T = B * S
    x2 = x.reshape(T, D)
    q, kv, xr, gr = _in_proj(x2, ln1, w_in, q_norm, k_norm)
    bias_tab = _bias_table(rel_bias)
    attn_n = _attention(q.reshape(B, S, ATTN_WIDTH), kv.reshape(B, S, 2 * KV_WIDTH), bias_tab, attn_sink,
                        out_norm_attn)
    lru_n = _rglru(xr.reshape(B, S, LRU_WIDTH), gr.reshape(B, S, LRU_WIDTH), conv_w, conv_b,
                   lru_wa, lru_ba, lru_wi, lru_bi, lru_lambda, out_norm_lru)
    x1, h2, gates, ei, cnt = _out_route(attn_n.reshape(T, ATTN_WIDTH), lru_n.reshape(T, LRU_WIDTH), x2, w_out, ln2,
                                        w_group, b_group, w_er, b_er)
    cap = _moe_cap(T)
    dest, pstart, block_expert, n_used, group_expert = _layout(ei, cnt, cap // MOE_BLOCK)
    xs = _padfill(_sc_dispatch(h2, dest, cap), pstart, cnt)
    yb = _experts(xs, block_expert, n_used, group_expert, w_gate, w_up, w_down)
    out = _combine(x1, gates, _sc_gather(yb, dest))
    return out.reshape(B, S, D)


def kernel(x, rel_bias, ln1, w_in, q_norm, k_norm, attn_sink, conv_w, conv_b, lru_wa, lru_ba, lru_wi, lru_bi,
           lru_lambda, out_norm_attn, out_norm_lru, w_out, ln2, w_group, b_group, w_expert_router, b_expert_router,
           w_gate, w_up, w_down):
    depth = ln1.shape[0]
    for l in range(depth):
        x = _layer(x, rel_bias, ln1[l], w_in[l], q_norm[l], k_norm[l], attn_sink[l], conv_w[l], conv_b[l],
                   lru_wa[l], lru_ba[l], lru_wi[l], lru_bi[l], lru_lambda[l], out_norm_attn[l], out_norm_lru[l],
                   w_out[l], ln2[l], w_group[l], b_group[l], w_expert_router[l], b_expert_router[l],
                   w_gate[l], w_up[l], w_down[l])
    return x
```

```python
import functools
import math

import jax
import jax.numpy as jnp
import numpy as np
from jax import lax
from jax.experimental import pallas as pl
from jax.experimental.pallas import tpu as pltpu
from jax.experimental.pallas import tpu_sc as plsc

D_MODEL = 1024
N_HEADS = 8
N_KV_HEADS = 2
HEAD_DIM = 64
Q_PER_KV = N_HEADS // N_KV_HEADS
ATTN_WIDTH = N_HEADS * HEAD_DIM
KV_WIDTH = N_KV_HEADS * HEAD_DIM
WINDOW = 128
BLOCK = 128
NUM_BUCKETS = 32
MAX_DISTANCE = 128
LRU_WIDTH = D_MODEL - ATTN_WIDTH
LRU_BLOCKS = 8
LRU_BLOCK_DIM = LRU_WIDTH // LRU_BLOCKS
LRU_C = 8.0
CONV_W = 4
CONV_LEFT = 2
N_GROUPS = 4
EXPERTS_PER_GROUP = 8
N_EXPERTS = N_GROUPS * EXPERTS_PER_GROUP
TOP_K = 2
D_EXPERT = 512
MOE_BLOCK = 256
EPS = 1e-6
NEG_INF = -1e30

LANES = 128
SUBLANES = 8
VMEM_LIMIT = 56 * 1024 * 1024

F32 = jnp.float32
BF16 = jnp.bfloat16


def _cparams(n_axes, vmem=VMEM_LIMIT):
    return pltpu.CompilerParams(dimension_semantics=("arbitrary",) * n_axes, vmem_limit_bytes=vmem)


def _rms(x, gain):
    return x * lax.rsqrt(jnp.mean(x * x, axis=-1, keepdims=True) + EPS) * gain


U32 = jnp.uint32
HI_MASK = 0xFFFF0000
PACKED = D_MODEL // 2


def _pack_rows(x):
    h = x.shape[1] // 2
    lo = lax.bitcast_convert_type(x[:, :h].astype(BF16).astype(F32), U32) >> 16
    hi = lax.bitcast_convert_type(x[:, h:].astype(BF16).astype(F32), U32) & jnp.uint32(HI_MASK)
    return lo | hi


def _unpack_rows(p):
    lo = lax.bitcast_convert_type(p << 16, F32)
    hi = lax.bitcast_convert_type(p & jnp.uint32(HI_MASK), F32)
    return lo, hi


IN_TM = 512


def _head_rms(x, n_heads, gain):
    head = lax.broadcasted_iota(jnp.int32, (1, n_heads * HEAD_DIM), 1) // HEAD_DIM
    x2 = x * x
    scale = jnp.zeros_like(x)
    for h in range(n_heads):
        ms = jnp.sum(jnp.where(head == h, x2, 0.0), axis=-1, keepdims=True) * (1.0 / HEAD_DIM)
        scale = jnp.where(head == h, lax.rsqrt(ms + EPS), scale)
    return x * scale * gain


def _in_proj_kernel(x_ref, g_ref, w_ref, qg_ref, kg_ref, q_ref, kv_ref, xr_ref, gr_ref, wb_ref):
    @pl.when(pl.program_id(0) == 0)
    def _():
        wb_ref[...] = w_ref[...].astype(BF16)

    h = _rms(x_ref[...], g_ref[...]).astype(BF16)
    c_k = ATTN_WIDTH
    c_v = c_k + KV_WIDTH
    c_x = c_v + KV_WIDTH
    c_g = c_x + LRU_WIDTH
    q = jnp.dot(h, wb_ref[:, :c_k], preferred_element_type=F32)
    q_ref[...] = _head_rms(q, N_HEADS, qg_ref[...]).astype(BF16)
    k = jnp.dot(h, wb_ref[:, c_k:c_v], preferred_element_type=F32)
    kv_ref[:, :KV_WIDTH] = _head_rms(k, N_KV_HEADS, kg_ref[...]).astype(BF16)
    kv_ref[:, KV_WIDTH:] = jnp.dot(h, wb_ref[:, c_v:c_x], preferred_element_type=F32).astype(BF16)
    xr_ref[...] = jnp.dot(h, wb_ref[:, c_x:c_g], preferred_element_type=F32)
    gr_ref[...] = jnp.dot(h, wb_ref[:, c_g:], preferred_element_type=F32)


def _in_proj(x2, ln1, w_in, q_gain, k_gain):
    T = x2.shape[0]
    n_in = w_in.shape[1]
    row = lambda w: pl.BlockSpec((IN_TM, w), lambda i: (i, 0))
    qg = (jnp.tile(q_gain.astype(F32), N_HEADS) * (HEAD_DIM ** -0.5)).reshape(1, ATTN_WIDTH)
    kg = jnp.tile(k_gain.astype(F32), N_KV_HEADS).reshape(1, KV_WIDTH)
    return pl.pallas_call(
        _in_proj_kernel,
        grid=(T // IN_TM,),
        in_specs=[row(D_MODEL),
                  pl.BlockSpec((1, D_MODEL), lambda i: (0, 0)),
                  pl.BlockSpec((D_MODEL, n_in), lambda i: (0, 0)),
                  pl.BlockSpec((1, ATTN_WIDTH), lambda i: (0, 0)),
                  pl.BlockSpec((1, KV_WIDTH), lambda i: (0, 0))],
        out_specs=[row(ATTN_WIDTH), row(2 * KV_WIDTH), row(LRU_WIDTH), row(LRU_WIDTH)],
        out_shape=[jax.ShapeDtypeStruct((T, ATTN_WIDTH), BF16),
                   jax.ShapeDtypeStruct((T, 2 * KV_WIDTH), BF16),
                   jax.ShapeDtypeStruct((T, LRU_WIDTH), F32),
                   jax.ShapeDtypeStruct((T, LRU_WIDTH), F32)],
        scratch_shapes=[pltpu.VMEM((D_MODEL, n_in), BF16)],
        compiler_params=_cparams(1),
        name="in_proj",
    )(x2, ln1.reshape(1, D_MODEL), w_in, qg, kg)


def _t5_bucket(rel):
    half = NUM_BUCKETS // 2
    max_exact = half // 2
    base = jnp.where(rel > 0, half, 0)
    n = jnp.abs(rel)
    nf = jnp.maximum(n, 1).astype(jnp.float32)
    large = max_exact + (jnp.log(nf / max_exact) / math.log(MAX_DISTANCE / max_exact)
                         * (half - max_exact)).astype(jnp.int32)
    large = jnp.minimum(large, half - 1)
    return base + jnp.where(n < max_exact, n, large)


HEAD_PAIRS = Q_PER_KV // 2
EDGE_VARIANTS = 3


def _bias_kernel(rb_ref, bucket_ref, band_ref, o_ref):
    bucket = bucket_ref[...]
    band = band_ref[...] > 0
    col = lax.broadcasted_iota(jnp.int32, bucket.shape, 1)
    valid = (band & (col >= BLOCK), band, band & (col < 2 * BLOCK))
    for h in range(N_HEADS):
        acc = jnp.zeros(bucket.shape, F32)
        for b in range(NUM_BUCKETS):
            acc = jnp.where(bucket == b, rb_ref[b, h], acc)
        kv, g = divmod(h, Q_PER_KV)
        pair, parity = divmod(g, 2)
        for var in range(EDGE_VARIANTS):
            o_ref[var, kv, parity, pair * BLOCK:(pair + 1) * BLOCK, :] = jnp.where(valid[var], acc, NEG_INF)


def _bias_table(rel_bias):
    qi = jnp.arange(BLOCK, dtype=jnp.int32)
    kj = jnp.arange(3 * BLOCK, dtype=jnp.int32)
    rel = kj[None, :] - BLOCK - qi[:, None]
    bucket = _t5_bucket(rel).astype(jnp.int32)
    band = (jnp.abs(rel) <= WINDOW).astype(jnp.int32)
    return pl.pallas_call(
        _bias_kernel,
        in_specs=[pl.BlockSpec(memory_space=pltpu.SMEM),
                  pl.BlockSpec(memory_space=pltpu.VMEM),
                  pl.BlockSpec(memory_space=pltpu.VMEM)],
        out_specs=pl.BlockSpec(memory_space=pltpu.VMEM),
        out_shape=jax.ShapeDtypeStruct((EDGE_VARIANTS, N_KV_HEADS, 2, HEAD_PAIRS * BLOCK, 3 * BLOCK), F32),
        name="bias_table",
    )(rel_bias.astype(F32), bucket, band)


def _attn_kernel(sink_ref, q_ref, kp_ref, kc_ref, kn_ref, bias_ref, og_ref, o_ref):
    n = pl.program_id(1)
    kv_all = jnp.concatenate([kp_ref[0], kc_ref[0], kn_ref[0]], axis=0)
    for qb in range(ATTN_QB):
        variant = 1
        if qb == 0:
            variant = jnp.where(n == 0, 0, 1)
        if qb == ATTN_QB - 1:
            variant = jnp.where(n == pl.num_programs(1) - 1, 2, variant)
        out = _attn_block(q_ref[0, qb * BLOCK:(qb + 1) * BLOCK, :], kv_all[qb * BLOCK:(qb + 3) * BLOCK, :],
                          lambda kv, parity: bias_ref[variant, kv, parity], sink_ref)
        o_ref[0, qb * BLOCK:(qb + 1) * BLOCK, :] = _rms(out, og_ref[...]).astype(o_ref.dtype)


def _attn_block(q, kvw, bias, sink_ref):
    low = lax.broadcasted_iota(jnp.int32, (1, LANES), 1) < HEAD_DIM
    rowi = lax.broadcasted_iota(jnp.int32, (HEAD_PAIRS * BLOCK, 1), 0)
    swap = lambda slab: pltpu.roll(slab.astype(F32), HEAD_DIM, 1).astype(BF16)
    kslab, vslab = kvw[:, :KV_WIDTH], kvw[:, KV_WIDTH:]
    kslab_sw, vslab_sw = swap(kslab), swap(vslab)
    outs = []
    for kv in range(N_KV_HEADS):
        src = ((kslab, vslab), (kslab_sw, vslab_sw)) if kv == 0 else ((kslab_sw, vslab_sw), (kslab, vslab))
        base = kv * Q_PER_KV * HEAD_DIM
        qpair = jnp.concatenate([q[:, base + j * LANES:base + (j + 1) * LANES] for j in range(HEAD_PAIRS)], axis=0)
        acc = None
        for parity in range(2):
            keep = low if parity == 0 else jnp.logical_not(low)
            kz = jnp.where(keep, src[parity][0], jnp.zeros_like(kslab))
            vz = jnp.where(keep, src[parity][1], jnp.zeros_like(vslab))
            sink = jnp.zeros((HEAD_PAIRS * BLOCK, 1), F32)
            for j in range(HEAD_PAIRS):
                sink = jnp.where(rowi // BLOCK == j, sink_ref[kv * Q_PER_KV + 2 * j + parity], sink)
            s = lax.dot_general(qpair, kz, (((1,), (1,)), ((), ())), preferred_element_type=F32)
            s = s + bias(kv, parity)
            m = jnp.maximum(jnp.max(s, axis=-1, keepdims=True), sink)
            p = jnp.exp(s - m)
            denom = jnp.sum(p, axis=-1, keepdims=True) + jnp.exp(sink - m)
            o = jnp.dot(p.astype(BF16), vz, preferred_element_type=F32) * (1.0 / denom)
            acc = o if acc is None else acc + o
        outs += [acc[j * BLOCK:(j + 1) * BLOCK, :] for j in range(HEAD_PAIRS)]
    return jnp.concatenate(outs, axis=1)


ATTN_QB = 4


def _attention(q, kv, bias_tab, sink, out_gain):
    B, S, _ = q.shape
    nb = S // BLOCK
    assert ATTN_QB >= 2 and nb % ATTN_QB == 0, "a step's first and last query blocks must be distinct"
    ns = nb // ATTN_QB
    rows = ATTN_QB * BLOCK
    kvspec = lambda f: pl.BlockSpec((1, BLOCK, 2 * KV_WIDTH), f)
    return pl.pallas_call(
        _attn_kernel,
        grid=(B, ns),
        in_specs=[pl.BlockSpec(memory_space=pltpu.SMEM),
                  pl.BlockSpec((1, rows, ATTN_WIDTH), lambda b, n: (b, n, 0)),
                  kvspec(lambda b, n: (b, jnp.maximum(n * ATTN_QB - 1, 0), 0)),
                  pl.BlockSpec((1, rows, 2 * KV_WIDTH), lambda b, n: (b, n, 0)),
                  kvspec(lambda b, n: (b, jnp.minimum((n + 1) * ATTN_QB, nb - 1), 0)),
                  pl.BlockSpec((EDGE_VARIANTS, N_KV_HEADS, 2, HEAD_PAIRS * BLOCK, 3 * BLOCK),
                               lambda b, n: (0, 0, 0, 0, 0)),
                  pl.BlockSpec((1, ATTN_WIDTH), lambda b, n: (0, 0))],
        out_specs=pl.BlockSpec((1, rows, ATTN_WIDTH), lambda b, n: (b, n, 0)),
        out_shape=jax.ShapeDtypeStruct((B, S, ATTN_WIDTH), BF16),
        compiler_params=_cparams(2),
        name="attention",
    )(sink.astype(F32), q, kv, kv, kv, bias_tab, out_gain.reshape(1, ATTN_WIDTH))


LRU_TC = 128
LRU_PITCH = LRU_TC + SUBLANES
LRU_SLABS = LRU_WIDTH // LANES
HALO = SUBLANES


def _softplus(x):
    return jnp.maximum(x, 0.0) + jnp.log(1.0 + jnp.exp(-jnp.abs(x)))


def _gelu_tanh(x):
    k = math.sqrt(2.0 / math.pi)
    hx = 0.5 * x
    return hx + hx * jnp.tanh(x * (k + (k * 0.044715) * (x * x)))


def _sigmoid(x):
    return 0.5 + 0.5 * jnp.tanh(0.5 * x)


def _rglru_kernel(xr_ref, xp_ref, xn_ref, gr_ref, cw_ref, cb_ref, wg_ref, bg_ref, lam_ref, og_ref,
                  o_ref, sx_ref, a_ref, u_ref, h_ref, carry_ref, hf_ref):
    p = pl.program_id(0)
    i = pl.program_id(1)
    nc = pl.num_programs(1)
    c = i + p * (nc - 1 - 2 * i)
    B = xr_ref.shape[0]
    TC = LRU_TC

    sx_ref[:, HALO:HALO + TC, :] = xr_ref[...]
    sx_ref[:, 0:HALO, :] = jnp.where(c > 0, xp_ref[...], 0.0)
    sx_ref[:, HALO + TC:, :] = jnp.where(c < nc - 1, xn_ref[...], 0.0)
    xc = cb_ref[...][None]
    for j in range(CONV_W):
        off = HALO + j - CONV_LEFT
        xc = xc + cw_ref[j:j + 1, :][None] * sx_ref[:, off:off + TC, :]
    xc2 = xc.reshape(B * TC, LRU_WIDTH)

    g = jnp.dot(xc2.astype(BF16), wg_ref[0], preferred_element_type=F32) + bg_ref[0]
    r = _sigmoid(g[:, :LRU_WIDTH])
    ig = _sigmoid(g[:, LRU_WIDTH:])
    a = jnp.exp(r * (-LRU_C * _softplus(-lam_ref[0])))
    z = 1.0 - a * a
    u = z * lax.rsqrt(jnp.maximum(z, 1e-30)) * ig * xc2
    for b in range(B):
        for s in range(LRU_SLABS):
            a_ref[s, b * LRU_PITCH:b * LRU_PITCH + TC, :] = a[b * TC:(b + 1) * TC, s * LANES:(s + 1) * LANES]
            u_ref[s, b * LRU_PITCH:b * LRU_PITCH + TC, :] = u[b * TC:(b + 1) * TC, s * LANES:(s + 1) * LANES]

    @pl.when(i == 0)
    def _():
        carry_ref[...] = jnp.zeros_like(carry_ref)

    def step(k, hs):
        t = k + p * (TC - 1 - 2 * k)
        out = []
        for s in range(LRU_SLABS):
            idx = pl.ds(t, B, stride=LRU_PITCH)
            hn = a_ref[s, idx, :] * hs[s] + u_ref[s, idx, :]
            h_ref[s, idx, :] = hn
            out.append(hn)
        return tuple(out)

    hs = lax.fori_loop(0, TC, step, tuple(carry_ref[s] for s in range(LRU_SLABS)), unroll=8)
    for s in range(LRU_SLABS):
        carry_ref[s] = hs[s]

    @pl.when(p == 0)
    def _():
        for b in range(B):
            for s in range(LRU_SLABS):
                hf_ref[c, s, b * TC:(b + 1) * TC, :] = h_ref[s, b * LRU_PITCH:b * LRU_PITCH + TC, :].astype(hf_ref.dtype)

    @pl.when(p == 1)
    def _():
        for b in range(B):
            hsum = jnp.concatenate(
                [h_ref[s, b * LRU_PITCH:b * LRU_PITCH + TC, :] + hf_ref[c, s, b * TC:(b + 1) * TC, :].astype(F32)
                 for s in range(LRU_SLABS)], axis=1)
            y = hsum * _gelu_tanh(gr_ref[b])
            o_ref[b] = _rms(y, og_ref[...]).astype(o_ref.dtype)


def _block_diag(w):
    eye = jnp.eye(LRU_BLOCKS, dtype=w.dtype)
    return jnp.einsum('hij,hg->higj', w, eye).reshape(LRU_WIDTH, LRU_WIDTH)


def _rglru(xr, gr, conv_w, conv_b, w_a, b_a, w_i, b_i, lam, out_gain):
    B, S, W = xr.shape
    nc = S // LRU_TC
    hb = LRU_TC // HALO
    wg = jnp.stack([jnp.concatenate([_block_diag(w_a[d]), _block_diag(w_i[d])], axis=1) for d in range(2)]).astype(BF16)
    bg = jnp.concatenate([b_a, b_i], axis=-1).reshape(2, 1, 2 * W).astype(F32)
    chunk = lambda p, i: i + p * (nc - 1 - 2 * i)
    full2 = lambda shape: pl.BlockSpec(shape, lambda p, i: (0,) * len(shape))
    return pl.pallas_call(
        _rglru_kernel,
        grid=(2, nc),
        in_specs=[pl.BlockSpec((B, LRU_TC, W), lambda p, i: (0, chunk(p, i), 0)),
                  pl.BlockSpec((B, HALO, W), lambda p, i: (0, jnp.maximum(chunk(p, i) * hb - 1, 0), 0)),
                  pl.BlockSpec((B, HALO, W), lambda p, i: (0, jnp.minimum((chunk(p, i) + 1) * hb, S // HALO - 1), 0)),
                  pl.BlockSpec((B, LRU_TC, W), lambda p, i: (0, chunk(p, i), 0)),
                  full2((CONV_W, W)),
                  full2((1, W)),
                  pl.BlockSpec((1, W, 2 * W), lambda p, i: (p, 0, 0)),
                  pl.BlockSpec((1, 1, 2 * W), lambda p, i: (p, 0, 0)),
                  pl.BlockSpec((1, 1, W), lambda p, i: (p, 0, 0)),
                  full2((1, W))],
        out_specs=pl.BlockSpec((B, LRU_TC, W), lambda p, i: (0, nc - 1 - p * i, 0)),
        out_shape=jax.ShapeDtypeStruct((B, S, W), BF16),
        scratch_shapes=[pltpu.VMEM((B, LRU_TC + 2 * HALO, W), F32),
                        pltpu.VMEM((LRU_SLABS, B * LRU_PITCH, LANES), F32),
                        pltpu.VMEM((LRU_SLABS, B * LRU_PITCH, LANES), F32),
                        pltpu.VMEM((LRU_SLABS, B * LRU_PITCH, LANES), F32),
                        pltpu.VMEM((LRU_SLABS, B, LANES), F32),
                        pltpu.VMEM((nc, LRU_SLABS, B * LRU_TC, LANES), BF16)],
        compiler_params=_cparams(2),
        name="rglru",
    )(xr, xr, xr, gr, conv_w.astype(F32), conv_b.reshape(1, W).astype(F32), wg, bg,
      lam.reshape(2, 1, W).astype(F32), out_gain.reshape(1, W).astype(F32))


RT_TM = 512
RT_COLS = LANES
RINFO = 8


def _split_bf16(x):
    hi = x.astype(BF16)
    lo = (x - hi.astype(F32)).astype(BF16)
    return hi, lo


def _route_kernel(an_ref, ln_ref, x_ref, wo_ref, g2_ref, wr_ref, br_ref,
                  x1_ref, h2_ref, gt_ref, ei_ref, cnt_ref, wob_ref, wrb_ref, tri_ref, run_ref):
    @pl.when(pl.program_id(0) == 0)
    def _():
        wob_ref[...] = wo_ref[...].astype(BF16)
        hi, lo = _split_bf16(wr_ref[...])
        wrb_ref[:, :RT_COLS] = hi
        wrb_ref[:, RT_COLS:] = lo
        r = lax.broadcasted_iota(jnp.int32, (RT_TM, RT_TM), 0)
        cidx = lax.broadcasted_iota(jnp.int32, (RT_TM, RT_TM), 1)
        tri_ref[...] = (cidx < r).astype(BF16)
        run_ref[...] = jnp.zeros_like(run_ref)

    x1 = (x_ref[...]
          + jnp.dot(an_ref[...], wob_ref[:ATTN_WIDTH, :], preferred_element_type=F32)
          + jnp.dot(ln_ref[...], wob_ref[ATTN_WIDTH:, :], preferred_element_type=F32))
    x1_ref[...] = x1
    h2 = _rms(x1, g2_ref[...])
    h2_ref[...] = _pack_rows(h2)

    hi, lo = _split_bf16(h2)
    t1 = jnp.dot(hi, wrb_ref[...], preferred_element_type=F32)
    t2 = jnp.dot(lo, wrb_ref[:, :RT_COLS], preferred_element_type=F32)
    logit = t1[:, :RT_COLS] + t1[:, RT_COLS:] + t2 + br_ref[...]

    lane = lax.broadcasted_iota(jnp.int32, logit.shape, 1)
    big = jnp.int32(4 * RT_COLS)
    is_g = lane < N_GROUPS
    gl = jnp.where(is_g, logit, -jnp.inf)
    gm = jnp.max(gl, axis=-1, keepdims=True)
    gidx = jnp.min(jnp.where(gl == gm, lane, big), axis=-1, keepdims=True)
    g_p = 1.0 / jnp.sum(jnp.where(is_g, jnp.exp(logit - gm), 0.0), axis=-1, keepdims=True)
    lo_lane = N_GROUPS + gidx * EXPERTS_PER_GROUP
    el = jnp.where((lane >= lo_lane) & (lane < lo_lane + EXPERTS_PER_GROUP), logit, -jnp.inf)
    m1 = jnp.max(el, axis=-1, keepdims=True)
    i1 = jnp.min(jnp.where(el == m1, lane, big), axis=-1, keepdims=True)
    el2 = jnp.where(lane == i1, -jnp.inf, el)
    m2 = jnp.max(el2, axis=-1, keepdims=True)
    i2 = jnp.min(jnp.where(el2 == m2, lane, big), axis=-1, keepdims=True)
    t = jnp.exp(m2 - m1)
    gate1 = g_p / (1.0 + t)
    gate2 = g_p * t / (1.0 + t)
    e1 = i1 - N_GROUPS
    e2 = i2 - N_GROUPS

    oh1 = lane == e1
    oh2 = lane == e2
    oh = (oh1 | oh2).astype(F32)
    cum = jnp.dot(tri_ref[...], oh.astype(BF16), preferred_element_type=F32) + run_ref[...]
    rank1 = jnp.sum(jnp.where(oh1, cum, 0.0), axis=-1, keepdims=True)
    rank2 = jnp.sum(jnp.where(oh2, cum, 0.0), axis=-1, keepdims=True)
    run_ref[...] = run_ref[...] + jnp.sum(oh, axis=0, keepdims=True)
    cnt_ref[...] = run_ref[...].astype(jnp.int32)

    gt_ref[...] = jnp.where(lax.broadcasted_iota(jnp.int32, (RT_TM, RINFO), 1) == 0, gate1, gate2)
    vals = [e1.astype(F32), e2.astype(F32), rank1, rank2]
    ri = jnp.zeros(logit.shape, F32)
    for k, v in enumerate(vals):
        ri = jnp.where(lane == k, v, ri)
    ei_ref[0] = ri.T[:RINFO, :].astype(jnp.int32)


def _out_route(attn_n, lru_n, x2, w_out, ln2, w_group, b_group, w_er, b_er):
    T = x2.shape[0]
    wr = jnp.concatenate([w_group, jnp.transpose(w_er, (1, 0, 2)).reshape(D_MODEL, N_EXPERTS)], axis=1)
    wr = jnp.pad(wr, ((0, 0), (0, RT_COLS - wr.shape[1]))).astype(F32)
    br = jnp.pad(jnp.concatenate([b_group, b_er.reshape(-1)]), (0, RT_COLS - N_GROUPS - N_EXPERTS)).reshape(1, RT_COLS)
    row = lambda w: pl.BlockSpec((RT_TM, w), lambda i: (i, 0))
    const = lambda shape: pl.BlockSpec(shape, lambda i: (0, 0))
    return pl.pallas_call(
        _route_kernel,
        grid=(T // RT_TM,),
        in_specs=[row(ATTN_WIDTH), row(LRU_WIDTH), row(D_MODEL), const((D_MODEL, D_MODEL)), const((1, D_MODEL)),
                  const((D_MODEL, RT_COLS)), const((1, RT_COLS))],
        out_specs=[row(D_MODEL), row(PACKED), row(RINFO),
                   pl.BlockSpec((1, RINFO, RT_TM), lambda i: (i, 0, 0)), const((1, RT_COLS))],
        out_shape=[jax.ShapeDtypeStruct((T, D_MODEL), F32),
                   jax.ShapeDtypeStruct((T, PACKED), U32),
                   jax.ShapeDtypeStruct((T, RINFO), F32),
                   jax.ShapeDtypeStruct((T // RT_TM, RINFO, RT_TM), jnp.int32),
                   jax.ShapeDtypeStruct((1, RT_COLS), jnp.int32)],
        scratch_shapes=[pltpu.VMEM((D_MODEL, D_MODEL), BF16),
                        pltpu.VMEM((D_MODEL, 2 * RT_COLS), BF16),
                        pltpu.VMEM((RT_TM, RT_TM), BF16),
                        pltpu.VMEM((1, RT_COLS), F32)],
        compiler_params=_cparams(1),
        name="out_route",
    )(attn_n, lru_n, x2, w_out, ln2.reshape(1, D_MODEL).astype(F32), wr, br.astype(F32))


def _moe_cap(T):
    A = T * TOP_K
    return ((A + MOE_BLOCK - 1) // MOE_BLOCK) * MOE_BLOCK + N_EXPERTS * MOE_BLOCK


PAD_BITS = tuple(1 << b for b in reversed(range(3, MOE_BLOCK.bit_length() - 1)))


def _layout_kernel(cnt_ref, ei_ref, dest_ref, pstart, be_ref, nu_ref, ge_ref):
    n_blocks = be_ref.shape[0]

    def lay(e, carry):
        start, blk, grp = carry
        pstart[e] = start
        nb = (cnt_ref[0, e] + MOE_BLOCK - 1) // MOE_BLOCK
        ge_ref[grp] = e

        def fill(k, c):
            be_ref[blk + k] = e
            return c
        lax.fori_loop(0, nb, fill, 0)
        return start + nb * MOE_BLOCK, blk + nb, grp + (nb > 0).astype(jnp.int32)
    _, used, groups = lax.fori_loop(0, N_EXPERTS, lay, (jnp.int32(0), jnp.int32(0), jnp.int32(0)))
    nu_ref[0] = used

    def tail(k, c):
        be_ref[k] = N_EXPERTS - 1
        return c
    lax.fori_loop(used, n_blocks, tail, 0)

    def no_group(k, c):
        ge_ref[k] = -1
        return c
    lax.fori_loop(groups, N_EXPERTS + 1, no_group, 0)

    expert = ei_ref[:, 0:TOP_K, :]
    dest = ei_ref[:, TOP_K:2 * TOP_K, :]
    for e in range(N_EXPERTS):
        dest = dest + jnp.where(expert == e, pstart[e], 0)
    dest_ref[...] = dest


def _layout(ei, cnt, n_blocks):
    nt = ei.shape[0]
    smem = pl.BlockSpec(memory_space=pltpu.SMEM)
    vmem = pl.BlockSpec(memory_space=pltpu.VMEM)
    return pl.pallas_call(
        _layout_kernel,
        in_specs=[smem, vmem],
        out_specs=[vmem, smem, smem, smem, smem],
        out_shape=[jax.ShapeDtypeStruct((nt, TOP_K, RT_TM), jnp.int32),
                   jax.ShapeDtypeStruct((N_EXPERTS,), jnp.int32),
                   jax.ShapeDtypeStruct((n_blocks,), jnp.int32),
                   jax.ShapeDtypeStruct((1,), jnp.int32),
                   jax.ShapeDtypeStruct((N_EXPERTS + 1,), jnp.int32)],
        name="layout",
    )(cnt, ei)


SC_CHUNK = 64
SC_BUFS = 3
SC_LEAD = SC_BUFS - 1


def _sc_workers():
    info = plsc.get_sparse_core_info()
    return info.num_cores, info.num_subcores


def _sc_ring(n_chunks, read, write):
    for c in range(min(SC_LEAD, n_chunks)):
        for cp in read(c):
            cp.start()
    reclaimed = set()
    for c in range(n_chunks):
        for cp in read(c):
            cp.wait()
        for cp in write(c):
            cp.start()
        nxt = c + SC_LEAD
        if nxt < n_chunks:
            if nxt - SC_BUFS >= 0:
                for cp in write(nxt - SC_BUFS):
                    cp.wait()
                reclaimed.add(nxt - SC_BUFS)
            for cp in read(nxt):
                cp.start()
    for c in range(n_chunks):
        if c not in reclaimed:
            for cp in write(c):
                cp.wait()


def _sc_dispatch(h2p, dest, cap):
    T = h2p.shape[0]
    nc, ns = _sc_workers()
    per_w = T // (nc * ns)
    n_ch = per_w // SC_CHUNK
    assert dest.shape == (nc * ns, TOP_K, per_w) and per_w % SC_CHUNK == 0
    idx = dest.reshape(nc * ns, TOP_K * n_ch, SC_CHUNK)
    mesh = plsc.VectorSubcoreMesh(core_axis_name="c", subcore_axis_name="s")

    @functools.partial(
        pl.kernel, mesh=mesh,
        out_type=jax.ShapeDtypeStruct((cap, PACKED), U32),
        scratch_types=[pltpu.VMEM((TOP_K * n_ch, SC_CHUNK), jnp.int32),
                       pltpu.VMEM((SC_BUFS, SC_CHUNK, PACKED), U32),
                       pltpu.SemaphoreType.DMA((SC_BUFS,)),
                       pltpu.SemaphoreType.DMA((SC_BUFS,))])
    def scatter(src_hbm, idx_hbm, out_hbm, idx_v, rows_v, rsem, wsem):
        wid = lax.axis_index("s") * nc + lax.axis_index("c")
        base = pl.multiple_of(wid * per_w, per_w)
        pltpu.sync_copy(idx_hbm.at[wid], idx_v)

        def read(c):
            b = c % SC_BUFS
            return [pltpu.make_async_copy(src_hbm.at[pl.ds(base + c * SC_CHUNK, SC_CHUNK)], rows_v.at[b], rsem.at[b])]

        def write(c):
            b = c % SC_BUFS
            return [pltpu.make_async_copy(rows_v.at[b], out_hbm.at[idx_v.at[k * n_ch + c]], wsem.at[b])
                    for k in range(TOP_K)]
        _sc_ring(n_ch, read, write)

    return scatter(h2p, idx)


def _sc_gather(yb, dest):
    nt, _, tm = dest.shape
    nc, ns = _sc_workers()
    n_rows = nt * TOP_K * tm
    per_w = n_rows // (nc * ns)
    n_ch = per_w // SC_CHUNK
    assert per_w * nc * ns == n_rows and per_w % SC_CHUNK == 0
    mesh = plsc.VectorSubcoreMesh(core_axis_name="c", subcore_axis_name="s")

    @functools.partial(
        pl.kernel, mesh=mesh,
        out_type=jax.ShapeDtypeStruct((n_rows, PACKED), U32),
        scratch_types=[pltpu.VMEM((per_w,), jnp.int32),
                       pltpu.VMEM((SC_BUFS, SC_CHUNK, PACKED), U32),
                       pltpu.SemaphoreType.DMA((SC_BUFS,)),
                       pltpu.SemaphoreType.DMA((SC_BUFS,))])
    def gather(table_hbm, idx_hbm, out_hbm, idx_v, rows_v, gsem, wsem):
        wid = lax.axis_index("s") * nc + lax.axis_index("c")
        base = pl.multiple_of(wid * per_w, per_w)
        pltpu.sync_copy(idx_hbm.at[pl.ds(base, per_w)], idx_v)

        def read(c):
            b = c % SC_BUFS
            return [pltpu.make_async_copy(table_hbm.at[idx_v.at[pl.ds(c * SC_CHUNK, SC_CHUNK)]], rows_v.at[b], gsem.at[b])]

        def write(c):
            b = c % SC_BUFS
            return [pltpu.make_async_copy(rows_v.at[b], out_hbm.at[pl.ds(base + c * SC_CHUNK, SC_CHUNK)], wsem.at[b])]
        _sc_ring(n_ch, read, write)

    return gather(yb, dest.reshape(n_rows)).reshape(nt, TOP_K, tm, PACKED)


def _padfill_kernel(cnt_ref, pstart, xs_in, xs_ref, zeros, zsem):
    del xs_in

    def pad_copies(fn):
        for e in range(N_EXPERTS):
            cnt = cnt_ref[0, e]
            head = (-cnt) & (SUBLANES - 1)
            rest = ((-cnt) & (MOE_BLOCK - 1)) - head
            off = pstart[e] + cnt
            for k in range(SUBLANES - 1):
                @pl.when(k < head)
                def _(off=off, k=k):
                    fn(pltpu.make_async_copy(zeros.at[pl.ds(0, 1), :], xs_ref.at[pl.ds(off + k, 1), :], zsem))
            off = off + head
            for bit in PAD_BITS:
                @pl.when((rest & bit) != 0)
                def _(off=off, bit=bit):
                    fn(pltpu.make_async_copy(zeros.at[pl.ds(0, bit), :],
                                             xs_ref.at[pl.ds(pl.multiple_of(off, SUBLANES), bit), :], zsem))
                off = off + (rest & bit)

    zeros[...] = jnp.zeros_like(zeros)
    pad_copies(lambda cp: cp.start())
    pad_copies(lambda cp: cp.wait())


def _padfill(xs, pstart, cnt):
    smem = pl.BlockSpec(memory_space=pltpu.SMEM)
    hbm = pl.BlockSpec(memory_space=pl.ANY)
    return pl.pallas_call(
        _padfill_kernel,
        in_specs=[smem, smem, hbm],
        out_specs=hbm,
        out_shape=jax.ShapeDtypeStruct(xs.shape, xs.dtype),
        input_output_aliases={2: 0},
        scratch_shapes=[pltpu.VMEM((MOE_BLOCK // 2, PACKED), U32), pltpu.SemaphoreType.DMA(())],
        name="padfill",
    )(cnt, pstart, xs)


W_SLOTS = 2
EXPERT_GROUP = 8


def _expert_kernel(be_ref, nu_ref, ge_ref, x_ref, wg_hbm, wu_hbm, wd_hbm, o_ref,
                   wgf, wuf, wdf, wgb, wub, wdb, grp_ref, sems):
    step = pl.program_id(0)

    def weight_copies(e, slot):
        return (pltpu.make_async_copy(wg_hbm.at[e], wgf.at[slot], sems.at[slot, 0]),
                pltpu.make_async_copy(wu_hbm.at[e], wuf.at[slot], sems.at[slot, 1]),
                pltpu.make_async_copy(wd_hbm.at[e], wdf.at[slot], sems.at[slot, 2]))

    @pl.when(step == 0)
    def _():
        grp_ref[0] = 0
        for cp in weight_copies(ge_ref[0], 0):
            cp.start()

    def block(s, carry):
        j = step * EXPERT_GROUP + s

        @pl.when(j < nu_ref[0])
        def _():
            first = jnp.logical_or(j == 0, be_ref[j] != be_ref[jnp.maximum(j - 1, 0)])

            @pl.when(first)
            def _():
                grp = grp_ref[0]
                slot = grp % W_SLOTS
                for cp in weight_copies(be_ref[j], slot):
                    cp.wait()
                wgb[...] = wgf[slot].astype(BF16)
                wub[...] = wuf[slot].astype(BF16)
                wdb[...] = wdf[slot].astype(BF16)
                nxt = ge_ref[grp + 1]

                @pl.when(nxt >= 0)
                def _():
                    for cp in weight_copies(nxt, 1 - slot):
                        cp.start()
                grp_ref[0] = grp + 1

            rows = pl.ds(pl.multiple_of(s * MOE_BLOCK, MOE_BLOCK), MOE_BLOCK)
            lo, hi = _unpack_rows(x_ref[rows, :])
            lo = lo.astype(BF16)
            hi = hi.astype(BF16)
            g = (jnp.dot(lo, wgb[:PACKED, :], preferred_element_type=F32)
                 + jnp.dot(hi, wgb[PACKED:, :], preferred_element_type=F32))
            u = (jnp.dot(lo, wub[:PACKED, :], preferred_element_type=F32)
                 + jnp.dot(hi, wub[PACKED:, :], preferred_element_type=F32))
            h = (g * _sigmoid(g) * u).astype(BF16)
            o_ref[rows, :] = _pack_rows(jnp.dot(h, wdb[...], preferred_element_type=F32))
        return carry
    lax.fori_loop(0, EXPERT_GROUP, block, 0)


def _experts(xs, block_expert, n_used, group_expert, w_gate, w_up, w_down):
    cap = xs.shape[0]
    n_blocks = cap // MOE_BLOCK
    assert n_blocks % EXPERT_GROUP == 0
    rows = EXPERT_GROUP * MOE_BLOCK
    last = lambda j, be, nu, ge: jnp.minimum(j, (nu[0] - 1) // EXPERT_GROUP)
    hbm = pl.BlockSpec(memory_space=pl.ANY)
    gs = pltpu.PrefetchScalarGridSpec(
        num_scalar_prefetch=3,
        grid=(n_blocks // EXPERT_GROUP,),
        in_specs=[pl.BlockSpec((rows, PACKED), lambda j, be, nu, ge: (last(j, be, nu, ge), 0)), hbm, hbm, hbm],
        out_specs=pl.BlockSpec((rows, PACKED), lambda j, be, nu, ge: (last(j, be, nu, ge), 0)),
        scratch_shapes=[pltpu.VMEM((W_SLOTS, D_MODEL, D_EXPERT), F32),
                        pltpu.VMEM((W_SLOTS, D_MODEL, D_EXPERT), F32),
                        pltpu.VMEM((W_SLOTS, D_EXPERT, D_MODEL), F32),
                        pltpu.VMEM((D_MODEL, D_EXPERT), BF16),
                        pltpu.VMEM((D_MODEL, D_EXPERT), BF16),
                        pltpu.VMEM((D_EXPERT, D_MODEL), BF16),
                        pltpu.SMEM((1,), jnp.int32),
                        pltpu.SemaphoreType.DMA((W_SLOTS, 3))],
    )
    return pl.pallas_call(
        _expert_kernel,
        grid_spec=gs,
        out_shape=jax.ShapeDtypeStruct((cap, PACKED), U32),
        compiler_params=_cparams(1),
        name="experts",
    )(block_expert, n_used, group_expert, xs, w_gate, w_up, w_down)


CB_TM = RT_TM


def _combine_kernel(x1_ref, gt_ref, y2_ref, o_ref):
    g = gt_ref[...]
    lo1, hi1 = _unpack_rows(y2_ref[0, 0])
    lo2, hi2 = _unpack_rows(y2_ref[0, 1])
    o_ref[:, :PACKED] = x1_ref[:, :PACKED] + g[:, 0:1] * lo1 + g[:, 1:2] * lo2
    o_ref[:, PACKED:] = x1_ref[:, PACKED:] + g[:, 0:1] * hi1 + g[:, 1:2] * hi2


def _combine(x1, gates, y2):
    T = x1.shape[0]
    nt = T // CB_TM
    return pl.pallas_call(
        _combine_kernel,
        grid=(nt,),
        in_specs=[pl.BlockSpec((CB_TM, D_MODEL), lambda i: (i, 0)),
                  pl.BlockSpec((CB_TM, RINFO), lambda i: (i, 0)),
                  pl.BlockSpec((1, TOP_K, CB_TM, PACKED), lambda i: (i, 0, 0, 0))],
        out_specs=pl.BlockSpec((CB_TM, D_MODEL), lambda i: (i, 0)),
        out_shape=jax.ShapeDtypeStruct((T, D_MODEL), F32),
        compiler_params=_cparams(1),
        name="combine",
    )(x1, gates, y2)


def _layer(x, rel_bias, ln1, w_in, q_norm, k_norm, attn_sink, conv_w, conv_b, lru_wa, lru_ba, lru_wi, lru_bi,
           lru_lambda, out_norm_attn, out_norm_lru, w_out, ln2, w_group, b_group, w_er, b_er, w_gate, w_up, w_down):
    B, S, D = x.shape
    T = B * S
    x2 = x.reshape(T, D)
    q, kv, xr, gr = _in_proj(x2, ln1, w_in, q_norm, k_norm)
    bias_tab = _bias_table(rel_bias)
    attn_n = _attention(q.reshape(B, S, ATTN_WIDTH), kv.reshape(B, S, 2 * KV_WIDTH), bias_tab, attn_sink,
                        out_norm_attn)
    lru_n = _rglru(xr.reshape(B, S, LRU_WIDTH), gr.reshape(B, S, LRU_WIDTH), conv_w, conv_b,
                   lru_wa, lru_ba, lru_wi, lru_bi, lru_lambda, out_norm_lru)
    x1, h2, gates, ei, cnt = _out_route(attn_n.reshape(T, ATTN_WIDTH), lru_n.reshape(T, LRU_WIDTH), x2, w_out, ln2,
                                        w_group, b_group, w_er, b_er)
    cap = _moe_cap(T)
    dest, pstart, block_expert, n_used, group_expert = _layout(ei, cnt, cap // MOE_BLOCK)
    xs = _padfill(_sc_dispatch(h2, dest, cap), pstart, cnt)
    yb = _experts(xs, block_expert, n_used, group_expert, w_gate, w_up, w_down)
    out = _combine(x1, gates, _sc_gather(yb, dest))
    return out.reshape(B, S, D)


def kernel(x, rel_bias, ln1, w_in, q_norm, k_norm, attn_sink, conv_w, conv_b, lru_wa, lru_ba, lru_wi, lru_bi,
           lru_lambda, out_norm_attn, out_norm_lru, w_out, ln2, w_group, b_group, w_expert_router, b_expert_router,
           w_gate, w_up, w_down):
    depth = ln1.shape[0]
    for l in range(depth):
        x = _layer(x, rel_bias, ln1[l], w_in[l], q_norm[l], k_norm[l], attn_sink[l], conv_w[l], conv_b[l],
                   lru_wa[l], lru_ba[l], lru_wi[l], lru_bi[l], lru_lambda[l], out_norm_attn[l], out_norm_lru[l],
                   w_out[l], ln2[l], w_group[l], b_group[l], w_expert_router[l], b_expert_router[l],
                   w_gate[l], w_up[l], w_down[l])
    return x
```

```python
import functools
import math

import jax
import jax.numpy as jnp
import numpy as np
from jax import lax
from jax.experimental import pallas as pl
from jax.experimental.pallas import tpu as pltpu
from jax.experimental.pallas import tpu_sc as plsc

D_MODEL = 1024
N_HEADS = 8
N_KV_HEADS = 2
HEAD_DIM = 64
Q_PER_KV = N_HEADS // N_KV_HEADS
ATTN_WIDTH = N_HEADS * HEAD_DIM
KV_WIDTH = N_KV_HEADS * HEAD_DIM
WINDOW = 128
BLOCK = 128
NUM_BUCKETS = 32
MAX_DISTANCE = 128
LRU_WIDTH = D_MODEL - ATTN_WIDTH
LRU_BLOCKS = 8
LRU_BLOCK_DIM = LRU_WIDTH // LRU_BLOCKS
LRU_C = 8.0
CONV_W = 4
CONV_LEFT = 2
N_GROUPS = 4
EXPERTS_PER_GROUP = 8
N_EXPERTS = N_GROUPS * EXPERTS_PER_GROUP
TOP_K = 2
D_EXPERT = 512
MOE_BLOCK = 256
EPS = 1e-6
NEG_INF = -1e30

LANES = 128
SUBLANES = 8
VMEM_LIMIT = 56 * 1024 * 1024

F32 = jnp.float32
BF16 = jnp.bfloat16


def _cparams(n_axes, vmem=VMEM_LIMIT):
    return pltpu.CompilerParams(dimension_semantics=("arbitrary",) * n_axes, vmem_limit_bytes=vmem)


def _rms(x, gain):
    return x * lax.rsqrt(jnp.mean(x * x, axis=-1, keepdims=True) + EPS) * gain


U32 = jnp.uint32
HI_MASK = 0xFFFF0000
PACKED = D_MODEL // 2


def _pack_rows(x):
    h = x.shape[1] // 2
    lo = lax.bitcast_convert_type(x[:, :h].astype(BF16).astype(F32), U32) >> 16
    hi = lax.bitcast_convert_type(x[:, h:].astype(BF16).astype(F32), U32) & jnp.uint32(HI_MASK)
    return lo | hi


def _unpack_rows(p):
    lo = lax.bitcast_convert_type(p << 16, F32)
    hi = lax.bitcast_convert_type(p & jnp.uint32(HI_MASK), F32)
    return lo, hi


IN_TM = 1024


def _head_rms(x, n_heads, gain):
    head = lax.broadcasted_iota(jnp.int32, (1, n_heads * HEAD_DIM), 1) // HEAD_DIM
    x2 = x * x
    scale = jnp.zeros_like(x)
    for h in range(n_heads):
        ms = jnp.sum(jnp.where(head == h, x2, 0.0), axis=-1, keepdims=True) * (1.0 / HEAD_DIM)
        scale = jnp.where(head == h, lax.rsqrt(ms + EPS), scale)
    return x * scale * gain


def _in_proj_kernel(x_ref, g_ref, w_ref, qg_ref, kg_ref, q_ref, kv_ref, xr_ref, gr_ref, wb_ref):
    @pl.when(pl.program_id(0) == 0)
    def _():
        wb_ref[...] = w_ref[...].astype(BF16)

    h = _rms(x_ref[...], g_ref[...]).astype(BF16)
    c_k = ATTN_WIDTH
    c_v = c_k + KV_WIDTH
    c_x = c_v + KV_WIDTH
    c_g = c_x + LRU_WIDTH
    q = jnp.dot(h, wb_ref[:, :c_k], preferred_element_type=F32)
    q_ref[...] = _head_rms(q, N_HEADS, qg_ref[...]).astype(BF16)
    k = jnp.dot(h, wb_ref[:, c_k:c_v], preferred_element_type=F32)
    kv_ref[:, :KV_WIDTH] = _head_rms(k, N_KV_HEADS, kg_ref[...]).astype(BF16)
    kv_ref[:, KV_WIDTH:] = jnp.dot(h, wb_ref[:, c_v:c_x], preferred_element_type=F32).astype(BF16)
    xr_ref[...] = jnp.dot(h, wb_ref[:, c_x:c_g], preferred_element_type=F32)
    gr_ref[...] = jnp.dot(h, wb_ref[:, c_g:], preferred_element_type=F32)


def _in_proj(x2, ln1, w_in, q_gain, k_gain):
    T = x2.shape[0]
    n_in = w_in.shape[1]
    row = lambda w: pl.BlockSpec((IN_TM, w), lambda i: (i, 0))
    qg = (jnp.tile(q_gain.astype(F32), N_HEADS) * (HEAD_DIM ** -0.5)).reshape(1, ATTN_WIDTH)
    kg = jnp.tile(k_gain.astype(F32), N_KV_HEADS).reshape(1, KV_WIDTH)
    return pl.pallas_call(
        _in_proj_kernel,
        grid=(T // IN_TM,),
        in_specs=[row(D_MODEL),
                  pl.BlockSpec((1, D_MODEL), lambda i: (0, 0)),
                  pl.BlockSpec((D_MODEL, n_in), lambda i: (0, 0)),
                  pl.BlockSpec((1, ATTN_WIDTH), lambda i: (0, 0)),
                  pl.BlockSpec((1, KV_WIDTH), lambda i: (0, 0))],
        out_specs=[row(ATTN_WIDTH), row(2 * KV_WIDTH), row(LRU_WIDTH), row(LRU_WIDTH)],
        out_shape=[jax.ShapeDtypeStruct((T, ATTN_WIDTH), BF16),
                   jax.ShapeDtypeStruct((T, 2 * KV_WIDTH), BF16),
                   jax.ShapeDtypeStruct((T, LRU_WIDTH), F32),
                   jax.ShapeDtypeStruct((T, LRU_WIDTH), F32)],
        scratch_shapes=[pltpu.VMEM((D_MODEL, n_in), BF16)],
        compiler_params=_cparams(1),
        name="in_proj",
    )(x2, ln1.reshape(1, D_MODEL), w_in, qg, kg)


def _t5_bucket(rel):
    half = NUM_BUCKETS // 2
    max_exact = half // 2
    base = jnp.where(rel > 0, half, 0)
    n = jnp.abs(rel)
    nf = jnp.maximum(n, 1).astype(jnp.float32)
    large = max_exact + (jnp.log(nf / max_exact) / math.log(MAX_DISTANCE / max_exact)
                         * (half - max_exact)).astype(jnp.int32)
    large = jnp.minimum(large, half - 1)
    return base + jnp.where(n < max_exact, n, large)


HEAD_PAIRS = Q_PER_KV // 2
EDGE_VARIANTS = 3


def _bias_kernel(rb_ref, bucket_ref, band_ref, o_ref):
    bucket = bucket_ref[...]
    band = band_ref[...] > 0
    col = lax.broadcasted_iota(jnp.int32, bucket.shape, 1)
    valid = (band & (col >= BLOCK), band, band & (col < 2 * BLOCK))
    for h in range(N_HEADS):
        acc = jnp.zeros(bucket.shape, F32)
        for b in range(NUM_BUCKETS):
            acc = jnp.where(bucket == b, rb_ref[b, h], acc)
        kv, g = divmod(h, Q_PER_KV)
        pair, parity = divmod(g, 2)
        for var in range(EDGE_VARIANTS):
            o_ref[var, kv, parity, pair * BLOCK:(pair + 1) * BLOCK, :] = jnp.where(valid[var], acc, NEG_INF)


def _bias_table(rel_bias):
    qi = jnp.arange(BLOCK, dtype=jnp.int32)
    kj = jnp.arange(3 * BLOCK, dtype=jnp.int32)
    rel = kj[None, :] - BLOCK - qi[:, None]
    bucket = _t5_bucket(rel).astype(jnp.int32)
    band = (jnp.abs(rel) <= WINDOW).astype(jnp.int32)
    return pl.pallas_call(
        _bias_kernel,
        in_specs=[pl.BlockSpec(memory_space=pltpu.SMEM),
                  pl.BlockSpec(memory_space=pltpu.VMEM),
                  pl.BlockSpec(memory_space=pltpu.VMEM)],
        out_specs=pl.BlockSpec(memory_space=pltpu.VMEM),
        out_shape=jax.ShapeDtypeStruct((EDGE_VARIANTS, N_KV_HEADS, 2, HEAD_PAIRS * BLOCK, 3 * BLOCK), F32),
        name="bias_table",
    )(rel_bias.astype(F32), bucket, band)


def _attn_kernel(sink_ref, q_ref, kp_ref, kc_ref, kn_ref, bias_ref, og_ref, o_ref):
    n = pl.program_id(1)
    kv_all = jnp.concatenate([kp_ref[0], kc_ref[0], kn_ref[0]], axis=0)
    for qb in range(ATTN_QB):
        variant = 1
        if qb == 0:
            variant = jnp.where(n == 0, 0, 1)
        if qb == ATTN_QB - 1:
            variant = jnp.where(n == pl.num_programs(1) - 1, 2, variant)
        out = _attn_block(q_ref[0, qb * BLOCK:(qb + 1) * BLOCK, :], kv_all[qb * BLOCK:(qb + 3) * BLOCK, :],
                          lambda kv, parity: bias_ref[variant, kv, parity], sink_ref)
        o_ref[0, qb * BLOCK:(qb + 1) * BLOCK, :] = _rms(out, og_ref[...]).astype(o_ref.dtype)


def _attn_block(q, kvw, bias, sink_ref):
    low = lax.broadcasted_iota(jnp.int32, (1, LANES), 1) < HEAD_DIM
    rowi = lax.broadcasted_iota(jnp.int32, (HEAD_PAIRS * BLOCK, 1), 0)
    swap = lambda slab: pltpu.roll(slab.astype(F32), HEAD_DIM, 1).astype(BF16)
    kslab, vslab = kvw[:, :KV_WIDTH], kvw[:, KV_WIDTH:]
    kslab_sw, vslab_sw = swap(kslab), swap(vslab)
    outs = []
    for kv in range(N_KV_HEADS):
        src = ((kslab, vslab), (kslab_sw, vslab_sw)) if kv == 0 else ((kslab_sw, vslab_sw), (kslab, vslab))
        base = kv * Q_PER_KV * HEAD_DIM
        qpair = jnp.concatenate([q[:, base + j * LANES:base + (j + 1) * LANES] for j in range(HEAD_PAIRS)], axis=0)
        acc = None
        for parity in range(2):
            keep = low if parity == 0 else jnp.logical_not(low)
            kz = jnp.where(keep, src[parity][0], jnp.zeros_like(kslab))
            vz = jnp.where(keep, src[parity][1], jnp.zeros_like(vslab))
            sink = jnp.zeros((HEAD_PAIRS * BLOCK, 1), F32)
            for j in range(HEAD_PAIRS):
                sink = jnp.where(rowi // BLOCK == j, sink_ref[kv * Q_PER_KV + 2 * j + parity], sink)
            s = lax.dot_general(qpair, kz, (((1,), (1,)), ((), ())), preferred_element_type=F32)
            s = s + bias(kv, parity)
            m = jnp.maximum(jnp.max(s, axis=-1, keepdims=True), sink)
            p = jnp.exp(s - m)
            denom = jnp.sum(p, axis=-1, keepdims=True) + jnp.exp(sink - m)
            o = jnp.dot(p.astype(BF16), vz, preferred_element_type=F32) * (1.0 / denom)
            acc = o if acc is None else acc + o
        outs += [acc[j * BLOCK:(j + 1) * BLOCK, :] for j in range(HEAD_PAIRS)]
    return jnp.concatenate(outs, axis=1)


ATTN_QB = 4


def _attention(q, kv, bias_tab, sink, out_gain):
    B, S, _ = q.shape
    nb = S // BLOCK
    assert ATTN_QB >= 2 and nb % ATTN_QB == 0, "a step's first and last query blocks must be distinct"
    ns = nb // ATTN_QB
    rows = ATTN_QB * BLOCK
    kvspec = lambda f: pl.BlockSpec((1, BLOCK, 2 * KV_WIDTH), f)
    return pl.pallas_call(
        _attn_kernel,
        grid=(B, ns),
        in_specs=[pl.BlockSpec(memory_space=pltpu.SMEM),
                  pl.BlockSpec((1, rows, ATTN_WIDTH), lambda b, n: (b, n, 0)),
                  kvspec(lambda b, n: (b, jnp.maximum(n * ATTN_QB - 1, 0), 0)),
                  pl.BlockSpec((1, rows, 2 * KV_WIDTH), lambda b, n: (b, n, 0)),
                  kvspec(lambda b, n: (b, jnp.minimum((n + 1) * ATTN_QB, nb - 1), 0)),
                  pl.BlockSpec((EDGE_VARIANTS, N_KV_HEADS, 2, HEAD_PAIRS * BLOCK, 3 * BLOCK),
                               lambda b, n: (0, 0, 0, 0, 0)),
                  pl.BlockSpec((1, ATTN_WIDTH), lambda b, n: (0, 0))],
        out_specs=pl.BlockSpec((1, rows, ATTN_WIDTH), lambda b, n: (b, n, 0)),
        out_shape=jax.ShapeDtypeStruct((B, S, ATTN_WIDTH), BF16),
        compiler_params=_cparams(2),
        name="attention",
    )(sink.astype(F32), q, kv, kv, kv, bias_tab, out_gain.reshape(1, ATTN_WIDTH))


LRU_TC = 128
LRU_PITCH = LRU_TC + SUBLANES
LRU_SLABS = LRU_WIDTH // LANES
HALO = SUBLANES


def _softplus(x):
    return jnp.maximum(x, 0.0) + jnp.log(1.0 + jnp.exp(-jnp.abs(x)))


def _gelu_tanh(x):
    k = math.sqrt(2.0 / math.pi)
    hx = 0.5 * x
    return hx + hx * jnp.tanh(x * (k + (k * 0.044715) * (x * x)))


def _sigmoid(x):
    return 0.5 + 0.5 * jnp.tanh(0.5 * x)


def _rglru_kernel(xr_ref, xp_ref, xn_ref, gr_ref, cw_ref, cb_ref, wg_ref, bg_ref, lam_ref, og_ref,
                  o_ref, sx_ref, a_ref, u_ref, h_ref, carry_ref, hf_ref):
    p = pl.program_id(0)
    i = pl.program_id(1)
    nc = pl.num_programs(1)
    c = i + p * (nc - 1 - 2 * i)
    B = xr_ref.shape[0]
    TC = LRU_TC

    sx_ref[:, HALO:HALO + TC, :] = xr_ref[...]
    sx_ref[:, 0:HALO, :] = jnp.where(c > 0, xp_ref[...], 0.0)
    sx_ref[:, HALO + TC:, :] = jnp.where(c < nc - 1, xn_ref[...], 0.0)
    xc = cb_ref[...][None]
    for j in range(CONV_W):
        off = HALO + j - CONV_LEFT
        xc = xc + cw_ref[j:j + 1, :][None] * sx_ref[:, off:off + TC, :]
    xc2 = xc.reshape(B * TC, LRU_WIDTH)

    g = jnp.dot(xc2.astype(BF16), wg_ref[0], preferred_element_type=F32) + bg_ref[0]
    r = _sigmoid(g[:, :LRU_WIDTH])
    ig = _sigmoid(g[:, LRU_WIDTH:])
    a = jnp.exp(r * (-LRU_C * _softplus(-lam_ref[0])))
    z = 1.0 - a * a
    u = z * lax.rsqrt(jnp.maximum(z, 1e-30)) * ig * xc2
    for b in range(B):
        for s in range(LRU_SLABS):
            a_ref[s, b * LRU_PITCH:b * LRU_PITCH + TC, :] = a[b * TC:(b + 1) * TC, s * LANES:(s + 1) * LANES]
            u_ref[s, b * LRU_PITCH:b * LRU_PITCH + TC, :] = u[b * TC:(b + 1) * TC, s * LANES:(s + 1) * LANES]

    @pl.when(i == 0)
    def _():
        carry_ref[...] = jnp.zeros_like(carry_ref)

    def step(k, hs):
        t = k + p * (TC - 1 - 2 * k)
        out = []
        for s in range(LRU_SLABS):
            idx = pl.ds(t, B, stride=LRU_PITCH)
            hn = a_ref[s, idx, :] * hs[s] + u_ref[s, idx, :]
            h_ref[s, idx, :] = hn
            out.append(hn)
        return tuple(out)

    hs = lax.fori_loop(0, TC, step, tuple(carry_ref[s] for s in range(LRU_SLABS)), unroll=8)
    for s in range(LRU_SLABS):
        carry_ref[s] = hs[s]

    @pl.when(p == 0)
    def _():
        for b in range(B):
            for s in range(LRU_SLABS):
                hf_ref[c, s, b * TC:(b + 1) * TC, :] = h_ref[s, b * LRU_PITCH:b * LRU_PITCH + TC, :].astype(hf_ref.dtype)

    @pl.when(p == 1)
    def _():
        for b in range(B):
            hsum = jnp.concatenate(
                [h_ref[s, b * LRU_PITCH:b * LRU_PITCH + TC, :] + hf_ref[c, s, b * TC:(b + 1) * TC, :].astype(F32)
                 for s in range(LRU_SLABS)], axis=1)
            y = hsum * _gelu_tanh(gr_ref[b])
            o_ref[b] = _rms(y, og_ref[...]).astype(o_ref.dtype)


def _block_diag(w):
    eye = jnp.eye(LRU_BLOCKS, dtype=w.dtype)
    return jnp.einsum('hij,hg->higj', w, eye).reshape(LRU_WIDTH, LRU_WIDTH)


def _rglru(xr, gr, conv_w, conv_b, w_a, b_a, w_i, b_i, lam, out_gain):
    B, S, W = xr.shape
    nc = S // LRU_TC
    hb = LRU_TC // HALO
    wg = jnp.stack([jnp.concatenate([_block_diag(w_a[d]), _block_diag(w_i[d])], axis=1) for d in range(2)]).astype(BF16)
    bg = jnp.concatenate([b_a, b_i], axis=-1).reshape(2, 1, 2 * W).astype(F32)
    chunk = lambda p, i: i + p * (nc - 1 - 2 * i)
    full2 = lambda shape: pl.BlockSpec(shape, lambda p, i: (0,) * len(shape))
    return pl.pallas_call(
        _rglru_kernel,
        grid=(2, nc),
        in_specs=[pl.BlockSpec((B, LRU_TC, W), lambda p, i: (0, chunk(p, i), 0)),
                  pl.BlockSpec((B, HALO, W), lambda p, i: (0, jnp.maximum(chunk(p, i) * hb - 1, 0), 0)),
                  pl.BlockSpec((B, HALO, W), lambda p, i: (0, jnp.minimum((chunk(p, i) + 1) * hb, S // HALO - 1), 0)),
                  pl.BlockSpec((B, LRU_TC, W), lambda p, i: (0, chunk(p, i), 0)),
                  full2((CONV_W, W)),
                  full2((1, W)),
                  pl.BlockSpec((1, W, 2 * W), lambda p, i: (p, 0, 0)),
                  pl.BlockSpec((1, 1, 2 * W), lambda p, i: (p, 0, 0)),
                  pl.BlockSpec((1, 1, W), lambda p, i: (p, 0, 0)),
                  full2((1, W))],
        out_specs=pl.BlockSpec((B, LRU_TC, W), lambda p, i: (0, nc - 1 - p * i, 0)),
        out_shape=jax.ShapeDtypeStruct((B, S, W), BF16),
        scratch_shapes=[pltpu.VMEM((B, LRU_TC + 2 * HALO, W), F32),
                        pltpu.VMEM((LRU_SLABS, B * LRU_PITCH, LANES), F32),
                        pltpu.VMEM((LRU_SLABS, B * LRU_PITCH, LANES), F32),
                        pltpu.VMEM((LRU_SLABS, B * LRU_PITCH, LANES), F32),
                        pltpu.VMEM((LRU_SLABS, B, LANES), F32),
                        pltpu.VMEM((nc, LRU_SLABS, B * LRU_TC, LANES), BF16)],
        compiler_params=_cparams(2),
        name="rglru",
    )(xr, xr, xr, gr, conv_w.astype(F32), conv_b.reshape(1, W).astype(F32), wg, bg,
      lam.reshape(2, 1, W).astype(F32), out_gain.reshape(1, W).astype(F32))


RT_TM = 512
RT_COLS = LANES
RT_ROWS = 48
RINFO = SUBLANES


def _split_bf16(x):
    hi = x.astype(BF16)
    lo = (x - hi.astype(F32)).astype(BF16)
    return hi, lo


def _route_kernel(an_ref, ln_ref, x_ref, wo_ref, g2_ref, wr_ref, br_ref,
                  x1_ref, h2_ref, gt_ref, ei_ref, cnt_ref, wob_ref, wrb_ref, tri_ref, run_ref, runc_ref):
    @pl.when(pl.program_id(0) == 0)
    def _():
        wob_ref[...] = wo_ref[...].astype(BF16)
        hi, lo = _split_bf16(wr_ref[...])
        wrb_ref[:RT_ROWS, :] = hi
        wrb_ref[RT_ROWS:, :] = lo
        r = lax.broadcasted_iota(jnp.int32, (RT_TM, RT_TM), 0)
        cidx = lax.broadcasted_iota(jnp.int32, (RT_TM, RT_TM), 1)
        tri_ref[...] = (r < cidx).astype(BF16)
        run_ref[...] = jnp.zeros_like(run_ref)
        runc_ref[...] = jnp.zeros_like(runc_ref)

    x1 = (x_ref[...]
          + jnp.dot(an_ref[...], wob_ref[:ATTN_WIDTH, :], preferred_element_type=F32)
          + jnp.dot(ln_ref[...], wob_ref[ATTN_WIDTH:, :], preferred_element_type=F32))
    x1_ref[...] = x1
    h2 = _rms(x1, g2_ref[...])
    h2_ref[...] = _pack_rows(h2)

    hi, lo = _split_bf16(h2)
    nt_dims = (((1,), (1,)), ((), ()))
    t1 = lax.dot_general(wrb_ref[...], hi, nt_dims, preferred_element_type=F32)
    t2 = lax.dot_general(wrb_ref[:RT_ROWS, :], lo, nt_dims, preferred_element_type=F32)
    logit = t1[:RT_ROWS] + t1[RT_ROWS:] + t2 + br_ref[...]

    sub = lax.broadcasted_iota(jnp.int32, (SUBLANES, RT_TM), 0)
    first_min = lambda hit: jnp.min(jnp.where(hit, sub, SUBLANES), axis=0, keepdims=True)
    is_g = sub < N_GROUPS
    gl = jnp.where(is_g, logit[:SUBLANES], -jnp.inf)
    gm = jnp.max(gl, axis=0, keepdims=True)
    gidx = first_min(gl == gm)
    g_p = 1.0 / jnp.sum(jnp.where(is_g, jnp.exp(logit[:SUBLANES] - gm), 0.0), axis=0, keepdims=True)
    el = logit[SUBLANES:2 * SUBLANES]
    for g in range(1, N_GROUPS):
        el = jnp.where(gidx == g, logit[(g + 1) * SUBLANES:(g + 2) * SUBLANES], el)
    m1 = jnp.max(el, axis=0, keepdims=True)
    i1 = first_min(el == m1)
    el2 = jnp.where(sub == i1, -jnp.inf, el)
    m2 = jnp.max(el2, axis=0, keepdims=True)
    i2 = first_min(el2 == m2)
    t = jnp.exp(m2 - m1)
    gate1 = g_p / (1.0 + t)
    gate2 = g_p * t / (1.0 + t)
    e1 = gidx * EXPERTS_PER_GROUP + i1
    e2 = gidx * EXPERTS_PER_GROUP + i2

    erow = lax.broadcasted_iota(jnp.int32, (N_EXPERTS, RT_TM), 0)
    oh1 = erow == e1
    oh2 = erow == e2
    oh = (oh1 | oh2).astype(F32)
    ohb = oh.astype(BF16)
    cum = jnp.dot(ohb, tri_ref[...], preferred_element_type=F32) + runc_ref[...]
    rank1 = jnp.sum(jnp.where(oh1, cum, 0.0), axis=0, keepdims=True)
    rank2 = jnp.sum(jnp.where(oh2, cum, 0.0), axis=0, keepdims=True)
    runc_ref[...] = runc_ref[...] + jnp.sum(oh, axis=1, keepdims=True)
    tile_cnt = lax.dot_general(jnp.ones((SUBLANES, RT_TM), BF16), ohb, nt_dims, preferred_element_type=F32)
    run_ref[:, :N_EXPERTS] = run_ref[:, :N_EXPERTS] + tile_cnt[0:1]
    cnt_ref[...] = run_ref[...].astype(jnp.int32)

    rows = [e1, e2, rank1.astype(jnp.int32), rank2.astype(jnp.int32)]
    ei = jnp.zeros((RINFO, RT_TM), jnp.int32)
    for k, v in enumerate(rows):
        ei = jnp.where(sub == k, v, ei)
    ei_ref[0] = ei
    gt_ref[...] = jnp.where(sub == 0, gate1, jnp.where(sub == 1, gate2, 0.0)).T


def _out_route(attn_n, lru_n, x2, w_out, ln2, w_group, b_group, w_er, b_er):
    T = x2.shape[0]
    pad_g = SUBLANES - N_GROUPS
    wr = jnp.concatenate([jnp.pad(w_group.T, ((0, pad_g), (0, 0))),
                          jnp.transpose(w_er, (0, 2, 1)).reshape(N_EXPERTS, D_MODEL)], axis=0)
    wr = jnp.pad(wr, ((0, RT_ROWS - wr.shape[0]), (0, 0))).astype(F32)
    br = jnp.concatenate([jnp.pad(b_group, (0, pad_g)), b_er.reshape(-1)])
    br = jnp.pad(br, (0, RT_ROWS - br.shape[0])).reshape(RT_ROWS, 1).astype(F32)
    row = lambda w: pl.BlockSpec((RT_TM, w), lambda i: (i, 0))
    const = lambda shape: pl.BlockSpec(shape, lambda i: (0, 0))
    return pl.pallas_call(
        _route_kernel,
        grid=(T // RT_TM,),
        in_specs=[row(ATTN_WIDTH), row(LRU_WIDTH), row(D_MODEL), const((D_MODEL, D_MODEL)), const((1, D_MODEL)),
                  const((RT_ROWS, D_MODEL)), const((RT_ROWS, 1))],
        out_specs=[row(D_MODEL), row(PACKED), row(RINFO),
                   pl.BlockSpec((1, RINFO, RT_TM), lambda i: (i, 0, 0)), const((1, RT_COLS))],
        out_shape=[jax.ShapeDtypeStruct((T, D_MODEL), F32),
                   jax.ShapeDtypeStruct((T, PACKED), U32),
                   jax.ShapeDtypeStruct((T, RINFO), F32),
                   jax.ShapeDtypeStruct((T // RT_TM, RINFO, RT_TM), jnp.int32),
                   jax.ShapeDtypeStruct((1, RT_COLS), jnp.int32)],
        scratch_shapes=[pltpu.VMEM((D_MODEL, D_MODEL), BF16),
                        pltpu.VMEM((2 * RT_ROWS, D_MODEL), BF16),
                        pltpu.VMEM((RT_TM, RT_TM), BF16),
                        pltpu.VMEM((1, RT_COLS), F32),
                        pltpu.VMEM((N_EXPERTS, 1), F32)],
        compiler_params=_cparams(1),
        name="out_route",
    )(attn_n, lru_n, x2, w_out, ln2.reshape(1, D_MODEL).astype(F32), wr, br)


def _moe_cap(T):
    A = T * TOP_K
    return ((A + MOE_BLOCK - 1) // MOE_BLOCK) * MOE_BLOCK + N_EXPERTS * MOE_BLOCK


PAD_BITS = tuple(1 << b for b in reversed(range(3, MOE_BLOCK.bit_length() - 1)))


def _layout_kernel(cnt_ref, ei_ref, dest_ref, pstart, be_ref, nu_ref, ge_ref):
    n_blocks = be_ref.shape[0]

    def lay(e, carry):
        start, blk, grp = carry
        pstart[e] = start
        nb = (cnt_ref[0, e] + MOE_BLOCK - 1) // MOE_BLOCK
        ge_ref[grp] = e

        def fill(k, c):
            be_ref[blk + k] = e
            return c
        lax.fori_loop(0, nb, fill, 0)
        return start + nb * MOE_BLOCK, blk + nb, grp + (nb > 0).astype(jnp.int32)
    _, used, groups = lax.fori_loop(0, N_EXPERTS, lay, (jnp.int32(0), jnp.int32(0), jnp.int32(0)))
    nu_ref[0] = used

    def tail(k, c):
        be_ref[k] = N_EXPERTS - 1
        return c
    lax.fori_loop(used, n_blocks, tail, 0)

    def no_group(k, c):
        ge_ref[k] = -1
        return c
    lax.fori_loop(groups, N_EXPERTS + 1, no_group, 0)

    expert = ei_ref[:, 0:TOP_K, :]
    dest = ei_ref[:, TOP_K:2 * TOP_K, :]
    for e in range(N_EXPERTS):
        dest = dest + jnp.where(expert == e, pstart[e], 0)
    dest_ref[...] = dest


def _layout(ei, cnt, n_blocks):
    nt = ei.shape[0]
    smem = pl.BlockSpec(memory_space=pltpu.SMEM)
    vmem = pl.BlockSpec(memory_space=pltpu.VMEM)
    return pl.pallas_call(
        _layout_kernel,
        in_specs=[smem, vmem],
        out_specs=[vmem, smem, smem, smem, smem],
        out_shape=[jax.ShapeDtypeStruct((nt, TOP_K, RT_TM), jnp.int32),
                   jax.ShapeDtypeStruct((N_EXPERTS,), jnp.int32),
                   jax.ShapeDtypeStruct((n_blocks,), jnp.int32),
                   jax.ShapeDtypeStruct((1,), jnp.int32),
                   jax.ShapeDtypeStruct((N_EXPERTS + 1,), jnp.int32)],
        name="layout",
    )(cnt, ei)


SC_CHUNK = 64
SC_BUFS = 3
SC_LEAD = SC_BUFS - 1


def _sc_workers():
    info = plsc.get_sparse_core_info()
    return info.num_cores, info.num_subcores


def _sc_ring(n_chunks, read, write):
    for c in range(min(SC_LEAD, n_chunks)):
        for cp in read(c):
            cp.start()
    reclaimed = set()
    for c in range(n_chunks):
        for cp in read(c):
            cp.wait()
        for cp in write(c):
            cp.start()
        nxt = c + SC_LEAD
        if nxt < n_chunks:
            if nxt - SC_BUFS >= 0:
                for cp in write(nxt - SC_BUFS):
                    cp.wait()
                reclaimed.add(nxt - SC_BUFS)
            for cp in read(nxt):
                cp.start()
    for c in range(n_chunks):
        if c not in reclaimed:
            for cp in write(c):
                cp.wait()


def _sc_dispatch(h2p, dest, cap):
    T = h2p.shape[0]
    nc, ns = _sc_workers()
    per_w = T // (nc * ns)
    n_ch = per_w // SC_CHUNK
    assert dest.shape == (nc * ns, TOP_K, per_w) and per_w % SC_CHUNK == 0
    idx = dest.reshape(nc * ns, TOP_K * n_ch, SC_CHUNK)
    mesh = plsc.VectorSubcoreMesh(core_axis_name="c", subcore_axis_name="s")

    @functools.partial(
        pl.kernel, mesh=mesh,
        out_type=jax.ShapeDtypeStruct((cap, PACKED), U32),
        scratch_types=[pltpu.VMEM((TOP_K * n_ch, SC_CHUNK), jnp.int32),
                       pltpu.VMEM((SC_BUFS, SC_CHUNK, PACKED), U32),
                       pltpu.SemaphoreType.DMA((SC_BUFS,)),
                       pltpu.SemaphoreType.DMA((SC_BUFS,))])
    def scatter(src_hbm, idx_hbm, out_hbm, idx_v, rows_v, rsem, wsem):
        wid = lax.axis_index("s") * nc + lax.axis_index("c")
        base = pl.multiple_of(wid * per_w, per_w)
        pltpu.sync_copy(idx_hbm.at[wid], idx_v)

        def read(c):
            b = c % SC_BUFS
            return [pltpu.make_async_copy(src_hbm.at[pl.ds(base + c * SC_CHUNK, SC_CHUNK)], rows_v.at[b], rsem.at[b])]

        def write(c):
            b = c % SC_BUFS
            return [pltpu.make_async_copy(rows_v.at[b], out_hbm.at[idx_v.at[k * n_ch + c]], wsem.at[b])
                    for k in range(TOP_K)]
        _sc_ring(n_ch, read, write)

    return scatter(h2p, idx)


def _sc_gather(yb, dest):
    nt, _, tm = dest.shape
    nc, ns = _sc_workers()
    n_rows = nt * TOP_K * tm
    per_w = n_rows // (nc * ns)
    n_ch = per_w // SC_CHUNK
    assert per_w * nc * ns == n_rows and per_w % SC_CHUNK == 0
    mesh = plsc.VectorSubcoreMesh(core_axis_name="c", subcore_axis_name="s")

    @functools.partial(
        pl.kernel, mesh=mesh,
        out_type=jax.ShapeDtypeStruct((n_rows, PACKED), U32),
        scratch_types=[pltpu.VMEM((per_w,), jnp.int32),
                       pltpu.VMEM((SC_BUFS, SC_CHUNK, PACKED), U32),
                       pltpu.SemaphoreType.DMA((SC_BUFS,)),
                       pltpu.SemaphoreType.DMA((SC_BUFS,))])
    def gather(table_hbm, idx_hbm, out_hbm, idx_v, rows_v, gsem, wsem):
        wid = lax.axis_index("s") * nc + lax.axis_index("c")
        base = pl.multiple_of(wid * per_w, per_w)
        pltpu.sync_copy(idx_hbm.at[pl.ds(base, per_w)], idx_v)

        def read(c):
            b = c % SC_BUFS
            return [pltpu.make_async_copy(table_hbm.at[idx_v.at[pl.ds(c * SC_CHUNK, SC_CHUNK)]], rows_v.at[b], gsem.at[b])]

        def write(c):
            b = c % SC_BUFS
            return [pltpu.make_async_copy(rows_v.at[b], out_hbm.at[pl.ds(base + c * SC_CHUNK, SC_CHUNK)], wsem.at[b])]
        _sc_ring(n_ch, read, write)

    return gather(yb, dest.reshape(n_rows)).reshape(nt, TOP_K, tm, PACKED)


def _padfill_kernel(cnt_ref, pstart, xs_in, xs_ref, zeros, zsem):
    del xs_in

    def pad_copies(fn):
        for e in range(N_EXPERTS):
            cnt = cnt_ref[0, e]
            head = (-cnt) & (SUBLANES - 1)
            rest = ((-cnt) & (MOE_BLOCK - 1)) - head
            off = pstart[e] + cnt
            for k in range(SUBLANES - 1):
                @pl.when(k < head)
                def _(off=off, k=k):
                    fn(pltpu.make_async_copy(zeros.at[pl.ds(0, 1), :], xs_ref.at[pl.ds(off + k, 1), :], zsem))
            off = off + head
            for bit in PAD_BITS:
                @pl.when((rest & bit) != 0)
                def _(off=off, bit=bit):
                    fn(pltpu.make_async_copy(zeros.at[pl.ds(0, bit), :],
                                             xs_ref.at[pl.ds(pl.multiple_of(off, SUBLANES), bit), :], zsem))
                off = off + (rest & bit)

    zeros[...] = jnp.zeros_like(zeros)
    pad_copies(lambda cp: cp.start())
    pad_copies(lambda cp: cp.wait())


def _padfill(xs, pstart, cnt):
    smem = pl.BlockSpec(memory_space=pltpu.SMEM)
    hbm = pl.BlockSpec(memory_space=pl.ANY)
    return pl.pallas_call(
        _padfill_kernel,
        in_specs=[smem, smem, hbm],
        out_specs=hbm,
        out_shape=jax.ShapeDtypeStruct(xs.shape, xs.dtype),
        input_output_aliases={2: 0},
        scratch_shapes=[pltpu.VMEM((MOE_BLOCK // 2, PACKED), U32), pltpu.SemaphoreType.DMA(())],
        name="padfill",
    )(cnt, pstart, xs)


W_SLOTS = 2
EXPERT_GROUP = 8


def _expert_kernel(be_ref, nu_ref, ge_ref, x_ref, wg_hbm, wu_hbm, wd_hbm, o_ref,
                   wgf, wuf, wdf, wgb, wub, wdb, grp_ref, sems):
    step = pl.program_id(0)

    def weight_copies(e, slot):
        return (pltpu.make_async_copy(wg_hbm.at[e], wgf.at[slot], sems.at[slot, 0]),
                pltpu.make_async_copy(wu_hbm.at[e], wuf.at[slot], sems.at[slot, 1]),
                pltpu.make_async_copy(wd_hbm.at[e], wdf.at[slot], sems.at[slot, 2]))

    @pl.when(step == 0)
    def _():
        grp_ref[0] = 0
        for cp in weight_copies(ge_ref[0], 0):
            cp.start()

    def block(s, carry):
        j = step * EXPERT_GROUP + s

        @pl.when(j < nu_ref[0])
        def _():
            first = jnp.logical_or(j == 0, be_ref[j] != be_ref[jnp.maximum(j - 1, 0)])

            @pl.when(first)
            def _():
                grp = grp_ref[0]
                slot = grp % W_SLOTS
                for cp in weight_copies(be_ref[j], slot):
                    cp.wait()
                wgb[...] = wgf[slot].astype(BF16)
                wub[...] = wuf[slot].astype(BF16)
                wdb[...] = wdf[slot].astype(BF16)
                nxt = ge_ref[grp + 1]

                @pl.when(nxt >= 0)
                def _():
                    for cp in weight_copies(nxt, 1 - slot):
                        cp.start()
                grp_ref[0] = grp + 1

            rows = pl.ds(pl.multiple_of(s * MOE_BLOCK, MOE_BLOCK), MOE_BLOCK)
            lo, hi = _unpack_rows(x_ref[rows, :])
            lo = lo.astype(BF16)
            hi = hi.astype(BF16)
            g = (jnp.dot(lo, wgb[:PACKED, :], preferred_element_type=F32)
                 + jnp.dot(hi, wgb[PACKED:, :], preferred_element_type=F32))
            u = (jnp.dot(lo, wub[:PACKED, :], preferred_element_type=F32)
                 + jnp.dot(hi, wub[PACKED:, :], preferred_element_type=F32))
            h = (g * _sigmoid(g) * u).astype(BF16)
            o_ref[rows, :] = _pack_rows(jnp.dot(h, wdb[...], preferred_element_type=F32))
        return carry
    lax.fori_loop(0, EXPERT_GROUP, block, 0)


def _experts(xs, block_expert, n_used, group_expert, w_gate, w_up, w_down):
    cap = xs.shape[0]
    n_blocks = cap // MOE_BLOCK
    assert n_blocks % EXPERT_GROUP == 0
    rows = EXPERT_GROUP * MOE_BLOCK
    last = lambda j, be, nu, ge: jnp.minimum(j, (nu[0] - 1) // EXPERT_GROUP)
    hbm = pl.BlockSpec(memory_space=pl.ANY)
    gs = pltpu.PrefetchScalarGridSpec(
        num_scalar_prefetch=3,
        grid=(n_blocks // EXPERT_GROUP,),
        in_specs=[pl.BlockSpec((rows, PACKED), lambda j, be, nu, ge: (last(j, be, nu, ge), 0)), hbm, hbm, hbm],
        out_specs=pl.BlockSpec((rows, PACKED), lambda j, be, nu, ge: (last(j, be, nu, ge), 0)),
        scratch_shapes=[pltpu.VMEM((W_SLOTS, D_MODEL, D_EXPERT), F32),
                        pltpu.VMEM((W_SLOTS, D_MODEL, D_EXPERT), F32),
                        pltpu.VMEM((W_SLOTS, D_EXPERT, D_MODEL), F32),
                        pltpu.VMEM((D_MODEL, D_EXPERT), BF16),
                        pltpu.VMEM((D_MODEL, D_EXPERT), BF16),
                        pltpu.VMEM((D_EXPERT, D_MODEL), BF16),
                        pltpu.SMEM((1,), jnp.int32),
                        pltpu.SemaphoreType.DMA((W_SLOTS, 3))],
    )
    return pl.pallas_call(
        _expert_kernel,
        grid_spec=gs,
        out_shape=jax.ShapeDtypeStruct((cap, PACKED), U32),
        compiler_params=_cparams(1),
        name="experts",
    )(block_expert, n_used, group_expert, xs, w_gate, w_up, w_down)


CB_TM = RT_TM


def _combine_kernel(x1_ref, gt_ref, y2_ref, o_ref):
    g = gt_ref[...]
    lo1, hi1 = _unpack_rows(y2_ref[0, 0])
    lo2, hi2 = _unpack_rows(y2_ref[0, 1])
    o_ref[:, :PACKED] = x1_ref[:, :PACKED] + g[:, 0:1] * lo1 + g[:, 1:2] * lo2
    o_ref[:, PACKED:] = x1_ref[:, PACKED:] + g[:, 0:1] * hi1 + g[:, 1:2] * hi2


def _combine(x1, gates, y2):
    T = x1.shape[0]
    nt = T // CB_TM
    return pl.pallas_call(
        _combine_kernel,
        grid=(nt,),
        in_specs=[pl.BlockSpec((CB_TM, D_MODEL), lambda i: (i, 0)),
                  pl.BlockSpec((CB_TM, RINFO), lambda i: (i, 0)),
                  pl.BlockSpec((1, TOP_K, CB_TM, PACKED), lambda i: (i, 0, 0, 0))],
        out_specs=pl.BlockSpec((CB_TM, D_MODEL), lambda i: (i, 0)),
        out_shape=jax.ShapeDtypeStruct((T, D_MODEL), F32),
        compiler_params=_cparams(1),
        name="combine",
    )(x1, gates, y2)


def _layer(x, rel_bias, ln1, w_in, q_norm, k_norm, attn_sink, conv_w, conv_b, lru_wa, lru_ba, lru_wi, lru_bi,
           lru_lambda, out_norm_attn, out_norm_lru, w_out, ln2, w_group, b_group, w_er, b_er, w_gate, w_up, w_down):
    B, S, D = x.shape
    T = B * S
    x2 = x.reshape(T, D)
    q, kv, xr, gr = _in_proj(x2, ln1, w_in, q_norm, k_norm)
    bias_tab = _bias_table(rel_bias)
    attn_n = _attention(q.reshape(B, S, ATTN_WIDTH), kv.reshape(B, S, 2 * KV_WIDTH), bias_tab, attn_sink,
                        out_norm_attn)
    lru_n = _rglru(xr.reshape(B, S, LRU_WIDTH), gr.reshape(B, S, LRU_WIDTH), conv_w, conv_b,
                   lru_wa, lru_ba, lru_wi, lru_bi, lru_lambda, out_norm_lru)
    x1, h2, gates, ei, cnt = _out_route(attn_n.reshape(T, ATTN_WIDTH), lru_n.reshape(T, LRU_WIDTH), x2, w_out, ln2,
                                        w_group, b_group, w_er, b_er)
    cap = _moe_cap(T)
    dest, pstart, block_expert, n_used, group_expert = _layout(ei, cnt, cap // MOE_BLOCK)
    xs = _padfill(_sc_dispatch(h2, dest, cap), pstart, cnt)
    yb = _experts(xs, block_expert, n_used, group_expert, w_gate, w_up, w_down)
    out = _combine(x1, gates, _sc_gather(yb, dest))
    return out.reshape(B, S, D)


def kernel(x, rel_bias, ln1, w_in, q_norm, k_norm, attn_sink, conv_w, conv_b, lru_wa, lru_ba, lru_wi, lru_bi,
           lru_lambda, out_norm_attn, out_norm_lru, w_out, ln2, w_group, b_group, w_expert_router, b_expert_router,
           w_gate, w_up, w_down):
    depth = ln1.shape[0]
    for l in range(depth):
        x = _layer(x, rel_bias, ln1[l], w_in[l], q_norm[l], k_norm[l], attn_sink[l], conv_w[l], conv_b[l],
                   lru_wa[l], lru_ba[l], lru_wi[l], lru_bi[l], lru_lambda[l], out_norm_attn[l], out_norm_lru[l],
                   w_out[l], ln2[l], w_group[l], b_group[l], w_expert_router[l], b_expert_router[l],
                   w_gate[l], w_up[l], w_down[l])
    return x
```

```python
import functools
import math

import jax
import jax.numpy as jnp
import numpy as np
from jax import lax
from jax.experimental import pallas as pl
from jax.experimental.pallas import tpu as pltpu
from jax.experimental.pallas import tpu_sc as plsc

D_MODEL = 1024
N_HEADS = 8
N_KV_HEADS = 2
HEAD_DIM = 64
Q_PER_KV = N_HEADS // N_KV_HEADS
ATTN_WIDTH = N_HEADS * HEAD_DIM
KV_WIDTH = N_KV_HEADS * HEAD_DIM
WINDOW = 128
BLOCK = 128
NUM_BUCKETS = 32
MAX_DISTANCE = 128
LRU_WIDTH = D_MODEL - ATTN_WIDTH
LRU_BLOCKS = 8
LRU_BLOCK_DIM = LRU_WIDTH // LRU_BLOCKS
LRU_C = 8.0
CONV_W = 4
CONV_LEFT = 2
N_GROUPS = 4
EXPERTS_PER_GROUP = 8
N_EXPERTS = N_GROUPS * EXPERTS_PER_GROUP
TOP_K = 2
D_EXPERT = 512
MOE_BLOCK = 256
EPS = 1e-6
NEG_INF = -1e30

LANES = 128
SUBLANES = 8
VMEM_LIMIT = 56 * 1024 * 1024

F32 = jnp.float32
BF16 = jnp.bfloat16


def _cparams(n_axes, vmem=VMEM_LIMIT):
    return pltpu.CompilerParams(dimension_semantics=("arbitrary",) * n_axes, vmem_limit_bytes=vmem)


def _rms(x, gain):
    return x * lax.rsqrt(jnp.mean(x * x, axis=-1, keepdims=True) + EPS) * gain


U32 = jnp.uint32
HI_MASK = 0xFFFF0000
PACKED = D_MODEL // 2


def _pack_rows(x):
    h = x.shape[1] // 2
    lo = lax.bitcast_convert_type(x[:, :h].astype(BF16).astype(F32), U32) >> 16
    hi = lax.bitcast_convert_type(x[:, h:].astype(BF16).astype(F32), U32) & jnp.uint32(HI_MASK)
    return lo | hi


def _unpack_rows(p):
    lo = lax.bitcast_convert_type(p << 16, F32)
    hi = lax.bitcast_convert_type(p & jnp.uint32(HI_MASK), F32)
    return lo, hi


IN_TM = 1024


def _head_rms(x, n_heads, gain):
    head = lax.broadcasted_iota(jnp.int32, (1, n_heads * HEAD_DIM), 1) // HEAD_DIM
    x2 = x * x
    scale = jnp.zeros_like(x)
    for h in range(n_heads):
        ms = jnp.sum(jnp.where(head == h, x2, 0.0), axis=-1, keepdims=True) * (1.0 / HEAD_DIM)
        scale = jnp.where(head == h, lax.rsqrt(ms + EPS), scale)
    return x * scale * gain


def _in_proj_kernel(x_ref, g_ref, w_ref, qg_ref, kg_ref, q_ref, kv_ref, xr_ref, gr_ref, wb_ref):
    @pl.when(pl.program_id(0) == 0)
    def _():
        wb_ref[...] = w_ref[...].astype(BF16)

    h = _rms(x_ref[...], g_ref[...]).astype(BF16)
    c_k = ATTN_WIDTH
    c_v = c_k + KV_WIDTH
    c_x = c_v + KV_WIDTH
    c_g = c_x + LRU_WIDTH
    q = jnp.dot(h, wb_ref[:, :c_k], preferred_element_type=F32)
    q_ref[...] = _head_rms(q, N_HEADS, qg_ref[...]).astype(BF16)
    k = jnp.dot(h, wb_ref[:, c_k:c_v], preferred_element_type=F32)
    kv_ref[:, :KV_WIDTH] = _head_rms(k, N_KV_HEADS, kg_ref[...]).astype(BF16)
    kv_ref[:, KV_WIDTH:] = jnp.dot(h, wb_ref[:, c_v:c_x], preferred_element_type=F32).astype(BF16)
    xr_ref[...] = jnp.dot(h, wb_ref[:, c_x:c_g], preferred_element_type=F32)
    gr_ref[...] = jnp.dot(h, wb_ref[:, c_g:], preferred_element_type=F32)


def _in_proj(x2, ln1, w_in, q_gain, k_gain):
    T = x2.shape[0]
    n_in = w_in.shape[1]
    row = lambda w: pl.BlockSpec((IN_TM, w), lambda i: (i, 0))
    qg = (jnp.tile(q_gain.astype(F32), N_HEADS) * (HEAD_DIM ** -0.5)).reshape(1, ATTN_WIDTH)
    kg = jnp.tile(k_gain.astype(F32), N_KV_HEADS).reshape(1, KV_WIDTH)
    return pl.pallas_call(
        _in_proj_kernel,
        grid=(T // IN_TM,),
        in_specs=[row(D_MODEL),
                  pl.BlockSpec((1, D_MODEL), lambda i: (0, 0)),
                  pl.BlockSpec((D_MODEL, n_in), lambda i: (0, 0)),
                  pl.BlockSpec((1, ATTN_WIDTH), lambda i: (0, 0)),
                  pl.BlockSpec((1, KV_WIDTH), lambda i: (0, 0))],
        out_specs=[row(ATTN_WIDTH), row(2 * KV_WIDTH), row(LRU_WIDTH), row(LRU_WIDTH)],
        out_shape=[jax.ShapeDtypeStruct((T, ATTN_WIDTH), BF16),
                   jax.ShapeDtypeStruct((T, 2 * KV_WIDTH), BF16),
                   jax.ShapeDtypeStruct((T, LRU_WIDTH), F32),
                   jax.ShapeDtypeStruct((T, LRU_WIDTH), F32)],
        scratch_shapes=[pltpu.VMEM((D_MODEL, n_in), BF16)],
        compiler_params=_cparams(1),
        name="in_proj",
    )(x2, ln1.reshape(1, D_MODEL), w_in, qg, kg)


def _t5_bucket(rel):
    half = NUM_BUCKETS // 2
    max_exact = half // 2
    base = jnp.where(rel > 0, half, 0)
    n = jnp.abs(rel)
    nf = jnp.maximum(n, 1).astype(jnp.float32)
    large = max_exact + (jnp.log(nf / max_exact) / math.log(MAX_DISTANCE / max_exact)
                         * (half - max_exact)).astype(jnp.int32)
    large = jnp.minimum(large, half - 1)
    return base + jnp.where(n < max_exact, n, large)


HEAD_PAIRS = Q_PER_KV // 2
EDGE_VARIANTS = 3


def _bias_kernel(rb_ref, bucket_ref, band_ref, o_ref):
    bucket = bucket_ref[...]
    band = band_ref[...] > 0
    col = lax.broadcasted_iota(jnp.int32, bucket.shape, 1)
    valid = (band & (col >= BLOCK), band, band & (col < 2 * BLOCK))
    for h in range(N_HEADS):
        acc = jnp.zeros(bucket.shape, F32)
        for b in range(NUM_BUCKETS):
            acc = jnp.where(bucket == b, rb_ref[b, h], acc)
        kv, g = divmod(h, Q_PER_KV)
        pair, parity = divmod(g, 2)
        for var in range(EDGE_VARIANTS):
            o_ref[var, kv, parity, pair * BLOCK:(pair + 1) * BLOCK, :] = jnp.where(valid[var], acc, NEG_INF)


def _bias_table(rel_bias):
    qi = jnp.arange(BLOCK, dtype=jnp.int32)
    kj = jnp.arange(3 * BLOCK, dtype=jnp.int32)
    rel = kj[None, :] - BLOCK - qi[:, None]
    bucket = _t5_bucket(rel).astype(jnp.int32)
    band = (jnp.abs(rel) <= WINDOW).astype(jnp.int32)
    return pl.pallas_call(
        _bias_kernel,
        in_specs=[pl.BlockSpec(memory_space=pltpu.SMEM),
                  pl.BlockSpec(memory_space=pltpu.VMEM),
                  pl.BlockSpec(memory_space=pltpu.VMEM)],
        out_specs=pl.BlockSpec(memory_space=pltpu.VMEM),
        out_shape=jax.ShapeDtypeStruct((EDGE_VARIANTS, N_KV_HEADS, 2, HEAD_PAIRS * BLOCK, 3 * BLOCK), F32),
        name="bias_table",
    )(rel_bias.astype(F32), bucket, band)


def _attn_kernel(sink_ref, q_ref, kp_ref, kc_ref, kn_ref, bias_ref, og_ref, o_ref):
    n = pl.program_id(1)
    kv_all = jnp.concatenate([kp_ref[0], kc_ref[0], kn_ref[0]], axis=0)
    for qb in range(ATTN_QB):
        variant = 1
        if qb == 0:
            variant = jnp.where(n == 0, 0, 1)
        if qb == ATTN_QB - 1:
            variant = jnp.where(n == pl.num_programs(1) - 1, 2, variant)
        out = _attn_block(q_ref[0, qb * BLOCK:(qb + 1) * BLOCK, :], kv_all[qb * BLOCK:(qb + 3) * BLOCK, :],
                          lambda kv, parity: bias_ref[variant, kv, parity], sink_ref)
        o_ref[0, qb * BLOCK:(qb + 1) * BLOCK, :] = _rms(out, og_ref[...]).astype(o_ref.dtype)


def _attn_block(q, kvw, bias, sink_ref):
    low = lax.broadcasted_iota(jnp.int32, (1, LANES), 1) < HEAD_DIM
    rowi = lax.broadcasted_iota(jnp.int32, (HEAD_PAIRS * BLOCK, 1), 0)
    swap = lambda slab: pltpu.roll(slab.astype(F32), HEAD_DIM, 1).astype(BF16)
    kslab, vslab = kvw[:, :KV_WIDTH], kvw[:, KV_WIDTH:]
    kslab_sw, vslab_sw = swap(kslab), swap(vslab)
    outs = []
    for kv in range(N_KV_HEADS):
        src = ((kslab, vslab), (kslab_sw, vslab_sw)) if kv == 0 else ((kslab_sw, vslab_sw), (kslab, vslab))
        base = kv * Q_PER_KV * HEAD_DIM
        qpair = jnp.concatenate([q[:, base + j * LANES:base + (j + 1) * LANES] for j in range(HEAD_PAIRS)], axis=0)
        acc = None
        for parity in range(2):
            keep = low if parity == 0 else jnp.logical_not(low)
            kz = jnp.where(keep, src[parity][0], jnp.zeros_like(kslab))
            vz = jnp.where(keep, src[parity][1], jnp.zeros_like(vslab))
            sink = jnp.zeros((HEAD_PAIRS * BLOCK, 1), F32)
            for j in range(HEAD_PAIRS):
                sink = jnp.where(rowi // BLOCK == j, sink_ref[kv * Q_PER_KV + 2 * j + parity], sink)
            s = lax.dot_general(qpair, kz, (((1,), (1,)), ((), ())), preferred_element_type=F32)
            s = s + bias(kv, parity)
            m = jnp.maximum(jnp.max(s, axis=-1, keepdims=True), sink)
            p = jnp.exp(s - m)
            denom = jnp.sum(p, axis=-1, keepdims=True) + jnp.exp(sink - m)
            o = jnp.dot(p.astype(BF16), vz, preferred_element_type=F32) * (1.0 / denom)
            acc = o if acc is None else acc + o
        outs += [acc[j * BLOCK:(j + 1) * BLOCK, :] for j in range(HEAD_PAIRS)]
    return jnp.concatenate(outs, axis=1)


ATTN_QB = 4


def _attention(q, kv, bias_tab, sink, out_gain):
    B, S, _ = q.shape
    nb = S // BLOCK
    assert ATTN_QB >= 2 and nb % ATTN_QB == 0, "a step's first and last query blocks must be distinct"
    ns = nb // ATTN_QB
    rows = ATTN_QB * BLOCK
    kvspec = lambda f: pl.BlockSpec((1, BLOCK, 2 * KV_WIDTH), f)
    return pl.pallas_call(
        _attn_kernel,
        grid=(B, ns),
        in_specs=[pl.BlockSpec(memory_space=pltpu.SMEM),
                  pl.BlockSpec((1, rows, ATTN_WIDTH), lambda b, n: (b, n, 0)),
                  kvspec(lambda b, n: (b, jnp.maximum(n * ATTN_QB - 1, 0), 0)),
                  pl.BlockSpec((1, rows, 2 * KV_WIDTH), lambda b, n: (b, n, 0)),
                  kvspec(lambda b, n: (b, jnp.minimum((n + 1) * ATTN_QB, nb - 1), 0)),
                  pl.BlockSpec((EDGE_VARIANTS, N_KV_HEADS, 2, HEAD_PAIRS * BLOCK, 3 * BLOCK),
                               lambda b, n: (0, 0, 0, 0, 0)),
                  pl.BlockSpec((1, ATTN_WIDTH), lambda b, n: (0, 0))],
        out_specs=pl.BlockSpec((1, rows, ATTN_WIDTH), lambda b, n: (b, n, 0)),
        out_shape=jax.ShapeDtypeStruct((B, S, ATTN_WIDTH), BF16),
        compiler_params=_cparams(2),
        name="attention",
    )(sink.astype(F32), q, kv, kv, kv, bias_tab, out_gain.reshape(1, ATTN_WIDTH))


LRU_TC = 128
LRU_PITCH = LRU_TC + SUBLANES
LRU_SLABS = LRU_WIDTH // LANES
HALO = SUBLANES


def _softplus(x):
    return jnp.maximum(x, 0.0) + jnp.log(1.0 + jnp.exp(-jnp.abs(x)))


def _gelu_tanh(x):
    k = math.sqrt(2.0 / math.pi)
    hx = 0.5 * x
    return hx + hx * jnp.tanh(x * (k + (k * 0.044715) * (x * x)))


def _sigmoid(x):
    return 0.5 + 0.5 * jnp.tanh(0.5 * x)


def _rglru_kernel(xr_ref, xp_ref, xn_ref, gr_ref, cw_ref, cb_ref, wg_ref, bg_ref, lam_ref, og_ref,
                  o_ref, sx_ref, a_ref, u_ref, h_ref, carry_ref, hf_ref):
    p = pl.program_id(0)
    i = pl.program_id(1)
    nc = pl.num_programs(1)
    c = i + p * (nc - 1 - 2 * i)
    B = xr_ref.shape[0]
    TC = LRU_TC

    sx_ref[:, HALO:HALO + TC, :] = xr_ref[...]
    sx_ref[:, 0:HALO, :] = jnp.where(c > 0, xp_ref[...], 0.0)
    sx_ref[:, HALO + TC:, :] = jnp.where(c < nc - 1, xn_ref[...], 0.0)
    xc = cb_ref[...][None]
    for j in range(CONV_W):
        off = HALO + j - CONV_LEFT
        xc = xc + cw_ref[j:j + 1, :][None] * sx_ref[:, off:off + TC, :]
    xc2 = xc.reshape(B * TC, LRU_WIDTH)

    g = jnp.dot(xc2.astype(BF16), wg_ref[0], preferred_element_type=F32) + bg_ref[0]
    r = _sigmoid(g[:, :LRU_WIDTH])
    ig = _sigmoid(g[:, LRU_WIDTH:])
    a = jnp.exp(r * (-LRU_C * _softplus(-lam_ref[0])))
    z = 1.0 - a * a
    u = z * lax.rsqrt(jnp.maximum(z, 1e-30)) * ig * xc2
    for b in range(B):
        for s in range(LRU_SLABS):
            a_ref[s, b * LRU_PITCH:b * LRU_PITCH + TC, :] = a[b * TC:(b + 1) * TC, s * LANES:(s + 1) * LANES]
            u_ref[s, b * LRU_PITCH:b * LRU_PITCH + TC, :] = u[b * TC:(b + 1) * TC, s * LANES:(s + 1) * LANES]

    @pl.when(i == 0)
    def _():
        carry_ref[...] = jnp.zeros_like(carry_ref)

    def step(k, hs):
        t = k + p * (TC - 1 - 2 * k)
        out = []
        for s in range(LRU_SLABS):
            idx = pl.ds(t, B, stride=LRU_PITCH)
            hn = a_ref[s, idx, :] * hs[s] + u_ref[s, idx, :]
            h_ref[s, idx, :] = hn
            out.append(hn)
        return tuple(out)

    hs = lax.fori_loop(0, TC, step, tuple(carry_ref[s] for s in range(LRU_SLABS)), unroll=8)
    for s in range(LRU_SLABS):
        carry_ref[s] = hs[s]

    @pl.when(p == 0)
    def _():
        for b in range(B):
            for s in range(LRU_SLABS):
                hf_ref[c, s, b * TC:(b + 1) * TC, :] = h_ref[s, b * LRU_PITCH:b * LRU_PITCH + TC, :].astype(hf_ref.dtype)

    @pl.when(p == 1)
    def _():
        for b in range(B):
            hsum = jnp.concatenate(
                [h_ref[s, b * LRU_PITCH:b * LRU_PITCH + TC, :] + hf_ref[c, s, b * TC:(b + 1) * TC, :].astype(F32)
                 for s in range(LRU_SLABS)], axis=1)
            y = hsum * _gelu_tanh(gr_ref[b])
            o_ref[b] = _rms(y, og_ref[...]).astype(o_ref.dtype)


def _block_diag(w):
    eye = jnp.eye(LRU_BLOCKS, dtype=w.dtype)
    return jnp.einsum('hij,hg->higj', w, eye).reshape(LRU_WIDTH, LRU_WIDTH)


def _rglru(xr, gr, conv_w, conv_b, w_a, b_a, w_i, b_i, lam, out_gain):
    B, S, W = xr.shape
    nc = S // LRU_TC
    hb = LRU_TC // HALO
    wg = jnp.stack([jnp.concatenate([_block_diag(w_a[d]), _block_diag(w_i[d])], axis=1) for d in range(2)]).astype(BF16)
    bg = jnp.concatenate([b_a, b_i], axis=-1).reshape(2, 1, 2 * W).astype(F32)
    chunk = lambda p, i: i + p * (nc - 1 - 2 * i)
    full2 = lambda shape: pl.BlockSpec(shape, lambda p, i: (0,) * len(shape))
    return pl.pallas_call(
        _rglru_kernel,
        grid=(2, nc),
        in_specs=[pl.BlockSpec((B, LRU_TC, W), lambda p, i: (0, chunk(p, i), 0)),
                  pl.BlockSpec((B, HALO, W), lambda p, i: (0, jnp.maximum(chunk(p, i) * hb - 1, 0), 0)),
                  pl.BlockSpec((B, HALO, W), lambda p, i: (0, jnp.minimum((chunk(p, i) + 1) * hb, S // HALO - 1), 0)),
                  pl.BlockSpec((B, LRU_TC, W), lambda p, i: (0, chunk(p, i), 0)),
                  full2((CONV_W, W)),
                  full2((1, W)),
                  pl.BlockSpec((1, W, 2 * W), lambda p, i: (p, 0, 0)),
                  pl.BlockSpec((1, 1, 2 * W), lambda p, i: (p, 0, 0)),
                  pl.BlockSpec((1, 1, W), lambda p, i: (p, 0, 0)),
                  full2((1, W))],
        out_specs=pl.BlockSpec((B, LRU_TC, W), lambda p, i: (0, nc - 1 - p * i, 0)),
        out_shape=jax.ShapeDtypeStruct((B, S, W), BF16),
        scratch_shapes=[pltpu.VMEM((B, LRU_TC + 2 * HALO, W), F32),
                        pltpu.VMEM((LRU_SLABS, B * LRU_PITCH, LANES), F32),
                        pltpu.VMEM((LRU_SLABS, B * LRU_PITCH, LANES), F32),
                        pltpu.VMEM((LRU_SLABS, B * LRU_PITCH, LANES), F32),
                        pltpu.VMEM((LRU_SLABS, B, LANES), F32),
                        pltpu.VMEM((nc, LRU_SLABS, B * LRU_TC, LANES), BF16)],
        compiler_params=_cparams(2),
        name="rglru",
    )(xr, xr, xr, gr, conv_w.astype(F32), conv_b.reshape(1, W).astype(F32), wg, bg,
      lam.reshape(2, 1, W).astype(F32), out_gain.reshape(1, W).astype(F32))


RT_TM = 512
RT_COLS = LANES
RT_ROWS = 48
RINFO = SUBLANES


def _split_bf16(x):
    hi = x.astype(BF16)
    lo = (x - hi.astype(F32)).astype(BF16)
    return hi, lo


def _route_kernel(an_ref, ln_ref, x_ref, wo_ref, g2_ref, wr_ref, br_ref,
                  x1_ref, h2_ref, gt_ref, ei_ref, cnt_ref, wob_ref, wrb_ref, tri_ref, run_ref, runc_ref):
    @pl.when(pl.program_id(0) == 0)
    def _():
        wob_ref[...] = wo_ref[...].astype(BF16)
        hi, lo = _split_bf16(wr_ref[...])
        wrb_ref[:RT_ROWS, :] = hi
        wrb_ref[RT_ROWS:, :] = lo
        r = lax.broadcasted_iota(jnp.int32, (RT_TM, RT_TM), 0)
        cidx = lax.broadcasted_iota(jnp.int32, (RT_TM, RT_TM), 1)
        tri_ref[...] = (r < cidx).astype(BF16)
        run_ref[...] = jnp.zeros_like(run_ref)
        runc_ref[...] = jnp.zeros_like(runc_ref)

    x1 = (x_ref[...]
          + jnp.dot(an_ref[...], wob_ref[:ATTN_WIDTH, :], preferred_element_type=F32)
          + jnp.dot(ln_ref[...], wob_ref[ATTN_WIDTH:, :], preferred_element_type=F32))
    x1_ref[...] = x1
    h2 = _rms(x1, g2_ref[...])
    h2_ref[...] = _pack_rows(h2)

    hi, lo = _split_bf16(h2)
    nt_dims = (((1,), (1,)), ((), ()))
    t1 = lax.dot_general(wrb_ref[...], hi, nt_dims, preferred_element_type=F32)
    t2 = lax.dot_general(wrb_ref[:RT_ROWS, :], lo, nt_dims, preferred_element_type=F32)
    logit = t1[:RT_ROWS] + t1[RT_ROWS:] + t2 + br_ref[...]

    sub = lax.broadcasted_iota(jnp.int32, (SUBLANES, RT_TM), 0)
    first_min = lambda hit: jnp.min(jnp.where(hit, sub, SUBLANES), axis=0, keepdims=True)
    is_g = sub < N_GROUPS
    gl = jnp.where(is_g, logit[:SUBLANES], -jnp.inf)
    gm = jnp.max(gl, axis=0, keepdims=True)
    gidx = first_min(gl == gm)
    g_p = 1.0 / jnp.sum(jnp.where(is_g, jnp.exp(logit[:SUBLANES] - gm), 0.0), axis=0, keepdims=True)
    el = logit[SUBLANES:2 * SUBLANES]
    for g in range(1, N_GROUPS):
        el = jnp.where(gidx == g, logit[(g + 1) * SUBLANES:(g + 2) * SUBLANES], el)
    m1 = jnp.max(el, axis=0, keepdims=True)
    i1 = first_min(el == m1)
    el2 = jnp.where(sub == i1, -jnp.inf, el)
    m2 = jnp.max(el2, axis=0, keepdims=True)
    i2 = first_min(el2 == m2)
    t = jnp.exp(m2 - m1)
    gate1 = g_p / (1.0 + t)
    gate2 = g_p * t / (1.0 + t)
    e1 = gidx * EXPERTS_PER_GROUP + i1
    e2 = gidx * EXPERTS_PER_GROUP + i2

    erow = lax.broadcasted_iota(jnp.int32, (N_EXPERTS, RT_TM), 0)
    oh1 = erow == e1
    oh2 = erow == e2
    oh = (oh1 | oh2).astype(F32)
    ohb = oh.astype(BF16)
    cum = jnp.dot(ohb, tri_ref[...], preferred_element_type=F32) + runc_ref[...]
    rank1 = jnp.sum(jnp.where(oh1, cum, 0.0), axis=0, keepdims=True)
    rank2 = jnp.sum(jnp.where(oh2, cum, 0.0), axis=0, keepdims=True)
    runc_ref[...] = runc_ref[...] + jnp.sum(oh, axis=1, keepdims=True)
    tile_cnt = lax.dot_general(jnp.ones((SUBLANES, RT_TM), BF16), ohb, nt_dims, preferred_element_type=F32)
    run_ref[:, :N_EXPERTS] = run_ref[:, :N_EXPERTS] + tile_cnt[0:1]
    cnt_ref[...] = run_ref[...].astype(jnp.int32)

    rows = [e1, e2, rank1.astype(jnp.int32), rank2.astype(jnp.int32)]
    ei = jnp.zeros((RINFO, RT_TM), jnp.int32)
    for k, v in enumerate(rows):
        ei = jnp.where(sub == k, v, ei)
    ei_ref[0] = ei
    gt_ref[...] = jnp.where(sub == 0, gate1, jnp.where(sub == 1, gate2, 0.0)).T


def _out_route(attn_n, lru_n, x2, w_out, ln2, w_group, b_group, w_er, b_er):
    T = x2.shape[0]
    pad_g = SUBLANES - N_GROUPS
    wr = jnp.concatenate([jnp.pad(w_group.T, ((0, pad_g), (0, 0))),
                          jnp.transpose(w_er, (0, 2, 1)).reshape(N_EXPERTS, D_MODEL)], axis=0)
    wr = jnp.pad(wr, ((0, RT_ROWS - wr.shape[0]), (0, 0))).astype(F32)
    br = jnp.concatenate([jnp.pad(b_group, (0, pad_g)), b_er.reshape(-1)])
    br = jnp.pad(br, (0, RT_ROWS - br.shape[0])).reshape(RT_ROWS, 1).astype(F32)
    row = lambda w: pl.BlockSpec((RT_TM, w), lambda i: (i, 0))
    const = lambda shape: pl.BlockSpec(shape, lambda i: (0, 0))
    return pl.pallas_call(
        _route_kernel,
        grid=(T // RT_TM,),
        in_specs=[row(ATTN_WIDTH), row(LRU_WIDTH), row(D_MODEL), const((D_MODEL, D_MODEL)), const((1, D_MODEL)),
                  const((RT_ROWS, D_MODEL)), const((RT_ROWS, 1))],
        out_specs=[row(D_MODEL), row(PACKED), row(RINFO),
                   pl.BlockSpec((1, RINFO, RT_TM), lambda i: (i, 0, 0)), const((1, RT_COLS))],
        out_shape=[jax.ShapeDtypeStruct((T, D_MODEL), F32),
                   jax.ShapeDtypeStruct((T, PACKED), U32),
                   jax.ShapeDtypeStruct((T, RINFO), F32),
                   jax.ShapeDtypeStruct((T // RT_TM, RINFO, RT_TM), jnp.int32),
                   jax.ShapeDtypeStruct((1, RT_COLS), jnp.int32)],
        scratch_shapes=[pltpu.VMEM((D_MODEL, D_MODEL), BF16),
                        pltpu.VMEM((2 * RT_ROWS, D_MODEL), BF16),
                        pltpu.VMEM((RT_TM, RT_TM), BF16),
                        pltpu.VMEM((1, RT_COLS), F32),
                        pltpu.VMEM((N_EXPERTS, 1), F32)],
        compiler_params=_cparams(1),
        name="out_route",
    )(attn_n, lru_n, x2, w_out, ln2.reshape(1, D_MODEL).astype(F32), wr, br)


def _moe_cap(T):
    A = T * TOP_K
    return ((A + MOE_BLOCK - 1) // MOE_BLOCK) * MOE_BLOCK + N_EXPERTS * MOE_BLOCK


PAD_BITS = tuple(1 << b for b in reversed(range(3, MOE_BLOCK.bit_length() - 1)))


def _layout_kernel(cnt_ref, ei_ref, dest_ref, pstart, be_ref, nu_ref, ge_ref):
    n_blocks = be_ref.shape[0]

    def lay(e, carry):
        start, blk, grp = carry
        pstart[e] = start
        nb = (cnt_ref[0, e] + MOE_BLOCK - 1) // MOE_BLOCK
        ge_ref[grp] = e

        def fill(k, c):
            be_ref[blk + k] = e
            return c
        lax.fori_loop(0, nb, fill, 0)
        return start + nb * MOE_BLOCK, blk + nb, grp + (nb > 0).astype(jnp.int32)
    _, used, groups = lax.fori_loop(0, N_EXPERTS, lay, (jnp.int32(0), jnp.int32(0), jnp.int32(0)))
    nu_ref[0] = used

    def tail(k, c):
        be_ref[k] = N_EXPERTS - 1
        return c
    lax.fori_loop(used, n_blocks, tail, 0)

    def no_group(k, c):
        ge_ref[k] = -1
        return c
    lax.fori_loop(groups, N_EXPERTS + 1, no_group, 0)

    expert = ei_ref[:, 0:TOP_K, :]
    dest = ei_ref[:, TOP_K:2 * TOP_K, :]
    for e in range(N_EXPERTS):
        dest = dest + jnp.where(expert == e, pstart[e], 0)
    dest_ref[...] = dest


def _layout(ei, cnt, n_blocks):
    nt = ei.shape[0]
    smem = pl.BlockSpec(memory_space=pltpu.SMEM)
    vmem = pl.BlockSpec(memory_space=pltpu.VMEM)
    return pl.pallas_call(
        _layout_kernel,
        in_specs=[smem, vmem],
        out_specs=[vmem, smem, smem, smem, smem],
        out_shape=[jax.ShapeDtypeStruct((nt, TOP_K, RT_TM), jnp.int32),
                   jax.ShapeDtypeStruct((N_EXPERTS,), jnp.int32),
                   jax.ShapeDtypeStruct((n_blocks,), jnp.int32),
                   jax.ShapeDtypeStruct((1,), jnp.int32),
                   jax.ShapeDtypeStruct((N_EXPERTS + 1,), jnp.int32)],
        name="layout",
    )(cnt, ei)


SC_CHUNK = 64
SC_BUFS = 3
SC_LEAD = SC_BUFS - 1


def _sc_workers():
    info = plsc.get_sparse_core_info()
    return info.num_cores, info.num_subcores


def _sc_ring(n_chunks, read, write):
    for c in range(min(SC_LEAD, n_chunks)):
        for cp in read(c):
            cp.start()
    reclaimed = set()
    for c in range(n_chunks):
        for cp in read(c):
            cp.wait()
        for cp in write(c):
            cp.start()
        nxt = c + SC_LEAD
        if nxt < n_chunks:
            if nxt - SC_BUFS >= 0:
                for cp in write(nxt - SC_BUFS):
                    cp.wait()
                reclaimed.add(nxt - SC_BUFS)
            for cp in read(nxt):
                cp.start()
    for c in range(n_chunks):
        if c not in reclaimed:
            for cp in write(c):
                cp.wait()


def _sc_dispatch(h2p, dest, cap):
    T = h2p.shape[0]
    nc, ns = _sc_workers()
    per_w = T // (nc * ns)
    n_ch = per_w // SC_CHUNK
    assert dest.shape == (nc * ns, TOP_K, per_w) and per_w % SC_CHUNK == 0
    idx = dest.reshape(nc * ns, TOP_K * n_ch, SC_CHUNK)
    mesh = plsc.VectorSubcoreMesh(core_axis_name="c", subcore_axis_name="s")

    @functools.partial(
        pl.kernel, mesh=mesh,
        out_type=jax.ShapeDtypeStruct((cap, PACKED), U32),
        scratch_types=[pltpu.VMEM((TOP_K * n_ch, SC_CHUNK), jnp.int32),
                       pltpu.VMEM((SC_BUFS, SC_CHUNK, PACKED), U32),
                       pltpu.SemaphoreType.DMA((SC_BUFS,)),
                       pltpu.SemaphoreType.DMA((SC_BUFS,))])
    def scatter(src_hbm, idx_hbm, out_hbm, idx_v, rows_v, rsem, wsem):
        wid = lax.axis_index("s") * nc + lax.axis_index("c")
        base = pl.multiple_of(wid * per_w, per_w)
        pltpu.sync_copy(idx_hbm.at[wid], idx_v)

        def read(c):
            b = c % SC_BUFS
            return [pltpu.make_async_copy(src_hbm.at[pl.ds(base + c * SC_CHUNK, SC_CHUNK)], rows_v.at[b], rsem.at[b])]

        def write(c):
            b = c % SC_BUFS
            return [pltpu.make_async_copy(rows_v.at[b], out_hbm.at[idx_v.at[k * n_ch + c]], wsem.at[b])
                    for k in range(TOP_K)]
        _sc_ring(n_ch, read, write)

    return scatter(h2p, idx)


def _sc_gather(yb, dest):
    nt, _, tm = dest.shape
    nc, ns = _sc_workers()
    n_rows = nt * TOP_K * tm
    per_w = n_rows // (nc * ns)
    n_ch = per_w // SC_CHUNK
    assert per_w * nc * ns == n_rows and per_w % SC_CHUNK == 0
    mesh = plsc.VectorSubcoreMesh(core_axis_name="c", subcore_axis_name="s")

    @functools.partial(
        pl.kernel, mesh=mesh,
        out_type=jax.ShapeDtypeStruct((n_rows, PACKED), U32),
        scratch_types=[pltpu.VMEM((per_w,), jnp.int32),
                       pltpu.VMEM((SC_BUFS, SC_CHUNK, PACKED), U32),
                       pltpu.SemaphoreType.DMA((SC_BUFS,)),
                       pltpu.SemaphoreType.DMA((SC_BUFS,))])
    def gather(table_hbm, idx_hbm, out_hbm, idx_v, rows_v, gsem, wsem):
        wid = lax.axis_index("s") * nc + lax.axis_index("c")
        base = pl.multiple_of(wid * per_w, per_w)
        pltpu.sync_copy(idx_hbm.at[pl.ds(base, per_w)], idx_v)

        def read(c):
            b = c % SC_BUFS
            return [pltpu.make_async_copy(table_hbm.at[idx_v.at[pl.ds(c * SC_CHUNK, SC_CHUNK)]], rows_v.at[b], gsem.at[b])]

        def write(c):
            b = c % SC_BUFS
            return [pltpu.make_async_copy(rows_v.at[b], out_hbm.at[pl.ds(base + c * SC_CHUNK, SC_CHUNK)], wsem.at[b])]
        _sc_ring(n_ch, read, write)

    return gather(yb, dest.reshape(n_rows)).reshape(nt, TOP_K, tm, PACKED)


def _padfill_kernel(cnt_ref, pstart, xs_in, xs_ref, zeros, zsem):
    del xs_in

    def pad_copies(fn):
        for e in range(N_EXPERTS):
            cnt = cnt_ref[0, e]
            head = (-cnt) & (SUBLANES - 1)
            rest = ((-cnt) & (MOE_BLOCK - 1)) - head
            off = pstart[e] + cnt
            for k in range(SUBLANES - 1):
                @pl.when(k < head)
                def _(off=off, k=k):
                    fn(pltpu.make_async_copy(zeros.at[pl.ds(0, 1), :], xs_ref.at[pl.ds(off + k, 1), :], zsem))
            off = off + head
            for bit in PAD_BITS:
                @pl.when((rest & bit) != 0)
                def _(off=off, bit=bit):
                    fn(pltpu.make_async_copy(zeros.at[pl.ds(0, bit), :],
                                             xs_ref.at[pl.ds(pl.multiple_of(off, SUBLANES), bit), :], zsem))
                off = off + (rest & bit)

    zeros[...] = jnp.zeros_like(zeros)
    pad_copies(lambda cp: cp.start())
    pad_copies(lambda cp: cp.wait())


def _padfill(xs, pstart, cnt):
    smem = pl.BlockSpec(memory_space=pltpu.SMEM)
    hbm = pl.BlockSpec(memory_space=pl.ANY)
    return pl.pallas_call(
        _padfill_kernel,
        in_specs=[smem, smem, hbm],
        out_specs=hbm,
        out_shape=jax.ShapeDtypeStruct(xs.shape, xs.dtype),
        input_output_aliases={2: 0},
        scratch_shapes=[pltpu.VMEM((MOE_BLOCK // 2, PACKED), U32), pltpu.SemaphoreType.DMA(())],
        name="padfill",
    )(cnt, pstart, xs)


W_SLOTS = 2
EXPERT_GROUP = 8
EXPERT_RUNS = (1, 2, 4)


def _expert_kernel(be_ref, nu_ref, ge_ref, x_ref, wg_hbm, wu_hbm, wd_hbm, o_ref,
                   wgf, wuf, wdf, wgb, wub, wdb, grp_ref, sems):
    step = pl.program_id(0)

    def weight_copies(e, slot):
        return (pltpu.make_async_copy(wg_hbm.at[e], wgf.at[slot], sems.at[slot, 0]),
                pltpu.make_async_copy(wu_hbm.at[e], wuf.at[slot], sems.at[slot, 1]),
                pltpu.make_async_copy(wd_hbm.at[e], wdf.at[slot], sems.at[slot, 2]))

    @pl.when(step == 0)
    def _():
        grp_ref[0] = 0
        for cp in weight_copies(ge_ref[0], 0):
            cp.start()

    n_blocks = be_ref.shape[0]
    n_used = nu_ref[0]

    def swiglu(s, n):
        rows = pl.ds(pl.multiple_of(s * MOE_BLOCK, MOE_BLOCK), n * MOE_BLOCK)
        lo, hi = _unpack_rows(x_ref[rows, :])
        lo = lo.astype(BF16)
        hi = hi.astype(BF16)
        g = (jnp.dot(lo, wgb[:PACKED, :], preferred_element_type=F32)
             + jnp.dot(hi, wgb[PACKED:, :], preferred_element_type=F32))
        u = (jnp.dot(lo, wub[:PACKED, :], preferred_element_type=F32)
             + jnp.dot(hi, wub[PACKED:, :], preferred_element_type=F32))
        h = (g * _sigmoid(g) * u).astype(BF16)
        o_ref[rows, :] = _pack_rows(jnp.dot(h, wdb[...], preferred_element_type=F32))

    def run(s):
        j = step * EXPERT_GROUP + s
        e = be_ref[j]
        first = jnp.logical_or(j == 0, e != be_ref[jnp.maximum(j - 1, 0)])

        @pl.when(first)
        def _():
            grp = grp_ref[0]
            slot = grp % W_SLOTS
            for cp in weight_copies(e, slot):
                cp.wait()
            wgb[...] = wgf[slot].astype(BF16)
            wub[...] = wuf[slot].astype(BF16)
            wdb[...] = wdf[slot].astype(BF16)
            nxt = ge_ref[grp + 1]

            @pl.when(nxt >= 0)
            def _():
                for cp in weight_copies(nxt, 1 - slot):
                    cp.start()
            grp_ref[0] = grp + 1

        def same(k):
            return (s + k < EXPERT_GROUP) & (j + k < n_used) & (be_ref[jnp.minimum(j + k, n_blocks - 1)] == e)
        take = jnp.int32(1)
        for n in EXPERT_RUNS[1:]:
            ok = same(n - 1)
            for k in range(1, n - 1):
                ok = ok & same(k)
            take = jnp.where(ok, n, take)
        for n in EXPERT_RUNS:
            @pl.when(take == n)
            def _(n=n):
                swiglu(s, n)
        return s + take

    lax.while_loop(lambda s: (s < EXPERT_GROUP) & (step * EXPERT_GROUP + s < n_used), run, jnp.int32(0))


def _experts(xs, block_expert, n_used, group_expert, w_gate, w_up, w_down):
    cap = xs.shape[0]
    n_blocks = cap // MOE_BLOCK
    assert n_blocks % EXPERT_GROUP == 0
    rows = EXPERT_GROUP * MOE_BLOCK
    last = lambda j, be, nu, ge: jnp.minimum(j, (nu[0] - 1) // EXPERT_GROUP)
    hbm = pl.BlockSpec(memory_space=pl.ANY)
    gs = pltpu.PrefetchScalarGridSpec(
        num_scalar_prefetch=3,
        grid=(n_blocks // EXPERT_GROUP,),
        in_specs=[pl.BlockSpec((rows, PACKED), lambda j, be, nu, ge: (last(j, be, nu, ge), 0)), hbm, hbm, hbm],
        out_specs=pl.BlockSpec((rows, PACKED), lambda j, be, nu, ge: (last(j, be, nu, ge), 0)),
        scratch_shapes=[pltpu.VMEM((W_SLOTS, D_MODEL, D_EXPERT), F32),
                        pltpu.VMEM((W_SLOTS, D_MODEL, D_EXPERT), F32),
                        pltpu.VMEM((W_SLOTS, D_EXPERT, D_MODEL), F32),
                        pltpu.VMEM((D_MODEL, D_EXPERT), BF16),
                        pltpu.VMEM((D_MODEL, D_EXPERT), BF16),
                        pltpu.VMEM((D_EXPERT, D_MODEL), BF16),
                        pltpu.SMEM((1,), jnp.int32),
                        pltpu.SemaphoreType.DMA((W_SLOTS, 3))],
    )
    return pl.pallas_call(
        _expert_kernel,
        grid_spec=gs,
        out_shape=jax.ShapeDtypeStruct((cap, PACKED), U32),
        compiler_params=_cparams(1),
        name="experts",
    )(block_expert, n_used, group_expert, xs, w_gate, w_up, w_down)


CB_TM = RT_TM


def _combine_kernel(x1_ref, gt_ref, y2_ref, o_ref):
    g = gt_ref[...]
    lo1, hi1 = _unpack_rows(y2_ref[0, 0])
    lo2, hi2 = _unpack_rows(y2_ref[0, 1])
    o_ref[:, :PACKED] = x1_ref[:, :PACKED] + g[:, 0:1] * lo1 + g[:, 1:2] * lo2
    o_ref[:, PACKED:] = x1_ref[:, PACKED:] + g[:, 0:1] * hi1 + g[:, 1:2] * hi2


def _combine(x1, gates, y2):
    T = x1.shape[0]
    nt = T // CB_TM
    return pl.pallas_call(
        _combine_kernel,
        grid=(nt,),
        in_specs=[pl.BlockSpec((CB_TM, D_MODEL), lambda i: (i, 0)),
                  pl.BlockSpec((CB_TM, RINFO), lambda i: (i, 0)),
                  pl.BlockSpec((1, TOP_K, CB_TM, PACKED), lambda i: (i, 0, 0, 0))],
        out_specs=pl.BlockSpec((CB_TM, D_MODEL), lambda i: (i, 0)),
        out_shape=jax.ShapeDtypeStruct((T, D_MODEL), F32),
        compiler_params=_cparams(1),
        name="combine",
    )(x1, gates, y2)


def _layer(x, rel_bias, ln1, w_in, q_norm, k_norm, attn_sink, conv_w, conv_b, lru_wa, lru_ba, lru_wi, lru_bi,
           lru_lambda, out_norm_attn, out_norm_lru, w_out, ln2, w_group, b_group, w_er, b_er, w_gate, w_up, w_down):
    B, S, D = x.shape
    T = B * S
    x2 = x.reshape(T, D)
    q, kv, xr, gr = _in_proj(x2, ln1, w_in, q_norm, k_norm)
    bias_tab = _bias_table(rel_bias)
    attn_n = _attention(q.reshape(B, S, ATTN_WIDTH), kv.reshape(B, S, 2 * KV_WIDTH), bias_tab, attn_sink,
                        out_norm_attn)
    lru_n = _rglru(xr.reshape(B, S, LRU_WIDTH), gr.reshape(B, S, LRU_WIDTH), conv_w, conv_b,
                   lru_wa, lru_ba, lru_wi, lru_bi, lru_lambda, out_norm_lru)
    x1, h2, gates, ei, cnt = _out_route(attn_n.reshape(T, ATTN_WIDTH), lru_n.reshape(T, LRU_WIDTH), x2, w_out, ln2,
                                        w_group, b_group, w_er, b_er)
    cap = _moe_cap(T)
    dest, pstart, block_expert, n_used, group_expert = _layout(ei, cnt, cap // MOE_BLOCK)
    xs = _padfill(_sc_dispatch(h2, dest, cap), pstart, cnt)
    yb = _experts(xs, block_expert, n_used, group_expert, w_gate, w_up, w_down)
    out = _combine(x1, gates, _sc_gather(yb, dest))
    return out.reshape(B, S, D)


def kernel(x, rel_bias, ln1, w_in, q_norm, k_norm, attn_sink, conv_w, conv_b, lru_wa, lru_ba, lru_wi, lru_bi,
           lru_lambda, out_norm_attn, out_norm_lru, w_out, ln2, w_group, b_group, w_expert_router, b_expert_router,
           w_gate, w_up, w_down):
    depth = ln1.shape[0]
    for l in range(depth):
        x = _layer(x, rel_bias, ln1[l], w_in[l], q_norm[l], k_norm[l], attn_sink[l], conv_w[l], conv_b[l],
                   lru_wa[l], lru_ba[l], lru_wi[l], lru_bi[l], lru_lambda[l], out_norm_attn[l], out_norm_lru[l],
                   w_out[l], ln2[l], w_group[l], b_group[l], w_expert_router[l], b_expert_router[l],
                   w_gate[l], w_up[l], w_down[l])
    return x
```

```python
import functools
import math

import jax
import jax.numpy as jnp
import numpy as np
from jax import lax
from jax.experimental import pallas as pl
from jax.experimental.pallas import tpu as pltpu
from jax.experimental.pallas import tpu_sc as plsc

D_MODEL = 1024
N_HEADS = 8
N_KV_HEADS = 2
HEAD_DIM = 64
Q_PER_KV = N_HEADS // N_KV_HEADS
ATTN_WIDTH = N_HEADS * HEAD_DIM
KV_WIDTH = N_KV_HEADS * HEAD_DIM
WINDOW = 128
BLOCK = 128
NUM_BUCKETS = 32
MAX_DISTANCE = 128
LRU_WIDTH = D_MODEL - ATTN_WIDTH
LRU_BLOCKS = 8
LRU_BLOCK_DIM = LRU_WIDTH // LRU_BLOCKS
LRU_C = 8.0
CONV_W = 4
CONV_LEFT = 2
N_GROUPS = 4
EXPERTS_PER_GROUP = 8
N_EXPERTS = N_GROUPS * EXPERTS_PER_GROUP
TOP_K = 2
D_EXPERT = 512
MOE_BLOCK = 256
EPS = 1e-6
NEG_INF = -1e30

LANES = 128
SUBLANES = 8
VMEM_LIMIT = 56 * 1024 * 1024

F32 = jnp.float32
BF16 = jnp.bfloat16


def _cparams(n_axes, vmem=VMEM_LIMIT):
    return pltpu.CompilerParams(dimension_semantics=("arbitrary",) * n_axes, vmem_limit_bytes=vmem)


def _rms(x, gain):
    return x * lax.rsqrt(jnp.mean(x * x, axis=-1, keepdims=True) + EPS) * gain


U32 = jnp.uint32
HI_MASK = 0xFFFF0000
PACKED = D_MODEL // 2


def _pack_rows(x):
    h = x.shape[1] // 2
    lo = lax.bitcast_convert_type(x[:, :h].astype(BF16).astype(F32), U32) >> 16
    hi = lax.bitcast_convert_type(x[:, h:].astype(BF16).astype(F32), U32) & jnp.uint32(HI_MASK)
    return lo | hi


def _unpack_rows(p):
    lo = lax.bitcast_convert_type(p << 16, F32)
    hi = lax.bitcast_convert_type(p & jnp.uint32(HI_MASK), F32)
    return lo, hi


IN_TM = 1024


def _head_rms(x, n_heads, gain):
    head = lax.broadcasted_iota(jnp.int32, (1, n_heads * HEAD_DIM), 1) // HEAD_DIM
    x2 = x * x
    scale = jnp.zeros_like(x)
    for h in range(n_heads):
        ms = jnp.sum(jnp.where(head == h, x2, 0.0), axis=-1, keepdims=True) * (1.0 / HEAD_DIM)
        scale = jnp.where(head == h, lax.rsqrt(ms + EPS), scale)
    return x * scale * gain


def _in_proj_kernel(x_ref, g_ref, w_ref, qg_ref, kg_ref, q_ref, kv_ref, xr_ref, gr_ref, wb_ref):
    @pl.when(pl.program_id(0) == 0)
    def _():
        wb_ref[...] = w_ref[...].astype(BF16)

    h = _rms(x_ref[...], g_ref[...]).astype(BF16)
    c_k = ATTN_WIDTH
    c_v = c_k + KV_WIDTH
    c_x = c_v + KV_WIDTH
    c_g = c_x + LRU_WIDTH
    q = jnp.dot(h, wb_ref[:, :c_k], preferred_element_type=F32)
    q_ref[...] = _head_rms(q, N_HEADS, qg_ref[...]).astype(BF16)
    k = jnp.dot(h, wb_ref[:, c_k:c_v], preferred_element_type=F32)
    kv_ref[:, :KV_WIDTH] = _head_rms(k, N_KV_HEADS, kg_ref[...]).astype(BF16)
    kv_ref[:, KV_WIDTH:] = jnp.dot(h, wb_ref[:, c_v:c_x], preferred_element_type=F32).astype(BF16)
    xr_ref[...] = jnp.dot(h, wb_ref[:, c_x:c_g], preferred_element_type=F32)
    gr_ref[...] = jnp.dot(h, wb_ref[:, c_g:], preferred_element_type=F32)


def _in_proj(x2, ln1, w_in, q_gain, k_gain):
    T = x2.shape[0]
    n_in = w_in.shape[1]
    row = lambda w: pl.BlockSpec((IN_TM, w), lambda i: (i, 0))
    qg = (jnp.tile(q_gain.astype(F32), N_HEADS) * (HEAD_DIM ** -0.5)).reshape(1, ATTN_WIDTH)
    kg = jnp.tile(k_gain.astype(F32), N_KV_HEADS).reshape(1, KV_WIDTH)
    return pl.pallas_call(
        _in_proj_kernel,
        grid=(T // IN_TM,),
        in_specs=[row(D_MODEL),
                  pl.BlockSpec((1, D_MODEL), lambda i: (0, 0)),
                  pl.BlockSpec((D_MODEL, n_in), lambda i: (0, 0)),
                  pl.BlockSpec((1, ATTN_WIDTH), lambda i: (0, 0)),
                  pl.BlockSpec((1, KV_WIDTH), lambda i: (0, 0))],
        out_specs=[row(ATTN_WIDTH), row(2 * KV_WIDTH), row(LRU_WIDTH), row(LRU_WIDTH)],
        out_shape=[jax.ShapeDtypeStruct((T, ATTN_WIDTH), BF16),
                   jax.ShapeDtypeStruct((T, 2 * KV_WIDTH), BF16),
                   jax.ShapeDtypeStruct((T, LRU_WIDTH), F32),
                   jax.ShapeDtypeStruct((T, LRU_WIDTH), F32)],
        scratch_shapes=[pltpu.VMEM((D_MODEL, n_in), BF16)],
        compiler_params=_cparams(1),
        name="in_proj",
    )(x2, ln1.reshape(1, D_MODEL), w_in, qg, kg)


def _t5_bucket(rel):
    half = NUM_BUCKETS // 2
    max_exact = half // 2
    base = jnp.where(rel > 0, half, 0)
    n = jnp.abs(rel)
    nf = jnp.maximum(n, 1).astype(jnp.float32)
    large = max_exact + (jnp.log(nf / max_exact) / math.log(MAX_DISTANCE / max_exact)
                         * (half - max_exact)).astype(jnp.int32)
    large = jnp.minimum(large, half - 1)
    return base + jnp.where(n < max_exact, n, large)


HEAD_PAIRS = Q_PER_KV // 2
EDGE_VARIANTS = 3


def _bias_kernel(rb_ref, bucket_ref, band_ref, o_ref):
    bucket = bucket_ref[...]
    band = band_ref[...] > 0
    col = lax.broadcasted_iota(jnp.int32, bucket.shape, 1)
    valid = (band & (col >= BLOCK), band, band & (col < 2 * BLOCK))
    for h in range(N_HEADS):
        acc = jnp.zeros(bucket.shape, F32)
        for b in range(NUM_BUCKETS):
            acc = jnp.where(bucket == b, rb_ref[b, h], acc)
        kv, g = divmod(h, Q_PER_KV)
        pair, parity = divmod(g, 2)
        for var in range(EDGE_VARIANTS):
            o_ref[var, kv, parity, pair * BLOCK:(pair + 1) * BLOCK, :] = jnp.where(valid[var], acc, NEG_INF)


def _bias_table(rel_bias):
    qi = jnp.arange(BLOCK, dtype=jnp.int32)
    kj = jnp.arange(3 * BLOCK, dtype=jnp.int32)
    rel = kj[None, :] - BLOCK - qi[:, None]
    bucket = _t5_bucket(rel).astype(jnp.int32)
    band = (jnp.abs(rel) <= WINDOW).astype(jnp.int32)
    return pl.pallas_call(
        _bias_kernel,
        in_specs=[pl.BlockSpec(memory_space=pltpu.SMEM),
                  pl.BlockSpec(memory_space=pltpu.VMEM),
                  pl.BlockSpec(memory_space=pltpu.VMEM)],
        out_specs=pl.BlockSpec(memory_space=pltpu.VMEM),
        out_shape=jax.ShapeDtypeStruct((EDGE_VARIANTS, N_KV_HEADS, 2, HEAD_PAIRS * BLOCK, 3 * BLOCK), F32),
        name="bias_table",
    )(rel_bias.astype(F32), bucket, band)


def _attn_kernel(sink_ref, q_ref, kp_ref, kc_ref, kn_ref, bias_ref, og_ref, o_ref):
    n = pl.program_id(1)
    kv_all = jnp.concatenate([kp_ref[0], kc_ref[0], kn_ref[0]], axis=0)
    for qb in range(ATTN_QB):
        variant = 1
        if qb == 0:
            variant = jnp.where(n == 0, 0, 1)
        if qb == ATTN_QB - 1:
            variant = jnp.where(n == pl.num_programs(1) - 1, 2, variant)
        out = _attn_block(q_ref[0, qb * BLOCK:(qb + 1) * BLOCK, :], kv_all[qb * BLOCK:(qb + 3) * BLOCK, :],
                          lambda kv, parity: bias_ref[variant, kv, parity], sink_ref)
        o_ref[0, qb * BLOCK:(qb + 1) * BLOCK, :] = _rms(out, og_ref[...]).astype(o_ref.dtype)


def _attn_block(q, kvw, bias, sink_ref):
    low = lax.broadcasted_iota(jnp.int32, (1, LANES), 1) < HEAD_DIM
    swap = lambda slab: pltpu.roll(slab.astype(F32), HEAD_DIM, 1).astype(BF16)
    kslab, vslab = kvw[:, :KV_WIDTH], kvw[:, KV_WIDTH:]
    kslab_sw, vslab_sw = swap(kslab), swap(vslab)
    rowi = lax.broadcasted_iota(jnp.int32, (HEAD_PAIRS * BLOCK, 1), 0)
    combos = [(kv, parity) for kv in range(N_KV_HEADS) for parity in range(2)]
    scores, vzs, sinks = [], [], []
    for kv, parity in combos:
        ks, vs = (kslab, vslab) if (kv == 0) == (parity == 0) else (kslab_sw, vslab_sw)
        keep = low if parity == 0 else jnp.logical_not(low)
        kz = jnp.where(keep, ks, jnp.zeros_like(ks))
        vzs.append(jnp.where(keep, vs, jnp.zeros_like(vs)))
        base = kv * Q_PER_KV * HEAD_DIM
        qpair = jnp.concatenate([q[:, base + j * LANES:base + (j + 1) * LANES] for j in range(HEAD_PAIRS)], axis=0)
        s = lax.dot_general(qpair, kz, (((1,), (1,)), ((), ())), preferred_element_type=F32)
        scores.append(s + bias(kv, parity))
        sink = jnp.zeros((HEAD_PAIRS * BLOCK, 1), F32)
        for j in range(HEAD_PAIRS):
            sink = jnp.where(rowi // BLOCK == j, sink_ref[kv * Q_PER_KV + 2 * j + parity], sink)
        sinks.append(sink)
    probs, inv = [], []
    for s, sink in zip(scores, sinks):
        m = jnp.maximum(jnp.max(s, axis=-1, keepdims=True), sink)
        p = jnp.exp(s - m)
        inv.append(1.0 / (jnp.sum(p, axis=-1, keepdims=True) + jnp.exp(sink - m)))
        probs.append(p.astype(BF16))
    outs = [jnp.dot(p, vz, preferred_element_type=F32) * r for p, vz, r in zip(probs, vzs, inv)]
    cols = []
    for kv in range(N_KV_HEADS):
        acc = outs[2 * kv] + outs[2 * kv + 1]
        cols += [acc[j * BLOCK:(j + 1) * BLOCK, :] for j in range(HEAD_PAIRS)]
    return jnp.concatenate(cols, axis=1)


ATTN_QB = 4


def _attention(q, kv, bias_tab, sink, out_gain):
    B, S, _ = q.shape
    nb = S // BLOCK
    assert ATTN_QB >= 2 and nb % ATTN_QB == 0, "a step's first and last query blocks must be distinct"
    ns = nb // ATTN_QB
    rows = ATTN_QB * BLOCK
    kvspec = lambda f: pl.BlockSpec((1, BLOCK, 2 * KV_WIDTH), f)
    return pl.pallas_call(
        _attn_kernel,
        grid=(B, ns),
        in_specs=[pl.BlockSpec(memory_space=pltpu.SMEM),
                  pl.BlockSpec((1, rows, ATTN_WIDTH), lambda b, n: (b, n, 0)),
                  kvspec(lambda b, n: (b, jnp.maximum(n * ATTN_QB - 1, 0), 0)),
                  pl.BlockSpec((1, rows, 2 * KV_WIDTH), lambda b, n: (b, n, 0)),
                  kvspec(lambda b, n: (b, jnp.minimum((n + 1) * ATTN_QB, nb - 1), 0)),
                  pl.BlockSpec((EDGE_VARIANTS, N_KV_HEADS, 2, HEAD_PAIRS * BLOCK, 3 * BLOCK),
                               lambda b, n: (0, 0, 0, 0, 0)),
                  pl.BlockSpec((1, ATTN_WIDTH), lambda b, n: (0, 0))],
        out_specs=pl.BlockSpec((1, rows, ATTN_WIDTH), lambda b, n: (b, n, 0)),
        out_shape=jax.ShapeDtypeStruct((B, S, ATTN_WIDTH), BF16),
        compiler_params=_cparams(2),
        name="attention",
    )(sink.astype(F32), q, kv, kv, kv, bias_tab, out_gain.reshape(1, ATTN_WIDTH))


LRU_TC = 128
LRU_PITCH = LRU_TC + SUBLANES
LRU_SLABS = LRU_WIDTH // LANES
HALO = SUBLANES


def _softplus(x):
    return jnp.maximum(x, 0.0) + jnp.log(1.0 + jnp.exp(-jnp.abs(x)))


def _gelu_tanh(x):
    k = math.sqrt(2.0 / math.pi)
    hx = 0.5 * x
    return hx + hx * jnp.tanh(x * (k + (k * 0.044715) * (x * x)))


def _sigmoid(x):
    return 0.5 + 0.5 * jnp.tanh(0.5 * x)


def _rglru_kernel(xr_ref, xp_ref, xn_ref, gr_ref, cw_ref, cb_ref, wg_ref, bg_ref, lam_ref, og_ref,
                  o_ref, sx_ref, a_ref, u_ref, h_ref, carry_ref, hf_ref):
    p = pl.program_id(0)
    i = pl.program_id(1)
    nc = pl.num_programs(1)
    c = i + p * (nc - 1 - 2 * i)
    B = xr_ref.shape[0]
    TC = LRU_TC

    sx_ref[:, HALO:HALO + TC, :] = xr_ref[...]
    sx_ref[:, 0:HALO, :] = jnp.where(c > 0, xp_ref[...], 0.0)
    sx_ref[:, HALO + TC:, :] = jnp.where(c < nc - 1, xn_ref[...], 0.0)
    xc = cb_ref[...][None]
    for j in range(CONV_W):
        off = HALO + j - CONV_LEFT
        xc = xc + cw_ref[j:j + 1, :][None] * sx_ref[:, off:off + TC, :]
    xc2 = xc.reshape(B * TC, LRU_WIDTH)

    g = jnp.dot(xc2.astype(BF16), wg_ref[0], preferred_element_type=F32) + bg_ref[0]
    r = _sigmoid(g[:, :LRU_WIDTH])
    ig = _sigmoid(g[:, LRU_WIDTH:])
    a = jnp.exp(r * (-LRU_C * _softplus(-lam_ref[0])))
    z = 1.0 - a * a
    u = z * lax.rsqrt(jnp.maximum(z, 1e-30)) * ig * xc2
    for b in range(B):
        for s in range(LRU_SLABS):
            a_ref[s, b * LRU_PITCH:b * LRU_PITCH + TC, :] = a[b * TC:(b + 1) * TC, s * LANES:(s + 1) * LANES]
            u_ref[s, b * LRU_PITCH:b * LRU_PITCH + TC, :] = u[b * TC:(b + 1) * TC, s * LANES:(s + 1) * LANES]

    @pl.when(i == 0)
    def _():
        carry_ref[...] = jnp.zeros_like(carry_ref)

    def step(k, hs):
        t = k + p * (TC - 1 - 2 * k)
        out = []
        for s in range(LRU_SLABS):
            idx = pl.ds(t, B, stride=LRU_PITCH)
            hn = a_ref[s, idx, :] * hs[s] + u_ref[s, idx, :]
            h_ref[s, idx, :] = hn
            out.append(hn)
        return tuple(out)

    hs = lax.fori_loop(0, TC, step, tuple(carry_ref[s] for s in range(LRU_SLABS)), unroll=8)
    for s in range(LRU_SLABS):
        carry_ref[s] = hs[s]

    @pl.when(p == 0)
    def _():
        for b in range(B):
            for s in range(LRU_SLABS):
                hf_ref[c, s, b * TC:(b + 1) * TC, :] = h_ref[s, b * LRU_PITCH:b * LRU_PITCH + TC, :].astype(hf_ref.dtype)

    @pl.when(p == 1)
    def _():
        for b in range(B):
            hsum = jnp.concatenate(
                [h_ref[s, b * LRU_PITCH:b * LRU_PITCH + TC, :] + hf_ref[c, s, b * TC:(b + 1) * TC, :].astype(F32)
                 for s in range(LRU_SLABS)], axis=1)
            y = hsum * _gelu_tanh(gr_ref[b])
            o_ref[b] = _rms(y, og_ref[...]).astype(o_ref.dtype)


def _block_diag(w):
    eye = jnp.eye(LRU_BLOCKS, dtype=w.dtype)
    return jnp.einsum('hij,hg->higj', w, eye).reshape(LRU_WIDTH, LRU_WIDTH)


def _rglru(xr, gr, conv_w, conv_b, w_a, b_a, w_i, b_i, lam, out_gain):
    B, S, W = xr.shape
    nc = S // LRU_TC
    hb = LRU_TC // HALO
    wg = jnp.stack([jnp.concatenate([_block_diag(w_a[d]), _block_diag(w_i[d])], axis=1) for d in range(2)]).astype(BF16)
    bg = jnp.concatenate([b_a, b_i], axis=-1).reshape(2, 1, 2 * W).astype(F32)
    chunk = lambda p, i: i + p * (nc - 1 - 2 * i)
    full2 = lambda shape: pl.BlockSpec(shape, lambda p, i: (0,) * len(shape))
    return pl.pallas_call(
        _rglru_kernel,
        grid=(2, nc),
        in_specs=[pl.BlockSpec((B, LRU_TC, W), lambda p, i: (0, chunk(p, i), 0)),
                  pl.BlockSpec((B, HALO, W), lambda p, i: (0, jnp.maximum(chunk(p, i) * hb - 1, 0), 0)),
                  pl.BlockSpec((B, HALO, W), lambda p, i: (0, jnp.minimum((chunk(p, i) + 1) * hb, S // HALO - 1), 0)),
                  pl.BlockSpec((B, LRU_TC, W), lambda p, i: (0, chunk(p, i), 0)),
                  full2((CONV_W, W)),
                  full2((1, W)),
                  pl.BlockSpec((1, W, 2 * W), lambda p, i: (p, 0, 0)),
                  pl.BlockSpec((1, 1, 2 * W), lambda p, i: (p, 0, 0)),
                  pl.BlockSpec((1, 1, W), lambda p, i: (p, 0, 0)),
                  full2((1, W))],
        out_specs=pl.BlockSpec((B, LRU_TC, W), lambda p, i: (0, nc - 1 - p * i, 0)),
        out_shape=jax.ShapeDtypeStruct((B, S, W), BF16),
        scratch_shapes=[pltpu.VMEM((B, LRU_TC + 2 * HALO, W), F32),
                        pltpu.VMEM((LRU_SLABS, B * LRU_PITCH, LANES), F32),
                        pltpu.VMEM((LRU_SLABS, B * LRU_PITCH, LANES), F32),
                        pltpu.VMEM((LRU_SLABS, B * LRU_PITCH, LANES), F32),
                        pltpu.VMEM((LRU_SLABS, B, LANES), F32),
                        pltpu.VMEM((nc, LRU_SLABS, B * LRU_TC, LANES), BF16)],
        compiler_params=_cparams(2),
        name="rglru",
    )(xr, xr, xr, gr, conv_w.astype(F32), conv_b.reshape(1, W).astype(F32), wg, bg,
      lam.reshape(2, 1, W).astype(F32), out_gain.reshape(1, W).astype(F32))


RT_TM = 512
RT_COLS = LANES
RT_ROWS = 48
RINFO = SUBLANES


def _split_bf16(x):
    hi = x.astype(BF16)
    lo = (x - hi.astype(F32)).astype(BF16)
    return hi, lo


def _route_kernel(an_ref, ln_ref, x_ref, wo_ref, g2_ref, wr_ref, br_ref,
                  x1_ref, h2_ref, gt_ref, ei_ref, cnt_ref, wob_ref, wrb_ref, tri_ref, run_ref, runc_ref):
    @pl.when(pl.program_id(0) == 0)
    def _():
        wob_ref[...] = wo_ref[...].astype(BF16)
        hi, lo = _split_bf16(wr_ref[...])
        wrb_ref[:RT_ROWS, :] = hi
        wrb_ref[RT_ROWS:, :] = lo
        r = lax.broadcasted_iota(jnp.int32, (RT_TM, RT_TM), 0)
        cidx = lax.broadcasted_iota(jnp.int32, (RT_TM, RT_TM), 1)
        tri_ref[...] = (r < cidx).astype(BF16)
        run_ref[...] = jnp.zeros_like(run_ref)
        runc_ref[...] = jnp.zeros_like(runc_ref)

    x1 = (x_ref[...]
          + jnp.dot(an_ref[...], wob_ref[:ATTN_WIDTH, :], preferred_element_type=F32)
          + jnp.dot(ln_ref[...], wob_ref[ATTN_WIDTH:, :], preferred_element_type=F32))
    x1_ref[...] = x1
    h2 = _rms(x1, g2_ref[...])
    h2_ref[...] = _pack_rows(h2)

    hi, lo = _split_bf16(h2)
    nt_dims = (((1,), (1,)), ((), ()))
    t1 = lax.dot_general(wrb_ref[...], hi, nt_dims, preferred_element_type=F32)
    t2 = lax.dot_general(wrb_ref[:RT_ROWS, :], lo, nt_dims, preferred_element_type=F32)
    logit = t1[:RT_ROWS] + t1[RT_ROWS:] + t2 + br_ref[...]

    sub = lax.broadcasted_iota(jnp.int32, (SUBLANES, RT_TM), 0)
    first_min = lambda hit: jnp.min(jnp.where(hit, sub, SUBLANES), axis=0, keepdims=True)
    is_g = sub < N_GROUPS
    gl = jnp.where(is_g, logit[:SUBLANES], -jnp.inf)
    gm = jnp.max(gl, axis=0, keepdims=True)
    gidx = first_min(gl == gm)
    g_p = 1.0 / jnp.sum(jnp.where(is_g, jnp.exp(logit[:SUBLANES] - gm), 0.0), axis=0, keepdims=True)
    el = logit[SUBLANES:2 * SUBLANES]
    for g in range(1, N_GROUPS):
        el = jnp.where(gidx == g, logit[(g + 1) * SUBLANES:(g + 2) * SUBLANES], el)
    m1 = jnp.max(el, axis=0, keepdims=True)
    i1 = first_min(el == m1)
    el2 = jnp.where(sub == i1, -jnp.inf, el)
    m2 = jnp.max(el2, axis=0, keepdims=True)
    i2 = first_min(el2 == m2)
    t = jnp.exp(m2 - m1)
    gate1 = g_p / (1.0 + t)
    gate2 = g_p * t / (1.0 + t)
    e1 = gidx * EXPERTS_PER_GROUP + i1
    e2 = gidx * EXPERTS_PER_GROUP + i2

    erow = lax.broadcasted_iota(jnp.int32, (N_EXPERTS, RT_TM), 0)
    oh1 = erow == e1
    oh2 = erow == e2
    oh = (oh1 | oh2).astype(F32)
    ohb = oh.astype(BF16)
    cum = jnp.dot(ohb, tri_ref[...], preferred_element_type=F32) + runc_ref[...]
    rank1 = jnp.sum(jnp.where(oh1, cum, 0.0), axis=0, keepdims=True)
    rank2 = jnp.sum(jnp.where(oh2, cum, 0.0), axis=0, keepdims=True)
    runc_ref[...] = runc_ref[...] + jnp.sum(oh, axis=1, keepdims=True)
    tile_cnt = lax.dot_general(jnp.ones((SUBLANES, RT_TM), BF16), ohb, nt_dims, preferred_element_type=F32)
    run_ref[:, :N_EXPERTS] = run_ref[:, :N_EXPERTS] + tile_cnt[0:1]
    cnt_ref[...] = run_ref[...].astype(jnp.int32)

    rows = [e1, e2, rank1.astype(jnp.int32), rank2.astype(jnp.int32)]
    ei = jnp.zeros((RINFO, RT_TM), jnp.int32)
    for k, v in enumerate(rows):
        ei = jnp.where(sub == k, v, ei)
    ei_ref[0] = ei
    gt_ref[...] = jnp.where(sub == 0, gate1, jnp.where(sub == 1, gate2, 0.0)).T


def _out_route(attn_n, lru_n, x2, w_out, ln2, w_group, b_group, w_er, b_er):
    T = x2.shape[0]
    pad_g = SUBLANES - N_GROUPS
    wr = jnp.concatenate([jnp.pad(w_group.T, ((0, pad_g), (0, 0))),
                          jnp.transpose(w_er, (0, 2, 1)).reshape(N_EXPERTS, D_MODEL)], axis=0)
    wr = jnp.pad(wr, ((0, RT_ROWS - wr.shape[0]), (0, 0))).astype(F32)
    br = jnp.concatenate([jnp.pad(b_group, (0, pad_g)), b_er.reshape(-1)])
    br = jnp.pad(br, (0, RT_ROWS - br.shape[0])).reshape(RT_ROWS, 1).astype(F32)
    row = lambda w: pl.BlockSpec((RT_TM, w), lambda i: (i, 0))
    const = lambda shape: pl.BlockSpec(shape, lambda i: (0, 0))
    return pl.pallas_call(
        _route_kernel,
        grid=(T // RT_TM,),
        in_specs=[row(ATTN_WIDTH), row(LRU_WIDTH), row(D_MODEL), const((D_MODEL, D_MODEL)), const((1, D_MODEL)),
                  const((RT_ROWS, D_MODEL)), const((RT_ROWS, 1))],
        out_specs=[row(D_MODEL), row(PACKED), row(RINFO),
                   pl.BlockSpec((1, RINFO, RT_TM), lambda i: (i, 0, 0)), const((1, RT_COLS))],
        out_shape=[jax.ShapeDtypeStruct((T, D_MODEL), F32),
                   jax.ShapeDtypeStruct((T, PACKED), U32),
                   jax.ShapeDtypeStruct((T, RINFO), F32),
                   jax.ShapeDtypeStruct((T // RT_TM, RINFO, RT_TM), jnp.int32),
                   jax.ShapeDtypeStruct((1, RT_COLS), jnp.int32)],
        scratch_shapes=[pltpu.VMEM((D_MODEL, D_MODEL), BF16),
                        pltpu.VMEM((2 * RT_ROWS, D_MODEL), BF16),
                        pltpu.VMEM((RT_TM, RT_TM), BF16),
                        pltpu.VMEM((1, RT_COLS), F32),
                        pltpu.VMEM((N_EXPERTS, 1), F32)],
        compiler_params=_cparams(1),
        name="out_route",
    )(attn_n, lru_n, x2, w_out, ln2.reshape(1, D_MODEL).astype(F32), wr, br)


def _moe_cap(T):
    A = T * TOP_K
    return ((A + MOE_BLOCK - 1) // MOE_BLOCK) * MOE_BLOCK + N_EXPERTS * MOE_BLOCK


PAD_BITS = tuple(1 << b for b in reversed(range(3, MOE_BLOCK.bit_length() - 1)))


def _layout_kernel(cnt_ref, ei_ref, dest_ref, pstart, be_ref, nu_ref, ge_ref):
    n_blocks = be_ref.shape[0]

    def lay(e, carry):
        start, blk, grp = carry
        pstart[e] = start
        nb = (cnt_ref[0, e] + MOE_BLOCK - 1) // MOE_BLOCK
        ge_ref[grp] = e

        def fill(k, c):
            be_ref[blk + k] = e
            return c
        lax.fori_loop(0, nb, fill, 0)
        return start + nb * MOE_BLOCK, blk + nb, grp + (nb > 0).astype(jnp.int32)
    _, used, groups = lax.fori_loop(0, N_EXPERTS, lay, (jnp.int32(0), jnp.int32(0), jnp.int32(0)))
    nu_ref[0] = used

    def tail(k, c):
        be_ref[k] = N_EXPERTS - 1
        return c
    lax.fori_loop(used, n_blocks, tail, 0)

    def no_group(k, c):
        ge_ref[k] = -1
        return c
    lax.fori_loop(groups, N_EXPERTS + 1, no_group, 0)

    expert = ei_ref[:, 0:TOP_K, :]
    dest = ei_ref[:, TOP_K:2 * TOP_K, :]
    for e in range(N_EXPERTS):
        dest = dest + jnp.where(expert == e, pstart[e], 0)
    dest_ref[...] = dest


def _layout(ei, cnt, n_blocks):
    nt = ei.shape[0]
    smem = pl.BlockSpec(memory_space=pltpu.SMEM)
    vmem = pl.BlockSpec(memory_space=pltpu.VMEM)
    return pl.pallas_call(
        _layout_kernel,
        in_specs=[smem, vmem],
        out_specs=[vmem, smem, smem, smem, smem],
        out_shape=[jax.ShapeDtypeStruct((nt, TOP_K, RT_TM), jnp.int32),
                   jax.ShapeDtypeStruct((N_EXPERTS,), jnp.int32),
                   jax.ShapeDtypeStruct((n_blocks,), jnp.int32),
                   jax.ShapeDtypeStruct((1,), jnp.int32),
                   jax.ShapeDtypeStruct((N_EXPERTS + 1,), jnp.int32)],
        name="layout",
    )(cnt, ei)


SC_CHUNK = 64
SC_BUFS = 3
SC_LEAD = SC_BUFS - 1


def _sc_workers():
    info = plsc.get_sparse_core_info()
    return info.num_cores, info.num_subcores


def _sc_ring(n_chunks, read, write):
    for c in range(min(SC_LEAD, n_chunks)):
        for cp in read(c):
            cp.start()
    reclaimed = set()
    for c in range(n_chunks):
        for cp in read(c):
            cp.wait()
        for cp in write(c):
            cp.start()
        nxt = c + SC_LEAD
        if nxt < n_chunks:
            if nxt - SC_BUFS >= 0:
                for cp in write(nxt - SC_BUFS):
                    cp.wait()
                reclaimed.add(nxt - SC_BUFS)
            for cp in read(nxt):
                cp.start()
    for c in range(n_chunks):
        if c not in reclaimed:
            for cp in write(c):
                cp.wait()


def _sc_dispatch(h2p, dest, cap):
    T = h2p.shape[0]
    nc, ns = _sc_workers()
    per_w = T // (nc * ns)
    n_ch = per_w // SC_CHUNK
    assert dest.shape == (nc * ns, TOP_K, per_w) and per_w % SC_CHUNK == 0
    idx = dest.reshape(nc * ns, TOP_K * n_ch, SC_CHUNK)
    mesh = plsc.VectorSubcoreMesh(core_axis_name="c", subcore_axis_name="s")

    @functools.partial(
        pl.kernel, mesh=mesh,
        out_type=jax.ShapeDtypeStruct((cap, PACKED), U32),
        scratch_types=[pltpu.VMEM((TOP_K * n_ch, SC_CHUNK), jnp.int32),
                       pltpu.VMEM((SC_BUFS, SC_CHUNK, PACKED), U32),
                       pltpu.SemaphoreType.DMA((SC_BUFS,)),
                       pltpu.SemaphoreType.DMA((SC_BUFS,))])
    def scatter(src_hbm, idx_hbm, out_hbm, idx_v, rows_v, rsem, wsem):
        wid = lax.axis_index("s") * nc + lax.axis_index("c")
        base = pl.multiple_of(wid * per_w, per_w)
        pltpu.sync_copy(idx_hbm.at[wid], idx_v)

        def read(c):
            b = c % SC_BUFS
            return [pltpu.make_async_copy(src_hbm.at[pl.ds(base + c * SC_CHUNK, SC_CHUNK)], rows_v.at[b], rsem.at[b])]

        def write(c):
            b = c % SC_BUFS
            return [pltpu.make_async_copy(rows_v.at[b], out_hbm.at[idx_v.at[k * n_ch + c]], wsem.at[b])
                    for k in range(TOP_K)]
        _sc_ring(n_ch, read, write)

    return scatter(h2p, idx)


def _sc_gather(yb, dest):
    nt, _, tm = dest.shape
    nc, ns = _sc_workers()
    n_rows = nt * TOP_K * tm
    per_w = n_rows // (nc * ns)
    n_ch = per_w // SC_CHUNK
    assert per_w * nc * ns == n_rows and per_w % SC_CHUNK == 0
    mesh = plsc.VectorSubcoreMesh(core_axis_name="c", subcore_axis_name="s")

    @functools.partial(
        pl.kernel, mesh=mesh,
        out_type=jax.ShapeDtypeStruct((n_rows, PACKED), U32),
        scratch_types=[pltpu.VMEM((per_w,), jnp.int32),
                       pltpu.VMEM((SC_BUFS, SC_CHUNK, PACKED), U32),
                       pltpu.SemaphoreType.DMA((SC_BUFS,)),
                       pltpu.SemaphoreType.DMA((SC_BUFS,))])
    def gather(table_hbm, idx_hbm, out_hbm, idx_v, rows_v, gsem, wsem):
        wid = lax.axis_index("s") * nc + lax.axis_index("c")
        base = pl.multiple_of(wid * per_w, per_w)
        pltpu.sync_copy(idx_hbm.at[pl.ds(base, per_w)], idx_v)

        def read(c):
            b = c % SC_BUFS
            return [pltpu.make_async_copy(table_hbm.at[idx_v.at[pl.ds(c * SC_CHUNK, SC_CHUNK)]], rows_v.at[b], gsem.at[b])]

        def write(c):
            b = c % SC_BUFS
            return [pltpu.make_async_copy(rows_v.at[b], out_hbm.at[pl.ds(base + c * SC_CHUNK, SC_CHUNK)], wsem.at[b])]
        _sc_ring(n_ch, read, write)

    return gather(yb, dest.reshape(n_rows)).reshape(nt, TOP_K, tm, PACKED)


def _padfill_kernel(cnt_ref, pstart, xs_in, xs_ref, zeros, zsem):
    del xs_in

    def pad_copies(fn):
        for e in range(N_EXPERTS):
            cnt = cnt_ref[0, e]
            head = (-cnt) & (SUBLANES - 1)
            rest = ((-cnt) & (MOE_BLOCK - 1)) - head
            off = pstart[e] + cnt
            for k in range(SUBLANES - 1):
                @pl.when(k < head)
                def _(off=off, k=k):
                    fn(pltpu.make_async_copy(zeros.at[pl.ds(0, 1), :], xs_ref.at[pl.ds(off + k, 1), :], zsem))
            off = off + head
            for bit in PAD_BITS:
                @pl.when((rest & bit) != 0)
                def _(off=off, bit=bit):
                    fn(pltpu.make_async_copy(zeros.at[pl.ds(0, bit), :],
                                             xs_ref.at[pl.ds(pl.multiple_of(off, SUBLANES), bit), :], zsem))
                off = off + (rest & bit)

    zeros[...] = jnp.zeros_like(zeros)
    pad_copies(lambda cp: cp.start())
    pad_copies(lambda cp: cp.wait())


def _padfill(xs, pstart, cnt):
    smem = pl.BlockSpec(memory_space=pltpu.SMEM)
    hbm = pl.BlockSpec(memory_space=pl.ANY)
    return pl.pallas_call(
        _padfill_kernel,
        in_specs=[smem, smem, hbm],
        out_specs=hbm,
        out_shape=jax.ShapeDtypeStruct(xs.shape, xs.dtype),
        input_output_aliases={2: 0},
        scratch_shapes=[pltpu.VMEM((MOE_BLOCK // 2, PACKED), U32), pltpu.SemaphoreType.DMA(())],
        name="padfill",
    )(cnt, pstart, xs)


W_SLOTS = 2
EXPERT_GROUP = 8
EXPERT_RUNS = (1, 2, 4)


def _expert_kernel(be_ref, nu_ref, ge_ref, x_ref, wg_hbm, wu_hbm, wd_hbm, o_ref,
                   wgf, wuf, wdf, wgb, wub, wdb, grp_ref, sems):
    step = pl.program_id(0)

    def weight_copies(e, slot):
        return (pltpu.make_async_copy(wg_hbm.at[e], wgf.at[slot], sems.at[slot, 0]),
                pltpu.make_async_copy(wu_hbm.at[e], wuf.at[slot], sems.at[slot, 1]),
                pltpu.make_async_copy(wd_hbm.at[e], wdf.at[slot], sems.at[slot, 2]))

    @pl.when(step == 0)
    def _():
        grp_ref[0] = 0
        for cp in weight_copies(ge_ref[0], 0):
            cp.start()

    n_blocks = be_ref.shape[0]
    n_used = nu_ref[0]

    def swiglu(s, n):
        rows = pl.ds(pl.multiple_of(s * MOE_BLOCK, MOE_BLOCK), n * MOE_BLOCK)
        lo, hi = _unpack_rows(x_ref[rows, :])
        lo = lo.astype(BF16)
        hi = hi.astype(BF16)
        g = (jnp.dot(lo, wgb[:PACKED, :], preferred_element_type=F32)
             + jnp.dot(hi, wgb[PACKED:, :], preferred_element_type=F32))
        u = (jnp.dot(lo, wub[:PACKED, :], preferred_element_type=F32)
             + jnp.dot(hi, wub[PACKED:, :], preferred_element_type=F32))
        h = (g * _sigmoid(g) * u).astype(BF16)
        o_ref[rows, :] = _pack_rows(jnp.dot(h, wdb[...], preferred_element_type=F32))

    def run(s):
        j = step * EXPERT_GROUP + s
        e = be_ref[j]
        first = jnp.logical_or(j == 0, e != be_ref[jnp.maximum(j - 1, 0)])

        @pl.when(first)
        def _():
            grp = grp_ref[0]
            slot = grp % W_SLOTS
            for cp in weight_copies(e, slot):
                cp.wait()
            wgb[...] = wgf[slot].astype(BF16)
            wub[...] = wuf[slot].astype(BF16)
            wdb[...] = wdf[slot].astype(BF16)
            nxt = ge_ref[grp + 1]

            @pl.when(nxt >= 0)
            def _():
                for cp in weight_copies(nxt, 1 - slot):
                    cp.start()
            grp_ref[0] = grp + 1

        def same(k):
            return (s + k < EXPERT_GROUP) & (j + k < n_used) & (be_ref[jnp.minimum(j + k, n_blocks - 1)] == e)
        take = jnp.int32(1)
        for n in EXPERT_RUNS[1:]:
            ok = same(n - 1)
            for k in range(1, n - 1):
                ok = ok & same(k)
            take = jnp.where(ok, n, take)
        for n in EXPERT_RUNS:
            @pl.when(take == n)
            def _(n=n):
                swiglu(s, n)
        return s + take

    lax.while_loop(lambda s: (s < EXPERT_GROUP) & (step * EXPERT_GROUP + s < n_used), run, jnp.int32(0))


def _experts(xs, block_expert, n_used, group_expert, w_gate, w_up, w_down):
    cap = xs.shape[0]
    n_blocks = cap // MOE_BLOCK
    assert n_blocks % EXPERT_GROUP == 0
    rows = EXPERT_GROUP * MOE_BLOCK
    last = lambda j, be, nu, ge: jnp.minimum(j, (nu[0] - 1) // EXPERT_GROUP)
    hbm = pl.BlockSpec(memory_space=pl.ANY)
    gs = pltpu.PrefetchScalarGridSpec(
        num_scalar_prefetch=3,
        grid=(n_blocks // EXPERT_GROUP,),
        in_specs=[pl.BlockSpec((rows, PACKED), lambda j, be, nu, ge: (last(j, be, nu, ge), 0)), hbm, hbm, hbm],
        out_specs=pl.BlockSpec((rows, PACKED), lambda j, be, nu, ge: (last(j, be, nu, ge), 0)),
        scratch_shapes=[pltpu.VMEM((W_SLOTS, D_MODEL, D_EXPERT), F32),
                        pltpu.VMEM((W_SLOTS, D_MODEL, D_EXPERT), F32),
                        pltpu.VMEM((W_SLOTS, D_EXPERT, D_MODEL), F32),
                        pltpu.VMEM((D_MODEL, D_EXPERT), BF16),
                        pltpu.VMEM((D_MODEL, D_EXPERT), BF16),
                        pltpu.VMEM((D_EXPERT, D_MODEL), BF16),
                        pltpu.SMEM((1,), jnp.int32),
                        pltpu.SemaphoreType.DMA((W_SLOTS, 3))],
    )
    return pl.pallas_call(
        _expert_kernel,
        grid_spec=gs,
        out_shape=jax.ShapeDtypeStruct((cap, PACKED), U32),
        compiler_params=_cparams(1),
        name="experts",
    )(block_expert, n_used, group_expert, xs, w_gate, w_up, w_down)


CB_TM = RT_TM


def _combine_kernel(x1_ref, gt_ref, y2_ref, o_ref):
    g = gt_ref[...]
    lo1, hi1 = _unpack_rows(y2_ref[0, 0])
    lo2, hi2 = _unpack_rows(y2_ref[0, 1])
    o_ref[:, :PACKED] = x1_ref[:, :PACKED] + g[:, 0:1] * lo1 + g[:, 1:2] * lo2
    o_ref[:, PACKED:] = x1_ref[:, PACKED:] + g[:, 0:1] * hi1 + g[:, 1:2] * hi2


def _combine(x1, gates, y2):
    T = x1.shape[0]
    nt = T // CB_TM
    return pl.pallas_call(
        _combine_kernel,
        grid=(nt,),
        in_specs=[pl.BlockSpec((CB_TM, D_MODEL), lambda i: (i, 0)),
                  pl.BlockSpec((CB_TM, RINFO), lambda i: (i, 0)),
                  pl.BlockSpec((1, TOP_K, CB_TM, PACKED), lambda i: (i, 0, 0, 0))],
        out_specs=pl.BlockSpec((CB_TM, D_MODEL), lambda i: (i, 0)),
        out_shape=jax.ShapeDtypeStruct((T, D_MODEL), F32),
        compiler_params=_cparams(1),
        name="combine",
    )(x1, gates, y2)


def _layer(x, rel_bias, ln1, w_in, q_norm, k_norm, attn_sink, conv_w, conv_b, lru_wa, lru_ba, lru_wi, lru_bi,
           lru_lambda, out_norm_attn, out_norm_lru, w_out, ln2, w_group, b_group, w_er, b_er, w_gate, w_up, w_down):
    B, S, D = x.shape
    T = B * S
    x2 = x.reshape(T, D)
    q, kv, xr, gr = _in_proj(x2, ln1, w_in, q_norm, k_norm)
    bias_tab = _bias_table(rel_bias)
    attn_n = _attention(q.reshape(B, S, ATTN_WIDTH), kv.reshape(B, S, 2 * KV_WIDTH), bias_tab, attn_sink,
                        out_norm_attn)
    lru_n = _rglru(xr.reshape(B, S, LRU_WIDTH), gr.reshape(B, S, LRU_WIDTH), conv_w, conv_b,
                   lru_wa, lru_ba, lru_wi, lru_bi, lru_lambda, out_norm_lru)
    x1, h2, gates, ei, cnt = _out_route(attn_n.reshape(T, ATTN_WIDTH), lru_n.reshape(T, LRU_WIDTH), x2, w_out, ln2,
                                        w_group, b_group, w_er, b_er)
    cap = _moe_cap(T)
    dest, pstart, block_expert, n_used, group_expert = _layout(ei, cnt, cap // MOE_BLOCK)
    xs = _padfill(_sc_dispatch(h2, dest, cap), pstart, cnt)
    yb = _experts(xs, block_expert, n_used, group_expert, w_gate, w_up, w_down)
    out = _combine(x1, gates, _sc_gather(yb, dest))
    return out.reshape(B, S, D)


def kernel(x, rel_bias, ln1, w_in, q_norm, k_norm, attn_sink, conv_w, conv_b, lru_wa, lru_ba, lru_wi, lru_bi,
           lru_lambda, out_norm_attn, out_norm_lru, w_out, ln2, w_group, b_group, w_expert_router, b_expert_router,
           w_gate, w_up, w_down):
    depth = ln1.shape[0]
    for l in range(depth):
        x = _layer(x, rel_bias, ln1[l], w_in[l], q_norm[l], k_norm[l], attn_sink[l], conv_w[l], conv_b[l],
                   lru_wa[l], lru_ba[l], lru_wi[l], lru_bi[l], lru_lambda[l], out_norm_attn[l], out_norm_lru[l],
                   w_out[l], ln2[l], w_group[l], b_group[l], w_expert_router[l], b_expert_router[l],
                   w_gate[l], w_up[l], w_down[l])
    return x
```

```python
import functools
import math

import jax
import jax.numpy as jnp
import numpy as np
from jax import lax
from jax.experimental import pallas as pl
from jax.experimental.pallas import tpu as pltpu
from jax.experimental.pallas import tpu_sc as plsc

D_MODEL = 1024
N_HEADS = 8
N_KV_HEADS = 2
HEAD_DIM = 64
Q_PER_KV = N_HEADS // N_KV_HEADS
ATTN_WIDTH = N_HEADS * HEAD_DIM
KV_WIDTH = N_KV_HEADS * HEAD_DIM
WINDOW = 128
BLOCK = 128
NUM_BUCKETS = 32
MAX_DISTANCE = 128
LRU_WIDTH = D_MODEL - ATTN_WIDTH
LRU_BLOCKS = 8
LRU_BLOCK_DIM = LRU_WIDTH // LRU_BLOCKS
LRU_C = 8.0
CONV_W = 4
CONV_LEFT = 2
N_GROUPS = 4
EXPERTS_PER_GROUP = 8
N_EXPERTS = N_GROUPS * EXPERTS_PER_GROUP
TOP_K = 2
D_EXPERT = 512
MOE_BLOCK = 256
EPS = 1e-6
NEG_INF = -1e30

LANES = 128
SUBLANES = 8
VMEM_LIMIT = 56 * 1024 * 1024

F32 = jnp.float32
BF16 = jnp.bfloat16


def _cparams(n_axes, vmem=VMEM_LIMIT):
    return pltpu.CompilerParams(dimension_semantics=("arbitrary",) * n_axes, vmem_limit_bytes=vmem)


def _rms(x, gain):
    return x * lax.rsqrt(jnp.mean(x * x, axis=-1, keepdims=True) + EPS) * gain


U32 = jnp.uint32
HI_MASK = 0xFFFF0000
PACKED = D_MODEL // 2


def _pack_rows(x):
    h = x.shape[1] // 2
    lo = lax.bitcast_convert_type(x[:, :h].astype(BF16).astype(F32), U32) >> 16
    hi = lax.bitcast_convert_type(x[:, h:].astype(BF16).astype(F32), U32) & jnp.uint32(HI_MASK)
    return lo | hi


def _unpack_rows(p):
    lo = lax.bitcast_convert_type(p << 16, F32)
    hi = lax.bitcast_convert_type(p & jnp.uint32(HI_MASK), F32)
    return lo, hi


IN_TM = 1024


def _head_rms(x, n_heads, gain):
    head = lax.broadcasted_iota(jnp.int32, (1, n_heads * HEAD_DIM), 1) // HEAD_DIM
    x2 = x * x
    scale = jnp.zeros_like(x)
    for h in range(n_heads):
        ms = jnp.sum(jnp.where(head == h, x2, 0.0), axis=-1, keepdims=True) * (1.0 / HEAD_DIM)
        scale = jnp.where(head == h, lax.rsqrt(ms + EPS), scale)
    return x * scale * gain


def _in_proj_kernel(x_ref, g_ref, w_ref, qg_ref, kg_ref, q_ref, kv_ref, xr_ref, gr_ref, wb_ref):
    @pl.when(pl.program_id(0) == 0)
    def _():
        wb_ref[...] = w_ref[...].astype(BF16)

    h = _rms(x_ref[...], g_ref[...]).astype(BF16)
    c_k = ATTN_WIDTH
    c_v = c_k + KV_WIDTH
    c_x = c_v + KV_WIDTH
    c_g = c_x + LRU_WIDTH
    q = jnp.dot(h, wb_ref[:, :c_k], preferred_element_type=F32)
    q_ref[...] = _head_rms(q, N_HEADS, qg_ref[...]).astype(BF16)
    k = jnp.dot(h, wb_ref[:, c_k:c_v], preferred_element_type=F32)
    kv_ref[:, :KV_WIDTH] = _head_rms(k, N_KV_HEADS, kg_ref[...]).astype(BF16)
    kv_ref[:, KV_WIDTH:] = jnp.dot(h, wb_ref[:, c_v:c_x], preferred_element_type=F32).astype(BF16)
    xr_ref[...] = jnp.dot(h, wb_ref[:, c_x:c_g], preferred_element_type=F32)
    gr_ref[...] = jnp.dot(h, wb_ref[:, c_g:], preferred_element_type=F32)


def _in_proj(x2, ln1, w_in, q_gain, k_gain):
    T = x2.shape[0]
    n_in = w_in.shape[1]
    row = lambda w: pl.BlockSpec((IN_TM, w), lambda i: (i, 0))
    qg = (jnp.tile(q_gain.astype(F32), N_HEADS) * (HEAD_DIM ** -0.5)).reshape(1, ATTN_WIDTH)
    kg = jnp.tile(k_gain.astype(F32), N_KV_HEADS).reshape(1, KV_WIDTH)
    return pl.pallas_call(
        _in_proj_kernel,
        grid=(T // IN_TM,),
        in_specs=[row(D_MODEL),
                  pl.BlockSpec((1, D_MODEL), lambda i: (0, 0)),
                  pl.BlockSpec((D_MODEL, n_in), lambda i: (0, 0)),
                  pl.BlockSpec((1, ATTN_WIDTH), lambda i: (0, 0)),
                  pl.BlockSpec((1, KV_WIDTH), lambda i: (0, 0))],
        out_specs=[row(ATTN_WIDTH), row(2 * KV_WIDTH), row(LRU_WIDTH), row(LRU_WIDTH)],
        out_shape=[jax.ShapeDtypeStruct((T, ATTN_WIDTH), BF16),
                   jax.ShapeDtypeStruct((T, 2 * KV_WIDTH), BF16),
                   jax.ShapeDtypeStruct((T, LRU_WIDTH), F32),
                   jax.ShapeDtypeStruct((T, LRU_WIDTH), F32)],
        scratch_shapes=[pltpu.VMEM((D_MODEL, n_in), BF16)],
        compiler_params=_cparams(1),
        name="in_proj",
    )(x2, ln1.reshape(1, D_MODEL), w_in, qg, kg)


def _t5_bucket(rel):
    half = NUM_BUCKETS // 2
    max_exact = half // 2
    base = jnp.where(rel > 0, half, 0)
    n = jnp.abs(rel)
    nf = jnp.maximum(n, 1).astype(jnp.float32)
    large = max_exact + (jnp.log(nf / max_exact) / math.log(MAX_DISTANCE / max_exact)
                         * (half - max_exact)).astype(jnp.int32)
    large = jnp.minimum(large, half - 1)
    return base + jnp.where(n < max_exact, n, large)


HEAD_PAIRS = Q_PER_KV // 2
EDGE_VARIANTS = 3


def _bias_kernel(rb_ref, bucket_ref, band_ref, o_ref):
    bucket = bucket_ref[...]
    band = band_ref[...] > 0
    col = lax.broadcasted_iota(jnp.int32, bucket.shape, 1)
    valid = (band & (col >= BLOCK), band, band & (col < 2 * BLOCK))
    for h in range(N_HEADS):
        acc = jnp.zeros(bucket.shape, F32)
        for b in range(NUM_BUCKETS):
            acc = jnp.where(bucket == b, rb_ref[b, h], acc)
        kv, g = divmod(h, Q_PER_KV)
        pair, parity = divmod(g, 2)
        for var in range(EDGE_VARIANTS):
            o_ref[var, kv, parity, pair * BLOCK:(pair + 1) * BLOCK, :] = jnp.where(valid[var], acc, NEG_INF)


def _bias_table(rel_bias):
    qi = jnp.arange(BLOCK, dtype=jnp.int32)
    kj = jnp.arange(3 * BLOCK, dtype=jnp.int32)
    rel = kj[None, :] - BLOCK - qi[:, None]
    bucket = _t5_bucket(rel).astype(jnp.int32)
    band = (jnp.abs(rel) <= WINDOW).astype(jnp.int32)
    return pl.pallas_call(
        _bias_kernel,
        in_specs=[pl.BlockSpec(memory_space=pltpu.SMEM),
                  pl.BlockSpec(memory_space=pltpu.VMEM),
                  pl.BlockSpec(memory_space=pltpu.VMEM)],
        out_specs=pl.BlockSpec(memory_space=pltpu.VMEM),
        out_shape=jax.ShapeDtypeStruct((EDGE_VARIANTS, N_KV_HEADS, 2, HEAD_PAIRS * BLOCK, 3 * BLOCK), F32),
        name="bias_table",
    )(rel_bias.astype(F32), bucket, band)


def _attn_kernel(sink_ref, q_ref, kp_ref, kc_ref, kn_ref, bias_ref, og_ref, o_ref):
    n = pl.program_id(1)
    kv_all = jnp.concatenate([kp_ref[0], kc_ref[0], kn_ref[0]], axis=0)
    for qb in range(ATTN_QB):
        variant = 1
        if qb == 0:
            variant = jnp.where(n == 0, 0, 1)
        if qb == ATTN_QB - 1:
            variant = jnp.where(n == pl.num_programs(1) - 1, 2, variant)
        out = _attn_block(q_ref[0, qb * BLOCK:(qb + 1) * BLOCK, :], kv_all[qb * BLOCK:(qb + 3) * BLOCK, :],
                          lambda kv, parity: bias_ref[variant, kv, parity], sink_ref)
        o_ref[0, qb * BLOCK:(qb + 1) * BLOCK, :] = _rms(out, og_ref[...]).astype(o_ref.dtype)


def _attn_block(q, kvw, bias, sink_ref):
    low = lax.broadcasted_iota(jnp.int32, (1, LANES), 1) < HEAD_DIM
    swap = lambda slab: pltpu.roll(slab.astype(F32), HEAD_DIM, 1).astype(BF16)
    kslab, vslab = kvw[:, :KV_WIDTH], kvw[:, KV_WIDTH:]
    kslab_sw, vslab_sw = swap(kslab), swap(vslab)
    rowi = lax.broadcasted_iota(jnp.int32, (HEAD_PAIRS * BLOCK, 1), 0)
    combos = [(kv, parity) for kv in range(N_KV_HEADS) for parity in range(2)]
    scores, vzs, sinks = [], [], []
    for kv, parity in combos:
        ks, vs = (kslab, vslab) if (kv == 0) == (parity == 0) else (kslab_sw, vslab_sw)
        keep = low if parity == 0 else jnp.logical_not(low)
        kz = jnp.where(keep, ks, jnp.zeros_like(ks))
        vzs.append(jnp.where(keep, vs, jnp.zeros_like(vs)))
        base = kv * Q_PER_KV * HEAD_DIM
        qpair = jnp.concatenate([q[:, base + j * LANES:base + (j + 1) * LANES] for j in range(HEAD_PAIRS)], axis=0)
        s = lax.dot_general(qpair, kz, (((1,), (1,)), ((), ())), preferred_element_type=F32)
        scores.append(s + bias(kv, parity))
        sink = jnp.zeros((HEAD_PAIRS * BLOCK, 1), F32)
        for j in range(HEAD_PAIRS):
            sink = jnp.where(rowi // BLOCK == j, sink_ref[kv * Q_PER_KV + 2 * j + parity], sink)
        sinks.append(sink)
    probs, inv = [], []
    for s, sink in zip(scores, sinks):
        m = jnp.maximum(jnp.max(s, axis=-1, keepdims=True), sink)
        p = jnp.exp(s - m)
        inv.append(1.0 / (jnp.sum(p, axis=-1, keepdims=True) + jnp.exp(sink - m)))
        probs.append(p.astype(BF16))
    outs = [jnp.dot(p, vz, preferred_element_type=F32) * r for p, vz, r in zip(probs, vzs, inv)]
    cols = []
    for kv in range(N_KV_HEADS):
        acc = outs[2 * kv] + outs[2 * kv + 1]
        cols += [acc[j * BLOCK:(j + 1) * BLOCK, :] for j in range(HEAD_PAIRS)]
    return jnp.concatenate(cols, axis=1)


ATTN_QB = 4


def _attention(q, kv, bias_tab, sink, out_gain):
    B, S, _ = q.shape
    nb = S // BLOCK
    assert ATTN_QB >= 2 and nb % ATTN_QB == 0, "a step's first and last query blocks must be distinct"
    ns = nb // ATTN_QB
    rows = ATTN_QB * BLOCK
    kvspec = lambda f: pl.BlockSpec((1, BLOCK, 2 * KV_WIDTH), f)
    return pl.pallas_call(
        _attn_kernel,
        grid=(B, ns),
        in_specs=[pl.BlockSpec(memory_space=pltpu.SMEM),
                  pl.BlockSpec((1, rows, ATTN_WIDTH), lambda b, n: (b, n, 0)),
                  kvspec(lambda b, n: (b, jnp.maximum(n * ATTN_QB - 1, 0), 0)),
                  pl.BlockSpec((1, rows, 2 * KV_WIDTH), lambda b, n: (b, n, 0)),
                  kvspec(lambda b, n: (b, jnp.minimum((n + 1) * ATTN_QB, nb - 1), 0)),
                  pl.BlockSpec((EDGE_VARIANTS, N_KV_HEADS, 2, HEAD_PAIRS * BLOCK, 3 * BLOCK),
                               lambda b, n: (0, 0, 0, 0, 0)),
                  pl.BlockSpec((1, ATTN_WIDTH), lambda b, n: (0, 0))],
        out_specs=pl.BlockSpec((1, rows, ATTN_WIDTH), lambda b, n: (b, n, 0)),
        out_shape=jax.ShapeDtypeStruct((B, S, ATTN_WIDTH), BF16),
        compiler_params=_cparams(2),
        name="attention",
    )(sink.astype(F32), q, kv, kv, kv, bias_tab, out_gain.reshape(1, ATTN_WIDTH))


LRU_TC = 128
LRU_PITCH = LRU_TC + SUBLANES
LRU_SLABS = LRU_WIDTH // LANES
HALO = SUBLANES


def _softplus(x):
    return jnp.maximum(x, 0.0) + jnp.log(1.0 + jnp.exp(-jnp.abs(x)))


def _gelu_tanh(x):
    k = math.sqrt(2.0 / math.pi)
    hx = 0.5 * x
    return hx + hx * jnp.tanh(x * (k + (k * 0.044715) * (x * x)))


def _sigmoid(x):
    return 0.5 + 0.5 * jnp.tanh(0.5 * x)


def _rglru_kernel(xr_ref, xp_ref, xn_ref, gr_ref, cw_ref, cb_ref, wg_ref, bg_ref, lam_ref, og_ref,
                  o_ref, sx_ref, a_ref, u_ref, h_ref, carry_ref, hf_ref):
    p = pl.program_id(0)
    i = pl.program_id(1)
    nc = pl.num_programs(1)
    c = i + p * (nc - 1 - 2 * i)
    B = xr_ref.shape[0]
    TC = LRU_TC

    sx_ref[:, HALO:HALO + TC, :] = xr_ref[...]
    sx_ref[:, 0:HALO, :] = jnp.where(c > 0, xp_ref[...], 0.0)
    sx_ref[:, HALO + TC:, :] = jnp.where(c < nc - 1, xn_ref[...], 0.0)
    xc = cb_ref[...][None]
    for j in range(CONV_W):
        off = HALO + j - CONV_LEFT
        xc = xc + cw_ref[j:j + 1, :][None] * sx_ref[:, off:off + TC, :]
    xc2 = xc.reshape(B * TC, LRU_WIDTH)

    g = jnp.dot(xc2.astype(BF16), wg_ref[0], preferred_element_type=F32) + bg_ref[0]
    r = _sigmoid(g[:, :LRU_WIDTH])
    ig = _sigmoid(g[:, LRU_WIDTH:])
    a = jnp.exp(r * (-LRU_C * _softplus(-lam_ref[0])))
    z = 1.0 - a * a
    u = z * lax.rsqrt(jnp.maximum(z, 1e-30)) * ig * xc2
    for b in range(B):
        for s in range(LRU_SLABS):
            a_ref[s, b * LRU_PITCH:b * LRU_PITCH + TC, :] = a[b * TC:(b + 1) * TC, s * LANES:(s + 1) * LANES]
            u_ref[s, b * LRU_PITCH:b * LRU_PITCH + TC, :] = u[b * TC:(b + 1) * TC, s * LANES:(s + 1) * LANES]

    @pl.when(i == 0)
    def _():
        carry_ref[...] = jnp.zeros_like(carry_ref)

    def step(k, hs):
        t = k + p * (TC - 1 - 2 * k)
        out = []
        for s in range(LRU_SLABS):
            idx = pl.ds(t, B, stride=LRU_PITCH)
            hn = a_ref[s, idx, :] * hs[s] + u_ref[s, idx, :]
            h_ref[s, idx, :] = hn
            out.append(hn)
        return tuple(out)

    hs = lax.fori_loop(0, TC, step, tuple(carry_ref[s] for s in range(LRU_SLABS)), unroll=8)
    for s in range(LRU_SLABS):
        carry_ref[s] = hs[s]

    @pl.when(p == 0)
    def _():
        for b in range(B):
            for s in range(LRU_SLABS):
                hf_ref[c, s, b * TC:(b + 1) * TC, :] = h_ref[s, b * LRU_PITCH:b * LRU_PITCH + TC, :].astype(hf_ref.dtype)

    @pl.when(p == 1)
    def _():
        for b in range(B):
            hsum = jnp.concatenate(
                [h_ref[s, b * LRU_PITCH:b * LRU_PITCH + TC, :] + hf_ref[c, s, b * TC:(b + 1) * TC, :].astype(F32)
                 for s in range(LRU_SLABS)], axis=1)
            y = hsum * _gelu_tanh(gr_ref[b])
            o_ref[b] = _rms(y, og_ref[...]).astype(o_ref.dtype)


def _block_diag(w):
    eye = jnp.eye(LRU_BLOCKS, dtype=w.dtype)
    return jnp.einsum('hij,hg->higj', w, eye).reshape(LRU_WIDTH, LRU_WIDTH)


def _rglru(xr, gr, conv_w, conv_b, w_a, b_a, w_i, b_i, lam, out_gain):
    B, S, W = xr.shape
    nc = S // LRU_TC
    hb = LRU_TC // HALO
    wg = jnp.stack([jnp.concatenate([_block_diag(w_a[d]), _block_diag(w_i[d])], axis=1) for d in range(2)]).astype(BF16)
    bg = jnp.concatenate([b_a, b_i], axis=-1).reshape(2, 1, 2 * W).astype(F32)
    chunk = lambda p, i: i + p * (nc - 1 - 2 * i)
    full2 = lambda shape: pl.BlockSpec(shape, lambda p, i: (0,) * len(shape))
    return pl.pallas_call(
        _rglru_kernel,
        grid=(2, nc),
        in_specs=[pl.BlockSpec((B, LRU_TC, W), lambda p, i: (0, chunk(p, i), 0)),
                  pl.BlockSpec((B, HALO, W), lambda p, i: (0, jnp.maximum(chunk(p, i) * hb - 1, 0), 0)),
                  pl.BlockSpec((B, HALO, W), lambda p, i: (0, jnp.minimum((chunk(p, i) + 1) * hb, S // HALO - 1), 0)),
                  pl.BlockSpec((B, LRU_TC, W), lambda p, i: (0, chunk(p, i), 0)),
                  full2((CONV_W, W)),
                  full2((1, W)),
                  pl.BlockSpec((1, W, 2 * W), lambda p, i: (p, 0, 0)),
                  pl.BlockSpec((1, 1, 2 * W), lambda p, i: (p, 0, 0)),
                  pl.BlockSpec((1, 1, W), lambda p, i: (p, 0, 0)),
                  full2((1, W))],
        out_specs=pl.BlockSpec((B, LRU_TC, W), lambda p, i: (0, nc - 1 - p * i, 0)),
        out_shape=jax.ShapeDtypeStruct((B, S, W), BF16),
        scratch_shapes=[pltpu.VMEM((B, LRU_TC + 2 * HALO, W), F32),
                        pltpu.VMEM((LRU_SLABS, B * LRU_PITCH, LANES), F32),
                        pltpu.VMEM((LRU_SLABS, B * LRU_PITCH, LANES), F32),
                        pltpu.VMEM((LRU_SLABS, B * LRU_PITCH, LANES), F32),
                        pltpu.VMEM((LRU_SLABS, B, LANES), F32),
                        pltpu.VMEM((nc, LRU_SLABS, B * LRU_TC, LANES), BF16)],
        compiler_params=_cparams(2),
        name="rglru",
    )(xr, xr, xr, gr, conv_w.astype(F32), conv_b.reshape(1, W).astype(F32), wg, bg,
      lam.reshape(2, 1, W).astype(F32), out_gain.reshape(1, W).astype(F32))


RT_TM = 512
RT_COLS = LANES
RT_ROWS = 48
RINFO = SUBLANES


def _split_bf16(x):
    hi = x.astype(BF16)
    lo = (x - hi.astype(F32)).astype(BF16)
    return hi, lo


def _route_kernel(an_ref, ln_ref, x_ref, wo_ref, g2_ref, wr_ref, br_ref,
                  x1_ref, h2_ref, gt_ref, ei_ref, cnt_ref, wob_ref, wrb_ref, tri_ref, run_ref, runc_ref):
    @pl.when(pl.program_id(0) == 0)
    def _():
        wob_ref[...] = wo_ref[...].astype(BF16)
        hi, lo = _split_bf16(wr_ref[...])
        wrb_ref[:RT_ROWS, :] = hi
        wrb_ref[RT_ROWS:, :] = lo
        r = lax.broadcasted_iota(jnp.int32, (RT_TM, RT_TM), 0)
        cidx = lax.broadcasted_iota(jnp.int32, (RT_TM, RT_TM), 1)
        tri_ref[...] = (r < cidx).astype(BF16)
        run_ref[...] = jnp.zeros_like(run_ref)
        runc_ref[...] = jnp.zeros_like(runc_ref)

    x1 = (x_ref[...]
          + jnp.dot(an_ref[...], wob_ref[:ATTN_WIDTH, :], preferred_element_type=F32)
          + jnp.dot(ln_ref[...], wob_ref[ATTN_WIDTH:, :], preferred_element_type=F32))
    x1_ref[...] = x1
    h2 = _rms(x1, g2_ref[...])
    h2_ref[...] = _pack_rows(h2)

    hi, lo = _split_bf16(h2)
    nt_dims = (((1,), (1,)), ((), ()))
    t1 = lax.dot_general(wrb_ref[...], hi, nt_dims, preferred_element_type=F32)
    t2 = lax.dot_general(wrb_ref[:RT_ROWS, :], lo, nt_dims, preferred_element_type=F32)
    logit = t1[:RT_ROWS] + t1[RT_ROWS:] + t2 + br_ref[...]

    sub = lax.broadcasted_iota(jnp.int32, (SUBLANES, RT_TM), 0)
    first_min = lambda hit: jnp.min(jnp.where(hit, sub, SUBLANES), axis=0, keepdims=True)
    is_g = sub < N_GROUPS
    gl = jnp.where(is_g, logit[:SUBLANES], -jnp.inf)
    gm = jnp.max(gl, axis=0, keepdims=True)
    gidx = first_min(gl == gm)
    g_p = 1.0 / jnp.sum(jnp.where(is_g, jnp.exp(logit[:SUBLANES] - gm), 0.0), axis=0, keepdims=True)
    el = logit[SUBLANES:2 * SUBLANES]
    for g in range(1, N_GROUPS):
        el = jnp.where(gidx == g, logit[(g + 1) * SUBLANES:(g + 2) * SUBLANES], el)
    m1 = jnp.max(el, axis=0, keepdims=True)
    i1 = first_min(el == m1)
    el2 = jnp.where(sub == i1, -jnp.inf, el)
    m2 = jnp.max(el2, axis=0, keepdims=True)
    i2 = first_min(el2 == m2)
    t = jnp.exp(m2 - m1)
    gate1 = g_p / (1.0 + t)
    gate2 = g_p * t / (1.0 + t)
    e1 = gidx * EXPERTS_PER_GROUP + i1
    e2 = gidx * EXPERTS_PER_GROUP + i2

    erow = lax.broadcasted_iota(jnp.int32, (N_EXPERTS, RT_TM), 0)
    oh1 = erow == e1
    oh2 = erow == e2
    oh = (oh1 | oh2).astype(F32)
    ohb = oh.astype(BF16)
    cum = jnp.dot(ohb, tri_ref[...], preferred_element_type=F32) + runc_ref[...]
    rank1 = jnp.sum(jnp.where(oh1, cum, 0.0), axis=0, keepdims=True)
    rank2 = jnp.sum(jnp.where(oh2, cum, 0.0), axis=0, keepdims=True)
    runc_ref[...] = runc_ref[...] + jnp.sum(oh, axis=1, keepdims=True)
    tile_cnt = lax.dot_general(jnp.ones((SUBLANES, RT_TM), BF16), ohb, nt_dims, preferred_element_type=F32)
    run_ref[:, :N_EXPERTS] = run_ref[:, :N_EXPERTS] + tile_cnt[0:1]
    cnt_ref[...] = run_ref[...].astype(jnp.int32)

    rows = [e1, e2, rank1.astype(jnp.int32), rank2.astype(jnp.int32)]
    ei = jnp.zeros((RINFO, RT_TM), jnp.int32)
    for k, v in enumerate(rows):
        ei = jnp.where(sub == k, v, ei)
    ei_ref[0] = ei
    gt_ref[...] = jnp.where(sub == 0, gate1, jnp.where(sub == 1, gate2, 0.0)).T


def _out_route(attn_n, lru_n, x2, w_out, ln2, w_group, b_group, w_er, b_er):
    T = x2.shape[0]
    pad_g = SUBLANES - N_GROUPS
    wr = jnp.concatenate([jnp.pad(w_group.T, ((0, pad_g), (0, 0))),
                          jnp.transpose(w_er, (0, 2, 1)).reshape(N_EXPERTS, D_MODEL)], axis=0)
    wr = jnp.pad(wr, ((0, RT_ROWS - wr.shape[0]), (0, 0))).astype(F32)
    br = jnp.concatenate([jnp.pad(b_group, (0, pad_g)), b_er.reshape(-1)])
    br = jnp.pad(br, (0, RT_ROWS - br.shape[0])).reshape(RT_ROWS, 1).astype(F32)
    row = lambda w: pl.BlockSpec((RT_TM, w), lambda i: (i, 0))
    const = lambda shape: pl.BlockSpec(shape, lambda i: (0, 0))
    return pl.pallas_call(
        _route_kernel,
        grid=(T // RT_TM,),
        in_specs=[row(ATTN_WIDTH), row(LRU_WIDTH), row(D_MODEL), const((D_MODEL, D_MODEL)), const((1, D_MODEL)),
                  const((RT_ROWS, D_MODEL)), const((RT_ROWS, 1))],
        out_specs=[row(D_MODEL), row(PACKED), row(RINFO),
                   pl.BlockSpec((1, RINFO, RT_TM), lambda i: (i, 0, 0)), const((1, RT_COLS))],
        out_shape=[jax.ShapeDtypeStruct((T, D_MODEL), F32),
                   jax.ShapeDtypeStruct((T, PACKED), U32),
                   jax.ShapeDtypeStruct((T, RINFO), F32),
                   jax.ShapeDtypeStruct((T // RT_TM, RINFO, RT_TM), jnp.int32),
                   jax.ShapeDtypeStruct((1, RT_COLS), jnp.int32)],
        scratch_shapes=[pltpu.VMEM((D_MODEL, D_MODEL), BF16),
                        pltpu.VMEM((2 * RT_ROWS, D_MODEL), BF16),
                        pltpu.VMEM((RT_TM, RT_TM), BF16),
                        pltpu.VMEM((1, RT_COLS), F32),
                        pltpu.VMEM((N_EXPERTS, 1), F32)],
        compiler_params=_cparams(1),
        name="out_route",
    )(attn_n, lru_n, x2, w_out, ln2.reshape(1, D_MODEL).astype(F32), wr, br)


def _moe_cap(T):
    A = T * TOP_K
    return ((A + MOE_BLOCK - 1) // MOE_BLOCK) * MOE_BLOCK + N_EXPERTS * MOE_BLOCK


PAD_BITS = tuple(1 << b for b in reversed(range(3, MOE_BLOCK.bit_length() - 1)))


def _layout_kernel(cnt_ref, ei_ref, dest_ref, pstart, be_ref, nu_ref, ge_ref):
    n_blocks = be_ref.shape[0]

    def lay(e, carry):
        start, blk, grp = carry
        pstart[e] = start
        nb = (cnt_ref[0, e] + MOE_BLOCK - 1) // MOE_BLOCK
        ge_ref[grp] = e

        def fill(k, c):
            be_ref[blk + k] = e
            return c
        lax.fori_loop(0, nb, fill, 0)
        return start + nb * MOE_BLOCK, blk + nb, grp + (nb > 0).astype(jnp.int32)
    _, used, groups = lax.fori_loop(0, N_EXPERTS, lay, (jnp.int32(0), jnp.int32(0), jnp.int32(0)))
    nu_ref[0] = used

    def tail(k, c):
        be_ref[k] = N_EXPERTS - 1
        return c
    lax.fori_loop(used, n_blocks, tail, 0)

    def no_group(k, c):
        ge_ref[k] = -1
        return c
    lax.fori_loop(groups, N_EXPERTS + 1, no_group, 0)

    expert = ei_ref[:, 0:TOP_K, :]
    dest = ei_ref[:, TOP_K:2 * TOP_K, :]
    for e in range(N_EXPERTS):
        dest = dest + jnp.where(expert == e, pstart[e], 0)
    dest_ref[...] = dest


def _layout(ei, cnt, n_blocks):
    nt = ei.shape[0]
    smem = pl.BlockSpec(memory_space=pltpu.SMEM)
    vmem = pl.BlockSpec(memory_space=pltpu.VMEM)
    return pl.pallas_call(
        _layout_kernel,
        in_specs=[smem, vmem],
        out_specs=[vmem, smem, smem, smem, smem],
        out_shape=[jax.ShapeDtypeStruct((nt, TOP_K, RT_TM), jnp.int32),
                   jax.ShapeDtypeStruct((N_EXPERTS,), jnp.int32),
                   jax.ShapeDtypeStruct((n_blocks,), jnp.int32),
                   jax.ShapeDtypeStruct((1,), jnp.int32),
                   jax.ShapeDtypeStruct((N_EXPERTS + 1,), jnp.int32)],
        name="layout",
    )(cnt, ei)


SC_CHUNK = 64
SC_BUFS = 3
SC_LEAD = SC_BUFS - 1


def _sc_workers():
    info = plsc.get_sparse_core_info()
    return info.num_cores, info.num_subcores


def _sc_ring(n_chunks, read, write):
    for c in range(min(SC_LEAD, n_chunks)):
        for cp in read(c):
            cp.start()
    reclaimed = set()
    for c in range(n_chunks):
        for cp in read(c):
            cp.wait()
        for cp in write(c):
            cp.start()
        nxt = c + SC_LEAD
        if nxt < n_chunks:
            if nxt - SC_BUFS >= 0:
                for cp in write(nxt - SC_BUFS):
                    cp.wait()
                reclaimed.add(nxt - SC_BUFS)
            for cp in read(nxt):
                cp.start()
    for c in range(n_chunks):
        if c not in reclaimed:
            for cp in write(c):
                cp.wait()


def _sc_dispatch(h2p, dest, cap):
    T = h2p.shape[0]
    nc, ns = _sc_workers()
    per_w = T // (nc * ns)
    n_ch = per_w // SC_CHUNK
    assert dest.shape == (nc * ns, TOP_K, per_w) and per_w % SC_CHUNK == 0
    idx = dest.reshape(nc * ns, TOP_K * n_ch, SC_CHUNK)
    mesh = plsc.VectorSubcoreMesh(core_axis_name="c", subcore_axis_name="s")

    @functools.partial(
        pl.kernel, mesh=mesh,
        out_type=jax.ShapeDtypeStruct((cap, PACKED), U32),
        scratch_types=[pltpu.VMEM((TOP_K * n_ch, SC_CHUNK), jnp.int32),
                       pltpu.VMEM((SC_BUFS, SC_CHUNK, PACKED), U32),
                       pltpu.SemaphoreType.DMA((SC_BUFS,)),
                       pltpu.SemaphoreType.DMA((SC_BUFS,))])
    def scatter(src_hbm, idx_hbm, out_hbm, idx_v, rows_v, rsem, wsem):
        wid = lax.axis_index("s") * nc + lax.axis_index("c")
        base = pl.multiple_of(wid * per_w, per_w)
        pltpu.sync_copy(idx_hbm.at[wid], idx_v)

        def read(c):
            b = c % SC_BUFS
            return [pltpu.make_async_copy(src_hbm.at[pl.ds(base + c * SC_CHUNK, SC_CHUNK)], rows_v.at[b], rsem.at[b])]

        def write(c):
            b = c % SC_BUFS
            return [pltpu.make_async_copy(rows_v.at[b], out_hbm.at[idx_v.at[k * n_ch + c]], wsem.at[b])
                    for k in range(TOP_K)]
        _sc_ring(n_ch, read, write)

    return scatter(h2p, idx)


def _sc_gather(yb, dest):
    nt, _, tm = dest.shape
    nc, ns = _sc_workers()
    n_rows = nt * TOP_K * tm
    per_w = n_rows // (nc * ns)
    n_ch = per_w // SC_CHUNK
    assert per_w * nc * ns == n_rows and per_w % SC_CHUNK == 0
    mesh = plsc.VectorSubcoreMesh(core_axis_name="c", subcore_axis_name="s")

    @functools.partial(
        pl.kernel, mesh=mesh,
        out_type=jax.ShapeDtypeStruct((n_rows, PACKED), U32),
        scratch_types=[pltpu.VMEM((per_w,), jnp.int32),
                       pltpu.VMEM((SC_BUFS, SC_CHUNK, PACKED), U32),
                       pltpu.SemaphoreType.DMA((SC_BUFS,)),
                       pltpu.SemaphoreType.DMA((SC_BUFS,))])
    def gather(table_hbm, idx_hbm, out_hbm, idx_v, rows_v, gsem, wsem):
        wid = lax.axis_index("s") * nc + lax.axis_index("c")
        base = pl.multiple_of(wid * per_w, per_w)
        pltpu.sync_copy(idx_hbm.at[pl.ds(base, per_w)], idx_v)

        def read(c):
            b = c % SC_BUFS
            return [pltpu.make_async_copy(table_hbm.at[idx_v.at[pl.ds(c * SC_CHUNK, SC_CHUNK)]], rows_v.at[b], gsem.at[b])]

        def write(c):
            b = c % SC_BUFS
            return [pltpu.make_async_copy(rows_v.at[b], out_hbm.at[pl.ds(base + c * SC_CHUNK, SC_CHUNK)], wsem.at[b])]
        _sc_ring(n_ch, read, write)

    return gather(yb, dest.reshape(n_rows)).reshape(nt, TOP_K, tm, PACKED)


def _padfill_kernel(cnt_ref, pstart, xs_in, xs_ref, zeros, zsem):
    del xs_in

    def pad_copies(fn):
        for e in range(N_EXPERTS):
            cnt = cnt_ref[0, e]
            head = (-cnt) & (SUBLANES - 1)
            rest = ((-cnt) & (MOE_BLOCK - 1)) - head
            off = pstart[e] + cnt
            for k in range(SUBLANES - 1):
                @pl.when(k < head)
                def _(off=off, k=k):
                    fn(pltpu.make_async_copy(zeros.at[pl.ds(0, 1), :], xs_ref.at[pl.ds(off + k, 1), :], zsem))
            off = off + head
            for bit in PAD_BITS:
                @pl.when((rest & bit) != 0)
                def _(off=off, bit=bit):
                    fn(pltpu.make_async_copy(zeros.at[pl.ds(0, bit), :],
                                             xs_ref.at[pl.ds(pl.multiple_of(off, SUBLANES), bit), :], zsem))
                off = off + (rest & bit)

    zeros[...] = jnp.zeros_like(zeros)
    pad_copies(lambda cp: cp.start())
    pad_copies(lambda cp: cp.wait())


def _padfill(xs, pstart, cnt):
    smem = pl.BlockSpec(memory_space=pltpu.SMEM)
    hbm = pl.BlockSpec(memory_space=pl.ANY)
    return pl.pallas_call(
        _padfill_kernel,
        in_specs=[smem, smem, hbm],
        out_specs=hbm,
        out_shape=jax.ShapeDtypeStruct(xs.shape, xs.dtype),
        input_output_aliases={2: 0},
        scratch_shapes=[pltpu.VMEM((MOE_BLOCK // 2, PACKED), U32), pltpu.SemaphoreType.DMA(())],
        name="padfill",
    )(cnt, pstart, xs)


W_SLOTS = 2
EXPERT_GROUP = 8
EXPERT_RUNS = (1, 2, 4)


def _expert_kernel(be_ref, nu_ref, ge_ref, x_ref, wg_hbm, wu_hbm, wd_hbm, o_ref,
                   wgf, wuf, wdf, wgb, wub, wdb, grp_ref, sems):
    step = pl.program_id(0)

    def weight_copies(e, slot):
        return (pltpu.make_async_copy(wg_hbm.at[e], wgf.at[slot], sems.at[slot, 0]),
                pltpu.make_async_copy(wu_hbm.at[e], wuf.at[slot], sems.at[slot, 1]),
                pltpu.make_async_copy(wd_hbm.at[e], wdf.at[slot], sems.at[slot, 2]))

    @pl.when(step == 0)
    def _():
        grp_ref[0] = 0
        for cp in weight_copies(ge_ref[0], 0):
            cp.start()

    n_blocks = be_ref.shape[0]
    n_used = nu_ref[0]

    def swiglu(s, n):
        rows = pl.ds(pl.multiple_of(s * MOE_BLOCK, MOE_BLOCK), n * MOE_BLOCK)
        lo, hi = _unpack_rows(x_ref[rows, :])
        lo = lo.astype(BF16)
        hi = hi.astype(BF16)
        g = (jnp.dot(lo, wgb[:PACKED, :], preferred_element_type=F32)
             + jnp.dot(hi, wgb[PACKED:, :], preferred_element_type=F32))
        u = (jnp.dot(lo, wub[:PACKED, :], preferred_element_type=F32)
             + jnp.dot(hi, wub[PACKED:, :], preferred_element_type=F32))
        h = (g * _sigmoid(g) * u).astype(BF16)
        o_ref[rows, :] = _pack_rows(jnp.dot(h, wdb[...], preferred_element_type=F32))

    def run(s):
        j = step * EXPERT_GROUP + s
        e = be_ref[j]
        first = jnp.logical_or(j == 0, e != be_ref[jnp.maximum(j - 1, 0)])

        @pl.when(first)
        def _():
            grp = grp_ref[0]
            slot = grp % W_SLOTS
            for cp in weight_copies(e, slot):
                cp.wait()
            wgb[...] = wgf[slot].astype(BF16)
            wub[...] = wuf[slot].astype(BF16)
            wdb[...] = wdf[slot].astype(BF16)
            nxt = ge_ref[grp + 1]

            @pl.when(nxt >= 0)
            def _():
                for cp in weight_copies(nxt, 1 - slot):
                    cp.start()
            grp_ref[0] = grp + 1

        def same(k):
            return (s + k < EXPERT_GROUP) & (j + k < n_used) & (be_ref[jnp.minimum(j + k, n_blocks - 1)] == e)
        take = jnp.int32(1)
        for n in EXPERT_RUNS[1:]:
            ok = same(n - 1)
            for k in range(1, n - 1):
                ok = ok & same(k)
            take = jnp.where(ok, n, take)
        for n in EXPERT_RUNS:
            @pl.when(take == n)
            def _(n=n):
                swiglu(s, n)
        return s + take

    lax.while_loop(lambda s: (s < EXPERT_GROUP) & (step * EXPERT_GROUP + s < n_used), run, jnp.int32(0))


def _experts(xs, block_expert, n_used, group_expert, w_gate, w_up, w_down):
    cap = xs.shape[0]
    n_blocks = cap // MOE_BLOCK
    assert n_blocks % EXPERT_GROUP == 0
    rows = EXPERT_GROUP * MOE_BLOCK
    last = lambda j, be, nu, ge: jnp.minimum(j, (nu[0] - 1) // EXPERT_GROUP)
    hbm = pl.BlockSpec(memory_space=pl.ANY)
    gs = pltpu.PrefetchScalarGridSpec(
        num_scalar_prefetch=3,
        grid=(n_blocks // EXPERT_GROUP,),
        in_specs=[pl.BlockSpec((rows, PACKED), lambda j, be, nu, ge: (last(j, be, nu, ge), 0)), hbm, hbm, hbm],
        out_specs=pl.BlockSpec((rows, PACKED), lambda j, be, nu, ge: (last(j, be, nu, ge), 0)),
        scratch_shapes=[pltpu.VMEM((W_SLOTS, D_MODEL, D_EXPERT), F32),
                        pltpu.VMEM((W_SLOTS, D_MODEL, D_EXPERT), F32),
                        pltpu.VMEM((W_SLOTS, D_EXPERT, D_MODEL), F32),
                        pltpu.VMEM((D_MODEL, D_EXPERT), BF16),
                        pltpu.VMEM((D_MODEL, D_EXPERT), BF16),
                        pltpu.VMEM((D_EXPERT, D_MODEL), BF16),
                        pltpu.SMEM((1,), jnp.int32),
                        pltpu.SemaphoreType.DMA((W_SLOTS, 3))],
    )
    return pl.pallas_call(
        _expert_kernel,
        grid_spec=gs,
        out_shape=jax.ShapeDtypeStruct((cap, PACKED), U32),
        compiler_params=_cparams(1),
        name="experts",
    )(block_expert, n_used, group_expert, xs, w_gate, w_up, w_down)


CB_TM = RT_TM


def _combine_kernel(x1_ref, gt_ref, y2_ref, *rest):
    o_ref = rest[-1]
    g = gt_ref[...]
    lo1, hi1 = _unpack_rows(y2_ref[0, 0])
    lo2, hi2 = _unpack_rows(y2_ref[0, 1])
    o_ref[:, :PACKED] = x1_ref[:, :PACKED] + g[:, 0:1] * lo1 + g[:, 1:2] * lo2
    o_ref[:, PACKED:] = x1_ref[:, PACKED:] + g[:, 0:1] * hi1 + g[:, 1:2] * hi2


COMBINE_PARTS = 2


def _combine_part(x1, gates, y2, part, prev_out):
    T = x1.shape[0]
    n_tiles = y2.shape[0]
    first = part * n_tiles
    in_specs = [pl.BlockSpec((CB_TM, D_MODEL), lambda i: (first + i, 0)),
                pl.BlockSpec((CB_TM, RINFO), lambda i: (first + i, 0)),
                pl.BlockSpec((1, TOP_K, CB_TM, PACKED), lambda i: (i, 0, 0, 0))]
    args = [x1, gates, y2]
    aliases = {}
    if prev_out is not None:
        in_specs.append(pl.BlockSpec(memory_space=pl.ANY))
        args.append(prev_out)
        aliases = {3: 0}
    return pl.pallas_call(
        _combine_kernel,
        grid=(n_tiles,),
        in_specs=in_specs,
        out_specs=pl.BlockSpec((CB_TM, D_MODEL), lambda i: (first + i, 0)),
        out_shape=jax.ShapeDtypeStruct((T, D_MODEL), F32),
        input_output_aliases=aliases,
        compiler_params=_cparams(1),
        name="combine",
    )(*args)


def _combine(x1, gates, yb, dest):
    nt = dest.shape[0]
    assert nt % COMBINE_PARTS == 0
    per = nt // COMBINE_PARTS
    out = None
    for part in range(COMBINE_PARTS):
        y2 = _sc_gather(yb, dest[part * per:(part + 1) * per])
        out = _combine_part(x1, gates, y2, part, out)
    return out


def _layer(x, rel_bias, ln1, w_in, q_norm, k_norm, attn_sink, conv_w, conv_b, lru_wa, lru_ba, lru_wi, lru_bi,
           lru_lambda, out_norm_attn, out_norm_lru, w_out, ln2, w_group, b_group, w_er, b_er, w_gate, w_up, w_down):
    B, S, D = x.shape
    T = B * S
    x2 = x.reshape(T, D)
    q, kv, xr, gr = _in_proj(x2, ln1, w_in, q_norm, k_norm)
    bias_tab = _bias_table(rel_bias)
    attn_n = _attention(q.reshape(B, S, ATTN_WIDTH), kv.reshape(B, S, 2 * KV_WIDTH), bias_tab, attn_sink,
                        out_norm_attn)
    lru_n = _rglru(xr.reshape(B, S, LRU_WIDTH), gr.reshape(B, S, LRU_WIDTH), conv_w, conv_b,
                   lru_wa, lru_ba, lru_wi, lru_bi, lru_lambda, out_norm_lru)
    x1, h2, gates, ei, cnt = _out_route(attn_n.reshape(T, ATTN_WIDTH), lru_n.reshape(T, LRU_WIDTH), x2, w_out, ln2,
                                        w_group, b_group, w_er, b_er)
    cap = _moe_cap(T)
    dest, pstart, block_expert, n_used, group_expert = _layout(ei, cnt, cap // MOE_BLOCK)
    xs = _padfill(_sc_dispatch(h2, dest, cap), pstart, cnt)
    yb = _experts(xs, block_expert, n_used, group_expert, w_gate, w_up, w_down)
    out = _combine(x1, gates, yb, dest)
    return out.reshape(B, S, D)


def kernel(x, rel_bias, ln1, w_in, q_norm, k_norm, attn_sink, conv_w, conv_b, lru_wa, lru_ba, lru_wi, lru_bi,
           lru_lambda, out_norm_attn, out_norm_lru, w_out, ln2, w_group, b_group, w_expert_router, b_expert_router,
           w_gate, w_up, w_down):
    depth = ln1.shape[0]
    for l in range(depth):
        x = _layer(x, rel_bias, ln1[l], w_in[l], q_norm[l], k_norm[l], attn_sink[l], conv_w[l], conv_b[l],
                   lru_wa[l], lru_ba[l], lru_wi[l], lru_bi[l], lru_lambda[l], out_norm_attn[l], out_norm_lru[l],
                   w_out[l], ln2[l], w_group[l], b_group[l], w_expert_router[l], b_expert_router[l],
                   w_gate[l], w_up[l], w_down[l])
    return x
```

```python
import functools
import math

import jax
import jax.numpy as jnp
import numpy as np
from jax import lax
from jax.experimental import pallas as pl
from jax.experimental.pallas import tpu as pltpu
from jax.experimental.pallas import tpu_sc as plsc

D_MODEL = 1024
N_HEADS = 8
N_KV_HEADS = 2
HEAD_DIM = 64
Q_PER_KV = N_HEADS // N_KV_HEADS
ATTN_WIDTH = N_HEADS * HEAD_DIM
KV_WIDTH = N_KV_HEADS * HEAD_DIM
WINDOW = 128
BLOCK = 128
NUM_BUCKETS = 32
MAX_DISTANCE = 128
LRU_WIDTH = D_MODEL - ATTN_WIDTH
LRU_BLOCKS = 8
LRU_BLOCK_DIM = LRU_WIDTH // LRU_BLOCKS
LRU_C = 8.0
CONV_W = 4
CONV_LEFT = 2
N_GROUPS = 4
EXPERTS_PER_GROUP = 8
N_EXPERTS = N_GROUPS * EXPERTS_PER_GROUP
TOP_K = 2
D_EXPERT = 512
MOE_BLOCK = 256
EPS = 1e-6
NEG_INF = -1e30

LANES = 128
SUBLANES = 8
VMEM_LIMIT = 56 * 1024 * 1024

F32 = jnp.float32
BF16 = jnp.bfloat16


def _cparams(n_axes, vmem=VMEM_LIMIT):
    return pltpu.CompilerParams(dimension_semantics=("arbitrary",) * n_axes, vmem_limit_bytes=vmem)


def _rms(x, gain):
    return x * lax.rsqrt(jnp.mean(x * x, axis=-1, keepdims=True) + EPS) * gain


U32 = jnp.uint32
HI_MASK = 0xFFFF0000
PACKED = D_MODEL // 2


def _pack_rows(x):
    h = x.shape[1] // 2
    lo = lax.bitcast_convert_type(x[:, :h].astype(BF16).astype(F32), U32) >> 16
    hi = lax.bitcast_convert_type(x[:, h:].astype(BF16).astype(F32), U32) & jnp.uint32(HI_MASK)
    return lo | hi


def _unpack_rows(p):
    lo = lax.bitcast_convert_type(p << 16, F32)
    hi = lax.bitcast_convert_type(p & jnp.uint32(HI_MASK), F32)
    return lo, hi


IN_TM = 1024


def _head_rms(x, n_heads, gain):
    head = lax.broadcasted_iota(jnp.int32, (1, n_heads * HEAD_DIM), 1) // HEAD_DIM
    x2 = x * x
    scale = jnp.zeros_like(x)
    for h in range(n_heads):
        ms = jnp.sum(jnp.where(head == h, x2, 0.0), axis=-1, keepdims=True) * (1.0 / HEAD_DIM)
        scale = jnp.where(head == h, lax.rsqrt(ms + EPS), scale)
    return x * scale * gain


def _in_proj_kernel(x_ref, g_ref, w_ref, qg_ref, kg_ref, q_ref, kv_ref, xr_ref, gr_ref, wb_ref):
    @pl.when(pl.program_id(0) == 0)
    def _():
        wb_ref[...] = w_ref[...].astype(BF16)

    h = _rms(x_ref[...], g_ref[...]).astype(BF16)
    c_k = ATTN_WIDTH
    c_v = c_k + KV_WIDTH
    c_x = c_v + KV_WIDTH
    c_g = c_x + LRU_WIDTH
    q = jnp.dot(h, wb_ref[:, :c_k], preferred_element_type=F32)
    q_ref[...] = _head_rms(q, N_HEADS, qg_ref[...]).astype(BF16)
    k = jnp.dot(h, wb_ref[:, c_k:c_v], preferred_element_type=F32)
    kv_ref[:, :KV_WIDTH] = _head_rms(k, N_KV_HEADS, kg_ref[...]).astype(BF16)
    kv_ref[:, KV_WIDTH:] = jnp.dot(h, wb_ref[:, c_v:c_x], preferred_element_type=F32).astype(BF16)
    xr_ref[...] = jnp.dot(h, wb_ref[:, c_x:c_g], preferred_element_type=F32)
    gr_ref[...] = jnp.dot(h, wb_ref[:, c_g:], preferred_element_type=F32)


def _in_proj(x2, ln1, w_in, q_gain, k_gain):
    T = x2.shape[0]
    n_in = w_in.shape[1]
    row = lambda w: pl.BlockSpec((IN_TM, w), lambda i: (i, 0))
    qg = (jnp.tile(q_gain.astype(F32), N_HEADS) * (HEAD_DIM ** -0.5)).reshape(1, ATTN_WIDTH)
    kg = jnp.tile(k_gain.astype(F32), N_KV_HEADS).reshape(1, KV_WIDTH)
    return pl.pallas_call(
        _in_proj_kernel,
        grid=(T // IN_TM,),
        in_specs=[row(D_MODEL),
                  pl.BlockSpec((1, D_MODEL), lambda i: (0, 0)),
                  pl.BlockSpec((D_MODEL, n_in), lambda i: (0, 0)),
                  pl.BlockSpec((1, ATTN_WIDTH), lambda i: (0, 0)),
                  pl.BlockSpec((1, KV_WIDTH), lambda i: (0, 0))],
        out_specs=[row(ATTN_WIDTH), row(2 * KV_WIDTH), row(LRU_WIDTH), row(LRU_WIDTH)],
        out_shape=[jax.ShapeDtypeStruct((T, ATTN_WIDTH), BF16),
                   jax.ShapeDtypeStruct((T, 2 * KV_WIDTH), BF16),
                   jax.ShapeDtypeStruct((T, LRU_WIDTH), F32),
                   jax.ShapeDtypeStruct((T, LRU_WIDTH), F32)],
        scratch_shapes=[pltpu.VMEM((D_MODEL, n_in), BF16)],
        compiler_params=_cparams(1),
        name="in_proj",
    )(x2, ln1.reshape(1, D_MODEL), w_in, qg, kg)


def _t5_bucket(rel):
    half = NUM_BUCKETS // 2
    max_exact = half // 2
    base = jnp.where(rel > 0, half, 0)
    n = jnp.abs(rel)
    nf = jnp.maximum(n, 1).astype(jnp.float32)
    large = max_exact + (jnp.log(nf / max_exact) / math.log(MAX_DISTANCE / max_exact)
                         * (half - max_exact)).astype(jnp.int32)
    large = jnp.minimum(large, half - 1)
    return base + jnp.where(n < max_exact, n, large)


HEAD_PAIRS = Q_PER_KV // 2
EDGE_VARIANTS = 3


def _bias_kernel(rb_ref, bucket_ref, band_ref, o_ref):
    bucket = bucket_ref[...]
    band = band_ref[...] > 0
    col = lax.broadcasted_iota(jnp.int32, bucket.shape, 1)
    valid = (band & (col >= BLOCK), band, band & (col < 2 * BLOCK))
    for h in range(N_HEADS):
        acc = jnp.zeros(bucket.shape, F32)
        for b in range(NUM_BUCKETS):
            acc = jnp.where(bucket == b, rb_ref[b, h], acc)
        kv, g = divmod(h, Q_PER_KV)
        pair, parity = divmod(g, 2)
        for var in range(EDGE_VARIANTS):
            o_ref[var, kv, parity, pair * BLOCK:(pair + 1) * BLOCK, :] = jnp.where(valid[var], acc, NEG_INF)


def _bias_table(rel_bias):
    qi = jnp.arange(BLOCK, dtype=jnp.int32)
    kj = jnp.arange(3 * BLOCK, dtype=jnp.int32)
    rel = kj[None, :] - BLOCK - qi[:, None]
    bucket = _t5_bucket(rel).astype(jnp.int32)
    band = (jnp.abs(rel) <= WINDOW).astype(jnp.int32)
    return pl.pallas_call(
        _bias_kernel,
        in_specs=[pl.BlockSpec(memory_space=pltpu.SMEM),
                  pl.BlockSpec(memory_space=pltpu.VMEM),
                  pl.BlockSpec(memory_space=pltpu.VMEM)],
        out_specs=pl.BlockSpec(memory_space=pltpu.VMEM),
        out_shape=jax.ShapeDtypeStruct((EDGE_VARIANTS, N_KV_HEADS, 2, HEAD_PAIRS * BLOCK, 3 * BLOCK), F32),
        name="bias_table",
    )(rel_bias.astype(F32), bucket, band)


def _attn_kernel(sink_ref, q_ref, kp_ref, kc_ref, kn_ref, bias_ref, og_ref, o_ref):
    n = pl.program_id(1)
    kv_all = jnp.concatenate([kp_ref[0], kc_ref[0], kn_ref[0]], axis=0)
    for qb in range(ATTN_QB):
        variant = 1
        if qb == 0:
            variant = jnp.where(n == 0, 0, 1)
        if qb == ATTN_QB - 1:
            variant = jnp.where(n == pl.num_programs(1) - 1, 2, variant)
        out = _attn_block(q_ref[0, qb * BLOCK:(qb + 1) * BLOCK, :], kv_all[qb * BLOCK:(qb + 3) * BLOCK, :],
                          lambda kv, parity: bias_ref[variant, kv, parity], sink_ref)
        o_ref[0, qb * BLOCK:(qb + 1) * BLOCK, :] = _rms(out, og_ref[...]).astype(o_ref.dtype)


def _attn_block(q, kvw, bias, sink_ref):
    low = lax.broadcasted_iota(jnp.int32, (1, LANES), 1) < HEAD_DIM
    swap = lambda slab: pltpu.roll(slab.astype(F32), HEAD_DIM, 1).astype(BF16)
    kslab, vslab = kvw[:, :KV_WIDTH], kvw[:, KV_WIDTH:]
    kslab_sw, vslab_sw = swap(kslab), swap(vslab)
    rowi = lax.broadcasted_iota(jnp.int32, (HEAD_PAIRS * BLOCK, 1), 0)
    combos = [(kv, parity) for kv in range(N_KV_HEADS) for parity in range(2)]
    scores, vzs, sinks = [], [], []
    for kv, parity in combos:
        ks, vs = (kslab, vslab) if (kv == 0) == (parity == 0) else (kslab_sw, vslab_sw)
        keep = low if parity == 0 else jnp.logical_not(low)
        kz = jnp.where(keep, ks, jnp.zeros_like(ks))
        vzs.append(jnp.where(keep, vs, jnp.zeros_like(vs)))
        base = kv * Q_PER_KV * HEAD_DIM
        qpair = jnp.concatenate([q[:, base + j * LANES:base + (j + 1) * LANES] for j in range(HEAD_PAIRS)], axis=0)
        s = lax.dot_general(qpair, kz, (((1,), (1,)), ((), ())), preferred_element_type=F32)
        scores.append(s + bias(kv, parity))
        sink = jnp.zeros((HEAD_PAIRS * BLOCK, 1), F32)
        for j in range(HEAD_PAIRS):
            sink = jnp.where(rowi // BLOCK == j, sink_ref[kv * Q_PER_KV + 2 * j + parity], sink)
        sinks.append(sink)
    probs, inv = [], []
    for s, sink in zip(scores, sinks):
        m = jnp.maximum(jnp.max(s, axis=-1, keepdims=True), sink)
        p = jnp.exp(s - m)
        inv.append(1.0 / (jnp.sum(p, axis=-1, keepdims=True) + jnp.exp(sink - m)))
        probs.append(p.astype(BF16))
    outs = [jnp.dot(p, vz, preferred_element_type=F32) * r for p, vz, r in zip(probs, vzs, inv)]
    cols = []
    for kv in range(N_KV_HEADS):
        acc = outs[2 * kv] + outs[2 * kv + 1]
        cols += [acc[j * BLOCK:(j + 1) * BLOCK, :] for j in range(HEAD_PAIRS)]
    return jnp.concatenate(cols, axis=1)


ATTN_QB = 4


def _attention(q, kv, bias_tab, sink, out_gain):
    B, S, _ = q.shape
    nb = S // BLOCK
    assert ATTN_QB >= 2 and nb % ATTN_QB == 0, "a step's first and last query blocks must be distinct"
    ns = nb // ATTN_QB
    rows = ATTN_QB * BLOCK
    kvspec = lambda f: pl.BlockSpec((1, BLOCK, 2 * KV_WIDTH), f)
    return pl.pallas_call(
        _attn_kernel,
        grid=(B, ns),
        in_specs=[pl.BlockSpec(memory_space=pltpu.SMEM),
                  pl.BlockSpec((1, rows, ATTN_WIDTH), lambda b, n: (b, n, 0)),
                  kvspec(lambda b, n: (b, jnp.maximum(n * ATTN_QB - 1, 0), 0)),
                  pl.BlockSpec((1, rows, 2 * KV_WIDTH), lambda b, n: (b, n, 0)),
                  kvspec(lambda b, n: (b, jnp.minimum((n + 1) * ATTN_QB, nb - 1), 0)),
                  pl.BlockSpec((EDGE_VARIANTS, N_KV_HEADS, 2, HEAD_PAIRS * BLOCK, 3 * BLOCK),
                               lambda b, n: (0, 0, 0, 0, 0)),
                  pl.BlockSpec((1, ATTN_WIDTH), lambda b, n: (0, 0))],
        out_specs=pl.BlockSpec((1, rows, ATTN_WIDTH), lambda b, n: (b, n, 0)),
        out_shape=jax.ShapeDtypeStruct((B, S, ATTN_WIDTH), BF16),
        compiler_params=_cparams(2),
        name="attention",
    )(sink.astype(F32), q, kv, kv, kv, bias_tab, out_gain.reshape(1, ATTN_WIDTH))


LRU_TC = 128
LRU_PITCH = LRU_TC + SUBLANES
LRU_SLABS = LRU_WIDTH // LANES
HALO = SUBLANES


def _softplus(x):
    return jnp.maximum(x, 0.0) + jnp.log(1.0 + jnp.exp(-jnp.abs(x)))


def _gelu_tanh(x):
    k = math.sqrt(2.0 / math.pi)
    hx = 0.5 * x
    return hx + hx * jnp.tanh(x * (k + (k * 0.044715) * (x * x)))


def _sigmoid(x):
    return 0.5 + 0.5 * jnp.tanh(0.5 * x)


def _rglru_kernel(xr_ref, xp_ref, xn_ref, gr_ref, cw_ref, cb_ref, wg_ref, bg_ref, lam_ref, og_ref,
                  o_ref, sx_ref, a_ref, u_ref, h_ref, carry_ref, hf_ref):
    p = pl.program_id(0)
    i = pl.program_id(1)
    nc = pl.num_programs(1)
    c = i + p * (nc - 1 - 2 * i)
    B = xr_ref.shape[0]
    TC = LRU_TC

    sx_ref[:, HALO:HALO + TC, :] = xr_ref[...]
    sx_ref[:, 0:HALO, :] = jnp.where(c > 0, xp_ref[...], 0.0)
    sx_ref[:, HALO + TC:, :] = jnp.where(c < nc - 1, xn_ref[...], 0.0)
    xc = cb_ref[...][None]
    for j in range(CONV_W):
        off = HALO + j - CONV_LEFT
        xc = xc + cw_ref[j:j + 1, :][None] * sx_ref[:, off:off + TC, :]
    xc2 = xc.reshape(B * TC, LRU_WIDTH)

    g = jnp.dot(xc2.astype(BF16), wg_ref[0], preferred_element_type=F32) + bg_ref[0]
    r = _sigmoid(g[:, :LRU_WIDTH])
    ig = _sigmoid(g[:, LRU_WIDTH:])
    a = jnp.exp(r * (-LRU_C * _softplus(-lam_ref[0])))
    z = 1.0 - a * a
    u = z * lax.rsqrt(jnp.maximum(z, 1e-30)) * ig * xc2
    for b in range(B):
        for s in range(LRU_SLABS):
            a_ref[s, b * LRU_PITCH:b * LRU_PITCH + TC, :] = a[b * TC:(b + 1) * TC, s * LANES:(s + 1) * LANES]
            u_ref[s, b * LRU_PITCH:b * LRU_PITCH + TC, :] = u[b * TC:(b + 1) * TC, s * LANES:(s + 1) * LANES]

    @pl.when(i == 0)
    def _():
        carry_ref[...] = jnp.zeros_like(carry_ref)

    def step(k, hs):
        t = k + p * (TC - 1 - 2 * k)
        out = []
        for s in range(LRU_SLABS):
            idx = pl.ds(t, B, stride=LRU_PITCH)
            hn = a_ref[s, idx, :] * hs[s] + u_ref[s, idx, :]
            h_ref[s, idx, :] = hn
            out.append(hn)
        return tuple(out)

    hs = lax.fori_loop(0, TC, step, tuple(carry_ref[s] for s in range(LRU_SLABS)), unroll=8)
    for s in range(LRU_SLABS):
        carry_ref[s] = hs[s]

    @pl.when(p == 0)
    def _():
        for b in range(B):
            for s in range(LRU_SLABS):
                hf_ref[c, s, b * TC:(b + 1) * TC, :] = h_ref[s, b * LRU_PITCH:b * LRU_PITCH + TC, :].astype(hf_ref.dtype)

    @pl.when(p == 1)
    def _():
        for b in range(B):
            hsum = jnp.concatenate(
                [h_ref[s, b * LRU_PITCH:b * LRU_PITCH + TC, :] + hf_ref[c, s, b * TC:(b + 1) * TC, :].astype(F32)
                 for s in range(LRU_SLABS)], axis=1)
            y = hsum * _gelu_tanh(gr_ref[b])
            o_ref[b] = _rms(y, og_ref[...]).astype(o_ref.dtype)


def _block_diag(w):
    eye = jnp.eye(LRU_BLOCKS, dtype=w.dtype)
    return jnp.einsum('hij,hg->higj', w, eye).reshape(LRU_WIDTH, LRU_WIDTH)


def _rglru(xr, gr, conv_w, conv_b, w_a, b_a, w_i, b_i, lam, out_gain):
    B, S, W = xr.shape
    nc = S // LRU_TC
    hb = LRU_TC // HALO
    wg = jnp.stack([jnp.concatenate([_block_diag(w_a[d]), _block_diag(w_i[d])], axis=1) for d in range(2)]).astype(BF16)
    bg = jnp.concatenate([b_a, b_i], axis=-1).reshape(2, 1, 2 * W).astype(F32)
    chunk = lambda p, i: i + p * (nc - 1 - 2 * i)
    full2 = lambda shape: pl.BlockSpec(shape, lambda p, i: (0,) * len(shape))
    return pl.pallas_call(
        _rglru_kernel,
        grid=(2, nc),
        in_specs=[pl.BlockSpec((B, LRU_TC, W), lambda p, i: (0, chunk(p, i), 0)),
                  pl.BlockSpec((B, HALO, W), lambda p, i: (0, jnp.maximum(chunk(p, i) * hb - 1, 0), 0)),
                  pl.BlockSpec((B, HALO, W), lambda p, i: (0, jnp.minimum((chunk(p, i) + 1) * hb, S // HALO - 1), 0)),
                  pl.BlockSpec((B, LRU_TC, W), lambda p, i: (0, chunk(p, i), 0)),
                  full2((CONV_W, W)),
                  full2((1, W)),
                  pl.BlockSpec((1, W, 2 * W), lambda p, i: (p, 0, 0)),
                  pl.BlockSpec((1, 1, 2 * W), lambda p, i: (p, 0, 0)),
                  pl.BlockSpec((1, 1, W), lambda p, i: (p, 0, 0)),
                  full2((1, W))],
        out_specs=pl.BlockSpec((B, LRU_TC, W), lambda p, i: (0, nc - 1 - p * i, 0)),
        out_shape=jax.ShapeDtypeStruct((B, S, W), BF16),
        scratch_shapes=[pltpu.VMEM((B, LRU_TC + 2 * HALO, W), F32),
                        pltpu.VMEM((LRU_SLABS, B * LRU_PITCH, LANES), F32),
                        pltpu.VMEM((LRU_SLABS, B * LRU_PITCH, LANES), F32),
                        pltpu.VMEM((LRU_SLABS, B * LRU_PITCH, LANES), F32),
                        pltpu.VMEM((LRU_SLABS, B, LANES), F32),
                        pltpu.VMEM((nc, LRU_SLABS, B * LRU_TC, LANES), BF16)],
        compiler_params=_cparams(2),
        name="rglru",
    )(xr, xr, xr, gr, conv_w.astype(F32), conv_b.reshape(1, W).astype(F32), wg, bg,
      lam.reshape(2, 1, W).astype(F32), out_gain.reshape(1, W).astype(F32))


RT_TM = 512
RT_COLS = LANES
RT_ROWS = 48
RINFO = SUBLANES


def _split_bf16(x):
    hi = x.astype(BF16)
    lo = (x - hi.astype(F32)).astype(BF16)
    return hi, lo


def _route_kernel(an_ref, ln_ref, x_ref, wo_ref, g2_ref, wr_ref, br_ref,
                  x1_ref, h2_ref, gt_ref, ei_ref, cnt_ref, wob_ref, wrb_ref, tri_ref, run_ref, runc_ref):
    @pl.when(pl.program_id(0) == 0)
    def _():
        wob_ref[...] = wo_ref[...].astype(BF16)
        hi, lo = _split_bf16(wr_ref[...])
        wrb_ref[:RT_ROWS, :] = hi
        wrb_ref[RT_ROWS:, :] = lo
        r = lax.broadcasted_iota(jnp.int32, (RT_TM, RT_TM), 0)
        cidx = lax.broadcasted_iota(jnp.int32, (RT_TM, RT_TM), 1)
        tri_ref[...] = (r < cidx).astype(BF16)
        run_ref[...] = jnp.zeros_like(run_ref)
        runc_ref[...] = jnp.zeros_like(runc_ref)

    x1 = (x_ref[...]
          + jnp.dot(an_ref[...], wob_ref[:ATTN_WIDTH, :], preferred_element_type=F32)
          + jnp.dot(ln_ref[...], wob_ref[ATTN_WIDTH:, :], preferred_element_type=F32))
    x1_ref[...] = x1
    h2 = _rms(x1, g2_ref[...])
    h2_ref[...] = _pack_rows(h2)

    hi, lo = _split_bf16(h2)
    nt_dims = (((1,), (1,)), ((), ()))
    t1 = lax.dot_general(wrb_ref[...], hi, nt_dims, preferred_element_type=F32)
    t2 = lax.dot_general(wrb_ref[:RT_ROWS, :], lo, nt_dims, preferred_element_type=F32)
    logit = t1[:RT_ROWS] + t1[RT_ROWS:] + t2 + br_ref[...]

    sub = lax.broadcasted_iota(jnp.int32, (SUBLANES, RT_TM), 0)
    first_min = lambda hit: jnp.min(jnp.where(hit, sub, SUBLANES), axis=0, keepdims=True)
    is_g = sub < N_GROUPS
    gl = jnp.where(is_g, logit[:SUBLANES], -jnp.inf)
    gm = jnp.max(gl, axis=0, keepdims=True)
    gidx = first_min(gl == gm)
    g_p = 1.0 / jnp.sum(jnp.where(is_g, jnp.exp(logit[:SUBLANES] - gm), 0.0), axis=0, keepdims=True)
    el = logit[SUBLANES:2 * SUBLANES]
    for g in range(1, N_GROUPS):
        el = jnp.where(gidx == g, logit[(g + 1) * SUBLANES:(g + 2) * SUBLANES], el)
    m1 = jnp.max(el, axis=0, keepdims=True)
    i1 = first_min(el == m1)
    el2 = jnp.where(sub == i1, -jnp.inf, el)
    m2 = jnp.max(el2, axis=0, keepdims=True)
    i2 = first_min(el2 == m2)
    t = jnp.exp(m2 - m1)
    gate1 = g_p / (1.0 + t)
    gate2 = g_p * t / (1.0 + t)
    e1 = gidx * EXPERTS_PER_GROUP + i1
    e2 = gidx * EXPERTS_PER_GROUP + i2

    erow = lax.broadcasted_iota(jnp.int32, (N_EXPERTS, RT_TM), 0)
    oh1 = erow == e1
    oh2 = erow == e2
    oh = (oh1 | oh2).astype(F32)
    ohb = oh.astype(BF16)
    cum = jnp.dot(ohb, tri_ref[...], preferred_element_type=F32) + runc_ref[...]
    rank1 = jnp.sum(jnp.where(oh1, cum, 0.0), axis=0, keepdims=True)
    rank2 = jnp.sum(jnp.where(oh2, cum, 0.0), axis=0, keepdims=True)
    runc_ref[...] = runc_ref[...] + jnp.sum(oh, axis=1, keepdims=True)
    tile_cnt = lax.dot_general(jnp.ones((SUBLANES, RT_TM), BF16), ohb, nt_dims, preferred_element_type=F32)
    run_ref[:, :N_EXPERTS] = run_ref[:, :N_EXPERTS] + tile_cnt[0:1]
    cnt_ref[...] = run_ref[...].astype(jnp.int32)

    rows = [e1, e2, rank1.astype(jnp.int32), rank2.astype(jnp.int32)]
    ei = jnp.zeros((RINFO, RT_TM), jnp.int32)
    for k, v in enumerate(rows):
        ei = jnp.where(sub == k, v, ei)
    ei_ref[0] = ei
    gt_ref[...] = jnp.where(sub == 0, gate1, jnp.where(sub == 1, gate2, 0.0)).T


def _out_route(attn_n, lru_n, x2, w_out, ln2, w_group, b_group, w_er, b_er):
    T = x2.shape[0]
    pad_g = SUBLANES - N_GROUPS
    wr = jnp.concatenate([jnp.pad(w_group.T, ((0, pad_g), (0, 0))),
                          jnp.transpose(w_er, (0, 2, 1)).reshape(N_EXPERTS, D_MODEL)], axis=0)
    wr = jnp.pad(wr, ((0, RT_ROWS - wr.shape[0]), (0, 0))).astype(F32)
    br = jnp.concatenate([jnp.pad(b_group, (0, pad_g)), b_er.reshape(-1)])
    br = jnp.pad(br, (0, RT_ROWS - br.shape[0])).reshape(RT_ROWS, 1).astype(F32)
    row = lambda w: pl.BlockSpec((RT_TM, w), lambda i: (i, 0))
    const = lambda shape: pl.BlockSpec(shape, lambda i: (0, 0))
    return pl.pallas_call(
        _route_kernel,
        grid=(T // RT_TM,),
        in_specs=[row(ATTN_WIDTH), row(LRU_WIDTH), row(D_MODEL), const((D_MODEL, D_MODEL)), const((1, D_MODEL)),
                  const((RT_ROWS, D_MODEL)), const((RT_ROWS, 1))],
        out_specs=[row(D_MODEL), row(PACKED), row(RINFO),
                   pl.BlockSpec((1, RINFO, RT_TM), lambda i: (i, 0, 0)), const((1, RT_COLS))],
        out_shape=[jax.ShapeDtypeStruct((T, D_MODEL), F32),
                   jax.ShapeDtypeStruct((T, PACKED), U32),
                   jax.ShapeDtypeStruct((T, RINFO), F32),
                   jax.ShapeDtypeStruct((T // RT_TM, RINFO, RT_TM), jnp.int32),
                   jax.ShapeDtypeStruct((1, RT_COLS), jnp.int32)],
        scratch_shapes=[pltpu.VMEM((D_MODEL, D_MODEL), BF16),
                        pltpu.VMEM((2 * RT_ROWS, D_MODEL), BF16),
                        pltpu.VMEM((RT_TM, RT_TM), BF16),
                        pltpu.VMEM((1, RT_COLS), F32),
                        pltpu.VMEM((N_EXPERTS, 1), F32)],
        compiler_params=_cparams(1),
        name="out_route",
    )(attn_n, lru_n, x2, w_out, ln2.reshape(1, D_MODEL).astype(F32), wr, br)


def _moe_cap(T):
    A = T * TOP_K
    return ((A + MOE_BLOCK - 1) // MOE_BLOCK) * MOE_BLOCK + N_EXPERTS * MOE_BLOCK


PAD_BITS = tuple(1 << b for b in reversed(range(3, MOE_BLOCK.bit_length() - 1)))


def _layout_kernel(cnt_ref, ei_ref, dest_ref, pstart, be_ref, nu_ref, ge_ref):
    n_blocks = be_ref.shape[0]

    def lay(e, carry):
        start, blk, grp = carry
        pstart[e] = start
        nb = (cnt_ref[0, e] + MOE_BLOCK - 1) // MOE_BLOCK
        ge_ref[grp] = e

        def fill(k, c):
            be_ref[blk + k] = e
            return c
        lax.fori_loop(0, nb, fill, 0)
        return start + nb * MOE_BLOCK, blk + nb, grp + (nb > 0).astype(jnp.int32)
    _, used, groups = lax.fori_loop(0, N_EXPERTS, lay, (jnp.int32(0), jnp.int32(0), jnp.int32(0)))
    nu_ref[0] = used

    def tail(k, c):
        be_ref[k] = N_EXPERTS - 1
        return c
    lax.fori_loop(used, n_blocks, tail, 0)

    def no_group(k, c):
        ge_ref[k] = -1
        return c
    lax.fori_loop(groups, ge_ref.shape[0], no_group, 0)

    expert = ei_ref[:, 0:TOP_K, :]
    dest = ei_ref[:, TOP_K:2 * TOP_K, :]
    for e in range(N_EXPERTS):
        dest = dest + jnp.where(expert == e, pstart[e], 0)
    dest_ref[...] = dest


def _layout(ei, cnt, n_blocks):
    nt = ei.shape[0]
    smem = pl.BlockSpec(memory_space=pltpu.SMEM)
    vmem = pl.BlockSpec(memory_space=pltpu.VMEM)
    return pl.pallas_call(
        _layout_kernel,
        in_specs=[smem, vmem],
        out_specs=[vmem, smem, smem, smem, smem],
        out_shape=[jax.ShapeDtypeStruct((nt, TOP_K, RT_TM), jnp.int32),
                   jax.ShapeDtypeStruct((N_EXPERTS,), jnp.int32),
                   jax.ShapeDtypeStruct((n_blocks,), jnp.int32),
                   jax.ShapeDtypeStruct((1,), jnp.int32),
                   jax.ShapeDtypeStruct((N_EXPERTS + W_AHEAD,), jnp.int32)],
        name="layout",
    )(cnt, ei)


SC_CHUNK = 64
SC_BUFS = 3
SC_LEAD = SC_BUFS - 1


def _sc_workers():
    info = plsc.get_sparse_core_info()
    return info.num_cores, info.num_subcores


def _sc_ring(n_chunks, read, write):
    for c in range(min(SC_LEAD, n_chunks)):
        for cp in read(c):
            cp.start()
    reclaimed = set()
    for c in range(n_chunks):
        for cp in read(c):
            cp.wait()
        for cp in write(c):
            cp.start()
        nxt = c + SC_LEAD
        if nxt < n_chunks:
            if nxt - SC_BUFS >= 0:
                for cp in write(nxt - SC_BUFS):
                    cp.wait()
                reclaimed.add(nxt - SC_BUFS)
            for cp in read(nxt):
                cp.start()
    for c in range(n_chunks):
        if c not in reclaimed:
            for cp in write(c):
                cp.wait()


def _sc_dispatch(h2p, dest, cap):
    T = h2p.shape[0]
    nc, ns = _sc_workers()
    per_w = T // (nc * ns)
    n_ch = per_w // SC_CHUNK
    assert dest.shape == (nc * ns, TOP_K, per_w) and per_w % SC_CHUNK == 0
    idx = dest.reshape(nc * ns, TOP_K * n_ch, SC_CHUNK)
    mesh = plsc.VectorSubcoreMesh(core_axis_name="c", subcore_axis_name="s")

    @functools.partial(
        pl.kernel, mesh=mesh,
        out_type=jax.ShapeDtypeStruct((cap, PACKED), U32),
        scratch_types=[pltpu.VMEM((TOP_K * n_ch, SC_CHUNK), jnp.int32),
                       pltpu.VMEM((SC_BUFS, SC_CHUNK, PACKED), U32),
                       pltpu.SemaphoreType.DMA((SC_BUFS,)),
                       pltpu.SemaphoreType.DMA((SC_BUFS,))])
    def scatter(src_hbm, idx_hbm, out_hbm, idx_v, rows_v, rsem, wsem):
        wid = lax.axis_index("s") * nc + lax.axis_index("c")
        base = pl.multiple_of(wid * per_w, per_w)
        pltpu.sync_copy(idx_hbm.at[wid], idx_v)

        def read(c):
            b = c % SC_BUFS
            return [pltpu.make_async_copy(src_hbm.at[pl.ds(base + c * SC_CHUNK, SC_CHUNK)], rows_v.at[b], rsem.at[b])]

        def write(c):
            b = c % SC_BUFS
            return [pltpu.make_async_copy(rows_v.at[b], out_hbm.at[idx_v.at[k * n_ch + c]], wsem.at[b])
                    for k in range(TOP_K)]
        _sc_ring(n_ch, read, write)

    return scatter(h2p, idx)


def _sc_gather(yb, dest):
    nt, _, tm = dest.shape
    nc, ns = _sc_workers()
    n_rows = nt * TOP_K * tm
    per_w = n_rows // (nc * ns)
    n_ch = per_w // SC_CHUNK
    assert per_w * nc * ns == n_rows and per_w % SC_CHUNK == 0
    mesh = plsc.VectorSubcoreMesh(core_axis_name="c", subcore_axis_name="s")

    @functools.partial(
        pl.kernel, mesh=mesh,
        out_type=jax.ShapeDtypeStruct((n_rows, PACKED), U32),
        scratch_types=[pltpu.VMEM((per_w,), jnp.int32),
                       pltpu.VMEM((SC_BUFS, SC_CHUNK, PACKED), U32),
                       pltpu.SemaphoreType.DMA((SC_BUFS,)),
                       pltpu.SemaphoreType.DMA((SC_BUFS,))])
    def gather(table_hbm, idx_hbm, out_hbm, idx_v, rows_v, gsem, wsem):
        wid = lax.axis_index("s") * nc + lax.axis_index("c")
        base = pl.multiple_of(wid * per_w, per_w)
        pltpu.sync_copy(idx_hbm.at[pl.ds(base, per_w)], idx_v)

        def read(c):
            b = c % SC_BUFS
            return [pltpu.make_async_copy(table_hbm.at[idx_v.at[pl.ds(c * SC_CHUNK, SC_CHUNK)]], rows_v.at[b], gsem.at[b])]

        def write(c):
            b = c % SC_BUFS
            return [pltpu.make_async_copy(rows_v.at[b], out_hbm.at[pl.ds(base + c * SC_CHUNK, SC_CHUNK)], wsem.at[b])]
        _sc_ring(n_ch, read, write)

    return gather(yb, dest.reshape(n_rows)).reshape(nt, TOP_K, tm, PACKED)


def _padfill_kernel(cnt_ref, pstart, xs_in, xs_ref, zeros, zsem):
    del xs_in

    def pad_copies(fn):
        for e in range(N_EXPERTS):
            cnt = cnt_ref[0, e]
            head = (-cnt) & (SUBLANES - 1)
            rest = ((-cnt) & (MOE_BLOCK - 1)) - head
            off = pstart[e] + cnt
            for k in range(SUBLANES - 1):
                @pl.when(k < head)
                def _(off=off, k=k):
                    fn(pltpu.make_async_copy(zeros.at[pl.ds(0, 1), :], xs_ref.at[pl.ds(off + k, 1), :], zsem))
            off = off + head
            for bit in PAD_BITS:
                @pl.when((rest & bit) != 0)
                def _(off=off, bit=bit):
                    fn(pltpu.make_async_copy(zeros.at[pl.ds(0, bit), :],
                                             xs_ref.at[pl.ds(pl.multiple_of(off, SUBLANES), bit), :], zsem))
                off = off + (rest & bit)

    zeros[...] = jnp.zeros_like(zeros)
    pad_copies(lambda cp: cp.start())
    pad_copies(lambda cp: cp.wait())


def _padfill(xs, pstart, cnt):
    smem = pl.BlockSpec(memory_space=pltpu.SMEM)
    hbm = pl.BlockSpec(memory_space=pl.ANY)
    return pl.pallas_call(
        _padfill_kernel,
        in_specs=[smem, smem, hbm],
        out_specs=hbm,
        out_shape=jax.ShapeDtypeStruct(xs.shape, xs.dtype),
        input_output_aliases={2: 0},
        scratch_shapes=[pltpu.VMEM((MOE_BLOCK // 2, PACKED), U32), pltpu.SemaphoreType.DMA(())],
        name="padfill",
    )(cnt, pstart, xs)


W_SLOTS = 3
W_AHEAD = W_SLOTS - 1
EXPERT_GROUP = 8
EXPERT_RUNS = (1, 2, 4)


def _expert_kernel(be_ref, nu_ref, ge_ref, x_ref, wg_hbm, wu_hbm, wd_hbm, o_ref,
                   wgf, wuf, wdf, wgb, wub, wdb, grp_ref, sems):
    step = pl.program_id(0)

    def weight_copies(e, slot):
        return (pltpu.make_async_copy(wg_hbm.at[e], wgf.at[slot], sems.at[slot, 0]),
                pltpu.make_async_copy(wu_hbm.at[e], wuf.at[slot], sems.at[slot, 1]),
                pltpu.make_async_copy(wd_hbm.at[e], wdf.at[slot], sems.at[slot, 2]))

    @pl.when(step == 0)
    def _():
        grp_ref[0] = 0
        for a in range(W_AHEAD):
            @pl.when(ge_ref[a] >= 0)
            def _(a=a):
                for cp in weight_copies(ge_ref[a], a):
                    cp.start()

    n_blocks = be_ref.shape[0]
    n_used = nu_ref[0]

    def swiglu(s, n):
        rows = pl.ds(pl.multiple_of(s * MOE_BLOCK, MOE_BLOCK), n * MOE_BLOCK)
        lo, hi = _unpack_rows(x_ref[rows, :])
        lo = lo.astype(BF16)
        hi = hi.astype(BF16)
        g = (jnp.dot(lo, wgb[:PACKED, :], preferred_element_type=F32)
             + jnp.dot(hi, wgb[PACKED:, :], preferred_element_type=F32))
        u = (jnp.dot(lo, wub[:PACKED, :], preferred_element_type=F32)
             + jnp.dot(hi, wub[PACKED:, :], preferred_element_type=F32))
        h = (g * _sigmoid(g) * u).astype(BF16)
        o_ref[rows, :] = _pack_rows(jnp.dot(h, wdb[...], preferred_element_type=F32))

    def run(s):
        j = step * EXPERT_GROUP + s
        e = be_ref[j]
        first = jnp.logical_or(j == 0, e != be_ref[jnp.maximum(j - 1, 0)])

        @pl.when(first)
        def _():
            grp = grp_ref[0]
            slot = grp % W_SLOTS
            for cp in weight_copies(e, slot):
                cp.wait()
            wgb[...] = wgf[slot].astype(BF16)
            wub[...] = wuf[slot].astype(BF16)
            wdb[...] = wdf[slot].astype(BF16)
            nxt = ge_ref[grp + W_AHEAD]

            @pl.when(nxt >= 0)
            def _():
                for cp in weight_copies(nxt, (grp + W_AHEAD) % W_SLOTS):
                    cp.start()
            grp_ref[0] = grp + 1

        def same(k):
            return (s + k < EXPERT_GROUP) & (j + k < n_used) & (be_ref[jnp.minimum(j + k, n_blocks - 1)] == e)
        take = jnp.int32(1)
        for n in EXPERT_RUNS[1:]:
            ok = same(n - 1)
            for k in range(1, n - 1):
                ok = ok & same(k)
            take = jnp.where(ok, n, take)
        for n in EXPERT_RUNS:
            @pl.when(take == n)
            def _(n=n):
                swiglu(s, n)
        return s + take

    lax.while_loop(lambda s: (s < EXPERT_GROUP) & (step * EXPERT_GROUP + s < n_used), run, jnp.int32(0))


def _experts(xs, block_expert, n_used, group_expert, w_gate, w_up, w_down):
    cap = xs.shape[0]
    n_blocks = cap // MOE_BLOCK
    assert n_blocks % EXPERT_GROUP == 0
    rows = EXPERT_GROUP * MOE_BLOCK
    last = lambda j, be, nu, ge: jnp.minimum(j, (nu[0] - 1) // EXPERT_GROUP)
    hbm = pl.BlockSpec(memory_space=pl.ANY)
    gs = pltpu.PrefetchScalarGridSpec(
        num_scalar_prefetch=3,
        grid=(n_blocks // EXPERT_GROUP,),
        in_specs=[pl.BlockSpec((rows, PACKED), lambda j, be, nu, ge: (last(j, be, nu, ge), 0)), hbm, hbm, hbm],
        out_specs=pl.BlockSpec((rows, PACKED), lambda j, be, nu, ge: (last(j, be, nu, ge), 0)),
        scratch_shapes=[pltpu.VMEM((W_SLOTS, D_MODEL, D_EXPERT), F32),
                        pltpu.VMEM((W_SLOTS, D_MODEL, D_EXPERT), F32),
                        pltpu.VMEM((W_SLOTS, D_EXPERT, D_MODEL), F32),
                        pltpu.VMEM((D_MODEL, D_EXPERT), BF16),
                        pltpu.VMEM((D_MODEL, D_EXPERT), BF16),
                        pltpu.VMEM((D_EXPERT, D_MODEL), BF16),
                        pltpu.SMEM((1,), jnp.int32),
                        pltpu.SemaphoreType.DMA((W_SLOTS, 3))],
    )
    return pl.pallas_call(
        _expert_kernel,
        grid_spec=gs,
        out_shape=jax.ShapeDtypeStruct((cap, PACKED), U32),
        compiler_params=_cparams(1),
        name="experts",
    )(block_expert, n_used, group_expert, xs, w_gate, w_up, w_down)


CB_TM = RT_TM


def _combine_kernel(x1_ref, gt_ref, y2_ref, o_ref):
    g = gt_ref[...]
    lo1, hi1 = _unpack_rows(y2_ref[0, 0])
    lo2, hi2 = _unpack_rows(y2_ref[0, 1])
    o_ref[:, :PACKED] = x1_ref[:, :PACKED] + g[:, 0:1] * lo1 + g[:, 1:2] * lo2
    o_ref[:, PACKED:] = x1_ref[:, PACKED:] + g[:, 0:1] * hi1 + g[:, 1:2] * hi2


def _combine(x1, gates, y2):
    T = x1.shape[0]
    nt = T // CB_TM
    return pl.pallas_call(
        _combine_kernel,
        grid=(nt,),
        in_specs=[pl.BlockSpec((CB_TM, D_MODEL), lambda i: (i, 0)),
                  pl.BlockSpec((CB_TM, RINFO), lambda i: (i, 0)),
                  pl.BlockSpec((1, TOP_K, CB_TM, PACKED), lambda i: (i, 0, 0, 0))],
        out_specs=pl.BlockSpec((CB_TM, D_MODEL), lambda i: (i, 0)),
        out_shape=jax.ShapeDtypeStruct((T, D_MODEL), F32),
        compiler_params=_cparams(1),
        name="combine",
    )(x1, gates, y2)


def _layer(x, rel_bias, ln1, w_in, q_norm, k_norm, attn_sink, conv_w, conv_b, lru_wa, lru_ba, lru_wi, lru_bi,
           lru_lambda, out_norm_attn, out_norm_lru, w_out, ln2, w_group, b_group, w_er, b_er, w_gate, w_up, w_down):
    B, S, D = x.shape
    T = B * S
    x2 = x.reshape(T, D)
    q, kv, xr, gr = _in_proj(x2, ln1, w_in, q_norm, k_norm)
    bias_tab = _bias_table(rel_bias)
    attn_n = _attention(q.reshape(B, S, ATTN_WIDTH), kv.reshape(B, S, 2 * KV_WIDTH), bias_tab, attn_sink,
                        out_norm_attn)
    lru_n = _rglru(xr.reshape(B, S, LRU_WIDTH), gr.reshape(B, S, LRU_WIDTH), conv_w, conv_b,
                   lru_wa, lru_ba, lru_wi, lru_bi, lru_lambda, out_norm_lru)
    x1, h2, gates, ei, cnt = _out_route(attn_n.reshape(T, ATTN_WIDTH), lru_n.reshape(T, LRU_WIDTH), x2, w_out, ln2,
                                        w_group, b_group, w_er, b_er)
    cap = _moe_cap(T)
    dest, pstart, block_expert, n_used, group_expert = _layout(ei, cnt, cap // MOE_BLOCK)
    xs = _padfill(_sc_dispatch(h2, dest, cap), pstart, cnt)
    yb = _experts(xs, block_expert, n_used, group_expert, w_gate, w_up, w_down)
    out = _combine(x1, gates, _sc_gather(yb, dest))
    return out.reshape(B, S, D)


def kernel(x, rel_bias, ln1, w_in, q_norm, k_norm, attn_sink, conv_w, conv_b, lru_wa, lru_ba, lru_wi, lru_bi,
           lru_lambda, out_norm_attn, out_norm_lru, w_out, ln2, w_group, b_group, w_expert_router, b_expert_router,
           w_gate, w_up, w_down):
    depth = ln1.shape[0]
    for l in range(depth):
        x = _layer(x, rel_bias, ln1[l], w_in[l], q_norm[l], k_norm[l], attn_sink[l], conv_w[l], conv_b[l],
                   lru_wa[l], lru_ba[l], lru_wi[l], lru_bi[l], lru_lambda[l], out_norm_attn[l], out_norm_lru[l],
                   w_out[l], ln2[l], w_group[l], b_group[l], w_expert_router[l], b_expert_router[l],
                   w_gate[l], w_up[l], w_down[l])
    return x
```

```python
import functools
import math

import jax
import jax.numpy as jnp
import numpy as np
from jax import lax
from jax.experimental import pallas as pl
from jax.experimental.pallas import tpu as pltpu
from jax.experimental.pallas import tpu_sc as plsc

D_MODEL = 1024
N_HEADS = 8
N_KV_HEADS = 2
HEAD_DIM = 64
Q_PER_KV = N_HEADS // N_KV_HEADS
ATTN_WIDTH = N_HEADS * HEAD_DIM
KV_WIDTH = N_KV_HEADS * HEAD_DIM
WINDOW = 128
BLOCK = 128
NUM_BUCKETS = 32
MAX_DISTANCE = 128
LRU_WIDTH = D_MODEL - ATTN_WIDTH
LRU_BLOCKS = 8
LRU_BLOCK_DIM = LRU_WIDTH // LRU_BLOCKS
LRU_C = 8.0
CONV_W = 4
CONV_LEFT = 2
N_GROUPS = 4
EXPERTS_PER_GROUP = 8
N_EXPERTS = N_GROUPS * EXPERTS_PER_GROUP
TOP_K = 2
D_EXPERT = 512
MOE_BLOCK = 256
EPS = 1e-6
NEG_INF = -1e30

LANES = 128
SUBLANES = 8
VMEM_LIMIT = 56 * 1024 * 1024

F32 = jnp.float32
BF16 = jnp.bfloat16


def _cparams(n_axes, vmem=VMEM_LIMIT):
    return pltpu.CompilerParams(dimension_semantics=("arbitrary",) * n_axes, vmem_limit_bytes=vmem)


def _rms(x, gain):
    return x * lax.rsqrt(jnp.mean(x * x, axis=-1, keepdims=True) + EPS) * gain


U32 = jnp.uint32
HI_MASK = 0xFFFF0000
PACKED = D_MODEL // 2


def _pack_rows(x):
    h = x.shape[1] // 2
    lo = lax.bitcast_convert_type(x[:, :h].astype(BF16).astype(F32), U32) >> 16
    hi = lax.bitcast_convert_type(x[:, h:].astype(BF16).astype(F32), U32) & jnp.uint32(HI_MASK)
    return lo | hi


def _unpack_rows(p):
    lo = lax.bitcast_convert_type(p << 16, F32)
    hi = lax.bitcast_convert_type(p & jnp.uint32(HI_MASK), F32)
    return lo, hi


IN_TM = 1024


def _head_rms(x, n_heads, gain):
    head = lax.broadcasted_iota(jnp.int32, (1, n_heads * HEAD_DIM), 1) // HEAD_DIM
    x2 = x * x
    scale = jnp.zeros_like(x)
    for h in range(n_heads):
        ms = jnp.sum(jnp.where(head == h, x2, 0.0), axis=-1, keepdims=True) * (1.0 / HEAD_DIM)
        scale = jnp.where(head == h, lax.rsqrt(ms + EPS), scale)
    return x * scale * gain


def _in_proj_kernel(x_ref, g_ref, w_ref, qg_ref, kg_ref, q_ref, kv_ref, xr_ref, gr_ref, wb_ref):
    @pl.when(pl.program_id(0) == 0)
    def _():
        wb_ref[...] = w_ref[...].astype(BF16)

    h = _rms(x_ref[...], g_ref[...]).astype(BF16)
    c_k = ATTN_WIDTH
    c_v = c_k + KV_WIDTH
    c_x = c_v + KV_WIDTH
    c_g = c_x + LRU_WIDTH
    q = jnp.dot(h, wb_ref[:, :c_k], preferred_element_type=F32)
    q_ref[...] = _head_rms(q, N_HEADS, qg_ref[...]).astype(BF16)
    k = jnp.dot(h, wb_ref[:, c_k:c_v], preferred_element_type=F32)
    kv_ref[:, :KV_WIDTH] = _head_rms(k, N_KV_HEADS, kg_ref[...]).astype(BF16)
    kv_ref[:, KV_WIDTH:] = jnp.dot(h, wb_ref[:, c_v:c_x], preferred_element_type=F32).astype(BF16)
    xr_ref[...] = jnp.dot(h, wb_ref[:, c_x:c_g], preferred_element_type=F32)
    gr_ref[...] = jnp.dot(h, wb_ref[:, c_g:], preferred_element_type=F32)


def _in_proj(x2, ln1, w_in, q_gain, k_gain):
    T = x2.shape[0]
    n_in = w_in.shape[1]
    row = lambda w: pl.BlockSpec((IN_TM, w), lambda i: (i, 0))
    qg = (jnp.tile(q_gain.astype(F32), N_HEADS) * (HEAD_DIM ** -0.5)).reshape(1, ATTN_WIDTH)
    kg = jnp.tile(k_gain.astype(F32), N_KV_HEADS).reshape(1, KV_WIDTH)
    return pl.pallas_call(
        _in_proj_kernel,
        grid=(T // IN_TM,),
        in_specs=[row(D_MODEL),
                  pl.BlockSpec((1, D_MODEL), lambda i: (0, 0)),
                  pl.BlockSpec((D_MODEL, n_in), lambda i: (0, 0)),
                  pl.BlockSpec((1, ATTN_WIDTH), lambda i: (0, 0)),
                  pl.BlockSpec((1, KV_WIDTH), lambda i: (0, 0))],
        out_specs=[row(ATTN_WIDTH), row(2 * KV_WIDTH), row(LRU_WIDTH), row(LRU_WIDTH)],
        out_shape=[jax.ShapeDtypeStruct((T, ATTN_WIDTH), BF16),
                   jax.ShapeDtypeStruct((T, 2 * KV_WIDTH), BF16),
                   jax.ShapeDtypeStruct((T, LRU_WIDTH), F32),
                   jax.ShapeDtypeStruct((T, LRU_WIDTH), F32)],
        scratch_shapes=[pltpu.VMEM((D_MODEL, n_in), BF16)],
        compiler_params=_cparams(1),
        name="in_proj",
    )(x2, ln1.reshape(1, D_MODEL), w_in, qg, kg)


def _t5_bucket(rel):
    half = NUM_BUCKETS // 2
    max_exact = half // 2
    base = jnp.where(rel > 0, half, 0)
    n = jnp.abs(rel)
    nf = jnp.maximum(n, 1).astype(jnp.float32)
    large = max_exact + (jnp.log(nf / max_exact) / math.log(MAX_DISTANCE / max_exact)
                         * (half - max_exact)).astype(jnp.int32)
    large = jnp.minimum(large, half - 1)
    return base + jnp.where(n < max_exact, n, large)


HEAD_PAIRS = Q_PER_KV // 2
EDGE_VARIANTS = 3


def _bias_kernel(rb_ref, bucket_ref, band_ref, o_ref):
    bucket = bucket_ref[...]
    band = band_ref[...] > 0
    col = lax.broadcasted_iota(jnp.int32, bucket.shape, 1)
    valid = (band & (col >= BLOCK), band, band & (col < 2 * BLOCK))
    for h in range(N_HEADS):
        acc = jnp.zeros(bucket.shape, F32)
        for b in range(NUM_BUCKETS):
            acc = jnp.where(bucket == b, rb_ref[b, h], acc)
        kv, g = divmod(h, Q_PER_KV)
        pair, parity = divmod(g, 2)
        for var in range(EDGE_VARIANTS):
            o_ref[var, kv, parity, pair * BLOCK:(pair + 1) * BLOCK, :] = jnp.where(valid[var], acc, NEG_INF)


def _bias_table(rel_bias):
    qi = jnp.arange(BLOCK, dtype=jnp.int32)
    kj = jnp.arange(3 * BLOCK, dtype=jnp.int32)
    rel = kj[None, :] - BLOCK - qi[:, None]
    bucket = _t5_bucket(rel).astype(jnp.int32)
    band = (jnp.abs(rel) <= WINDOW).astype(jnp.int32)
    return pl.pallas_call(
        _bias_kernel,
        in_specs=[pl.BlockSpec(memory_space=pltpu.SMEM),
                  pl.BlockSpec(memory_space=pltpu.VMEM),
                  pl.BlockSpec(memory_space=pltpu.VMEM)],
        out_specs=pl.BlockSpec(memory_space=pltpu.VMEM),
        out_shape=jax.ShapeDtypeStruct((EDGE_VARIANTS, N_KV_HEADS, 2, HEAD_PAIRS * BLOCK, 3 * BLOCK), F32),
        name="bias_table",
    )(rel_bias.astype(F32), bucket, band)


def _attn_kernel(sink_ref, q_ref, kp_ref, kc_ref, kn_ref, bias_ref, og_ref, o_ref):
    n = pl.program_id(1)
    kv_all = jnp.concatenate([kp_ref[0], kc_ref[0], kn_ref[0]], axis=0)
    for qb in range(ATTN_QB):
        variant = 1
        if qb == 0:
            variant = jnp.where(n == 0, 0, 1)
        if qb == ATTN_QB - 1:
            variant = jnp.where(n == pl.num_programs(1) - 1, 2, variant)
        out = _attn_block(q_ref[0, qb * BLOCK:(qb + 1) * BLOCK, :], kv_all[qb * BLOCK:(qb + 3) * BLOCK, :],
                          lambda kv, parity: bias_ref[variant, kv, parity], sink_ref)
        o_ref[0, qb * BLOCK:(qb + 1) * BLOCK, :] = _rms(out, og_ref[...]).astype(o_ref.dtype)


def _attn_block(q, kvw, bias, sink_ref):
    low = lax.broadcasted_iota(jnp.int32, (1, LANES), 1) < HEAD_DIM
    swap = lambda slab: pltpu.roll(slab.astype(F32), HEAD_DIM, 1).astype(BF16)
    kslab, vslab = kvw[:, :KV_WIDTH], kvw[:, KV_WIDTH:]
    kslab_sw, vslab_sw = swap(kslab), swap(vslab)
    rowi = lax.broadcasted_iota(jnp.int32, (HEAD_PAIRS * BLOCK, 1), 0)
    combos = [(kv, parity) for kv in range(N_KV_HEADS) for parity in range(2)]
    scores, vzs, sinks = [], [], []
    for kv, parity in combos:
        ks, vs = (kslab, vslab) if (kv == 0) == (parity == 0) else (kslab_sw, vslab_sw)
        keep = low if parity == 0 else jnp.logical_not(low)
        kz = jnp.where(keep, ks, jnp.zeros_like(ks))
        vzs.append(jnp.where(keep, vs, jnp.zeros_like(vs)))
        base = kv * Q_PER_KV * HEAD_DIM
        qpair = jnp.concatenate([q[:, base + j * LANES:base + (j + 1) * LANES] for j in range(HEAD_PAIRS)], axis=0)
        s = lax.dot_general(qpair, kz, (((1,), (1,)), ((), ())), preferred_element_type=F32)
        scores.append(s + bias(kv, parity))
        sink = jnp.zeros((HEAD_PAIRS * BLOCK, 1), F32)
        for j in range(HEAD_PAIRS):
            sink = jnp.where(rowi // BLOCK == j, sink_ref[kv * Q_PER_KV + 2 * j + parity], sink)
        sinks.append(sink)
    probs, inv = [], []
    for s, sink in zip(scores, sinks):
        m = jnp.maximum(jnp.max(s, axis=-1, keepdims=True), sink)
        p = jnp.exp(s - m)
        inv.append(1.0 / (jnp.sum(p, axis=-1, keepdims=True) + jnp.exp(sink - m)))
        probs.append(p.astype(BF16))
    outs = [jnp.dot(p, vz, preferred_element_type=F32) * r for p, vz, r in zip(probs, vzs, inv)]
    cols = []
    for kv in range(N_KV_HEADS):
        acc = outs[2 * kv] + outs[2 * kv + 1]
        cols += [acc[j * BLOCK:(j + 1) * BLOCK, :] for j in range(HEAD_PAIRS)]
    return jnp.concatenate(cols, axis=1)


ATTN_QB = 4


def _attention(q, kv, bias_tab, sink, out_gain):
    B, S, _ = q.shape
    nb = S // BLOCK
    assert ATTN_QB >= 2 and nb % ATTN_QB == 0, "a step's first and last query blocks must be distinct"
    ns = nb // ATTN_QB
    rows = ATTN_QB * BLOCK
    kvspec = lambda f: pl.BlockSpec((1, BLOCK, 2 * KV_WIDTH), f)
    return pl.pallas_call(
        _attn_kernel,
        grid=(B, ns),
        in_specs=[pl.BlockSpec(memory_space=pltpu.SMEM),
                  pl.BlockSpec((1, rows, ATTN_WIDTH), lambda b, n: (b, n, 0)),
                  kvspec(lambda b, n: (b, jnp.maximum(n * ATTN_QB - 1, 0), 0)),
                  pl.BlockSpec((1, rows, 2 * KV_WIDTH), lambda b, n: (b, n, 0)),
                  kvspec(lambda b, n: (b, jnp.minimum((n + 1) * ATTN_QB, nb - 1), 0)),
                  pl.BlockSpec((EDGE_VARIANTS, N_KV_HEADS, 2, HEAD_PAIRS * BLOCK, 3 * BLOCK),
                               lambda b, n: (0, 0, 0, 0, 0)),
                  pl.BlockSpec((1, ATTN_WIDTH), lambda b, n: (0, 0))],
        out_specs=pl.BlockSpec((1, rows, ATTN_WIDTH), lambda b, n: (b, n, 0)),
        out_shape=jax.ShapeDtypeStruct((B, S, ATTN_WIDTH), BF16),
        compiler_params=_cparams(2),
        name="attention",
    )(sink.astype(F32), q, kv, kv, kv, bias_tab, out_gain.reshape(1, ATTN_WIDTH))


LRU_TC = 128
LRU_PITCH = LRU_TC + SUBLANES
LRU_SLABS = LRU_WIDTH // LANES
HALO = SUBLANES


def _softplus(x):
    return jnp.maximum(x, 0.0) + jnp.log(1.0 + jnp.exp(-jnp.abs(x)))


def _gelu_tanh(x):
    k = math.sqrt(2.0 / math.pi)
    hx = 0.5 * x
    return hx + hx * jnp.tanh(x * (k + (k * 0.044715) * (x * x)))


def _sigmoid(x):
    return 0.5 + 0.5 * jnp.tanh(0.5 * x)


def _rglru_kernel(xr_ref, xp_ref, xn_ref, gr_ref, cw_ref, cb_ref, wg_ref, bg_ref, lam_ref, og_ref,
                  o_ref, sx_ref, a_ref, u_ref, h_ref, carry_ref, hf_ref):
    p = pl.program_id(0)
    i = pl.program_id(1)
    nc = pl.num_programs(1)
    c = i + p * (nc - 1 - 2 * i)
    B = xr_ref.shape[0]
    TC = LRU_TC

    sx_ref[:, HALO:HALO + TC, :] = xr_ref[...]
    sx_ref[:, 0:HALO, :] = jnp.where(c > 0, xp_ref[...], 0.0)
    sx_ref[:, HALO + TC:, :] = jnp.where(c < nc - 1, xn_ref[...], 0.0)
    xc = cb_ref[...][None]
    for j in range(CONV_W):
        off = HALO + j - CONV_LEFT
        xc = xc + cw_ref[j:j + 1, :][None] * sx_ref[:, off:off + TC, :]
    xc2 = xc.reshape(B * TC, LRU_WIDTH)

    g = jnp.dot(xc2.astype(BF16), wg_ref[0], preferred_element_type=F32) + bg_ref[0]
    ta = jnp.tanh(g[:, :LRU_WIDTH])
    ig = 0.5 + 0.5 * jnp.tanh(g[:, LRU_WIDTH:])
    a = jnp.exp2((1.0 + ta) * ((-0.5 * LRU_C * math.log2(math.e)) * _softplus(-lam_ref[0])))
    z = 1.0 - a * a
    u = z * lax.rsqrt(jnp.maximum(z, 1e-30)) * ig * xc2
    for b in range(B):
        for s in range(LRU_SLABS):
            a_ref[s, b * LRU_PITCH:b * LRU_PITCH + TC, :] = a[b * TC:(b + 1) * TC, s * LANES:(s + 1) * LANES]
            u_ref[s, b * LRU_PITCH:b * LRU_PITCH + TC, :] = u[b * TC:(b + 1) * TC, s * LANES:(s + 1) * LANES]

    @pl.when(i == 0)
    def _():
        carry_ref[...] = jnp.zeros_like(carry_ref)

    def step(k, hs):
        t = k + p * (TC - 1 - 2 * k)
        out = []
        for s in range(LRU_SLABS):
            idx = pl.ds(t, B, stride=LRU_PITCH)
            hn = a_ref[s, idx, :] * hs[s] + u_ref[s, idx, :]
            h_ref[s, idx, :] = hn
            out.append(hn)
        return tuple(out)

    hs = lax.fori_loop(0, TC, step, tuple(carry_ref[s] for s in range(LRU_SLABS)), unroll=8)
    for s in range(LRU_SLABS):
        carry_ref[s] = hs[s]

    @pl.when(p == 0)
    def _():
        for b in range(B):
            for s in range(LRU_SLABS):
                hf_ref[c, s, b * TC:(b + 1) * TC, :] = h_ref[s, b * LRU_PITCH:b * LRU_PITCH + TC, :].astype(hf_ref.dtype)

    @pl.when(p == 1)
    def _():
        for b in range(B):
            hsum = jnp.concatenate(
                [h_ref[s, b * LRU_PITCH:b * LRU_PITCH + TC, :] + hf_ref[c, s, b * TC:(b + 1) * TC, :].astype(F32)
                 for s in range(LRU_SLABS)], axis=1)
            y = hsum * _gelu_tanh(gr_ref[b])
            o_ref[b] = _rms(y, og_ref[...]).astype(o_ref.dtype)


def _block_diag(w):
    eye = jnp.eye(LRU_BLOCKS, dtype=w.dtype)
    return jnp.einsum('hij,hg->higj', w, eye).reshape(LRU_WIDTH, LRU_WIDTH)


def _rglru(xr, gr, conv_w, conv_b, w_a, b_a, w_i, b_i, lam, out_gain):
    B, S, W = xr.shape
    nc = S // LRU_TC
    hb = LRU_TC // HALO
    wg = jnp.stack([jnp.concatenate([_block_diag(w_a[d]), _block_diag(w_i[d])], axis=1) for d in range(2)])
    wg = (0.5 * wg).astype(BF16)
    bg = 0.5 * jnp.concatenate([b_a, b_i], axis=-1).reshape(2, 1, 2 * W).astype(F32)
    chunk = lambda p, i: i + p * (nc - 1 - 2 * i)
    full2 = lambda shape: pl.BlockSpec(shape, lambda p, i: (0,) * len(shape))
    return pl.pallas_call(
        _rglru_kernel,
        grid=(2, nc),
        in_specs=[pl.BlockSpec((B, LRU_TC, W), lambda p, i: (0, chunk(p, i), 0)),
                  pl.BlockSpec((B, HALO, W), lambda p, i: (0, jnp.maximum(chunk(p, i) * hb - 1, 0), 0)),
                  pl.BlockSpec((B, HALO, W), lambda p, i: (0, jnp.minimum((chunk(p, i) + 1) * hb, S // HALO - 1), 0)),
                  pl.BlockSpec((B, LRU_TC, W), lambda p, i: (0, chunk(p, i), 0)),
                  full2((CONV_W, W)),
                  full2((1, W)),
                  pl.BlockSpec((1, W, 2 * W), lambda p, i: (p, 0, 0)),
                  pl.BlockSpec((1, 1, 2 * W), lambda p, i: (p, 0, 0)),
                  pl.BlockSpec((1, 1, W), lambda p, i: (p, 0, 0)),
                  full2((1, W))],
        out_specs=pl.BlockSpec((B, LRU_TC, W), lambda p, i: (0, nc - 1 - p * i, 0)),
        out_shape=jax.ShapeDtypeStruct((B, S, W), BF16),
        scratch_shapes=[pltpu.VMEM((B, LRU_TC + 2 * HALO, W), F32),
                        pltpu.VMEM((LRU_SLABS, B * LRU_PITCH, LANES), F32),
                        pltpu.VMEM((LRU_SLABS, B * LRU_PITCH, LANES), F32),
                        pltpu.VMEM((LRU_SLABS, B * LRU_PITCH, LANES), F32),
                        pltpu.VMEM((LRU_SLABS, B, LANES), F32),
                        pltpu.VMEM((nc, LRU_SLABS, B * LRU_TC, LANES), BF16)],
        compiler_params=_cparams(2),
        name="rglru",
    )(xr, xr, xr, gr, conv_w.astype(F32), conv_b.reshape(1, W).astype(F32), wg, bg,
      lam.reshape(2, 1, W).astype(F32), out_gain.reshape(1, W).astype(F32))


RT_TM = 1024
RT_COLS = LANES
RT_ROWS = 48
RINFO = SUBLANES


def _split_bf16(x):
    hi = x.astype(BF16)
    lo = (x - hi.astype(F32)).astype(BF16)
    return hi, lo


def _route_kernel(an_ref, ln_ref, x_ref, wo_ref, g2_ref, wr_ref, br_ref,
                  x1_ref, h2_ref, gt_ref, ei_ref, cnt_ref, wob_ref, wrb_ref, tri_ref, run_ref, runc_ref):
    @pl.when(pl.program_id(0) == 0)
    def _():
        wob_ref[...] = wo_ref[...].astype(BF16)
        hi, lo = _split_bf16(wr_ref[...])
        wrb_ref[:RT_ROWS, :] = hi
        wrb_ref[RT_ROWS:, :] = lo
        r = lax.broadcasted_iota(jnp.int32, (RT_TM, RT_TM), 0)
        cidx = lax.broadcasted_iota(jnp.int32, (RT_TM, RT_TM), 1)
        tri_ref[...] = (r < cidx).astype(BF16)
        run_ref[...] = jnp.zeros_like(run_ref)
        runc_ref[...] = jnp.zeros_like(runc_ref)

    x1 = (x_ref[...]
          + jnp.dot(an_ref[...], wob_ref[:ATTN_WIDTH, :], preferred_element_type=F32)
          + jnp.dot(ln_ref[...], wob_ref[ATTN_WIDTH:, :], preferred_element_type=F32))
    x1_ref[...] = x1
    h2 = _rms(x1, g2_ref[...])
    h2_ref[...] = _pack_rows(h2)

    hi, lo = _split_bf16(h2)
    nt_dims = (((1,), (1,)), ((), ()))
    t1 = lax.dot_general(wrb_ref[...], hi, nt_dims, preferred_element_type=F32)
    t2 = lax.dot_general(wrb_ref[:RT_ROWS, :], lo, nt_dims, preferred_element_type=F32)
    logit = t1[:RT_ROWS] + t1[RT_ROWS:] + t2 + br_ref[...]

    sub = lax.broadcasted_iota(jnp.int32, (SUBLANES, RT_TM), 0)
    first_min = lambda hit: jnp.min(jnp.where(hit, sub, SUBLANES), axis=0, keepdims=True)
    is_g = sub < N_GROUPS
    gl = jnp.where(is_g, logit[:SUBLANES], -jnp.inf)
    gm = jnp.max(gl, axis=0, keepdims=True)
    gidx = first_min(gl == gm)
    g_p = 1.0 / jnp.sum(jnp.where(is_g, jnp.exp(logit[:SUBLANES] - gm), 0.0), axis=0, keepdims=True)
    el = logit[SUBLANES:2 * SUBLANES]
    for g in range(1, N_GROUPS):
        el = jnp.where(gidx == g, logit[(g + 1) * SUBLANES:(g + 2) * SUBLANES], el)
    m1 = jnp.max(el, axis=0, keepdims=True)
    i1 = first_min(el == m1)
    el2 = jnp.where(sub == i1, -jnp.inf, el)
    m2 = jnp.max(el2, axis=0, keepdims=True)
    i2 = first_min(el2 == m2)
    t = jnp.exp(m2 - m1)
    gate1 = g_p / (1.0 + t)
    gate2 = g_p * t / (1.0 + t)
    e1 = gidx * EXPERTS_PER_GROUP + i1
    e2 = gidx * EXPERTS_PER_GROUP + i2

    erow = lax.broadcasted_iota(jnp.int32, (N_EXPERTS, RT_TM), 0)
    oh1 = erow == e1
    oh2 = erow == e2
    oh = (oh1 | oh2).astype(F32)
    ohb = oh.astype(BF16)
    cum = jnp.dot(ohb, tri_ref[...], preferred_element_type=F32) + runc_ref[...]
    rank1 = jnp.sum(jnp.where(oh1, cum, 0.0), axis=0, keepdims=True)
    rank2 = jnp.sum(jnp.where(oh2, cum, 0.0), axis=0, keepdims=True)
    runc_ref[...] = runc_ref[...] + jnp.sum(oh, axis=1, keepdims=True)
    tile_cnt = lax.dot_general(jnp.ones((SUBLANES, RT_TM), BF16), ohb, nt_dims, preferred_element_type=F32)
    run_ref[:, :N_EXPERTS] = run_ref[:, :N_EXPERTS] + tile_cnt[0:1]
    cnt_ref[...] = run_ref[...].astype(jnp.int32)

    rows = [e1, e2, rank1.astype(jnp.int32), rank2.astype(jnp.int32)]
    ei = jnp.zeros((RINFO, RT_TM), jnp.int32)
    for k, v in enumerate(rows):
        ei = jnp.where(sub == k, v, ei)
    ei_ref[0] = ei
    gt_ref[...] = jnp.where(sub == 0, gate1, jnp.where(sub == 1, gate2, 0.0)).T


def _out_route(attn_n, lru_n, x2, w_out, ln2, w_group, b_group, w_er, b_er):
    T = x2.shape[0]
    pad_g = SUBLANES - N_GROUPS
    wr = jnp.concatenate([jnp.pad(w_group.T, ((0, pad_g), (0, 0))),
                          jnp.transpose(w_er, (0, 2, 1)).reshape(N_EXPERTS, D_MODEL)], axis=0)
    wr = jnp.pad(wr, ((0, RT_ROWS - wr.shape[0]), (0, 0))).astype(F32)
    br = jnp.concatenate([jnp.pad(b_group, (0, pad_g)), b_er.reshape(-1)])
    br = jnp.pad(br, (0, RT_ROWS - br.shape[0])).reshape(RT_ROWS, 1).astype(F32)
    row = lambda w: pl.BlockSpec((RT_TM, w), lambda i: (i, 0))
    const = lambda shape: pl.BlockSpec(shape, lambda i: (0, 0))
    return pl.pallas_call(
        _route_kernel,
        grid=(T // RT_TM,),
        in_specs=[row(ATTN_WIDTH), row(LRU_WIDTH), row(D_MODEL), const((D_MODEL, D_MODEL)), const((1, D_MODEL)),
                  const((RT_ROWS, D_MODEL)), const((RT_ROWS, 1))],
        out_specs=[row(D_MODEL), row(PACKED), row(RINFO),
                   pl.BlockSpec((1, RINFO, RT_TM), lambda i: (i, 0, 0)), const((1, RT_COLS))],
        out_shape=[jax.ShapeDtypeStruct((T, D_MODEL), F32),
                   jax.ShapeDtypeStruct((T, PACKED), U32),
                   jax.ShapeDtypeStruct((T, RINFO), F32),
                   jax.ShapeDtypeStruct((T // RT_TM, RINFO, RT_TM), jnp.int32),
                   jax.ShapeDtypeStruct((1, RT_COLS), jnp.int32)],
        scratch_shapes=[pltpu.VMEM((D_MODEL, D_MODEL), BF16),
                        pltpu.VMEM((2 * RT_ROWS, D_MODEL), BF16),
                        pltpu.VMEM((RT_TM, RT_TM), BF16),
                        pltpu.VMEM((1, RT_COLS), F32),
                        pltpu.VMEM((N_EXPERTS, 1), F32)],
        compiler_params=_cparams(1),
        name="out_route",
    )(attn_n, lru_n, x2, w_out, ln2.reshape(1, D_MODEL).astype(F32), wr, br)


def _moe_cap(T):
    A = T * TOP_K
    return ((A + MOE_BLOCK - 1) // MOE_BLOCK) * MOE_BLOCK + N_EXPERTS * MOE_BLOCK


PAD_BITS = tuple(1 << b for b in reversed(range(3, MOE_BLOCK.bit_length() - 1)))


def _layout_kernel(cnt_ref, ei_ref, dest_ref, pstart, be_ref, nu_ref, ge_ref):
    n_blocks = be_ref.shape[0]

    def lay(e, carry):
        start, blk, grp = carry
        pstart[e] = start
        nb = (cnt_ref[0, e] + MOE_BLOCK - 1) // MOE_BLOCK
        ge_ref[grp] = e

        def fill(k, c):
            be_ref[blk + k] = e
            return c
        lax.fori_loop(0, nb, fill, 0)
        return start + nb * MOE_BLOCK, blk + nb, grp + (nb > 0).astype(jnp.int32)
    _, used, groups = lax.fori_loop(0, N_EXPERTS, lay, (jnp.int32(0), jnp.int32(0), jnp.int32(0)))
    nu_ref[0] = used

    def tail(k, c):
        be_ref[k] = N_EXPERTS - 1
        return c
    lax.fori_loop(used, n_blocks, tail, 0)

    def no_group(k, c):
        ge_ref[k] = -1
        return c
    lax.fori_loop(groups, ge_ref.shape[0], no_group, 0)

    expert = ei_ref[:, 0:TOP_K, :]
    dest = ei_ref[:, TOP_K:2 * TOP_K, :]
    for e in range(N_EXPERTS):
        dest = dest + jnp.where(expert == e, pstart[e], 0)
    dest_ref[...] = dest


def _layout(ei, cnt, n_blocks):
    nt = ei.shape[0]
    smem = pl.BlockSpec(memory_space=pltpu.SMEM)
    vmem = pl.BlockSpec(memory_space=pltpu.VMEM)
    return pl.pallas_call(
        _layout_kernel,
        in_specs=[smem, vmem],
        out_specs=[vmem, smem, smem, smem, smem],
        out_shape=[jax.ShapeDtypeStruct((nt, TOP_K, RT_TM), jnp.int32),
                   jax.ShapeDtypeStruct((N_EXPERTS,), jnp.int32),
                   jax.ShapeDtypeStruct((n_blocks,), jnp.int32),
                   jax.ShapeDtypeStruct((1,), jnp.int32),
                   jax.ShapeDtypeStruct((N_EXPERTS + W_AHEAD,), jnp.int32)],
        name="layout",
    )(cnt, ei)


SC_CHUNK = 64
SC_BUFS = 3
SC_LEAD = SC_BUFS - 1


def _sc_workers():
    info = plsc.get_sparse_core_info()
    return info.num_cores, info.num_subcores


def _sc_ring(n_chunks, read, write):
    for c in range(min(SC_LEAD, n_chunks)):
        for cp in read(c):
            cp.start()
    reclaimed = set()
    for c in range(n_chunks):
        for cp in read(c):
            cp.wait()
        for cp in write(c):
            cp.start()
        nxt = c + SC_LEAD
        if nxt < n_chunks:
            if nxt - SC_BUFS >= 0:
                for cp in write(nxt - SC_BUFS):
                    cp.wait()
                reclaimed.add(nxt - SC_BUFS)
            for cp in read(nxt):
                cp.start()
    for c in range(n_chunks):
        if c not in reclaimed:
            for cp in write(c):
                cp.wait()


def _sc_dispatch(h2p, dest, cap):
    T = h2p.shape[0]
    nc, ns = _sc_workers()
    per_w = T // (nc * ns)
    n_ch = per_w // SC_CHUNK
    nt, _, tm = dest.shape
    assert nt * tm == T and tm % per_w == 0 and per_w % SC_CHUNK == 0
    idx = dest.reshape(nt, TOP_K, tm // per_w, per_w).transpose(0, 2, 1, 3).reshape(nc * ns, TOP_K * n_ch, SC_CHUNK)
    mesh = plsc.VectorSubcoreMesh(core_axis_name="c", subcore_axis_name="s")

    @functools.partial(
        pl.kernel, mesh=mesh,
        out_type=jax.ShapeDtypeStruct((cap, PACKED), U32),
        scratch_types=[pltpu.VMEM((TOP_K * n_ch, SC_CHUNK), jnp.int32),
                       pltpu.VMEM((SC_BUFS, SC_CHUNK, PACKED), U32),
                       pltpu.SemaphoreType.DMA((SC_BUFS,)),
                       pltpu.SemaphoreType.DMA((SC_BUFS,))])
    def scatter(src_hbm, idx_hbm, out_hbm, idx_v, rows_v, rsem, wsem):
        wid = lax.axis_index("s") * nc + lax.axis_index("c")
        base = pl.multiple_of(wid * per_w, per_w)
        pltpu.sync_copy(idx_hbm.at[wid], idx_v)

        def read(c):
            b = c % SC_BUFS
            return [pltpu.make_async_copy(src_hbm.at[pl.ds(base + c * SC_CHUNK, SC_CHUNK)], rows_v.at[b], rsem.at[b])]

        def write(c):
            b = c % SC_BUFS
            return [pltpu.make_async_copy(rows_v.at[b], out_hbm.at[idx_v.at[k * n_ch + c]], wsem.at[b])
                    for k in range(TOP_K)]
        _sc_ring(n_ch, read, write)

    return scatter(h2p, idx)


def _sc_gather(yb, dest):
    nt, _, tm = dest.shape
    nc, ns = _sc_workers()
    n_rows = nt * TOP_K * tm
    per_w = n_rows // (nc * ns)
    n_ch = per_w // SC_CHUNK
    assert per_w * nc * ns == n_rows and per_w % SC_CHUNK == 0
    mesh = plsc.VectorSubcoreMesh(core_axis_name="c", subcore_axis_name="s")

    @functools.partial(
        pl.kernel, mesh=mesh,
        out_type=jax.ShapeDtypeStruct((n_rows, PACKED), U32),
        scratch_types=[pltpu.VMEM((per_w,), jnp.int32),
                       pltpu.VMEM((SC_BUFS, SC_CHUNK, PACKED), U32),
                       pltpu.SemaphoreType.DMA((SC_BUFS,)),
                       pltpu.SemaphoreType.DMA((SC_BUFS,))])
    def gather(table_hbm, idx_hbm, out_hbm, idx_v, rows_v, gsem, wsem):
        wid = lax.axis_index("s") * nc + lax.axis_index("c")
        base = pl.multiple_of(wid * per_w, per_w)
        pltpu.sync_copy(idx_hbm.at[pl.ds(base, per_w)], idx_v)

        def read(c):
            b = c % SC_BUFS
            return [pltpu.make_async_copy(table_hbm.at[idx_v.at[pl.ds(c * SC_CHUNK, SC_CHUNK)]], rows_v.at[b], gsem.at[b])]

        def write(c):
            b = c % SC_BUFS
            return [pltpu.make_async_copy(rows_v.at[b], out_hbm.at[pl.ds(base + c * SC_CHUNK, SC_CHUNK)], wsem.at[b])]
        _sc_ring(n_ch, read, write)

    return gather(yb, dest.reshape(n_rows)).reshape(nt, TOP_K, tm, PACKED)


def _padfill_kernel(cnt_ref, pstart, xs_in, xs_ref, zeros, zsem):
    del xs_in

    def pad_copies(fn):
        for e in range(N_EXPERTS):
            cnt = cnt_ref[0, e]
            head = (-cnt) & (SUBLANES - 1)
            rest = ((-cnt) & (MOE_BLOCK - 1)) - head
            off = pstart[e] + cnt
            for k in range(SUBLANES - 1):
                @pl.when(k < head)
                def _(off=off, k=k):
                    fn(pltpu.make_async_copy(zeros.at[pl.ds(0, 1), :], xs_ref.at[pl.ds(off + k, 1), :], zsem))
            off = off + head
            for bit in PAD_BITS:
                @pl.when((rest & bit) != 0)
                def _(off=off, bit=bit):
                    fn(pltpu.make_async_copy(zeros.at[pl.ds(0, bit), :],
                                             xs_ref.at[pl.ds(pl.multiple_of(off, SUBLANES), bit), :], zsem))
                off = off + (rest & bit)

    zeros[...] = jnp.zeros_like(zeros)
    pad_copies(lambda cp: cp.start())
    pad_copies(lambda cp: cp.wait())


def _padfill(xs, pstart, cnt):
    smem = pl.BlockSpec(memory_space=pltpu.SMEM)
    hbm = pl.BlockSpec(memory_space=pl.ANY)
    return pl.pallas_call(
        _padfill_kernel,
        in_specs=[smem, smem, hbm],
        out_specs=hbm,
        out_shape=jax.ShapeDtypeStruct(xs.shape, xs.dtype),
        input_output_aliases={2: 0},
        scratch_shapes=[pltpu.VMEM((MOE_BLOCK // 2, PACKED), U32), pltpu.SemaphoreType.DMA(())],
        name="padfill",
    )(cnt, pstart, xs)


W_SLOTS = 3
W_AHEAD = W_SLOTS - 1
EXPERT_GROUP = 8
EXPERT_RUNS = (1, 2, 4)


def _expert_kernel(be_ref, nu_ref, ge_ref, x_ref, wg_hbm, wu_hbm, wd_hbm, o_ref,
                   wgf, wuf, wdf, wgb, wub, wdb, grp_ref, sems):
    step = pl.program_id(0)

    def weight_copies(e, slot):
        return (pltpu.make_async_copy(wg_hbm.at[e], wgf.at[slot], sems.at[slot, 0]),
                pltpu.make_async_copy(wu_hbm.at[e], wuf.at[slot], sems.at[slot, 1]),
                pltpu.make_async_copy(wd_hbm.at[e], wdf.at[slot], sems.at[slot, 2]))

    @pl.when(step == 0)
    def _():
        grp_ref[0] = 0
        for a in range(W_AHEAD):
            @pl.when(ge_ref[a] >= 0)
            def _(a=a):
                for cp in weight_copies(ge_ref[a], a):
                    cp.start()

    n_blocks = be_ref.shape[0]
    n_used = nu_ref[0]

    def swiglu(s, n):
        rows = pl.ds(pl.multiple_of(s * MOE_BLOCK, MOE_BLOCK), n * MOE_BLOCK)
        lo, hi = _unpack_rows(x_ref[rows, :])
        lo = lo.astype(BF16)
        hi = hi.astype(BF16)
        g = (jnp.dot(lo, wgb[:PACKED, :], preferred_element_type=F32)
             + jnp.dot(hi, wgb[PACKED:, :], preferred_element_type=F32))
        u = (jnp.dot(lo, wub[:PACKED, :], preferred_element_type=F32)
             + jnp.dot(hi, wub[PACKED:, :], preferred_element_type=F32))
        h = (g * _sigmoid(g) * u).astype(BF16)
        o_ref[rows, :] = _pack_rows(jnp.dot(h, wdb[...], preferred_element_type=F32))

    def run(s):
        j = step * EXPERT_GROUP + s
        e = be_ref[j]
        first = jnp.logical_or(j == 0, e != be_ref[jnp.maximum(j - 1, 0)])

        @pl.when(first)
        def _():
            grp = grp_ref[0]
            slot = grp % W_SLOTS
            for cp in weight_copies(e, slot):
                cp.wait()
            wgb[...] = wgf[slot].astype(BF16)
            wub[...] = wuf[slot].astype(BF16)
            wdb[...] = wdf[slot].astype(BF16)
            nxt = ge_ref[grp + W_AHEAD]

            @pl.when(nxt >= 0)
            def _():
                for cp in weight_copies(nxt, (grp + W_AHEAD) % W_SLOTS):
                    cp.start()
            grp_ref[0] = grp + 1

        def same(k):
            return (s + k < EXPERT_GROUP) & (j + k < n_used) & (be_ref[jnp.minimum(j + k, n_blocks - 1)] == e)
        take = jnp.int32(1)
        for n in EXPERT_RUNS[1:]:
            ok = same(n - 1)
            for k in range(1, n - 1):
                ok = ok & same(k)
            take = jnp.where(ok, n, take)
        for n in EXPERT_RUNS:
            @pl.when(take == n)
            def _(n=n):
                swiglu(s, n)
        return s + take

    lax.while_loop(lambda s: (s < EXPERT_GROUP) & (step * EXPERT_GROUP + s < n_used), run, jnp.int32(0))


def _experts(xs, block_expert, n_used, group_expert, w_gate, w_up, w_down):
    cap = xs.shape[0]
    n_blocks = cap // MOE_BLOCK
    assert n_blocks % EXPERT_GROUP == 0
    rows = EXPERT_GROUP * MOE_BLOCK
    last = lambda j, be, nu, ge: jnp.minimum(j, (nu[0] - 1) // EXPERT_GROUP)
    hbm = pl.BlockSpec(memory_space=pl.ANY)
    gs = pltpu.PrefetchScalarGridSpec(
        num_scalar_prefetch=3,
        grid=(n_blocks // EXPERT_GROUP,),
        in_specs=[pl.BlockSpec((rows, PACKED), lambda j, be, nu, ge: (last(j, be, nu, ge), 0)), hbm, hbm, hbm],
        out_specs=pl.BlockSpec((rows, PACKED), lambda j, be, nu, ge: (last(j, be, nu, ge), 0)),
        scratch_shapes=[pltpu.VMEM((W_SLOTS, D_MODEL, D_EXPERT), F32),
                        pltpu.VMEM((W_SLOTS, D_MODEL, D_EXPERT), F32),
                        pltpu.VMEM((W_SLOTS, D_EXPERT, D_MODEL), F32),
                        pltpu.VMEM((D_MODEL, D_EXPERT), BF16),
                        pltpu.VMEM((D_MODEL, D_EXPERT), BF16),
                        pltpu.VMEM((D_EXPERT, D_MODEL), BF16),
                        pltpu.SMEM((1,), jnp.int32),
                        pltpu.SemaphoreType.DMA((W_SLOTS, 3))],
    )
    return pl.pallas_call(
        _expert_kernel,
        grid_spec=gs,
        out_shape=jax.ShapeDtypeStruct((cap, PACKED), U32),
        compiler_params=_cparams(1),
        name="experts",
    )(block_expert, n_used, group_expert, xs, w_gate, w_up, w_down)


CB_TM = RT_TM


def _combine_kernel(x1_ref, gt_ref, y2_ref, o_ref):
    g = gt_ref[...]
    lo1, hi1 = _unpack_rows(y2_ref[0, 0])
    lo2, hi2 = _unpack_rows(y2_ref[0, 1])
    o_ref[:, :PACKED] = x1_ref[:, :PACKED] + g[:, 0:1] * lo1 + g[:, 1:2] * lo2
    o_ref[:, PACKED:] = x1_ref[:, PACKED:] + g[:, 0:1] * hi1 + g[:, 1:2] * hi2


def _combine(x1, gates, y2):
    T = x1.shape[0]
    nt = T // CB_TM
    return pl.pallas_call(
        _combine_kernel,
        grid=(nt,),
        in_specs=[pl.BlockSpec((CB_TM, D_MODEL), lambda i: (i, 0)),
                  pl.BlockSpec((CB_TM, RINFO), lambda i: (i, 0)),
                  pl.BlockSpec((1, TOP_K, CB_TM, PACKED), lambda i: (i, 0, 0, 0))],
        out_specs=pl.BlockSpec((CB_TM, D_MODEL), lambda i: (i, 0)),
        out_shape=jax.ShapeDtypeStruct((T, D_MODEL), F32),
        compiler_params=_cparams(1),
        name="combine",
    )(x1, gates, y2)


def _layer(x, rel_bias, ln1, w_in, q_norm, k_norm, attn_sink, conv_w, conv_b, lru_wa, lru_ba, lru_wi, lru_bi,
           lru_lambda, out_norm_attn, out_norm_lru, w_out, ln2, w_group, b_group, w_er, b_er, w_gate, w_up, w_down):
    B, S, D = x.shape
    T = B * S
    x2 = x.reshape(T, D)
    q, kv, xr, gr = _in_proj(x2, ln1, w_in, q_norm, k_norm)
    bias_tab = _bias_table(rel_bias)
    attn_n = _attention(q.reshape(B, S, ATTN_WIDTH), kv.reshape(B, S, 2 * KV_WIDTH), bias_tab, attn_sink,
                        out_norm_attn)
    lru_n = _rglru(xr.reshape(B, S, LRU_WIDTH), gr.reshape(B, S, LRU_WIDTH), conv_w, conv_b,
                   lru_wa, lru_ba, lru_wi, lru_bi, lru_lambda, out_norm_lru)
    x1, h2, gates, ei, cnt = _out_route(attn_n.reshape(T, ATTN_WIDTH), lru_n.reshape(T, LRU_WIDTH), x2, w_out, ln2,
                                        w_group, b_group, w_er, b_er)
    cap = _moe_cap(T)
    dest, pstart, block_expert, n_used, group_expert = _layout(ei, cnt, cap // MOE_BLOCK)
    xs = _padfill(_sc_dispatch(h2, dest, cap), pstart, cnt)
    yb = _experts(xs, block_expert, n_used, group_expert, w_gate, w_up, w_down)
    out = _combine(x1, gates, _sc_gather(yb, dest))
    return out.reshape(B, S, D)


def kernel(x, rel_bias, ln1, w_in, q_norm, k_norm, attn_sink, conv_w, conv_b, lru_wa, lru_ba, lru_wi, lru_bi,
           lru_lambda, out_norm_attn, out_norm_lru, w_out, ln2, w_group, b_group, w_expert_router, b_expert_router,
           w_gate, w_up, w_down):
    depth = ln1.shape[0]
    for l in range(depth):
        x = _layer(x, rel_bias, ln1[l], w_in[l], q_norm[l], k_norm[l], attn_sink[l], conv_w[l], conv_b[l],
                   lru_wa[l], lru_ba[l], lru_wi[l], lru_bi[l], lru_lambda[l], out_norm_attn[l], out_norm_lru[l],
                   w_out[l], ln2[l], w_group[l], b_group[l], w_expert_router[l], b_expert_router[l],
                   w_gate[l], w_up[l], w_down[l])
    return x
```

```python
import functools
import math

import jax
import jax.numpy as jnp
import numpy as np
from jax import lax
from jax.experimental import pallas as pl
from jax.experimental.pallas import tpu as pltpu
from jax.experimental.pallas import tpu_sc as plsc

D_MODEL = 1024
N_HEADS = 8
N_KV_HEADS = 2
HEAD_DIM = 64
Q_PER_KV = N_HEADS // N_KV_HEADS
ATTN_WIDTH = N_HEADS * HEAD_DIM
KV_WIDTH = N_KV_HEADS * HEAD_DIM
WINDOW = 128
BLOCK = 128
NUM_BUCKETS = 32
MAX_DISTANCE = 128
LRU_WIDTH = D_MODEL - ATTN_WIDTH
LRU_BLOCKS = 8
LRU_BLOCK_DIM = LRU_WIDTH // LRU_BLOCKS
LRU_C = 8.0
CONV_W = 4
CONV_LEFT = 2
N_GROUPS = 4
EXPERTS_PER_GROUP = 8
N_EXPERTS = N_GROUPS * EXPERTS_PER_GROUP
TOP_K = 2
D_EXPERT = 512
MOE_BLOCK = 256
EPS = 1e-6
NEG_INF = -1e30

LANES = 128
SUBLANES = 8
VMEM_LIMIT = 56 * 1024 * 1024

F32 = jnp.float32
BF16 = jnp.bfloat16


def _cparams(n_axes, vmem=VMEM_LIMIT):
    return pltpu.CompilerParams(dimension_semantics=("arbitrary",) * n_axes, vmem_limit_bytes=vmem)


def _rms(x, gain):
    return x * lax.rsqrt(jnp.mean(x * x, axis=-1, keepdims=True) + EPS) * gain


U32 = jnp.uint32
HI_MASK = 0xFFFF0000
PACKED = D_MODEL // 2


def _pack_rows(x):
    h = x.shape[1] // 2
    lo = lax.bitcast_convert_type(x[:, :h].astype(BF16).astype(F32), U32) >> 16
    hi = lax.bitcast_convert_type(x[:, h:].astype(BF16).astype(F32), U32) & jnp.uint32(HI_MASK)
    return lo | hi


def _unpack_rows(p):
    lo = lax.bitcast_convert_type(p << 16, F32)
    hi = lax.bitcast_convert_type(p & jnp.uint32(HI_MASK), F32)
    return lo, hi


IN_TM = 1024


def _head_rms(x, n_heads, gain):
    head = lax.broadcasted_iota(jnp.int32, (1, n_heads * HEAD_DIM), 1) // HEAD_DIM
    x2 = x * x
    scale = jnp.zeros_like(x)
    for h in range(n_heads):
        ms = jnp.sum(jnp.where(head == h, x2, 0.0), axis=-1, keepdims=True) * (1.0 / HEAD_DIM)
        scale = jnp.where(head == h, lax.rsqrt(ms + EPS), scale)
    return x * scale * gain


def _in_proj_kernel(x_ref, g_ref, w_ref, qg_ref, kg_ref, q_ref, kv_ref, xr_ref, gr_ref, wb_ref):
    @pl.when(pl.program_id(0) == 0)
    def _():
        wb_ref[...] = w_ref[...].astype(BF16)

    h = _rms(x_ref[...], g_ref[...]).astype(BF16)
    c_k = ATTN_WIDTH
    c_v = c_k + KV_WIDTH
    c_x = c_v + KV_WIDTH
    c_g = c_x + LRU_WIDTH
    q = jnp.dot(h, wb_ref[:, :c_k], preferred_element_type=F32)
    q_ref[...] = _head_rms(q, N_HEADS, qg_ref[...]).astype(BF16)
    k = jnp.dot(h, wb_ref[:, c_k:c_v], preferred_element_type=F32)
    kv_ref[:, :KV_WIDTH] = _head_rms(k, N_KV_HEADS, kg_ref[...]).astype(BF16)
    kv_ref[:, KV_WIDTH:] = jnp.dot(h, wb_ref[:, c_v:c_x], preferred_element_type=F32).astype(BF16)
    xr_ref[...] = jnp.dot(h, wb_ref[:, c_x:c_g], preferred_element_type=F32)
    gr_ref[...] = jnp.dot(h, wb_ref[:, c_g:], preferred_element_type=F32)


def _in_proj(x2, ln1, w_in, q_gain, k_gain):
    T = x2.shape[0]
    n_in = w_in.shape[1]
    row = lambda w: pl.BlockSpec((IN_TM, w), lambda i: (i, 0))
    qg = (jnp.tile(q_gain.astype(F32), N_HEADS) * (HEAD_DIM ** -0.5)).reshape(1, ATTN_WIDTH)
    kg = jnp.tile(k_gain.astype(F32), N_KV_HEADS).reshape(1, KV_WIDTH)
    return pl.pallas_call(
        _in_proj_kernel,
        grid=(T // IN_TM,),
        in_specs=[row(D_MODEL),
                  pl.BlockSpec((1, D_MODEL), lambda i: (0, 0)),
                  pl.BlockSpec((D_MODEL, n_in), lambda i: (0, 0)),
                  pl.BlockSpec((1, ATTN_WIDTH), lambda i: (0, 0)),
                  pl.BlockSpec((1, KV_WIDTH), lambda i: (0, 0))],
        out_specs=[row(ATTN_WIDTH), row(2 * KV_WIDTH), row(LRU_WIDTH), row(LRU_WIDTH)],
        out_shape=[jax.ShapeDtypeStruct((T, ATTN_WIDTH), BF16),
                   jax.ShapeDtypeStruct((T, 2 * KV_WIDTH), BF16),
                   jax.ShapeDtypeStruct((T, LRU_WIDTH), F32),
                   jax.ShapeDtypeStruct((T, LRU_WIDTH), F32)],
        scratch_shapes=[pltpu.VMEM((D_MODEL, n_in), BF16)],
        compiler_params=_cparams(1),
        name="in_proj",
    )(x2, ln1.reshape(1, D_MODEL), w_in, qg, kg)


def _t5_bucket(rel):
    half = NUM_BUCKETS // 2
    max_exact = half // 2
    base = jnp.where(rel > 0, half, 0)
    n = jnp.abs(rel)
    nf = jnp.maximum(n, 1).astype(jnp.float32)
    large = max_exact + (jnp.log(nf / max_exact) / math.log(MAX_DISTANCE / max_exact)
                         * (half - max_exact)).astype(jnp.int32)
    large = jnp.minimum(large, half - 1)
    return base + jnp.where(n < max_exact, n, large)


HEAD_PAIRS = Q_PER_KV // 2
EDGE_VARIANTS = 3


def _bias_kernel(rb_ref, bucket_ref, band_ref, o_ref):
    bucket = bucket_ref[...]
    band = band_ref[...] > 0
    col = lax.broadcasted_iota(jnp.int32, bucket.shape, 1)
    valid = (band & (col >= BLOCK), band, band & (col < 2 * BLOCK))
    for h in range(N_HEADS):
        acc = jnp.zeros(bucket.shape, F32)
        for b in range(NUM_BUCKETS):
            acc = jnp.where(bucket == b, rb_ref[b, h], acc)
        kv, g = divmod(h, Q_PER_KV)
        pair, parity = divmod(g, 2)
        for var in range(EDGE_VARIANTS):
            o_ref[var, kv, parity, pair * BLOCK:(pair + 1) * BLOCK, :] = jnp.where(valid[var], acc, NEG_INF)


def _bias_table(rel_bias):
    qi = jnp.arange(BLOCK, dtype=jnp.int32)
    kj = jnp.arange(3 * BLOCK, dtype=jnp.int32)
    rel = kj[None, :] - BLOCK - qi[:, None]
    bucket = _t5_bucket(rel).astype(jnp.int32)
    band = (jnp.abs(rel) <= WINDOW).astype(jnp.int32)
    return pl.pallas_call(
        _bias_kernel,
        in_specs=[pl.BlockSpec(memory_space=pltpu.SMEM),
                  pl.BlockSpec(memory_space=pltpu.VMEM),
                  pl.BlockSpec(memory_space=pltpu.VMEM)],
        out_specs=pl.BlockSpec(memory_space=pltpu.VMEM),
        out_shape=jax.ShapeDtypeStruct((EDGE_VARIANTS, N_KV_HEADS, 2, HEAD_PAIRS * BLOCK, 3 * BLOCK), F32),
        name="bias_table",
    )(rel_bias.astype(F32), bucket, band)


def _attn_kernel(sink_ref, q_ref, kp_ref, kc_ref, kn_ref, bias_ref, og_ref, o_ref):
    n = pl.program_id(1)
    kv_all = jnp.concatenate([kp_ref[0], kc_ref[0], kn_ref[0]], axis=0)
    for qb in range(ATTN_QB):
        variant = 1
        if qb == 0:
            variant = jnp.where(n == 0, 0, 1)
        if qb == ATTN_QB - 1:
            variant = jnp.where(n == pl.num_programs(1) - 1, 2, variant)
        out = _attn_block(q_ref[0, qb * BLOCK:(qb + 1) * BLOCK, :], kv_all[qb * BLOCK:(qb + 3) * BLOCK, :],
                          lambda kv, parity: bias_ref[variant, kv, parity], sink_ref)
        o_ref[0, qb * BLOCK:(qb + 1) * BLOCK, :] = _rms(out, og_ref[...]).astype(o_ref.dtype)


def _attn_block(q, kvw, bias, sink_ref):
    low = lax.broadcasted_iota(jnp.int32, (1, LANES), 1) < HEAD_DIM
    swap = lambda slab: pltpu.roll(slab.astype(F32), HEAD_DIM, 1).astype(BF16)
    kslab, vslab = kvw[:, :KV_WIDTH], kvw[:, KV_WIDTH:]
    kslab_sw, vslab_sw = swap(kslab), swap(vslab)
    rowi = lax.broadcasted_iota(jnp.int32, (HEAD_PAIRS * BLOCK, 1), 0)
    combos = [(kv, parity) for kv in range(N_KV_HEADS) for parity in range(2)]
    scores, vzs, sinks = [], [], []
    for kv, parity in combos:
        ks, vs = (kslab, vslab) if (kv == 0) == (parity == 0) else (kslab_sw, vslab_sw)
        keep = low if parity == 0 else jnp.logical_not(low)
        kz = jnp.where(keep, ks, jnp.zeros_like(ks))
        vzs.append(jnp.where(keep, vs, jnp.zeros_like(vs)))
        base = kv * Q_PER_KV * HEAD_DIM
        qpair = jnp.concatenate([q[:, base + j * LANES:base + (j + 1) * LANES] for j in range(HEAD_PAIRS)], axis=0)
        s = lax.dot_general(qpair, kz, (((1,), (1,)), ((), ())), preferred_element_type=F32)
        scores.append(s + bias(kv, parity))
        sink = jnp.zeros((HEAD_PAIRS * BLOCK, 1), F32)
        for j in range(HEAD_PAIRS):
            sink = jnp.where(rowi // BLOCK == j, sink_ref[kv * Q_PER_KV + 2 * j + parity], sink)
        sinks.append(sink)
    probs, inv = [], []
    for s, sink in zip(scores, sinks):
        m = jnp.maximum(jnp.max(s, axis=-1, keepdims=True), sink)
        p = jnp.exp(s - m)
        inv.append(1.0 / (jnp.sum(p, axis=-1, keepdims=True) + jnp.exp(sink - m)))
        probs.append(p.astype(BF16))
    outs = [jnp.dot(p, vz, preferred_element_type=F32) * r for p, vz, r in zip(probs, vzs, inv)]
    cols = []
    for kv in range(N_KV_HEADS):
        acc = outs[2 * kv] + outs[2 * kv + 1]
        cols += [acc[j * BLOCK:(j + 1) * BLOCK, :] for j in range(HEAD_PAIRS)]
    return jnp.concatenate(cols, axis=1)


ATTN_QB = 4


def _attention(q, kv, bias_tab, sink, out_gain):
    B, S, _ = q.shape
    nb = S // BLOCK
    assert ATTN_QB >= 2 and nb % ATTN_QB == 0, "a step's first and last query blocks must be distinct"
    ns = nb // ATTN_QB
    rows = ATTN_QB * BLOCK
    kvspec = lambda f: pl.BlockSpec((1, BLOCK, 2 * KV_WIDTH), f)
    return pl.pallas_call(
        _attn_kernel,
        grid=(B, ns),
        in_specs=[pl.BlockSpec(memory_space=pltpu.SMEM),
                  pl.BlockSpec((1, rows, ATTN_WIDTH), lambda b, n: (b, n, 0)),
                  kvspec(lambda b, n: (b, jnp.maximum(n * ATTN_QB - 1, 0), 0)),
                  pl.BlockSpec((1, rows, 2 * KV_WIDTH), lambda b, n: (b, n, 0)),
                  kvspec(lambda b, n: (b, jnp.minimum((n + 1) * ATTN_QB, nb - 1), 0)),
                  pl.BlockSpec((EDGE_VARIANTS, N_KV_HEADS, 2, HEAD_PAIRS * BLOCK, 3 * BLOCK),
                               lambda b, n: (0, 0, 0, 0, 0)),
                  pl.BlockSpec((1, ATTN_WIDTH), lambda b, n: (0, 0))],
        out_specs=pl.BlockSpec((1, rows, ATTN_WIDTH), lambda b, n: (b, n, 0)),
        out_shape=jax.ShapeDtypeStruct((B, S, ATTN_WIDTH), BF16),
        compiler_params=_cparams(2),
        name="attention",
    )(sink.astype(F32), q, kv, kv, kv, bias_tab, out_gain.reshape(1, ATTN_WIDTH))


LRU_TC = 128
LRU_PITCH = LRU_TC + SUBLANES
LRU_SLABS = LRU_WIDTH // LANES
HALO = SUBLANES


def _softplus(x):
    return jnp.maximum(x, 0.0) + jnp.log(1.0 + jnp.exp(-jnp.abs(x)))


def _gelu_tanh(x):
    k = math.sqrt(2.0 / math.pi)
    hx = 0.5 * x
    return hx + hx * jnp.tanh(x * (k + (k * 0.044715) * (x * x)))


def _sigmoid(x):
    return 0.5 + 0.5 * jnp.tanh(0.5 * x)


def _rglru_kernel(xr_ref, xp_ref, xn_ref, gr_ref, cw_ref, cb_ref, wa_ref, wi_ref, ba_ref, bi_ref, lam_ref, og_ref,
                  o_ref, sx_ref, a_ref, u_ref, h_ref, carry_ref, hf_ref, wg_ref, bg_ref, k_ref):
    p = pl.program_id(0)
    i = pl.program_id(1)
    nc = pl.num_programs(1)
    c = i + p * (nc - 1 - 2 * i)
    B = xr_ref.shape[0]
    TC = LRU_TC

    @pl.when(i == 0)
    def _():
        carry_ref[...] = jnp.zeros_like(carry_ref)
        wg_ref[...] = jnp.zeros_like(wg_ref)
        for sel, w_ref in enumerate((wa_ref, wi_ref)):
            for h in range(LRU_BLOCKS):
                lo = h * LRU_BLOCK_DIM
                wg_ref[lo:lo + LRU_BLOCK_DIM, sel * LRU_WIDTH + lo:sel * LRU_WIDTH + lo + LRU_BLOCK_DIM] = (
                    0.5 * w_ref[0, h]).astype(BF16)
        row = pl.ds(p, 1)
        bg_ref[:, :LRU_WIDTH] = 0.5 * ba_ref[row, :]
        bg_ref[:, LRU_WIDTH:] = 0.5 * bi_ref[row, :]
        k_ref[...] = (-0.5 * LRU_C * math.log2(math.e)) * _softplus(-lam_ref[row, :])

    sx_ref[:, HALO:HALO + TC, :] = xr_ref[...]
    sx_ref[:, 0:HALO, :] = jnp.where(c > 0, xp_ref[...], 0.0)
    sx_ref[:, HALO + TC:, :] = jnp.where(c < nc - 1, xn_ref[...], 0.0)
    xc = cb_ref[...][None]
    for j in range(CONV_W):
        off = HALO + j - CONV_LEFT
        xc = xc + cw_ref[j:j + 1, :][None] * sx_ref[:, off:off + TC, :]
    xc2 = xc.reshape(B * TC, LRU_WIDTH)

    g = jnp.dot(xc2.astype(BF16), wg_ref[...], preferred_element_type=F32) + bg_ref[...]
    ta = jnp.tanh(g[:, :LRU_WIDTH])
    ig = 0.5 + 0.5 * jnp.tanh(g[:, LRU_WIDTH:])
    a = jnp.exp2((1.0 + ta) * k_ref[...])
    z = 1.0 - a * a
    u = z * lax.rsqrt(jnp.maximum(z, 1e-30)) * ig * xc2
    for b in range(B):
        for s in range(LRU_SLABS):
            a_ref[s, b * LRU_PITCH:b * LRU_PITCH + TC, :] = a[b * TC:(b + 1) * TC, s * LANES:(s + 1) * LANES]
            u_ref[s, b * LRU_PITCH:b * LRU_PITCH + TC, :] = u[b * TC:(b + 1) * TC, s * LANES:(s + 1) * LANES]

    def step(k, hs):
        t = k + p * (TC - 1 - 2 * k)
        out = []
        for s in range(LRU_SLABS):
            idx = pl.ds(t, B, stride=LRU_PITCH)
            hn = a_ref[s, idx, :] * hs[s] + u_ref[s, idx, :]
            h_ref[s, idx, :] = hn
            out.append(hn)
        return tuple(out)

    hs = lax.fori_loop(0, TC, step, tuple(carry_ref[s] for s in range(LRU_SLABS)), unroll=8)
    for s in range(LRU_SLABS):
        carry_ref[s] = hs[s]

    @pl.when(p == 0)
    def _():
        for b in range(B):
            for s in range(LRU_SLABS):
                hf_ref[c, s, b * TC:(b + 1) * TC, :] = h_ref[s, b * LRU_PITCH:b * LRU_PITCH + TC, :].astype(hf_ref.dtype)

    @pl.when(p == 1)
    def _():
        for b in range(B):
            hsum = jnp.concatenate(
                [h_ref[s, b * LRU_PITCH:b * LRU_PITCH + TC, :] + hf_ref[c, s, b * TC:(b + 1) * TC, :].astype(F32)
                 for s in range(LRU_SLABS)], axis=1)
            y = hsum * _gelu_tanh(gr_ref[b])
            o_ref[b] = _rms(y, og_ref[...]).astype(o_ref.dtype)


def _rglru(xr, gr, conv_w, conv_b, w_a, b_a, w_i, b_i, lam, out_gain):
    B, S, W = xr.shape
    nc = S // LRU_TC
    hb = LRU_TC // HALO
    chunk = lambda p, i: i + p * (nc - 1 - 2 * i)
    full2 = lambda shape: pl.BlockSpec(shape, lambda p, i: (0,) * len(shape))
    wblock = pl.BlockSpec((1, LRU_BLOCKS, LRU_BLOCK_DIM, LRU_BLOCK_DIM), lambda p, i: (p, 0, 0, 0))
    return pl.pallas_call(
        _rglru_kernel,
        grid=(2, nc),
        in_specs=[pl.BlockSpec((B, LRU_TC, W), lambda p, i: (0, chunk(p, i), 0)),
                  pl.BlockSpec((B, HALO, W), lambda p, i: (0, jnp.maximum(chunk(p, i) * hb - 1, 0), 0)),
                  pl.BlockSpec((B, HALO, W), lambda p, i: (0, jnp.minimum((chunk(p, i) + 1) * hb, S // HALO - 1), 0)),
                  pl.BlockSpec((B, LRU_TC, W), lambda p, i: (0, chunk(p, i), 0)),
                  full2((CONV_W, W)),
                  full2((1, W)),
                  wblock, wblock,
                  full2((2, W)), full2((2, W)), full2((2, W)),
                  full2((1, W))],
        out_specs=pl.BlockSpec((B, LRU_TC, W), lambda p, i: (0, nc - 1 - p * i, 0)),
        out_shape=jax.ShapeDtypeStruct((B, S, W), BF16),
        scratch_shapes=[pltpu.VMEM((B, LRU_TC + 2 * HALO, W), F32),
                        pltpu.VMEM((LRU_SLABS, B * LRU_PITCH, LANES), F32),
                        pltpu.VMEM((LRU_SLABS, B * LRU_PITCH, LANES), F32),
                        pltpu.VMEM((LRU_SLABS, B * LRU_PITCH, LANES), F32),
                        pltpu.VMEM((LRU_SLABS, B, LANES), F32),
                        pltpu.VMEM((nc, LRU_SLABS, B * LRU_TC, LANES), BF16),
                        pltpu.VMEM((W, 2 * W), BF16),
                        pltpu.VMEM((1, 2 * W), F32),
                        pltpu.VMEM((1, W), F32)],
        compiler_params=_cparams(2),
        name="rglru",
    )(xr, xr, xr, gr, conv_w.astype(F32), conv_b.reshape(1, W).astype(F32), w_a.astype(F32), w_i.astype(F32),
      b_a.astype(F32), b_i.astype(F32), lam.astype(F32), out_gain.reshape(1, W).astype(F32))


RT_TM = 1024
RT_COLS = LANES
RT_ROWS = 48
RINFO = SUBLANES


def _split_bf16(x):
    hi = x.astype(BF16)
    lo = (x - hi.astype(F32)).astype(BF16)
    return hi, lo


def _route_kernel(an_ref, ln_ref, x_ref, wo_ref, g2_ref, wr_ref, br_ref,
                  x1_ref, h2_ref, gt_ref, ei_ref, cnt_ref, wob_ref, wrb_ref, tri_ref, run_ref, runc_ref):
    @pl.when(pl.program_id(0) == 0)
    def _():
        wob_ref[...] = wo_ref[...].astype(BF16)
        hi, lo = _split_bf16(wr_ref[...])
        wrb_ref[:RT_ROWS, :] = hi
        wrb_ref[RT_ROWS:, :] = lo
        r = lax.broadcasted_iota(jnp.int32, (RT_TM, RT_TM), 0)
        cidx = lax.broadcasted_iota(jnp.int32, (RT_TM, RT_TM), 1)
        tri_ref[...] = (r < cidx).astype(BF16)
        run_ref[...] = jnp.zeros_like(run_ref)
        runc_ref[...] = jnp.zeros_like(runc_ref)

    x1 = (x_ref[...]
          + jnp.dot(an_ref[...], wob_ref[:ATTN_WIDTH, :], preferred_element_type=F32)
          + jnp.dot(ln_ref[...], wob_ref[ATTN_WIDTH:, :], preferred_element_type=F32))
    x1_ref[...] = x1
    h2 = _rms(x1, g2_ref[...])
    h2_ref[...] = _pack_rows(h2)

    hi, lo = _split_bf16(h2)
    nt_dims = (((1,), (1,)), ((), ()))
    t1 = lax.dot_general(wrb_ref[...], hi, nt_dims, preferred_element_type=F32)
    t2 = lax.dot_general(wrb_ref[:RT_ROWS, :], lo, nt_dims, preferred_element_type=F32)
    logit = t1[:RT_ROWS] + t1[RT_ROWS:] + t2 + br_ref[...]

    sub = lax.broadcasted_iota(jnp.int32, (SUBLANES, RT_TM), 0)
    first_min = lambda hit: jnp.min(jnp.where(hit, sub, SUBLANES), axis=0, keepdims=True)
    is_g = sub < N_GROUPS
    gl = jnp.where(is_g, logit[:SUBLANES], -jnp.inf)
    gm = jnp.max(gl, axis=0, keepdims=True)
    gidx = first_min(gl == gm)
    g_p = 1.0 / jnp.sum(jnp.where(is_g, jnp.exp(logit[:SUBLANES] - gm), 0.0), axis=0, keepdims=True)
    el = logit[SUBLANES:2 * SUBLANES]
    for g in range(1, N_GROUPS):
        el = jnp.where(gidx == g, logit[(g + 1) * SUBLANES:(g + 2) * SUBLANES], el)
    m1 = jnp.max(el, axis=0, keepdims=True)
    i1 = first_min(el == m1)
    el2 = jnp.where(sub == i1, -jnp.inf, el)
    m2 = jnp.max(el2, axis=0, keepdims=True)
    i2 = first_min(el2 == m2)
    t = jnp.exp(m2 - m1)
    gate1 = g_p / (1.0 + t)
    gate2 = g_p * t / (1.0 + t)
    e1 = gidx * EXPERTS_PER_GROUP + i1
    e2 = gidx * EXPERTS_PER_GROUP + i2

    erow = lax.broadcasted_iota(jnp.int32, (N_EXPERTS, RT_TM), 0)
    oh1 = erow == e1
    oh2 = erow == e2
    oh = (oh1 | oh2).astype(F32)
    ohb = oh.astype(BF16)
    cum = jnp.dot(ohb, tri_ref[...], preferred_element_type=F32) + runc_ref[...]
    rank1 = jnp.sum(jnp.where(oh1, cum, 0.0), axis=0, keepdims=True)
    rank2 = jnp.sum(jnp.where(oh2, cum, 0.0), axis=0, keepdims=True)
    runc_ref[...] = runc_ref[...] + jnp.sum(oh, axis=1, keepdims=True)
    tile_cnt = lax.dot_general(jnp.ones((SUBLANES, RT_TM), BF16), ohb, nt_dims, preferred_element_type=F32)
    run_ref[:, :N_EXPERTS] = run_ref[:, :N_EXPERTS] + tile_cnt[0:1]
    cnt_ref[...] = run_ref[...].astype(jnp.int32)

    rows = [e1, e2, rank1.astype(jnp.int32), rank2.astype(jnp.int32)]
    ei = jnp.zeros((RINFO, RT_TM), jnp.int32)
    for k, v in enumerate(rows):
        ei = jnp.where(sub == k, v, ei)
    ei_ref[0] = ei
    gt_ref[...] = jnp.where(sub == 0, gate1, jnp.where(sub == 1, gate2, 0.0)).T


def _out_route(attn_n, lru_n, x2, w_out, ln2, w_group, b_group, w_er, b_er):
    T = x2.shape[0]
    pad_g = SUBLANES - N_GROUPS
    wr = jnp.concatenate([jnp.pad(w_group.T, ((0, pad_g), (0, 0))),
                          jnp.transpose(w_er, (0, 2, 1)).reshape(N_EXPERTS, D_MODEL)], axis=0)
    wr = jnp.pad(wr, ((0, RT_ROWS - wr.shape[0]), (0, 0))).astype(F32)
    br = jnp.concatenate([jnp.pad(b_group, (0, pad_g)), b_er.reshape(-1)])
    br = jnp.pad(br, (0, RT_ROWS - br.shape[0])).reshape(RT_ROWS, 1).astype(F32)
    row = lambda w: pl.BlockSpec((RT_TM, w), lambda i: (i, 0))
    const = lambda shape: pl.BlockSpec(shape, lambda i: (0, 0))
    return pl.pallas_call(
        _route_kernel,
        grid=(T // RT_TM,),
        in_specs=[row(ATTN_WIDTH), row(LRU_WIDTH), row(D_MODEL), const((D_MODEL, D_MODEL)), const((1, D_MODEL)),
                  const((RT_ROWS, D_MODEL)), const((RT_ROWS, 1))],
        out_specs=[row(D_MODEL), row(PACKED), row(RINFO),
                   pl.BlockSpec((1, RINFO, RT_TM), lambda i: (i, 0, 0)), const((1, RT_COLS))],
        out_shape=[jax.ShapeDtypeStruct((T, D_MODEL), F32),
                   jax.ShapeDtypeStruct((T, PACKED), U32),
                   jax.ShapeDtypeStruct((T, RINFO), F32),
                   jax.ShapeDtypeStruct((T // RT_TM, RINFO, RT_TM), jnp.int32),
                   jax.ShapeDtypeStruct((1, RT_COLS), jnp.int32)],
        scratch_shapes=[pltpu.VMEM((D_MODEL, D_MODEL), BF16),
                        pltpu.VMEM((2 * RT_ROWS, D_MODEL), BF16),
                        pltpu.VMEM((RT_TM, RT_TM), BF16),
                        pltpu.VMEM((1, RT_COLS), F32),
                        pltpu.VMEM((N_EXPERTS, 1), F32)],
        compiler_params=_cparams(1),
        name="out_route",
    )(attn_n, lru_n, x2, w_out, ln2.reshape(1, D_MODEL).astype(F32), wr, br)


def _moe_cap(T):
    A = T * TOP_K
    return ((A + MOE_BLOCK - 1) // MOE_BLOCK) * MOE_BLOCK + N_EXPERTS * MOE_BLOCK


PAD_BITS = tuple(1 << b for b in reversed(range(3, MOE_BLOCK.bit_length() - 1)))


def _layout_kernel(cnt_ref, ei_ref, dest_ref, pstart, be_ref, nu_ref, ge_ref):
    n_blocks = be_ref.shape[0]

    def lay(e, carry):
        start, blk, grp = carry
        pstart[e] = start
        nb = (cnt_ref[0, e] + MOE_BLOCK - 1) // MOE_BLOCK
        ge_ref[grp] = e

        def fill(k, c):
            be_ref[blk + k] = e
            return c
        lax.fori_loop(0, nb, fill, 0)
        return start + nb * MOE_BLOCK, blk + nb, grp + (nb > 0).astype(jnp.int32)
    _, used, groups = lax.fori_loop(0, N_EXPERTS, lay, (jnp.int32(0), jnp.int32(0), jnp.int32(0)))
    nu_ref[0] = used

    def tail(k, c):
        be_ref[k] = N_EXPERTS - 1
        return c
    lax.fori_loop(used, n_blocks, tail, 0)

    def no_group(k, c):
        ge_ref[k] = -1
        return c
    lax.fori_loop(groups, ge_ref.shape[0], no_group, 0)

    expert = ei_ref[:, 0:TOP_K, :]
    dest = ei_ref[:, TOP_K:2 * TOP_K, :]
    for e in range(N_EXPERTS):
        dest = dest + jnp.where(expert == e, pstart[e], 0)
    dest_ref[...] = dest


def _layout(ei, cnt, n_blocks):
    nt = ei.shape[0]
    smem = pl.BlockSpec(memory_space=pltpu.SMEM)
    vmem = pl.BlockSpec(memory_space=pltpu.VMEM)
    return pl.pallas_call(
        _layout_kernel,
        in_specs=[smem, vmem],
        out_specs=[vmem, smem, smem, smem, smem],
        out_shape=[jax.ShapeDtypeStruct((nt, TOP_K, RT_TM), jnp.int32),
                   jax.ShapeDtypeStruct((N_EXPERTS,), jnp.int32),
                   jax.ShapeDtypeStruct((n_blocks,), jnp.int32),
                   jax.ShapeDtypeStruct((1,), jnp.int32),
                   jax.ShapeDtypeStruct((N_EXPERTS + W_AHEAD,), jnp.int32)],
        name="layout",
    )(cnt, ei)


SC_CHUNK = 64
SC_BUFS = 3
SC_LEAD = SC_BUFS - 1


def _sc_workers():
    info = plsc.get_sparse_core_info()
    return info.num_cores, info.num_subcores


def _sc_ring(n_chunks, read, write):
    for c in range(min(SC_LEAD, n_chunks)):
        for cp in read(c):
            cp.start()
    reclaimed = set()
    for c in range(n_chunks):
        for cp in read(c):
            cp.wait()
        for cp in write(c):
            cp.start()
        nxt = c + SC_LEAD
        if nxt < n_chunks:
            if nxt - SC_BUFS >= 0:
                for cp in write(nxt - SC_BUFS):
                    cp.wait()
                reclaimed.add(nxt - SC_BUFS)
            for cp in read(nxt):
                cp.start()
    for c in range(n_chunks):
        if c not in reclaimed:
            for cp in write(c):
                cp.wait()


def _sc_dispatch(h2p, dest, cap):
    T = h2p.shape[0]
    nc, ns = _sc_workers()
    per_w = T // (nc * ns)
    n_ch = per_w // SC_CHUNK
    nt, _, tm = dest.shape
    assert nt * tm == T and tm % per_w == 0 and per_w % SC_CHUNK == 0
    idx = dest.reshape(nt, TOP_K, tm // per_w, per_w).transpose(0, 2, 1, 3).reshape(nc * ns, TOP_K * n_ch, SC_CHUNK)
    mesh = plsc.VectorSubcoreMesh(core_axis_name="c", subcore_axis_name="s")

    @functools.partial(
        pl.kernel, mesh=mesh,
        out_type=jax.ShapeDtypeStruct((cap, PACKED), U32),
        scratch_types=[pltpu.VMEM((TOP_K * n_ch, SC_CHUNK), jnp.int32),
                       pltpu.VMEM((SC_BUFS, SC_CHUNK, PACKED), U32),
                       pltpu.SemaphoreType.DMA((SC_BUFS,)),
                       pltpu.SemaphoreType.DMA((SC_BUFS,))])
    def scatter(src_hbm, idx_hbm, out_hbm, idx_v, rows_v, rsem, wsem):
        wid = lax.axis_index("s") * nc + lax.axis_index("c")
        base = pl.multiple_of(wid * per_w, per_w)
        pltpu.sync_copy(idx_hbm.at[wid], idx_v)

        def read(c):
            b = c % SC_BUFS
            return [pltpu.make_async_copy(src_hbm.at[pl.ds(base + c * SC_CHUNK, SC_CHUNK)], rows_v.at[b], rsem.at[b])]

        def write(c):
            b = c % SC_BUFS
            return [pltpu.make_async_copy(rows_v.at[b], out_hbm.at[idx_v.at[k * n_ch + c]], wsem.at[b])
                    for k in range(TOP_K)]
        _sc_ring(n_ch, read, write)

    return scatter(h2p, idx)


def _sc_gather(yb, dest):
    nt, _, tm = dest.shape
    nc, ns = _sc_workers()
    n_rows = nt * TOP_K * tm
    per_w = n_rows // (nc * ns)
    n_ch = per_w // SC_CHUNK
    assert per_w * nc * ns == n_rows and per_w % SC_CHUNK == 0
    mesh = plsc.VectorSubcoreMesh(core_axis_name="c", subcore_axis_name="s")

    @functools.partial(
        pl.kernel, mesh=mesh,
        out_type=jax.ShapeDtypeStruct((n_rows, PACKED), U32),
        scratch_types=[pltpu.VMEM((per_w,), jnp.int32),
                       pltpu.VMEM((SC_BUFS, SC_CHUNK, PACKED), U32),
                       pltpu.SemaphoreType.DMA((SC_BUFS,)),
                       pltpu.SemaphoreType.DMA((SC_BUFS,))])
    def gather(table_hbm, idx_hbm, out_hbm, idx_v, rows_v, gsem, wsem):
        wid = lax.axis_index("s") * nc + lax.axis_index("c")
        base = pl.multiple_of(wid * per_w, per_w)
        pltpu.sync_copy(idx_hbm.at[pl.ds(base, per_w)], idx_v)

        def read(c):
            b = c % SC_BUFS
            return [pltpu.make_async_copy(table_hbm.at[idx_v.at[pl.ds(c * SC_CHUNK, SC_CHUNK)]], rows_v.at[b], gsem.at[b])]

        def write(c):
            b = c % SC_BUFS
            return [pltpu.make_async_copy(rows_v.at[b], out_hbm.at[pl.ds(base + c * SC_CHUNK, SC_CHUNK)], wsem.at[b])]
        _sc_ring(n_ch, read, write)

    return gather(yb, dest.reshape(n_rows)).reshape(nt, TOP_K, tm, PACKED)


def _padfill_kernel(cnt_ref, pstart, xs_in, xs_ref, zeros, zsem):
    del xs_in

    def pad_copies(fn):
        for e in range(N_EXPERTS):
            cnt = cnt_ref[0, e]
            head = (-cnt) & (SUBLANES - 1)
            rest = ((-cnt) & (MOE_BLOCK - 1)) - head
            off = pstart[e] + cnt
            for k in range(SUBLANES - 1):
                @pl.when(k < head)
                def _(off=off, k=k):
                    fn(pltpu.make_async_copy(zeros.at[pl.ds(0, 1), :], xs_ref.at[pl.ds(off + k, 1), :], zsem))
            off = off + head
            for bit in PAD_BITS:
                @pl.when((rest & bit) != 0)
                def _(off=off, bit=bit):
                    fn(pltpu.make_async_copy(zeros.at[pl.ds(0, bit), :],
                                             xs_ref.at[pl.ds(pl.multiple_of(off, SUBLANES), bit), :], zsem))
                off = off + (rest & bit)

    zeros[...] = jnp.zeros_like(zeros)
    pad_copies(lambda cp: cp.start())
    pad_copies(lambda cp: cp.wait())


def _padfill(xs, pstart, cnt):
    smem = pl.BlockSpec(memory_space=pltpu.SMEM)
    hbm = pl.BlockSpec(memory_space=pl.ANY)
    return pl.pallas_call(
        _padfill_kernel,
        in_specs=[smem, smem, hbm],
        out_specs=hbm,
        out_shape=jax.ShapeDtypeStruct(xs.shape, xs.dtype),
        input_output_aliases={2: 0},
        scratch_shapes=[pltpu.VMEM((MOE_BLOCK // 2, PACKED), U32), pltpu.SemaphoreType.DMA(())],
        name="padfill",
    )(cnt, pstart, xs)


W_SLOTS = 3
W_AHEAD = W_SLOTS - 1
EXPERT_GROUP = 8
EXPERT_RUNS = (1, 2, 4)


def _expert_kernel(be_ref, nu_ref, ge_ref, x_ref, wg_hbm, wu_hbm, wd_hbm, o_ref,
                   wgf, wuf, wdf, wgb, wub, wdb, grp_ref, sems):
    step = pl.program_id(0)

    def weight_copies(e, slot):
        return (pltpu.make_async_copy(wg_hbm.at[e], wgf.at[slot], sems.at[slot, 0]),
                pltpu.make_async_copy(wu_hbm.at[e], wuf.at[slot], sems.at[slot, 1]),
                pltpu.make_async_copy(wd_hbm.at[e], wdf.at[slot], sems.at[slot, 2]))

    @pl.when(step == 0)
    def _():
        grp_ref[0] = 0
        for a in range(W_AHEAD):
            @pl.when(ge_ref[a] >= 0)
            def _(a=a):
                for cp in weight_copies(ge_ref[a], a):
                    cp.start()

    n_blocks = be_ref.shape[0]
    n_used = nu_ref[0]

    def swiglu(s, n):
        rows = pl.ds(pl.multiple_of(s * MOE_BLOCK, MOE_BLOCK), n * MOE_BLOCK)
        lo, hi = _unpack_rows(x_ref[rows, :])
        lo = lo.astype(BF16)
        hi = hi.astype(BF16)
        g = (jnp.dot(lo, wgb[:PACKED, :], preferred_element_type=F32)
             + jnp.dot(hi, wgb[PACKED:, :], preferred_element_type=F32))
        u = (jnp.dot(lo, wub[:PACKED, :], preferred_element_type=F32)
             + jnp.dot(hi, wub[PACKED:, :], preferred_element_type=F32))
        h = (g * _sigmoid(g) * u).astype(BF16)
        o_ref[rows, :] = _pack_rows(jnp.dot(h, wdb[...], preferred_element_type=F32))

    def run(s):
        j = step * EXPERT_GROUP + s
        e = be_ref[j]
        first = jnp.logical_or(j == 0, e != be_ref[jnp.maximum(j - 1, 0)])

        @pl.when(first)
        def _():
            grp = grp_ref[0]
            slot = grp % W_SLOTS
            for cp in weight_copies(e, slot):
                cp.wait()
            wgb[...] = wgf[slot].astype(BF16)
            wub[...] = wuf[slot].astype(BF16)
            wdb[...] = wdf[slot].astype(BF16)
            nxt = ge_ref[grp + W_AHEAD]

            @pl.when(nxt >= 0)
            def _():
                for cp in weight_copies(nxt, (grp + W_AHEAD) % W_SLOTS):
                    cp.start()
            grp_ref[0] = grp + 1

        def same(k):
            return (s + k < EXPERT_GROUP) & (j + k < n_used) & (be_ref[jnp.minimum(j + k, n_blocks - 1)] == e)
        take = jnp.int32(1)
        for n in EXPERT_RUNS[1:]:
            ok = same(n - 1)
            for k in range(1, n - 1):
                ok = ok & same(k)
            take = jnp.where(ok, n, take)
        for n in EXPERT_RUNS:
            @pl.when(take == n)
            def _(n=n):
                swiglu(s, n)
        return s + take

    lax.while_loop(lambda s: (s < EXPERT_GROUP) & (step * EXPERT_GROUP + s < n_used), run, jnp.int32(0))


def _experts(xs, block_expert, n_used, group_expert, w_gate, w_up, w_down):
    cap = xs.shape[0]
    n_blocks = cap // MOE_BLOCK
    assert n_blocks % EXPERT_GROUP == 0
    rows = EXPERT_GROUP * MOE_BLOCK
    last = lambda j, be, nu, ge: jnp.minimum(j, (nu[0] - 1) // EXPERT_GROUP)
    hbm = pl.BlockSpec(memory_space=pl.ANY)
    gs = pltpu.PrefetchScalarGridSpec(
        num_scalar_prefetch=3,
        grid=(n_blocks // EXPERT_GROUP,),
        in_specs=[pl.BlockSpec((rows, PACKED), lambda j, be, nu, ge: (last(j, be, nu, ge), 0)), hbm, hbm, hbm],
        out_specs=pl.BlockSpec((rows, PACKED), lambda j, be, nu, ge: (last(j, be, nu, ge), 0)),
        scratch_shapes=[pltpu.VMEM((W_SLOTS, D_MODEL, D_EXPERT), F32),
                        pltpu.VMEM((W_SLOTS, D_MODEL, D_EXPERT), F32),
                        pltpu.VMEM((W_SLOTS, D_EXPERT, D_MODEL), F32),
                        pltpu.VMEM((D_MODEL, D_EXPERT), BF16),
                        pltpu.VMEM((D_MODEL, D_EXPERT), BF16),
                        pltpu.VMEM((D_EXPERT, D_MODEL), BF16),
                        pltpu.SMEM((1,), jnp.int32),
                        pltpu.SemaphoreType.DMA((W_SLOTS, 3))],
    )
    return pl.pallas_call(
        _expert_kernel,
        grid_spec=gs,
        out_shape=jax.ShapeDtypeStruct((cap, PACKED), U32),
        compiler_params=_cparams(1),
        name="experts",
    )(block_expert, n_used, group_expert, xs, w_gate, w_up, w_down)


CB_TM = RT_TM


def _combine_kernel(x1_ref, gt_ref, y2_ref, o_ref):
    g = gt_ref[...]
    lo1, hi1 = _unpack_rows(y2_ref[0, 0])
    lo2, hi2 = _unpack_rows(y2_ref[0, 1])
    o_ref[:, :PACKED] = x1_ref[:, :PACKED] + g[:, 0:1] * lo1 + g[:, 1:2] * lo2
    o_ref[:, PACKED:] = x1_ref[:, PACKED:] + g[:, 0:1] * hi1 + g[:, 1:2] * hi2


def _combine(x1, gates, y2):
    T = x1.shape[0]
    nt = T // CB_TM
    return pl.pallas_call(
        _combine_kernel,
        grid=(nt,),
        in_specs=[pl.BlockSpec((CB_TM, D_MODEL), lambda i: (i, 0)),
                  pl.BlockSpec((CB_TM, RINFO), lambda i: (i, 0)),
                  pl.BlockSpec((1, TOP_K, CB_TM, PACKED), lambda i: (i, 0, 0, 0))],
        out_specs=pl.BlockSpec((CB_TM, D_MODEL), lambda i: (i, 0)),
        out_shape=jax.ShapeDtypeStruct((T, D_MODEL), F32),
        compiler_params=_cparams(1),
        name="combine",
    )(x1, gates, y2)


def _layer(x, rel_bias, ln1, w_in, q_norm, k_norm, attn_sink, conv_w, conv_b, lru_wa, lru_ba, lru_wi, lru_bi,
           lru_lambda, out_norm_attn, out_norm_lru, w_out, ln2, w_group, b_group, w_er, b_er, w_gate, w_up, w_down):
    B, S, D = x.shape
    T = B * S
    x2 = x.reshape(T, D)
    q, kv, xr, gr = _in_proj(x2, ln1, w_in, q_norm, k_norm)
    bias_tab = _bias_table(rel_bias)
    attn_n = _attention(q.reshape(B, S, ATTN_WIDTH), kv.reshape(B, S, 2 * KV_WIDTH), bias_tab, attn_sink,
                        out_norm_attn)
    lru_n = _rglru(xr.reshape(B, S, LRU_WIDTH), gr.reshape(B, S, LRU_WIDTH), conv_w, conv_b,
                   lru_wa, lru_ba, lru_wi, lru_bi, lru_lambda, out_norm_lru)
    x1, h2, gates, ei, cnt = _out_route(attn_n.reshape(T, ATTN_WIDTH), lru_n.reshape(T, LRU_WIDTH), x2, w_out, ln2,
                                        w_group, b_group, w_er, b_er)
    cap = _moe_cap(T)
    dest, pstart, block_expert, n_used, group_expert = _layout(ei, cnt, cap // MOE_BLOCK)
    xs = _padfill(_sc_dispatch(h2, dest, cap), pstart, cnt)
    yb = _experts(xs, block_expert, n_used, group_expert, w_gate, w_up, w_down)
    out = _combine(x1, gates, _sc_gather(yb, dest))
    return out.reshape(B, S, D)


def kernel(x, rel_bias, ln1, w_in, q_norm, k_norm, attn_sink, conv_w, conv_b, lru_wa, lru_ba, lru_wi, lru_bi,
           lru_lambda, out_norm_attn, out_norm_lru, w_out, ln2, w_group, b_group, w_expert_router, b_expert_router,
           w_gate, w_up, w_down):
    depth = ln1.shape[0]
    for l in range(depth):
        x = _layer(x, rel_bias, ln1[l], w_in[l], q_norm[l], k_norm[l], attn_sink[l], conv_w[l], conv_b[l],
                   lru_wa[l], lru_ba[l], lru_wi[l], lru_bi[l], lru_lambda[l], out_norm_attn[l], out_norm_lru[l],
                   w_out[l], ln2[l], w_group[l], b_group[l], w_expert_router[l], b_expert_router[l],
                   w_gate[l], w_up[l], w_down[l])
    return x
```

```python
import functools
import math

import jax
import jax.numpy as jnp
import numpy as np
from jax import lax
from jax.experimental import pallas as pl
from jax.experimental.pallas import tpu as pltpu
from jax.experimental.pallas import tpu_sc as plsc

D_MODEL = 1024
N_HEADS = 8
N_KV_HEADS = 2
HEAD_DIM = 64
Q_PER_KV = N_HEADS // N_KV_HEADS
ATTN_WIDTH = N_HEADS * HEAD_DIM
KV_WIDTH = N_KV_HEADS * HEAD_DIM
WINDOW = 128
BLOCK = 128
NUM_BUCKETS = 32
MAX_DISTANCE = 128
LRU_WIDTH = D_MODEL - ATTN_WIDTH
LRU_BLOCKS = 8
LRU_BLOCK_DIM = LRU_WIDTH // LRU_BLOCKS
LRU_C = 8.0
CONV_W = 4
CONV_LEFT = 2
N_GROUPS = 4
EXPERTS_PER_GROUP = 8
N_EXPERTS = N_GROUPS * EXPERTS_PER_GROUP
TOP_K = 2
D_EXPERT = 512
MOE_BLOCK = 256
EPS = 1e-6
NEG_INF = -1e30

LANES = 128
SUBLANES = 8
VMEM_LIMIT = 56 * 1024 * 1024

F32 = jnp.float32
BF16 = jnp.bfloat16


def _cparams(n_axes, vmem=VMEM_LIMIT):
    return pltpu.CompilerParams(dimension_semantics=("arbitrary",) * n_axes, vmem_limit_bytes=vmem)


def _rms(x, gain):
    return x * lax.rsqrt(jnp.mean(x * x, axis=-1, keepdims=True) + EPS) * gain


U32 = jnp.uint32
HI_MASK = 0xFFFF0000
PACKED = D_MODEL // 2


def _pack_rows(x):
    return _pack_rounded(x.astype(BF16).astype(F32))


def _pack_rounded(xb):
    h = xb.shape[1] // 2
    lo = lax.bitcast_convert_type(xb[:, :h], U32) >> 16
    hi = lax.bitcast_convert_type(xb[:, h:], U32) & jnp.uint32(HI_MASK)
    return lo | hi


def _unpack_rows(p):
    lo = lax.bitcast_convert_type(p << 16, F32)
    hi = lax.bitcast_convert_type(p & jnp.uint32(HI_MASK), F32)
    return lo, hi


IN_TM = 1024


def _head_rms(x, n_heads, gain):
    head = lax.broadcasted_iota(jnp.int32, (1, n_heads * HEAD_DIM), 1) // HEAD_DIM
    x2 = x * x
    scale = jnp.zeros_like(x)
    for h in range(n_heads):
        ms = jnp.sum(jnp.where(head == h, x2, 0.0), axis=-1, keepdims=True) * (1.0 / HEAD_DIM)
        scale = jnp.where(head == h, lax.rsqrt(ms + EPS), scale)
    return x * scale * gain


def _in_proj_kernel(x_ref, g_ref, w_ref, qg_ref, kg_ref, q_ref, kv_ref, xr_ref, gr_ref, wb_ref):
    @pl.when(pl.program_id(0) == 0)
    def _():
        wb_ref[...] = w_ref[...].astype(BF16)

    h = _rms(x_ref[...], g_ref[...]).astype(BF16)
    c_k = ATTN_WIDTH
    c_v = c_k + KV_WIDTH
    c_x = c_v + KV_WIDTH
    c_g = c_x + LRU_WIDTH
    q = jnp.dot(h, wb_ref[:, :c_k], preferred_element_type=F32)
    q_ref[...] = _head_rms(q, N_HEADS, qg_ref[...]).astype(BF16)
    k = jnp.dot(h, wb_ref[:, c_k:c_v], preferred_element_type=F32)
    kv_ref[:, :KV_WIDTH] = _head_rms(k, N_KV_HEADS, kg_ref[...]).astype(BF16)
    kv_ref[:, KV_WIDTH:] = jnp.dot(h, wb_ref[:, c_v:c_x], preferred_element_type=F32).astype(BF16)
    xr_ref[...] = jnp.dot(h, wb_ref[:, c_x:c_g], preferred_element_type=F32)
    gr_ref[...] = jnp.dot(h, wb_ref[:, c_g:], preferred_element_type=F32)


def _in_proj(x2, ln1, w_in, q_gain, k_gain):
    T = x2.shape[0]
    n_in = w_in.shape[1]
    row = lambda w: pl.BlockSpec((IN_TM, w), lambda i: (i, 0))
    qg = (jnp.tile(q_gain.astype(F32), N_HEADS) * (HEAD_DIM ** -0.5)).reshape(1, ATTN_WIDTH)
    kg = jnp.tile(k_gain.astype(F32), N_KV_HEADS).reshape(1, KV_WIDTH)
    return pl.pallas_call(
        _in_proj_kernel,
        grid=(T // IN_TM,),
        in_specs=[row(D_MODEL),
                  pl.BlockSpec((1, D_MODEL), lambda i: (0, 0)),
                  pl.BlockSpec((D_MODEL, n_in), lambda i: (0, 0)),
                  pl.BlockSpec((1, ATTN_WIDTH), lambda i: (0, 0)),
                  pl.BlockSpec((1, KV_WIDTH), lambda i: (0, 0))],
        out_specs=[row(ATTN_WIDTH), row(2 * KV_WIDTH), row(LRU_WIDTH), row(LRU_WIDTH)],
        out_shape=[jax.ShapeDtypeStruct((T, ATTN_WIDTH), BF16),
                   jax.ShapeDtypeStruct((T, 2 * KV_WIDTH), BF16),
                   jax.ShapeDtypeStruct((T, LRU_WIDTH), F32),
                   jax.ShapeDtypeStruct((T, LRU_WIDTH), F32)],
        scratch_shapes=[pltpu.VMEM((D_MODEL, n_in), BF16)],
        compiler_params=_cparams(1),
        name="in_proj",
    )(x2, ln1.reshape(1, D_MODEL), w_in, qg, kg)


def _t5_bucket(rel):
    half = NUM_BUCKETS // 2
    max_exact = half // 2
    base = jnp.where(rel > 0, half, 0)
    n = jnp.abs(rel)
    nf = jnp.maximum(n, 1).astype(jnp.float32)
    large = max_exact + (jnp.log(nf / max_exact) / math.log(MAX_DISTANCE / max_exact)
                         * (half - max_exact)).astype(jnp.int32)
    large = jnp.minimum(large, half - 1)
    return base + jnp.where(n < max_exact, n, large)


HEAD_PAIRS = Q_PER_KV // 2
EDGE_VARIANTS = 3


def _fill_bias_table(rb_ref, bucket_ref, band_ref, o_ref):
    bucket = bucket_ref[...]
    band = band_ref[...] > 0
    col = lax.broadcasted_iota(jnp.int32, bucket.shape, 1)
    valid = (band & (col >= BLOCK), band, band & (col < 2 * BLOCK))
    for h in range(N_HEADS):
        acc = jnp.zeros(bucket.shape, F32)
        for b in range(NUM_BUCKETS):
            acc = jnp.where(bucket == b, rb_ref[b, h], acc)
        kv, g = divmod(h, Q_PER_KV)
        pair, parity = divmod(g, 2)
        for var in range(EDGE_VARIANTS):
            o_ref[var, kv, parity, pair * BLOCK:(pair + 1) * BLOCK, :] = jnp.where(valid[var], acc, NEG_INF)


def _attn_kernel(sink_ref, rb_ref, q_ref, kp_ref, kc_ref, kn_ref, bucket_ref, band_ref, og_ref, o_ref, bias_ref):
    n = pl.program_id(1)

    @pl.when((pl.program_id(0) == 0) & (n == 0))
    def _():
        _fill_bias_table(rb_ref, bucket_ref, band_ref, bias_ref)

    kv_all = jnp.concatenate([kp_ref[0], kc_ref[0], kn_ref[0]], axis=0)
    for qb in range(ATTN_QB):
        variant = 1
        if qb == 0:
            variant = jnp.where(n == 0, 0, 1)
        if qb == ATTN_QB - 1:
            variant = jnp.where(n == pl.num_programs(1) - 1, 2, variant)
        out = _attn_block(q_ref[0, qb * BLOCK:(qb + 1) * BLOCK, :], kv_all[qb * BLOCK:(qb + 3) * BLOCK, :],
                          lambda kv, parity: bias_ref[variant, kv, parity], sink_ref)
        o_ref[0, qb * BLOCK:(qb + 1) * BLOCK, :] = _rms(out, og_ref[...]).astype(o_ref.dtype)


def _attn_block(q, kvw, bias, sink_ref):
    low = lax.broadcasted_iota(jnp.int32, (1, LANES), 1) < HEAD_DIM
    swap = lambda slab: pltpu.roll(slab.astype(F32), HEAD_DIM, 1).astype(BF16)
    kslab, vslab = kvw[:, :KV_WIDTH], kvw[:, KV_WIDTH:]
    kslab_sw, vslab_sw = swap(kslab), swap(vslab)
    rowi = lax.broadcasted_iota(jnp.int32, (HEAD_PAIRS * BLOCK, 1), 0)
    combos = [(kv, parity) for kv in range(N_KV_HEADS) for parity in range(2)]
    scores, vzs, sinks = [], [], []
    for kv, parity in combos:
        ks, vs = (kslab, vslab) if (kv == 0) == (parity == 0) else (kslab_sw, vslab_sw)
        keep = low if parity == 0 else jnp.logical_not(low)
        kz = jnp.where(keep, ks, jnp.zeros_like(ks))
        vzs.append(jnp.where(keep, vs, jnp.zeros_like(vs)))
        base = kv * Q_PER_KV * HEAD_DIM
        qpair = jnp.concatenate([q[:, base + j * LANES:base + (j + 1) * LANES] for j in range(HEAD_PAIRS)], axis=0)
        s = lax.dot_general(qpair, kz, (((1,), (1,)), ((), ())), preferred_element_type=F32)
        scores.append(s + bias(kv, parity))
        sink = jnp.zeros((HEAD_PAIRS * BLOCK, 1), F32)
        for j in range(HEAD_PAIRS):
            sink = jnp.where(rowi // BLOCK == j, sink_ref[kv * Q_PER_KV + 2 * j + parity], sink)
        sinks.append(sink)
    probs, inv = [], []
    for s, sink in zip(scores, sinks):
        m = jnp.maximum(jnp.max(s, axis=-1, keepdims=True), sink)
        p = jnp.exp(s - m)
        inv.append(1.0 / (jnp.sum(p, axis=-1, keepdims=True) + jnp.exp(sink - m)))
        probs.append(p.astype(BF16))
    outs = [jnp.dot(p, vz, preferred_element_type=F32) * r for p, vz, r in zip(probs, vzs, inv)]
    cols = []
    for kv in range(N_KV_HEADS):
        acc = outs[2 * kv] + outs[2 * kv + 1]
        cols += [acc[j * BLOCK:(j + 1) * BLOCK, :] for j in range(HEAD_PAIRS)]
    return jnp.concatenate(cols, axis=1)


ATTN_QB = 4


def _attention(q, kv, rel_bias, sink, out_gain):
    B, S, _ = q.shape
    nb = S // BLOCK
    assert ATTN_QB >= 2 and nb % ATTN_QB == 0, "a step's first and last query blocks must be distinct"
    ns = nb // ATTN_QB
    rows = ATTN_QB * BLOCK
    qi = jnp.arange(BLOCK, dtype=jnp.int32)
    kj = jnp.arange(3 * BLOCK, dtype=jnp.int32)
    rel = kj[None, :] - BLOCK - qi[:, None]
    bucket = _t5_bucket(rel).astype(jnp.int32)
    band = (jnp.abs(rel) <= WINDOW).astype(jnp.int32)
    kvspec = lambda f: pl.BlockSpec((1, BLOCK, 2 * KV_WIDTH), f)
    smem = pl.BlockSpec(memory_space=pltpu.SMEM)
    geom = pl.BlockSpec((BLOCK, 3 * BLOCK), lambda b, n: (0, 0))
    return pl.pallas_call(
        _attn_kernel,
        grid=(B, ns),
        in_specs=[smem, smem,
                  pl.BlockSpec((1, rows, ATTN_WIDTH), lambda b, n: (b, n, 0)),
                  kvspec(lambda b, n: (b, jnp.maximum(n * ATTN_QB - 1, 0), 0)),
                  pl.BlockSpec((1, rows, 2 * KV_WIDTH), lambda b, n: (b, n, 0)),
                  kvspec(lambda b, n: (b, jnp.minimum((n + 1) * ATTN_QB, nb - 1), 0)),
                  geom, geom,
                  pl.BlockSpec((1, ATTN_WIDTH), lambda b, n: (0, 0))],
        out_specs=pl.BlockSpec((1, rows, ATTN_WIDTH), lambda b, n: (b, n, 0)),
        out_shape=jax.ShapeDtypeStruct((B, S, ATTN_WIDTH), BF16),
        scratch_shapes=[pltpu.VMEM((EDGE_VARIANTS, N_KV_HEADS, 2, HEAD_PAIRS * BLOCK, 3 * BLOCK), F32)],
        compiler_params=_cparams(2),
        name="attention",
    )(sink.astype(F32), rel_bias.astype(F32), q, kv, kv, kv, bucket, band, out_gain.reshape(1, ATTN_WIDTH))


LRU_TC = 128
LRU_PITCH = LRU_TC + SUBLANES
LRU_SLABS = LRU_WIDTH // LANES
HALO = SUBLANES


def _softplus(x):
    return jnp.maximum(x, 0.0) + jnp.log(1.0 + jnp.exp(-jnp.abs(x)))


def _gelu_tanh(x):
    k = math.sqrt(2.0 / math.pi)
    hx = 0.5 * x
    return hx + hx * jnp.tanh(x * (k + (k * 0.044715) * (x * x)))


def _sigmoid(x):
    return 0.5 + 0.5 * jnp.tanh(0.5 * x)


def _rglru_kernel(xr_ref, xp_ref, xn_ref, gr_ref, cw_ref, cb_ref, wa_ref, wi_ref, ba_ref, bi_ref, lam_ref, og_ref,
                  o_ref, sx_ref, a_ref, u_ref, h_ref, carry_ref, hf_ref, wg_ref, bg_ref, k_ref):
    p = pl.program_id(0)
    i = pl.program_id(1)
    nc = pl.num_programs(1)
    c = i + p * (nc - 1 - 2 * i)
    B = xr_ref.shape[0]
    TC = LRU_TC

    @pl.when(i == 0)
    def _():
        carry_ref[...] = jnp.zeros_like(carry_ref)
        wg_ref[...] = jnp.zeros_like(wg_ref)
        for sel, w_ref in enumerate((wa_ref, wi_ref)):
            for h in range(LRU_BLOCKS):
                lo = h * LRU_BLOCK_DIM
                wg_ref[lo:lo + LRU_BLOCK_DIM, sel * LRU_WIDTH + lo:sel * LRU_WIDTH + lo + LRU_BLOCK_DIM] = (
                    0.5 * w_ref[0, h]).astype(BF16)
        row = pl.ds(p, 1)
        bg_ref[:, :LRU_WIDTH] = 0.5 * ba_ref[row, :]
        bg_ref[:, LRU_WIDTH:] = 0.5 * bi_ref[row, :]
        k_ref[...] = (-0.5 * LRU_C * math.log2(math.e)) * _softplus(-lam_ref[row, :])

    sx_ref[:, HALO:HALO + TC, :] = xr_ref[...]
    sx_ref[:, 0:HALO, :] = jnp.where(c > 0, xp_ref[...], 0.0)
    sx_ref[:, HALO + TC:, :] = jnp.where(c < nc - 1, xn_ref[...], 0.0)
    xc = cb_ref[...][None]
    for j in range(CONV_W):
        off = HALO + j - CONV_LEFT
        xc = xc + cw_ref[j:j + 1, :][None] * sx_ref[:, off:off + TC, :]
    xc2 = xc.reshape(B * TC, LRU_WIDTH)

    g = jnp.dot(xc2.astype(BF16), wg_ref[...], preferred_element_type=F32) + bg_ref[...]
    ta = jnp.tanh(g[:, :LRU_WIDTH])
    ig = 0.5 + 0.5 * jnp.tanh(g[:, LRU_WIDTH:])
    a = jnp.exp2((1.0 + ta) * k_ref[...])
    z = 1.0 - a * a
    u = z * lax.rsqrt(jnp.maximum(z, 1e-30)) * ig * xc2
    for b in range(B):
        for s in range(LRU_SLABS):
            a_ref[s, b * LRU_PITCH:b * LRU_PITCH + TC, :] = a[b * TC:(b + 1) * TC, s * LANES:(s + 1) * LANES]
            u_ref[s, b * LRU_PITCH:b * LRU_PITCH + TC, :] = u[b * TC:(b + 1) * TC, s * LANES:(s + 1) * LANES]

    def step(k, hs):
        t = k + p * (TC - 1 - 2 * k)
        out = []
        for s in range(LRU_SLABS):
            idx = pl.ds(t, B, stride=LRU_PITCH)
            hn = a_ref[s, idx, :] * hs[s] + u_ref[s, idx, :]
            h_ref[s, idx, :] = hn
            out.append(hn)
        return tuple(out)

    hs = lax.fori_loop(0, TC, step, tuple(carry_ref[s] for s in range(LRU_SLABS)), unroll=8)
    for s in range(LRU_SLABS):
        carry_ref[s] = hs[s]

    @pl.when(p == 0)
    def _():
        for b in range(B):
            for s in range(LRU_SLABS):
                hf_ref[c, s, b * TC:(b + 1) * TC, :] = h_ref[s, b * LRU_PITCH:b * LRU_PITCH + TC, :].astype(hf_ref.dtype)

    @pl.when(p == 1)
    def _():
        for b in range(B):
            hsum = jnp.concatenate(
                [h_ref[s, b * LRU_PITCH:b * LRU_PITCH + TC, :] + hf_ref[c, s, b * TC:(b + 1) * TC, :].astype(F32)
                 for s in range(LRU_SLABS)], axis=1)
            y = hsum * _gelu_tanh(gr_ref[b])
            o_ref[b] = _rms(y, og_ref[...]).astype(o_ref.dtype)


def _rglru(xr, gr, conv_w, conv_b, w_a, b_a, w_i, b_i, lam, out_gain):
    B, S, W = xr.shape
    nc = S // LRU_TC
    hb = LRU_TC // HALO
    chunk = lambda p, i: i + p * (nc - 1 - 2 * i)
    full2 = lambda shape: pl.BlockSpec(shape, lambda p, i: (0,) * len(shape))
    wblock = pl.BlockSpec((1, LRU_BLOCKS, LRU_BLOCK_DIM, LRU_BLOCK_DIM), lambda p, i: (p, 0, 0, 0))
    return pl.pallas_call(
        _rglru_kernel,
        grid=(2, nc),
        in_specs=[pl.BlockSpec((B, LRU_TC, W), lambda p, i: (0, chunk(p, i), 0)),
                  pl.BlockSpec((B, HALO, W), lambda p, i: (0, jnp.maximum(chunk(p, i) * hb - 1, 0), 0)),
                  pl.BlockSpec((B, HALO, W), lambda p, i: (0, jnp.minimum((chunk(p, i) + 1) * hb, S // HALO - 1), 0)),
                  pl.BlockSpec((B, LRU_TC, W), lambda p, i: (0, chunk(p, i), 0)),
                  full2((CONV_W, W)),
                  full2((1, W)),
                  wblock, wblock,
                  full2((2, W)), full2((2, W)), full2((2, W)),
                  full2((1, W))],
        out_specs=pl.BlockSpec((B, LRU_TC, W), lambda p, i: (0, nc - 1 - p * i, 0)),
        out_shape=jax.ShapeDtypeStruct((B, S, W), BF16),
        scratch_shapes=[pltpu.VMEM((B, LRU_TC + 2 * HALO, W), F32),
                        pltpu.VMEM((LRU_SLABS, B * LRU_PITCH, LANES), F32),
                        pltpu.VMEM((LRU_SLABS, B * LRU_PITCH, LANES), F32),
                        pltpu.VMEM((LRU_SLABS, B * LRU_PITCH, LANES), F32),
                        pltpu.VMEM((LRU_SLABS, B, LANES), F32),
                        pltpu.VMEM((nc, LRU_SLABS, B * LRU_TC, LANES), BF16),
                        pltpu.VMEM((W, 2 * W), BF16),
                        pltpu.VMEM((1, 2 * W), F32),
                        pltpu.VMEM((1, W), F32)],
        compiler_params=_cparams(2),
        name="rglru",
    )(xr, xr, xr, gr, conv_w.astype(F32), conv_b.reshape(1, W).astype(F32), w_a.astype(F32), w_i.astype(F32),
      b_a.astype(F32), b_i.astype(F32), lam.astype(F32), out_gain.reshape(1, W).astype(F32))


RT_TM = 1024
RT_PARTS = 4
RT_COLS = LANES
RT_ROWS = 48
RINFO = SUBLANES


def _split_bf16(x):
    hi = x.astype(BF16)
    lo = (x - hi.astype(F32)).astype(BF16)
    return hi, lo


def _route_kernel(an_ref, ln_ref, x_ref, wo_ref, g2_ref, wr_ref, br_ref,
                  x1_ref, h2_ref, gt_ref, ei_ref, cnt_ref, wob_ref, wrb_ref, tri_ref, run_ref, runc_ref):
    @pl.when(pl.program_id(0) == 0)
    def _():
        wob_ref[...] = wo_ref[...].astype(BF16)
        hi, lo = _split_bf16(wr_ref[...])
        wrb_ref[:RT_ROWS, :] = hi
        wrb_ref[RT_ROWS:, :] = lo
        r = lax.broadcasted_iota(jnp.int32, (RT_TM, RT_TM), 0)
        cidx = lax.broadcasted_iota(jnp.int32, (RT_TM, RT_TM), 1)
        tri_ref[...] = (r < cidx).astype(BF16)
        run_ref[...] = jnp.zeros_like(run_ref)
        runc_ref[...] = jnp.zeros_like(runc_ref)

    nt_dims = (((1,), (1,)), ((), ()))
    part = RT_TM // RT_PARTS
    x1s = []
    for r in range(RT_PARTS):
        rows = slice(r * part, (r + 1) * part)
        x1 = (x_ref[rows, :]
              + jnp.dot(an_ref[rows, :], wob_ref[:ATTN_WIDTH, :], preferred_element_type=F32)
              + jnp.dot(ln_ref[rows, :], wob_ref[ATTN_WIDTH:, :], preferred_element_type=F32))
        x1_ref[rows, :] = x1
        x1s.append(x1)
    splits = []
    for r, x1 in enumerate(x1s):
        h2 = _rms(x1, g2_ref[...])
        hi = h2.astype(BF16)
        hi_f = hi.astype(F32)
        h2_ref[r * part:(r + 1) * part, :] = _pack_rounded(hi_f)
        splits.append((hi, (h2 - hi_f).astype(BF16)))
    logits = []
    for hi, lo in splits:
        t1 = lax.dot_general(wrb_ref[...], hi, nt_dims, preferred_element_type=F32)
        t2 = lax.dot_general(wrb_ref[:RT_ROWS, :], lo, nt_dims, preferred_element_type=F32)
        logits.append(t1[:RT_ROWS] + t1[RT_ROWS:] + t2)
    logit = jnp.concatenate(logits, axis=1) + br_ref[...]

    sub = lax.broadcasted_iota(jnp.int32, (SUBLANES, RT_TM), 0)
    first_min = lambda hit: jnp.min(jnp.where(hit, sub, SUBLANES), axis=0, keepdims=True)
    is_g = sub < N_GROUPS
    gl = jnp.where(is_g, logit[:SUBLANES], -jnp.inf)
    gm = jnp.max(gl, axis=0, keepdims=True)
    gidx = first_min(gl == gm)
    g_p = 1.0 / jnp.sum(jnp.where(is_g, jnp.exp(logit[:SUBLANES] - gm), 0.0), axis=0, keepdims=True)
    el = logit[SUBLANES:2 * SUBLANES]
    for g in range(1, N_GROUPS):
        el = jnp.where(gidx == g, logit[(g + 1) * SUBLANES:(g + 2) * SUBLANES], el)
    m1 = jnp.max(el, axis=0, keepdims=True)
    i1 = first_min(el == m1)
    el2 = jnp.where(sub == i1, -jnp.inf, el)
    m2 = jnp.max(el2, axis=0, keepdims=True)
    i2 = first_min(el2 == m2)
    t = jnp.exp(m2 - m1)
    gate1 = g_p / (1.0 + t)
    gate2 = g_p * t / (1.0 + t)
    e1 = gidx * EXPERTS_PER_GROUP + i1
    e2 = gidx * EXPERTS_PER_GROUP + i2

    erow = lax.broadcasted_iota(jnp.int32, (N_EXPERTS, RT_TM), 0)
    oh1 = erow == e1
    oh2 = erow == e2
    oh = (oh1 | oh2).astype(F32)
    ohb = oh.astype(BF16)
    cum = jnp.dot(ohb, tri_ref[...], preferred_element_type=F32) + runc_ref[...]
    rank1 = jnp.sum(jnp.where(oh1, cum, 0.0), axis=0, keepdims=True)
    rank2 = jnp.sum(jnp.where(oh2, cum, 0.0), axis=0, keepdims=True)
    runc_ref[...] = runc_ref[...] + jnp.sum(oh, axis=1, keepdims=True)
    tile_cnt = lax.dot_general(jnp.ones((SUBLANES, RT_TM), BF16), ohb, nt_dims, preferred_element_type=F32)
    run_ref[:, :N_EXPERTS] = run_ref[:, :N_EXPERTS] + tile_cnt[0:1]
    cnt_ref[...] = run_ref[...].astype(jnp.int32)

    rows = [e1, e2, rank1.astype(jnp.int32), rank2.astype(jnp.int32)]
    ei = jnp.zeros((RINFO, RT_TM), jnp.int32)
    for k, v in enumerate(rows):
        ei = jnp.where(sub == k, v, ei)
    ei_ref[0] = ei
    gt_ref[...] = jnp.where(sub == 0, gate1, jnp.where(sub == 1, gate2, 0.0)).T


def _out_route(attn_n, lru_n, x2, w_out, ln2, w_group, b_group, w_er, b_er):
    T = x2.shape[0]
    pad_g = SUBLANES - N_GROUPS
    wr = jnp.concatenate([jnp.pad(w_group.T, ((0, pad_g), (0, 0))),
                          jnp.transpose(w_er, (0, 2, 1)).reshape(N_EXPERTS, D_MODEL)], axis=0)
    wr = jnp.pad(wr, ((0, RT_ROWS - wr.shape[0]), (0, 0))).astype(F32)
    br = jnp.concatenate([jnp.pad(b_group, (0, pad_g)), b_er.reshape(-1)])
    br = jnp.pad(br, (0, RT_ROWS - br.shape[0])).reshape(RT_ROWS, 1).astype(F32)
    row = lambda w: pl.BlockSpec((RT_TM, w), lambda i: (i, 0))
    const = lambda shape: pl.BlockSpec(shape, lambda i: (0, 0))
    return pl.pallas_call(
        _route_kernel,
        grid=(T // RT_TM,),
        in_specs=[row(ATTN_WIDTH), row(LRU_WIDTH), row(D_MODEL), const((D_MODEL, D_MODEL)), const((1, D_MODEL)),
                  const((RT_ROWS, D_MODEL)), const((RT_ROWS, 1))],
        out_specs=[row(D_MODEL), row(PACKED), row(RINFO),
                   pl.BlockSpec((1, RINFO, RT_TM), lambda i: (i, 0, 0)), const((1, RT_COLS))],
        out_shape=[jax.ShapeDtypeStruct((T, D_MODEL), F32),
                   jax.ShapeDtypeStruct((T, PACKED), U32),
                   jax.ShapeDtypeStruct((T, RINFO), F32),
                   jax.ShapeDtypeStruct((T // RT_TM, RINFO, RT_TM), jnp.int32),
                   jax.ShapeDtypeStruct((1, RT_COLS), jnp.int32)],
        scratch_shapes=[pltpu.VMEM((D_MODEL, D_MODEL), BF16),
                        pltpu.VMEM((2 * RT_ROWS, D_MODEL), BF16),
                        pltpu.VMEM((RT_TM, RT_TM), BF16),
                        pltpu.VMEM((1, RT_COLS), F32),
                        pltpu.VMEM((N_EXPERTS, 1), F32)],
        compiler_params=_cparams(1),
        name="out_route",
    )(attn_n, lru_n, x2, w_out, ln2.reshape(1, D_MODEL).astype(F32), wr, br)


def _moe_cap(T):
    A = T * TOP_K
    return ((A + MOE_BLOCK - 1) // MOE_BLOCK) * MOE_BLOCK + N_EXPERTS * MOE_BLOCK


PAD_BITS = tuple(1 << b for b in reversed(range(3, MOE_BLOCK.bit_length() - 1)))


def _layout_kernel(cnt_ref, ei_ref, dest_ref, pstart, be_ref, nu_ref, ge_ref):
    n_blocks = be_ref.shape[0]

    def lay(e, carry):
        start, blk, grp = carry
        pstart[e] = start
        nb = (cnt_ref[0, e] + MOE_BLOCK - 1) // MOE_BLOCK
        ge_ref[grp] = e

        def fill(k, c):
            be_ref[blk + k] = e
            return c
        lax.fori_loop(0, nb, fill, 0)
        return start + nb * MOE_BLOCK, blk + nb, grp + (nb > 0).astype(jnp.int32)
    _, used, groups = lax.fori_loop(0, N_EXPERTS, lay, (jnp.int32(0), jnp.int32(0), jnp.int32(0)))
    nu_ref[0] = used

    def tail(k, c):
        be_ref[k] = N_EXPERTS - 1
        return c
    lax.fori_loop(used, n_blocks, tail, 0)

    def no_group(k, c):
        ge_ref[k] = -1
        return c
    lax.fori_loop(groups, ge_ref.shape[0], no_group, 0)

    expert = ei_ref[:, 0:TOP_K, :]
    dest = ei_ref[:, TOP_K:2 * TOP_K, :]
    for e in range(N_EXPERTS):
        dest = dest + jnp.where(expert == e, pstart[e], 0)
    dest_ref[...] = dest


def _layout(ei, cnt, n_blocks):
    nt = ei.shape[0]
    smem = pl.BlockSpec(memory_space=pltpu.SMEM)
    vmem = pl.BlockSpec(memory_space=pltpu.VMEM)
    return pl.pallas_call(
        _layout_kernel,
        in_specs=[smem, vmem],
        out_specs=[vmem, smem, smem, smem, smem],
        out_shape=[jax.ShapeDtypeStruct((nt, TOP_K, RT_TM), jnp.int32),
                   jax.ShapeDtypeStruct((N_EXPERTS,), jnp.int32),
                   jax.ShapeDtypeStruct((n_blocks,), jnp.int32),
                   jax.ShapeDtypeStruct((1,), jnp.int32),
                   jax.ShapeDtypeStruct((N_EXPERTS + W_AHEAD,), jnp.int32)],
        name="layout",
    )(cnt, ei)


SC_CHUNK = 64
SC_BUFS = 3
SC_LEAD = SC_BUFS - 1


def _sc_workers():
    info = plsc.get_sparse_core_info()
    return info.num_cores, info.num_subcores


def _sc_ring(n_chunks, read, write):
    for c in range(min(SC_LEAD, n_chunks)):
        for cp in read(c):
            cp.start()
    reclaimed = set()
    for c in range(n_chunks):
        for cp in read(c):
            cp.wait()
        for cp in write(c):
            cp.start()
        nxt = c + SC_LEAD
        if nxt < n_chunks:
            if nxt - SC_BUFS >= 0:
                for cp in write(nxt - SC_BUFS):
                    cp.wait()
                reclaimed.add(nxt - SC_BUFS)
            for cp in read(nxt):
                cp.start()
    for c in range(n_chunks):
        if c not in reclaimed:
            for cp in write(c):
                cp.wait()


def _sc_dispatch(h2p, dest, cap):
    T = h2p.shape[0]
    nc, ns = _sc_workers()
    per_w = T // (nc * ns)
    n_ch = per_w // SC_CHUNK
    nt, _, tm = dest.shape
    assert nt * tm == T and tm % per_w == 0 and per_w % SC_CHUNK == 0
    idx = dest.reshape(nt, TOP_K, tm // per_w, per_w).transpose(0, 2, 1, 3).reshape(nc * ns, TOP_K * n_ch, SC_CHUNK)
    mesh = plsc.VectorSubcoreMesh(core_axis_name="c", subcore_axis_name="s")

    @functools.partial(
        pl.kernel, mesh=mesh,
        out_type=jax.ShapeDtypeStruct((cap, PACKED), U32),
        scratch_types=[pltpu.VMEM((TOP_K * n_ch, SC_CHUNK), jnp.int32),
                       pltpu.VMEM((SC_BUFS, SC_CHUNK, PACKED), U32),
                       pltpu.SemaphoreType.DMA((SC_BUFS,)),
                       pltpu.SemaphoreType.DMA((SC_BUFS,))])
    def scatter(src_hbm, idx_hbm, out_hbm, idx_v, rows_v, rsem, wsem):
        wid = lax.axis_index("s") * nc + lax.axis_index("c")
        base = pl.multiple_of(wid * per_w, per_w)
        pltpu.sync_copy(idx_hbm.at[wid], idx_v)

        def read(c):
            b = c % SC_BUFS
            return [pltpu.make_async_copy(src_hbm.at[pl.ds(base + c * SC_CHUNK, SC_CHUNK)], rows_v.at[b], rsem.at[b])]

        def write(c):
            b = c % SC_BUFS
            return [pltpu.make_async_copy(rows_v.at[b], out_hbm.at[idx_v.at[k * n_ch + c]], wsem.at[b])
                    for k in range(TOP_K)]
        _sc_ring(n_ch, read, write)

    return scatter(h2p, idx)


def _sc_gather(yb, dest):
    nt, _, tm = dest.shape
    nc, ns = _sc_workers()
    n_rows = nt * TOP_K * tm
    per_w = n_rows // (nc * ns)
    n_ch = per_w // SC_CHUNK
    assert per_w * nc * ns == n_rows and per_w % SC_CHUNK == 0
    mesh = plsc.VectorSubcoreMesh(core_axis_name="c", subcore_axis_name="s")

    @functools.partial(
        pl.kernel, mesh=mesh,
        out_type=jax.ShapeDtypeStruct((n_rows, PACKED), U32),
        scratch_types=[pltpu.VMEM((per_w,), jnp.int32),
                       pltpu.VMEM((SC_BUFS, SC_CHUNK, PACKED), U32),
                       pltpu.SemaphoreType.DMA((SC_BUFS,)),
                       pltpu.SemaphoreType.DMA((SC_BUFS,))])
    def gather(table_hbm, idx_hbm, out_hbm, idx_v, rows_v, gsem, wsem):
        wid = lax.axis_index("s") * nc + lax.axis_index("c")
        base = pl.multiple_of(wid * per_w, per_w)
        pltpu.sync_copy(idx_hbm.at[pl.ds(base, per_w)], idx_v)

        def read(c):
            b = c % SC_BUFS
            return [pltpu.make_async_copy(table_hbm.at[idx_v.at[pl.ds(c * SC_CHUNK, SC_CHUNK)]], rows_v.at[b], gsem.at[b])]

        def write(c):
            b = c % SC_BUFS
            return [pltpu.make_async_copy(rows_v.at[b], out_hbm.at[pl.ds(base + c * SC_CHUNK, SC_CHUNK)], wsem.at[b])]
        _sc_ring(n_ch, read, write)

    return gather(yb, dest.reshape(n_rows)).reshape(nt, TOP_K, tm, PACKED)


def _padfill_kernel(cnt_ref, pstart, xs_in, xs_ref, zeros, zsem):
    del xs_in

    def pad_copies(fn):
        for e in range(N_EXPERTS):
            cnt = cnt_ref[0, e]
            head = (-cnt) & (SUBLANES - 1)
            rest = ((-cnt) & (MOE_BLOCK - 1)) - head
            off = pstart[e] + cnt
            for k in range(SUBLANES - 1):
                @pl.when(k < head)
                def _(off=off, k=k):
                    fn(pltpu.make_async_copy(zeros.at[pl.ds(0, 1), :], xs_ref.at[pl.ds(off + k, 1), :], zsem))
            off = off + head
            for bit in PAD_BITS:
                @pl.when((rest & bit) != 0)
                def _(off=off, bit=bit):
                    fn(pltpu.make_async_copy(zeros.at[pl.ds(0, bit), :],
                                             xs_ref.at[pl.ds(pl.multiple_of(off, SUBLANES), bit), :], zsem))
                off = off + (rest & bit)

    zeros[...] = jnp.zeros_like(zeros)
    pad_copies(lambda cp: cp.start())
    pad_copies(lambda cp: cp.wait())


def _padfill(xs, pstart, cnt):
    smem = pl.BlockSpec(memory_space=pltpu.SMEM)
    hbm = pl.BlockSpec(memory_space=pl.ANY)
    return pl.pallas_call(
        _padfill_kernel,
        in_specs=[smem, smem, hbm],
        out_specs=hbm,
        out_shape=jax.ShapeDtypeStruct(xs.shape, xs.dtype),
        input_output_aliases={2: 0},
        scratch_shapes=[pltpu.VMEM((MOE_BLOCK // 2, PACKED), U32), pltpu.SemaphoreType.DMA(())],
        name="padfill",
    )(cnt, pstart, xs)


W_SLOTS = 3
W_AHEAD = W_SLOTS - 1
EXPERT_GROUP = 8
EXPERT_RUNS = (1, 2, 4)


def _expert_kernel(be_ref, nu_ref, ge_ref, x_ref, wg_hbm, wu_hbm, wd_hbm, o_ref,
                   wgf, wuf, wdf, wgb, wub, wdb, grp_ref, sems):
    step = pl.program_id(0)

    def weight_copies(e, slot):
        return (pltpu.make_async_copy(wg_hbm.at[e], wgf.at[slot], sems.at[slot, 0]),
                pltpu.make_async_copy(wu_hbm.at[e], wuf.at[slot], sems.at[slot, 1]),
                pltpu.make_async_copy(wd_hbm.at[e], wdf.at[slot], sems.at[slot, 2]))

    @pl.when(step == 0)
    def _():
        grp_ref[0] = 0
        for a in range(W_AHEAD):
            @pl.when(ge_ref[a] >= 0)
            def _(a=a):
                for cp in weight_copies(ge_ref[a], a):
                    cp.start()

    n_blocks = be_ref.shape[0]
    n_used = nu_ref[0]

    def swiglu(s, n):
        rows = pl.ds(pl.multiple_of(s * MOE_BLOCK, MOE_BLOCK), n * MOE_BLOCK)
        lo, hi = _unpack_rows(x_ref[rows, :])
        lo = lo.astype(BF16)
        hi = hi.astype(BF16)
        g = (jnp.dot(lo, wgb[:PACKED, :], preferred_element_type=F32)
             + jnp.dot(hi, wgb[PACKED:, :], preferred_element_type=F32))
        u = (jnp.dot(lo, wub[:PACKED, :], preferred_element_type=F32)
             + jnp.dot(hi, wub[PACKED:, :], preferred_element_type=F32))
        h = (g * _sigmoid(g) * u).astype(BF16)
        o_ref[rows, :] = _pack_rows(jnp.dot(h, wdb[...], preferred_element_type=F32))

    def run(s):
        j = step * EXPERT_GROUP + s
        e = be_ref[j]
        first = jnp.logical_or(j == 0, e != be_ref[jnp.maximum(j - 1, 0)])

        @pl.when(first)
        def _():
            grp = grp_ref[0]
            slot = grp % W_SLOTS
            for cp in weight_copies(e, slot):
                cp.wait()
            wgb[...] = wgf[slot].astype(BF16)
            wub[...] = wuf[slot].astype(BF16)
            wdb[...] = wdf[slot].astype(BF16)
            nxt = ge_ref[grp + W_AHEAD]

            @pl.when(nxt >= 0)
            def _():
                for cp in weight_copies(nxt, (grp + W_AHEAD) % W_SLOTS):
                    cp.start()
            grp_ref[0] = grp + 1

        def same(k):
            return (s + k < EXPERT_GROUP) & (j + k < n_used) & (be_ref[jnp.minimum(j + k, n_blocks - 1)] == e)
        take = jnp.int32(1)
        for n in EXPERT_RUNS[1:]:
            ok = same(n - 1)
            for k in range(1, n - 1):
                ok = ok & same(k)
            take = jnp.where(ok, n, take)
        for n in EXPERT_RUNS:
            @pl.when(take == n)
            def _(n=n):
                swiglu(s, n)
        return s + take

    lax.while_loop(lambda s: (s < EXPERT_GROUP) & (step * EXPERT_GROUP + s < n_used), run, jnp.int32(0))


def _experts(xs, block_expert, n_used, group_expert, w_gate, w_up, w_down):
    cap = xs.shape[0]
    n_blocks = cap // MOE_BLOCK
    assert n_blocks % EXPERT_GROUP == 0
    rows = EXPERT_GROUP * MOE_BLOCK
    last = lambda j, be, nu, ge: jnp.minimum(j, (nu[0] - 1) // EXPERT_GROUP)
    hbm = pl.BlockSpec(memory_space=pl.ANY)
    gs = pltpu.PrefetchScalarGridSpec(
        num_scalar_prefetch=3,
        grid=(n_blocks // EXPERT_GROUP,),
        in_specs=[pl.BlockSpec((rows, PACKED), lambda j, be, nu, ge: (last(j, be, nu, ge), 0)), hbm, hbm, hbm],
        out_specs=pl.BlockSpec((rows, PACKED), lambda j, be, nu, ge: (last(j, be, nu, ge), 0)),
        scratch_shapes=[pltpu.VMEM((W_SLOTS, D_MODEL, D_EXPERT), F32),
                        pltpu.VMEM((W_SLOTS, D_MODEL, D_EXPERT), F32),
                        pltpu.VMEM((W_SLOTS, D_EXPERT, D_MODEL), F32),
                        pltpu.VMEM((D_MODEL, D_EXPERT), BF16),
                        pltpu.VMEM((D_MODEL, D_EXPERT), BF16),
                        pltpu.VMEM((D_EXPERT, D_MODEL), BF16),
                        pltpu.SMEM((1,), jnp.int32),
                        pltpu.SemaphoreType.DMA((W_SLOTS, 3))],
    )
    return pl.pallas_call(
        _expert_kernel,
        grid_spec=gs,
        out_shape=jax.ShapeDtypeStruct((cap, PACKED), U32),
        compiler_params=_cparams(1),
        name="experts",
    )(block_expert, n_used, group_expert, xs, w_gate, w_up, w_down)


CB_TM = RT_TM


def _combine_kernel(x1_ref, gt_ref, y2_ref, o_ref):
    g = gt_ref[...]
    lo1, hi1 = _unpack_rows(y2_ref[0, 0])
    lo2, hi2 = _unpack_rows(y2_ref[0, 1])
    o_ref[:, :PACKED] = x1_ref[:, :PACKED] + g[:, 0:1] * lo1 + g[:, 1:2] * lo2
    o_ref[:, PACKED:] = x1_ref[:, PACKED:] + g[:, 0:1] * hi1 + g[:, 1:2] * hi2


def _combine(x1, gates, y2):
    T = x1.shape[0]
    nt = T // CB_TM
    return pl.pallas_call(
        _combine_kernel,
        grid=(nt,),
        in_specs=[pl.BlockSpec((CB_TM, D_MODEL), lambda i: (i, 0)),
                  pl.BlockSpec((CB_TM, RINFO), lambda i: (i, 0)),
                  pl.BlockSpec((1, TOP_K, CB_TM, PACKED), lambda i: (i, 0, 0, 0))],
        out_specs=pl.BlockSpec((CB_TM, D_MODEL), lambda i: (i, 0)),
        out_shape=jax.ShapeDtypeStruct((T, D_MODEL), F32),
        compiler_params=_cparams(1),
        name="combine",
    )(x1, gates, y2)


def _layer(x, rel_bias, ln1, w_in, q_norm, k_norm, attn_sink, conv_w, conv_b, lru_wa, lru_ba, lru_wi, lru_bi,
           lru_lambda, out_norm_attn, out_norm_lru, w_out, ln2, w_group, b_group, w_er, b_er, w_gate, w_up, w_down):
    B, S, D = x.shape
    T = B * S
    x2 = x.reshape(T, D)
    q, kv, xr, gr = _in_proj(x2, ln1, w_in, q_norm, k_norm)
    attn_n = _attention(q.reshape(B, S, ATTN_WIDTH), kv.reshape(B, S, 2 * KV_WIDTH), rel_bias, attn_sink,
                        out_norm_attn)
    lru_n = _rglru(xr.reshape(B, S, LRU_WIDTH), gr.reshape(B, S, LRU_WIDTH), conv_w, conv_b,
                   lru_wa, lru_ba, lru_wi, lru_bi, lru_lambda, out_norm_lru)
    x1, h2, gates, ei, cnt = _out_route(attn_n.reshape(T, ATTN_WIDTH), lru_n.reshape(T, LRU_WIDTH), x2, w_out, ln2,
                                        w_group, b_group, w_er, b_er)
    cap = _moe_cap(T)
    dest, pstart, block_expert, n_used, group_expert = _layout(ei, cnt, cap // MOE_BLOCK)
    xs = _padfill(_sc_dispatch(h2, dest, cap), pstart, cnt)
    yb = _experts(xs, block_expert, n_used, group_expert, w_gate, w_up, w_down)
    out = _combine(x1, gates, _sc_gather(yb, dest))
    return out.reshape(B, S, D)


def kernel(x, rel_bias, ln1, w_in, q_norm, k_norm, attn_sink, conv_w, conv_b, lru_wa, lru_ba, lru_wi, lru_bi,
           lru_lambda, out_norm_attn, out_norm_lru, w_out, ln2, w_group, b_group, w_expert_router, b_expert_router,
           w_gate, w_up, w_down):
    depth = ln1.shape[0]
    for l in range(depth):
        x = _layer(x, rel_bias, ln1[l], w_in[l], q_norm[l], k_norm[l], attn_sink[l], conv_w[l], conv_b[l],
                   lru_wa[l], lru_ba[l], lru_wi[l], lru_bi[l], lru_lambda[l], out_norm_attn[l], out_norm_lru[l],
                   w_out[l], ln2[l], w_group[l], b_group[l], w_expert_router[l], b_expert_router[l],
                   w_gate[l], w_up[l], w_down[l])
    return x
```

```python
import functools
import math

import jax
import jax.numpy as jnp
import numpy as np
from jax import lax
from jax.experimental import pallas as pl
from jax.experimental.pallas import tpu as pltpu
from jax.experimental.pallas import tpu_sc as plsc

D_MODEL = 1024
N_HEADS = 8
N_KV_HEADS = 2
HEAD_DIM = 64
Q_PER_KV = N_HEADS // N_KV_HEADS
ATTN_WIDTH = N_HEADS * HEAD_DIM
KV_WIDTH = N_KV_HEADS * HEAD_DIM
WINDOW = 128
BLOCK = 128
NUM_BUCKETS = 32
MAX_DISTANCE = 128
LRU_WIDTH = D_MODEL - ATTN_WIDTH
LRU_BLOCKS = 8
LRU_BLOCK_DIM = LRU_WIDTH // LRU_BLOCKS
LRU_C = 8.0
CONV_W = 4
CONV_LEFT = 2
N_GROUPS = 4
EXPERTS_PER_GROUP = 8
N_EXPERTS = N_GROUPS * EXPERTS_PER_GROUP
TOP_K = 2
D_EXPERT = 512
MOE_BLOCK = 256
EPS = 1e-6
NEG_INF = -1e30

LANES = 128
SUBLANES = 8
VMEM_LIMIT = 56 * 1024 * 1024

F32 = jnp.float32
BF16 = jnp.bfloat16
LOG2E = math.log2(math.e)


def _cparams(n_axes, vmem=VMEM_LIMIT):
    return pltpu.CompilerParams(dimension_semantics=("arbitrary",) * n_axes, vmem_limit_bytes=vmem)


def _rms(x, gain):
    return x * lax.rsqrt(jnp.mean(x * x, axis=-1, keepdims=True) + EPS) * gain


U32 = jnp.uint32
HI_MASK = 0xFFFF0000
PACKED = D_MODEL // 2


def _pack_rows(x):
    return _pack_rounded(x.astype(BF16).astype(F32))


def _pack_rounded(xb):
    h = xb.shape[1] // 2
    lo = lax.bitcast_convert_type(xb[:, :h], U32) >> 16
    hi = lax.bitcast_convert_type(xb[:, h:], U32) & jnp.uint32(HI_MASK)
    return lo | hi


def _unpack_rows(p):
    lo = lax.bitcast_convert_type(p << 16, F32)
    hi = lax.bitcast_convert_type(p & jnp.uint32(HI_MASK), F32)
    return lo, hi


IN_TM = 1024


def _head_rms(x, n_heads, gain):
    head = lax.broadcasted_iota(jnp.int32, (1, n_heads * HEAD_DIM), 1) // HEAD_DIM
    x2 = x * x
    scale = jnp.zeros_like(x)
    for h in range(n_heads):
        ms = jnp.sum(jnp.where(head == h, x2, 0.0), axis=-1, keepdims=True) * (1.0 / HEAD_DIM)
        scale = jnp.where(head == h, lax.rsqrt(ms + EPS), scale)
    return x * scale * gain


def _in_proj_kernel(x_ref, g_ref, w_ref, qg_ref, kg_ref, q_ref, kv_ref, xr_ref, gr_ref, wb_ref):
    @pl.when(pl.program_id(0) == 0)
    def _():
        wb_ref[...] = w_ref[...].astype(BF16)

    h = _rms(x_ref[...], g_ref[...]).astype(BF16)
    c_k = ATTN_WIDTH
    c_v = c_k + KV_WIDTH
    c_x = c_v + KV_WIDTH
    c_g = c_x + LRU_WIDTH
    q = jnp.dot(h, wb_ref[:, :c_k], preferred_element_type=F32)
    q_ref[...] = _head_rms(q, N_HEADS, qg_ref[...]).astype(BF16)
    k = jnp.dot(h, wb_ref[:, c_k:c_v], preferred_element_type=F32)
    kv_ref[:, :KV_WIDTH] = _head_rms(k, N_KV_HEADS, kg_ref[...]).astype(BF16)
    kv_ref[:, KV_WIDTH:] = jnp.dot(h, wb_ref[:, c_v:c_x], preferred_element_type=F32).astype(BF16)
    xr_ref[...] = jnp.dot(h, wb_ref[:, c_x:c_g], preferred_element_type=F32)
    gr_ref[...] = jnp.dot(h, wb_ref[:, c_g:], preferred_element_type=F32)


def _in_proj(x2, ln1, w_in, q_gain, k_gain):
    T = x2.shape[0]
    n_in = w_in.shape[1]
    row = lambda w: pl.BlockSpec((IN_TM, w), lambda i: (i, 0))
    qg = (jnp.tile(q_gain.astype(F32), N_HEADS) * (HEAD_DIM ** -0.5 * LOG2E)).reshape(1, ATTN_WIDTH)
    kg = jnp.tile(k_gain.astype(F32), N_KV_HEADS).reshape(1, KV_WIDTH)
    return pl.pallas_call(
        _in_proj_kernel,
        grid=(T // IN_TM,),
        in_specs=[row(D_MODEL),
                  pl.BlockSpec((1, D_MODEL), lambda i: (0, 0)),
                  pl.BlockSpec((D_MODEL, n_in), lambda i: (0, 0)),
                  pl.BlockSpec((1, ATTN_WIDTH), lambda i: (0, 0)),
                  pl.BlockSpec((1, KV_WIDTH), lambda i: (0, 0))],
        out_specs=[row(ATTN_WIDTH), row(2 * KV_WIDTH), row(LRU_WIDTH), row(LRU_WIDTH)],
        out_shape=[jax.ShapeDtypeStruct((T, ATTN_WIDTH), BF16),
                   jax.ShapeDtypeStruct((T, 2 * KV_WIDTH), BF16),
                   jax.ShapeDtypeStruct((T, LRU_WIDTH), F32),
                   jax.ShapeDtypeStruct((T, LRU_WIDTH), F32)],
        scratch_shapes=[pltpu.VMEM((D_MODEL, n_in), BF16)],
        compiler_params=_cparams(1),
        name="in_proj",
    )(x2, ln1.reshape(1, D_MODEL), w_in, qg, kg)


def _t5_bucket(rel):
    half = NUM_BUCKETS // 2
    max_exact = half // 2
    base = jnp.where(rel > 0, half, 0)
    n = jnp.abs(rel)
    nf = jnp.maximum(n, 1).astype(jnp.float32)
    large = max_exact + (jnp.log(nf / max_exact) / math.log(MAX_DISTANCE / max_exact)
                         * (half - max_exact)).astype(jnp.int32)
    large = jnp.minimum(large, half - 1)
    return base + jnp.where(n < max_exact, n, large)


HEAD_PAIRS = Q_PER_KV // 2
EDGE_VARIANTS = 3


def _fill_bias_table(rb_ref, bucket_ref, band_ref, o_ref):
    bucket = bucket_ref[...]
    band = band_ref[...] > 0
    col = lax.broadcasted_iota(jnp.int32, bucket.shape, 1)
    valid = (band & (col >= BLOCK), band, band & (col < 2 * BLOCK))
    for h in range(N_HEADS):
        acc = jnp.zeros(bucket.shape, F32)
        for b in range(NUM_BUCKETS):
            acc = jnp.where(bucket == b, rb_ref[b, h], acc)
        kv, g = divmod(h, Q_PER_KV)
        pair, parity = divmod(g, 2)
        for var in range(EDGE_VARIANTS):
            o_ref[var, kv, parity, pair * BLOCK:(pair + 1) * BLOCK, :] = jnp.where(valid[var], acc * LOG2E, NEG_INF)


def _attn_kernel(sink_ref, rb_ref, q_ref, kp_ref, kc_ref, kn_ref, bucket_ref, band_ref, og_ref, o_ref, bias_ref):
    n = pl.program_id(1)

    @pl.when((pl.program_id(0) == 0) & (n == 0))
    def _():
        _fill_bias_table(rb_ref, bucket_ref, band_ref, bias_ref)

    kv_all = jnp.concatenate([kp_ref[0], kc_ref[0], kn_ref[0]], axis=0)
    for qb in range(ATTN_QB):
        variant = 1
        if qb == 0:
            variant = jnp.where(n == 0, 0, 1)
        if qb == ATTN_QB - 1:
            variant = jnp.where(n == pl.num_programs(1) - 1, 2, variant)
        out = _attn_block(q_ref[0, qb * BLOCK:(qb + 1) * BLOCK, :], kv_all[qb * BLOCK:(qb + 3) * BLOCK, :],
                          lambda kv, parity: bias_ref[variant, kv, parity], sink_ref)
        o_ref[0, qb * BLOCK:(qb + 1) * BLOCK, :] = _rms(out, og_ref[...]).astype(o_ref.dtype)


def _attn_block(q, kvw, bias, sink_ref):
    low = lax.broadcasted_iota(jnp.int32, (1, LANES), 1) < HEAD_DIM
    swap = lambda slab: pltpu.roll(slab.astype(F32), HEAD_DIM, 1).astype(BF16)
    kslab, vslab = kvw[:, :KV_WIDTH], kvw[:, KV_WIDTH:]
    kslab_sw, vslab_sw = swap(kslab), swap(vslab)
    rowi = lax.broadcasted_iota(jnp.int32, (HEAD_PAIRS * BLOCK, 1), 0)
    combos = [(kv, parity) for kv in range(N_KV_HEADS) for parity in range(2)]
    scores, vzs, sinks = [], [], []
    for kv, parity in combos:
        ks, vs = (kslab, vslab) if (kv == 0) == (parity == 0) else (kslab_sw, vslab_sw)
        keep = low if parity == 0 else jnp.logical_not(low)
        kz = jnp.where(keep, ks, jnp.zeros_like(ks))
        vzs.append(jnp.where(keep, vs, jnp.zeros_like(vs)))
        base = kv * Q_PER_KV * HEAD_DIM
        qpair = jnp.concatenate([q[:, base + j * LANES:base + (j + 1) * LANES] for j in range(HEAD_PAIRS)], axis=0)
        s = lax.dot_general(qpair, kz, (((1,), (1,)), ((), ())), preferred_element_type=F32)
        scores.append(s + bias(kv, parity))
        sink = jnp.zeros((HEAD_PAIRS * BLOCK, 1), F32)
        for j in range(HEAD_PAIRS):
            sink = jnp.where(rowi // BLOCK == j, sink_ref[kv * Q_PER_KV + 2 * j + parity], sink)
        sinks.append(sink)
    probs, inv = [], []
    for s, sink in zip(scores, sinks):
        m = jnp.maximum(jnp.max(s, axis=-1, keepdims=True), sink)
        p = jnp.exp2(s - m)
        inv.append(1.0 / (jnp.sum(p, axis=-1, keepdims=True) + jnp.exp2(sink - m)))
        probs.append(p.astype(BF16))
    outs = [jnp.dot(p, vz, preferred_element_type=F32) * r for p, vz, r in zip(probs, vzs, inv)]
    cols = []
    for kv in range(N_KV_HEADS):
        acc = outs[2 * kv] + outs[2 * kv + 1]
        cols += [acc[j * BLOCK:(j + 1) * BLOCK, :] for j in range(HEAD_PAIRS)]
    return jnp.concatenate(cols, axis=1)


ATTN_QB = 4


def _attention(q, kv, rel_bias, sink, out_gain):
    B, S, _ = q.shape
    nb = S // BLOCK
    assert ATTN_QB >= 2 and nb % ATTN_QB == 0, "a step's first and last query blocks must be distinct"
    ns = nb // ATTN_QB
    rows = ATTN_QB * BLOCK
    qi = jnp.arange(BLOCK, dtype=jnp.int32)
    kj = jnp.arange(3 * BLOCK, dtype=jnp.int32)
    rel = kj[None, :] - BLOCK - qi[:, None]
    bucket = _t5_bucket(rel).astype(jnp.int32)
    band = (jnp.abs(rel) <= WINDOW).astype(jnp.int32)
    kvspec = lambda f: pl.BlockSpec((1, BLOCK, 2 * KV_WIDTH), f)
    smem = pl.BlockSpec(memory_space=pltpu.SMEM)
    geom = pl.BlockSpec((BLOCK, 3 * BLOCK), lambda b, n: (0, 0))
    return pl.pallas_call(
        _attn_kernel,
        grid=(B, ns),
        in_specs=[smem, smem,
                  pl.BlockSpec((1, rows, ATTN_WIDTH), lambda b, n: (b, n, 0)),
                  kvspec(lambda b, n: (b, jnp.maximum(n * ATTN_QB - 1, 0), 0)),
                  pl.BlockSpec((1, rows, 2 * KV_WIDTH), lambda b, n: (b, n, 0)),
                  kvspec(lambda b, n: (b, jnp.minimum((n + 1) * ATTN_QB, nb - 1), 0)),
                  geom, geom,
                  pl.BlockSpec((1, ATTN_WIDTH), lambda b, n: (0, 0))],
        out_specs=pl.BlockSpec((1, rows, ATTN_WIDTH), lambda b, n: (b, n, 0)),
        out_shape=jax.ShapeDtypeStruct((B, S, ATTN_WIDTH), BF16),
        scratch_shapes=[pltpu.VMEM((EDGE_VARIANTS, N_KV_HEADS, 2, HEAD_PAIRS * BLOCK, 3 * BLOCK), F32)],
        compiler_params=_cparams(2),
        name="attention",
    )(sink.astype(F32) * LOG2E, rel_bias.astype(F32), q, kv, kv, kv, bucket, band, out_gain.reshape(1, ATTN_WIDTH))


LRU_TC = 128
LRU_PITCH = LRU_TC + SUBLANES
LRU_SLABS = LRU_WIDTH // LANES
HALO = SUBLANES


def _softplus(x):
    return jnp.maximum(x, 0.0) + jnp.log(1.0 + jnp.exp(-jnp.abs(x)))


def _gelu_tanh(x):
    k = math.sqrt(2.0 / math.pi)
    hx = 0.5 * x
    return hx + hx * jnp.tanh(x * (k + (k * 0.044715) * (x * x)))


def _sigmoid(x):
    return 0.5 + 0.5 * jnp.tanh(0.5 * x)


def _rglru_kernel(xr_ref, xp_ref, xn_ref, gr_ref, cw_ref, cb_ref, wa_ref, wi_ref, ba_ref, bi_ref, lam_ref, og_ref,
                  o_ref, sx_ref, a_ref, u_ref, h_ref, carry_ref, hf_ref, wg_ref, bg_ref, k_ref):
    p = pl.program_id(0)
    i = pl.program_id(1)
    nc = pl.num_programs(1)
    c = i + p * (nc - 1 - 2 * i)
    B = xr_ref.shape[0]
    TC = LRU_TC

    @pl.when(i == 0)
    def _():
        carry_ref[...] = jnp.zeros_like(carry_ref)
        wg_ref[...] = jnp.zeros_like(wg_ref)
        for sel, w_ref in enumerate((wa_ref, wi_ref)):
            for h in range(LRU_BLOCKS):
                lo = h * LRU_BLOCK_DIM
                wg_ref[lo:lo + LRU_BLOCK_DIM, sel * LRU_WIDTH + lo:sel * LRU_WIDTH + lo + LRU_BLOCK_DIM] = (
                    0.5 * w_ref[0, h]).astype(BF16)
        row = pl.ds(p, 1)
        bg_ref[:, :LRU_WIDTH] = 0.5 * ba_ref[row, :]
        bg_ref[:, LRU_WIDTH:] = 0.5 * bi_ref[row, :]
        k_ref[...] = (-0.5 * LRU_C * math.log2(math.e)) * _softplus(-lam_ref[row, :])

    sx_ref[:, HALO:HALO + TC, :] = xr_ref[...]
    sx_ref[:, 0:HALO, :] = jnp.where(c > 0, xp_ref[...], 0.0)
    sx_ref[:, HALO + TC:, :] = jnp.where(c < nc - 1, xn_ref[...], 0.0)
    xc = cb_ref[...][None]
    for j in range(CONV_W):
        off = HALO + j - CONV_LEFT
        xc = xc + cw_ref[j:j + 1, :][None] * sx_ref[:, off:off + TC, :]
    xc2 = xc.reshape(B * TC, LRU_WIDTH)

    g = jnp.dot(xc2.astype(BF16), wg_ref[...], preferred_element_type=F32) + bg_ref[...]
    ta = jnp.tanh(g[:, :LRU_WIDTH])
    ig = 0.5 + 0.5 * jnp.tanh(g[:, LRU_WIDTH:])
    a = jnp.exp2((1.0 + ta) * k_ref[...])
    z = 1.0 - a * a
    u = z * lax.rsqrt(jnp.maximum(z, 1e-30)) * ig * xc2
    for b in range(B):
        for s in range(LRU_SLABS):
            a_ref[s, b * LRU_PITCH:b * LRU_PITCH + TC, :] = a[b * TC:(b + 1) * TC, s * LANES:(s + 1) * LANES]
            u_ref[s, b * LRU_PITCH:b * LRU_PITCH + TC, :] = u[b * TC:(b + 1) * TC, s * LANES:(s + 1) * LANES]

    def step(k, hs):
        t = k + p * (TC - 1 - 2 * k)
        out = []
        for s in range(LRU_SLABS):
            idx = pl.ds(t, B, stride=LRU_PITCH)
            hn = a_ref[s, idx, :] * hs[s] + u_ref[s, idx, :]
            h_ref[s, idx, :] = hn
            out.append(hn)
        return tuple(out)

    hs = lax.fori_loop(0, TC, step, tuple(carry_ref[s] for s in range(LRU_SLABS)), unroll=8)
    for s in range(LRU_SLABS):
        carry_ref[s] = hs[s]

    @pl.when(p == 0)
    def _():
        for b in range(B):
            for s in range(LRU_SLABS):
                hf_ref[c, s, b * TC:(b + 1) * TC, :] = h_ref[s, b * LRU_PITCH:b * LRU_PITCH + TC, :].astype(hf_ref.dtype)

    @pl.when(p == 1)
    def _():
        for b in range(B):
            hsum = jnp.concatenate(
                [h_ref[s, b * LRU_PITCH:b * LRU_PITCH + TC, :] + hf_ref[c, s, b * TC:(b + 1) * TC, :].astype(F32)
                 for s in range(LRU_SLABS)], axis=1)
            y = hsum * _gelu_tanh(gr_ref[b])
            o_ref[b] = _rms(y, og_ref[...]).astype(o_ref.dtype)


def _rglru(xr, gr, conv_w, conv_b, w_a, b_a, w_i, b_i, lam, out_gain):
    B, S, W = xr.shape
    nc = S // LRU_TC
    hb = LRU_TC // HALO
    chunk = lambda p, i: i + p * (nc - 1 - 2 * i)
    full2 = lambda shape: pl.BlockSpec(shape, lambda p, i: (0,) * len(shape))
    wblock = pl.BlockSpec((1, LRU_BLOCKS, LRU_BLOCK_DIM, LRU_BLOCK_DIM), lambda p, i: (p, 0, 0, 0))
    return pl.pallas_call(
        _rglru_kernel,
        grid=(2, nc),
        in_specs=[pl.BlockSpec((B, LRU_TC, W), lambda p, i: (0, chunk(p, i), 0)),
                  pl.BlockSpec((B, HALO, W), lambda p, i: (0, jnp.maximum(chunk(p, i) * hb - 1, 0), 0)),
                  pl.BlockSpec((B, HALO, W), lambda p, i: (0, jnp.minimum((chunk(p, i) + 1) * hb, S // HALO - 1), 0)),
                  pl.BlockSpec((B, LRU_TC, W), lambda p, i: (0, chunk(p, i), 0)),
                  full2((CONV_W, W)),
                  full2((1, W)),
                  wblock, wblock,
                  full2((2, W)), full2((2, W)), full2((2, W)),
                  full2((1, W))],
        out_specs=pl.BlockSpec((B, LRU_TC, W), lambda p, i: (0, nc - 1 - p * i, 0)),
        out_shape=jax.ShapeDtypeStruct((B, S, W), BF16),
        scratch_shapes=[pltpu.VMEM((B, LRU_TC + 2 * HALO, W), F32),
                        pltpu.VMEM((LRU_SLABS, B * LRU_PITCH, LANES), F32),
                        pltpu.VMEM((LRU_SLABS, B * LRU_PITCH, LANES), F32),
                        pltpu.VMEM((LRU_SLABS, B * LRU_PITCH, LANES), F32),
                        pltpu.VMEM((LRU_SLABS, B, LANES), F32),
                        pltpu.VMEM((nc, LRU_SLABS, B * LRU_TC, LANES), BF16),
                        pltpu.VMEM((W, 2 * W), BF16),
                        pltpu.VMEM((1, 2 * W), F32),
                        pltpu.VMEM((1, W), F32)],
        compiler_params=_cparams(2),
        name="rglru",
    )(xr, xr, xr, gr, conv_w.astype(F32), conv_b.reshape(1, W).astype(F32), w_a.astype(F32), w_i.astype(F32),
      b_a.astype(F32), b_i.astype(F32), lam.astype(F32), out_gain.reshape(1, W).astype(F32))


RT_TM = 1024
RT_PARTS = 4
RT_COLS = LANES
RT_ROWS = 48
RINFO = SUBLANES


def _split_bf16(x):
    hi = x.astype(BF16)
    lo = (x - hi.astype(F32)).astype(BF16)
    return hi, lo


def _route_kernel(an_ref, ln_ref, x_ref, wo_ref, g2_ref, wr_ref, br_ref,
                  x1_ref, h2_ref, gt_ref, ei_ref, cnt_ref, wob_ref, wrb_ref, tri_ref, run_ref, runc_ref):
    @pl.when(pl.program_id(0) == 0)
    def _():
        wob_ref[...] = wo_ref[...].astype(BF16)
        hi, lo = _split_bf16(wr_ref[...])
        wrb_ref[:RT_ROWS, :] = hi
        wrb_ref[RT_ROWS:, :] = lo
        r = lax.broadcasted_iota(jnp.int32, (RT_TM, RT_TM), 0)
        cidx = lax.broadcasted_iota(jnp.int32, (RT_TM, RT_TM), 1)
        tri_ref[...] = (r < cidx).astype(BF16)
        run_ref[...] = jnp.zeros_like(run_ref)
        runc_ref[...] = jnp.zeros_like(runc_ref)

    nt_dims = (((1,), (1,)), ((), ()))
    part = RT_TM // RT_PARTS
    x1s = []
    for r in range(RT_PARTS):
        rows = slice(r * part, (r + 1) * part)
        x1 = (x_ref[rows, :]
              + jnp.dot(an_ref[rows, :], wob_ref[:ATTN_WIDTH, :], preferred_element_type=F32)
              + jnp.dot(ln_ref[rows, :], wob_ref[ATTN_WIDTH:, :], preferred_element_type=F32))
        x1_ref[rows, :] = x1
        x1s.append(x1)
    splits = []
    for r, x1 in enumerate(x1s):
        h2 = _rms(x1, g2_ref[...])
        hi = h2.astype(BF16)
        hi_f = hi.astype(F32)
        h2_ref[r * part:(r + 1) * part, :] = _pack_rounded(hi_f)
        splits.append((hi, (h2 - hi_f).astype(BF16)))
    logits = []
    for hi, lo in splits:
        t1 = lax.dot_general(wrb_ref[...], hi, nt_dims, preferred_element_type=F32)
        t2 = lax.dot_general(wrb_ref[:RT_ROWS, :], lo, nt_dims, preferred_element_type=F32)
        logits.append(t1[:RT_ROWS] + t1[RT_ROWS:] + t2)
    logit = jnp.concatenate(logits, axis=1) + br_ref[...]

    sub = lax.broadcasted_iota(jnp.int32, (SUBLANES, RT_TM), 0)
    first_min = lambda hit: jnp.min(jnp.where(hit, sub, SUBLANES), axis=0, keepdims=True)
    is_g = sub < N_GROUPS
    gl = jnp.where(is_g, logit[:SUBLANES], -jnp.inf)
    gm = jnp.max(gl, axis=0, keepdims=True)
    gidx = first_min(gl == gm)
    g_p = 1.0 / jnp.sum(jnp.where(is_g, jnp.exp(logit[:SUBLANES] - gm), 0.0), axis=0, keepdims=True)
    el = logit[SUBLANES:2 * SUBLANES]
    for g in range(1, N_GROUPS):
        el = jnp.where(gidx == g, logit[(g + 1) * SUBLANES:(g + 2) * SUBLANES], el)
    m1 = jnp.max(el, axis=0, keepdims=True)
    i1 = first_min(el == m1)
    el2 = jnp.where(sub == i1, -jnp.inf, el)
    m2 = jnp.max(el2, axis=0, keepdims=True)
    i2 = first_min(el2 == m2)
    t = jnp.exp(m2 - m1)
    gate1 = g_p / (1.0 + t)
    gate2 = g_p * t / (1.0 + t)
    e1 = gidx * EXPERTS_PER_GROUP + i1
    e2 = gidx * EXPERTS_PER_GROUP + i2

    erow = lax.broadcasted_iota(jnp.int32, (N_EXPERTS, RT_TM), 0)
    oh1 = erow == e1
    oh2 = erow == e2
    oh = (oh1 | oh2).astype(F32)
    ohb = oh.astype(BF16)
    cum = jnp.dot(ohb, tri_ref[...], preferred_element_type=F32) + runc_ref[...]
    rank1 = jnp.sum(jnp.where(oh1, cum, 0.0), axis=0, keepdims=True)
    rank2 = jnp.sum(jnp.where(oh2, cum, 0.0), axis=0, keepdims=True)
    runc_ref[...] = runc_ref[...] + jnp.sum(oh, axis=1, keepdims=True)
    tile_cnt = lax.dot_general(jnp.ones((SUBLANES, RT_TM), BF16), ohb, nt_dims, preferred_element_type=F32)
    run_ref[:, :N_EXPERTS] = run_ref[:, :N_EXPERTS] + tile_cnt[0:1]
    cnt_ref[...] = run_ref[...].astype(jnp.int32)

    rows = [e1, e2, rank1.astype(jnp.int32), rank2.astype(jnp.int32)]
    ei = jnp.zeros((RINFO, RT_TM), jnp.int32)
    for k, v in enumerate(rows):
        ei = jnp.where(sub == k, v, ei)
    ei_ref[0] = ei
    gt_ref[...] = jnp.where(sub == 0, gate1, jnp.where(sub == 1, gate2, 0.0)).T


def _out_route(attn_n, lru_n, x2, w_out, ln2, w_group, b_group, w_er, b_er):
    T = x2.shape[0]
    pad_g = SUBLANES - N_GROUPS
    wr = jnp.concatenate([jnp.pad(w_group.T, ((0, pad_g), (0, 0))),
                          jnp.transpose(w_er, (0, 2, 1)).reshape(N_EXPERTS, D_MODEL)], axis=0)
    wr = jnp.pad(wr, ((0, RT_ROWS - wr.shape[0]), (0, 0))).astype(F32)
    br = jnp.concatenate([jnp.pad(b_group, (0, pad_g)), b_er.reshape(-1)])
    br = jnp.pad(br, (0, RT_ROWS - br.shape[0])).reshape(RT_ROWS, 1).astype(F32)
    row = lambda w: pl.BlockSpec((RT_TM, w), lambda i: (i, 0))
    const = lambda shape: pl.BlockSpec(shape, lambda i: (0, 0))
    return pl.pallas_call(
        _route_kernel,
        grid=(T // RT_TM,),
        in_specs=[row(ATTN_WIDTH), row(LRU_WIDTH), row(D_MODEL), const((D_MODEL, D_MODEL)), const((1, D_MODEL)),
                  const((RT_ROWS, D_MODEL)), const((RT_ROWS, 1))],
        out_specs=[row(D_MODEL), row(PACKED), row(RINFO),
                   pl.BlockSpec((1, RINFO, RT_TM), lambda i: (i, 0, 0)), const((1, RT_COLS))],
        out_shape=[jax.ShapeDtypeStruct((T, D_MODEL), F32),
                   jax.ShapeDtypeStruct((T, PACKED), U32),
                   jax.ShapeDtypeStruct((T, RINFO), F32),
                   jax.ShapeDtypeStruct((T // RT_TM, RINFO, RT_TM), jnp.int32),
                   jax.ShapeDtypeStruct((1, RT_COLS), jnp.int32)],
        scratch_shapes=[pltpu.VMEM((D_MODEL, D_MODEL), BF16),
                        pltpu.VMEM((2 * RT_ROWS, D_MODEL), BF16),
                        pltpu.VMEM((RT_TM, RT_TM), BF16),
                        pltpu.VMEM((1, RT_COLS), F32),
                        pltpu.VMEM((N_EXPERTS, 1), F32)],
        compiler_params=_cparams(1),
        name="out_route",
    )(attn_n, lru_n, x2, w_out, ln2.reshape(1, D_MODEL).astype(F32), wr, br)


def _moe_cap(T):
    A = T * TOP_K
    return ((A + MOE_BLOCK - 1) // MOE_BLOCK) * MOE_BLOCK + N_EXPERTS * MOE_BLOCK


PAD_BITS = tuple(1 << b for b in reversed(range(3, MOE_BLOCK.bit_length() - 1)))


def _layout_kernel(cnt_ref, ei_ref, dest_ref, pstart, be_ref, nu_ref, ge_ref):
    n_blocks = be_ref.shape[0]

    def lay(e, carry):
        start, blk, grp = carry
        pstart[e] = start
        nb = (cnt_ref[0, e] + MOE_BLOCK - 1) // MOE_BLOCK
        ge_ref[grp] = e

        def fill(k, c):
            be_ref[blk + k] = e
            return c
        lax.fori_loop(0, nb, fill, 0)
        return start + nb * MOE_BLOCK, blk + nb, grp + (nb > 0).astype(jnp.int32)
    _, used, groups = lax.fori_loop(0, N_EXPERTS, lay, (jnp.int32(0), jnp.int32(0), jnp.int32(0)))
    nu_ref[0] = used

    def tail(k, c):
        be_ref[k] = N_EXPERTS - 1
        return c
    lax.fori_loop(used, n_blocks, tail, 0)

    def no_group(k, c):
        ge_ref[k] = -1
        return c
    lax.fori_loop(groups, ge_ref.shape[0], no_group, 0)

    expert = ei_ref[:, 0:TOP_K, :]
    dest = ei_ref[:, TOP_K:2 * TOP_K, :]
    for e in range(N_EXPERTS):
        dest = dest + jnp.where(expert == e, pstart[e], 0)
    dest_ref[...] = dest


def _layout(ei, cnt, n_blocks):
    nt = ei.shape[0]
    smem = pl.BlockSpec(memory_space=pltpu.SMEM)
    vmem = pl.BlockSpec(memory_space=pltpu.VMEM)
    return pl.pallas_call(
        _layout_kernel,
        in_specs=[smem, vmem],
        out_specs=[vmem, smem, smem, smem, smem],
        out_shape=[jax.ShapeDtypeStruct((nt, TOP_K, RT_TM), jnp.int32),
                   jax.ShapeDtypeStruct((N_EXPERTS,), jnp.int32),
                   jax.ShapeDtypeStruct((n_blocks,), jnp.int32),
                   jax.ShapeDtypeStruct((1,), jnp.int32),
                   jax.ShapeDtypeStruct((N_EXPERTS + W_AHEAD,), jnp.int32)],
        name="layout",
    )(cnt, ei)


SC_CHUNK = 64
SC_BUFS = 3
SC_LEAD = SC_BUFS - 1


def _sc_workers():
    info = plsc.get_sparse_core_info()
    return info.num_cores, info.num_subcores


def _sc_ring(n_chunks, read, write):
    for c in range(min(SC_LEAD, n_chunks)):
        for cp in read(c):
            cp.start()
    reclaimed = set()
    for c in range(n_chunks):
        for cp in read(c):
            cp.wait()
        for cp in write(c):
            cp.start()
        nxt = c + SC_LEAD
        if nxt < n_chunks:
            if nxt - SC_BUFS >= 0:
                for cp in write(nxt - SC_BUFS):
                    cp.wait()
                reclaimed.add(nxt - SC_BUFS)
            for cp in read(nxt):
                cp.start()
    for c in range(n_chunks):
        if c not in reclaimed:
            for cp in write(c):
                cp.wait()


def _sc_dispatch(h2p, dest, cap):
    T = h2p.shape[0]
    nc, ns = _sc_workers()
    per_w = T // (nc * ns)
    n_ch = per_w // SC_CHUNK
    nt, _, tm = dest.shape
    assert nt * tm == T and tm % per_w == 0 and per_w % SC_CHUNK == 0
    idx = dest.reshape(nt, TOP_K, tm // per_w, per_w).transpose(0, 2, 1, 3).reshape(nc * ns, TOP_K * n_ch, SC_CHUNK)
    mesh = plsc.VectorSubcoreMesh(core_axis_name="c", subcore_axis_name="s")

    @functools.partial(
        pl.kernel, mesh=mesh,
        out_type=jax.ShapeDtypeStruct((cap, PACKED), U32),
        scratch_types=[pltpu.VMEM((TOP_K * n_ch, SC_CHUNK), jnp.int32),
                       pltpu.VMEM((SC_BUFS, SC_CHUNK, PACKED), U32),
                       pltpu.SemaphoreType.DMA((SC_BUFS,)),
                       pltpu.SemaphoreType.DMA((SC_BUFS,))])
    def scatter(src_hbm, idx_hbm, out_hbm, idx_v, rows_v, rsem, wsem):
        wid = lax.axis_index("s") * nc + lax.axis_index("c")
        base = pl.multiple_of(wid * per_w, per_w)
        pltpu.sync_copy(idx_hbm.at[wid], idx_v)

        def read(c):
            b = c % SC_BUFS
            return [pltpu.make_async_copy(src_hbm.at[pl.ds(base + c * SC_CHUNK, SC_CHUNK)], rows_v.at[b], rsem.at[b])]

        def write(c):
            b = c % SC_BUFS
            return [pltpu.make_async_copy(rows_v.at[b], out_hbm.at[idx_v.at[k * n_ch + c]], wsem.at[b])
                    for k in range(TOP_K)]
        _sc_ring(n_ch, read, write)

    return scatter(h2p, idx)


def _sc_gather(yb, dest):
    nt, _, tm = dest.shape
    nc, ns = _sc_workers()
    n_rows = nt * TOP_K * tm
    per_w = n_rows // (nc * ns)
    n_ch = per_w // SC_CHUNK
    assert per_w * nc * ns == n_rows and per_w % SC_CHUNK == 0
    mesh = plsc.VectorSubcoreMesh(core_axis_name="c", subcore_axis_name="s")

    @functools.partial(
        pl.kernel, mesh=mesh,
        out_type=jax.ShapeDtypeStruct((n_rows, PACKED), U32),
        scratch_types=[pltpu.VMEM((per_w,), jnp.int32),
                       pltpu.VMEM((SC_BUFS, SC_CHUNK, PACKED), U32),
                       pltpu.SemaphoreType.DMA((SC_BUFS,)),
                       pltpu.SemaphoreType.DMA((SC_BUFS,))])
    def gather(table_hbm, idx_hbm, out_hbm, idx_v, rows_v, gsem, wsem):
        wid = lax.axis_index("s") * nc + lax.axis_index("c")
        base = pl.multiple_of(wid * per_w, per_w)
        pltpu.sync_copy(idx_hbm.at[pl.ds(base, per_w)], idx_v)

        def read(c):
            b = c % SC_BUFS
            return [pltpu.make_async_copy(table_hbm.at[idx_v.at[pl.ds(c * SC_CHUNK, SC_CHUNK)]], rows_v.at[b], gsem.at[b])]

        def write(c):
            b = c % SC_BUFS
            return [pltpu.make_async_copy(rows_v.at[b], out_hbm.at[pl.ds(base + c * SC_CHUNK, SC_CHUNK)], wsem.at[b])]
        _sc_ring(n_ch, read, write)

    return gather(yb, dest.reshape(n_rows)).reshape(nt, TOP_K, tm, PACKED)


def _padfill_kernel(cnt_ref, pstart, xs_in, xs_ref, zeros, zsem):
    del xs_in

    def pad_copies(fn):
        for e in range(N_EXPERTS):
            cnt = cnt_ref[0, e]
            head = (-cnt) & (SUBLANES - 1)
            rest = ((-cnt) & (MOE_BLOCK - 1)) - head
            off = pstart[e] + cnt
            for k in range(SUBLANES - 1):
                @pl.when(k < head)
                def _(off=off, k=k):
                    fn(pltpu.make_async_copy(zeros.at[pl.ds(0, 1), :], xs_ref.at[pl.ds(off + k, 1), :], zsem))
            off = off + head
            for bit in PAD_BITS:
                @pl.when((rest & bit) != 0)
                def _(off=off, bit=bit):
                    fn(pltpu.make_async_copy(zeros.at[pl.ds(0, bit), :],
                                             xs_ref.at[pl.ds(pl.multiple_of(off, SUBLANES), bit), :], zsem))
                off = off + (rest & bit)

    zeros[...] = jnp.zeros_like(zeros)
    pad_copies(lambda cp: cp.start())
    pad_copies(lambda cp: cp.wait())


def _padfill(xs, pstart, cnt):
    smem = pl.BlockSpec(memory_space=pltpu.SMEM)
    hbm = pl.BlockSpec(memory_space=pl.ANY)
    return pl.pallas_call(
        _padfill_kernel,
        in_specs=[smem, smem, hbm],
        out_specs=hbm,
        out_shape=jax.ShapeDtypeStruct(xs.shape, xs.dtype),
        input_output_aliases={2: 0},
        scratch_shapes=[pltpu.VMEM((MOE_BLOCK // 2, PACKED), U32), pltpu.SemaphoreType.DMA(())],
        name="padfill",
    )(cnt, pstart, xs)


W_SLOTS = 3
W_AHEAD = W_SLOTS - 1
EXPERT_GROUP = 8
EXPERT_RUNS = (1, 2, 4)


def _expert_kernel(be_ref, nu_ref, ge_ref, x_ref, wg_hbm, wu_hbm, wd_hbm, o_ref,
                   wgf, wuf, wdf, wgb, wub, wdb, grp_ref, sems):
    step = pl.program_id(0)

    def weight_copies(e, slot):
        return (pltpu.make_async_copy(wg_hbm.at[e], wgf.at[slot], sems.at[slot, 0]),
                pltpu.make_async_copy(wu_hbm.at[e], wuf.at[slot], sems.at[slot, 1]),
                pltpu.make_async_copy(wd_hbm.at[e], wdf.at[slot], sems.at[slot, 2]))

    @pl.when(step == 0)
    def _():
        grp_ref[0] = 0
        for a in range(W_AHEAD):
            @pl.when(ge_ref[a] >= 0)
            def _(a=a):
                for cp in weight_copies(ge_ref[a], a):
                    cp.start()

    n_blocks = be_ref.shape[0]
    n_used = nu_ref[0]

    def swiglu(s, n):
        rows = pl.ds(pl.multiple_of(s * MOE_BLOCK, MOE_BLOCK), n * MOE_BLOCK)
        lo, hi = _unpack_rows(x_ref[rows, :])
        lo = lo.astype(BF16)
        hi = hi.astype(BF16)
        g = (jnp.dot(lo, wgb[:PACKED, :], preferred_element_type=F32)
             + jnp.dot(hi, wgb[PACKED:, :], preferred_element_type=F32))
        u = (jnp.dot(lo, wub[:PACKED, :], preferred_element_type=F32)
             + jnp.dot(hi, wub[PACKED:, :], preferred_element_type=F32))
        h = (g * _sigmoid(g) * u).astype(BF16)
        o_ref[rows, :] = _pack_rows(jnp.dot(h, wdb[...], preferred_element_type=F32))

    def run(s):
        j = step * EXPERT_GROUP + s
        e = be_ref[j]
        first = jnp.logical_or(j == 0, e != be_ref[jnp.maximum(j - 1, 0)])

        @pl.when(first)
        def _():
            grp = grp_ref[0]
            slot = grp % W_SLOTS
            for cp in weight_copies(e, slot):
                cp.wait()
            wgb[...] = wgf[slot].astype(BF16)
            wub[...] = wuf[slot].astype(BF16)
            wdb[...] = wdf[slot].astype(BF16)
            nxt = ge_ref[grp + W_AHEAD]

            @pl.when(nxt >= 0)
            def _():
                for cp in weight_copies(nxt, (grp + W_AHEAD) % W_SLOTS):
                    cp.start()
            grp_ref[0] = grp + 1

        def same(k):
            return (s + k < EXPERT_GROUP) & (j + k < n_used) & (be_ref[jnp.minimum(j + k, n_blocks - 1)] == e)
        take = jnp.int32(1)
        for n in EXPERT_RUNS[1:]:
            ok = same(n - 1)
            for k in range(1, n - 1):
                ok = ok & same(k)
            take = jnp.where(ok, n, take)
        for n in EXPERT_RUNS:
            @pl.when(take == n)
            def _(n=n):
                swiglu(s, n)
        return s + take

    lax.while_loop(lambda s: (s < EXPERT_GROUP) & (step * EXPERT_GROUP + s < n_used), run, jnp.int32(0))


def _experts(xs, block_expert, n_used, group_expert, w_gate, w_up, w_down):
    cap = xs.shape[0]
    n_blocks = cap // MOE_BLOCK
    assert n_blocks % EXPERT_GROUP == 0
    rows = EXPERT_GROUP * MOE_BLOCK
    last = lambda j, be, nu, ge: jnp.minimum(j, (nu[0] - 1) // EXPERT_GROUP)
    hbm = pl.BlockSpec(memory_space=pl.ANY)
    gs = pltpu.PrefetchScalarGridSpec(
        num_scalar_prefetch=3,
        grid=(n_blocks // EXPERT_GROUP,),
        in_specs=[pl.BlockSpec((rows, PACKED), lambda j, be, nu, ge: (last(j, be, nu, ge), 0)), hbm, hbm, hbm],
        out_specs=pl.BlockSpec((rows, PACKED), lambda j, be, nu, ge: (last(j, be, nu, ge), 0)),
        scratch_shapes=[pltpu.VMEM((W_SLOTS, D_MODEL, D_EXPERT), F32),
                        pltpu.VMEM((W_SLOTS, D_MODEL, D_EXPERT), F32),
                        pltpu.VMEM((W_SLOTS, D_EXPERT, D_MODEL), F32),
                        pltpu.VMEM((D_MODEL, D_EXPERT), BF16),
                        pltpu.VMEM((D_MODEL, D_EXPERT), BF16),
                        pltpu.VMEM((D_EXPERT, D_MODEL), BF16),
                        pltpu.SMEM((1,), jnp.int32),
                        pltpu.SemaphoreType.DMA((W_SLOTS, 3))],
    )
    return pl.pallas_call(
        _expert_kernel,
        grid_spec=gs,
        out_shape=jax.ShapeDtypeStruct((cap, PACKED), U32),
        compiler_params=_cparams(1),
        name="experts",
    )(block_expert, n_used, group_expert, xs, w_gate, w_up, w_down)


CB_TM = RT_TM


def _combine_kernel(x1_ref, gt_ref, y2_ref, o_ref):
    g = gt_ref[...]
    lo1, hi1 = _unpack_rows(y2_ref[0, 0])
    lo2, hi2 = _unpack_rows(y2_ref[0, 1])
    o_ref[:, :PACKED] = x1_ref[:, :PACKED] + g[:, 0:1] * lo1 + g[:, 1:2] * lo2
    o_ref[:, PACKED:] = x1_ref[:, PACKED:] + g[:, 0:1] * hi1 + g[:, 1:2] * hi2


def _combine(x1, gates, y2):
    T = x1.shape[0]
    nt = T // CB_TM
    return pl.pallas_call(
        _combine_kernel,
        grid=(nt,),
        in_specs=[pl.BlockSpec((CB_TM, D_MODEL), lambda i: (i, 0)),
                  pl.BlockSpec((CB_TM, RINFO), lambda i: (i, 0)),
                  pl.BlockSpec((1, TOP_K, CB_TM, PACKED), lambda i: (i, 0, 0, 0))],
        out_specs=pl.BlockSpec((CB_TM, D_MODEL), lambda i: (i, 0)),
        out_shape=jax.ShapeDtypeStruct((T, D_MODEL), F32),
        compiler_params=_cparams(1),
        name="combine",
    )(x1, gates, y2)


def _layer(x, rel_bias, ln1, w_in, q_norm, k_norm, attn_sink, conv_w, conv_b, lru_wa, lru_ba, lru_wi, lru_bi,
           lru_lambda, out_norm_attn, out_norm_lru, w_out, ln2, w_group, b_group, w_er, b_er, w_gate, w_up, w_down):
    B, S, D = x.shape
    T = B * S
    x2 = x.reshape(T, D)
    q, kv, xr, gr = _in_proj(x2, ln1, w_in, q_norm, k_norm)
    attn_n = _attention(q.reshape(B, S, ATTN_WIDTH), kv.reshape(B, S, 2 * KV_WIDTH), rel_bias, attn_sink,
                        out_norm_attn)
    lru_n = _rglru(xr.reshape(B, S, LRU_WIDTH), gr.reshape(B, S, LRU_WIDTH), conv_w, conv_b,
                   lru_wa, lru_ba, lru_wi, lru_bi, lru_lambda, out_norm_lru)
    x1, h2, gates, ei, cnt = _out_route(attn_n.reshape(T, ATTN_WIDTH), lru_n.reshape(T, LRU_WIDTH), x2, w_out, ln2,
                                        w_group, b_group, w_er, b_er)
    cap = _moe_cap(T)
    dest, pstart, block_expert, n_used, group_expert = _layout(ei, cnt, cap // MOE_BLOCK)
    xs = _padfill(_sc_dispatch(h2, dest, cap), pstart, cnt)
    yb = _experts(xs, block_expert, n_used, group_expert, w_gate, w_up, w_down)
    out = _combine(x1, gates, _sc_gather(yb, dest))
    return out.reshape(B, S, D)


def kernel(x, rel_bias, ln1, w_in, q_norm, k_norm, attn_sink, conv_w, conv_b, lru_wa, lru_ba, lru_wi, lru_bi,
           lru_lambda, out_norm_attn, out_norm_lru, w_out, ln2, w_group, b_group, w_expert_router, b_expert_router,
           w_gate, w_up, w_down):
    depth = ln1.shape[0]
    for l in range(depth):
        x = _layer(x, rel_bias, ln1[l], w_in[l], q_norm[l], k_norm[l], attn_sink[l], conv_w[l], conv_b[l],
                   lru_wa[l], lru_ba[l], lru_wi[l], lru_bi[l], lru_lambda[l], out_norm_attn[l], out_norm_lru[l],
                   w_out[l], ln2[l], w_group[l], b_group[l], w_expert_router[l], b_expert_router[l],
                   w_gate[l], w_up[l], w_down[l])
    return x
```

```python
import functools
import math

import jax
import jax.numpy as jnp
import numpy as np
from jax import lax
from jax.experimental import pallas as pl
from jax.experimental.pallas import tpu as pltpu
from jax.experimental.pallas import tpu_sc as plsc

D_MODEL = 1024
N_HEADS = 8
N_KV_HEADS = 2
HEAD_DIM = 64
Q_PER_KV = N_HEADS // N_KV_HEADS
ATTN_WIDTH = N_HEADS * HEAD_DIM
KV_WIDTH = N_KV_HEADS * HEAD_DIM
WINDOW = 128
BLOCK = 128
NUM_BUCKETS = 32
MAX_DISTANCE = 128
LRU_WIDTH = D_MODEL - ATTN_WIDTH
LRU_BLOCKS = 8
LRU_BLOCK_DIM = LRU_WIDTH // LRU_BLOCKS
LRU_C = 8.0
CONV_W = 4
CONV_LEFT = 2
N_GROUPS = 4
EXPERTS_PER_GROUP = 8
N_EXPERTS = N_GROUPS * EXPERTS_PER_GROUP
TOP_K = 2
D_EXPERT = 512
MOE_BLOCK = 128
EPS = 1e-6
NEG_INF = -1e30

LANES = 128
SUBLANES = 8
VMEM_LIMIT = 56 * 1024 * 1024

F32 = jnp.float32
BF16 = jnp.bfloat16
LOG2E = math.log2(math.e)


def _cparams(n_axes, vmem=VMEM_LIMIT):
    return pltpu.CompilerParams(dimension_semantics=("arbitrary",) * n_axes, vmem_limit_bytes=vmem)


def _rms(x, gain):
    return x * lax.rsqrt(jnp.mean(x * x, axis=-1, keepdims=True) + EPS) * gain


U32 = jnp.uint32
HI_MASK = 0xFFFF0000
PACKED = D_MODEL // 2


def _pack_rows(x):
    return _pack_rounded(x.astype(BF16).astype(F32))


def _pack_rounded(xb):
    h = xb.shape[1] // 2
    lo = lax.bitcast_convert_type(xb[:, :h], U32) >> 16
    hi = lax.bitcast_convert_type(xb[:, h:], U32) & jnp.uint32(HI_MASK)
    return lo | hi


def _unpack_rows(p):
    lo = lax.bitcast_convert_type(p << 16, F32)
    hi = lax.bitcast_convert_type(p & jnp.uint32(HI_MASK), F32)
    return lo, hi


IN_TM = 1024


def _head_rms(x, n_heads, gain):
    head = lax.broadcasted_iota(jnp.int32, (1, n_heads * HEAD_DIM), 1) // HEAD_DIM
    x2 = x * x
    scale = jnp.zeros_like(x)
    for h in range(n_heads):
        ms = jnp.sum(jnp.where(head == h, x2, 0.0), axis=-1, keepdims=True) * (1.0 / HEAD_DIM)
        scale = jnp.where(head == h, lax.rsqrt(ms + EPS), scale)
    return x * scale * gain


def _in_proj_kernel(x_ref, g_ref, w_ref, qg_ref, kg_ref, q_ref, kv_ref, xr_ref, gr_ref, wb_ref):
    @pl.when(pl.program_id(0) == 0)
    def _():
        wb_ref[...] = w_ref[...].astype(BF16)

    h = _rms(x_ref[...], g_ref[...]).astype(BF16)
    c_k = ATTN_WIDTH
    c_v = c_k + KV_WIDTH
    c_x = c_v + KV_WIDTH
    c_g = c_x + LRU_WIDTH
    q = jnp.dot(h, wb_ref[:, :c_k], preferred_element_type=F32)
    q_ref[...] = _head_rms(q, N_HEADS, qg_ref[...]).astype(BF16)
    k = jnp.dot(h, wb_ref[:, c_k:c_v], preferred_element_type=F32)
    kv_ref[:, :KV_WIDTH] = _head_rms(k, N_KV_HEADS, kg_ref[...]).astype(BF16)
    kv_ref[:, KV_WIDTH:] = jnp.dot(h, wb_ref[:, c_v:c_x], preferred_element_type=F32).astype(BF16)
    xr_ref[...] = jnp.dot(h, wb_ref[:, c_x:c_g], preferred_element_type=F32)
    gr_ref[...] = jnp.dot(h, wb_ref[:, c_g:], preferred_element_type=F32)


def _in_proj(x2, ln1, w_in, q_gain, k_gain):
    T = x2.shape[0]
    n_in = w_in.shape[1]
    row = lambda w: pl.BlockSpec((IN_TM, w), lambda i: (i, 0))
    qg = (jnp.tile(q_gain.astype(F32), N_HEADS) * (HEAD_DIM ** -0.5 * LOG2E)).reshape(1, ATTN_WIDTH)
    kg = jnp.tile(k_gain.astype(F32), N_KV_HEADS).reshape(1, KV_WIDTH)
    return pl.pallas_call(
        _in_proj_kernel,
        grid=(T // IN_TM,),
        in_specs=[row(D_MODEL),
                  pl.BlockSpec((1, D_MODEL), lambda i: (0, 0)),
                  pl.BlockSpec((D_MODEL, n_in), lambda i: (0, 0)),
                  pl.BlockSpec((1, ATTN_WIDTH), lambda i: (0, 0)),
                  pl.BlockSpec((1, KV_WIDTH), lambda i: (0, 0))],
        out_specs=[row(ATTN_WIDTH), row(2 * KV_WIDTH), row(LRU_WIDTH), row(LRU_WIDTH)],
        out_shape=[jax.ShapeDtypeStruct((T, ATTN_WIDTH), BF16),
                   jax.ShapeDtypeStruct((T, 2 * KV_WIDTH), BF16),
                   jax.ShapeDtypeStruct((T, LRU_WIDTH), F32),
                   jax.ShapeDtypeStruct((T, LRU_WIDTH), F32)],
        scratch_shapes=[pltpu.VMEM((D_MODEL, n_in), BF16)],
        compiler_params=_cparams(1),
        name="in_proj",
    )(x2, ln1.reshape(1, D_MODEL), w_in, qg, kg)


def _t5_bucket(rel):
    half = NUM_BUCKETS // 2
    max_exact = half // 2
    base = jnp.where(rel > 0, half, 0)
    n = jnp.abs(rel)
    nf = jnp.maximum(n, 1).astype(jnp.float32)
    large = max_exact + (jnp.log(nf / max_exact) / math.log(MAX_DISTANCE / max_exact)
                         * (half - max_exact)).astype(jnp.int32)
    large = jnp.minimum(large, half - 1)
    return base + jnp.where(n < max_exact, n, large)


HEAD_PAIRS = Q_PER_KV // 2
EDGE_VARIANTS = 3


def _fill_bias_table(rb_ref, bucket_ref, band_ref, o_ref):
    bucket = bucket_ref[...]
    band = band_ref[...] > 0
    col = lax.broadcasted_iota(jnp.int32, bucket.shape, 1)
    valid = (band & (col >= BLOCK), band, band & (col < 2 * BLOCK))
    for h in range(N_HEADS):
        acc = jnp.zeros(bucket.shape, F32)
        for b in range(NUM_BUCKETS):
            acc = jnp.where(bucket == b, rb_ref[b, h], acc)
        kv, g = divmod(h, Q_PER_KV)
        pair, parity = divmod(g, 2)
        for var in range(EDGE_VARIANTS):
            o_ref[var, kv, parity, pair * BLOCK:(pair + 1) * BLOCK, :] = jnp.where(valid[var], acc * LOG2E, NEG_INF)


def _attn_kernel(sink_ref, rb_ref, q_ref, kp_ref, kc_ref, kn_ref, bucket_ref, band_ref, og_ref, o_ref, bias_ref):
    n = pl.program_id(1)

    @pl.when((pl.program_id(0) == 0) & (n == 0))
    def _():
        _fill_bias_table(rb_ref, bucket_ref, band_ref, bias_ref)

    kv_all = jnp.concatenate([kp_ref[0], kc_ref[0], kn_ref[0]], axis=0)
    for qb in range(ATTN_QB):
        variant = 1
        if qb == 0:
            variant = jnp.where(n == 0, 0, 1)
        if qb == ATTN_QB - 1:
            variant = jnp.where(n == pl.num_programs(1) - 1, 2, variant)
        out = _attn_block(q_ref[0, qb * BLOCK:(qb + 1) * BLOCK, :], kv_all[qb * BLOCK:(qb + 3) * BLOCK, :],
                          lambda kv, parity: bias_ref[variant, kv, parity], sink_ref)
        o_ref[0, qb * BLOCK:(qb + 1) * BLOCK, :] = _rms(out, og_ref[...]).astype(o_ref.dtype)


def _attn_block(q, kvw, bias, sink_ref):
    low = lax.broadcasted_iota(jnp.int32, (1, LANES), 1) < HEAD_DIM
    swap = lambda slab: pltpu.roll(slab.astype(F32), HEAD_DIM, 1).astype(BF16)
    kslab, vslab = kvw[:, :KV_WIDTH], kvw[:, KV_WIDTH:]
    kslab_sw, vslab_sw = swap(kslab), swap(vslab)
    rowi = lax.broadcasted_iota(jnp.int32, (HEAD_PAIRS * BLOCK, 1), 0)
    combos = [(kv, parity) for kv in range(N_KV_HEADS) for parity in range(2)]
    scores, vzs, sinks = [], [], []
    for kv, parity in combos:
        ks, vs = (kslab, vslab) if (kv == 0) == (parity == 0) else (kslab_sw, vslab_sw)
        keep = low if parity == 0 else jnp.logical_not(low)
        kz = jnp.where(keep, ks, jnp.zeros_like(ks))
        vzs.append(jnp.where(keep, vs, jnp.zeros_like(vs)))
        base = kv * Q_PER_KV * HEAD_DIM
        qpair = jnp.concatenate([q[:, base + j * LANES:base + (j + 1) * LANES] for j in range(HEAD_PAIRS)], axis=0)
        s = lax.dot_general(qpair, kz, (((1,), (1,)), ((), ())), preferred_element_type=F32)
        scores.append(s + bias(kv, parity))
        sink = jnp.zeros((HEAD_PAIRS * BLOCK, 1), F32)
        for j in range(HEAD_PAIRS):
            sink = jnp.where(rowi // BLOCK == j, sink_ref[kv * Q_PER_KV + 2 * j + parity], sink)
        sinks.append(sink)
    probs, inv = [], []
    for s, sink in zip(scores, sinks):
        m = jnp.maximum(jnp.max(s, axis=-1, keepdims=True), sink)
        p = jnp.exp2(s - m)
        inv.append(1.0 / (jnp.sum(p, axis=-1, keepdims=True) + jnp.exp2(sink - m)))
        probs.append(p.astype(BF16))
    outs = [jnp.dot(p, vz, preferred_element_type=F32) * r for p, vz, r in zip(probs, vzs, inv)]
    cols = []
    for kv in range(N_KV_HEADS):
        acc = outs[2 * kv] + outs[2 * kv + 1]
        cols += [acc[j * BLOCK:(j + 1) * BLOCK, :] for j in range(HEAD_PAIRS)]
    return jnp.concatenate(cols, axis=1)


ATTN_QB = 4


def _attention(q, kv, rel_bias, sink, out_gain):
    B, S, _ = q.shape
    nb = S // BLOCK
    assert ATTN_QB >= 2 and nb % ATTN_QB == 0, "a step's first and last query blocks must be distinct"
    ns = nb // ATTN_QB
    rows = ATTN_QB * BLOCK
    qi = jnp.arange(BLOCK, dtype=jnp.int32)
    kj = jnp.arange(3 * BLOCK, dtype=jnp.int32)
    rel = kj[None, :] - BLOCK - qi[:, None]
    bucket = _t5_bucket(rel).astype(jnp.int32)
    band = (jnp.abs(rel) <= WINDOW).astype(jnp.int32)
    kvspec = lambda f: pl.BlockSpec((1, BLOCK, 2 * KV_WIDTH), f)
    smem = pl.BlockSpec(memory_space=pltpu.SMEM)
    geom = pl.BlockSpec((BLOCK, 3 * BLOCK), lambda b, n: (0, 0))
    return pl.pallas_call(
        _attn_kernel,
        grid=(B, ns),
        in_specs=[smem, smem,
                  pl.BlockSpec((1, rows, ATTN_WIDTH), lambda b, n: (b, n, 0)),
                  kvspec(lambda b, n: (b, jnp.maximum(n * ATTN_QB - 1, 0), 0)),
                  pl.BlockSpec((1, rows, 2 * KV_WIDTH), lambda b, n: (b, n, 0)),
                  kvspec(lambda b, n: (b, jnp.minimum((n + 1) * ATTN_QB, nb - 1), 0)),
                  geom, geom,
                  pl.BlockSpec((1, ATTN_WIDTH), lambda b, n: (0, 0))],
        out_specs=pl.BlockSpec((1, rows, ATTN_WIDTH), lambda b, n: (b, n, 0)),
        out_shape=jax.ShapeDtypeStruct((B, S, ATTN_WIDTH), BF16),
        scratch_shapes=[pltpu.VMEM((EDGE_VARIANTS, N_KV_HEADS, 2, HEAD_PAIRS * BLOCK, 3 * BLOCK), F32)],
        compiler_params=_cparams(2),
        name="attention",
    )(sink.astype(F32) * LOG2E, rel_bias.astype(F32), q, kv, kv, kv, bucket, band, out_gain.reshape(1, ATTN_WIDTH))


LRU_TC = 128
LRU_PITCH = LRU_TC + SUBLANES
LRU_SLABS = LRU_WIDTH // LANES
HALO = SUBLANES


def _softplus(x):
    return jnp.maximum(x, 0.0) + jnp.log(1.0 + jnp.exp(-jnp.abs(x)))


def _gelu_tanh(x):
    k = math.sqrt(2.0 / math.pi)
    hx = 0.5 * x
    return hx + hx * jnp.tanh(x * (k + (k * 0.044715) * (x * x)))


def _sigmoid(x):
    return 0.5 + 0.5 * jnp.tanh(0.5 * x)


def _rglru_kernel(xr_ref, xp_ref, xn_ref, gr_ref, cw_ref, cb_ref, wa_ref, wi_ref, ba_ref, bi_ref, lam_ref, og_ref,
                  o_ref, sx_ref, a_ref, u_ref, h_ref, carry_ref, hf_ref, wg_ref, bg_ref, k_ref):
    p = pl.program_id(0)
    i = pl.program_id(1)
    nc = pl.num_programs(1)
    c = i + p * (nc - 1 - 2 * i)
    B = xr_ref.shape[0]
    TC = LRU_TC

    @pl.when(i == 0)
    def _():
        carry_ref[...] = jnp.zeros_like(carry_ref)
        wg_ref[...] = jnp.zeros_like(wg_ref)
        for sel, w_ref in enumerate((wa_ref, wi_ref)):
            for h in range(LRU_BLOCKS):
                lo = h * LRU_BLOCK_DIM
                wg_ref[lo:lo + LRU_BLOCK_DIM, sel * LRU_WIDTH + lo:sel * LRU_WIDTH + lo + LRU_BLOCK_DIM] = (
                    0.5 * w_ref[0, h]).astype(BF16)
        row = pl.ds(p, 1)
        bg_ref[:, :LRU_WIDTH] = 0.5 * ba_ref[row, :]
        bg_ref[:, LRU_WIDTH:] = 0.5 * bi_ref[row, :]
        k_ref[...] = (-0.5 * LRU_C * math.log2(math.e)) * _softplus(-lam_ref[row, :])

    sx_ref[:, HALO:HALO + TC, :] = xr_ref[...]
    sx_ref[:, 0:HALO, :] = jnp.where(c > 0, xp_ref[...], 0.0)
    sx_ref[:, HALO + TC:, :] = jnp.where(c < nc - 1, xn_ref[...], 0.0)
    xc = cb_ref[...][None]
    for j in range(CONV_W):
        off = HALO + j - CONV_LEFT
        xc = xc + cw_ref[j:j + 1, :][None] * sx_ref[:, off:off + TC, :]
    xc2 = xc.reshape(B * TC, LRU_WIDTH)

    g = jnp.dot(xc2.astype(BF16), wg_ref[...], preferred_element_type=F32) + bg_ref[...]
    ta = jnp.tanh(g[:, :LRU_WIDTH])
    ig = 0.5 + 0.5 * jnp.tanh(g[:, LRU_WIDTH:])
    a = jnp.exp2((1.0 + ta) * k_ref[...])
    z = 1.0 - a * a
    u = z * lax.rsqrt(jnp.maximum(z, 1e-30)) * ig * xc2
    for b in range(B):
        for s in range(LRU_SLABS):
            a_ref[s, b * LRU_PITCH:b * LRU_PITCH + TC, :] = a[b * TC:(b + 1) * TC, s * LANES:(s + 1) * LANES]
            u_ref[s, b * LRU_PITCH:b * LRU_PITCH + TC, :] = u[b * TC:(b + 1) * TC, s * LANES:(s + 1) * LANES]

    def step(k, hs):
        t = k + p * (TC - 1 - 2 * k)
        out = []
        for s in range(LRU_SLABS):
            idx = pl.ds(t, B, stride=LRU_PITCH)
            hn = a_ref[s, idx, :] * hs[s] + u_ref[s, idx, :]
            h_ref[s, idx, :] = hn
            out.append(hn)
        return tuple(out)

    hs = lax.fori_loop(0, TC, step, tuple(carry_ref[s] for s in range(LRU_SLABS)), unroll=8)
    for s in range(LRU_SLABS):
        carry_ref[s] = hs[s]

    @pl.when(p == 0)
    def _():
        for b in range(B):
            for s in range(LRU_SLABS):
                hf_ref[c, s, b * TC:(b + 1) * TC, :] = h_ref[s, b * LRU_PITCH:b * LRU_PITCH + TC, :].astype(hf_ref.dtype)

    @pl.when(p == 1)
    def _():
        for b in range(B):
            hsum = jnp.concatenate(
                [h_ref[s, b * LRU_PITCH:b * LRU_PITCH + TC, :] + hf_ref[c, s, b * TC:(b + 1) * TC, :].astype(F32)
                 for s in range(LRU_SLABS)], axis=1)
            y = hsum * _gelu_tanh(gr_ref[b])
            o_ref[b] = _rms(y, og_ref[...]).astype(o_ref.dtype)


def _rglru(xr, gr, conv_w, conv_b, w_a, b_a, w_i, b_i, lam, out_gain):
    B, S, W = xr.shape
    nc = S // LRU_TC
    hb = LRU_TC // HALO
    chunk = lambda p, i: i + p * (nc - 1 - 2 * i)
    full2 = lambda shape: pl.BlockSpec(shape, lambda p, i: (0,) * len(shape))
    wblock = pl.BlockSpec((1, LRU_BLOCKS, LRU_BLOCK_DIM, LRU_BLOCK_DIM), lambda p, i: (p, 0, 0, 0))
    return pl.pallas_call(
        _rglru_kernel,
        grid=(2, nc),
        in_specs=[pl.BlockSpec((B, LRU_TC, W), lambda p, i: (0, chunk(p, i), 0)),
                  pl.BlockSpec((B, HALO, W), lambda p, i: (0, jnp.maximum(chunk(p, i) * hb - 1, 0), 0)),
                  pl.BlockSpec((B, HALO, W), lambda p, i: (0, jnp.minimum((chunk(p, i) + 1) * hb, S // HALO - 1), 0)),
                  pl.BlockSpec((B, LRU_TC, W), lambda p, i: (0, chunk(p, i), 0)),
                  full2((CONV_W, W)),
                  full2((1, W)),
                  wblock, wblock,
                  full2((2, W)), full2((2, W)), full2((2, W)),
                  full2((1, W))],
        out_specs=pl.BlockSpec((B, LRU_TC, W), lambda p, i: (0, nc - 1 - p * i, 0)),
        out_shape=jax.ShapeDtypeStruct((B, S, W), BF16),
        scratch_shapes=[pltpu.VMEM((B, LRU_TC + 2 * HALO, W), F32),
                        pltpu.VMEM((LRU_SLABS, B * LRU_PITCH, LANES), F32),
                        pltpu.VMEM((LRU_SLABS, B * LRU_PITCH, LANES), F32),
                        pltpu.VMEM((LRU_SLABS, B * LRU_PITCH, LANES), F32),
                        pltpu.VMEM((LRU_SLABS, B, LANES), F32),
                        pltpu.VMEM((nc, LRU_SLABS, B * LRU_TC, LANES), BF16),
                        pltpu.VMEM((W, 2 * W), BF16),
                        pltpu.VMEM((1, 2 * W), F32),
                        pltpu.VMEM((1, W), F32)],
        compiler_params=_cparams(2),
        name="rglru",
    )(xr, xr, xr, gr, conv_w.astype(F32), conv_b.reshape(1, W).astype(F32), w_a.astype(F32), w_i.astype(F32),
      b_a.astype(F32), b_i.astype(F32), lam.astype(F32), out_gain.reshape(1, W).astype(F32))


RT_TM = 1024
RT_PARTS = 4
RT_COLS = LANES
RT_ROWS = 48
RINFO = SUBLANES


def _split_bf16(x):
    hi = x.astype(BF16)
    lo = (x - hi.astype(F32)).astype(BF16)
    return hi, lo


def _route_kernel(an_ref, ln_ref, x_ref, wo_ref, g2_ref, wr_ref, br_ref,
                  x1_ref, h2_ref, gt_ref, ei_ref, cnt_ref, wob_ref, wrb_ref, tri_ref, run_ref, runc_ref):
    @pl.when(pl.program_id(0) == 0)
    def _():
        wob_ref[...] = wo_ref[...].astype(BF16)
        hi, lo = _split_bf16(wr_ref[...])
        wrb_ref[:RT_ROWS, :] = hi
        wrb_ref[RT_ROWS:, :] = lo
        r = lax.broadcasted_iota(jnp.int32, (RT_TM, RT_TM), 0)
        cidx = lax.broadcasted_iota(jnp.int32, (RT_TM, RT_TM), 1)
        tri_ref[...] = (r < cidx).astype(BF16)
        run_ref[...] = jnp.zeros_like(run_ref)
        runc_ref[...] = jnp.zeros_like(runc_ref)

    nt_dims = (((1,), (1,)), ((), ()))
    part = RT_TM // RT_PARTS
    x1s = []
    for r in range(RT_PARTS):
        rows = slice(r * part, (r + 1) * part)
        x1 = (x_ref[rows, :]
              + jnp.dot(an_ref[rows, :], wob_ref[:ATTN_WIDTH, :], preferred_element_type=F32)
              + jnp.dot(ln_ref[rows, :], wob_ref[ATTN_WIDTH:, :], preferred_element_type=F32))
        x1_ref[rows, :] = x1
        x1s.append(x1)
    splits = []
    for r, x1 in enumerate(x1s):
        h2 = _rms(x1, g2_ref[...])
        hi = h2.astype(BF16)
        hi_f = hi.astype(F32)
        h2_ref[r * part:(r + 1) * part, :] = _pack_rounded(hi_f)
        splits.append((hi, (h2 - hi_f).astype(BF16)))
    logits = []
    for hi, lo in splits:
        t1 = lax.dot_general(wrb_ref[...], hi, nt_dims, preferred_element_type=F32)
        t2 = lax.dot_general(wrb_ref[:RT_ROWS, :], lo, nt_dims, preferred_element_type=F32)
        logits.append(t1[:RT_ROWS] + t1[RT_ROWS:] + t2)
    logit = jnp.concatenate(logits, axis=1) + br_ref[...]

    sub = lax.broadcasted_iota(jnp.int32, (SUBLANES, RT_TM), 0)
    first_min = lambda hit: jnp.min(jnp.where(hit, sub, SUBLANES), axis=0, keepdims=True)
    is_g = sub < N_GROUPS
    gl = jnp.where(is_g, logit[:SUBLANES], -jnp.inf)
    gm = jnp.max(gl, axis=0, keepdims=True)
    gidx = first_min(gl == gm)
    g_p = 1.0 / jnp.sum(jnp.where(is_g, jnp.exp(logit[:SUBLANES] - gm), 0.0), axis=0, keepdims=True)
    el = logit[SUBLANES:2 * SUBLANES]
    for g in range(1, N_GROUPS):
        el = jnp.where(gidx == g, logit[(g + 1) * SUBLANES:(g + 2) * SUBLANES], el)
    m1 = jnp.max(el, axis=0, keepdims=True)
    i1 = first_min(el == m1)
    el2 = jnp.where(sub == i1, -jnp.inf, el)
    m2 = jnp.max(el2, axis=0, keepdims=True)
    i2 = first_min(el2 == m2)
    t = jnp.exp(m2 - m1)
    gate1 = g_p / (1.0 + t)
    gate2 = g_p * t / (1.0 + t)
    e1 = gidx * EXPERTS_PER_GROUP + i1
    e2 = gidx * EXPERTS_PER_GROUP + i2

    erow = lax.broadcasted_iota(jnp.int32, (N_EXPERTS, RT_TM), 0)
    oh1 = erow == e1
    oh2 = erow == e2
    oh = (oh1 | oh2).astype(F32)
    ohb = oh.astype(BF16)
    cum = jnp.dot(ohb, tri_ref[...], preferred_element_type=F32) + runc_ref[...]
    rank1 = jnp.sum(jnp.where(oh1, cum, 0.0), axis=0, keepdims=True)
    rank2 = jnp.sum(jnp.where(oh2, cum, 0.0), axis=0, keepdims=True)
    runc_ref[...] = runc_ref[...] + jnp.sum(oh, axis=1, keepdims=True)
    tile_cnt = lax.dot_general(jnp.ones((SUBLANES, RT_TM), BF16), ohb, nt_dims, preferred_element_type=F32)
    run_ref[:, :N_EXPERTS] = run_ref[:, :N_EXPERTS] + tile_cnt[0:1]
    cnt_ref[...] = run_ref[...].astype(jnp.int32)

    rows = [e1, e2, rank1.astype(jnp.int32), rank2.astype(jnp.int32)]
    ei = jnp.zeros((RINFO, RT_TM), jnp.int32)
    for k, v in enumerate(rows):
        ei = jnp.where(sub == k, v, ei)
    ei_ref[0] = ei
    gt_ref[...] = jnp.where(sub == 0, gate1, jnp.where(sub == 1, gate2, 0.0)).T


def _out_route(attn_n, lru_n, x2, w_out, ln2, w_group, b_group, w_er, b_er):
    T = x2.shape[0]
    pad_g = SUBLANES - N_GROUPS
    wr = jnp.concatenate([jnp.pad(w_group.T, ((0, pad_g), (0, 0))),
                          jnp.transpose(w_er, (0, 2, 1)).reshape(N_EXPERTS, D_MODEL)], axis=0)
    wr = jnp.pad(wr, ((0, RT_ROWS - wr.shape[0]), (0, 0))).astype(F32)
    br = jnp.concatenate([jnp.pad(b_group, (0, pad_g)), b_er.reshape(-1)])
    br = jnp.pad(br, (0, RT_ROWS - br.shape[0])).reshape(RT_ROWS, 1).astype(F32)
    row = lambda w: pl.BlockSpec((RT_TM, w), lambda i: (i, 0))
    const = lambda shape: pl.BlockSpec(shape, lambda i: (0, 0))
    return pl.pallas_call(
        _route_kernel,
        grid=(T // RT_TM,),
        in_specs=[row(ATTN_WIDTH), row(LRU_WIDTH), row(D_MODEL), const((D_MODEL, D_MODEL)), const((1, D_MODEL)),
                  const((RT_ROWS, D_MODEL)), const((RT_ROWS, 1))],
        out_specs=[row(D_MODEL), row(PACKED), row(RINFO),
                   pl.BlockSpec((1, RINFO, RT_TM), lambda i: (i, 0, 0)), const((1, RT_COLS))],
        out_shape=[jax.ShapeDtypeStruct((T, D_MODEL), F32),
                   jax.ShapeDtypeStruct((T, PACKED), U32),
                   jax.ShapeDtypeStruct((T, RINFO), F32),
                   jax.ShapeDtypeStruct((T // RT_TM, RINFO, RT_TM), jnp.int32),
                   jax.ShapeDtypeStruct((1, RT_COLS), jnp.int32)],
        scratch_shapes=[pltpu.VMEM((D_MODEL, D_MODEL), BF16),
                        pltpu.VMEM((2 * RT_ROWS, D_MODEL), BF16),
                        pltpu.VMEM((RT_TM, RT_TM), BF16),
                        pltpu.VMEM((1, RT_COLS), F32),
                        pltpu.VMEM((N_EXPERTS, 1), F32)],
        compiler_params=_cparams(1),
        name="out_route",
    )(attn_n, lru_n, x2, w_out, ln2.reshape(1, D_MODEL).astype(F32), wr, br)


def _moe_cap(T):
    A = T * TOP_K
    return ((A + MOE_BLOCK - 1) // MOE_BLOCK) * MOE_BLOCK + N_EXPERTS * MOE_BLOCK


PAD_BITS = tuple(1 << b for b in reversed(range(3, MOE_BLOCK.bit_length() - 1)))


def _layout_kernel(cnt_ref, ei_ref, dest_ref, pstart, be_ref, nu_ref, ge_ref):
    n_blocks = be_ref.shape[0]

    def lay(e, carry):
        start, blk, grp = carry
        pstart[e] = start
        nb = (cnt_ref[0, e] + MOE_BLOCK - 1) // MOE_BLOCK
        ge_ref[grp] = e

        def fill(k, c):
            be_ref[blk + k] = e
            return c
        lax.fori_loop(0, nb, fill, 0)
        return start + nb * MOE_BLOCK, blk + nb, grp + (nb > 0).astype(jnp.int32)
    _, used, groups = lax.fori_loop(0, N_EXPERTS, lay, (jnp.int32(0), jnp.int32(0), jnp.int32(0)))
    nu_ref[0] = used

    def tail(k, c):
        be_ref[k] = N_EXPERTS - 1
        return c
    lax.fori_loop(used, n_blocks, tail, 0)

    def no_group(k, c):
        ge_ref[k] = -1
        return c
    lax.fori_loop(groups, ge_ref.shape[0], no_group, 0)

    expert = ei_ref[:, 0:TOP_K, :]
    dest = ei_ref[:, TOP_K:2 * TOP_K, :]
    for e in range(N_EXPERTS):
        dest = dest + jnp.where(expert == e, pstart[e], 0)
    dest_ref[...] = dest


def _layout(ei, cnt, n_blocks):
    nt = ei.shape[0]
    smem = pl.BlockSpec(memory_space=pltpu.SMEM)
    vmem = pl.BlockSpec(memory_space=pltpu.VMEM)
    return pl.pallas_call(
        _layout_kernel,
        in_specs=[smem, vmem],
        out_specs=[vmem, smem, smem, smem, smem],
        out_shape=[jax.ShapeDtypeStruct((nt, TOP_K, RT_TM), jnp.int32),
                   jax.ShapeDtypeStruct((N_EXPERTS,), jnp.int32),
                   jax.ShapeDtypeStruct((n_blocks,), jnp.int32),
                   jax.ShapeDtypeStruct((1,), jnp.int32),
                   jax.ShapeDtypeStruct((N_EXPERTS + W_AHEAD,), jnp.int32)],
        name="layout",
    )(cnt, ei)


SC_CHUNK = 64
SC_BUFS = 3
SC_LEAD = SC_BUFS - 1


def _sc_workers():
    info = plsc.get_sparse_core_info()
    return info.num_cores, info.num_subcores


def _sc_ring(n_chunks, read, write):
    for c in range(min(SC_LEAD, n_chunks)):
        for cp in read(c):
            cp.start()
    reclaimed = set()
    for c in range(n_chunks):
        for cp in read(c):
            cp.wait()
        for cp in write(c):
            cp.start()
        nxt = c + SC_LEAD
        if nxt < n_chunks:
            if nxt - SC_BUFS >= 0:
                for cp in write(nxt - SC_BUFS):
                    cp.wait()
                reclaimed.add(nxt - SC_BUFS)
            for cp in read(nxt):
                cp.start()
    for c in range(n_chunks):
        if c not in reclaimed:
            for cp in write(c):
                cp.wait()


def _sc_dispatch(h2p, dest, cap):
    T = h2p.shape[0]
    nc, ns = _sc_workers()
    per_w = T // (nc * ns)
    n_ch = per_w // SC_CHUNK
    nt, _, tm = dest.shape
    assert nt * tm == T and tm % per_w == 0 and per_w % SC_CHUNK == 0
    idx = dest.reshape(nt, TOP_K, tm // per_w, per_w).transpose(0, 2, 1, 3).reshape(nc * ns, TOP_K * n_ch, SC_CHUNK)
    mesh = plsc.VectorSubcoreMesh(core_axis_name="c", subcore_axis_name="s")

    @functools.partial(
        pl.kernel, mesh=mesh,
        out_type=jax.ShapeDtypeStruct((cap, PACKED), U32),
        scratch_types=[pltpu.VMEM((TOP_K * n_ch, SC_CHUNK), jnp.int32),
                       pltpu.VMEM((SC_BUFS, SC_CHUNK, PACKED), U32),
                       pltpu.SemaphoreType.DMA((SC_BUFS,)),
                       pltpu.SemaphoreType.DMA((SC_BUFS,))])
    def scatter(src_hbm, idx_hbm, out_hbm, idx_v, rows_v, rsem, wsem):
        wid = lax.axis_index("s") * nc + lax.axis_index("c")
        base = pl.multiple_of(wid * per_w, per_w)
        pltpu.sync_copy(idx_hbm.at[wid], idx_v)

        def read(c):
            b = c % SC_BUFS
            return [pltpu.make_async_copy(src_hbm.at[pl.ds(base + c * SC_CHUNK, SC_CHUNK)], rows_v.at[b], rsem.at[b])]

        def write(c):
            b = c % SC_BUFS
            return [pltpu.make_async_copy(rows_v.at[b], out_hbm.at[idx_v.at[k * n_ch + c]], wsem.at[b])
                    for k in range(TOP_K)]
        _sc_ring(n_ch, read, write)

    return scatter(h2p, idx)


def _sc_gather(yb, dest):
    nt, _, tm = dest.shape
    nc, ns = _sc_workers()
    n_rows = nt * TOP_K * tm
    per_w = n_rows // (nc * ns)
    n_ch = per_w // SC_CHUNK
    assert per_w * nc * ns == n_rows and per_w % SC_CHUNK == 0
    mesh = plsc.VectorSubcoreMesh(core_axis_name="c", subcore_axis_name="s")

    @functools.partial(
        pl.kernel, mesh=mesh,
        out_type=jax.ShapeDtypeStruct((n_rows, PACKED), U32),
        scratch_types=[pltpu.VMEM((per_w,), jnp.int32),
                       pltpu.VMEM((SC_BUFS, SC_CHUNK, PACKED), U32),
                       pltpu.SemaphoreType.DMA((SC_BUFS,)),
                       pltpu.SemaphoreType.DMA((SC_BUFS,))])
    def gather(table_hbm, idx_hbm, out_hbm, idx_v, rows_v, gsem, wsem):
        wid = lax.axis_index("s") * nc + lax.axis_index("c")
        base = pl.multiple_of(wid * per_w, per_w)
        pltpu.sync_copy(idx_hbm.at[pl.ds(base, per_w)], idx_v)

        def read(c):
            b = c % SC_BUFS
            return [pltpu.make_async_copy(table_hbm.at[idx_v.at[pl.ds(c * SC_CHUNK, SC_CHUNK)]], rows_v.at[b], gsem.at[b])]

        def write(c):
            b = c % SC_BUFS
            return [pltpu.make_async_copy(rows_v.at[b], out_hbm.at[pl.ds(base + c * SC_CHUNK, SC_CHUNK)], wsem.at[b])]
        _sc_ring(n_ch, read, write)

    return gather(yb, dest.reshape(n_rows)).reshape(nt, TOP_K, tm, PACKED)


def _padfill_kernel(cnt_ref, pstart, xs_in, xs_ref, zeros, zsem):
    del xs_in

    def pad_copies(fn):
        for e in range(N_EXPERTS):
            cnt = cnt_ref[0, e]
            head = (-cnt) & (SUBLANES - 1)
            rest = ((-cnt) & (MOE_BLOCK - 1)) - head
            off = pstart[e] + cnt
            for k in range(SUBLANES - 1):
                @pl.when(k < head)
                def _(off=off, k=k):
                    fn(pltpu.make_async_copy(zeros.at[pl.ds(0, 1), :], xs_ref.at[pl.ds(off + k, 1), :], zsem))
            off = off + head
            for bit in PAD_BITS:
                @pl.when((rest & bit) != 0)
                def _(off=off, bit=bit):
                    fn(pltpu.make_async_copy(zeros.at[pl.ds(0, bit), :],
                                             xs_ref.at[pl.ds(pl.multiple_of(off, SUBLANES), bit), :], zsem))
                off = off + (rest & bit)

    zeros[...] = jnp.zeros_like(zeros)
    pad_copies(lambda cp: cp.start())
    pad_copies(lambda cp: cp.wait())


def _padfill(xs, pstart, cnt):
    smem = pl.BlockSpec(memory_space=pltpu.SMEM)
    hbm = pl.BlockSpec(memory_space=pl.ANY)
    return pl.pallas_call(
        _padfill_kernel,
        in_specs=[smem, smem, hbm],
        out_specs=hbm,
        out_shape=jax.ShapeDtypeStruct(xs.shape, xs.dtype),
        input_output_aliases={2: 0},
        scratch_shapes=[pltpu.VMEM((MOE_BLOCK // 2, PACKED), U32), pltpu.SemaphoreType.DMA(())],
        name="padfill",
    )(cnt, pstart, xs)


W_SLOTS = 3
W_AHEAD = W_SLOTS - 1
EXPERT_GROUP = 16
EXPERT_RUNS = (1, 2, 4, 8)


def _expert_kernel(be_ref, nu_ref, ge_ref, x_ref, wg_hbm, wu_hbm, wd_hbm, o_ref,
                   wgf, wuf, wdf, wgb, wub, wdb, grp_ref, sems):
    step = pl.program_id(0)

    def weight_copies(e, slot):
        return (pltpu.make_async_copy(wg_hbm.at[e], wgf.at[slot], sems.at[slot, 0]),
                pltpu.make_async_copy(wu_hbm.at[e], wuf.at[slot], sems.at[slot, 1]),
                pltpu.make_async_copy(wd_hbm.at[e], wdf.at[slot], sems.at[slot, 2]))

    @pl.when(step == 0)
    def _():
        grp_ref[0] = 0
        for a in range(W_AHEAD):
            @pl.when(ge_ref[a] >= 0)
            def _(a=a):
                for cp in weight_copies(ge_ref[a], a):
                    cp.start()

    n_blocks = be_ref.shape[0]
    n_used = nu_ref[0]

    def swiglu(s, n):
        rows = pl.ds(pl.multiple_of(s * MOE_BLOCK, MOE_BLOCK), n * MOE_BLOCK)
        lo, hi = _unpack_rows(x_ref[rows, :])
        lo = lo.astype(BF16)
        hi = hi.astype(BF16)
        g = (jnp.dot(lo, wgb[:PACKED, :], preferred_element_type=F32)
             + jnp.dot(hi, wgb[PACKED:, :], preferred_element_type=F32))
        u = (jnp.dot(lo, wub[:PACKED, :], preferred_element_type=F32)
             + jnp.dot(hi, wub[PACKED:, :], preferred_element_type=F32))
        h = (g * _sigmoid(g) * u).astype(BF16)
        o_ref[rows, :] = _pack_rows(jnp.dot(h, wdb[...], preferred_element_type=F32))

    def run(s):
        j = step * EXPERT_GROUP + s
        e = be_ref[j]
        first = jnp.logical_or(j == 0, e != be_ref[jnp.maximum(j - 1, 0)])

        @pl.when(first)
        def _():
            grp = grp_ref[0]
            slot = grp % W_SLOTS
            for cp in weight_copies(e, slot):
                cp.wait()
            wgb[...] = wgf[slot].astype(BF16)
            wub[...] = wuf[slot].astype(BF16)
            wdb[...] = wdf[slot].astype(BF16)
            nxt = ge_ref[grp + W_AHEAD]

            @pl.when(nxt >= 0)
            def _():
                for cp in weight_copies(nxt, (grp + W_AHEAD) % W_SLOTS):
                    cp.start()
            grp_ref[0] = grp + 1

        def same(k):
            return (s + k < EXPERT_GROUP) & (j + k < n_used) & (be_ref[jnp.minimum(j + k, n_blocks - 1)] == e)
        take = jnp.int32(1)
        for n in EXPERT_RUNS[1:]:
            ok = same(n - 1)
            for k in range(1, n - 1):
                ok = ok & same(k)
            take = jnp.where(ok, n, take)
        for n in EXPERT_RUNS:
            @pl.when(take == n)
            def _(n=n):
                swiglu(s, n)
        return s + take

    lax.while_loop(lambda s: (s < EXPERT_GROUP) & (step * EXPERT_GROUP + s < n_used), run, jnp.int32(0))


def _experts(xs, block_expert, n_used, group_expert, w_gate, w_up, w_down):
    cap = xs.shape[0]
    n_blocks = cap // MOE_BLOCK
    assert n_blocks % EXPERT_GROUP == 0
    rows = EXPERT_GROUP * MOE_BLOCK
    last = lambda j, be, nu, ge: jnp.minimum(j, (nu[0] - 1) // EXPERT_GROUP)
    hbm = pl.BlockSpec(memory_space=pl.ANY)
    gs = pltpu.PrefetchScalarGridSpec(
        num_scalar_prefetch=3,
        grid=(n_blocks // EXPERT_GROUP,),
        in_specs=[pl.BlockSpec((rows, PACKED), lambda j, be, nu, ge: (last(j, be, nu, ge), 0)), hbm, hbm, hbm],
        out_specs=pl.BlockSpec((rows, PACKED), lambda j, be, nu, ge: (last(j, be, nu, ge), 0)),
        scratch_shapes=[pltpu.VMEM((W_SLOTS, D_MODEL, D_EXPERT), F32),
                        pltpu.VMEM((W_SLOTS, D_MODEL, D_EXPERT), F32),
                        pltpu.VMEM((W_SLOTS, D_EXPERT, D_MODEL), F32),
                        pltpu.VMEM((D_MODEL, D_EXPERT), BF16),
                        pltpu.VMEM((D_MODEL, D_EXPERT), BF16),
                        pltpu.VMEM((D_EXPERT, D_MODEL), BF16),
                        pltpu.SMEM((1,), jnp.int32),
                        pltpu.SemaphoreType.DMA((W_SLOTS, 3))],
    )
    return pl.pallas_call(
        _expert_kernel,
        grid_spec=gs,
        out_shape=jax.ShapeDtypeStruct((cap, PACKED), U32),
        compiler_params=_cparams(1),
        name="experts",
    )(block_expert, n_used, group_expert, xs, w_gate, w_up, w_down)


CB_TM = RT_TM


def _combine_kernel(x1_ref, gt_ref, y2_ref, o_ref):
    g = gt_ref[...]
    lo1, hi1 = _unpack_rows(y2_ref[0, 0])
    lo2, hi2 = _unpack_rows(y2_ref[0, 1])
    o_ref[:, :PACKED] = x1_ref[:, :PACKED] + g[:, 0:1] * lo1 + g[:, 1:2] * lo2
    o_ref[:, PACKED:] = x1_ref[:, PACKED:] + g[:, 0:1] * hi1 + g[:, 1:2] * hi2


def _combine(x1, gates, y2):
    T = x1.shape[0]
    nt = T // CB_TM
    return pl.pallas_call(
        _combine_kernel,
        grid=(nt,),
        in_specs=[pl.BlockSpec((CB_TM, D_MODEL), lambda i: (i, 0)),
                  pl.BlockSpec((CB_TM, RINFO), lambda i: (i, 0)),
                  pl.BlockSpec((1, TOP_K, CB_TM, PACKED), lambda i: (i, 0, 0, 0))],
        out_specs=pl.BlockSpec((CB_TM, D_MODEL), lambda i: (i, 0)),
        out_shape=jax.ShapeDtypeStruct((T, D_MODEL), F32),
        compiler_params=_cparams(1),
        name="combine",
    )(x1, gates, y2)


def _layer(x, rel_bias, ln1, w_in, q_norm, k_norm, attn_sink, conv_w, conv_b, lru_wa, lru_ba, lru_wi, lru_bi,
           lru_lambda, out_norm_attn, out_norm_lru, w_out, ln2, w_group, b_group, w_er, b_er, w_gate, w_up, w_down):
    B, S, D = x.shape
    T = B * S
    x2 = x.reshape(T, D)
    q, kv, xr, gr = _in_proj(x2, ln1, w_in, q_norm, k_norm)
    attn_n = _attention(q.reshape(B, S, ATTN_WIDTH), kv.reshape(B, S, 2 * KV_WIDTH), rel_bias, attn_sink,
                        out_norm_attn)
    lru_n = _rglru(xr.reshape(B, S, LRU_WIDTH), gr.reshape(B, S, LRU_WIDTH), conv_w, conv_b,
                   lru_wa, lru_ba, lru_wi, lru_bi, lru_lambda, out_norm_lru)
    x1, h2, gates, ei, cnt = _out_route(attn_n.reshape(T, ATTN_WIDTH), lru_n.reshape(T, LRU_WIDTH), x2, w_out, ln2,
                                        w_group, b_group, w_er, b_er)
    cap = _moe_cap(T)
    dest, pstart, block_expert, n_used, group_expert = _layout(ei, cnt, cap // MOE_BLOCK)
    xs = _padfill(_sc_dispatch(h2, dest, cap), pstart, cnt)
    yb = _experts(xs, block_expert, n_used, group_expert, w_gate, w_up, w_down)
    out = _combine(x1, gates, _sc_gather(yb, dest))
    return out.reshape(B, S, D)


def kernel(x, rel_bias, ln1, w_in, q_norm, k_norm, attn_sink, conv_w, conv_b, lru_wa, lru_ba, lru_wi, lru_bi,
           lru_lambda, out_norm_attn, out_norm_lru, w_out, ln2, w_group, b_group, w_expert_router, b_expert_router,
           w_gate, w_up, w_down):
    depth = ln1.shape[0]
    for l in range(depth):
        x = _layer(x, rel_bias, ln1[l], w_in[l], q_norm[l], k_norm[l], attn_sink[l], conv_w[l], conv_b[l],
                   lru_wa[l], lru_ba[l], lru_wi[l], lru_bi[l], lru_lambda[l], out_norm_attn[l], out_norm_lru[l],
                   w_out[l], ln2[l], w_group[l], b_group[l], w_expert_router[l], b_expert_router[l],
                   w_gate[l], w_up[l], w_down[l])
    return x
```

```python
import functools
import math

import jax
import jax.numpy as jnp
import numpy as np
from jax import lax
from jax.experimental import pallas as pl
from jax.experimental.pallas import tpu as pltpu
from jax.experimental.pallas import tpu_sc as plsc

D_MODEL = 1024
N_HEADS = 8
N_KV_HEADS = 2
HEAD_DIM = 64
Q_PER_KV = N_HEADS // N_KV_HEADS
ATTN_WIDTH = N_HEADS * HEAD_DIM
KV_WIDTH = N_KV_HEADS * HEAD_DIM
WINDOW = 128
BLOCK = 128
NUM_BUCKETS = 32
MAX_DISTANCE = 128
LRU_WIDTH = D_MODEL - ATTN_WIDTH
LRU_BLOCKS = 8
LRU_BLOCK_DIM = LRU_WIDTH // LRU_BLOCKS
LRU_C = 8.0
CONV_W = 4
CONV_LEFT = 2
N_GROUPS = 4
EXPERTS_PER_GROUP = 8
N_EXPERTS = N_GROUPS * EXPERTS_PER_GROUP
TOP_K = 2
D_EXPERT = 512
MOE_BLOCK = 128
EPS = 1e-6
NEG_INF = -1e30

LANES = 128
SUBLANES = 8
VMEM_LIMIT = 56 * 1024 * 1024
LRU_VMEM_LIMIT = 62 * 1024 * 1024

F32 = jnp.float32
BF16 = jnp.bfloat16
LOG2E = math.log2(math.e)


def _cparams(n_axes, vmem=VMEM_LIMIT):
    return pltpu.CompilerParams(dimension_semantics=("arbitrary",) * n_axes, vmem_limit_bytes=vmem)


def _rms(x, gain):
    return x * lax.rsqrt(jnp.mean(x * x, axis=-1, keepdims=True) + EPS) * gain


U32 = jnp.uint32
HI_MASK = 0xFFFF0000
PACKED = D_MODEL // 2


def _pack_rows(x):
    return _pack_rounded(x.astype(BF16).astype(F32))


def _pack_rounded(xb):
    h = xb.shape[1] // 2
    lo = lax.bitcast_convert_type(xb[:, :h], U32) >> 16
    hi = lax.bitcast_convert_type(xb[:, h:], U32) & jnp.uint32(HI_MASK)
    return lo | hi


def _unpack_rows(p):
    lo = lax.bitcast_convert_type(p << 16, F32)
    hi = lax.bitcast_convert_type(p & jnp.uint32(HI_MASK), F32)
    return lo, hi


IN_TM = 1024


def _head_rms(x, n_heads, gain):
    head = lax.broadcasted_iota(jnp.int32, (1, n_heads * HEAD_DIM), 1) // HEAD_DIM
    x2 = x * x
    scale = jnp.zeros_like(x)
    for h in range(n_heads):
        ms = jnp.sum(jnp.where(head == h, x2, 0.0), axis=-1, keepdims=True) * (1.0 / HEAD_DIM)
        scale = jnp.where(head == h, lax.rsqrt(ms + EPS), scale)
    return x * scale * gain


def _in_proj_kernel(x_ref, g_ref, w_ref, qg_ref, kg_ref, q_ref, kv_ref, xr_ref, gr_ref, wb_ref):
    @pl.when(pl.program_id(0) == 0)
    def _():
        wb_ref[...] = w_ref[...].astype(BF16)

    h = _rms(x_ref[...], g_ref[...]).astype(BF16)
    c_k = ATTN_WIDTH
    c_v = c_k + KV_WIDTH
    c_x = c_v + KV_WIDTH
    c_g = c_x + LRU_WIDTH
    q = jnp.dot(h, wb_ref[:, :c_k], preferred_element_type=F32)
    q_ref[...] = _head_rms(q, N_HEADS, qg_ref[...]).astype(BF16)
    k = jnp.dot(h, wb_ref[:, c_k:c_v], preferred_element_type=F32)
    kv_ref[:, :KV_WIDTH] = _head_rms(k, N_KV_HEADS, kg_ref[...]).astype(BF16)
    kv_ref[:, KV_WIDTH:] = jnp.dot(h, wb_ref[:, c_v:c_x], preferred_element_type=F32).astype(BF16)
    xr_ref[...] = jnp.dot(h, wb_ref[:, c_x:c_g], preferred_element_type=F32)
    gr_ref[...] = jnp.dot(h, wb_ref[:, c_g:], preferred_element_type=F32)


def _in_proj(x2, ln1, w_in, q_gain, k_gain):
    T = x2.shape[0]
    n_in = w_in.shape[1]
    row = lambda w: pl.BlockSpec((IN_TM, w), lambda i: (i, 0))
    qg = (jnp.tile(q_gain.astype(F32), N_HEADS) * (HEAD_DIM ** -0.5 * LOG2E)).reshape(1, ATTN_WIDTH)
    kg = jnp.tile(k_gain.astype(F32), N_KV_HEADS).reshape(1, KV_WIDTH)
    return pl.pallas_call(
        _in_proj_kernel,
        grid=(T // IN_TM,),
        in_specs=[row(D_MODEL),
                  pl.BlockSpec((1, D_MODEL), lambda i: (0, 0)),
                  pl.BlockSpec((D_MODEL, n_in), lambda i: (0, 0)),
                  pl.BlockSpec((1, ATTN_WIDTH), lambda i: (0, 0)),
                  pl.BlockSpec((1, KV_WIDTH), lambda i: (0, 0))],
        out_specs=[row(ATTN_WIDTH), row(2 * KV_WIDTH), row(LRU_WIDTH), row(LRU_WIDTH)],
        out_shape=[jax.ShapeDtypeStruct((T, ATTN_WIDTH), BF16),
                   jax.ShapeDtypeStruct((T, 2 * KV_WIDTH), BF16),
                   jax.ShapeDtypeStruct((T, LRU_WIDTH), F32),
                   jax.ShapeDtypeStruct((T, LRU_WIDTH), F32)],
        scratch_shapes=[pltpu.VMEM((D_MODEL, n_in), BF16)],
        compiler_params=_cparams(1),
        name="in_proj",
    )(x2, ln1.reshape(1, D_MODEL), w_in, qg, kg)


def _t5_bucket(rel):
    half = NUM_BUCKETS // 2
    max_exact = half // 2
    base = jnp.where(rel > 0, half, 0)
    n = jnp.abs(rel)
    nf = jnp.maximum(n, 1).astype(jnp.float32)
    large = max_exact + (jnp.log(nf / max_exact) / math.log(MAX_DISTANCE / max_exact)
                         * (half - max_exact)).astype(jnp.int32)
    large = jnp.minimum(large, half - 1)
    return base + jnp.where(n < max_exact, n, large)


HEAD_PAIRS = Q_PER_KV // 2
EDGE_VARIANTS = 3


def _fill_bias_table(rb_ref, bucket_ref, band_ref, o_ref):
    bucket = bucket_ref[...]
    band = band_ref[...] > 0
    col = lax.broadcasted_iota(jnp.int32, bucket.shape, 1)
    valid = (band & (col >= BLOCK), band, band & (col < 2 * BLOCK))
    for h in range(N_HEADS):
        acc = jnp.zeros(bucket.shape, F32)
        for b in range(NUM_BUCKETS):
            acc = jnp.where(bucket == b, rb_ref[b, h], acc)
        kv, g = divmod(h, Q_PER_KV)
        pair, parity = divmod(g, 2)
        for var in range(EDGE_VARIANTS):
            o_ref[var, kv, parity, pair * BLOCK:(pair + 1) * BLOCK, :] = jnp.where(valid[var], acc * LOG2E, NEG_INF)


def _attn_kernel(sink_ref, rb_ref, q_ref, kp_ref, kc_ref, kn_ref, bucket_ref, band_ref, og_ref, o_ref, bias_ref):
    n = pl.program_id(1)

    @pl.when((pl.program_id(0) == 0) & (n == 0))
    def _():
        _fill_bias_table(rb_ref, bucket_ref, band_ref, bias_ref)

    kv_all = jnp.concatenate([kp_ref[0], kc_ref[0], kn_ref[0]], axis=0)
    for qb in range(ATTN_QB):
        variant = 1
        if qb == 0:
            variant = jnp.where(n == 0, 0, 1)
        if qb == ATTN_QB - 1:
            variant = jnp.where(n == pl.num_programs(1) - 1, 2, variant)
        out = _attn_block(q_ref[0, qb * BLOCK:(qb + 1) * BLOCK, :], kv_all[qb * BLOCK:(qb + 3) * BLOCK, :],
                          lambda kv, parity: bias_ref[variant, kv, parity], sink_ref)
        o_ref[0, qb * BLOCK:(qb + 1) * BLOCK, :] = _rms(out, og_ref[...]).astype(o_ref.dtype)


def _attn_block(q, kvw, bias, sink_ref):
    low = lax.broadcasted_iota(jnp.int32, (1, LANES), 1) < HEAD_DIM
    swap = lambda slab: pltpu.roll(slab.astype(F32), HEAD_DIM, 1).astype(BF16)
    kslab, vslab = kvw[:, :KV_WIDTH], kvw[:, KV_WIDTH:]
    kslab_sw, vslab_sw = swap(kslab), swap(vslab)
    rowi = lax.broadcasted_iota(jnp.int32, (HEAD_PAIRS * BLOCK, 1), 0)
    combos = [(kv, parity) for kv in range(N_KV_HEADS) for parity in range(2)]
    scores, vzs, sinks = [], [], []
    for kv, parity in combos:
        ks, vs = (kslab, vslab) if (kv == 0) == (parity == 0) else (kslab_sw, vslab_sw)
        keep = low if parity == 0 else jnp.logical_not(low)
        kz = jnp.where(keep, ks, jnp.zeros_like(ks))
        vzs.append(jnp.where(keep, vs, jnp.zeros_like(vs)))
        base = kv * Q_PER_KV * HEAD_DIM
        qpair = jnp.concatenate([q[:, base + j * LANES:base + (j + 1) * LANES] for j in range(HEAD_PAIRS)], axis=0)
        s = lax.dot_general(qpair, kz, (((1,), (1,)), ((), ())), preferred_element_type=F32)
        scores.append(s + bias(kv, parity))
        sink = jnp.zeros((HEAD_PAIRS * BLOCK, 1), F32)
        for j in range(HEAD_PAIRS):
            sink = jnp.where(rowi // BLOCK == j, sink_ref[kv * Q_PER_KV + 2 * j + parity], sink)
        sinks.append(sink)
    probs, inv = [], []
    for s, sink in zip(scores, sinks):
        m = jnp.maximum(jnp.max(s, axis=-1, keepdims=True), sink)
        p = jnp.exp2(s - m)
        inv.append(1.0 / (jnp.sum(p, axis=-1, keepdims=True) + jnp.exp2(sink - m)))
        probs.append(p.astype(BF16))
    outs = [jnp.dot(p, vz, preferred_element_type=F32) * r for p, vz, r in zip(probs, vzs, inv)]
    cols = []
    for kv in range(N_KV_HEADS):
        acc = outs[2 * kv] + outs[2 * kv + 1]
        cols += [acc[j * BLOCK:(j + 1) * BLOCK, :] for j in range(HEAD_PAIRS)]
    return jnp.concatenate(cols, axis=1)


ATTN_QB = 4


def _attention(q, kv, rel_bias, sink, out_gain):
    B, S, _ = q.shape
    nb = S // BLOCK
    assert ATTN_QB >= 2 and nb % ATTN_QB == 0, "a step's first and last query blocks must be distinct"
    ns = nb // ATTN_QB
    rows = ATTN_QB * BLOCK
    qi = jnp.arange(BLOCK, dtype=jnp.int32)
    kj = jnp.arange(3 * BLOCK, dtype=jnp.int32)
    rel = kj[None, :] - BLOCK - qi[:, None]
    bucket = _t5_bucket(rel).astype(jnp.int32)
    band = (jnp.abs(rel) <= WINDOW).astype(jnp.int32)
    kvspec = lambda f: pl.BlockSpec((1, BLOCK, 2 * KV_WIDTH), f)
    smem = pl.BlockSpec(memory_space=pltpu.SMEM)
    geom = pl.BlockSpec((BLOCK, 3 * BLOCK), lambda b, n: (0, 0))
    return pl.pallas_call(
        _attn_kernel,
        grid=(B, ns),
        in_specs=[smem, smem,
                  pl.BlockSpec((1, rows, ATTN_WIDTH), lambda b, n: (b, n, 0)),
                  kvspec(lambda b, n: (b, jnp.maximum(n * ATTN_QB - 1, 0), 0)),
                  pl.BlockSpec((1, rows, 2 * KV_WIDTH), lambda b, n: (b, n, 0)),
                  kvspec(lambda b, n: (b, jnp.minimum((n + 1) * ATTN_QB, nb - 1), 0)),
                  geom, geom,
                  pl.BlockSpec((1, ATTN_WIDTH), lambda b, n: (0, 0))],
        out_specs=pl.BlockSpec((1, rows, ATTN_WIDTH), lambda b, n: (b, n, 0)),
        out_shape=jax.ShapeDtypeStruct((B, S, ATTN_WIDTH), BF16),
        scratch_shapes=[pltpu.VMEM((EDGE_VARIANTS, N_KV_HEADS, 2, HEAD_PAIRS * BLOCK, 3 * BLOCK), F32)],
        compiler_params=_cparams(2),
        name="attention",
    )(sink.astype(F32) * LOG2E, rel_bias.astype(F32), q, kv, kv, kv, bucket, band, out_gain.reshape(1, ATTN_WIDTH))


LRU_TC = 128
LRU_PITCH = LRU_TC + SUBLANES
LRU_SLABS = LRU_WIDTH // LANES
HALO = SUBLANES


def _softplus(x):
    return jnp.maximum(x, 0.0) + jnp.log(1.0 + jnp.exp(-jnp.abs(x)))


def _gelu_tanh(x):
    k = math.sqrt(2.0 / math.pi)
    hx = 0.5 * x
    return hx + hx * jnp.tanh(x * (k + (k * 0.044715) * (x * x)))


def _sigmoid(x):
    return 0.5 + 0.5 * jnp.tanh(0.5 * x)


def _rglru_kernel(xr_ref, xp_ref, xn_ref, gr_ref, cw_ref, cb_ref, wa_ref, wi_ref, ba_ref, bi_ref, lam_ref, og_ref,
                  o_ref, sx_ref, a_ref, u_ref, h_ref, carry_ref, hf_ref, xcs_ref, wg_ref, bg_ref, k_ref):
    p = pl.program_id(0)
    i = pl.program_id(1)
    nc = pl.num_programs(1)
    c = i + p * (nc - 1 - 2 * i)
    B = xr_ref.shape[0]
    TC = LRU_TC

    @pl.when(i == 0)
    def _():
        carry_ref[...] = jnp.zeros_like(carry_ref)
        wg_ref[...] = jnp.zeros_like(wg_ref)
        for sel, w_ref in enumerate((wa_ref, wi_ref)):
            for h in range(LRU_BLOCKS):
                lo = h * LRU_BLOCK_DIM
                wg_ref[lo:lo + LRU_BLOCK_DIM, sel * LRU_WIDTH + lo:sel * LRU_WIDTH + lo + LRU_BLOCK_DIM] = (
                    0.5 * w_ref[0, h]).astype(BF16)
        row = pl.ds(p, 1)
        bg_ref[:, :LRU_WIDTH] = 0.5 * ba_ref[row, :]
        bg_ref[:, LRU_WIDTH:] = 0.5 * bi_ref[row, :]
        k_ref[...] = (-0.5 * LRU_C * math.log2(math.e)) * _softplus(-lam_ref[row, :])

    def gates_and_scan(xc2, backward):
        g = jnp.dot(xc2.astype(BF16), wg_ref[...], preferred_element_type=F32) + bg_ref[...]
        ta = jnp.tanh(g[:, :LRU_WIDTH])
        ig = 0.5 + 0.5 * jnp.tanh(g[:, LRU_WIDTH:])
        a = jnp.exp2((1.0 + ta) * k_ref[...])
        z = 1.0 - a * a
        u = z * lax.rsqrt(jnp.maximum(z, 1e-30)) * ig * xc2
        for b in range(B):
            for s in range(LRU_SLABS):
                a_ref[s, b * LRU_PITCH:b * LRU_PITCH + TC, :] = a[b * TC:(b + 1) * TC, s * LANES:(s + 1) * LANES]
                u_ref[s, b * LRU_PITCH:b * LRU_PITCH + TC, :] = u[b * TC:(b + 1) * TC, s * LANES:(s + 1) * LANES]

        def step(k, hs):
            t = TC - 1 - k if backward else k
            out = []
            for s in range(LRU_SLABS):
                idx = pl.ds(t, B, stride=LRU_PITCH)
                hn = a_ref[s, idx, :] * hs[s] + u_ref[s, idx, :]
                h_ref[s, idx, :] = hn
                out.append(hn)
            return tuple(out)

        hs = lax.fori_loop(0, TC, step, tuple(carry_ref[s] for s in range(LRU_SLABS)), unroll=8)
        for s in range(LRU_SLABS):
            carry_ref[s] = hs[s]

    @pl.when(p == 0)
    def _():
        sx_ref[:, HALO:HALO + TC, :] = xr_ref[...]
        sx_ref[:, 0:HALO, :] = jnp.where(c > 0, xp_ref[...], 0.0)
        sx_ref[:, HALO + TC:, :] = jnp.where(c < nc - 1, xn_ref[...], 0.0)
        xc = cb_ref[...][None]
        for j in range(CONV_W):
            off = HALO + j - CONV_LEFT
            xc = xc + cw_ref[j:j + 1, :][None] * sx_ref[:, off:off + TC, :]
        xc2 = xc.reshape(B * TC, LRU_WIDTH)
        xcs_ref[c] = xc2.astype(xcs_ref.dtype)
        gates_and_scan(xc2, backward=False)
        for b in range(B):
            for s in range(LRU_SLABS):
                hf_ref[c, s, b * TC:(b + 1) * TC, :] = h_ref[s, b * LRU_PITCH:b * LRU_PITCH + TC, :].astype(hf_ref.dtype)

    @pl.when(p == 1)
    def _():
        gates_and_scan(xcs_ref[c].astype(F32), backward=True)
        for b in range(B):
            hsum = jnp.concatenate(
                [h_ref[s, b * LRU_PITCH:b * LRU_PITCH + TC, :] + hf_ref[c, s, b * TC:(b + 1) * TC, :].astype(F32)
                 for s in range(LRU_SLABS)], axis=1)
            y = hsum * _gelu_tanh(gr_ref[b])
            o_ref[b] = _rms(y, og_ref[...]).astype(o_ref.dtype)


def _rglru(xr, gr, conv_w, conv_b, w_a, b_a, w_i, b_i, lam, out_gain):
    B, S, W = xr.shape
    nc = S // LRU_TC
    hb = LRU_TC // HALO
    fwd = lambda p, i: jnp.where(p == 0, i, nc - 1)
    bwd = lambda p, i: nc - 1 - p * i
    full2 = lambda shape: pl.BlockSpec(shape, lambda p, i: (0,) * len(shape))
    wblock = pl.BlockSpec((1, LRU_BLOCKS, LRU_BLOCK_DIM, LRU_BLOCK_DIM), lambda p, i: (p, 0, 0, 0))
    return pl.pallas_call(
        _rglru_kernel,
        grid=(2, nc),
        in_specs=[pl.BlockSpec((B, LRU_TC, W), lambda p, i: (0, fwd(p, i), 0)),
                  pl.BlockSpec((B, HALO, W), lambda p, i: (0, jnp.maximum(fwd(p, i) * hb - 1, 0), 0)),
                  pl.BlockSpec((B, HALO, W), lambda p, i: (0, jnp.minimum((fwd(p, i) + 1) * hb, S // HALO - 1), 0)),
                  pl.BlockSpec((B, LRU_TC, W), lambda p, i: (0, bwd(p, i), 0)),
                  full2((CONV_W, W)),
                  full2((1, W)),
                  wblock, wblock,
                  full2((2, W)), full2((2, W)), full2((2, W)),
                  full2((1, W))],
        out_specs=pl.BlockSpec((B, LRU_TC, W), lambda p, i: (0, bwd(p, i), 0)),
        out_shape=jax.ShapeDtypeStruct((B, S, W), BF16),
        scratch_shapes=[pltpu.VMEM((B, LRU_TC + 2 * HALO, W), F32),
                        pltpu.VMEM((LRU_SLABS, B * LRU_PITCH, LANES), F32),
                        pltpu.VMEM((LRU_SLABS, B * LRU_PITCH, LANES), F32),
                        pltpu.VMEM((LRU_SLABS, B * LRU_PITCH, LANES), F32),
                        pltpu.VMEM((LRU_SLABS, B, LANES), F32),
                        pltpu.VMEM((nc, LRU_SLABS, B * LRU_TC, LANES), BF16),
                        pltpu.VMEM((nc, B * LRU_TC, W), BF16),
                        pltpu.VMEM((W, 2 * W), BF16),
                        pltpu.VMEM((1, 2 * W), F32),
                        pltpu.VMEM((1, W), F32)],
        compiler_params=_cparams(2, LRU_VMEM_LIMIT),
        name="rglru",
    )(xr, xr, xr, gr, conv_w.astype(F32), conv_b.reshape(1, W).astype(F32), w_a.astype(F32), w_i.astype(F32),
      b_a.astype(F32), b_i.astype(F32), lam.astype(F32), out_gain.reshape(1, W).astype(F32))


RT_TM = 1024
RT_PARTS = 4
RT_COLS = LANES
RT_ROWS = 48
RINFO = SUBLANES


def _split_bf16(x):
    hi = x.astype(BF16)
    lo = (x - hi.astype(F32)).astype(BF16)
    return hi, lo


def _route_kernel(an_ref, ln_ref, x_ref, wo_ref, g2_ref, wr_ref, br_ref,
                  x1_ref, h2_ref, gt_ref, ei_ref, cnt_ref, wob_ref, wrb_ref, tri_ref, run_ref, runc_ref):
    @pl.when(pl.program_id(0) == 0)
    def _():
        wob_ref[...] = wo_ref[...].astype(BF16)
        hi, lo = _split_bf16(wr_ref[...])
        wrb_ref[:RT_ROWS, :] = hi
        wrb_ref[RT_ROWS:, :] = lo
        r = lax.broadcasted_iota(jnp.int32, (RT_TM, RT_TM), 0)
        cidx = lax.broadcasted_iota(jnp.int32, (RT_TM, RT_TM), 1)
        tri_ref[...] = (r < cidx).astype(BF16)
        run_ref[...] = jnp.zeros_like(run_ref)
        runc_ref[...] = jnp.zeros_like(runc_ref)

    nt_dims = (((1,), (1,)), ((), ()))
    part = RT_TM // RT_PARTS
    x1s = []
    for r in range(RT_PARTS):
        rows = slice(r * part, (r + 1) * part)
        x1 = (x_ref[rows, :]
              + jnp.dot(an_ref[rows, :], wob_ref[:ATTN_WIDTH, :], preferred_element_type=F32)
              + jnp.dot(ln_ref[rows, :], wob_ref[ATTN_WIDTH:, :], preferred_element_type=F32))
        x1_ref[rows, :] = x1
        x1s.append(x1)
    splits = []
    for r, x1 in enumerate(x1s):
        h2 = _rms(x1, g2_ref[...])
        hi = h2.astype(BF16)
        hi_f = hi.astype(F32)
        h2_ref[r * part:(r + 1) * part, :] = _pack_rounded(hi_f)
        splits.append((hi, (h2 - hi_f).astype(BF16)))
    logits = []
    for hi, lo in splits:
        t1 = lax.dot_general(wrb_ref[...], hi, nt_dims, preferred_element_type=F32)
        t2 = lax.dot_general(wrb_ref[:RT_ROWS, :], lo, nt_dims, preferred_element_type=F32)
        logits.append(t1[:RT_ROWS] + t1[RT_ROWS:] + t2)
    logit = jnp.concatenate(logits, axis=1) + br_ref[...]

    sub = lax.broadcasted_iota(jnp.int32, (SUBLANES, RT_TM), 0)
    first_min = lambda hit: jnp.min(jnp.where(hit, sub, SUBLANES), axis=0, keepdims=True)
    is_g = sub < N_GROUPS
    gl = jnp.where(is_g, logit[:SUBLANES], -jnp.inf)
    gm = jnp.max(gl, axis=0, keepdims=True)
    gidx = first_min(gl == gm)
    g_p = 1.0 / jnp.sum(jnp.where(is_g, jnp.exp(logit[:SUBLANES] - gm), 0.0), axis=0, keepdims=True)
    el = logit[SUBLANES:2 * SUBLANES]
    for g in range(1, N_GROUPS):
        el = jnp.where(gidx == g, logit[(g + 1) * SUBLANES:(g + 2) * SUBLANES], el)
    m1 = jnp.max(el, axis=0, keepdims=True)
    i1 = first_min(el == m1)
    el2 = jnp.where(sub == i1, -jnp.inf, el)
    m2 = jnp.max(el2, axis=0, keepdims=True)
    i2 = first_min(el2 == m2)
    t = jnp.exp(m2 - m1)
    gate1 = g_p / (1.0 + t)
    gate2 = g_p * t / (1.0 + t)
    e1 = gidx * EXPERTS_PER_GROUP + i1
    e2 = gidx * EXPERTS_PER_GROUP + i2

    erow = lax.broadcasted_iota(jnp.int32, (N_EXPERTS, RT_TM), 0)
    oh1 = erow == e1
    oh2 = erow == e2
    oh = (oh1 | oh2).astype(F32)
    ohb = oh.astype(BF16)
    cum = jnp.dot(ohb, tri_ref[...], preferred_element_type=F32) + runc_ref[...]
    rank1 = jnp.sum(jnp.where(oh1, cum, 0.0), axis=0, keepdims=True)
    rank2 = jnp.sum(jnp.where(oh2, cum, 0.0), axis=0, keepdims=True)
    runc_ref[...] = runc_ref[...] + jnp.sum(oh, axis=1, keepdims=True)
    tile_cnt = lax.dot_general(jnp.ones((SUBLANES, RT_TM), BF16), ohb, nt_dims, preferred_element_type=F32)
    run_ref[:, :N_EXPERTS] = run_ref[:, :N_EXPERTS] + tile_cnt[0:1]
    cnt_ref[...] = run_ref[...].astype(jnp.int32)

    rows = [e1, e2, rank1.astype(jnp.int32), rank2.astype(jnp.int32)]
    ei = jnp.zeros((RINFO, RT_TM), jnp.int32)
    for k, v in enumerate(rows):
        ei = jnp.where(sub == k, v, ei)
    ei_ref[0] = ei
    gt_ref[...] = jnp.where(sub == 0, gate1, jnp.where(sub == 1, gate2, 0.0)).T


def _out_route(attn_n, lru_n, x2, w_out, ln2, w_group, b_group, w_er, b_er):
    T = x2.shape[0]
    pad_g = SUBLANES - N_GROUPS
    wr = jnp.concatenate([jnp.pad(w_group.T, ((0, pad_g), (0, 0))),
                          jnp.transpose(w_er, (0, 2, 1)).reshape(N_EXPERTS, D_MODEL)], axis=0)
    wr = jnp.pad(wr, ((0, RT_ROWS - wr.shape[0]), (0, 0))).astype(F32)
    br = jnp.concatenate([jnp.pad(b_group, (0, pad_g)), b_er.reshape(-1)])
    br = jnp.pad(br, (0, RT_ROWS - br.shape[0])).reshape(RT_ROWS, 1).astype(F32)
    row = lambda w: pl.BlockSpec((RT_TM, w), lambda i: (i, 0))
    const = lambda shape: pl.BlockSpec(shape, lambda i: (0, 0))
    return pl.pallas_call(
        _route_kernel,
        grid=(T // RT_TM,),
        in_specs=[row(ATTN_WIDTH), row(LRU_WIDTH), row(D_MODEL), const((D_MODEL, D_MODEL)), const((1, D_MODEL)),
                  const((RT_ROWS, D_MODEL)), const((RT_ROWS, 1))],
        out_specs=[row(D_MODEL), row(PACKED), row(RINFO),
                   pl.BlockSpec((1, RINFO, RT_TM), lambda i: (i, 0, 0)), const((1, RT_COLS))],
        out_shape=[jax.ShapeDtypeStruct((T, D_MODEL), F32),
                   jax.ShapeDtypeStruct((T, PACKED), U32),
                   jax.ShapeDtypeStruct((T, RINFO), F32),
                   jax.ShapeDtypeStruct((T // RT_TM, RINFO, RT_TM), jnp.int32),
                   jax.ShapeDtypeStruct((1, RT_COLS), jnp.int32)],
        scratch_shapes=[pltpu.VMEM((D_MODEL, D_MODEL), BF16),
                        pltpu.VMEM((2 * RT_ROWS, D_MODEL), BF16),
                        pltpu.VMEM((RT_TM, RT_TM), BF16),
                        pltpu.VMEM((1, RT_COLS), F32),
                        pltpu.VMEM((N_EXPERTS, 1), F32)],
        compiler_params=_cparams(1),
        name="out_route",
    )(attn_n, lru_n, x2, w_out, ln2.reshape(1, D_MODEL).astype(F32), wr, br)


def _moe_cap(T):
    A = T * TOP_K
    return ((A + MOE_BLOCK - 1) // MOE_BLOCK) * MOE_BLOCK + N_EXPERTS * MOE_BLOCK


PAD_BITS = tuple(1 << b for b in reversed(range(3, MOE_BLOCK.bit_length() - 1)))


def _layout_kernel(cnt_ref, ei_ref, dest_ref, pstart, be_ref, nu_ref, ge_ref):
    n_blocks = be_ref.shape[0]

    def lay(e, carry):
        start, blk, grp = carry
        pstart[e] = start
        nb = (cnt_ref[0, e] + MOE_BLOCK - 1) // MOE_BLOCK
        ge_ref[grp] = e

        def fill(k, c):
            be_ref[blk + k] = e
            return c
        lax.fori_loop(0, nb, fill, 0)
        return start + nb * MOE_BLOCK, blk + nb, grp + (nb > 0).astype(jnp.int32)
    _, used, groups = lax.fori_loop(0, N_EXPERTS, lay, (jnp.int32(0), jnp.int32(0), jnp.int32(0)))
    nu_ref[0] = used

    def tail(k, c):
        be_ref[k] = N_EXPERTS - 1
        return c
    lax.fori_loop(used, n_blocks, tail, 0)

    def no_group(k, c):
        ge_ref[k] = -1
        return c
    lax.fori_loop(groups, ge_ref.shape[0], no_group, 0)

    expert = ei_ref[:, 0:TOP_K, :]
    dest = ei_ref[:, TOP_K:2 * TOP_K, :]
    for e in range(N_EXPERTS):
        dest = dest + jnp.where(expert == e, pstart[e], 0)
    dest_ref[...] = dest


def _layout(ei, cnt, n_blocks):
    nt = ei.shape[0]
    smem = pl.BlockSpec(memory_space=pltpu.SMEM)
    vmem = pl.BlockSpec(memory_space=pltpu.VMEM)
    return pl.pallas_call(
        _layout_kernel,
        in_specs=[smem, vmem],
        out_specs=[vmem, smem, smem, smem, smem],
        out_shape=[jax.ShapeDtypeStruct((nt, TOP_K, RT_TM), jnp.int32),
                   jax.ShapeDtypeStruct((N_EXPERTS,), jnp.int32),
                   jax.ShapeDtypeStruct((n_blocks,), jnp.int32),
                   jax.ShapeDtypeStruct((1,), jnp.int32),
                   jax.ShapeDtypeStruct((N_EXPERTS + W_AHEAD,), jnp.int32)],
        name="layout",
    )(cnt, ei)


SC_CHUNK = 64
SC_BUFS = 3
SC_LEAD = SC_BUFS - 1


def _sc_workers():
    info = plsc.get_sparse_core_info()
    return info.num_cores, info.num_subcores


def _sc_ring(n_chunks, read, write):
    for c in range(min(SC_LEAD, n_chunks)):
        for cp in read(c):
            cp.start()
    reclaimed = set()
    for c in range(n_chunks):
        for cp in read(c):
            cp.wait()
        for cp in write(c):
            cp.start()
        nxt = c + SC_LEAD
        if nxt < n_chunks:
            if nxt - SC_BUFS >= 0:
                for cp in write(nxt - SC_BUFS):
                    cp.wait()
                reclaimed.add(nxt - SC_BUFS)
            for cp in read(nxt):
                cp.start()
    for c in range(n_chunks):
        if c not in reclaimed:
            for cp in write(c):
                cp.wait()


def _sc_dispatch(h2p, dest, cap):
    T = h2p.shape[0]
    nc, ns = _sc_workers()
    per_w = T // (nc * ns)
    n_ch = per_w // SC_CHUNK
    nt, _, tm = dest.shape
    assert nt * tm == T and tm % per_w == 0 and per_w % SC_CHUNK == 0
    idx = dest.reshape(nt, TOP_K, tm // per_w, per_w).transpose(0, 2, 1, 3).reshape(nc * ns, TOP_K * n_ch, SC_CHUNK)
    mesh = plsc.VectorSubcoreMesh(core_axis_name="c", subcore_axis_name="s")

    @functools.partial(
        pl.kernel, mesh=mesh,
        out_type=jax.ShapeDtypeStruct((cap, PACKED), U32),
        scratch_types=[pltpu.VMEM((TOP_K * n_ch, SC_CHUNK), jnp.int32),
                       pltpu.VMEM((SC_BUFS, SC_CHUNK, PACKED), U32),
                       pltpu.SemaphoreType.DMA((SC_BUFS,)),
                       pltpu.SemaphoreType.DMA((SC_BUFS,))])
    def scatter(src_hbm, idx_hbm, out_hbm, idx_v, rows_v, rsem, wsem):
        wid = lax.axis_index("s") * nc + lax.axis_index("c")
        base = pl.multiple_of(wid * per_w, per_w)
        pltpu.sync_copy(idx_hbm.at[wid], idx_v)

        def read(c):
            b = c % SC_BUFS
            return [pltpu.make_async_copy(src_hbm.at[pl.ds(base + c * SC_CHUNK, SC_CHUNK)], rows_v.at[b], rsem.at[b])]

        def write(c):
            b = c % SC_BUFS
            return [pltpu.make_async_copy(rows_v.at[b], out_hbm.at[idx_v.at[k * n_ch + c]], wsem.at[b])
                    for k in range(TOP_K)]
        _sc_ring(n_ch, read, write)

    return scatter(h2p, idx)


def _sc_gather(yb, dest):
    nt, _, tm = dest.shape
    nc, ns = _sc_workers()
    n_rows = nt * TOP_K * tm
    per_w = n_rows // (nc * ns)
    n_ch = per_w // SC_CHUNK
    assert per_w * nc * ns == n_rows and per_w % SC_CHUNK == 0
    mesh = plsc.VectorSubcoreMesh(core_axis_name="c", subcore_axis_name="s")

    @functools.partial(
        pl.kernel, mesh=mesh,
        out_type=jax.ShapeDtypeStruct((n_rows, PACKED), U32),
        scratch_types=[pltpu.VMEM((per_w,), jnp.int32),
                       pltpu.VMEM((SC_BUFS, SC_CHUNK, PACKED), U32),
                       pltpu.SemaphoreType.DMA((SC_BUFS,)),
                       pltpu.SemaphoreType.DMA((SC_BUFS,))])
    def gather(table_hbm, idx_hbm, out_hbm, idx_v, rows_v, gsem, wsem):
        wid = lax.axis_index("s") * nc + lax.axis_index("c")
        base = pl.multiple_of(wid * per_w, per_w)
        pltpu.sync_copy(idx_hbm.at[pl.ds(base, per_w)], idx_v)

        def read(c):
            b = c % SC_BUFS
            return [pltpu.make_async_copy(table_hbm.at[idx_v.at[pl.ds(c * SC_CHUNK, SC_CHUNK)]], rows_v.at[b], gsem.at[b])]

        def write(c):
            b = c % SC_BUFS
            return [pltpu.make_async_copy(rows_v.at[b], out_hbm.at[pl.ds(base + c * SC_CHUNK, SC_CHUNK)], wsem.at[b])]
        _sc_ring(n_ch, read, write)

    return gather(yb, dest.reshape(n_rows)).reshape(nt, TOP_K, tm, PACKED)


def _padfill_kernel(cnt_ref, pstart, xs_in, xs_ref, zeros, zsem):
    del xs_in

    def pad_copies(fn):
        for e in range(N_EXPERTS):
            cnt = cnt_ref[0, e]
            head = (-cnt) & (SUBLANES - 1)
            rest = ((-cnt) & (MOE_BLOCK - 1)) - head
            off = pstart[e] + cnt
            for k in range(SUBLANES - 1):
                @pl.when(k < head)
                def _(off=off, k=k):
                    fn(pltpu.make_async_copy(zeros.at[pl.ds(0, 1), :], xs_ref.at[pl.ds(off + k, 1), :], zsem))
            off = off + head
            for bit in PAD_BITS:
                @pl.when((rest & bit) != 0)
                def _(off=off, bit=bit):
                    fn(pltpu.make_async_copy(zeros.at[pl.ds(0, bit), :],
                                             xs_ref.at[pl.ds(pl.multiple_of(off, SUBLANES), bit), :], zsem))
                off = off + (rest & bit)

    zeros[...] = jnp.zeros_like(zeros)
    pad_copies(lambda cp: cp.start())
    pad_copies(lambda cp: cp.wait())


def _padfill(xs, pstart, cnt):
    smem = pl.BlockSpec(memory_space=pltpu.SMEM)
    hbm = pl.BlockSpec(memory_space=pl.ANY)
    return pl.pallas_call(
        _padfill_kernel,
        in_specs=[smem, smem, hbm],
        out_specs=hbm,
        out_shape=jax.ShapeDtypeStruct(xs.shape, xs.dtype),
        input_output_aliases={2: 0},
        scratch_shapes=[pltpu.VMEM((MOE_BLOCK // 2, PACKED), U32), pltpu.SemaphoreType.DMA(())],
        name="padfill",
    )(cnt, pstart, xs)


W_SLOTS = 3
W_AHEAD = W_SLOTS - 1
EXPERT_GROUP = 16
EXPERT_RUNS = (1, 2, 4, 8)


def _expert_kernel(be_ref, nu_ref, ge_ref, x_ref, wg_hbm, wu_hbm, wd_hbm, o_ref,
                   wgf, wuf, wdf, wgb, wub, wdb, grp_ref, sems):
    step = pl.program_id(0)

    def weight_copies(e, slot):
        return (pltpu.make_async_copy(wg_hbm.at[e], wgf.at[slot], sems.at[slot, 0]),
                pltpu.make_async_copy(wu_hbm.at[e], wuf.at[slot], sems.at[slot, 1]),
                pltpu.make_async_copy(wd_hbm.at[e], wdf.at[slot], sems.at[slot, 2]))

    @pl.when(step == 0)
    def _():
        grp_ref[0] = 0
        for a in range(W_AHEAD):
            @pl.when(ge_ref[a] >= 0)
            def _(a=a):
                for cp in weight_copies(ge_ref[a], a):
                    cp.start()

    n_blocks = be_ref.shape[0]
    n_used = nu_ref[0]

    def swiglu(s, n):
        rows = pl.ds(pl.multiple_of(s * MOE_BLOCK, MOE_BLOCK), n * MOE_BLOCK)
        lo, hi = _unpack_rows(x_ref[rows, :])
        lo = lo.astype(BF16)
        hi = hi.astype(BF16)
        g = (jnp.dot(lo, wgb[:PACKED, :], preferred_element_type=F32)
             + jnp.dot(hi, wgb[PACKED:, :], preferred_element_type=F32))
        u = (jnp.dot(lo, wub[:PACKED, :], preferred_element_type=F32)
             + jnp.dot(hi, wub[PACKED:, :], preferred_element_type=F32))
        h = (g * _sigmoid(g) * u).astype(BF16)
        o_ref[rows, :] = _pack_rows(jnp.dot(h, wdb[...], preferred_element_type=F32))

    def run(s):
        j = step * EXPERT_GROUP + s
        e = be_ref[j]
        first = jnp.logical_or(j == 0, e != be_ref[jnp.maximum(j - 1, 0)])

        @pl.when(first)
        def _():
            grp = grp_ref[0]
            slot = grp % W_SLOTS
            for cp in weight_copies(e, slot):
                cp.wait()
            wgb[...] = wgf[slot].astype(BF16)
            wub[...] = wuf[slot].astype(BF16)
            wdb[...] = wdf[slot].astype(BF16)
            nxt = ge_ref[grp + W_AHEAD]

            @pl.when(nxt >= 0)
            def _():
                for cp in weight_copies(nxt, (grp + W_AHEAD) % W_SLOTS):
                    cp.start()
            grp_ref[0] = grp + 1

        def same(k):
            return (s + k < EXPERT_GROUP) & (j + k < n_used) & (be_ref[jnp.minimum(j + k, n_blocks - 1)] == e)
        take = jnp.int32(1)
        for n in EXPERT_RUNS[1:]:
            ok = same(n - 1)
            for k in range(1, n - 1):
                ok = ok & same(k)
            take = jnp.where(ok, n, take)
        for n in EXPERT_RUNS:
            @pl.when(take == n)
            def _(n=n):
                swiglu(s, n)
        return s + take

    lax.while_loop(lambda s: (s < EXPERT_GROUP) & (step * EXPERT_GROUP + s < n_used), run, jnp.int32(0))


def _experts(xs, block_expert, n_used, group_expert, w_gate, w_up, w_down):
    cap = xs.shape[0]
    n_blocks = cap // MOE_BLOCK
    assert n_blocks % EXPERT_GROUP == 0
    rows = EXPERT_GROUP * MOE_BLOCK
    last = lambda j, be, nu, ge: jnp.minimum(j, (nu[0] - 1) // EXPERT_GROUP)
    hbm = pl.BlockSpec(memory_space=pl.ANY)
    gs = pltpu.PrefetchScalarGridSpec(
        num_scalar_prefetch=3,
        grid=(n_blocks // EXPERT_GROUP,),
        in_specs=[pl.BlockSpec((rows, PACKED), lambda j, be, nu, ge: (last(j, be, nu, ge), 0)), hbm, hbm, hbm],
        out_specs=pl.BlockSpec((rows, PACKED), lambda j, be, nu, ge: (last(j, be, nu, ge), 0)),
        scratch_shapes=[pltpu.VMEM((W_SLOTS, D_MODEL, D_EXPERT), F32),
                        pltpu.VMEM((W_SLOTS, D_MODEL, D_EXPERT), F32),
                        pltpu.VMEM((W_SLOTS, D_EXPERT, D_MODEL), F32),
                        pltpu.VMEM((D_MODEL, D_EXPERT), BF16),
                        pltpu.VMEM((D_MODEL, D_EXPERT), BF16),
                        pltpu.VMEM((D_EXPERT, D_MODEL), BF16),
                        pltpu.SMEM((1,), jnp.int32),
                        pltpu.SemaphoreType.DMA((W_SLOTS, 3))],
    )
    return pl.pallas_call(
        _expert_kernel,
        grid_spec=gs,
        out_shape=jax.ShapeDtypeStruct((cap, PACKED), U32),
        compiler_params=_cparams(1),
        name="experts",
    )(block_expert, n_used, group_expert, xs, w_gate, w_up, w_down)


CB_TM = RT_TM


def _combine_kernel(x1_ref, gt_ref, y2_ref, o_ref):
    g = gt_ref[...]
    lo1, hi1 = _unpack_rows(y2_ref[0, 0])
    lo2, hi2 = _unpack_rows(y2_ref[0, 1])
    o_ref[:, :PACKED] = x1_ref[:, :PACKED] + g[:, 0:1] * lo1 + g[:, 1:2] * lo2
    o_ref[:, PACKED:] = x1_ref[:, PACKED:] + g[:, 0:1] * hi1 + g[:, 1:2] * hi2


def _combine(x1, gates, y2):
    T = x1.shape[0]
    nt = T // CB_TM
    return pl.pallas_call(
        _combine_kernel,
        grid=(nt,),
        in_specs=[pl.BlockSpec((CB_TM, D_MODEL), lambda i: (i, 0)),
                  pl.BlockSpec((CB_TM, RINFO), lambda i: (i, 0)),
                  pl.BlockSpec((1, TOP_K, CB_TM, PACKED), lambda i: (i, 0, 0, 0))],
        out_specs=pl.BlockSpec((CB_TM, D_MODEL), lambda i: (i, 0)),
        out_shape=jax.ShapeDtypeStruct((T, D_MODEL), F32),
        compiler_params=_cparams(1),
        name="combine",
    )(x1, gates, y2)


def _layer(x, rel_bias, ln1, w_in, q_norm, k_norm, attn_sink, conv_w, conv_b, lru_wa, lru_ba, lru_wi, lru_bi,
           lru_lambda, out_norm_attn, out_norm_lru, w_out, ln2, w_group, b_group, w_er, b_er, w_gate, w_up, w_down):
    B, S, D = x.shape
    T = B * S
    x2 = x.reshape(T, D)
    q, kv, xr, gr = _in_proj(x2, ln1, w_in, q_norm, k_norm)
    attn_n = _attention(q.reshape(B, S, ATTN_WIDTH), kv.reshape(B, S, 2 * KV_WIDTH), rel_bias, attn_sink,
                        out_norm_attn)
    lru_n = _rglru(xr.reshape(B, S, LRU_WIDTH), gr.reshape(B, S, LRU_WIDTH), conv_w, conv_b,
                   lru_wa, lru_ba, lru_wi, lru_bi, lru_lambda, out_norm_lru)
    x1, h2, gates, ei, cnt = _out_route(attn_n.reshape(T, ATTN_WIDTH), lru_n.reshape(T, LRU_WIDTH), x2, w_out, ln2,
                                        w_group, b_group, w_er, b_er)
    cap = _moe_cap(T)
    dest, pstart, block_expert, n_used, group_expert = _layout(ei, cnt, cap // MOE_BLOCK)
    xs = _padfill(_sc_dispatch(h2, dest, cap), pstart, cnt)
    yb = _experts(xs, block_expert, n_used, group_expert, w_gate, w_up, w_down)
    out = _combine(x1, gates, _sc_gather(yb, dest))
    return out.reshape(B, S, D)


def kernel(x, rel_bias, ln1, w_in, q_norm, k_norm, attn_sink, conv_w, conv_b, lru_wa, lru_ba, lru_wi, lru_bi,
           lru_lambda, out_norm_attn, out_norm_lru, w_out, ln2, w_group, b_group, w_expert_router, b_expert_router,
           w_gate, w_up, w_down):
    depth = ln1.shape[0]
    for l in range(depth):
        x = _layer(x, rel_bias, ln1[l], w_in[l], q_norm[l], k_norm[l], attn_sink[l], conv_w[l], conv_b[l],
                   lru_wa[l], lru_ba[l], lru_wi[l], lru_bi[l], lru_lambda[l], out_norm_attn[l], out_norm_lru[l],
                   w_out[l], ln2[l], w_group[l], b_group[l], w_expert_router[l], b_expert_router[l],
                   w_gate[l], w_up[l], w_down[l])
    return x
```

```python
import functools
import math

import jax
import jax.numpy as jnp
import numpy as np
from jax import lax
from jax.experimental import pallas as pl
from jax.experimental.pallas import tpu as pltpu
from jax.experimental.pallas import tpu_sc as plsc

D_MODEL = 1024
N_HEADS = 8
N_KV_HEADS = 2
HEAD_DIM = 64
Q_PER_KV = N_HEADS // N_KV_HEADS
ATTN_WIDTH = N_HEADS * HEAD_DIM
KV_WIDTH = N_KV_HEADS * HEAD_DIM
WINDOW = 128
BLOCK = 128
NUM_BUCKETS = 32
MAX_DISTANCE = 128
LRU_WIDTH = D_MODEL - ATTN_WIDTH
LRU_BLOCKS = 8
LRU_BLOCK_DIM = LRU_WIDTH // LRU_BLOCKS
LRU_C = 8.0
CONV_W = 4
CONV_LEFT = 2
N_GROUPS = 4
EXPERTS_PER_GROUP = 8
N_EXPERTS = N_GROUPS * EXPERTS_PER_GROUP
TOP_K = 2
D_EXPERT = 512
MOE_BLOCK = 128
EPS = 1e-6
NEG_INF = -1e30

LANES = 128
SUBLANES = 8
VMEM_LIMIT = 56 * 1024 * 1024
LRU_VMEM_LIMIT = 62 * 1024 * 1024

F32 = jnp.float32
BF16 = jnp.bfloat16
LOG2E = math.log2(math.e)


def _cparams(n_axes, vmem=VMEM_LIMIT):
    return pltpu.CompilerParams(dimension_semantics=("arbitrary",) * n_axes, vmem_limit_bytes=vmem)


def _rms(x, gain):
    return x * lax.rsqrt(jnp.mean(x * x, axis=-1, keepdims=True) + EPS) * gain


U32 = jnp.uint32
HI_MASK = 0xFFFF0000
PACKED = D_MODEL // 2


def _pack_rows(x):
    return _pack_rounded(x.astype(BF16).astype(F32))


def _pack_rounded(xb):
    h = xb.shape[1] // 2
    lo = lax.bitcast_convert_type(xb[:, :h], U32) >> 16
    hi = lax.bitcast_convert_type(xb[:, h:], U32) & jnp.uint32(HI_MASK)
    return lo | hi


def _unpack_rows(p):
    lo = lax.bitcast_convert_type(p << 16, F32)
    hi = lax.bitcast_convert_type(p & jnp.uint32(HI_MASK), F32)
    return lo, hi


IN_TM = 1024


def _head_rms(x, n_heads, gain):
    head = lax.broadcasted_iota(jnp.int32, (1, n_heads * HEAD_DIM), 1) // HEAD_DIM
    x2 = x * x
    scale = jnp.zeros_like(x)
    for h in range(n_heads):
        ms = jnp.sum(jnp.where(head == h, x2, 0.0), axis=-1, keepdims=True) * (1.0 / HEAD_DIM)
        scale = jnp.where(head == h, lax.rsqrt(ms + EPS), scale)
    return x * scale * gain


def _in_proj_kernel(x_ref, g_ref, w_ref, qg_ref, kg_ref, q_ref, kv_ref, xr_ref, gr_ref, wb_ref):
    @pl.when(pl.program_id(0) == 0)
    def _():
        wb_ref[...] = w_ref[...].astype(BF16)

    h = _rms(x_ref[...], g_ref[...]).astype(BF16)
    c_k = ATTN_WIDTH
    c_v = c_k + KV_WIDTH
    c_x = c_v + KV_WIDTH
    c_g = c_x + LRU_WIDTH
    q = jnp.dot(h, wb_ref[:, :c_k], preferred_element_type=F32)
    q_ref[...] = _head_rms(q, N_HEADS, qg_ref[...]).astype(BF16)
    k = jnp.dot(h, wb_ref[:, c_k:c_v], preferred_element_type=F32)
    kv_ref[:, :KV_WIDTH] = _head_rms(k, N_KV_HEADS, kg_ref[...]).astype(BF16)
    kv_ref[:, KV_WIDTH:] = jnp.dot(h, wb_ref[:, c_v:c_x], preferred_element_type=F32).astype(BF16)
    xr_ref[...] = jnp.dot(h, wb_ref[:, c_x:c_g], preferred_element_type=F32)
    gr_ref[...] = jnp.dot(h, wb_ref[:, c_g:], preferred_element_type=F32)


def _in_proj(x2, ln1, w_in, q_gain, k_gain):
    T = x2.shape[0]
    n_in = w_in.shape[1]
    row = lambda w: pl.BlockSpec((IN_TM, w), lambda i: (i, 0))
    qg = (jnp.tile(q_gain.astype(F32), N_HEADS) * (HEAD_DIM ** -0.5 * LOG2E)).reshape(1, ATTN_WIDTH)
    kg = jnp.tile(k_gain.astype(F32), N_KV_HEADS).reshape(1, KV_WIDTH)
    return pl.pallas_call(
        _in_proj_kernel,
        grid=(T // IN_TM,),
        in_specs=[row(D_MODEL),
                  pl.BlockSpec((1, D_MODEL), lambda i: (0, 0)),
                  pl.BlockSpec((D_MODEL, n_in), lambda i: (0, 0)),
                  pl.BlockSpec((1, ATTN_WIDTH), lambda i: (0, 0)),
                  pl.BlockSpec((1, KV_WIDTH), lambda i: (0, 0))],
        out_specs=[row(ATTN_WIDTH), row(2 * KV_WIDTH), row(LRU_WIDTH), row(LRU_WIDTH)],
        out_shape=[jax.ShapeDtypeStruct((T, ATTN_WIDTH), BF16),
                   jax.ShapeDtypeStruct((T, 2 * KV_WIDTH), BF16),
                   jax.ShapeDtypeStruct((T, LRU_WIDTH), F32),
                   jax.ShapeDtypeStruct((T, LRU_WIDTH), F32)],
        scratch_shapes=[pltpu.VMEM((D_MODEL, n_in), BF16)],
        compiler_params=_cparams(1),
        name="in_proj",
    )(x2, ln1.reshape(1, D_MODEL), w_in, qg, kg)


def _t5_bucket(rel):
    half = NUM_BUCKETS // 2
    max_exact = half // 2
    base = jnp.where(rel > 0, half, 0)
    n = jnp.abs(rel)
    nf = jnp.maximum(n, 1).astype(jnp.float32)
    large = max_exact + (jnp.log(nf / max_exact) / math.log(MAX_DISTANCE / max_exact)
                         * (half - max_exact)).astype(jnp.int32)
    large = jnp.minimum(large, half - 1)
    return base + jnp.where(n < max_exact, n, large)


HEAD_PAIRS = Q_PER_KV // 2
EDGE_VARIANTS = 3


def _fill_bias_table(rb_ref, bucket_ref, band_ref, o_ref):
    bucket = bucket_ref[...]
    band = band_ref[...] > 0
    col = lax.broadcasted_iota(jnp.int32, bucket.shape, 1)
    valid = (band & (col >= BLOCK), band, band & (col < 2 * BLOCK))
    for h in range(N_HEADS):
        acc = jnp.zeros(bucket.shape, F32)
        for b in range(NUM_BUCKETS):
            acc = jnp.where(bucket == b, rb_ref[b, h], acc)
        kv, g = divmod(h, Q_PER_KV)
        pair, parity = divmod(g, 2)
        for var in range(EDGE_VARIANTS):
            o_ref[var, kv, parity, pair * BLOCK:(pair + 1) * BLOCK, :] = jnp.where(valid[var], acc * LOG2E, NEG_INF)


def _attn_kernel(sink_ref, rb_ref, q_ref, kp_ref, kc_ref, kn_ref, bucket_ref, band_ref, og_ref, o_ref, bias_ref):
    n = pl.program_id(1)

    @pl.when((pl.program_id(0) == 0) & (n == 0))
    def _():
        _fill_bias_table(rb_ref, bucket_ref, band_ref, bias_ref)

    kv_all = jnp.concatenate([kp_ref[0], kc_ref[0], kn_ref[0]], axis=0)
    for qb in range(ATTN_QB):
        variant = 1
        if qb == 0:
            variant = jnp.where(n == 0, 0, 1)
        if qb == ATTN_QB - 1:
            variant = jnp.where(n == pl.num_programs(1) - 1, 2, variant)
        out = _attn_block(q_ref[0, qb * BLOCK:(qb + 1) * BLOCK, :], kv_all[qb * BLOCK:(qb + 3) * BLOCK, :],
                          lambda kv, parity: bias_ref[variant, kv, parity], sink_ref)
        o_ref[0, qb * BLOCK:(qb + 1) * BLOCK, :] = _rms(out, og_ref[...]).astype(o_ref.dtype)


def _attn_block(q, kvw, bias, sink_ref):
    low = lax.broadcasted_iota(jnp.int32, (1, LANES), 1) < HEAD_DIM
    swap = lambda slab: pltpu.roll(slab.astype(F32), HEAD_DIM, 1).astype(BF16)
    kslab, vslab = kvw[:, :KV_WIDTH], kvw[:, KV_WIDTH:]
    kslab_sw, vslab_sw = swap(kslab), swap(vslab)
    rowi = lax.broadcasted_iota(jnp.int32, (HEAD_PAIRS * BLOCK, 1), 0)
    combos = [(kv, parity) for kv in range(N_KV_HEADS) for parity in range(2)]
    scores, vzs, sinks = [], [], []
    for kv, parity in combos:
        ks, vs = (kslab, vslab) if (kv == 0) == (parity == 0) else (kslab_sw, vslab_sw)
        keep = low if parity == 0 else jnp.logical_not(low)
        kz = jnp.where(keep, ks, jnp.zeros_like(ks))
        vzs.append(jnp.where(keep, vs, jnp.zeros_like(vs)))
        base = kv * Q_PER_KV * HEAD_DIM
        qpair = jnp.concatenate([q[:, base + j * LANES:base + (j + 1) * LANES] for j in range(HEAD_PAIRS)], axis=0)
        s = lax.dot_general(qpair, kz, (((1,), (1,)), ((), ())), preferred_element_type=F32)
        scores.append(s + bias(kv, parity))
        sink = jnp.zeros((HEAD_PAIRS * BLOCK, 1), F32)
        for j in range(HEAD_PAIRS):
            sink = jnp.where(rowi // BLOCK == j, sink_ref[kv * Q_PER_KV + 2 * j + parity], sink)
        sinks.append(sink)
    probs, inv = [], []
    for s, sink in zip(scores, sinks):
        m = jnp.maximum(jnp.max(s, axis=-1, keepdims=True), sink)
        p = jnp.exp2(s - m)
        inv.append(1.0 / (jnp.sum(p, axis=-1, keepdims=True) + jnp.exp2(sink - m)))
        probs.append(p.astype(BF16))
    outs = [jnp.dot(p, vz, preferred_element_type=F32) * r for p, vz, r in zip(probs, vzs, inv)]
    cols = []
    for kv in range(N_KV_HEADS):
        acc = outs[2 * kv] + outs[2 * kv + 1]
        cols += [acc[j * BLOCK:(j + 1) * BLOCK, :] for j in range(HEAD_PAIRS)]
    return jnp.concatenate(cols, axis=1)


ATTN_QB = 4


def _attention(q, kv, rel_bias, sink, out_gain):
    B, S, _ = q.shape
    nb = S // BLOCK
    assert ATTN_QB >= 2 and nb % ATTN_QB == 0, "a step's first and last query blocks must be distinct"
    ns = nb // ATTN_QB
    rows = ATTN_QB * BLOCK
    qi = jnp.arange(BLOCK, dtype=jnp.int32)
    kj = jnp.arange(3 * BLOCK, dtype=jnp.int32)
    rel = kj[None, :] - BLOCK - qi[:, None]
    bucket = _t5_bucket(rel).astype(jnp.int32)
    band = (jnp.abs(rel) <= WINDOW).astype(jnp.int32)
    kvspec = lambda f: pl.BlockSpec((1, BLOCK, 2 * KV_WIDTH), f)
    smem = pl.BlockSpec(memory_space=pltpu.SMEM)
    geom = pl.BlockSpec((BLOCK, 3 * BLOCK), lambda b, n: (0, 0))
    return pl.pallas_call(
        _attn_kernel,
        grid=(B, ns),
        in_specs=[smem, smem,
                  pl.BlockSpec((1, rows, ATTN_WIDTH), lambda b, n: (b, n, 0)),
                  kvspec(lambda b, n: (b, jnp.maximum(n * ATTN_QB - 1, 0), 0)),
                  pl.BlockSpec((1, rows, 2 * KV_WIDTH), lambda b, n: (b, n, 0)),
                  kvspec(lambda b, n: (b, jnp.minimum((n + 1) * ATTN_QB, nb - 1), 0)),
                  geom, geom,
                  pl.BlockSpec((1, ATTN_WIDTH), lambda b, n: (0, 0))],
        out_specs=pl.BlockSpec((1, rows, ATTN_WIDTH), lambda b, n: (b, n, 0)),
        out_shape=jax.ShapeDtypeStruct((B, S, ATTN_WIDTH), BF16),
        scratch_shapes=[pltpu.VMEM((EDGE_VARIANTS, N_KV_HEADS, 2, HEAD_PAIRS * BLOCK, 3 * BLOCK), F32)],
        compiler_params=_cparams(2),
        name="attention",
    )(sink.astype(F32) * LOG2E, rel_bias.astype(F32), q, kv, kv, kv, bucket, band, out_gain.reshape(1, ATTN_WIDTH))


LRU_TC = 128
LRU_PITCH = LRU_TC + SUBLANES
LRU_SLABS = LRU_WIDTH // LANES
HALO = SUBLANES


def _softplus(x):
    return jnp.maximum(x, 0.0) + jnp.log(1.0 + jnp.exp(-jnp.abs(x)))


def _gelu_tanh(x):
    k = math.sqrt(2.0 / math.pi)
    hx = 0.5 * x
    return hx + hx * jnp.tanh(x * (k + (k * 0.044715) * (x * x)))


def _sigmoid(x):
    return 0.5 + 0.5 * jnp.tanh(0.5 * x)


def _rglru_kernel(xr_ref, xp_ref, xn_ref, gr_ref, cw_ref, cb_ref, wa_ref, wi_ref, ba_ref, bi_ref, lam_ref, og_ref,
                  o_ref, sx_ref, a_ref, u_ref, h_ref, carry_ref, hf_ref, xcs_ref, wg_ref, bg_ref, k_ref):
    p = pl.program_id(0)
    i = pl.program_id(1)
    nc = pl.num_programs(1)
    c = i + p * (nc - 1 - 2 * i)
    B = xr_ref.shape[0]
    TC = LRU_TC

    @pl.when(i == 0)
    def _():
        carry_ref[...] = jnp.zeros_like(carry_ref)
        wg_ref[...] = jnp.zeros_like(wg_ref)
        for sel, w_ref in enumerate((wa_ref, wi_ref)):
            for h in range(LRU_BLOCKS):
                lo = h * LRU_BLOCK_DIM
                wg_ref[lo:lo + LRU_BLOCK_DIM, sel * LRU_WIDTH + lo:sel * LRU_WIDTH + lo + LRU_BLOCK_DIM] = (
                    0.5 * w_ref[0, h]).astype(BF16)
        row = pl.ds(p, 1)
        bg_ref[:, :LRU_WIDTH] = 0.5 * ba_ref[row, :]
        bg_ref[:, LRU_WIDTH:] = 0.5 * bi_ref[row, :]
        k_ref[...] = (-0.5 * LRU_C * math.log2(math.e)) * _softplus(-lam_ref[row, :])

    def gates_and_scan(xc2, backward):
        g = jnp.dot(xc2.astype(BF16), wg_ref[...], preferred_element_type=F32) + bg_ref[...]
        ta = jnp.tanh(g[:, :LRU_WIDTH])
        ig = 0.5 + 0.5 * jnp.tanh(g[:, LRU_WIDTH:])
        a = jnp.exp2((1.0 + ta) * k_ref[...])
        z = 1.0 - a * a
        u = z * lax.rsqrt(jnp.maximum(z, 1e-30)) * ig * xc2
        for b in range(B):
            for s in range(LRU_SLABS):
                a_ref[s, b * LRU_PITCH:b * LRU_PITCH + TC, :] = a[b * TC:(b + 1) * TC, s * LANES:(s + 1) * LANES]
                u_ref[s, b * LRU_PITCH:b * LRU_PITCH + TC, :] = u[b * TC:(b + 1) * TC, s * LANES:(s + 1) * LANES]

        def step(k, hs):
            t = TC - 1 - k if backward else k
            out = []
            for s in range(LRU_SLABS):
                idx = pl.ds(t, B, stride=LRU_PITCH)
                hn = a_ref[s, idx, :] * hs[s] + u_ref[s, idx, :]
                h_ref[s, idx, :] = hn
                out.append(hn)
            return tuple(out)

        hs = lax.fori_loop(0, TC, step, tuple(carry_ref[s] for s in range(LRU_SLABS)), unroll=8)
        for s in range(LRU_SLABS):
            carry_ref[s] = hs[s]

    @pl.when(p == 0)
    def _():
        sx_ref[:, HALO:HALO + TC, :] = xr_ref[...]
        sx_ref[:, 0:HALO, :] = jnp.where(c > 0, xp_ref[...], 0.0)
        sx_ref[:, HALO + TC:, :] = jnp.where(c < nc - 1, xn_ref[...], 0.0)
        xc = cb_ref[...][None]
        for j in range(CONV_W):
            off = HALO + j - CONV_LEFT
            xc = xc + cw_ref[j:j + 1, :][None] * sx_ref[:, off:off + TC, :]
        xc2 = xc.reshape(B * TC, LRU_WIDTH)
        xcs_ref[c] = xc2.astype(xcs_ref.dtype)
        gates_and_scan(xc2, backward=False)
        for b in range(B):
            for s in range(LRU_SLABS):
                hf_ref[c, s, b * TC:(b + 1) * TC, :] = h_ref[s, b * LRU_PITCH:b * LRU_PITCH + TC, :].astype(hf_ref.dtype)

    @pl.when(p == 1)
    def _():
        gates_and_scan(xcs_ref[c].astype(F32), backward=True)
        for b in range(B):
            hsum = jnp.concatenate(
                [h_ref[s, b * LRU_PITCH:b * LRU_PITCH + TC, :] + hf_ref[c, s, b * TC:(b + 1) * TC, :].astype(F32)
                 for s in range(LRU_SLABS)], axis=1)
            y = hsum * _gelu_tanh(gr_ref[b])
            o_ref[b] = _rms(y, og_ref[...]).astype(o_ref.dtype)


def _rglru(xr, gr, conv_w, conv_b, w_a, b_a, w_i, b_i, lam, out_gain):
    B, S, W = xr.shape
    nc = S // LRU_TC
    hb = LRU_TC // HALO
    fwd = lambda p, i: jnp.where(p == 0, i, nc - 1)
    bwd = lambda p, i: nc - 1 - p * i
    full2 = lambda shape: pl.BlockSpec(shape, lambda p, i: (0,) * len(shape))
    wblock = pl.BlockSpec((1, LRU_BLOCKS, LRU_BLOCK_DIM, LRU_BLOCK_DIM), lambda p, i: (p, 0, 0, 0))
    return pl.pallas_call(
        _rglru_kernel,
        grid=(2, nc),
        in_specs=[pl.BlockSpec((B, LRU_TC, W), lambda p, i: (0, fwd(p, i), 0)),
                  pl.BlockSpec((B, HALO, W), lambda p, i: (0, jnp.maximum(fwd(p, i) * hb - 1, 0), 0)),
                  pl.BlockSpec((B, HALO, W), lambda p, i: (0, jnp.minimum((fwd(p, i) + 1) * hb, S // HALO - 1), 0)),
                  pl.BlockSpec((B, LRU_TC, W), lambda p, i: (0, bwd(p, i), 0)),
                  full2((CONV_W, W)),
                  full2((1, W)),
                  wblock, wblock,
                  full2((2, W)), full2((2, W)), full2((2, W)),
                  full2((1, W))],
        out_specs=pl.BlockSpec((B, LRU_TC, W), lambda p, i: (0, bwd(p, i), 0)),
        out_shape=jax.ShapeDtypeStruct((B, S, W), BF16),
        scratch_shapes=[pltpu.VMEM((B, LRU_TC + 2 * HALO, W), F32),
                        pltpu.VMEM((LRU_SLABS, B * LRU_PITCH, LANES), F32),
                        pltpu.VMEM((LRU_SLABS, B * LRU_PITCH, LANES), F32),
                        pltpu.VMEM((LRU_SLABS, B * LRU_PITCH, LANES), F32),
                        pltpu.VMEM((LRU_SLABS, B, LANES), F32),
                        pltpu.VMEM((nc, LRU_SLABS, B * LRU_TC, LANES), BF16),
                        pltpu.VMEM((nc, B * LRU_TC, W), BF16),
                        pltpu.VMEM((W, 2 * W), BF16),
                        pltpu.VMEM((1, 2 * W), F32),
                        pltpu.VMEM((1, W), F32)],
        compiler_params=_cparams(2, LRU_VMEM_LIMIT),
        name="rglru",
    )(xr, xr, xr, gr, conv_w.astype(F32), conv_b.reshape(1, W).astype(F32), w_a.astype(F32), w_i.astype(F32),
      b_a.astype(F32), b_i.astype(F32), lam.astype(F32), out_gain.reshape(1, W).astype(F32))


RT_TM = 1024
RT_PARTS = 4
RT_COLS = LANES
RT_ROWS = 48
RINFO = SUBLANES


def _split_bf16(x):
    hi = x.astype(BF16)
    lo = (x - hi.astype(F32)).astype(BF16)
    return hi, lo


def _route_kernel(an_ref, ln_ref, x_ref, wo_ref, g2_ref, wr_ref, br_ref,
                  x1_ref, h2_ref, gt_ref, ei_ref, cnt_ref, wob_ref, wrb_ref, tri_ref, run_ref, runc_ref):
    @pl.when(pl.program_id(0) == 0)
    def _():
        wob_ref[...] = wo_ref[...].astype(BF16)
        hi, lo = _split_bf16(wr_ref[...])
        wrb_ref[:RT_ROWS, :] = hi
        wrb_ref[RT_ROWS:, :] = lo
        r = lax.broadcasted_iota(jnp.int32, (RT_TM, RT_TM), 0)
        cidx = lax.broadcasted_iota(jnp.int32, (RT_TM, RT_TM), 1)
        tri_ref[...] = (r < cidx).astype(BF16)
        run_ref[...] = jnp.zeros_like(run_ref)
        runc_ref[...] = jnp.zeros_like(runc_ref)

    nt_dims = (((1,), (1,)), ((), ()))
    part = RT_TM // RT_PARTS
    x1s = []
    for r in range(RT_PARTS):
        rows = slice(r * part, (r + 1) * part)
        x1 = (x_ref[rows, :]
              + jnp.dot(an_ref[rows, :], wob_ref[:ATTN_WIDTH, :], preferred_element_type=F32)
              + jnp.dot(ln_ref[rows, :], wob_ref[ATTN_WIDTH:, :], preferred_element_type=F32))
        x1_ref[rows, :] = x1
        x1s.append(x1)
    splits = []
    for r, x1 in enumerate(x1s):
        h2 = _rms(x1, g2_ref[...])
        hi = h2.astype(BF16)
        hi_f = hi.astype(F32)
        h2_ref[r * part:(r + 1) * part, :] = _pack_rounded(hi_f)
        splits.append((hi, (h2 - hi_f).astype(BF16)))
    logits = []
    for hi, lo in splits:
        t1 = lax.dot_general(wrb_ref[...], hi, nt_dims, preferred_element_type=F32)
        t2 = lax.dot_general(wrb_ref[:RT_ROWS, :], lo, nt_dims, preferred_element_type=F32)
        logits.append(t1[:RT_ROWS] + t1[RT_ROWS:] + t2)
    logit = jnp.concatenate(logits, axis=1) + br_ref[...]

    sub = lax.broadcasted_iota(jnp.int32, (SUBLANES, RT_TM), 0)
    first_min = lambda hit: jnp.min(jnp.where(hit, sub, SUBLANES), axis=0, keepdims=True)
    is_g = sub < N_GROUPS
    gl = jnp.where(is_g, logit[:SUBLANES], -jnp.inf)
    gm = jnp.max(gl, axis=0, keepdims=True)
    gidx = first_min(gl == gm)
    g_p = 1.0 / jnp.sum(jnp.where(is_g, jnp.exp(logit[:SUBLANES] - gm), 0.0), axis=0, keepdims=True)
    el = logit[SUBLANES:2 * SUBLANES]
    for g in range(1, N_GROUPS):
        el = jnp.where(gidx == g, logit[(g + 1) * SUBLANES:(g + 2) * SUBLANES], el)
    m1 = jnp.max(el, axis=0, keepdims=True)
    i1 = first_min(el == m1)
    el2 = jnp.where(sub == i1, -jnp.inf, el)
    m2 = jnp.max(el2, axis=0, keepdims=True)
    i2 = first_min(el2 == m2)
    t = jnp.exp(m2 - m1)
    gate1 = g_p / (1.0 + t)
    gate2 = g_p * t / (1.0 + t)
    e1 = gidx * EXPERTS_PER_GROUP + i1
    e2 = gidx * EXPERTS_PER_GROUP + i2

    erow = lax.broadcasted_iota(jnp.int32, (N_EXPERTS, RT_TM), 0)
    oh1 = erow == e1
    oh2 = erow == e2
    oh = (oh1 | oh2).astype(F32)
    ohb = oh.astype(BF16)
    cum = jnp.dot(ohb, tri_ref[...], preferred_element_type=F32) + runc_ref[...]
    rank1 = jnp.sum(jnp.where(oh1, cum, 0.0), axis=0, keepdims=True)
    rank2 = jnp.sum(jnp.where(oh2, cum, 0.0), axis=0, keepdims=True)
    runc_ref[...] = runc_ref[...] + jnp.sum(oh, axis=1, keepdims=True)
    tile_cnt = lax.dot_general(jnp.ones((SUBLANES, RT_TM), BF16), ohb, nt_dims, preferred_element_type=F32)
    run_ref[:, :N_EXPERTS] = run_ref[:, :N_EXPERTS] + tile_cnt[0:1]
    cnt_ref[...] = run_ref[...].astype(jnp.int32)

    rows = [e1, e2, rank1.astype(jnp.int32), rank2.astype(jnp.int32)]
    ei = jnp.zeros((RINFO, RT_TM), jnp.int32)
    for k, v in enumerate(rows):
        ei = jnp.where(sub == k, v, ei)
    ei_ref[0] = ei
    gt_ref[...] = jnp.where(sub == 0, gate1, jnp.where(sub == 1, gate2, 0.0)).T


def _out_route(attn_n, lru_n, x2, w_out, ln2, w_group, b_group, w_er, b_er):
    T = x2.shape[0]
    pad_g = SUBLANES - N_GROUPS
    wr = jnp.concatenate([jnp.pad(w_group.T, ((0, pad_g), (0, 0))),
                          jnp.transpose(w_er, (0, 2, 1)).reshape(N_EXPERTS, D_MODEL)], axis=0)
    wr = jnp.pad(wr, ((0, RT_ROWS - wr.shape[0]), (0, 0))).astype(F32)
    br = jnp.concatenate([jnp.pad(b_group, (0, pad_g)), b_er.reshape(-1)])
    br = jnp.pad(br, (0, RT_ROWS - br.shape[0])).reshape(RT_ROWS, 1).astype(F32)
    row = lambda w: pl.BlockSpec((RT_TM, w), lambda i: (i, 0))
    const = lambda shape: pl.BlockSpec(shape, lambda i: (0, 0))
    return pl.pallas_call(
        _route_kernel,
        grid=(T // RT_TM,),
        in_specs=[row(ATTN_WIDTH), row(LRU_WIDTH), row(D_MODEL), const((D_MODEL, D_MODEL)), const((1, D_MODEL)),
                  const((RT_ROWS, D_MODEL)), const((RT_ROWS, 1))],
        out_specs=[row(D_MODEL), row(PACKED), row(RINFO),
                   pl.BlockSpec((1, RINFO, RT_TM), lambda i: (i, 0, 0)), const((1, RT_COLS))],
        out_shape=[jax.ShapeDtypeStruct((T, D_MODEL), F32),
                   jax.ShapeDtypeStruct((T, PACKED), U32),
                   jax.ShapeDtypeStruct((T, RINFO), F32),
                   jax.ShapeDtypeStruct((T // RT_TM, RINFO, RT_TM), jnp.int32),
                   jax.ShapeDtypeStruct((1, RT_COLS), jnp.int32)],
        scratch_shapes=[pltpu.VMEM((D_MODEL, D_MODEL), BF16),
                        pltpu.VMEM((2 * RT_ROWS, D_MODEL), BF16),
                        pltpu.VMEM((RT_TM, RT_TM), BF16),
                        pltpu.VMEM((1, RT_COLS), F32),
                        pltpu.VMEM((N_EXPERTS, 1), F32)],
        compiler_params=_cparams(1),
        name="out_route",
    )(attn_n, lru_n, x2, w_out, ln2.reshape(1, D_MODEL).astype(F32), wr, br)


def _moe_cap(T):
    A = T * TOP_K
    return ((A + MOE_BLOCK - 1) // MOE_BLOCK) * MOE_BLOCK + N_EXPERTS * MOE_BLOCK


PAD_BITS = tuple(1 << b for b in reversed(range(3, MOE_BLOCK.bit_length() - 1)))


def _layout_kernel(cnt_ref, ei_ref, dest_ref, pstart, be_ref, nu_ref, ge_ref):
    n_blocks = be_ref.shape[0]

    def lay(e, carry):
        start, blk, grp = carry
        pstart[e] = start
        nb = (cnt_ref[0, e] + MOE_BLOCK - 1) // MOE_BLOCK
        ge_ref[grp] = e

        def fill(k, c):
            be_ref[blk + k] = e
            return c
        lax.fori_loop(0, nb, fill, 0)
        return start + nb * MOE_BLOCK, blk + nb, grp + (nb > 0).astype(jnp.int32)
    _, used, groups = lax.fori_loop(0, N_EXPERTS, lay, (jnp.int32(0), jnp.int32(0), jnp.int32(0)))
    nu_ref[0] = used

    def tail(k, c):
        be_ref[k] = N_EXPERTS - 1
        return c
    lax.fori_loop(used, n_blocks, tail, 0)

    def no_group(k, c):
        ge_ref[k] = -1
        return c
    lax.fori_loop(groups, ge_ref.shape[0], no_group, 0)

    expert = ei_ref[:, 0:TOP_K, :]
    dest = ei_ref[:, TOP_K:2 * TOP_K, :]
    for e in range(N_EXPERTS):
        dest = dest + jnp.where(expert == e, pstart[e], 0)
    dest_ref[...] = dest


def _layout(ei, cnt, n_blocks):
    nt = ei.shape[0]
    smem = pl.BlockSpec(memory_space=pltpu.SMEM)
    vmem = pl.BlockSpec(memory_space=pltpu.VMEM)
    return pl.pallas_call(
        _layout_kernel,
        in_specs=[smem, vmem],
        out_specs=[vmem, smem, smem, smem, smem],
        out_shape=[jax.ShapeDtypeStruct((nt, TOP_K, RT_TM), jnp.int32),
                   jax.ShapeDtypeStruct((N_EXPERTS,), jnp.int32),
                   jax.ShapeDtypeStruct((n_blocks,), jnp.int32),
                   jax.ShapeDtypeStruct((1,), jnp.int32),
                   jax.ShapeDtypeStruct((N_EXPERTS + W_AHEAD,), jnp.int32)],
        name="layout",
    )(cnt, ei)


SC_CHUNK = 64
SC_BUFS = 3
SC_LEAD = SC_BUFS - 1


def _sc_workers():
    info = plsc.get_sparse_core_info()
    return info.num_cores, info.num_subcores


def _sc_ring(n_chunks, read, write):
    for c in range(min(SC_LEAD, n_chunks)):
        for cp in read(c):
            cp.start()
    reclaimed = set()
    for c in range(n_chunks):
        for cp in read(c):
            cp.wait()
        for cp in write(c):
            cp.start()
        nxt = c + SC_LEAD
        if nxt < n_chunks:
            if nxt - SC_BUFS >= 0:
                for cp in write(nxt - SC_BUFS):
                    cp.wait()
                reclaimed.add(nxt - SC_BUFS)
            for cp in read(nxt):
                cp.start()
    for c in range(n_chunks):
        if c not in reclaimed:
            for cp in write(c):
                cp.wait()


def _sc_dispatch(h2p, dest, cap):
    T = h2p.shape[0]
    nc, ns = _sc_workers()
    per_w = T // (nc * ns)
    n_ch = per_w // SC_CHUNK
    nt, _, tm = dest.shape
    assert nt * tm == T and tm % per_w == 0 and per_w % SC_CHUNK == 0
    idx = dest.reshape(nt, TOP_K, tm // per_w, per_w).transpose(0, 2, 1, 3).reshape(nc * ns, TOP_K * n_ch, SC_CHUNK)
    mesh = plsc.VectorSubcoreMesh(core_axis_name="c", subcore_axis_name="s")

    @functools.partial(
        pl.kernel, mesh=mesh,
        out_type=jax.ShapeDtypeStruct((cap, PACKED), U32),
        scratch_types=[pltpu.VMEM((TOP_K * n_ch, SC_CHUNK), jnp.int32),
                       pltpu.VMEM((SC_BUFS, SC_CHUNK, PACKED), U32),
                       pltpu.SemaphoreType.DMA((SC_BUFS,)),
                       pltpu.SemaphoreType.DMA((SC_BUFS,))])
    def scatter(src_hbm, idx_hbm, out_hbm, idx_v, rows_v, rsem, wsem):
        wid = lax.axis_index("s") * nc + lax.axis_index("c")
        base = pl.multiple_of(wid * per_w, per_w)
        pltpu.sync_copy(idx_hbm.at[wid], idx_v)

        def read(c):
            b = c % SC_BUFS
            return [pltpu.make_async_copy(src_hbm.at[pl.ds(base + c * SC_CHUNK, SC_CHUNK)], rows_v.at[b], rsem.at[b])]

        def write(c):
            b = c % SC_BUFS
            return [pltpu.make_async_copy(rows_v.at[b], out_hbm.at[idx_v.at[k * n_ch + c]], wsem.at[b])
                    for k in range(TOP_K)]
        _sc_ring(n_ch, read, write)

    return scatter(h2p, idx)


def _sc_gather(yb, dest):
    nt, _, tm = dest.shape
    nc, ns = _sc_workers()
    n_rows = nt * TOP_K * tm
    per_w = n_rows // (nc * ns)
    n_ch = per_w // SC_CHUNK
    assert per_w * nc * ns == n_rows and per_w % SC_CHUNK == 0
    mesh = plsc.VectorSubcoreMesh(core_axis_name="c", subcore_axis_name="s")

    @functools.partial(
        pl.kernel, mesh=mesh,
        out_type=jax.ShapeDtypeStruct((n_rows, PACKED), U32),
        scratch_types=[pltpu.VMEM((per_w,), jnp.int32),
                       pltpu.VMEM((SC_BUFS, SC_CHUNK, PACKED), U32),
                       pltpu.SemaphoreType.DMA((SC_BUFS,)),
                       pltpu.SemaphoreType.DMA((SC_BUFS,))])
    def gather(table_hbm, idx_hbm, out_hbm, idx_v, rows_v, gsem, wsem):
        wid = lax.axis_index("s") * nc + lax.axis_index("c")
        base = pl.multiple_of(wid * per_w, per_w)
        pltpu.sync_copy(idx_hbm.at[pl.ds(base, per_w)], idx_v)

        def read(c):
            b = c % SC_BUFS
            return [pltpu.make_async_copy(table_hbm.at[idx_v.at[pl.ds(c * SC_CHUNK, SC_CHUNK)]], rows_v.at[b], gsem.at[b])]

        def write(c):
            b = c % SC_BUFS
            return [pltpu.make_async_copy(rows_v.at[b], out_hbm.at[pl.ds(base + c * SC_CHUNK, SC_CHUNK)], wsem.at[b])]
        _sc_ring(n_ch, read, write)

    return gather(yb, dest.reshape(n_rows)).reshape(nt, TOP_K, tm, PACKED)


def _padfill_kernel(cnt_ref, pstart, xs_in, xs_ref, zeros, zsem):
    del xs_in

    def pad_copies(fn):
        for e in range(N_EXPERTS):
            cnt = cnt_ref[0, e]
            head = (-cnt) & (SUBLANES - 1)
            rest = ((-cnt) & (MOE_BLOCK - 1)) - head
            off = pstart[e] + cnt
            for k in range(SUBLANES - 1):
                @pl.when(k < head)
                def _(off=off, k=k):
                    fn(pltpu.make_async_copy(zeros.at[pl.ds(0, 1), :], xs_ref.at[pl.ds(off + k, 1), :], zsem))
            off = off + head
            for bit in PAD_BITS:
                @pl.when((rest & bit) != 0)
                def _(off=off, bit=bit):
                    fn(pltpu.make_async_copy(zeros.at[pl.ds(0, bit), :],
                                             xs_ref.at[pl.ds(pl.multiple_of(off, SUBLANES), bit), :], zsem))
                off = off + (rest & bit)

    zeros[...] = jnp.zeros_like(zeros)
    pad_copies(lambda cp: cp.start())
    pad_copies(lambda cp: cp.wait())


def _padfill(xs, pstart, cnt):
    smem = pl.BlockSpec(memory_space=pltpu.SMEM)
    hbm = pl.BlockSpec(memory_space=pl.ANY)
    return pl.pallas_call(
        _padfill_kernel,
        in_specs=[smem, smem, hbm],
        out_specs=hbm,
        out_shape=jax.ShapeDtypeStruct(xs.shape, xs.dtype),
        input_output_aliases={2: 0},
        scratch_shapes=[pltpu.VMEM((MOE_BLOCK // 2, PACKED), U32), pltpu.SemaphoreType.DMA(())],
        name="padfill",
    )(cnt, pstart, xs)


W_SLOTS = 3
W_AHEAD = W_SLOTS - 1
EXPERT_GROUP = 16
EXPERT_RUNS = (1, 2, 4, 8)


def _expert_kernel(be_ref, nu_ref, ge_ref, x_ref, wg_hbm, wu_hbm, wd_hbm, o_ref,
                   wgf, wuf, wdf, grp_ref, sems):
    step = pl.program_id(0)

    def weight_copies(e, slot):
        return (pltpu.make_async_copy(wg_hbm.at[e], wgf.at[slot], sems.at[slot, 0]),
                pltpu.make_async_copy(wu_hbm.at[e], wuf.at[slot], sems.at[slot, 1]),
                pltpu.make_async_copy(wd_hbm.at[e], wdf.at[slot], sems.at[slot, 2]))

    @pl.when(step == 0)
    def _():
        grp_ref[0] = 0
        for a in range(W_AHEAD):
            @pl.when(ge_ref[a] >= 0)
            def _(a=a):
                for cp in weight_copies(ge_ref[a], a):
                    cp.start()

    n_blocks = be_ref.shape[0]
    n_used = nu_ref[0]

    def swiglu(s, n, slot):
        rows = pl.ds(pl.multiple_of(s * MOE_BLOCK, MOE_BLOCK), n * MOE_BLOCK)
        lo, hi = _unpack_rows(x_ref[rows, :])
        lo = lo.astype(BF16)
        hi = hi.astype(BF16)
        g = (jnp.dot(lo, wgf[slot, :PACKED, :], preferred_element_type=F32)
             + jnp.dot(hi, wgf[slot, PACKED:, :], preferred_element_type=F32))
        u = (jnp.dot(lo, wuf[slot, :PACKED, :], preferred_element_type=F32)
             + jnp.dot(hi, wuf[slot, PACKED:, :], preferred_element_type=F32))
        h = (g * _sigmoid(g) * u).astype(BF16)
        o_ref[rows, :] = _pack_rows(jnp.dot(h, wdf[slot], preferred_element_type=F32))

    def run(s):
        j = step * EXPERT_GROUP + s
        e = be_ref[j]
        first = jnp.logical_or(j == 0, e != be_ref[jnp.maximum(j - 1, 0)])

        @pl.when(first)
        def _():
            grp = grp_ref[0]
            slot = grp % W_SLOTS
            for cp in weight_copies(e, slot):
                cp.wait()
            nxt = ge_ref[grp + W_AHEAD]

            @pl.when(nxt >= 0)
            def _():
                for cp in weight_copies(nxt, (grp + W_AHEAD) % W_SLOTS):
                    cp.start()
            grp_ref[0] = grp + 1

        def same(k):
            return (s + k < EXPERT_GROUP) & (j + k < n_used) & (be_ref[jnp.minimum(j + k, n_blocks - 1)] == e)
        take = jnp.int32(1)
        for n in EXPERT_RUNS[1:]:
            ok = same(n - 1)
            for k in range(1, n - 1):
                ok = ok & same(k)
            take = jnp.where(ok, n, take)
        slot = (grp_ref[0] + W_SLOTS - 1) % W_SLOTS
        for n in EXPERT_RUNS:
            @pl.when(take == n)
            def _(n=n):
                swiglu(s, n, slot)
        return s + take

    lax.while_loop(lambda s: (s < EXPERT_GROUP) & (step * EXPERT_GROUP + s < n_used), run, jnp.int32(0))


def _experts(xs, block_expert, n_used, group_expert, w_gate, w_up, w_down):
    cap = xs.shape[0]
    n_blocks = cap // MOE_BLOCK
    assert n_blocks % EXPERT_GROUP == 0
    rows = EXPERT_GROUP * MOE_BLOCK
    last = lambda j, be, nu, ge: jnp.minimum(j, (nu[0] - 1) // EXPERT_GROUP)
    hbm = pl.BlockSpec(memory_space=pl.ANY)
    gs = pltpu.PrefetchScalarGridSpec(
        num_scalar_prefetch=3,
        grid=(n_blocks // EXPERT_GROUP,),
        in_specs=[pl.BlockSpec((rows, PACKED), lambda j, be, nu, ge: (last(j, be, nu, ge), 0)), hbm, hbm, hbm],
        out_specs=pl.BlockSpec((rows, PACKED), lambda j, be, nu, ge: (last(j, be, nu, ge), 0)),
        scratch_shapes=[pltpu.VMEM((W_SLOTS, D_MODEL, D_EXPERT), F32),
                        pltpu.VMEM((W_SLOTS, D_MODEL, D_EXPERT), F32),
                        pltpu.VMEM((W_SLOTS, D_EXPERT, D_MODEL), F32),
                        pltpu.SMEM((1,), jnp.int32),
                        pltpu.SemaphoreType.DMA((W_SLOTS, 3))],
    )
    return pl.pallas_call(
        _expert_kernel,
        grid_spec=gs,
        out_shape=jax.ShapeDtypeStruct((cap, PACKED), U32),
        compiler_params=_cparams(1),
        name="experts",
    )(block_expert, n_used, group_expert, xs, w_gate, w_up, w_down)


CB_TM = RT_TM


def _combine_kernel(x1_ref, gt_ref, y2_ref, o_ref):
    g = gt_ref[...]
    lo1, hi1 = _unpack_rows(y2_ref[0, 0])
    lo2, hi2 = _unpack_rows(y2_ref[0, 1])
    o_ref[:, :PACKED] = x1_ref[:, :PACKED] + g[:, 0:1] * lo1 + g[:, 1:2] * lo2
    o_ref[:, PACKED:] = x1_ref[:, PACKED:] + g[:, 0:1] * hi1 + g[:, 1:2] * hi2


def _combine(x1, gates, y2):
    T = x1.shape[0]
    nt = T // CB_TM
    return pl.pallas_call(
        _combine_kernel,
        grid=(nt,),
        in_specs=[pl.BlockSpec((CB_TM, D_MODEL), lambda i: (i, 0)),
                  pl.BlockSpec((CB_TM, RINFO), lambda i: (i, 0)),
                  pl.BlockSpec((1, TOP_K, CB_TM, PACKED), lambda i: (i, 0, 0, 0))],
        out_specs=pl.BlockSpec((CB_TM, D_MODEL), lambda i: (i, 0)),
        out_shape=jax.ShapeDtypeStruct((T, D_MODEL), F32),
        compiler_params=_cparams(1),
        name="combine",
    )(x1, gates, y2)


def _layer(x, rel_bias, ln1, w_in, q_norm, k_norm, attn_sink, conv_w, conv_b, lru_wa, lru_ba, lru_wi, lru_bi,
           lru_lambda, out_norm_attn, out_norm_lru, w_out, ln2, w_group, b_group, w_er, b_er, w_gate, w_up, w_down):
    B, S, D = x.shape
    T = B * S
    x2 = x.reshape(T, D)
    q, kv, xr, gr = _in_proj(x2, ln1, w_in, q_norm, k_norm)
    attn_n = _attention(q.reshape(B, S, ATTN_WIDTH), kv.reshape(B, S, 2 * KV_WIDTH), rel_bias, attn_sink,
                        out_norm_attn)
    lru_n = _rglru(xr.reshape(B, S, LRU_WIDTH), gr.reshape(B, S, LRU_WIDTH), conv_w, conv_b,
                   lru_wa, lru_ba, lru_wi, lru_bi, lru_lambda, out_norm_lru)
    x1, h2, gates, ei, cnt = _out_route(attn_n.reshape(T, ATTN_WIDTH), lru_n.reshape(T, LRU_WIDTH), x2, w_out, ln2,
                                        w_group, b_group, w_er, b_er)
    cap = _moe_cap(T)
    dest, pstart, block_expert, n_used, group_expert = _layout(ei, cnt, cap // MOE_BLOCK)
    xs = _padfill(_sc_dispatch(h2, dest, cap), pstart, cnt)
    yb = _experts(xs, block_expert, n_used, group_expert, w_gate, w_up, w_down)
    out = _combine(x1, gates, _sc_gather(yb, dest))
    return out.reshape(B, S, D)


def kernel(x, rel_bias, ln1, w_in, q_norm, k_norm, attn_sink, conv_w, conv_b, lru_wa, lru_ba, lru_wi, lru_bi,
           lru_lambda, out_norm_attn, out_norm_lru, w_out, ln2, w_group, b_group, w_expert_router, b_expert_router,
           w_gate, w_up, w_down):
    depth = ln1.shape[0]
    for l in range(depth):
        x = _layer(x, rel_bias, ln1[l], w_in[l], q_norm[l], k_norm[l], attn_sink[l], conv_w[l], conv_b[l],
                   lru_wa[l], lru_ba[l], lru_wi[l], lru_bi[l], lru_lambda[l], out_norm_attn[l], out_norm_lru[l],
                   w_out[l], ln2[l], w_group[l], b_group[l], w_expert_router[l], b_expert_router[l],
                   w_gate[l], w_up[l], w_down[l])
    return x
```

```python
import functools
import math

import jax
import jax.numpy as jnp
import numpy as np
from jax import lax
from jax.experimental import pallas as pl
from jax.experimental.pallas import tpu as pltpu
from jax.experimental.pallas import tpu_sc as plsc

D_MODEL = 1024
N_HEADS = 8
N_KV_HEADS = 2
HEAD_DIM = 64
Q_PER_KV = N_HEADS // N_KV_HEADS
ATTN_WIDTH = N_HEADS * HEAD_DIM
KV_WIDTH = N_KV_HEADS * HEAD_DIM
WINDOW = 128
BLOCK = 128
NUM_BUCKETS = 32
MAX_DISTANCE = 128
LRU_WIDTH = D_MODEL - ATTN_WIDTH
LRU_BLOCKS = 8
LRU_BLOCK_DIM = LRU_WIDTH // LRU_BLOCKS
LRU_C = 8.0
CONV_W = 4
CONV_LEFT = 2
N_GROUPS = 4
EXPERTS_PER_GROUP = 8
N_EXPERTS = N_GROUPS * EXPERTS_PER_GROUP
TOP_K = 2
D_EXPERT = 512
MOE_BLOCK = 128
EPS = 1e-6
NEG_INF = -1e30

LANES = 128
SUBLANES = 8
VMEM_LIMIT = 56 * 1024 * 1024
LRU_VMEM_LIMIT = 62 * 1024 * 1024

F32 = jnp.float32
BF16 = jnp.bfloat16
LOG2E = math.log2(math.e)


def _cparams(n_axes, vmem=VMEM_LIMIT):
    return pltpu.CompilerParams(dimension_semantics=("arbitrary",) * n_axes, vmem_limit_bytes=vmem)


def _rms(x, gain):
    return x * lax.rsqrt(jnp.mean(x * x, axis=-1, keepdims=True) + EPS) * gain


U32 = jnp.uint32
HI_MASK = 0xFFFF0000
PACKED = D_MODEL // 2


def _pack_rows(x):
    return _pack_rounded(x.astype(BF16).astype(F32))


def _pack_rounded(xb):
    h = xb.shape[1] // 2
    lo = lax.bitcast_convert_type(xb[:, :h], U32) >> 16
    hi = lax.bitcast_convert_type(xb[:, h:], U32) & jnp.uint32(HI_MASK)
    return lo | hi


def _unpack_rows(p):
    lo = lax.bitcast_convert_type(p << 16, F32)
    hi = lax.bitcast_convert_type(p & jnp.uint32(HI_MASK), F32)
    return lo, hi


IN_TM = 1024


def _head_rms(x, n_heads, gain):
    head = lax.broadcasted_iota(jnp.int32, (1, n_heads * HEAD_DIM), 1) // HEAD_DIM
    x2 = x * x
    scale = jnp.zeros_like(x)
    for h in range(n_heads):
        ms = jnp.sum(jnp.where(head == h, x2, 0.0), axis=-1, keepdims=True) * (1.0 / HEAD_DIM)
        scale = jnp.where(head == h, lax.rsqrt(ms + EPS), scale)
    return x * scale * gain


def _in_proj_kernel(x_ref, g_ref, w_ref, qg_ref, kg_ref, q_ref, kv_ref, xr_ref, gr_ref, wb_ref):
    @pl.when(pl.program_id(0) == 0)
    def _():
        wb_ref[...] = w_ref[...].astype(BF16)

    h = _rms(x_ref[...], g_ref[...]).astype(BF16)
    c_k = ATTN_WIDTH
    c_v = c_k + KV_WIDTH
    c_x = c_v + KV_WIDTH
    c_g = c_x + LRU_WIDTH
    q = jnp.dot(h, wb_ref[:, :c_k], preferred_element_type=F32)
    q_ref[...] = _head_rms(q, N_HEADS, qg_ref[...]).astype(BF16)
    k = jnp.dot(h, wb_ref[:, c_k:c_v], preferred_element_type=F32)
    kv_ref[:, :KV_WIDTH] = _head_rms(k, N_KV_HEADS, kg_ref[...]).astype(BF16)
    kv_ref[:, KV_WIDTH:] = jnp.dot(h, wb_ref[:, c_v:c_x], preferred_element_type=F32).astype(BF16)
    xr_ref[...] = jnp.dot(h, wb_ref[:, c_x:c_g], preferred_element_type=F32)
    gr_ref[...] = jnp.dot(h, wb_ref[:, c_g:], preferred_element_type=F32)


def _in_proj(x2, ln1, w_in, q_gain, k_gain):
    T = x2.shape[0]
    n_in = w_in.shape[1]
    row = lambda w: pl.BlockSpec((IN_TM, w), lambda i: (i, 0))
    qg = (jnp.tile(q_gain.astype(F32), N_HEADS) * (HEAD_DIM ** -0.5 * LOG2E)).reshape(1, ATTN_WIDTH)
    kg = jnp.tile(k_gain.astype(F32), N_KV_HEADS).reshape(1, KV_WIDTH)
    return pl.pallas_call(
        _in_proj_kernel,
        grid=(T // IN_TM,),
        in_specs=[row(D_MODEL),
                  pl.BlockSpec((1, D_MODEL), lambda i: (0, 0)),
                  pl.BlockSpec((D_MODEL, n_in), lambda i: (0, 0)),
                  pl.BlockSpec((1, ATTN_WIDTH), lambda i: (0, 0)),
                  pl.BlockSpec((1, KV_WIDTH), lambda i: (0, 0))],
        out_specs=[row(ATTN_WIDTH), row(2 * KV_WIDTH), row(LRU_WIDTH), row(LRU_WIDTH)],
        out_shape=[jax.ShapeDtypeStruct((T, ATTN_WIDTH), BF16),
                   jax.ShapeDtypeStruct((T, 2 * KV_WIDTH), BF16),
                   jax.ShapeDtypeStruct((T, LRU_WIDTH), F32),
                   jax.ShapeDtypeStruct((T, LRU_WIDTH), F32)],
        scratch_shapes=[pltpu.VMEM((D_MODEL, n_in), BF16)],
        compiler_params=_cparams(1),
        name="in_proj",
    )(x2, ln1.reshape(1, D_MODEL), w_in, qg, kg)


def _t5_bucket(rel):
    half = NUM_BUCKETS // 2
    max_exact = half // 2
    base = jnp.where(rel > 0, half, 0)
    n = jnp.abs(rel)
    nf = jnp.maximum(n, 1).astype(jnp.float32)
    large = max_exact + (jnp.log(nf / max_exact) / math.log(MAX_DISTANCE / max_exact)
                         * (half - max_exact)).astype(jnp.int32)
    large = jnp.minimum(large, half - 1)
    return base + jnp.where(n < max_exact, n, large)


HEAD_PAIRS = Q_PER_KV // 2
EDGE_VARIANTS = 3


def _fill_bias_table(rb_ref, bucket_ref, band_ref, o_ref):
    bucket = bucket_ref[...]
    band = band_ref[...] > 0
    col = lax.broadcasted_iota(jnp.int32, bucket.shape, 1)
    valid = (band & (col >= BLOCK), band, band & (col < 2 * BLOCK))
    for h in range(N_HEADS):
        acc = jnp.zeros(bucket.shape, F32)
        for b in range(NUM_BUCKETS):
            acc = jnp.where(bucket == b, rb_ref[b, h], acc)
        kv, g = divmod(h, Q_PER_KV)
        pair, parity = divmod(g, 2)
        for var in range(EDGE_VARIANTS):
            o_ref[var, kv, parity, pair * BLOCK:(pair + 1) * BLOCK, :] = jnp.where(valid[var], acc * LOG2E, NEG_INF)


def _attn_kernel(sink_ref, rb_ref, q_ref, kp_ref, kc_ref, kn_ref, bucket_ref, band_ref, og_ref, o_ref, bias_ref):
    n = pl.program_id(1)

    @pl.when((pl.program_id(0) == 0) & (n == 0))
    def _():
        _fill_bias_table(rb_ref, bucket_ref, band_ref, bias_ref)

    kv_all = jnp.concatenate([kp_ref[0], kc_ref[0], kn_ref[0]], axis=0)
    for qb in range(ATTN_QB):
        variant = 1
        if qb == 0:
            variant = jnp.where(n == 0, 0, 1)
        if qb == ATTN_QB - 1:
            variant = jnp.where(n == pl.num_programs(1) - 1, 2, variant)
        out = _attn_block(q_ref[0, qb * BLOCK:(qb + 1) * BLOCK, :], kv_all[qb * BLOCK:(qb + 3) * BLOCK, :],
                          lambda kv, parity: bias_ref[variant, kv, parity], sink_ref)
        o_ref[0, qb * BLOCK:(qb + 1) * BLOCK, :] = _rms(out, og_ref[...]).astype(o_ref.dtype)


def _attn_block(q, kvw, bias, sink_ref):
    low = lax.broadcasted_iota(jnp.int32, (1, LANES), 1) < HEAD_DIM
    swap = lambda slab: pltpu.roll(slab.astype(F32), HEAD_DIM, 1).astype(BF16)
    kslab, vslab = kvw[:, :KV_WIDTH], kvw[:, KV_WIDTH:]
    kslab_sw, vslab_sw = swap(kslab), swap(vslab)
    rowi = lax.broadcasted_iota(jnp.int32, (HEAD_PAIRS * BLOCK, 1), 0)
    combos = [(kv, parity) for kv in range(N_KV_HEADS) for parity in range(2)]
    scores, vzs, sinks = [], [], []
    for kv, parity in combos:
        ks, vs = (kslab, vslab) if (kv == 0) == (parity == 0) else (kslab_sw, vslab_sw)
        keep = low if parity == 0 else jnp.logical_not(low)
        kz = jnp.where(keep, ks, jnp.zeros_like(ks))
        vzs.append(jnp.where(keep, vs, jnp.zeros_like(vs)))
        base = kv * Q_PER_KV * HEAD_DIM
        qpair = jnp.concatenate([q[:, base + j * LANES:base + (j + 1) * LANES] for j in range(HEAD_PAIRS)], axis=0)
        s = lax.dot_general(qpair, kz, (((1,), (1,)), ((), ())), preferred_element_type=F32)
        scores.append(s + bias(kv, parity))
        sink = jnp.zeros((HEAD_PAIRS * BLOCK, 1), F32)
        for j in range(HEAD_PAIRS):
            sink = jnp.where(rowi // BLOCK == j, sink_ref[kv * Q_PER_KV + 2 * j + parity], sink)
        sinks.append(sink)
    probs, inv = [], []
    for s, sink in zip(scores, sinks):
        m = jnp.maximum(jnp.max(s, axis=-1, keepdims=True), sink)
        p = jnp.exp2(s - m)
        inv.append(1.0 / (jnp.sum(p, axis=-1, keepdims=True) + jnp.exp2(sink - m)))
        probs.append(p.astype(BF16))
    outs = [jnp.dot(p, vz, preferred_element_type=F32) * r for p, vz, r in zip(probs, vzs, inv)]
    cols = []
    for kv in range(N_KV_HEADS):
        acc = outs[2 * kv] + outs[2 * kv + 1]
        cols += [acc[j * BLOCK:(j + 1) * BLOCK, :] for j in range(HEAD_PAIRS)]
    return jnp.concatenate(cols, axis=1)


ATTN_QB = 4


def _attention(q, kv, rel_bias, sink, out_gain):
    B, S, _ = q.shape
    nb = S // BLOCK
    assert ATTN_QB >= 2 and nb % ATTN_QB == 0, "a step's first and last query blocks must be distinct"
    ns = nb // ATTN_QB
    rows = ATTN_QB * BLOCK
    qi = jnp.arange(BLOCK, dtype=jnp.int32)
    kj = jnp.arange(3 * BLOCK, dtype=jnp.int32)
    rel = kj[None, :] - BLOCK - qi[:, None]
    bucket = _t5_bucket(rel).astype(jnp.int32)
    band = (jnp.abs(rel) <= WINDOW).astype(jnp.int32)
    kvspec = lambda f: pl.BlockSpec((1, BLOCK, 2 * KV_WIDTH), f)
    smem = pl.BlockSpec(memory_space=pltpu.SMEM)
    geom = pl.BlockSpec((BLOCK, 3 * BLOCK), lambda b, n: (0, 0))
    return pl.pallas_call(
        _attn_kernel,
        grid=(B, ns),
        in_specs=[smem, smem,
                  pl.BlockSpec((1, rows, ATTN_WIDTH), lambda b, n: (b, n, 0)),
                  kvspec(lambda b, n: (b, jnp.maximum(n * ATTN_QB - 1, 0), 0)),
                  pl.BlockSpec((1, rows, 2 * KV_WIDTH), lambda b, n: (b, n, 0)),
                  kvspec(lambda b, n: (b, jnp.minimum((n + 1) * ATTN_QB, nb - 1), 0)),
                  geom, geom,
                  pl.BlockSpec((1, ATTN_WIDTH), lambda b, n: (0, 0))],
        out_specs=pl.BlockSpec((1, rows, ATTN_WIDTH), lambda b, n: (b, n, 0)),
        out_shape=jax.ShapeDtypeStruct((B, S, ATTN_WIDTH), BF16),
        scratch_shapes=[pltpu.VMEM((EDGE_VARIANTS, N_KV_HEADS, 2, HEAD_PAIRS * BLOCK, 3 * BLOCK), F32)],
        compiler_params=_cparams(2),
        name="attention",
    )(sink.astype(F32) * LOG2E, rel_bias.astype(F32), q, kv, kv, kv, bucket, band, out_gain.reshape(1, ATTN_WIDTH))


LRU_TC = 128
LRU_PITCH = LRU_TC + SUBLANES
LRU_SLABS = LRU_WIDTH // LANES
HALO = SUBLANES


def _softplus(x):
    return jnp.maximum(x, 0.0) + jnp.log(1.0 + jnp.exp(-jnp.abs(x)))


def _gelu_tanh(x):
    k = math.sqrt(2.0 / math.pi)
    hx = 0.5 * x
    return hx + hx * jnp.tanh(x * (k + (k * 0.044715) * (x * x)))


def _sigmoid(x):
    return 0.5 + 0.5 * jnp.tanh(0.5 * x)


def _rglru_kernel(xr_ref, xp_ref, xn_ref, gr_ref, cw_ref, cb_ref, wa_ref, wi_ref, ba_ref, bi_ref, lam_ref, og_ref,
                  o_ref, sx_ref, a_ref, u_ref, h_ref, carry_ref, hf_ref, xcs_ref, wg_ref, bg_ref, k_ref):
    p = pl.program_id(0)
    i = pl.program_id(1)
    nc = pl.num_programs(1)
    c = i + p * (nc - 1 - 2 * i)
    B = xr_ref.shape[0]
    TC = LRU_TC

    @pl.when(i == 0)
    def _():
        carry_ref[...] = jnp.zeros_like(carry_ref)
        wg_ref[...] = jnp.zeros_like(wg_ref)
        for sel, w_ref in enumerate((wa_ref, wi_ref)):
            for h in range(LRU_BLOCKS):
                lo = h * LRU_BLOCK_DIM
                wg_ref[lo:lo + LRU_BLOCK_DIM, sel * LRU_WIDTH + lo:sel * LRU_WIDTH + lo + LRU_BLOCK_DIM] = (
                    0.5 * w_ref[0, h]).astype(BF16)
        row = pl.ds(p, 1)
        bg_ref[:, :LRU_WIDTH] = 0.5 * ba_ref[row, :]
        bg_ref[:, LRU_WIDTH:] = 0.5 * bi_ref[row, :]
        k_ref[...] = (-0.5 * LRU_C * math.log2(math.e)) * _softplus(-lam_ref[row, :])

    def gates_and_scan(xc2, backward):
        g = jnp.dot(xc2.astype(BF16), wg_ref[...], preferred_element_type=F32) + bg_ref[...]
        ta = jnp.tanh(g[:, :LRU_WIDTH])
        ig = 0.5 + 0.5 * jnp.tanh(g[:, LRU_WIDTH:])
        a = jnp.exp2((1.0 + ta) * k_ref[...])
        z = 1.0 - a * a
        u = z * lax.rsqrt(jnp.maximum(z, 1e-30)) * ig * xc2
        for b in range(B):
            for s in range(LRU_SLABS):
                a_ref[s, b * LRU_PITCH:b * LRU_PITCH + TC, :] = a[b * TC:(b + 1) * TC, s * LANES:(s + 1) * LANES]
                u_ref[s, b * LRU_PITCH:b * LRU_PITCH + TC, :] = u[b * TC:(b + 1) * TC, s * LANES:(s + 1) * LANES]

        def step(k, hs):
            t = TC - 1 - k if backward else k
            out = []
            for s in range(LRU_SLABS):
                idx = pl.ds(t, B, stride=LRU_PITCH)
                hn = a_ref[s, idx, :] * hs[s] + u_ref[s, idx, :]
                h_ref[s, idx, :] = hn
                out.append(hn)
            return tuple(out)

        hs = lax.fori_loop(0, TC, step, tuple(carry_ref[s] for s in range(LRU_SLABS)), unroll=8)
        for s in range(LRU_SLABS):
            carry_ref[s] = hs[s]

    @pl.when(p == 0)
    def _():
        sx_ref[:, HALO:HALO + TC, :] = xr_ref[...]
        sx_ref[:, 0:HALO, :] = jnp.where(c > 0, xp_ref[...], 0.0)
        sx_ref[:, HALO + TC:, :] = jnp.where(c < nc - 1, xn_ref[...], 0.0)
        xc = cb_ref[...][None]
        for j in range(CONV_W):
            off = HALO + j - CONV_LEFT
            xc = xc + cw_ref[j:j + 1, :][None] * sx_ref[:, off:off + TC, :]
        xc2 = xc.reshape(B * TC, LRU_WIDTH)
        xcs_ref[c] = xc2.astype(xcs_ref.dtype)
        gates_and_scan(xc2, backward=False)
        for b in range(B):
            for s in range(LRU_SLABS):
                hf_ref[c, s, b * TC:(b + 1) * TC, :] = h_ref[s, b * LRU_PITCH:b * LRU_PITCH + TC, :].astype(hf_ref.dtype)

    @pl.when(p == 1)
    def _():
        gates_and_scan(xcs_ref[c].astype(F32), backward=True)
        for b in range(B):
            hsum = jnp.concatenate(
                [h_ref[s, b * LRU_PITCH:b * LRU_PITCH + TC, :] + hf_ref[c, s, b * TC:(b + 1) * TC, :].astype(F32)
                 for s in range(LRU_SLABS)], axis=1)
            y = hsum * _gelu_tanh(gr_ref[b])
            o_ref[b] = _rms(y, og_ref[...]).astype(o_ref.dtype)


def _rglru(xr, gr, conv_w, conv_b, w_a, b_a, w_i, b_i, lam, out_gain):
    B, S, W = xr.shape
    nc = S // LRU_TC
    hb = LRU_TC // HALO
    fwd = lambda p, i: jnp.where(p == 0, i, nc - 1)
    bwd = lambda p, i: nc - 1 - p * i
    full2 = lambda shape: pl.BlockSpec(shape, lambda p, i: (0,) * len(shape))
    wblock = pl.BlockSpec((1, LRU_BLOCKS, LRU_BLOCK_DIM, LRU_BLOCK_DIM), lambda p, i: (p, 0, 0, 0))
    return pl.pallas_call(
        _rglru_kernel,
        grid=(2, nc),
        in_specs=[pl.BlockSpec((B, LRU_TC, W), lambda p, i: (0, fwd(p, i), 0)),
                  pl.BlockSpec((B, HALO, W), lambda p, i: (0, jnp.maximum(fwd(p, i) * hb - 1, 0), 0)),
                  pl.BlockSpec((B, HALO, W), lambda p, i: (0, jnp.minimum((fwd(p, i) + 1) * hb, S // HALO - 1), 0)),
                  pl.BlockSpec((B, LRU_TC, W), lambda p, i: (0, bwd(p, i), 0)),
                  full2((CONV_W, W)),
                  full2((1, W)),
                  wblock, wblock,
                  full2((2, W)), full2((2, W)), full2((2, W)),
                  full2((1, W))],
        out_specs=pl.BlockSpec((B, LRU_TC, W), lambda p, i: (0, bwd(p, i), 0)),
        out_shape=jax.ShapeDtypeStruct((B, S, W), BF16),
        scratch_shapes=[pltpu.VMEM((B, LRU_TC + 2 * HALO, W), F32),
                        pltpu.VMEM((LRU_SLABS, B * LRU_PITCH, LANES), F32),
                        pltpu.VMEM((LRU_SLABS, B * LRU_PITCH, LANES), F32),
                        pltpu.VMEM((LRU_SLABS, B * LRU_PITCH, LANES), F32),
                        pltpu.VMEM((LRU_SLABS, B, LANES), F32),
                        pltpu.VMEM((nc, LRU_SLABS, B * LRU_TC, LANES), BF16),
                        pltpu.VMEM((nc, B * LRU_TC, W), BF16),
                        pltpu.VMEM((W, 2 * W), BF16),
                        pltpu.VMEM((1, 2 * W), F32),
                        pltpu.VMEM((1, W), F32)],
        compiler_params=_cparams(2, LRU_VMEM_LIMIT),
        name="rglru",
    )(xr, xr, xr, gr, conv_w.astype(F32), conv_b.reshape(1, W).astype(F32), w_a.astype(F32), w_i.astype(F32),
      b_a.astype(F32), b_i.astype(F32), lam.astype(F32), out_gain.reshape(1, W).astype(F32))


RT_TM = 1024
RT_PARTS = 4
RT_COLS = LANES
RT_ROWS = 48
RINFO = SUBLANES


def _split_bf16(x):
    hi = x.astype(BF16)
    lo = (x - hi.astype(F32)).astype(BF16)
    return hi, lo


def _route_kernel(an_ref, ln_ref, x_ref, wo_ref, g2_ref, wr_ref, br_ref,
                  x1_ref, h2_ref, gt_ref, ei_ref, cnt_ref, wob_ref, wrb_ref, tri_ref, run_ref, runc_ref):
    @pl.when(pl.program_id(0) == 0)
    def _():
        wob_ref[...] = wo_ref[...].astype(BF16)
        hi, lo = _split_bf16(wr_ref[...])
        wrb_ref[:RT_ROWS, :] = hi
        wrb_ref[RT_ROWS:, :] = lo
        r = lax.broadcasted_iota(jnp.int32, (RT_TM, RT_TM), 0)
        cidx = lax.broadcasted_iota(jnp.int32, (RT_TM, RT_TM), 1)
        tri_ref[...] = (r < cidx).astype(BF16)
        run_ref[...] = jnp.zeros_like(run_ref)
        runc_ref[...] = jnp.zeros_like(runc_ref)

    nt_dims = (((1,), (1,)), ((), ()))
    part = RT_TM // RT_PARTS
    x1s = []
    for r in range(RT_PARTS):
        rows = slice(r * part, (r + 1) * part)
        x1 = (x_ref[rows, :]
              + jnp.dot(an_ref[rows, :], wob_ref[:ATTN_WIDTH, :], preferred_element_type=F32)
              + jnp.dot(ln_ref[rows, :], wob_ref[ATTN_WIDTH:, :], preferred_element_type=F32))
        x1_ref[rows, :] = x1
        x1s.append(x1)
    splits = []
    for r, x1 in enumerate(x1s):
        h2 = _rms(x1, g2_ref[...])
        hi = h2.astype(BF16)
        hi_f = hi.astype(F32)
        h2_ref[r * part:(r + 1) * part, :] = _pack_rounded(hi_f)
        splits.append((hi, (h2 - hi_f).astype(BF16)))
    logits = []
    for hi, lo in splits:
        t1 = lax.dot_general(wrb_ref[...], hi, nt_dims, preferred_element_type=F32)
        t2 = lax.dot_general(wrb_ref[:RT_ROWS, :], lo, nt_dims, preferred_element_type=F32)
        logits.append(t1[:RT_ROWS] + t1[RT_ROWS:] + t2)
    logit = jnp.concatenate(logits, axis=1) + br_ref[...]

    sub = lax.broadcasted_iota(jnp.int32, (SUBLANES, RT_TM), 0)
    first_min = lambda hit: jnp.min(jnp.where(hit, sub, SUBLANES), axis=0, keepdims=True)
    is_g = sub < N_GROUPS
    gl = jnp.where(is_g, logit[:SUBLANES], -jnp.inf)
    gm = jnp.max(gl, axis=0, keepdims=True)
    gidx = first_min(gl == gm)
    g_p = 1.0 / jnp.sum(jnp.where(is_g, jnp.exp(logit[:SUBLANES] - gm), 0.0), axis=0, keepdims=True)
    el = logit[SUBLANES:2 * SUBLANES]
    for g in range(1, N_GROUPS):
        el = jnp.where(gidx == g, logit[(g + 1) * SUBLANES:(g + 2) * SUBLANES], el)
    m1 = jnp.max(el, axis=0, keepdims=True)
    i1 = first_min(el == m1)
    el2 = jnp.where(sub == i1, -jnp.inf, el)
    m2 = jnp.max(el2, axis=0, keepdims=True)
    i2 = first_min(el2 == m2)
    t = jnp.exp(m2 - m1)
    gate1 = g_p / (1.0 + t)
    gate2 = g_p * t / (1.0 + t)
    e1 = gidx * EXPERTS_PER_GROUP + i1
    e2 = gidx * EXPERTS_PER_GROUP + i2

    erow = lax.broadcasted_iota(jnp.int32, (N_EXPERTS, RT_TM), 0)
    oh1 = erow == e1
    oh2 = erow == e2
    oh = (oh1 | oh2).astype(F32)
    ohb = oh.astype(BF16)
    cum = jnp.dot(ohb, tri_ref[...], preferred_element_type=F32) + runc_ref[...]
    rank1 = jnp.sum(jnp.where(oh1, cum, 0.0), axis=0, keepdims=True)
    rank2 = jnp.sum(jnp.where(oh2, cum, 0.0), axis=0, keepdims=True)
    runc_ref[...] = runc_ref[...] + jnp.sum(oh, axis=1, keepdims=True)
    tile_cnt = lax.dot_general(jnp.ones((SUBLANES, RT_TM), BF16), ohb, nt_dims, preferred_element_type=F32)
    run_ref[:, :N_EXPERTS] = run_ref[:, :N_EXPERTS] + tile_cnt[0:1]
    cnt_ref[...] = run_ref[...].astype(jnp.int32)

    rows = [e1, e2, rank1.astype(jnp.int32), rank2.astype(jnp.int32)]
    ei = jnp.zeros((RINFO, RT_TM), jnp.int32)
    for k, v in enumerate(rows):
        ei = jnp.where(sub == k, v, ei)
    ei_ref[0] = ei
    gt_ref[...] = jnp.where(sub == 0, gate1, jnp.where(sub == 1, gate2, 0.0)).T


def _out_route(attn_n, lru_n, x2, w_out, ln2, w_group, b_group, w_er, b_er):
    T = x2.shape[0]
    pad_g = SUBLANES - N_GROUPS
    wr = jnp.concatenate([jnp.pad(w_group.T, ((0, pad_g), (0, 0))),
                          jnp.transpose(w_er, (0, 2, 1)).reshape(N_EXPERTS, D_MODEL)], axis=0)
    wr = jnp.pad(wr, ((0, RT_ROWS - wr.shape[0]), (0, 0))).astype(F32)
    br = jnp.concatenate([jnp.pad(b_group, (0, pad_g)), b_er.reshape(-1)])
    br = jnp.pad(br, (0, RT_ROWS - br.shape[0])).reshape(RT_ROWS, 1).astype(F32)
    row = lambda w: pl.BlockSpec((RT_TM, w), lambda i: (i, 0))
    const = lambda shape: pl.BlockSpec(shape, lambda i: (0, 0))
    return pl.pallas_call(
        _route_kernel,
        grid=(T // RT_TM,),
        in_specs=[row(ATTN_WIDTH), row(LRU_WIDTH), row(D_MODEL), const((D_MODEL, D_MODEL)), const((1, D_MODEL)),
                  const((RT_ROWS, D_MODEL)), const((RT_ROWS, 1))],
        out_specs=[row(D_MODEL), row(PACKED), row(RINFO),
                   pl.BlockSpec((1, RINFO, RT_TM), lambda i: (i, 0, 0)), const((1, RT_COLS))],
        out_shape=[jax.ShapeDtypeStruct((T, D_MODEL), F32),
                   jax.ShapeDtypeStruct((T, PACKED), U32),
                   jax.ShapeDtypeStruct((T, RINFO), F32),
                   jax.ShapeDtypeStruct((T // RT_TM, RINFO, RT_TM), jnp.int32),
                   jax.ShapeDtypeStruct((1, RT_COLS), jnp.int32)],
        scratch_shapes=[pltpu.VMEM((D_MODEL, D_MODEL), BF16),
                        pltpu.VMEM((2 * RT_ROWS, D_MODEL), BF16),
                        pltpu.VMEM((RT_TM, RT_TM), BF16),
                        pltpu.VMEM((1, RT_COLS), F32),
                        pltpu.VMEM((N_EXPERTS, 1), F32)],
        compiler_params=_cparams(1),
        name="out_route",
    )(attn_n, lru_n, x2, w_out, ln2.reshape(1, D_MODEL).astype(F32), wr, br)


def _moe_cap(T):
    A = T * TOP_K
    return ((A + MOE_BLOCK - 1) // MOE_BLOCK) * MOE_BLOCK + N_EXPERTS * MOE_BLOCK


PAD_BITS = tuple(1 << b for b in reversed(range(3, MOE_BLOCK.bit_length() - 1)))


def _layout_kernel(cnt_ref, ei_ref, dest_ref, pstart, be_ref, nu_ref, ge_ref):
    n_blocks = be_ref.shape[0]

    def lay(e, carry):
        start, blk, grp = carry
        pstart[e] = start
        nb = (cnt_ref[0, e] + MOE_BLOCK - 1) // MOE_BLOCK
        ge_ref[grp] = e

        def fill(k, c):
            be_ref[blk + k] = e
            return c
        lax.fori_loop(0, nb, fill, 0)
        return start + nb * MOE_BLOCK, blk + nb, grp + (nb > 0).astype(jnp.int32)
    _, used, groups = lax.fori_loop(0, N_EXPERTS, lay, (jnp.int32(0), jnp.int32(0), jnp.int32(0)))
    nu_ref[0] = used

    def tail(k, c):
        be_ref[k] = N_EXPERTS - 1
        return c
    lax.fori_loop(used, n_blocks, tail, 0)

    def no_group(k, c):
        ge_ref[k] = -1
        return c
    lax.fori_loop(groups, ge_ref.shape[0], no_group, 0)

    expert = ei_ref[:, 0:TOP_K, :]
    dest = ei_ref[:, TOP_K:2 * TOP_K, :]
    for e in range(N_EXPERTS):
        dest = dest + jnp.where(expert == e, pstart[e], 0)
    dest_ref[...] = dest


def _layout(ei, cnt, n_blocks):
    nt = ei.shape[0]
    smem = pl.BlockSpec(memory_space=pltpu.SMEM)
    vmem = pl.BlockSpec(memory_space=pltpu.VMEM)
    return pl.pallas_call(
        _layout_kernel,
        in_specs=[smem, vmem],
        out_specs=[vmem, smem, smem, smem, smem],
        out_shape=[jax.ShapeDtypeStruct((nt, TOP_K, RT_TM), jnp.int32),
                   jax.ShapeDtypeStruct((N_EXPERTS,), jnp.int32),
                   jax.ShapeDtypeStruct((n_blocks,), jnp.int32),
                   jax.ShapeDtypeStruct((1,), jnp.int32),
                   jax.ShapeDtypeStruct((N_EXPERTS + W_AHEAD,), jnp.int32)],
        name="layout",
    )(cnt, ei)


SC_CHUNK = 64
SC_BUFS = 3
SC_LEAD = SC_BUFS - 1


def _sc_workers():
    info = plsc.get_sparse_core_info()
    return info.num_cores, info.num_subcores


def _sc_ring(n_chunks, read, write):
    for c in range(min(SC_LEAD, n_chunks)):
        for cp in read(c):
            cp.start()
    reclaimed = set()
    for c in range(n_chunks):
        for cp in read(c):
            cp.wait()
        for cp in write(c):
            cp.start()
        nxt = c + SC_LEAD
        if nxt < n_chunks:
            if nxt - SC_BUFS >= 0:
                for cp in write(nxt - SC_BUFS):
                    cp.wait()
                reclaimed.add(nxt - SC_BUFS)
            for cp in read(nxt):
                cp.start()
    for c in range(n_chunks):
        if c not in reclaimed:
            for cp in write(c):
                cp.wait()


def _sc_dispatch(h2p, dest, cap):
    T = h2p.shape[0]
    nc, ns = _sc_workers()
    per_w = T // (nc * ns)
    n_ch = per_w // SC_CHUNK
    nt, _, tm = dest.shape
    assert nt * tm == T and tm % per_w == 0 and per_w % SC_CHUNK == 0
    idx = dest.reshape(nt, TOP_K, tm // per_w, per_w).transpose(0, 2, 1, 3).reshape(nc * ns, TOP_K * n_ch, SC_CHUNK)
    mesh = plsc.VectorSubcoreMesh(core_axis_name="c", subcore_axis_name="s")

    @functools.partial(
        pl.kernel, mesh=mesh,
        out_type=jax.ShapeDtypeStruct((cap, PACKED), U32),
        scratch_types=[pltpu.VMEM((TOP_K * n_ch, SC_CHUNK), jnp.int32),
                       pltpu.VMEM((SC_BUFS, SC_CHUNK, PACKED), U32),
                       pltpu.SemaphoreType.DMA((SC_BUFS,)),
                       pltpu.SemaphoreType.DMA((SC_BUFS,))])
    def scatter(src_hbm, idx_hbm, out_hbm, idx_v, rows_v, rsem, wsem):
        wid = lax.axis_index("s") * nc + lax.axis_index("c")
        base = pl.multiple_of(wid * per_w, per_w)
        pltpu.sync_copy(idx_hbm.at[wid], idx_v)

        def read(c):
            b = c % SC_BUFS
            return [pltpu.make_async_copy(src_hbm.at[pl.ds(base + c * SC_CHUNK, SC_CHUNK)], rows_v.at[b], rsem.at[b])]

        def write(c):
            b = c % SC_BUFS
            return [pltpu.make_async_copy(rows_v.at[b], out_hbm.at[idx_v.at[k * n_ch + c]], wsem.at[b])
                    for k in range(TOP_K)]
        _sc_ring(n_ch, read, write)

    return scatter(h2p, idx)


def _sc_gather(yb, dest):
    nt, _, tm = dest.shape
    nc, ns = _sc_workers()
    n_rows = nt * TOP_K * tm
    per_w = n_rows // (nc * ns)
    n_ch = per_w // SC_CHUNK
    assert per_w * nc * ns == n_rows and per_w % SC_CHUNK == 0
    mesh = plsc.VectorSubcoreMesh(core_axis_name="c", subcore_axis_name="s")

    @functools.partial(
        pl.kernel, mesh=mesh,
        out_type=jax.ShapeDtypeStruct((n_rows, PACKED), U32),
        scratch_types=[pltpu.VMEM((per_w,), jnp.int32),
                       pltpu.VMEM((SC_BUFS, SC_CHUNK, PACKED), U32),
                       pltpu.SemaphoreType.DMA((SC_BUFS,)),
                       pltpu.SemaphoreType.DMA((SC_BUFS,))])
    def gather(table_hbm, idx_hbm, out_hbm, idx_v, rows_v, gsem, wsem):
        wid = lax.axis_index("s") * nc + lax.axis_index("c")
        base = pl.multiple_of(wid * per_w, per_w)
        pltpu.sync_copy(idx_hbm.at[pl.ds(base, per_w)], idx_v)

        def read(c):
            b = c % SC_BUFS
            return [pltpu.make_async_copy(table_hbm.at[idx_v.at[pl.ds(c * SC_CHUNK, SC_CHUNK)]], rows_v.at[b], gsem.at[b])]

        def write(c):
            b = c % SC_BUFS
            return [pltpu.make_async_copy(rows_v.at[b], out_hbm.at[pl.ds(base + c * SC_CHUNK, SC_CHUNK)], wsem.at[b])]
        _sc_ring(n_ch, read, write)

    return gather(yb, dest.reshape(n_rows)).reshape(nt, TOP_K, tm, PACKED)


def _padfill_kernel(cnt_ref, pstart, xs_in, xs_ref, zeros, zsem):
    del xs_in

    def pad_copies(fn):
        for e in range(N_EXPERTS):
            cnt = cnt_ref[0, e]
            head = (-cnt) & (SUBLANES - 1)
            rest = ((-cnt) & (MOE_BLOCK - 1)) - head
            off = pstart[e] + cnt
            for k in range(SUBLANES - 1):
                @pl.when(k < head)
                def _(off=off, k=k):
                    fn(pltpu.make_async_copy(zeros.at[pl.ds(0, 1), :], xs_ref.at[pl.ds(off + k, 1), :], zsem))
            off = off + head
            for bit in PAD_BITS:
                @pl.when((rest & bit) != 0)
                def _(off=off, bit=bit):
                    fn(pltpu.make_async_copy(zeros.at[pl.ds(0, bit), :],
                                             xs_ref.at[pl.ds(pl.multiple_of(off, SUBLANES), bit), :], zsem))
                off = off + (rest & bit)

    zeros[...] = jnp.zeros_like(zeros)
    pad_copies(lambda cp: cp.start())
    pad_copies(lambda cp: cp.wait())


def _padfill(xs, pstart, cnt):
    smem = pl.BlockSpec(memory_space=pltpu.SMEM)
    hbm = pl.BlockSpec(memory_space=pl.ANY)
    return pl.pallas_call(
        _padfill_kernel,
        in_specs=[smem, smem, hbm],
        out_specs=hbm,
        out_shape=jax.ShapeDtypeStruct(xs.shape, xs.dtype),
        input_output_aliases={2: 0},
        scratch_shapes=[pltpu.VMEM((MOE_BLOCK // 2, PACKED), U32), pltpu.SemaphoreType.DMA(())],
        name="padfill",
    )(cnt, pstart, xs)


W_SLOTS = 3
W_AHEAD = W_SLOTS - 1
EXPERT_GROUP = 32
EXPERT_RUNS = (1, 2, 4, 8)


def _expert_kernel(be_ref, nu_ref, ge_ref, x_ref, wg_hbm, wu_hbm, wd_hbm, o_ref,
                   wgf, wuf, wdf, grp_ref, sems):
    step = pl.program_id(0)

    def weight_copies(e, slot):
        return (pltpu.make_async_copy(wg_hbm.at[e], wgf.at[slot], sems.at[slot, 0]),
                pltpu.make_async_copy(wu_hbm.at[e], wuf.at[slot], sems.at[slot, 1]),
                pltpu.make_async_copy(wd_hbm.at[e], wdf.at[slot], sems.at[slot, 2]))

    @pl.when(step == 0)
    def _():
        grp_ref[0] = 0
        for a in range(W_AHEAD):
            @pl.when(ge_ref[a] >= 0)
            def _(a=a):
                for cp in weight_copies(ge_ref[a], a):
                    cp.start()

    n_blocks = be_ref.shape[0]
    n_used = nu_ref[0]

    def swiglu(s, n, slot):
        rows = pl.ds(pl.multiple_of(s * MOE_BLOCK, MOE_BLOCK), n * MOE_BLOCK)
        lo, hi = _unpack_rows(x_ref[rows, :])
        lo = lo.astype(BF16)
        hi = hi.astype(BF16)
        g = (jnp.dot(lo, wgf[slot, :PACKED, :], preferred_element_type=F32)
             + jnp.dot(hi, wgf[slot, PACKED:, :], preferred_element_type=F32))
        u = (jnp.dot(lo, wuf[slot, :PACKED, :], preferred_element_type=F32)
             + jnp.dot(hi, wuf[slot, PACKED:, :], preferred_element_type=F32))
        h = (g * _sigmoid(g) * u).astype(BF16)
        o_ref[rows, :] = _pack_rows(jnp.dot(h, wdf[slot], preferred_element_type=F32))

    def run(s):
        j = step * EXPERT_GROUP + s
        e = be_ref[j]
        first = jnp.logical_or(j == 0, e != be_ref[jnp.maximum(j - 1, 0)])

        @pl.when(first)
        def _():
            grp = grp_ref[0]
            slot = grp % W_SLOTS
            for cp in weight_copies(e, slot):
                cp.wait()
            nxt = ge_ref[grp + W_AHEAD]

            @pl.when(nxt >= 0)
            def _():
                for cp in weight_copies(nxt, (grp + W_AHEAD) % W_SLOTS):
                    cp.start()
            grp_ref[0] = grp + 1

        def same(k):
            return (s + k < EXPERT_GROUP) & (j + k < n_used) & (be_ref[jnp.minimum(j + k, n_blocks - 1)] == e)
        take = jnp.int32(1)
        for n in EXPERT_RUNS[1:]:
            ok = same(n - 1)
            for k in range(1, n - 1):
                ok = ok & same(k)
            take = jnp.where(ok, n, take)
        slot = (grp_ref[0] + W_SLOTS - 1) % W_SLOTS
        for n in EXPERT_RUNS:
            @pl.when(take == n)
            def _(n=n):
                swiglu(s, n, slot)
        return s + take

    lax.while_loop(lambda s: (s < EXPERT_GROUP) & (step * EXPERT_GROUP + s < n_used), run, jnp.int32(0))


def _experts(xs, block_expert, n_used, group_expert, w_gate, w_up, w_down):
    cap = xs.shape[0]
    n_blocks = cap // MOE_BLOCK
    assert n_blocks % EXPERT_GROUP == 0
    rows = EXPERT_GROUP * MOE_BLOCK
    last = lambda j, be, nu, ge: jnp.minimum(j, (nu[0] - 1) // EXPERT_GROUP)
    hbm = pl.BlockSpec(memory_space=pl.ANY)
    gs = pltpu.PrefetchScalarGridSpec(
        num_scalar_prefetch=3,
        grid=(n_blocks // EXPERT_GROUP,),
        in_specs=[pl.BlockSpec((rows, PACKED), lambda j, be, nu, ge: (last(j, be, nu, ge), 0)), hbm, hbm, hbm],
        out_specs=pl.BlockSpec((rows, PACKED), lambda j, be, nu, ge: (last(j, be, nu, ge), 0)),
        scratch_shapes=[pltpu.VMEM((W_SLOTS, D_MODEL, D_EXPERT), F32),
                        pltpu.VMEM((W_SLOTS, D_MODEL, D_EXPERT), F32),
                        pltpu.VMEM((W_SLOTS, D_EXPERT, D_MODEL), F32),
                        pltpu.SMEM((1,), jnp.int32),
                        pltpu.SemaphoreType.DMA((W_SLOTS, 3))],
    )
    return pl.pallas_call(
        _expert_kernel,
        grid_spec=gs,
        out_shape=jax.ShapeDtypeStruct((cap, PACKED), U32),
        compiler_params=_cparams(1),
        name="experts",
    )(block_expert, n_used, group_expert, xs, w_gate, w_up, w_down)


CB_TM = RT_TM


def _combine_kernel(x1_ref, gt_ref, y2_ref, o_ref):
    g = gt_ref[...]
    lo1, hi1 = _unpack_rows(y2_ref[0, 0])
    lo2, hi2 = _unpack_rows(y2_ref[0, 1])
    o_ref[:, :PACKED] = x1_ref[:, :PACKED] + g[:, 0:1] * lo1 + g[:, 1:2] * lo2
    o_ref[:, PACKED:] = x1_ref[:, PACKED:] + g[:, 0:1] * hi1 + g[:, 1:2] * hi2


def _combine(x1, gates, y2):
    T = x1.shape[0]
    nt = T // CB_TM
    return pl.pallas_call(
        _combine_kernel,
        grid=(nt,),
        in_specs=[pl.BlockSpec((CB_TM, D_MODEL), lambda i: (i, 0)),
                  pl.BlockSpec((CB_TM, RINFO), lambda i: (i, 0)),
                  pl.BlockSpec((1, TOP_K, CB_TM, PACKED), lambda i: (i, 0, 0, 0))],
        out_specs=pl.BlockSpec((CB_TM, D_MODEL), lambda i: (i, 0)),
        out_shape=jax.ShapeDtypeStruct((T, D_MODEL), F32),
        compiler_params=_cparams(1),
        name="combine",
    )(x1, gates, y2)


def _layer(x, rel_bias, ln1, w_in, q_norm, k_norm, attn_sink, conv_w, conv_b, lru_wa, lru_ba, lru_wi, lru_bi,
           lru_lambda, out_norm_attn, out_norm_lru, w_out, ln2, w_group, b_group, w_er, b_er, w_gate, w_up, w_down):
    B, S, D = x.shape
    T = B * S
    x2 = x.reshape(T, D)
    q, kv, xr, gr = _in_proj(x2, ln1, w_in, q_norm, k_norm)
    attn_n = _attention(q.reshape(B, S, ATTN_WIDTH), kv.reshape(B, S, 2 * KV_WIDTH), rel_bias, attn_sink,
                        out_norm_attn)
    lru_n = _rglru(xr.reshape(B, S, LRU_WIDTH), gr.reshape(B, S, LRU_WIDTH), conv_w, conv_b,
                   lru_wa, lru_ba, lru_wi, lru_bi, lru_lambda, out_norm_lru)
    x1, h2, gates, ei, cnt = _out_route(attn_n.reshape(T, ATTN_WIDTH), lru_n.reshape(T, LRU_WIDTH), x2, w_out, ln2,
                                        w_group, b_group, w_er, b_er)
    cap = _moe_cap(T)
    dest, pstart, block_expert, n_used, group_expert = _layout(ei, cnt, cap // MOE_BLOCK)
    xs = _padfill(_sc_dispatch(h2, dest, cap), pstart, cnt)
    yb = _experts(xs, block_expert, n_used, group_expert, w_gate, w_up, w_down)
    out = _combine(x1, gates, _sc_gather(yb, dest))
    return out.reshape(B, S, D)


def kernel(x, rel_bias, ln1, w_in, q_norm, k_norm, attn_sink, conv_w, conv_b, lru_wa, lru_ba, lru_wi, lru_bi,
           lru_lambda, out_norm_attn, out_norm_lru, w_out, ln2, w_group, b_group, w_expert_router, b_expert_router,
           w_gate, w_up, w_down):
    depth = ln1.shape[0]
    for l in range(depth):
        x = _layer(x, rel_bias, ln1[l], w_in[l], q_norm[l], k_norm[l], attn_sink[l], conv_w[l], conv_b[l],
                   lru_wa[l], lru_ba[l], lru_wi[l], lru_bi[l], lru_lambda[l], out_norm_attn[l], out_norm_lru[l],
                   w_out[l], ln2[l], w_group[l], b_group[l], w_expert_router[l], b_expert_router[l],
                   w_gate[l], w_up[l], w_down[l])
    return x
```

```python
import functools
import math

import jax
import jax.numpy as jnp
import numpy as np
from jax import lax
from jax.experimental import pallas as pl
from jax.experimental.pallas import tpu as pltpu
from jax.experimental.pallas import tpu_sc as plsc

D_MODEL = 1024
N_HEADS = 8
N_KV_HEADS = 2
HEAD_DIM = 64
Q_PER_KV = N_HEADS // N_KV_HEADS
ATTN_WIDTH = N_HEADS * HEAD_DIM
KV_WIDTH = N_KV_HEADS * HEAD_DIM
WINDOW = 128
BLOCK = 128
NUM_BUCKETS = 32
MAX_DISTANCE = 128
LRU_WIDTH = D_MODEL - ATTN_WIDTH
LRU_BLOCKS = 8
LRU_BLOCK_DIM = LRU_WIDTH // LRU_BLOCKS
LRU_C = 8.0
CONV_W = 4
CONV_LEFT = 2
N_GROUPS = 4
EXPERTS_PER_GROUP = 8
N_EXPERTS = N_GROUPS * EXPERTS_PER_GROUP
TOP_K = 2
D_EXPERT = 512
MOE_BLOCK = 128
EPS = 1e-6
NEG_INF = -1e30

LANES = 128
SUBLANES = 8
VMEM_LIMIT = 56 * 1024 * 1024
LRU_VMEM_LIMIT = 62 * 1024 * 1024

F32 = jnp.float32
BF16 = jnp.bfloat16
LOG2E = math.log2(math.e)


def _cparams(n_axes, vmem=VMEM_LIMIT):
    return pltpu.CompilerParams(dimension_semantics=("arbitrary",) * n_axes, vmem_limit_bytes=vmem)


def _rms(x, gain):
    return x * lax.rsqrt(jnp.mean(x * x, axis=-1, keepdims=True) + EPS) * gain


U32 = jnp.uint32
HI_MASK = 0xFFFF0000
PACKED = D_MODEL // 2


def _pack_rows(x):
    return _pack_rounded(x.astype(BF16).astype(F32))


def _pack_rounded(xb):
    h = xb.shape[1] // 2
    lo = lax.bitcast_convert_type(xb[:, :h], U32) >> 16
    hi = lax.bitcast_convert_type(xb[:, h:], U32) & jnp.uint32(HI_MASK)
    return lo | hi


def _unpack_rows(p):
    lo = lax.bitcast_convert_type(p << 16, F32)
    hi = lax.bitcast_convert_type(p & jnp.uint32(HI_MASK), F32)
    return lo, hi


IN_TM = 1024


def _head_rms(x, n_heads, gain):
    head = lax.broadcasted_iota(jnp.int32, (1, n_heads * HEAD_DIM), 1) // HEAD_DIM
    x2 = x * x
    scale = jnp.zeros_like(x)
    for h in range(n_heads):
        ms = jnp.sum(jnp.where(head == h, x2, 0.0), axis=-1, keepdims=True) * (1.0 / HEAD_DIM)
        scale = jnp.where(head == h, lax.rsqrt(ms + EPS), scale)
    return x * scale * gain


def _in_proj_kernel(x_ref, g_ref, w_ref, qn_ref, kn_ref, q_ref, kv_ref, xr_ref, gr_ref, wb_ref, qg_ref, kg_ref):
    @pl.when(pl.program_id(0) == 0)
    def _():
        wb_ref[...] = w_ref[...].astype(BF16)
        qg_ref[...] = jnp.concatenate([qn_ref[...]] * N_HEADS, axis=1) * (HEAD_DIM ** -0.5 * LOG2E)
        kg_ref[...] = jnp.concatenate([kn_ref[...]] * N_KV_HEADS, axis=1)

    h = _rms(x_ref[...], g_ref[...]).astype(BF16)
    c_k = ATTN_WIDTH
    c_v = c_k + KV_WIDTH
    c_x = c_v + KV_WIDTH
    c_g = c_x + LRU_WIDTH
    q = jnp.dot(h, wb_ref[:, :c_k], preferred_element_type=F32)
    q_ref[...] = _head_rms(q, N_HEADS, qg_ref[...]).astype(BF16)
    k = jnp.dot(h, wb_ref[:, c_k:c_v], preferred_element_type=F32)
    kv_ref[:, :KV_WIDTH] = _head_rms(k, N_KV_HEADS, kg_ref[...]).astype(BF16)
    kv_ref[:, KV_WIDTH:] = jnp.dot(h, wb_ref[:, c_v:c_x], preferred_element_type=F32).astype(BF16)
    xr_ref[...] = jnp.dot(h, wb_ref[:, c_x:c_g], preferred_element_type=F32)
    gr_ref[...] = jnp.dot(h, wb_ref[:, c_g:], preferred_element_type=F32)


def _in_proj(x2, ln1, w_in, q_gain, k_gain):
    T = x2.shape[0]
    n_in = w_in.shape[1]
    row = lambda w: pl.BlockSpec((IN_TM, w), lambda i: (i, 0))
    return pl.pallas_call(
        _in_proj_kernel,
        grid=(T // IN_TM,),
        in_specs=[row(D_MODEL),
                  pl.BlockSpec((1, D_MODEL), lambda i: (0, 0)),
                  pl.BlockSpec((D_MODEL, n_in), lambda i: (0, 0)),
                  pl.BlockSpec((1, HEAD_DIM), lambda i: (0, 0)),
                  pl.BlockSpec((1, HEAD_DIM), lambda i: (0, 0))],
        out_specs=[row(ATTN_WIDTH), row(2 * KV_WIDTH), row(LRU_WIDTH), row(LRU_WIDTH)],
        out_shape=[jax.ShapeDtypeStruct((T, ATTN_WIDTH), BF16),
                   jax.ShapeDtypeStruct((T, 2 * KV_WIDTH), BF16),
                   jax.ShapeDtypeStruct((T, LRU_WIDTH), F32),
                   jax.ShapeDtypeStruct((T, LRU_WIDTH), F32)],
        scratch_shapes=[pltpu.VMEM((D_MODEL, n_in), BF16),
                        pltpu.VMEM((1, ATTN_WIDTH), F32),
                        pltpu.VMEM((1, KV_WIDTH), F32)],
        compiler_params=_cparams(1),
        name="in_proj",
    )(x2, ln1.reshape(1, D_MODEL), w_in, q_gain.reshape(1, HEAD_DIM).astype(F32),
      k_gain.reshape(1, HEAD_DIM).astype(F32))


def _t5_bucket(rel):
    half = NUM_BUCKETS // 2
    max_exact = half // 2
    base = jnp.where(rel > 0, half, 0)
    n = jnp.abs(rel)
    nf = jnp.maximum(n, 1).astype(jnp.float32)
    large = max_exact + (jnp.log(nf / max_exact) / math.log(MAX_DISTANCE / max_exact)
                         * (half - max_exact)).astype(jnp.int32)
    large = jnp.minimum(large, half - 1)
    return base + jnp.where(n < max_exact, n, large)


HEAD_PAIRS = Q_PER_KV // 2
EDGE_VARIANTS = 3


def _fill_bias_table(rb_ref, bucket_ref, band_ref, o_ref):
    bucket = bucket_ref[...]
    band = band_ref[...] > 0
    col = lax.broadcasted_iota(jnp.int32, bucket.shape, 1)
    valid = (band & (col >= BLOCK), band, band & (col < 2 * BLOCK))
    for h in range(N_HEADS):
        acc = jnp.zeros(bucket.shape, F32)
        for b in range(NUM_BUCKETS):
            acc = jnp.where(bucket == b, rb_ref[b, h], acc)
        kv, g = divmod(h, Q_PER_KV)
        pair, parity = divmod(g, 2)
        for var in range(EDGE_VARIANTS):
            o_ref[var, kv, parity, pair * BLOCK:(pair + 1) * BLOCK, :] = jnp.where(valid[var], acc * LOG2E, NEG_INF)


def _attn_kernel(sink_ref, rb_ref, q_ref, kp_ref, kc_ref, kn_ref, bucket_ref, band_ref, og_ref, o_ref, bias_ref):
    n = pl.program_id(1)

    @pl.when((pl.program_id(0) == 0) & (n == 0))
    def _():
        _fill_bias_table(rb_ref, bucket_ref, band_ref, bias_ref)

    kv_all = jnp.concatenate([kp_ref[0], kc_ref[0], kn_ref[0]], axis=0)
    for qb in range(ATTN_QB):
        variant = 1
        if qb == 0:
            variant = jnp.where(n == 0, 0, 1)
        if qb == ATTN_QB - 1:
            variant = jnp.where(n == pl.num_programs(1) - 1, 2, variant)
        out = _attn_block(q_ref[0, qb * BLOCK:(qb + 1) * BLOCK, :], kv_all[qb * BLOCK:(qb + 3) * BLOCK, :],
                          lambda kv, parity: bias_ref[variant, kv, parity], sink_ref)
        o_ref[0, qb * BLOCK:(qb + 1) * BLOCK, :] = _rms(out, og_ref[...]).astype(o_ref.dtype)


def _attn_block(q, kvw, bias, sink_ref):
    low = lax.broadcasted_iota(jnp.int32, (1, LANES), 1) < HEAD_DIM
    swap = lambda slab: pltpu.roll(slab.astype(F32), HEAD_DIM, 1).astype(BF16)
    kslab, vslab = kvw[:, :KV_WIDTH], kvw[:, KV_WIDTH:]
    kslab_sw, vslab_sw = swap(kslab), swap(vslab)
    rowi = lax.broadcasted_iota(jnp.int32, (HEAD_PAIRS * BLOCK, 1), 0)
    combos = [(kv, parity) for kv in range(N_KV_HEADS) for parity in range(2)]
    scores, vzs, sinks = [], [], []
    for kv, parity in combos:
        ks, vs = (kslab, vslab) if (kv == 0) == (parity == 0) else (kslab_sw, vslab_sw)
        keep = low if parity == 0 else jnp.logical_not(low)
        kz = jnp.where(keep, ks, jnp.zeros_like(ks))
        vzs.append(jnp.where(keep, vs, jnp.zeros_like(vs)))
        base = kv * Q_PER_KV * HEAD_DIM
        qpair = jnp.concatenate([q[:, base + j * LANES:base + (j + 1) * LANES] for j in range(HEAD_PAIRS)], axis=0)
        s = lax.dot_general(qpair, kz, (((1,), (1,)), ((), ())), preferred_element_type=F32)
        scores.append(s + bias(kv, parity))
        sink = jnp.zeros((HEAD_PAIRS * BLOCK, 1), F32)
        for j in range(HEAD_PAIRS):
            sink = jnp.where(rowi // BLOCK == j, sink_ref[kv * Q_PER_KV + 2 * j + parity], sink)
        sinks.append(sink)
    probs, inv = [], []
    for s, sink in zip(scores, sinks):
        m = jnp.maximum(jnp.max(s, axis=-1, keepdims=True), sink)
        p = jnp.exp2(s - m)
        inv.append(1.0 / (jnp.sum(p, axis=-1, keepdims=True) + jnp.exp2(sink - m)))
        probs.append(p.astype(BF16))
    outs = [jnp.dot(p, vz, preferred_element_type=F32) * r for p, vz, r in zip(probs, vzs, inv)]
    cols = []
    for kv in range(N_KV_HEADS):
        acc = outs[2 * kv] + outs[2 * kv + 1]
        cols += [acc[j * BLOCK:(j + 1) * BLOCK, :] for j in range(HEAD_PAIRS)]
    return jnp.concatenate(cols, axis=1)


ATTN_QB = 4


def _attention(q, kv, rel_bias, sink, out_gain):
    B, S, _ = q.shape
    nb = S // BLOCK
    assert ATTN_QB >= 2 and nb % ATTN_QB == 0, "a step's first and last query blocks must be distinct"
    ns = nb // ATTN_QB
    rows = ATTN_QB * BLOCK
    qi = jnp.arange(BLOCK, dtype=jnp.int32)
    kj = jnp.arange(3 * BLOCK, dtype=jnp.int32)
    rel = kj[None, :] - BLOCK - qi[:, None]
    bucket = _t5_bucket(rel).astype(jnp.int32)
    band = (jnp.abs(rel) <= WINDOW).astype(jnp.int32)
    kvspec = lambda f: pl.BlockSpec((1, BLOCK, 2 * KV_WIDTH), f)
    smem = pl.BlockSpec(memory_space=pltpu.SMEM)
    geom = pl.BlockSpec((BLOCK, 3 * BLOCK), lambda b, n: (0, 0))
    return pl.pallas_call(
        _attn_kernel,
        grid=(B, ns),
        in_specs=[smem, smem,
                  pl.BlockSpec((1, rows, ATTN_WIDTH), lambda b, n: (b, n, 0)),
                  kvspec(lambda b, n: (b, jnp.maximum(n * ATTN_QB - 1, 0), 0)),
                  pl.BlockSpec((1, rows, 2 * KV_WIDTH), lambda b, n: (b, n, 0)),
                  kvspec(lambda b, n: (b, jnp.minimum((n + 1) * ATTN_QB, nb - 1), 0)),
                  geom, geom,
                  pl.BlockSpec((1, ATTN_WIDTH), lambda b, n: (0, 0))],
        out_specs=pl.BlockSpec((1, rows, ATTN_WIDTH), lambda b, n: (b, n, 0)),
        out_shape=jax.ShapeDtypeStruct((B, S, ATTN_WIDTH), BF16),
        scratch_shapes=[pltpu.VMEM((EDGE_VARIANTS, N_KV_HEADS, 2, HEAD_PAIRS * BLOCK, 3 * BLOCK), F32)],
        compiler_params=_cparams(2),
        name="attention",
    )(sink.astype(F32) * LOG2E, rel_bias.astype(F32), q, kv, kv, kv, bucket, band, out_gain.reshape(1, ATTN_WIDTH))


LRU_TC = 128
LRU_PITCH = LRU_TC + SUBLANES
LRU_SLABS = LRU_WIDTH // LANES
HALO = SUBLANES


def _softplus(x):
    return jnp.maximum(x, 0.0) + jnp.log(1.0 + jnp.exp(-jnp.abs(x)))


def _gelu_tanh(x):
    k = math.sqrt(2.0 / math.pi)
    hx = 0.5 * x
    return hx + hx * jnp.tanh(x * (k + (k * 0.044715) * (x * x)))


def _sigmoid(x):
    return 0.5 + 0.5 * jnp.tanh(0.5 * x)


def _rglru_kernel(xr_ref, xp_ref, xn_ref, gr_ref, cw_ref, cb_ref, wa_ref, wi_ref, ba_ref, bi_ref, lam_ref, og_ref,
                  o_ref, sx_ref, a_ref, u_ref, h_ref, carry_ref, hf_ref, xcs_ref, wg_ref, bg_ref, k_ref):
    p = pl.program_id(0)
    i = pl.program_id(1)
    nc = pl.num_programs(1)
    c = i + p * (nc - 1 - 2 * i)
    B = xr_ref.shape[0]
    TC = LRU_TC

    @pl.when(i == 0)
    def _():
        carry_ref[...] = jnp.zeros_like(carry_ref)
        wg_ref[...] = jnp.zeros_like(wg_ref)
        for sel, w_ref in enumerate((wa_ref, wi_ref)):
            for h in range(LRU_BLOCKS):
                lo = h * LRU_BLOCK_DIM
                wg_ref[lo:lo + LRU_BLOCK_DIM, sel * LRU_WIDTH + lo:sel * LRU_WIDTH + lo + LRU_BLOCK_DIM] = (
                    0.5 * w_ref[0, h]).astype(BF16)
        row = pl.ds(p, 1)
        bg_ref[:, :LRU_WIDTH] = 0.5 * ba_ref[row, :]
        bg_ref[:, LRU_WIDTH:] = 0.5 * bi_ref[row, :]
        k_ref[...] = (-0.5 * LRU_C * math.log2(math.e)) * _softplus(-lam_ref[row, :])

    def gates_and_scan(xc2, backward):
        g = jnp.dot(xc2.astype(BF16), wg_ref[...], preferred_element_type=F32) + bg_ref[...]
        ta = jnp.tanh(g[:, :LRU_WIDTH])
        ig = 0.5 + 0.5 * jnp.tanh(g[:, LRU_WIDTH:])
        a = jnp.exp2((1.0 + ta) * k_ref[...])
        z = 1.0 - a * a
        u = z * lax.rsqrt(jnp.maximum(z, 1e-30)) * ig * xc2
        for b in range(B):
            for s in range(LRU_SLABS):
                a_ref[s, b * LRU_PITCH:b * LRU_PITCH + TC, :] = a[b * TC:(b + 1) * TC, s * LANES:(s + 1) * LANES]
                u_ref[s, b * LRU_PITCH:b * LRU_PITCH + TC, :] = u[b * TC:(b + 1) * TC, s * LANES:(s + 1) * LANES]

        def step(k, hs):
            t = TC - 1 - k if backward else k
            out = []
            for s in range(LRU_SLABS):
                idx = pl.ds(t, B, stride=LRU_PITCH)
                hn = a_ref[s, idx, :] * hs[s] + u_ref[s, idx, :]
                h_ref[s, idx, :] = hn
                out.append(hn)
            return tuple(out)

        hs = lax.fori_loop(0, TC, step, tuple(carry_ref[s] for s in range(LRU_SLABS)), unroll=8)
        for s in range(LRU_SLABS):
            carry_ref[s] = hs[s]

    @pl.when(p == 0)
    def _():
        sx_ref[:, HALO:HALO + TC, :] = xr_ref[...]
        sx_ref[:, 0:HALO, :] = jnp.where(c > 0, xp_ref[...], 0.0)
        sx_ref[:, HALO + TC:, :] = jnp.where(c < nc - 1, xn_ref[...], 0.0)
        xc = cb_ref[...][None]
        for j in range(CONV_W):
            off = HALO + j - CONV_LEFT
            xc = xc + cw_ref[j:j + 1, :][None] * sx_ref[:, off:off + TC, :]
        xc2 = xc.reshape(B * TC, LRU_WIDTH)
        xcs_ref[c] = xc2.astype(xcs_ref.dtype)
        gates_and_scan(xc2, backward=False)
        for b in range(B):
            for s in range(LRU_SLABS):
                hf_ref[c, s, b * TC:(b + 1) * TC, :] = h_ref[s, b * LRU_PITCH:b * LRU_PITCH + TC, :].astype(hf_ref.dtype)

    @pl.when(p == 1)
    def _():
        gates_and_scan(xcs_ref[c].astype(F32), backward=True)
        for b in range(B):
            hsum = jnp.concatenate(
                [h_ref[s, b * LRU_PITCH:b * LRU_PITCH + TC, :] + hf_ref[c, s, b * TC:(b + 1) * TC, :].astype(F32)
                 for s in range(LRU_SLABS)], axis=1)
            y = hsum * _gelu_tanh(gr_ref[b])
            o_ref[b] = _rms(y, og_ref[...]).astype(o_ref.dtype)


def _rglru(xr, gr, conv_w, conv_b, w_a, b_a, w_i, b_i, lam, out_gain):
    B, S, W = xr.shape
    nc = S // LRU_TC
    hb = LRU_TC // HALO
    fwd = lambda p, i: jnp.where(p == 0, i, nc - 1)
    bwd = lambda p, i: nc - 1 - p * i
    full2 = lambda shape: pl.BlockSpec(shape, lambda p, i: (0,) * len(shape))
    wblock = pl.BlockSpec((1, LRU_BLOCKS, LRU_BLOCK_DIM, LRU_BLOCK_DIM), lambda p, i: (p, 0, 0, 0))
    return pl.pallas_call(
        _rglru_kernel,
        grid=(2, nc),
        in_specs=[pl.BlockSpec((B, LRU_TC, W), lambda p, i: (0, fwd(p, i), 0)),
                  pl.BlockSpec((B, HALO, W), lambda p, i: (0, jnp.maximum(fwd(p, i) * hb - 1, 0), 0)),
                  pl.BlockSpec((B, HALO, W), lambda p, i: (0, jnp.minimum((fwd(p, i) + 1) * hb, S // HALO - 1), 0)),
                  pl.BlockSpec((B, LRU_TC, W), lambda p, i: (0, bwd(p, i), 0)),
                  full2((CONV_W, W)),
                  full2((1, W)),
                  wblock, wblock,
                  full2((2, W)), full2((2, W)), full2((2, W)),
                  full2((1, W))],
        out_specs=pl.BlockSpec((B, LRU_TC, W), lambda p, i: (0, bwd(p, i), 0)),
        out_shape=jax.ShapeDtypeStruct((B, S, W), BF16),
        scratch_shapes=[pltpu.VMEM((B, LRU_TC + 2 * HALO, W), F32),
                        pltpu.VMEM((LRU_SLABS, B * LRU_PITCH, LANES), F32),
                        pltpu.VMEM((LRU_SLABS, B * LRU_PITCH, LANES), F32),
                        pltpu.VMEM((LRU_SLABS, B * LRU_PITCH, LANES), F32),
                        pltpu.VMEM((LRU_SLABS, B, LANES), F32),
                        pltpu.VMEM((nc, LRU_SLABS, B * LRU_TC, LANES), BF16),
                        pltpu.VMEM((nc, B * LRU_TC, W), BF16),
                        pltpu.VMEM((W, 2 * W), BF16),
                        pltpu.VMEM((1, 2 * W), F32),
                        pltpu.VMEM((1, W), F32)],
        compiler_params=_cparams(2, LRU_VMEM_LIMIT),
        name="rglru",
    )(xr, xr, xr, gr, conv_w.astype(F32), conv_b.reshape(1, W).astype(F32), w_a.astype(F32), w_i.astype(F32),
      b_a.astype(F32), b_i.astype(F32), lam.astype(F32), out_gain.reshape(1, W).astype(F32))


RT_TM = 1024
RT_PARTS = 4
RT_COLS = LANES
RT_ROWS = 48
RINFO = SUBLANES


def _split_bf16(x):
    hi = x.astype(BF16)
    lo = (x - hi.astype(F32)).astype(BF16)
    return hi, lo


def _route_kernel(bgr_ref, ber_ref, an_ref, ln_ref, x_ref, wo_ref, g2_ref, wgr_ref, wer_ref,
                  x1_ref, h2_ref, gt_ref, ei_ref, cnt_ref, wob_ref, wrf_ref, wrb_ref, br_ref, tri_ref, run_ref,
                  runc_ref):
    @pl.when(pl.program_id(0) == 0)
    def _():
        wob_ref[...] = wo_ref[...].astype(BF16)
        wrf_ref[...] = jnp.zeros_like(wrf_ref)
        wrf_ref[0:N_GROUPS, :] = wgr_ref[...].T
        row = lax.broadcasted_iota(jnp.int32, br_ref.shape, 0)
        bias = jnp.zeros(br_ref.shape, F32)
        for g in range(N_GROUPS):
            lo_row = SUBLANES + g * EXPERTS_PER_GROUP
            wrf_ref[lo_row:lo_row + EXPERTS_PER_GROUP, :] = wer_ref[g].T
            bias = jnp.where(row == g, bgr_ref[g], bias)
            for e in range(EXPERTS_PER_GROUP):
                bias = jnp.where(row == lo_row + e, ber_ref[g, e], bias)
        br_ref[...] = bias
        hi, lo = _split_bf16(wrf_ref[...])
        wrb_ref[:RT_ROWS, :] = hi
        wrb_ref[RT_ROWS:, :] = lo
        r = lax.broadcasted_iota(jnp.int32, (RT_TM, RT_TM), 0)
        cidx = lax.broadcasted_iota(jnp.int32, (RT_TM, RT_TM), 1)
        tri_ref[...] = (r < cidx).astype(BF16)
        run_ref[...] = jnp.zeros_like(run_ref)
        runc_ref[...] = jnp.zeros_like(runc_ref)

    nt_dims = (((1,), (1,)), ((), ()))
    part = RT_TM // RT_PARTS
    x1s = []
    for r in range(RT_PARTS):
        rows = slice(r * part, (r + 1) * part)
        x1 = (x_ref[rows, :]
              + jnp.dot(an_ref[rows, :], wob_ref[:ATTN_WIDTH, :], preferred_element_type=F32)
              + jnp.dot(ln_ref[rows, :], wob_ref[ATTN_WIDTH:, :], preferred_element_type=F32))
        x1_ref[rows, :] = x1
        x1s.append(x1)
    splits = []
    for r, x1 in enumerate(x1s):
        h2 = _rms(x1, g2_ref[...])
        hi = h2.astype(BF16)
        hi_f = hi.astype(F32)
        h2_ref[r * part:(r + 1) * part, :] = _pack_rounded(hi_f)
        splits.append((hi, (h2 - hi_f).astype(BF16)))
    logits = []
    for hi, lo in splits:
        t1 = lax.dot_general(wrb_ref[...], hi, nt_dims, preferred_element_type=F32)
        t2 = lax.dot_general(wrb_ref[:RT_ROWS, :], lo, nt_dims, preferred_element_type=F32)
        logits.append(t1[:RT_ROWS] + t1[RT_ROWS:] + t2)
    logit = jnp.concatenate(logits, axis=1) + br_ref[...]

    sub = lax.broadcasted_iota(jnp.int32, (SUBLANES, RT_TM), 0)
    first_min = lambda hit: jnp.min(jnp.where(hit, sub, SUBLANES), axis=0, keepdims=True)
    is_g = sub < N_GROUPS
    gl = jnp.where(is_g, logit[:SUBLANES], -jnp.inf)
    gm = jnp.max(gl, axis=0, keepdims=True)
    gidx = first_min(gl == gm)
    g_p = 1.0 / jnp.sum(jnp.where(is_g, jnp.exp(logit[:SUBLANES] - gm), 0.0), axis=0, keepdims=True)
    el = logit[SUBLANES:2 * SUBLANES]
    for g in range(1, N_GROUPS):
        el = jnp.where(gidx == g, logit[(g + 1) * SUBLANES:(g + 2) * SUBLANES], el)
    m1 = jnp.max(el, axis=0, keepdims=True)
    i1 = first_min(el == m1)
    el2 = jnp.where(sub == i1, -jnp.inf, el)
    m2 = jnp.max(el2, axis=0, keepdims=True)
    i2 = first_min(el2 == m2)
    t = jnp.exp(m2 - m1)
    gate1 = g_p / (1.0 + t)
    gate2 = g_p * t / (1.0 + t)
    e1 = gidx * EXPERTS_PER_GROUP + i1
    e2 = gidx * EXPERTS_PER_GROUP + i2

    erow = lax.broadcasted_iota(jnp.int32, (N_EXPERTS, RT_TM), 0)
    oh1 = erow == e1
    oh2 = erow == e2
    oh = (oh1 | oh2).astype(F32)
    ohb = oh.astype(BF16)
    cum = jnp.dot(ohb, tri_ref[...], preferred_element_type=F32) + runc_ref[...]
    rank1 = jnp.sum(jnp.where(oh1, cum, 0.0), axis=0, keepdims=True)
    rank2 = jnp.sum(jnp.where(oh2, cum, 0.0), axis=0, keepdims=True)
    runc_ref[...] = runc_ref[...] + jnp.sum(oh, axis=1, keepdims=True)
    tile_cnt = lax.dot_general(jnp.ones((SUBLANES, RT_TM), BF16), ohb, nt_dims, preferred_element_type=F32)
    run_ref[:, :N_EXPERTS] = run_ref[:, :N_EXPERTS] + tile_cnt[0:1]
    cnt_ref[...] = run_ref[...].astype(jnp.int32)

    rows = [e1, e2, rank1.astype(jnp.int32), rank2.astype(jnp.int32)]
    ei = jnp.zeros((RINFO, RT_TM), jnp.int32)
    for k, v in enumerate(rows):
        ei = jnp.where(sub == k, v, ei)
    ei_ref[0] = ei
    gt_ref[...] = jnp.where(sub == 0, gate1, jnp.where(sub == 1, gate2, 0.0)).T


def _out_route(attn_n, lru_n, x2, w_out, ln2, w_group, b_group, w_er, b_er):
    T = x2.shape[0]
    assert N_GROUPS <= SUBLANES and EXPERTS_PER_GROUP == SUBLANES and RT_ROWS >= SUBLANES + N_EXPERTS
    row = lambda w: pl.BlockSpec((RT_TM, w), lambda i: (i, 0))
    const = lambda shape: pl.BlockSpec(shape, lambda i: (0,) * len(shape))
    smem = pl.BlockSpec(memory_space=pltpu.SMEM)
    return pl.pallas_call(
        _route_kernel,
        grid=(T // RT_TM,),
        in_specs=[smem, smem,
                  row(ATTN_WIDTH), row(LRU_WIDTH), row(D_MODEL), const((D_MODEL, D_MODEL)), const((1, D_MODEL)),
                  const((D_MODEL, N_GROUPS)), const((N_GROUPS, D_MODEL, EXPERTS_PER_GROUP))],
        out_specs=[row(D_MODEL), row(PACKED), row(RINFO),
                   pl.BlockSpec((1, RINFO, RT_TM), lambda i: (i, 0, 0)), const((1, RT_COLS))],
        out_shape=[jax.ShapeDtypeStruct((T, D_MODEL), F32),
                   jax.ShapeDtypeStruct((T, PACKED), U32),
                   jax.ShapeDtypeStruct((T, RINFO), F32),
                   jax.ShapeDtypeStruct((T // RT_TM, RINFO, RT_TM), jnp.int32),
                   jax.ShapeDtypeStruct((1, RT_COLS), jnp.int32)],
        scratch_shapes=[pltpu.VMEM((D_MODEL, D_MODEL), BF16),
                        pltpu.VMEM((RT_ROWS, D_MODEL), F32),
                        pltpu.VMEM((2 * RT_ROWS, D_MODEL), BF16),
                        pltpu.VMEM((RT_ROWS, 1), F32),
                        pltpu.VMEM((RT_TM, RT_TM), BF16),
                        pltpu.VMEM((1, RT_COLS), F32),
                        pltpu.VMEM((N_EXPERTS, 1), F32)],
        compiler_params=_cparams(1),
        name="out_route",
    )(b_group.astype(F32), b_er.astype(F32), attn_n, lru_n, x2, w_out, ln2.reshape(1, D_MODEL).astype(F32),
      w_group.astype(F32), w_er.astype(F32))


def _moe_cap(T):
    A = T * TOP_K
    return ((A + MOE_BLOCK - 1) // MOE_BLOCK) * MOE_BLOCK + N_EXPERTS * MOE_BLOCK


PAD_BITS = tuple(1 << b for b in reversed(range(3, MOE_BLOCK.bit_length() - 1)))


def _layout_kernel(cnt_ref, ei_ref, dest_ref, pstart, be_ref, nu_ref, ge_ref):
    n_blocks = be_ref.shape[0]

    def lay(e, carry):
        start, blk, grp = carry
        pstart[e] = start
        nb = (cnt_ref[0, e] + MOE_BLOCK - 1) // MOE_BLOCK
        ge_ref[grp] = e

        def fill(k, c):
            be_ref[blk + k] = e
            return c
        lax.fori_loop(0, nb, fill, 0)
        return start + nb * MOE_BLOCK, blk + nb, grp + (nb > 0).astype(jnp.int32)
    _, used, groups = lax.fori_loop(0, N_EXPERTS, lay, (jnp.int32(0), jnp.int32(0), jnp.int32(0)))
    nu_ref[0] = used

    def tail(k, c):
        be_ref[k] = N_EXPERTS - 1
        return c
    lax.fori_loop(used, n_blocks, tail, 0)

    def no_group(k, c):
        ge_ref[k] = -1
        return c
    lax.fori_loop(groups, ge_ref.shape[0], no_group, 0)

    expert = ei_ref[:, 0:TOP_K, :]
    dest = ei_ref[:, TOP_K:2 * TOP_K, :]
    for e in range(N_EXPERTS):
        dest = dest + jnp.where(expert == e, pstart[e], 0)
    dest_ref[...] = dest


def _layout(ei, cnt, n_blocks):
    nt = ei.shape[0]
    smem = pl.BlockSpec(memory_space=pltpu.SMEM)
    vmem = pl.BlockSpec(memory_space=pltpu.VMEM)
    return pl.pallas_call(
        _layout_kernel,
        in_specs=[smem, vmem],
        out_specs=[vmem, smem, smem, smem, smem],
        out_shape=[jax.ShapeDtypeStruct((nt, TOP_K, RT_TM), jnp.int32),
                   jax.ShapeDtypeStruct((N_EXPERTS,), jnp.int32),
                   jax.ShapeDtypeStruct((n_blocks,), jnp.int32),
                   jax.ShapeDtypeStruct((1,), jnp.int32),
                   jax.ShapeDtypeStruct((N_EXPERTS + W_AHEAD,), jnp.int32)],
        name="layout",
    )(cnt, ei)


SC_CHUNK = 64
SC_BUFS = 3
SC_LEAD = SC_BUFS - 1


def _sc_workers():
    info = plsc.get_sparse_core_info()
    return info.num_cores, info.num_subcores


def _sc_ring(n_chunks, read, write):
    for c in range(min(SC_LEAD, n_chunks)):
        for cp in read(c):
            cp.start()
    reclaimed = set()
    for c in range(n_chunks):
        for cp in read(c):
            cp.wait()
        for cp in write(c):
            cp.start()
        nxt = c + SC_LEAD
        if nxt < n_chunks:
            if nxt - SC_BUFS >= 0:
                for cp in write(nxt - SC_BUFS):
                    cp.wait()
                reclaimed.add(nxt - SC_BUFS)
            for cp in read(nxt):
                cp.start()
    for c in range(n_chunks):
        if c not in reclaimed:
            for cp in write(c):
                cp.wait()


def _sc_dispatch(h2p, dest, cap):
    T = h2p.shape[0]
    nc, ns = _sc_workers()
    per_w = T // (nc * ns)
    n_ch = per_w // SC_CHUNK
    nt, _, tm = dest.shape
    assert nt * tm == T and tm % per_w == 0 and per_w % SC_CHUNK == 0
    idx = dest.reshape(nt, TOP_K, tm // per_w, per_w).transpose(0, 2, 1, 3).reshape(nc * ns, TOP_K * n_ch, SC_CHUNK)
    mesh = plsc.VectorSubcoreMesh(core_axis_name="c", subcore_axis_name="s")

    @functools.partial(
        pl.kernel, mesh=mesh,
        out_type=jax.ShapeDtypeStruct((cap, PACKED), U32),
        scratch_types=[pltpu.VMEM((TOP_K * n_ch, SC_CHUNK), jnp.int32),
                       pltpu.VMEM((SC_BUFS, SC_CHUNK, PACKED), U32),
                       pltpu.SemaphoreType.DMA((SC_BUFS,)),
                       pltpu.SemaphoreType.DMA((SC_BUFS,))])
    def scatter(src_hbm, idx_hbm, out_hbm, idx_v, rows_v, rsem, wsem):
        wid = lax.axis_index("s") * nc + lax.axis_index("c")
        base = pl.multiple_of(wid * per_w, per_w)
        pltpu.sync_copy(idx_hbm.at[wid], idx_v)

        def read(c):
            b = c % SC_BUFS
            return [pltpu.make_async_copy(src_hbm.at[pl.ds(base + c * SC_CHUNK, SC_CHUNK)], rows_v.at[b], rsem.at[b])]

        def write(c):
            b = c % SC_BUFS
            return [pltpu.make_async_copy(rows_v.at[b], out_hbm.at[idx_v.at[k * n_ch + c]], wsem.at[b])
                    for k in range(TOP_K)]
        _sc_ring(n_ch, read, write)

    return scatter(h2p, idx)


def _sc_gather(yb, dest):
    nt, _, tm = dest.shape
    nc, ns = _sc_workers()
    n_rows = nt * TOP_K * tm
    per_w = n_rows // (nc * ns)
    n_ch = per_w // SC_CHUNK
    assert per_w * nc * ns == n_rows and per_w % SC_CHUNK == 0
    mesh = plsc.VectorSubcoreMesh(core_axis_name="c", subcore_axis_name="s")

    @functools.partial(
        pl.kernel, mesh=mesh,
        out_type=jax.ShapeDtypeStruct((n_rows, PACKED), U32),
        scratch_types=[pltpu.VMEM((per_w,), jnp.int32),
                       pltpu.VMEM((SC_BUFS, SC_CHUNK, PACKED), U32),
                       pltpu.SemaphoreType.DMA((SC_BUFS,)),
                       pltpu.SemaphoreType.DMA((SC_BUFS,))])
    def gather(table_hbm, idx_hbm, out_hbm, idx_v, rows_v, gsem, wsem):
        wid = lax.axis_index("s") * nc + lax.axis_index("c")
        base = pl.multiple_of(wid * per_w, per_w)
        pltpu.sync_copy(idx_hbm.at[pl.ds(base, per_w)], idx_v)

        def read(c):
            b = c % SC_BUFS
            return [pltpu.make_async_copy(table_hbm.at[idx_v.at[pl.ds(c * SC_CHUNK, SC_CHUNK)]], rows_v.at[b], gsem.at[b])]

        def write(c):
            b = c % SC_BUFS
            return [pltpu.make_async_copy(rows_v.at[b], out_hbm.at[pl.ds(base + c * SC_CHUNK, SC_CHUNK)], wsem.at[b])]
        _sc_ring(n_ch, read, write)

    return gather(yb, dest.reshape(n_rows)).reshape(nt, TOP_K, tm, PACKED)


def _padfill_kernel(cnt_ref, pstart, xs_in, xs_ref, zeros, zsem):
    del xs_in

    def pad_copies(fn):
        for e in range(N_EXPERTS):
            cnt = cnt_ref[0, e]
            head = (-cnt) & (SUBLANES - 1)
            rest = ((-cnt) & (MOE_BLOCK - 1)) - head
            off = pstart[e] + cnt
            for k in range(SUBLANES - 1):
                @pl.when(k < head)
                def _(off=off, k=k):
                    fn(pltpu.make_async_copy(zeros.at[pl.ds(0, 1), :], xs_ref.at[pl.ds(off + k, 1), :], zsem))
            off = off + head
            for bit in PAD_BITS:
                @pl.when((rest & bit) != 0)
                def _(off=off, bit=bit):
                    fn(pltpu.make_async_copy(zeros.at[pl.ds(0, bit), :],
                                             xs_ref.at[pl.ds(pl.multiple_of(off, SUBLANES), bit), :], zsem))
                off = off + (rest & bit)

    zeros[...] = jnp.zeros_like(zeros)
    pad_copies(lambda cp: cp.start())
    pad_copies(lambda cp: cp.wait())


def _padfill(xs, pstart, cnt):
    smem = pl.BlockSpec(memory_space=pltpu.SMEM)
    hbm = pl.BlockSpec(memory_space=pl.ANY)
    return pl.pallas_call(
        _padfill_kernel,
        in_specs=[smem, smem, hbm],
        out_specs=hbm,
        out_shape=jax.ShapeDtypeStruct(xs.shape, xs.dtype),
        input_output_aliases={2: 0},
        scratch_shapes=[pltpu.VMEM((MOE_BLOCK // 2, PACKED), U32), pltpu.SemaphoreType.DMA(())],
        name="padfill",
    )(cnt, pstart, xs)


W_SLOTS = 3
W_AHEAD = W_SLOTS - 1
EXPERT_GROUP = 16
EXPERT_RUNS = (1, 2, 4, 8)


def _expert_kernel(be_ref, nu_ref, ge_ref, x_ref, wg_hbm, wu_hbm, wd_hbm, o_ref,
                   wgf, wuf, wdf, grp_ref, sems):
    step = pl.program_id(0)

    def weight_copies(e, slot):
        return (pltpu.make_async_copy(wg_hbm.at[e], wgf.at[slot], sems.at[slot, 0]),
                pltpu.make_async_copy(wu_hbm.at[e], wuf.at[slot], sems.at[slot, 1]),
                pltpu.make_async_copy(wd_hbm.at[e], wdf.at[slot], sems.at[slot, 2]))

    @pl.when(step == 0)
    def _():
        grp_ref[0] = 0
        for a in range(W_AHEAD):
            @pl.when(ge_ref[a] >= 0)
            def _(a=a):
                for cp in weight_copies(ge_ref[a], a):
                    cp.start()

    n_blocks = be_ref.shape[0]
    n_used = nu_ref[0]

    def swiglu(s, n, slot):
        rows = pl.ds(pl.multiple_of(s * MOE_BLOCK, MOE_BLOCK), n * MOE_BLOCK)
        lo, hi = _unpack_rows(x_ref[rows, :])
        lo = lo.astype(BF16)
        hi = hi.astype(BF16)
        g = (jnp.dot(lo, wgf[slot, :PACKED, :], preferred_element_type=F32)
             + jnp.dot(hi, wgf[slot, PACKED:, :], preferred_element_type=F32))
        u = (jnp.dot(lo, wuf[slot, :PACKED, :], preferred_element_type=F32)
             + jnp.dot(hi, wuf[slot, PACKED:, :], preferred_element_type=F32))
        h = (g * _sigmoid(g) * u).astype(BF16)
        o_ref[rows, :] = _pack_rows(jnp.dot(h, wdf[slot], preferred_element_type=F32))

    def run(s):
        j = step * EXPERT_GROUP + s
        e = be_ref[j]
        first = jnp.logical_or(j == 0, e != be_ref[jnp.maximum(j - 1, 0)])

        @pl.when(first)
        def _():
            grp = grp_ref[0]
            slot = grp % W_SLOTS
            for cp in weight_copies(e, slot):
                cp.wait()
            nxt = ge_ref[grp + W_AHEAD]

            @pl.when(nxt >= 0)
            def _():
                for cp in weight_copies(nxt, (grp + W_AHEAD) % W_SLOTS):
                    cp.start()
            grp_ref[0] = grp + 1

        def same(k):
            return (s + k < EXPERT_GROUP) & (j + k < n_used) & (be_ref[jnp.minimum(j + k, n_blocks - 1)] == e)
        take = jnp.int32(1)
        for n in EXPERT_RUNS[1:]:
            ok = same(n - 1)
            for k in range(1, n - 1):
                ok = ok & same(k)
            take = jnp.where(ok, n, take)
        slot = (grp_ref[0] + W_SLOTS - 1) % W_SLOTS
        for n in EXPERT_RUNS:
            @pl.when(take == n)
            def _(n=n):
                swiglu(s, n, slot)
        return s + take

    lax.while_loop(lambda s: (s < EXPERT_GROUP) & (step * EXPERT_GROUP + s < n_used), run, jnp.int32(0))


def _experts(xs, block_expert, n_used, group_expert, w_gate, w_up, w_down):
    cap = xs.shape[0]
    n_blocks = cap // MOE_BLOCK
    assert n_blocks % EXPERT_GROUP == 0
    rows = EXPERT_GROUP * MOE_BLOCK
    last = lambda j, be, nu, ge: jnp.minimum(j, (nu[0] - 1) // EXPERT_GROUP)
    hbm = pl.BlockSpec(memory_space=pl.ANY)
    gs = pltpu.PrefetchScalarGridSpec(
        num_scalar_prefetch=3,
        grid=(n_blocks // EXPERT_GROUP,),
        in_specs=[pl.BlockSpec((rows, PACKED), lambda j, be, nu, ge: (last(j, be, nu, ge), 0)), hbm, hbm, hbm],
        out_specs=pl.BlockSpec((rows, PACKED), lambda j, be, nu, ge: (last(j, be, nu, ge), 0)),
        scratch_shapes=[pltpu.VMEM((W_SLOTS, D_MODEL, D_EXPERT), F32),
                        pltpu.VMEM((W_SLOTS, D_MODEL, D_EXPERT), F32),
                        pltpu.VMEM((W_SLOTS, D_EXPERT, D_MODEL), F32),
                        pltpu.SMEM((1,), jnp.int32),
                        pltpu.SemaphoreType.DMA((W_SLOTS, 3))],
    )
    return pl.pallas_call(
        _expert_kernel,
        grid_spec=gs,
        out_shape=jax.ShapeDtypeStruct((cap, PACKED), U32),
        compiler_params=_cparams(1),
        name="experts",
    )(block_expert, n_used, group_expert, xs, w_gate, w_up, w_down)


CB_TM = RT_TM


def _combine_kernel(x1_ref, gt_ref, y2_ref, o_ref):
    g = gt_ref[...]
    lo1, hi1 = _unpack_rows(y2_ref[0, 0])
    lo2, hi2 = _unpack_rows(y2_ref[0, 1])
    o_ref[:, :PACKED] = x1_ref[:, :PACKED] + g[:, 0:1] * lo1 + g[:, 1:2] * lo2
    o_ref[:, PACKED:] = x1_ref[:, PACKED:] + g[:, 0:1] * hi1 + g[:, 1:2] * hi2


def _combine(x1, gates, y2):
    T = x1.shape[0]
    nt = T // CB_TM
    return pl.pallas_call(
        _combine_kernel,
        grid=(nt,),
        in_specs=[pl.BlockSpec((CB_TM, D_MODEL), lambda i: (i, 0)),
                  pl.BlockSpec((CB_TM, RINFO), lambda i: (i, 0)),
                  pl.BlockSpec((1, TOP_K, CB_TM, PACKED), lambda i: (i, 0, 0, 0))],
        out_specs=pl.BlockSpec((CB_TM, D_MODEL), lambda i: (i, 0)),
        out_shape=jax.ShapeDtypeStruct((T, D_MODEL), F32),
        compiler_params=_cparams(1),
        name="combine",
    )(x1, gates, y2)


def _layer(x, rel_bias, ln1, w_in, q_norm, k_norm, attn_sink, conv_w, conv_b, lru_wa, lru_ba, lru_wi, lru_bi,
           lru_lambda, out_norm_attn, out_norm_lru, w_out, ln2, w_group, b_group, w_er, b_er, w_gate, w_up, w_down):
    B, S, D = x.shape
    T = B * S
    x2 = x.reshape(T, D)
    q, kv, xr, gr = _in_proj(x2, ln1, w_in, q_norm, k_norm)
    attn_n = _attention(q.reshape(B, S, ATTN_WIDTH), kv.reshape(B, S, 2 * KV_WIDTH), rel_bias, attn_sink,
                        out_norm_attn)
    lru_n = _rglru(xr.reshape(B, S, LRU_WIDTH), gr.reshape(B, S, LRU_WIDTH), conv_w, conv_b,
                   lru_wa, lru_ba, lru_wi, lru_bi, lru_lambda, out_norm_lru)
    x1, h2, gates, ei, cnt = _out_route(attn_n.reshape(T, ATTN_WIDTH), lru_n.reshape(T, LRU_WIDTH), x2, w_out, ln2,
                                        w_group, b_group, w_er, b_er)
    cap = _moe_cap(T)
    dest, pstart, block_expert, n_used, group_expert = _layout(ei, cnt, cap // MOE_BLOCK)
    xs = _padfill(_sc_dispatch(h2, dest, cap), pstart, cnt)
    yb = _experts(xs, block_expert, n_used, group_expert, w_gate, w_up, w_down)
    out = _combine(x1, gates, _sc_gather(yb, dest))
    return out.reshape(B, S, D)


def kernel(x, rel_bias, ln1, w_in, q_norm, k_norm, attn_sink, conv_w, conv_b, lru_wa, lru_ba, lru_wi, lru_bi,
           lru_lambda, out_norm_attn, out_norm_lru, w_out, ln2, w_group, b_group, w_expert_router, b_expert_router,
           w_gate, w_up, w_down):
    depth = ln1.shape[0]
    for l in range(depth):
        x = _layer(x, rel_bias, ln1[l], w_in[l], q_norm[l], k_norm[l], attn_sink[l], conv_w[l], conv_b[l],
                   lru_wa[l], lru_ba[l], lru_wi[l], lru_bi[l], lru_lambda[l], out_norm_attn[l], out_norm_lru[l],
                   w_out[l], ln2[l], w_group[l], b_group[l], w_expert_router[l], b_expert_router[l],
                   w_gate[l], w_up[l], w_down[l])
    return x
```

```python
import functools
import math

import jax
import jax.numpy as jnp
import numpy as np
from jax import lax
from jax.experimental import pallas as pl
from jax.experimental.pallas import tpu as pltpu
from jax.experimental.pallas import tpu_sc as plsc

D_MODEL = 1024
N_HEADS = 8
N_KV_HEADS = 2
HEAD_DIM = 64
Q_PER_KV = N_HEADS // N_KV_HEADS
ATTN_WIDTH = N_HEADS * HEAD_DIM
KV_WIDTH = N_KV_HEADS * HEAD_DIM
WINDOW = 128
BLOCK = 128
NUM_BUCKETS = 32
MAX_DISTANCE = 128
LRU_WIDTH = D_MODEL - ATTN_WIDTH
LRU_BLOCKS = 8
LRU_BLOCK_DIM = LRU_WIDTH // LRU_BLOCKS
LRU_C = 8.0
CONV_W = 4
CONV_LEFT = 2
N_GROUPS = 4
EXPERTS_PER_GROUP = 8
N_EXPERTS = N_GROUPS * EXPERTS_PER_GROUP
TOP_K = 2
D_EXPERT = 512
MOE_BLOCK = 128
EPS = 1e-6
NEG_INF = -1e30

LANES = 128
SUBLANES = 8
VMEM_LIMIT = 56 * 1024 * 1024
LRU_VMEM_LIMIT = 62 * 1024 * 1024

F32 = jnp.float32
BF16 = jnp.bfloat16
LOG2E = math.log2(math.e)


def _cparams(n_axes, vmem=VMEM_LIMIT):
    return pltpu.CompilerParams(dimension_semantics=("arbitrary",) * n_axes, vmem_limit_bytes=vmem)


def _rms(x, gain):
    return x * lax.rsqrt(jnp.mean(x * x, axis=-1, keepdims=True) + EPS) * gain


U32 = jnp.uint32
HI_MASK = 0xFFFF0000
PACKED = D_MODEL // 2


def _pack_rows(x):
    return _pack_rounded(x.astype(BF16).astype(F32))


def _pack_rounded(xb):
    h = xb.shape[1] // 2
    lo = lax.bitcast_convert_type(xb[:, :h], U32) >> 16
    hi = lax.bitcast_convert_type(xb[:, h:], U32) & jnp.uint32(HI_MASK)
    return lo | hi


def _unpack_rows(p):
    lo = lax.bitcast_convert_type(p << 16, F32)
    hi = lax.bitcast_convert_type(p & jnp.uint32(HI_MASK), F32)
    return lo, hi


IN_TM = 1024


def _head_rms(x, n_heads, gain):
    head = lax.broadcasted_iota(jnp.int32, (1, n_heads * HEAD_DIM), 1) // HEAD_DIM
    x2 = x * x
    scale = jnp.zeros_like(x)
    for h in range(n_heads):
        ms = jnp.sum(jnp.where(head == h, x2, 0.0), axis=-1, keepdims=True) * (1.0 / HEAD_DIM)
        scale = jnp.where(head == h, lax.rsqrt(ms + EPS), scale)
    return x * scale * gain


def _in_proj_kernel(x_ref, g_ref, w_ref, qn_ref, kn_ref, q_ref, kv_ref, xr_ref, gr_ref, wb_ref, qg_ref, kg_ref):
    @pl.when(pl.program_id(0) == 0)
    def _():
        wb_ref[...] = w_ref[...].astype(BF16)
        qg_ref[...] = jnp.concatenate([qn_ref[...]] * N_HEADS, axis=1) * (HEAD_DIM ** -0.5 * LOG2E)
        kg_ref[...] = jnp.concatenate([kn_ref[...]] * N_KV_HEADS, axis=1)

    h = _rms(x_ref[...], g_ref[...]).astype(BF16)
    c_k = ATTN_WIDTH
    c_v = c_k + KV_WIDTH
    c_x = c_v + KV_WIDTH
    c_g = c_x + LRU_WIDTH
    q = jnp.dot(h, wb_ref[:, :c_k], preferred_element_type=F32)
    q_ref[...] = _head_rms(q, N_HEADS, qg_ref[...]).astype(BF16)
    k = jnp.dot(h, wb_ref[:, c_k:c_v], preferred_element_type=F32)
    kv_ref[:, :KV_WIDTH] = _head_rms(k, N_KV_HEADS, kg_ref[...]).astype(BF16)
    kv_ref[:, KV_WIDTH:] = jnp.dot(h, wb_ref[:, c_v:c_x], preferred_element_type=F32).astype(BF16)
    xr_ref[...] = jnp.dot(h, wb_ref[:, c_x:c_g], preferred_element_type=F32)
    gr_ref[...] = jnp.dot(h, wb_ref[:, c_g:], preferred_element_type=F32)


def _in_proj(x2, ln1, w_in, q_gain, k_gain):
    T = x2.shape[0]
    n_in = w_in.shape[1]
    row = lambda w: pl.BlockSpec((IN_TM, w), lambda i: (i, 0))
    return pl.pallas_call(
        _in_proj_kernel,
        grid=(T // IN_TM,),
        in_specs=[row(D_MODEL),
                  pl.BlockSpec((1, D_MODEL), lambda i: (0, 0)),
                  pl.BlockSpec((D_MODEL, n_in), lambda i: (0, 0)),
                  pl.BlockSpec((1, HEAD_DIM), lambda i: (0, 0)),
                  pl.BlockSpec((1, HEAD_DIM), lambda i: (0, 0))],
        out_specs=[row(ATTN_WIDTH), row(2 * KV_WIDTH), row(LRU_WIDTH), row(LRU_WIDTH)],
        out_shape=[jax.ShapeDtypeStruct((T, ATTN_WIDTH), BF16),
                   jax.ShapeDtypeStruct((T, 2 * KV_WIDTH), BF16),
                   jax.ShapeDtypeStruct((T, LRU_WIDTH), F32),
                   jax.ShapeDtypeStruct((T, LRU_WIDTH), F32)],
        scratch_shapes=[pltpu.VMEM((D_MODEL, n_in), BF16),
                        pltpu.VMEM((1, ATTN_WIDTH), F32),
                        pltpu.VMEM((1, KV_WIDTH), F32)],
        compiler_params=_cparams(1),
        name="in_proj",
    )(x2, ln1.reshape(1, D_MODEL), w_in, q_gain.reshape(1, HEAD_DIM).astype(F32),
      k_gain.reshape(1, HEAD_DIM).astype(F32))


def _t5_bucket(rel):
    half = NUM_BUCKETS // 2
    max_exact = half // 2
    base = jnp.where(rel > 0, half, 0)
    n = jnp.abs(rel)
    nf = jnp.maximum(n, 1).astype(jnp.float32)
    large = max_exact + (jnp.log(nf / max_exact) / math.log(MAX_DISTANCE / max_exact)
                         * (half - max_exact)).astype(jnp.int32)
    large = jnp.minimum(large, half - 1)
    return base + jnp.where(n < max_exact, n, large)


HEAD_PAIRS = Q_PER_KV // 2
EDGE_VARIANTS = 3


def _fill_bias_table(rb_ref, bucket_ref, band_ref, o_ref):
    bucket = bucket_ref[...]
    band = band_ref[...] > 0
    col = lax.broadcasted_iota(jnp.int32, bucket.shape, 1)
    valid = (band & (col >= BLOCK), band, band & (col < 2 * BLOCK))
    for h in range(N_HEADS):
        acc = jnp.zeros(bucket.shape, F32)
        for b in range(NUM_BUCKETS):
            acc = jnp.where(bucket == b, rb_ref[b, h], acc)
        kv, g = divmod(h, Q_PER_KV)
        pair, parity = divmod(g, 2)
        for var in range(EDGE_VARIANTS):
            o_ref[var, kv, parity, pair * BLOCK:(pair + 1) * BLOCK, :] = jnp.where(valid[var], acc * LOG2E, NEG_INF)


def _attn_kernel(sink_ref, rb_ref, q_ref, kp_ref, kc_ref, kn_ref, bucket_ref, band_ref, og_ref, o_ref, bias_ref):
    n = pl.program_id(1)

    @pl.when((pl.program_id(0) == 0) & (n == 0))
    def _():
        _fill_bias_table(rb_ref, bucket_ref, band_ref, bias_ref)

    kv_all = jnp.concatenate([kp_ref[0], kc_ref[0], kn_ref[0]], axis=0)
    for qb in range(ATTN_QB):
        variant = 1
        if qb == 0:
            variant = jnp.where(n == 0, 0, 1)
        if qb == ATTN_QB - 1:
            variant = jnp.where(n == pl.num_programs(1) - 1, 2, variant)
        out = _attn_block(q_ref[0, qb * BLOCK:(qb + 1) * BLOCK, :], kv_all[qb * BLOCK:(qb + 3) * BLOCK, :],
                          lambda kv, parity: bias_ref[variant, kv, parity], sink_ref)
        o_ref[0, qb * BLOCK:(qb + 1) * BLOCK, :] = _rms(out, og_ref[...]).astype(o_ref.dtype)


def _attn_block(q, kvw, bias, sink_ref):
    low = lax.broadcasted_iota(jnp.int32, (1, LANES), 1) < HEAD_DIM
    swap = lambda slab: pltpu.roll(slab.astype(F32), HEAD_DIM, 1).astype(BF16)
    kslab, vslab = kvw[:, :KV_WIDTH], kvw[:, KV_WIDTH:]
    kslab_sw, vslab_sw = swap(kslab), swap(vslab)
    rowi = lax.broadcasted_iota(jnp.int32, (HEAD_PAIRS * BLOCK, 1), 0)
    combos = [(kv, parity) for kv in range(N_KV_HEADS) for parity in range(2)]
    scores, vzs, sinks = [], [], []
    for kv, parity in combos:
        ks, vs = (kslab, vslab) if (kv == 0) == (parity == 0) else (kslab_sw, vslab_sw)
        keep = low if parity == 0 else jnp.logical_not(low)
        kz = jnp.where(keep, ks, jnp.zeros_like(ks))
        vzs.append(jnp.where(keep, vs, jnp.zeros_like(vs)))
        base = kv * Q_PER_KV * HEAD_DIM
        qpair = jnp.concatenate([q[:, base + j * LANES:base + (j + 1) * LANES] for j in range(HEAD_PAIRS)], axis=0)
        s = lax.dot_general(qpair, kz, (((1,), (1,)), ((), ())), preferred_element_type=F32)
        scores.append(s + bias(kv, parity))
        sink = jnp.zeros((HEAD_PAIRS * BLOCK, 1), F32)
        for j in range(HEAD_PAIRS):
            sink = jnp.where(rowi // BLOCK == j, sink_ref[kv * Q_PER_KV + 2 * j + parity], sink)
        sinks.append(sink)
    probs, inv = [], []
    for s, sink in zip(scores, sinks):
        m = jnp.maximum(jnp.max(s, axis=-1, keepdims=True), sink)
        p = jnp.exp2(s - m)
        inv.append(1.0 / (jnp.sum(p, axis=-1, keepdims=True) + jnp.exp2(sink - m)))
        probs.append(p.astype(BF16))
    outs = [jnp.dot(p, vz, preferred_element_type=F32) * r for p, vz, r in zip(probs, vzs, inv)]
    cols = []
    for kv in range(N_KV_HEADS):
        acc = outs[2 * kv] + outs[2 * kv + 1]
        cols += [acc[j * BLOCK:(j + 1) * BLOCK, :] for j in range(HEAD_PAIRS)]
    return jnp.concatenate(cols, axis=1)


ATTN_QB = 4


def _attention(q, kv, rel_bias, sink, out_gain):
    B, S, _ = q.shape
    nb = S // BLOCK
    assert ATTN_QB >= 2 and nb % ATTN_QB == 0, "a step's first and last query blocks must be distinct"
    ns = nb // ATTN_QB
    rows = ATTN_QB * BLOCK
    qi = jnp.arange(BLOCK, dtype=jnp.int32)
    kj = jnp.arange(3 * BLOCK, dtype=jnp.int32)
    rel = kj[None, :] - BLOCK - qi[:, None]
    bucket = _t5_bucket(rel).astype(jnp.int32)
    band = (jnp.abs(rel) <= WINDOW).astype(jnp.int32)
    kvspec = lambda f: pl.BlockSpec((1, BLOCK, 2 * KV_WIDTH), f)
    smem = pl.BlockSpec(memory_space=pltpu.SMEM)
    geom = pl.BlockSpec((BLOCK, 3 * BLOCK), lambda b, n: (0, 0))
    return pl.pallas_call(
        _attn_kernel,
        grid=(B, ns),
        in_specs=[smem, smem,
                  pl.BlockSpec((1, rows, ATTN_WIDTH), lambda b, n: (b, n, 0)),
                  kvspec(lambda b, n: (b, jnp.maximum(n * ATTN_QB - 1, 0), 0)),
                  pl.BlockSpec((1, rows, 2 * KV_WIDTH), lambda b, n: (b, n, 0)),
                  kvspec(lambda b, n: (b, jnp.minimum((n + 1) * ATTN_QB, nb - 1), 0)),
                  geom, geom,
                  pl.BlockSpec((1, ATTN_WIDTH), lambda b, n: (0, 0))],
        out_specs=pl.BlockSpec((1, rows, ATTN_WIDTH), lambda b, n: (b, n, 0)),
        out_shape=jax.ShapeDtypeStruct((B, S, ATTN_WIDTH), BF16),
        scratch_shapes=[pltpu.VMEM((EDGE_VARIANTS, N_KV_HEADS, 2, HEAD_PAIRS * BLOCK, 3 * BLOCK), F32)],
        compiler_params=_cparams(2),
        name="attention",
    )(sink.astype(F32) * LOG2E, rel_bias.astype(F32), q, kv, kv, kv, bucket, band, out_gain.reshape(1, ATTN_WIDTH))


LRU_TC = 128
LRU_PITCH = LRU_TC + SUBLANES // 2
LRU_SLABS = LRU_WIDTH // LANES
LRU_UNROLL = 8
HALO = SUBLANES


def _softplus(x):
    return jnp.maximum(x, 0.0) + jnp.log(1.0 + jnp.exp(-jnp.abs(x)))


def _gelu_tanh(x):
    k = math.sqrt(2.0 / math.pi)
    hx = 0.5 * x
    return hx + hx * jnp.tanh(x * (k + (k * 0.044715) * (x * x)))


def _sigmoid(x):
    return 0.5 + 0.5 * jnp.tanh(0.5 * x)


def _rglru_kernel(xr_ref, xp_ref, xn_ref, gr_ref, cw_ref, cb_ref, wa_ref, wi_ref, ba_ref, bi_ref, lam_ref, og_ref,
                  o_ref, sx_ref, a_ref, u_ref, h_ref, carry_ref, hf_ref, xcs_ref, wg_ref, bg_ref, k_ref):
    p = pl.program_id(0)
    i = pl.program_id(1)
    nc = pl.num_programs(1)
    c = i + p * (nc - 1 - 2 * i)
    B = xr_ref.shape[0]
    TC = LRU_TC

    @pl.when(i == 0)
    def _():
        carry_ref[...] = jnp.zeros_like(carry_ref)
        wg_ref[...] = jnp.zeros_like(wg_ref)
        for sel, w_ref in enumerate((wa_ref, wi_ref)):
            for h in range(LRU_BLOCKS):
                lo = h * LRU_BLOCK_DIM
                wg_ref[lo:lo + LRU_BLOCK_DIM, sel * LRU_WIDTH + lo:sel * LRU_WIDTH + lo + LRU_BLOCK_DIM] = (
                    0.5 * w_ref[0, h]).astype(BF16)
        row = pl.ds(p, 1)
        bg_ref[:, :LRU_WIDTH] = 0.5 * ba_ref[row, :]
        bg_ref[:, LRU_WIDTH:] = 0.5 * bi_ref[row, :]
        k_ref[...] = (-0.5 * LRU_C * math.log2(math.e)) * _softplus(-lam_ref[row, :])

    def gates_and_scan(xc2, backward):
        g = jnp.dot(xc2.astype(BF16), wg_ref[...], preferred_element_type=F32) + bg_ref[...]
        ta = jnp.tanh(g[:, :LRU_WIDTH])
        ig = 0.5 + 0.5 * jnp.tanh(g[:, LRU_WIDTH:])
        a = jnp.exp2((1.0 + ta) * k_ref[...])
        z = 1.0 - a * a
        u = z * lax.rsqrt(jnp.maximum(z, 1e-30)) * ig * xc2
        for b in range(B):
            for s in range(LRU_SLABS):
                a_ref[s, b * LRU_PITCH:b * LRU_PITCH + TC, :] = a[b * TC:(b + 1) * TC, s * LANES:(s + 1) * LANES]
                u_ref[s, b * LRU_PITCH:b * LRU_PITCH + TC, :] = u[b * TC:(b + 1) * TC, s * LANES:(s + 1) * LANES]

        def trip(i, hs):
            t0 = pl.multiple_of(i * LRU_UNROLL, LRU_UNROLL)
            if backward:
                t0 = TC - LRU_UNROLL - t0
            for j in range(LRU_UNROLL):
                t = t0 + (LRU_UNROLL - 1 - j if backward else j)
                out = []
                for s in range(LRU_SLABS):
                    idx = pl.ds(t, B, stride=LRU_PITCH)
                    hn = a_ref[s, idx, :] * hs[s] + u_ref[s, idx, :]
                    h_ref[s, idx, :] = hn
                    out.append(hn)
                hs = tuple(out)
            return hs

        hs = lax.fori_loop(0, TC // LRU_UNROLL, trip, tuple(carry_ref[s] for s in range(LRU_SLABS)))
        for s in range(LRU_SLABS):
            carry_ref[s] = hs[s]

    @pl.when(p == 0)
    def _():
        sx_ref[:, HALO:HALO + TC, :] = xr_ref[...]
        sx_ref[:, 0:HALO, :] = jnp.where(c > 0, xp_ref[...], 0.0)
        sx_ref[:, HALO + TC:, :] = jnp.where(c < nc - 1, xn_ref[...], 0.0)
        xc = cb_ref[...][None]
        for j in range(CONV_W):
            off = HALO + j - CONV_LEFT
            xc = xc + cw_ref[j:j + 1, :][None] * sx_ref[:, off:off + TC, :]
        xc2 = xc.reshape(B * TC, LRU_WIDTH)
        xcs_ref[c] = xc2.astype(xcs_ref.dtype)
        gates_and_scan(xc2, backward=False)
        for b in range(B):
            for s in range(LRU_SLABS):
                hf_ref[c, s, b * TC:(b + 1) * TC, :] = h_ref[s, b * LRU_PITCH:b * LRU_PITCH + TC, :].astype(hf_ref.dtype)

    @pl.when(p == 1)
    def _():
        gates_and_scan(xcs_ref[c].astype(F32), backward=True)
        for b in range(B):
            hsum = jnp.concatenate(
                [h_ref[s, b * LRU_PITCH:b * LRU_PITCH + TC, :] + hf_ref[c, s, b * TC:(b + 1) * TC, :].astype(F32)
                 for s in range(LRU_SLABS)], axis=1)
            y = hsum * _gelu_tanh(gr_ref[b])
            o_ref[b] = _rms(y, og_ref[...]).astype(o_ref.dtype)


def _rglru(xr, gr, conv_w, conv_b, w_a, b_a, w_i, b_i, lam, out_gain):
    B, S, W = xr.shape
    nc = S // LRU_TC
    hb = LRU_TC // HALO
    fwd = lambda p, i: jnp.where(p == 0, i, nc - 1)
    bwd = lambda p, i: nc - 1 - p * i
    full2 = lambda shape: pl.BlockSpec(shape, lambda p, i: (0,) * len(shape))
    wblock = pl.BlockSpec((1, LRU_BLOCKS, LRU_BLOCK_DIM, LRU_BLOCK_DIM), lambda p, i: (p, 0, 0, 0))
    return pl.pallas_call(
        _rglru_kernel,
        grid=(2, nc),
        in_specs=[pl.BlockSpec((B, LRU_TC, W), lambda p, i: (0, fwd(p, i), 0)),
                  pl.BlockSpec((B, HALO, W), lambda p, i: (0, jnp.maximum(fwd(p, i) * hb - 1, 0), 0)),
                  pl.BlockSpec((B, HALO, W), lambda p, i: (0, jnp.minimum((fwd(p, i) + 1) * hb, S // HALO - 1), 0)),
                  pl.BlockSpec((B, LRU_TC, W), lambda p, i: (0, bwd(p, i), 0)),
                  full2((CONV_W, W)),
                  full2((1, W)),
                  wblock, wblock,
                  full2((2, W)), full2((2, W)), full2((2, W)),
                  full2((1, W))],
        out_specs=pl.BlockSpec((B, LRU_TC, W), lambda p, i: (0, bwd(p, i), 0)),
        out_shape=jax.ShapeDtypeStruct((B, S, W), BF16),
        scratch_shapes=[pltpu.VMEM((B, LRU_TC + 2 * HALO, W), F32),
                        pltpu.VMEM((LRU_SLABS, B * LRU_PITCH, LANES), F32),
                        pltpu.VMEM((LRU_SLABS, B * LRU_PITCH, LANES), F32),
                        pltpu.VMEM((LRU_SLABS, B * LRU_PITCH, LANES), F32),
                        pltpu.VMEM((LRU_SLABS, B, LANES), F32),
                        pltpu.VMEM((nc, LRU_SLABS, B * LRU_TC, LANES), BF16),
                        pltpu.VMEM((nc, B * LRU_TC, W), BF16),
                        pltpu.VMEM((W, 2 * W), BF16),
                        pltpu.VMEM((1, 2 * W), F32),
                        pltpu.VMEM((1, W), F32)],
        compiler_params=_cparams(2, LRU_VMEM_LIMIT),
        name="rglru",
    )(xr, xr, xr, gr, conv_w.astype(F32), conv_b.reshape(1, W).astype(F32), w_a.astype(F32), w_i.astype(F32),
      b_a.astype(F32), b_i.astype(F32), lam.astype(F32), out_gain.reshape(1, W).astype(F32))


RT_TM = 1024
RT_PARTS = 4
RT_COLS = LANES
RT_ROWS = 48
RINFO = SUBLANES


def _split_bf16(x):
    hi = x.astype(BF16)
    lo = (x - hi.astype(F32)).astype(BF16)
    return hi, lo


def _route_kernel(an_ref, ln_ref, x_ref, wo_ref, g2_ref, wr_ref, br_ref,
                  x1_ref, h2_ref, gt_ref, ei_ref, cnt_ref, wob_ref, wrb_ref, tri_ref, run_ref, runc_ref):
    @pl.when(pl.program_id(0) == 0)
    def _():
        wob_ref[...] = wo_ref[...].astype(BF16)
        hi, lo = _split_bf16(wr_ref[...])
        wrb_ref[:RT_ROWS, :] = hi
        wrb_ref[RT_ROWS:, :] = lo
        r = lax.broadcasted_iota(jnp.int32, (RT_TM, RT_TM), 0)
        cidx = lax.broadcasted_iota(jnp.int32, (RT_TM, RT_TM), 1)
        tri_ref[...] = (r < cidx).astype(BF16)
        run_ref[...] = jnp.zeros_like(run_ref)
        runc_ref[...] = jnp.zeros_like(runc_ref)

    nt_dims = (((1,), (1,)), ((), ()))
    part = RT_TM // RT_PARTS
    x1s = []
    for r in range(RT_PARTS):
        rows = slice(r * part, (r + 1) * part)
        x1 = (x_ref[rows, :]
              + jnp.dot(an_ref[rows, :], wob_ref[:ATTN_WIDTH, :], preferred_element_type=F32)
              + jnp.dot(ln_ref[rows, :], wob_ref[ATTN_WIDTH:, :], preferred_element_type=F32))
        x1_ref[rows, :] = x1
        x1s.append(x1)
    splits = []
    for r, x1 in enumerate(x1s):
        h2 = _rms(x1, g2_ref[...])
        hi = h2.astype(BF16)
        hi_f = hi.astype(F32)
        h2_ref[r * part:(r + 1) * part, :] = _pack_rounded(hi_f)
        splits.append((hi, (h2 - hi_f).astype(BF16)))
    logits = []
    for hi, lo in splits:
        t1 = lax.dot_general(wrb_ref[...], hi, nt_dims, preferred_element_type=F32)
        t2 = lax.dot_general(wrb_ref[:RT_ROWS, :], lo, nt_dims, preferred_element_type=F32)
        logits.append(t1[:RT_ROWS] + t1[RT_ROWS:] + t2)
    logit = jnp.concatenate(logits, axis=1) + br_ref[...]

    sub = lax.broadcasted_iota(jnp.int32, (SUBLANES, RT_TM), 0)
    first_min = lambda hit: jnp.min(jnp.where(hit, sub, SUBLANES), axis=0, keepdims=True)
    is_g = sub < N_GROUPS
    gl = jnp.where(is_g, logit[:SUBLANES], -jnp.inf)
    gm = jnp.max(gl, axis=0, keepdims=True)
    gidx = first_min(gl == gm)
    g_p = 1.0 / jnp.sum(jnp.where(is_g, jnp.exp(logit[:SUBLANES] - gm), 0.0), axis=0, keepdims=True)
    el = logit[SUBLANES:2 * SUBLANES]
    for g in range(1, N_GROUPS):
        el = jnp.where(gidx == g, logit[(g + 1) * SUBLANES:(g + 2) * SUBLANES], el)
    m1 = jnp.max(el, axis=0, keepdims=True)
    i1 = first_min(el == m1)
    el2 = jnp.where(sub == i1, -jnp.inf, el)
    m2 = jnp.max(el2, axis=0, keepdims=True)
    i2 = first_min(el2 == m2)
    t = jnp.exp(m2 - m1)
    gate1 = g_p / (1.0 + t)
    gate2 = g_p * t / (1.0 + t)
    e1 = gidx * EXPERTS_PER_GROUP + i1
    e2 = gidx * EXPERTS_PER_GROUP + i2

    erow = lax.broadcasted_iota(jnp.int32, (N_EXPERTS, RT_TM), 0)
    oh1 = erow == e1
    oh2 = erow == e2
    oh = (oh1 | oh2).astype(F32)
    ohb = oh.astype(BF16)
    cum = jnp.dot(ohb, tri_ref[...], preferred_element_type=F32) + runc_ref[...]
    rank1 = jnp.sum(jnp.where(oh1, cum, 0.0), axis=0, keepdims=True)
    rank2 = jnp.sum(jnp.where(oh2, cum, 0.0), axis=0, keepdims=True)
    runc_ref[...] = runc_ref[...] + jnp.sum(oh, axis=1, keepdims=True)
    tile_cnt = lax.dot_general(jnp.ones((SUBLANES, RT_TM), BF16), ohb, nt_dims, preferred_element_type=F32)
    run_ref[:, :N_EXPERTS] = run_ref[:, :N_EXPERTS] + tile_cnt[0:1]
    cnt_ref[...] = run_ref[...].astype(jnp.int32)

    rows = [e1, e2, rank1.astype(jnp.int32), rank2.astype(jnp.int32)]
    ei = jnp.zeros((RINFO, RT_TM), jnp.int32)
    for k, v in enumerate(rows):
        ei = jnp.where(sub == k, v, ei)
    ei_ref[0] = ei
    gt_ref[...] = jnp.where(sub == 0, gate1, jnp.where(sub == 1, gate2, 0.0)).T


def _out_route(attn_n, lru_n, x2, w_out, ln2, w_group, b_group, w_er, b_er):
    T = x2.shape[0]
    pad_g = SUBLANES - N_GROUPS
    wr = jnp.concatenate([jnp.pad(w_group.T, ((0, pad_g), (0, 0))),
                          jnp.transpose(w_er, (0, 2, 1)).reshape(N_EXPERTS, D_MODEL)], axis=0)
    wr = jnp.pad(wr, ((0, RT_ROWS - wr.shape[0]), (0, 0))).astype(F32)
    br = jnp.concatenate([jnp.pad(b_group, (0, pad_g)), b_er.reshape(-1)])
    br = jnp.pad(br, (0, RT_ROWS - br.shape[0])).reshape(RT_ROWS, 1).astype(F32)
    row = lambda w: pl.BlockSpec((RT_TM, w), lambda i: (i, 0))
    const = lambda shape: pl.BlockSpec(shape, lambda i: (0, 0))
    return pl.pallas_call(
        _route_kernel,
        grid=(T // RT_TM,),
        in_specs=[row(ATTN_WIDTH), row(LRU_WIDTH), row(D_MODEL), const((D_MODEL, D_MODEL)), const((1, D_MODEL)),
                  const((RT_ROWS, D_MODEL)), const((RT_ROWS, 1))],
        out_specs=[row(D_MODEL), row(PACKED), row(RINFO),
                   pl.BlockSpec((1, RINFO, RT_TM), lambda i: (i, 0, 0)), const((1, RT_COLS))],
        out_shape=[jax.ShapeDtypeStruct((T, D_MODEL), F32),
                   jax.ShapeDtypeStruct((T, PACKED), U32),
                   jax.ShapeDtypeStruct((T, RINFO), F32),
                   jax.ShapeDtypeStruct((T // RT_TM, RINFO, RT_TM), jnp.int32),
                   jax.ShapeDtypeStruct((1, RT_COLS), jnp.int32)],
        scratch_shapes=[pltpu.VMEM((D_MODEL, D_MODEL), BF16),
                        pltpu.VMEM((2 * RT_ROWS, D_MODEL), BF16),
                        pltpu.VMEM((RT_TM, RT_TM), BF16),
                        pltpu.VMEM((1, RT_COLS), F32),
                        pltpu.VMEM((N_EXPERTS, 1), F32)],
        compiler_params=_cparams(1),
        name="out_route",
    )(attn_n, lru_n, x2, w_out, ln2.reshape(1, D_MODEL).astype(F32), wr, br)


def _moe_cap(T):
    A = T * TOP_K
    return ((A + MOE_BLOCK - 1) // MOE_BLOCK) * MOE_BLOCK + N_EXPERTS * MOE_BLOCK


PAD_BITS = tuple(1 << b for b in reversed(range(3, MOE_BLOCK.bit_length() - 1)))


def _layout_kernel(cnt_ref, ei_ref, dest_ref, pstart, be_ref, nu_ref, ge_ref):
    n_blocks = be_ref.shape[0]

    def lay(e, carry):
        start, blk, grp = carry
        pstart[e] = start
        nb = (cnt_ref[0, e] + MOE_BLOCK - 1) // MOE_BLOCK
        ge_ref[grp] = e

        def fill(k, c):
            be_ref[blk + k] = e
            return c
        lax.fori_loop(0, nb, fill, 0)
        return start + nb * MOE_BLOCK, blk + nb, grp + (nb > 0).astype(jnp.int32)
    _, used, groups = lax.fori_loop(0, N_EXPERTS, lay, (jnp.int32(0), jnp.int32(0), jnp.int32(0)))
    nu_ref[0] = used

    def tail(k, c):
        be_ref[k] = N_EXPERTS - 1
        return c
    lax.fori_loop(used, n_blocks, tail, 0)

    def no_group(k, c):
        ge_ref[k] = -1
        return c
    lax.fori_loop(groups, ge_ref.shape[0], no_group, 0)

    expert = ei_ref[:, 0:TOP_K, :]
    dest = ei_ref[:, TOP_K:2 * TOP_K, :]
    for e in range(N_EXPERTS):
        dest = dest + jnp.where(expert == e, pstart[e], 0)
    dest_ref[...] = dest


def _layout(ei, cnt, n_blocks):
    nt = ei.shape[0]
    smem = pl.BlockSpec(memory_space=pltpu.SMEM)
    vmem = pl.BlockSpec(memory_space=pltpu.VMEM)
    return pl.pallas_call(
        _layout_kernel,
        in_specs=[smem, vmem],
        out_specs=[vmem, smem, smem, smem, smem],
        out_shape=[jax.ShapeDtypeStruct((nt, TOP_K, RT_TM), jnp.int32),
                   jax.ShapeDtypeStruct((N_EXPERTS,), jnp.int32),
                   jax.ShapeDtypeStruct((n_blocks,), jnp.int32),
                   jax.ShapeDtypeStruct((1,), jnp.int32),
                   jax.ShapeDtypeStruct((N_EXPERTS + W_AHEAD,), jnp.int32)],
        name="layout",
    )(cnt, ei)


SC_CHUNK = 64
SC_BUFS = 3
SC_LEAD = SC_BUFS - 1


def _sc_workers():
    info = plsc.get_sparse_core_info()
    return info.num_cores, info.num_subcores


def _sc_ring(n_chunks, read, write):
    for c in range(min(SC_LEAD, n_chunks)):
        for cp in read(c):
            cp.start()
    reclaimed = set()
    for c in range(n_chunks):
        for cp in read(c):
            cp.wait()
        for cp in write(c):
            cp.start()
        nxt = c + SC_LEAD
        if nxt < n_chunks:
            if nxt - SC_BUFS >= 0:
                for cp in write(nxt - SC_BUFS):
                    cp.wait()
                reclaimed.add(nxt - SC_BUFS)
            for cp in read(nxt):
                cp.start()
    for c in range(n_chunks):
        if c not in reclaimed:
            for cp in write(c):
                cp.wait()


def _sc_dispatch(h2p, dest, cap):
    T = h2p.shape[0]
    nc, ns = _sc_workers()
    per_w = T // (nc * ns)
    n_ch = per_w // SC_CHUNK
    nt, _, tm = dest.shape
    assert nt * tm == T and tm % per_w == 0 and per_w % SC_CHUNK == 0
    idx = dest.reshape(nt, TOP_K, tm // per_w, per_w).transpose(0, 2, 1, 3).reshape(nc * ns, TOP_K * n_ch, SC_CHUNK)
    mesh = plsc.VectorSubcoreMesh(core_axis_name="c", subcore_axis_name="s")

    @functools.partial(
        pl.kernel, mesh=mesh,
        out_type=jax.ShapeDtypeStruct((cap, PACKED), U32),
        scratch_types=[pltpu.VMEM((TOP_K * n_ch, SC_CHUNK), jnp.int32),
                       pltpu.VMEM((SC_BUFS, SC_CHUNK, PACKED), U32),
                       pltpu.SemaphoreType.DMA((SC_BUFS,)),
                       pltpu.SemaphoreType.DMA((SC_BUFS,))])
    def scatter(src_hbm, idx_hbm, out_hbm, idx_v, rows_v, rsem, wsem):
        wid = lax.axis_index("s") * nc + lax.axis_index("c")
        base = pl.multiple_of(wid * per_w, per_w)
        pltpu.sync_copy(idx_hbm.at[wid], idx_v)

        def read(c):
            b = c % SC_BUFS
            return [pltpu.make_async_copy(src_hbm.at[pl.ds(base + c * SC_CHUNK, SC_CHUNK)], rows_v.at[b], rsem.at[b])]

        def write(c):
            b = c % SC_BUFS
            return [pltpu.make_async_copy(rows_v.at[b], out_hbm.at[idx_v.at[k * n_ch + c]], wsem.at[b])
                    for k in range(TOP_K)]
        _sc_ring(n_ch, read, write)

    return scatter(h2p, idx)


def _sc_gather(yb, dest):
    nt, _, tm = dest.shape
    nc, ns = _sc_workers()
    n_rows = nt * TOP_K * tm
    per_w = n_rows // (nc * ns)
    n_ch = per_w // SC_CHUNK
    assert per_w * nc * ns == n_rows and per_w % SC_CHUNK == 0
    mesh = plsc.VectorSubcoreMesh(core_axis_name="c", subcore_axis_name="s")

    @functools.partial(
        pl.kernel, mesh=mesh,
        out_type=jax.ShapeDtypeStruct((n_rows, PACKED), U32),
        scratch_types=[pltpu.VMEM((per_w,), jnp.int32),
                       pltpu.VMEM((SC_BUFS, SC_CHUNK, PACKED), U32),
                       pltpu.SemaphoreType.DMA((SC_BUFS,)),
                       pltpu.SemaphoreType.DMA((SC_BUFS,))])
    def gather(table_hbm, idx_hbm, out_hbm, idx_v, rows_v, gsem, wsem):
        wid = lax.axis_index("s") * nc + lax.axis_index("c")
        base = pl.multiple_of(wid * per_w, per_w)
        pltpu.sync_copy(idx_hbm.at[pl.ds(base, per_w)], idx_v)

        def read(c):
            b = c % SC_BUFS
            return [pltpu.make_async_copy(table_hbm.at[idx_v.at[pl.ds(c * SC_CHUNK, SC_CHUNK)]], rows_v.at[b], gsem.at[b])]

        def write(c):
            b = c % SC_BUFS
            return [pltpu.make_async_copy(rows_v.at[b], out_hbm.at[pl.ds(base + c * SC_CHUNK, SC_CHUNK)], wsem.at[b])]
        _sc_ring(n_ch, read, write)

    return gather(yb, dest.reshape(n_rows)).reshape(nt, TOP_K, tm, PACKED)


def _padfill_kernel(cnt_ref, pstart, xs_in, xs_ref, zeros, zsem):
    del xs_in

    def pad_copies(fn):
        for e in range(N_EXPERTS):
            cnt = cnt_ref[0, e]
            head = (-cnt) & (SUBLANES - 1)
            rest = ((-cnt) & (MOE_BLOCK - 1)) - head
            off = pstart[e] + cnt
            for k in range(SUBLANES - 1):
                @pl.when(k < head)
                def _(off=off, k=k):
                    fn(pltpu.make_async_copy(zeros.at[pl.ds(0, 1), :], xs_ref.at[pl.ds(off + k, 1), :], zsem))
            off = off + head
            for bit in PAD_BITS:
                @pl.when((rest & bit) != 0)
                def _(off=off, bit=bit):
                    fn(pltpu.make_async_copy(zeros.at[pl.ds(0, bit), :],
                                             xs_ref.at[pl.ds(pl.multiple_of(off, SUBLANES), bit), :], zsem))
                off = off + (rest & bit)

    zeros[...] = jnp.zeros_like(zeros)
    pad_copies(lambda cp: cp.start())
    pad_copies(lambda cp: cp.wait())


def _padfill(xs, pstart, cnt):
    smem = pl.BlockSpec(memory_space=pltpu.SMEM)
    hbm = pl.BlockSpec(memory_space=pl.ANY)
    return pl.pallas_call(
        _padfill_kernel,
        in_specs=[smem, smem, hbm],
        out_specs=hbm,
        out_shape=jax.ShapeDtypeStruct(xs.shape, xs.dtype),
        input_output_aliases={2: 0},
        scratch_shapes=[pltpu.VMEM((MOE_BLOCK // 2, PACKED), U32), pltpu.SemaphoreType.DMA(())],
        name="padfill",
    )(cnt, pstart, xs)


W_SLOTS = 3
W_AHEAD = W_SLOTS - 1
EXPERT_GROUP = 16
EXPERT_RUNS = (1, 2, 4, 8)


def _expert_kernel(be_ref, nu_ref, ge_ref, x_ref, wg_hbm, wu_hbm, wd_hbm, o_ref,
                   wgf, wuf, wdf, grp_ref, sems):
    step = pl.program_id(0)

    def weight_copies(e, slot):
        return (pltpu.make_async_copy(wg_hbm.at[e], wgf.at[slot], sems.at[slot, 0]),
                pltpu.make_async_copy(wu_hbm.at[e], wuf.at[slot], sems.at[slot, 1]),
                pltpu.make_async_copy(wd_hbm.at[e], wdf.at[slot], sems.at[slot, 2]))

    @pl.when(step == 0)
    def _():
        grp_ref[0] = 0
        for a in range(W_AHEAD):
            @pl.when(ge_ref[a] >= 0)
            def _(a=a):
                for cp in weight_copies(ge_ref[a], a):
                    cp.start()

    n_blocks = be_ref.shape[0]
    n_used = nu_ref[0]

    def swiglu(s, n, slot):
        rows = pl.ds(pl.multiple_of(s * MOE_BLOCK, MOE_BLOCK), n * MOE_BLOCK)
        lo, hi = _unpack_rows(x_ref[rows, :])
        lo = lo.astype(BF16)
        hi = hi.astype(BF16)
        g = (jnp.dot(lo, wgf[slot, :PACKED, :], preferred_element_type=F32)
             + jnp.dot(hi, wgf[slot, PACKED:, :], preferred_element_type=F32))
        u = (jnp.dot(lo, wuf[slot, :PACKED, :], preferred_element_type=F32)
             + jnp.dot(hi, wuf[slot, PACKED:, :], preferred_element_type=F32))
        h = (g * _sigmoid(g) * u).astype(BF16)
        o_ref[rows, :] = _pack_rows(jnp.dot(h, wdf[slot], preferred_element_type=F32))

    def run(s):
        j = step * EXPERT_GROUP + s
        e = be_ref[j]
        first = jnp.logical_or(j == 0, e != be_ref[jnp.maximum(j - 1, 0)])

        @pl.when(first)
        def _():
            grp = grp_ref[0]
            slot = grp % W_SLOTS
            for cp in weight_copies(e, slot):
                cp.wait()
            nxt = ge_ref[grp + W_AHEAD]

            @pl.when(nxt >= 0)
            def _():
                for cp in weight_copies(nxt, (grp + W_AHEAD) % W_SLOTS):
                    cp.start()
            grp_ref[0] = grp + 1

        def same(k):
            return (s + k < EXPERT_GROUP) & (j + k < n_used) & (be_ref[jnp.minimum(j + k, n_blocks - 1)] == e)
        take = jnp.int32(1)
        for n in EXPERT_RUNS[1:]:
            ok = same(n - 1)
            for k in range(1, n - 1):
                ok = ok & same(k)
            take = jnp.where(ok, n, take)
        slot = (grp_ref[0] + W_SLOTS - 1) % W_SLOTS
        for n in EXPERT_RUNS:
            @pl.when(take == n)
            def _(n=n):
                swiglu(s, n, slot)
        return s + take

    lax.while_loop(lambda s: (s < EXPERT_GROUP) & (step * EXPERT_GROUP + s < n_used), run, jnp.int32(0))


def _experts(xs, block_expert, n_used, group_expert, w_gate, w_up, w_down):
    cap = xs.shape[0]
    n_blocks = cap // MOE_BLOCK
    assert n_blocks % EXPERT_GROUP == 0
    rows = EXPERT_GROUP * MOE_BLOCK
    last = lambda j, be, nu, ge: jnp.minimum(j, (nu[0] - 1) // EXPERT_GROUP)
    hbm = pl.BlockSpec(memory_space=pl.ANY)
    gs = pltpu.PrefetchScalarGridSpec(
        num_scalar_prefetch=3,
        grid=(n_blocks // EXPERT_GROUP,),
        in_specs=[pl.BlockSpec((rows, PACKED), lambda j, be, nu, ge: (last(j, be, nu, ge), 0)), hbm, hbm, hbm],
        out_specs=pl.BlockSpec((rows, PACKED), lambda j, be, nu, ge: (last(j, be, nu, ge), 0)),
        scratch_shapes=[pltpu.VMEM((W_SLOTS, D_MODEL, D_EXPERT), F32),
                        pltpu.VMEM((W_SLOTS, D_MODEL, D_EXPERT), F32),
                        pltpu.VMEM((W_SLOTS, D_EXPERT, D_MODEL), F32),
                        pltpu.SMEM((1,), jnp.int32),
                        pltpu.SemaphoreType.DMA((W_SLOTS, 3))],
    )
    return pl.pallas_call(
        _expert_kernel,
        grid_spec=gs,
        out_shape=jax.ShapeDtypeStruct((cap, PACKED), U32),
        compiler_params=_cparams(1),
        name="experts",
    )(block_expert, n_used, group_expert, xs, w_gate, w_up, w_down)


CB_TM = RT_TM


def _combine_kernel(x1_ref, gt_ref, y2_ref, o_ref):
    g = gt_ref[...]
    lo1, hi1 = _unpack_rows(y2_ref[0, 0])
    lo2, hi2 = _unpack_rows(y2_ref[0, 1])
    o_ref[:, :PACKED] = x1_ref[:, :PACKED] + g[:, 0:1] * lo1 + g[:, 1:2] * lo2
    o_ref[:, PACKED:] = x1_ref[:, PACKED:] + g[:, 0:1] * hi1 + g[:, 1:2] * hi2


def _combine(x1, gates, y2):
    T = x1.shape[0]
    nt = T // CB_TM
    return pl.pallas_call(
        _combine_kernel,
        grid=(nt,),
        in_specs=[pl.BlockSpec((CB_TM, D_MODEL), lambda i: (i, 0)),
                  pl.BlockSpec((CB_TM, RINFO), lambda i: (i, 0)),
                  pl.BlockSpec((1, TOP_K, CB_TM, PACKED), lambda i: (i, 0, 0, 0))],
        out_specs=pl.BlockSpec((CB_TM, D_MODEL), lambda i: (i, 0)),
        out_shape=jax.ShapeDtypeStruct((T, D_MODEL), F32),
        compiler_params=_cparams(1),
        name="combine",
    )(x1, gates, y2)


def _layer(x, rel_bias, ln1, w_in, q_norm, k_norm, attn_sink, conv_w, conv_b, lru_wa, lru_ba, lru_wi, lru_bi,
           lru_lambda, out_norm_attn, out_norm_lru, w_out, ln2, w_group, b_group, w_er, b_er, w_gate, w_up, w_down):
    B, S, D = x.shape
    T = B * S
    x2 = x.reshape(T, D)
    q, kv, xr, gr = _in_proj(x2, ln1, w_in, q_norm, k_norm)
    attn_n = _attention(q.reshape(B, S, ATTN_WIDTH), kv.reshape(B, S, 2 * KV_WIDTH), rel_bias, attn_sink,
                        out_norm_attn)
    lru_n = _rglru(xr.reshape(B, S, LRU_WIDTH), gr.reshape(B, S, LRU_WIDTH), conv_w, conv_b,
                   lru_wa, lru_ba, lru_wi, lru_bi, lru_lambda, out_norm_lru)
    x1, h2, gates, ei, cnt = _out_route(attn_n.reshape(T, ATTN_WIDTH), lru_n.reshape(T, LRU_WIDTH), x2, w_out, ln2,
                                        w_group, b_group, w_er, b_er)
    cap = _moe_cap(T)
    dest, pstart, block_expert, n_used, group_expert = _layout(ei, cnt, cap // MOE_BLOCK)
    xs = _padfill(_sc_dispatch(h2, dest, cap), pstart, cnt)
    yb = _experts(xs, block_expert, n_used, group_expert, w_gate, w_up, w_down)
    out = _combine(x1, gates, _sc_gather(yb, dest))
    return out.reshape(B, S, D)


def kernel(x, rel_bias, ln1, w_in, q_norm, k_norm, attn_sink, conv_w, conv_b, lru_wa, lru_ba, lru_wi, lru_bi,
           lru_lambda, out_norm_attn, out_norm_lru, w_out, ln2, w_group, b_group, w_expert_router, b_expert_router,
           w_gate, w_up, w_down):
    depth = ln1.shape[0]
    for l in range(depth):
        x = _layer(x, rel_bias, ln1[l], w_in[l], q_norm[l], k_norm[l], attn_sink[l], conv_w[l], conv_b[l],
                   lru_wa[l], lru_ba[l], lru_wi[l], lru_bi[l], lru_lambda[l], out_norm_attn[l], out_norm_lru[l],
                   w_out[l], ln2[l], w_group[l], b_group[l], w_expert_router[l], b_expert_router[l],
                   w_gate[l], w_up[l], w_down[l])
    return x
```

```python
import functools
import math

import jax
import jax.numpy as jnp
import numpy as np
from jax import lax
from jax.experimental import pallas as pl
from jax.experimental.pallas import tpu as pltpu
from jax.experimental.pallas import tpu_sc as plsc

D_MODEL = 1024
N_HEADS = 8
N_KV_HEADS = 2
HEAD_DIM = 64
Q_PER_KV = N_HEADS // N_KV_HEADS
ATTN_WIDTH = N_HEADS * HEAD_DIM
KV_WIDTH = N_KV_HEADS * HEAD_DIM
WINDOW = 128
BLOCK = 128
NUM_BUCKETS = 32
MAX_DISTANCE = 128
LRU_WIDTH = D_MODEL - ATTN_WIDTH
LRU_BLOCKS = 8
LRU_BLOCK_DIM = LRU_WIDTH // LRU_BLOCKS
LRU_C = 8.0
CONV_W = 4
CONV_LEFT = 2
N_GROUPS = 4
EXPERTS_PER_GROUP = 8
N_EXPERTS = N_GROUPS * EXPERTS_PER_GROUP
TOP_K = 2
D_EXPERT = 512
MOE_BLOCK = 128
EPS = 1e-6
NEG_INF = -1e30

LANES = 128
SUBLANES = 8
VMEM_LIMIT = 56 * 1024 * 1024
LRU_VMEM_LIMIT = 62 * 1024 * 1024

F32 = jnp.float32
BF16 = jnp.bfloat16
LOG2E = math.log2(math.e)


def _cparams(n_axes, vmem=VMEM_LIMIT):
    return pltpu.CompilerParams(dimension_semantics=("arbitrary",) * n_axes, vmem_limit_bytes=vmem)


def _rms(x, gain):
    return x * lax.rsqrt(jnp.mean(x * x, axis=-1, keepdims=True) + EPS) * gain


U32 = jnp.uint32
HI_MASK = 0xFFFF0000
PACKED = D_MODEL // 2


def _pack_rows(x):
    return _pack_rounded(x.astype(BF16).astype(F32))


def _pack_rounded(xb):
    h = xb.shape[1] // 2
    lo = lax.bitcast_convert_type(xb[:, :h], U32) >> 16
    hi = lax.bitcast_convert_type(xb[:, h:], U32) & jnp.uint32(HI_MASK)
    return lo | hi


def _unpack_rows(p):
    lo = lax.bitcast_convert_type(p << 16, F32)
    hi = lax.bitcast_convert_type(p & jnp.uint32(HI_MASK), F32)
    return lo, hi


IN_TM = 1024


def _head_rms(x, n_heads, gain):
    head = lax.broadcasted_iota(jnp.int32, (1, n_heads * HEAD_DIM), 1) // HEAD_DIM
    x2 = x * x
    scale = jnp.zeros_like(x)
    for h in range(n_heads):
        ms = jnp.sum(jnp.where(head == h, x2, 0.0), axis=-1, keepdims=True) * (1.0 / HEAD_DIM)
        scale = jnp.where(head == h, lax.rsqrt(ms + EPS), scale)
    return x * scale * gain


def _in_proj_kernel(x_ref, g_ref, w_ref, qn_ref, kn_ref, q_ref, kv_ref, xr_ref, gr_ref, wb_ref, qg_ref, kg_ref):
    @pl.when(pl.program_id(0) == 0)
    def _():
        wb_ref[...] = w_ref[...].astype(BF16)
        qg_ref[...] = jnp.concatenate([qn_ref[...]] * N_HEADS, axis=1) * (HEAD_DIM ** -0.5 * LOG2E)
        kg_ref[...] = jnp.concatenate([kn_ref[...]] * N_KV_HEADS, axis=1)

    h = _rms(x_ref[...], g_ref[...]).astype(BF16)
    c_k = ATTN_WIDTH
    c_v = c_k + KV_WIDTH
    c_x = c_v + KV_WIDTH
    c_g = c_x + LRU_WIDTH
    q = jnp.dot(h, wb_ref[:, :c_k], preferred_element_type=F32)
    q_ref[...] = _head_rms(q, N_HEADS, qg_ref[...]).astype(BF16)
    k = jnp.dot(h, wb_ref[:, c_k:c_v], preferred_element_type=F32)
    kv_ref[:, :KV_WIDTH] = _head_rms(k, N_KV_HEADS, kg_ref[...]).astype(BF16)
    kv_ref[:, KV_WIDTH:] = jnp.dot(h, wb_ref[:, c_v:c_x], preferred_element_type=F32).astype(BF16)
    xr_ref[...] = jnp.dot(h, wb_ref[:, c_x:c_g], preferred_element_type=F32)
    gr_ref[...] = jnp.dot(h, wb_ref[:, c_g:], preferred_element_type=F32)


def _in_proj(x2, ln1, w_in, q_gain, k_gain):
    T = x2.shape[0]
    n_in = w_in.shape[1]
    row = lambda w: pl.BlockSpec((IN_TM, w), lambda i: (i, 0))
    return pl.pallas_call(
        _in_proj_kernel,
        grid=(T // IN_TM,),
        in_specs=[row(D_MODEL),
                  pl.BlockSpec((1, D_MODEL), lambda i: (0, 0)),
                  pl.BlockSpec((D_MODEL, n_in), lambda i: (0, 0)),
                  pl.BlockSpec((1, HEAD_DIM), lambda i: (0, 0)),
                  pl.BlockSpec((1, HEAD_DIM), lambda i: (0, 0))],
        out_specs=[row(ATTN_WIDTH), row(2 * KV_WIDTH), row(LRU_WIDTH), row(LRU_WIDTH)],
        out_shape=[jax.ShapeDtypeStruct((T, ATTN_WIDTH), BF16),
                   jax.ShapeDtypeStruct((T, 2 * KV_WIDTH), BF16),
                   jax.ShapeDtypeStruct((T, LRU_WIDTH), F32),
                   jax.ShapeDtypeStruct((T, LRU_WIDTH), F32)],
        scratch_shapes=[pltpu.VMEM((D_MODEL, n_in), BF16),
                        pltpu.VMEM((1, ATTN_WIDTH), F32),
                        pltpu.VMEM((1, KV_WIDTH), F32)],
        compiler_params=_cparams(1),
        name="in_proj",
    )(x2, ln1.reshape(1, D_MODEL), w_in, q_gain.reshape(1, HEAD_DIM).astype(F32),
      k_gain.reshape(1, HEAD_DIM).astype(F32))


def _t5_bucket(rel):
    half = NUM_BUCKETS // 2
    max_exact = half // 2
    base = jnp.where(rel > 0, half, 0)
    n = jnp.abs(rel)
    nf = jnp.maximum(n, 1).astype(jnp.float32)
    large = max_exact + (jnp.log(nf / max_exact) / math.log(MAX_DISTANCE / max_exact)
                         * (half - max_exact)).astype(jnp.int32)
    large = jnp.minimum(large, half - 1)
    return base + jnp.where(n < max_exact, n, large)


HEAD_PAIRS = Q_PER_KV // 2
EDGE_VARIANTS = 3


def _fill_bias_table(rb_ref, bucket_ref, band_ref, o_ref):
    bucket = bucket_ref[...]
    band = band_ref[...] > 0
    col = lax.broadcasted_iota(jnp.int32, bucket.shape, 1)
    valid = (band & (col >= BLOCK), band, band & (col < 2 * BLOCK))
    for h in range(N_HEADS):
        acc = jnp.zeros(bucket.shape, F32)
        for b in range(NUM_BUCKETS):
            acc = jnp.where(bucket == b, rb_ref[b, h], acc)
        kv, g = divmod(h, Q_PER_KV)
        pair, parity = divmod(g, 2)
        for var in range(EDGE_VARIANTS):
            o_ref[var, kv, parity, pair * BLOCK:(pair + 1) * BLOCK, :] = jnp.where(valid[var], acc * LOG2E, NEG_INF)


def _attn_kernel(sink_ref, rb_ref, q_ref, kp_ref, kc_ref, kn_ref, bucket_ref, band_ref, og_ref, o_ref, bias_ref):
    n = pl.program_id(1)

    @pl.when((pl.program_id(0) == 0) & (n == 0))
    def _():
        _fill_bias_table(rb_ref, bucket_ref, band_ref, bias_ref)

    kv_all = jnp.concatenate([kp_ref[0], kc_ref[0], kn_ref[0]], axis=0)
    for qb in range(ATTN_QB):
        variant = 1
        if qb == 0:
            variant = jnp.where(n == 0, 0, 1)
        if qb == ATTN_QB - 1:
            variant = jnp.where(n == pl.num_programs(1) - 1, 2, variant)
        out = _attn_block(q_ref[0, qb * BLOCK:(qb + 1) * BLOCK, :], kv_all[qb * BLOCK:(qb + 3) * BLOCK, :],
                          lambda kv, parity: bias_ref[variant, kv, parity], sink_ref)
        o_ref[0, qb * BLOCK:(qb + 1) * BLOCK, :] = _rms(out, og_ref[...]).astype(o_ref.dtype)


def _attn_block(q, kvw, bias, sink_ref):
    low = lax.broadcasted_iota(jnp.int32, (1, LANES), 1) < HEAD_DIM
    swap = lambda slab: pltpu.roll(slab.astype(F32), HEAD_DIM, 1).astype(BF16)
    kslab, vslab = kvw[:, :KV_WIDTH], kvw[:, KV_WIDTH:]
    kslab_sw, vslab_sw = swap(kslab), swap(vslab)
    rowi = lax.broadcasted_iota(jnp.int32, (HEAD_PAIRS * BLOCK, 1), 0)
    combos = [(kv, parity) for kv in range(N_KV_HEADS) for parity in range(2)]
    scores, vzs, sinks = [], [], []
    for kv, parity in combos:
        ks, vs = (kslab, vslab) if (kv == 0) == (parity == 0) else (kslab_sw, vslab_sw)
        keep = low if parity == 0 else jnp.logical_not(low)
        kz = jnp.where(keep, ks, jnp.zeros_like(ks))
        vzs.append(jnp.where(keep, vs, jnp.zeros_like(vs)))
        base = kv * Q_PER_KV * HEAD_DIM
        qpair = jnp.concatenate([q[:, base + j * LANES:base + (j + 1) * LANES] for j in range(HEAD_PAIRS)], axis=0)
        s = lax.dot_general(qpair, kz, (((1,), (1,)), ((), ())), preferred_element_type=F32)
        scores.append(s + bias(kv, parity))
        sink = jnp.zeros((HEAD_PAIRS * BLOCK, 1), F32)
        for j in range(HEAD_PAIRS):
            sink = jnp.where(rowi // BLOCK == j, sink_ref[kv * Q_PER_KV + 2 * j + parity], sink)
        sinks.append(sink)
    probs, inv = [], []
    for s, sink in zip(scores, sinks):
        m = jnp.maximum(jnp.max(s, axis=-1, keepdims=True), sink)
        p = jnp.exp2(s - m)
        inv.append(1.0 / (jnp.sum(p, axis=-1, keepdims=True) + jnp.exp2(sink - m)))
        probs.append(p.astype(BF16))
    outs = [jnp.dot(p, vz, preferred_element_type=F32) * r for p, vz, r in zip(probs, vzs, inv)]
    cols = []
    for kv in range(N_KV_HEADS):
        acc = outs[2 * kv] + outs[2 * kv + 1]
        cols += [acc[j * BLOCK:(j + 1) * BLOCK, :] for j in range(HEAD_PAIRS)]
    return jnp.concatenate(cols, axis=1)


ATTN_QB = 4


def _attention(q, kv, rel_bias, sink, out_gain):
    B, S, _ = q.shape
    nb = S // BLOCK
    assert ATTN_QB >= 2 and nb % ATTN_QB == 0, "a step's first and last query blocks must be distinct"
    ns = nb // ATTN_QB
    rows = ATTN_QB * BLOCK
    qi = jnp.arange(BLOCK, dtype=jnp.int32)
    kj = jnp.arange(3 * BLOCK, dtype=jnp.int32)
    rel = kj[None, :] - BLOCK - qi[:, None]
    bucket = _t5_bucket(rel).astype(jnp.int32)
    band = (jnp.abs(rel) <= WINDOW).astype(jnp.int32)
    kvspec = lambda f: pl.BlockSpec((1, BLOCK, 2 * KV_WIDTH), f)
    smem = pl.BlockSpec(memory_space=pltpu.SMEM)
    geom = pl.BlockSpec((BLOCK, 3 * BLOCK), lambda b, n: (0, 0))
    return pl.pallas_call(
        _attn_kernel,
        grid=(B, ns),
        in_specs=[smem, smem,
                  pl.BlockSpec((1, rows, ATTN_WIDTH), lambda b, n: (b, n, 0)),
                  kvspec(lambda b, n: (b, jnp.maximum(n * ATTN_QB - 1, 0), 0)),
                  pl.BlockSpec((1, rows, 2 * KV_WIDTH), lambda b, n: (b, n, 0)),
                  kvspec(lambda b, n: (b, jnp.minimum((n + 1) * ATTN_QB, nb - 1), 0)),
                  geom, geom,
                  pl.BlockSpec((1, ATTN_WIDTH), lambda b, n: (0, 0))],
        out_specs=pl.BlockSpec((1, rows, ATTN_WIDTH), lambda b, n: (b, n, 0)),
        out_shape=jax.ShapeDtypeStruct((B, S, ATTN_WIDTH), BF16),
        scratch_shapes=[pltpu.VMEM((EDGE_VARIANTS, N_KV_HEADS, 2, HEAD_PAIRS * BLOCK, 3 * BLOCK), F32)],
        compiler_params=_cparams(2),
        name="attention",
    )(sink.astype(F32) * LOG2E, rel_bias.astype(F32), q, kv, kv, kv, bucket, band, out_gain.reshape(1, ATTN_WIDTH))


LRU_TC = 128
LRU_PITCH = LRU_TC + SUBLANES // 2
LRU_SLABS = LRU_WIDTH // LANES
LRU_UNROLL = 8
HALO = SUBLANES


def _softplus(x):
    return jnp.maximum(x, 0.0) + jnp.log(1.0 + jnp.exp(-jnp.abs(x)))


def _gelu_tanh(x):
    k = math.sqrt(2.0 / math.pi)
    hx = 0.5 * x
    return hx + hx * jnp.tanh(x * (k + (k * 0.044715) * (x * x)))


def _sigmoid(x):
    return 0.5 + 0.5 * jnp.tanh(0.5 * x)


def _rglru_kernel(xr_ref, xp_ref, xn_ref, gr_ref, cw_ref, cb_ref, wa_ref, wi_ref, ba_ref, bi_ref, lam_ref, og_ref,
                  o_ref, sx_ref, a_ref, u_ref, h_ref, carry_ref, hf_ref, xcs_ref, wg_ref, bg_ref, k_ref):
    p = pl.program_id(0)
    i = pl.program_id(1)
    nc = pl.num_programs(1)
    c = i + p * (nc - 1 - 2 * i)
    B = xr_ref.shape[0]
    TC = LRU_TC

    @pl.when(i == 0)
    def _():
        carry_ref[...] = jnp.zeros_like(carry_ref)
        wg_ref[...] = jnp.zeros_like(wg_ref)
        for sel, w_ref in enumerate((wa_ref, wi_ref)):
            for h in range(LRU_BLOCKS):
                lo = h * LRU_BLOCK_DIM
                wg_ref[lo:lo + LRU_BLOCK_DIM, sel * LRU_WIDTH + lo:sel * LRU_WIDTH + lo + LRU_BLOCK_DIM] = (
                    0.5 * w_ref[0, h]).astype(BF16)
        row = pl.ds(p, 1)
        bg_ref[:, :LRU_WIDTH] = 0.5 * ba_ref[row, :]
        bg_ref[:, LRU_WIDTH:] = 0.5 * bi_ref[row, :]
        k_ref[...] = (-0.5 * LRU_C * math.log2(math.e)) * _softplus(-lam_ref[row, :])

    def gates_and_scan(xc2, backward):
        g = jnp.dot(xc2.astype(BF16), wg_ref[...], preferred_element_type=F32) + bg_ref[...]
        ta = jnp.tanh(g[:, :LRU_WIDTH])
        ig = 0.5 + 0.5 * jnp.tanh(g[:, LRU_WIDTH:])
        a = jnp.exp2((1.0 + ta) * k_ref[...])
        z = 1.0 - a * a
        u = z * lax.rsqrt(jnp.maximum(z, 1e-30)) * ig * xc2
        for b in range(B):
            for s in range(LRU_SLABS):
                a_ref[s, b * LRU_PITCH:b * LRU_PITCH + TC, :] = a[b * TC:(b + 1) * TC, s * LANES:(s + 1) * LANES]
                u_ref[s, b * LRU_PITCH:b * LRU_PITCH + TC, :] = u[b * TC:(b + 1) * TC, s * LANES:(s + 1) * LANES]

        def trip(i, hs):
            t0 = pl.multiple_of((TC // LRU_UNROLL - 1 - i if backward else i) * LRU_UNROLL, LRU_UNROLL)
            for j in range(LRU_UNROLL):
                t = t0 + (LRU_UNROLL - 1 - j if backward else j)
                out = []
                for s in range(LRU_SLABS):
                    idx = pl.ds(t, B, stride=LRU_PITCH)
                    hn = a_ref[s, idx, :] * hs[s] + u_ref[s, idx, :]
                    h_ref[s, idx, :] = hn
                    out.append(hn)
                hs = tuple(out)
            return hs

        hs = lax.fori_loop(0, TC // LRU_UNROLL, trip, tuple(carry_ref[s] for s in range(LRU_SLABS)))
        for s in range(LRU_SLABS):
            carry_ref[s] = hs[s]

    @pl.when(p == 0)
    def _():
        sx_ref[:, HALO:HALO + TC, :] = xr_ref[...]
        sx_ref[:, 0:HALO, :] = jnp.where(c > 0, xp_ref[...], 0.0)
        sx_ref[:, HALO + TC:, :] = jnp.where(c < nc - 1, xn_ref[...], 0.0)
        xc = cb_ref[...][None]
        for j in range(CONV_W):
            off = HALO + j - CONV_LEFT
            xc = xc + cw_ref[j:j + 1, :][None] * sx_ref[:, off:off + TC, :]
        xc2 = xc.reshape(B * TC, LRU_WIDTH)
        xcs_ref[c] = xc2.astype(xcs_ref.dtype)
        gates_and_scan(xc2, backward=False)
        for b in range(B):
            for s in range(LRU_SLABS):
                hf_ref[c, s, b * TC:(b + 1) * TC, :] = h_ref[s, b * LRU_PITCH:b * LRU_PITCH + TC, :].astype(hf_ref.dtype)

    @pl.when(p == 1)
    def _():
        gates_and_scan(xcs_ref[c].astype(F32), backward=True)
        for b in range(B):
            hsum = jnp.concatenate(
                [h_ref[s, b * LRU_PITCH:b * LRU_PITCH + TC, :] + hf_ref[c, s, b * TC:(b + 1) * TC, :].astype(F32)
                 for s in range(LRU_SLABS)], axis=1)
            y = hsum * _gelu_tanh(gr_ref[b])
            o_ref[b] = _rms(y, og_ref[...]).astype(o_ref.dtype)


def _rglru(xr, gr, conv_w, conv_b, w_a, b_a, w_i, b_i, lam, out_gain):
    B, S, W = xr.shape
    nc = S // LRU_TC
    hb = LRU_TC // HALO
    fwd = lambda p, i: jnp.where(p == 0, i, nc - 1)
    bwd = lambda p, i: nc - 1 - p * i
    full2 = lambda shape: pl.BlockSpec(shape, lambda p, i: (0,) * len(shape))
    wblock = pl.BlockSpec((1, LRU_BLOCKS, LRU_BLOCK_DIM, LRU_BLOCK_DIM), lambda p, i: (p, 0, 0, 0))
    return pl.pallas_call(
        _rglru_kernel,
        grid=(2, nc),
        in_specs=[pl.BlockSpec((B, LRU_TC, W), lambda p, i: (0, fwd(p, i), 0)),
                  pl.BlockSpec((B, HALO, W), lambda p, i: (0, jnp.maximum(fwd(p, i) * hb - 1, 0), 0)),
                  pl.BlockSpec((B, HALO, W), lambda p, i: (0, jnp.minimum((fwd(p, i) + 1) * hb, S // HALO - 1), 0)),
                  pl.BlockSpec((B, LRU_TC, W), lambda p, i: (0, bwd(p, i), 0)),
                  full2((CONV_W, W)),
                  full2((1, W)),
                  wblock, wblock,
                  full2((2, W)), full2((2, W)), full2((2, W)),
                  full2((1, W))],
        out_specs=pl.BlockSpec((B, LRU_TC, W), lambda p, i: (0, bwd(p, i), 0)),
        out_shape=jax.ShapeDtypeStruct((B, S, W), BF16),
        scratch_shapes=[pltpu.VMEM((B, LRU_TC + 2 * HALO, W), F32),
                        pltpu.VMEM((LRU_SLABS, B * LRU_PITCH, LANES), F32),
                        pltpu.VMEM((LRU_SLABS, B * LRU_PITCH, LANES), F32),
                        pltpu.VMEM((LRU_SLABS, B * LRU_PITCH, LANES), F32),
                        pltpu.VMEM((LRU_SLABS, B, LANES), F32),
                        pltpu.VMEM((nc, LRU_SLABS, B * LRU_TC, LANES), BF16),
                        pltpu.VMEM((nc, B * LRU_TC, W), BF16),
                        pltpu.VMEM((W, 2 * W), BF16),
                        pltpu.VMEM((1, 2 * W), F32),
                        pltpu.VMEM((1, W), F32)],
        compiler_params=_cparams(2, LRU_VMEM_LIMIT),
        name="rglru",
    )(xr, xr, xr, gr, conv_w.astype(F32), conv_b.reshape(1, W).astype(F32), w_a.astype(F32), w_i.astype(F32),
      b_a.astype(F32), b_i.astype(F32), lam.astype(F32), out_gain.reshape(1, W).astype(F32))


RT_TM = 1024
RT_PARTS = 4
RT_COLS = LANES
RT_ROWS = 48
RINFO = SUBLANES


def _split_bf16(x):
    hi = x.astype(BF16)
    lo = (x - hi.astype(F32)).astype(BF16)
    return hi, lo


def _route_kernel(an_ref, ln_ref, x_ref, wo_ref, g2_ref, wr_ref, br_ref,
                  x1_ref, h2_ref, gt_ref, ei_ref, cnt_ref, wob_ref, wrb_ref, tri_ref, run_ref, runc_ref):
    @pl.when(pl.program_id(0) == 0)
    def _():
        wob_ref[...] = wo_ref[...].astype(BF16)
        hi, lo = _split_bf16(wr_ref[...])
        wrb_ref[:RT_ROWS, :] = hi
        wrb_ref[RT_ROWS:, :] = lo
        r = lax.broadcasted_iota(jnp.int32, (RT_TM, RT_TM), 0)
        cidx = lax.broadcasted_iota(jnp.int32, (RT_TM, RT_TM), 1)
        tri_ref[...] = (r < cidx).astype(BF16)
        run_ref[...] = jnp.zeros_like(run_ref)
        runc_ref[...] = jnp.zeros_like(runc_ref)

    nt_dims = (((1,), (1,)), ((), ()))
    part = RT_TM // RT_PARTS
    x1s = []
    for r in range(RT_PARTS):
        rows = slice(r * part, (r + 1) * part)
        x1 = (x_ref[rows, :]
              + jnp.dot(an_ref[rows, :], wob_ref[:ATTN_WIDTH, :], preferred_element_type=F32)
              + jnp.dot(ln_ref[rows, :], wob_ref[ATTN_WIDTH:, :], preferred_element_type=F32))
        x1_ref[rows, :] = x1
        x1s.append(x1)
    splits = []
    for r, x1 in enumerate(x1s):
        h2 = _rms(x1, g2_ref[...])
        hi = h2.astype(BF16)
        hi_f = hi.astype(F32)
        h2_ref[r * part:(r + 1) * part, :] = _pack_rounded(hi_f)
        splits.append((hi, (h2 - hi_f).astype(BF16)))
    logits = []
    for hi, lo in splits:
        t1 = lax.dot_general(wrb_ref[...], hi, nt_dims, preferred_element_type=F32)
        t2 = lax.dot_general(wrb_ref[:RT_ROWS, :], lo, nt_dims, preferred_element_type=F32)
        logits.append(t1[:RT_ROWS] + t1[RT_ROWS:] + t2)
    logit = jnp.concatenate(logits, axis=1) + br_ref[...]

    sub = lax.broadcasted_iota(jnp.int32, (SUBLANES, RT_TM), 0)
    first_min = lambda hit: jnp.min(jnp.where(hit, sub, SUBLANES), axis=0, keepdims=True)
    is_g = sub < N_GROUPS
    gl = jnp.where(is_g, logit[:SUBLANES], -jnp.inf)
    gm = jnp.max(gl, axis=0, keepdims=True)
    gidx = first_min(gl == gm)
    g_p = 1.0 / jnp.sum(jnp.where(is_g, jnp.exp(logit[:SUBLANES] - gm), 0.0), axis=0, keepdims=True)
    el = logit[SUBLANES:2 * SUBLANES]
    for g in range(1, N_GROUPS):
        el = jnp.where(gidx == g, logit[(g + 1) * SUBLANES:(g + 2) * SUBLANES], el)
    m1 = jnp.max(el, axis=0, keepdims=True)
    i1 = first_min(el == m1)
    el2 = jnp.where(sub == i1, -jnp.inf, el)
    m2 = jnp.max(el2, axis=0, keepdims=True)
    i2 = first_min(el2 == m2)
    t = jnp.exp(m2 - m1)
    gate1 = g_p / (1.0 + t)
    gate2 = g_p * t / (1.0 + t)
    e1 = gidx * EXPERTS_PER_GROUP + i1
    e2 = gidx * EXPERTS_PER_GROUP + i2

    erow = lax.broadcasted_iota(jnp.int32, (N_EXPERTS, RT_TM), 0)
    oh1 = erow == e1
    oh2 = erow == e2
    oh = (oh1 | oh2).astype(F32)
    ohb = oh.astype(BF16)
    cum = jnp.dot(ohb, tri_ref[...], preferred_element_type=F32) + runc_ref[...]
    rank1 = jnp.sum(jnp.where(oh1, cum, 0.0), axis=0, keepdims=True)
    rank2 = jnp.sum(jnp.where(oh2, cum, 0.0), axis=0, keepdims=True)
    runc_ref[...] = runc_ref[...] + jnp.sum(oh, axis=1, keepdims=True)
    tile_cnt = lax.dot_general(jnp.ones((SUBLANES, RT_TM), BF16), ohb, nt_dims, preferred_element_type=F32)
    run_ref[:, :N_EXPERTS] = run_ref[:, :N_EXPERTS] + tile_cnt[0:1]
    cnt_ref[...] = run_ref[...].astype(jnp.int32)

    rows = [e1, e2, rank1.astype(jnp.int32), rank2.astype(jnp.int32)]
    ei = jnp.zeros((RINFO, RT_TM), jnp.int32)
    for k, v in enumerate(rows):
        ei = jnp.where(sub == k, v, ei)
    ei_ref[0] = ei
    gt_ref[...] = jnp.where(sub == 0, gate1, jnp.where(sub == 1, gate2, 0.0)).T


def _out_route(attn_n, lru_n, x2, w_out, ln2, w_group, b_group, w_er, b_er):
    T = x2.shape[0]
    pad_g = SUBLANES - N_GROUPS
    wr = jnp.concatenate([jnp.pad(w_group.T, ((0, pad_g), (0, 0))),
                          jnp.transpose(w_er, (0, 2, 1)).reshape(N_EXPERTS, D_MODEL)], axis=0)
    wr = jnp.pad(wr, ((0, RT_ROWS - wr.shape[0]), (0, 0))).astype(F32)
    br = jnp.concatenate([jnp.pad(b_group, (0, pad_g)), b_er.reshape(-1)])
    br = jnp.pad(br, (0, RT_ROWS - br.shape[0])).reshape(RT_ROWS, 1).astype(F32)
    row = lambda w: pl.BlockSpec((RT_TM, w), lambda i: (i, 0))
    const = lambda shape: pl.BlockSpec(shape, lambda i: (0, 0))
    return pl.pallas_call(
        _route_kernel,
        grid=(T // RT_TM,),
        in_specs=[row(ATTN_WIDTH), row(LRU_WIDTH), row(D_MODEL), const((D_MODEL, D_MODEL)), const((1, D_MODEL)),
                  const((RT_ROWS, D_MODEL)), const((RT_ROWS, 1))],
        out_specs=[row(D_MODEL), row(PACKED), row(RINFO),
                   pl.BlockSpec((1, RINFO, RT_TM), lambda i: (i, 0, 0)), const((1, RT_COLS))],
        out_shape=[jax.ShapeDtypeStruct((T, D_MODEL), F32),
                   jax.ShapeDtypeStruct((T, PACKED), U32),
                   jax.ShapeDtypeStruct((T, RINFO), F32),
                   jax.ShapeDtypeStruct((T // RT_TM, RINFO, RT_TM), jnp.int32),
                   jax.ShapeDtypeStruct((1, RT_COLS), jnp.int32)],
        scratch_shapes=[pltpu.VMEM((D_MODEL, D_MODEL), BF16),
                        pltpu.VMEM((2 * RT_ROWS, D_MODEL), BF16),
                        pltpu.VMEM((RT_TM, RT_TM), BF16),
                        pltpu.VMEM((1, RT_COLS), F32),
                        pltpu.VMEM((N_EXPERTS, 1), F32)],
        compiler_params=_cparams(1),
        name="out_route",
    )(attn_n, lru_n, x2, w_out, ln2.reshape(1, D_MODEL).astype(F32), wr, br)


def _moe_cap(T):
    A = T * TOP_K
    return ((A + MOE_BLOCK - 1) // MOE_BLOCK) * MOE_BLOCK + N_EXPERTS * MOE_BLOCK


PAD_BITS = tuple(1 << b for b in reversed(range(3, MOE_BLOCK.bit_length() - 1)))


def _layout_kernel(cnt_ref, ei_ref, dest_ref, pstart, be_ref, nu_ref, ge_ref):
    n_blocks = be_ref.shape[0]

    def lay(e, carry):
        start, blk, grp = carry
        pstart[e] = start
        nb = (cnt_ref[0, e] + MOE_BLOCK - 1) // MOE_BLOCK
        ge_ref[grp] = e

        def fill(k, c):
            be_ref[blk + k] = e
            return c
        lax.fori_loop(0, nb, fill, 0)
        return start + nb * MOE_BLOCK, blk + nb, grp + (nb > 0).astype(jnp.int32)
    _, used, groups = lax.fori_loop(0, N_EXPERTS, lay, (jnp.int32(0), jnp.int32(0), jnp.int32(0)))
    nu_ref[0] = used

    def tail(k, c):
        be_ref[k] = N_EXPERTS - 1
        return c
    lax.fori_loop(used, n_blocks, tail, 0)

    def no_group(k, c):
        ge_ref[k] = -1
        return c
    lax.fori_loop(groups, ge_ref.shape[0], no_group, 0)

    expert = ei_ref[:, 0:TOP_K, :]
    dest = ei_ref[:, TOP_K:2 * TOP_K, :]
    for e in range(N_EXPERTS):
        dest = dest + jnp.where(expert == e, pstart[e], 0)
    dest_ref[...] = dest


def _layout(ei, cnt, n_blocks):
    nt = ei.shape[0]
    smem = pl.BlockSpec(memory_space=pltpu.SMEM)
    vmem = pl.BlockSpec(memory_space=pltpu.VMEM)
    return pl.pallas_call(
        _layout_kernel,
        in_specs=[smem, vmem],
        out_specs=[vmem, smem, smem, smem, smem],
        out_shape=[jax.ShapeDtypeStruct((nt, TOP_K, RT_TM), jnp.int32),
                   jax.ShapeDtypeStruct((N_EXPERTS,), jnp.int32),
                   jax.ShapeDtypeStruct((n_blocks,), jnp.int32),
                   jax.ShapeDtypeStruct((1,), jnp.int32),
                   jax.ShapeDtypeStruct((N_EXPERTS + W_AHEAD,), jnp.int32)],
        name="layout",
    )(cnt, ei)


SC_CHUNK = 64
SC_BUFS = 3
SC_LEAD = SC_BUFS - 1


def _sc_workers():
    info = plsc.get_sparse_core_info()
    return info.num_cores, info.num_subcores


def _sc_ring(n_chunks, read, write):
    for c in range(min(SC_LEAD, n_chunks)):
        for cp in read(c):
            cp.start()
    reclaimed = set()
    for c in range(n_chunks):
        for cp in read(c):
            cp.wait()
        for cp in write(c):
            cp.start()
        nxt = c + SC_LEAD
        if nxt < n_chunks:
            if nxt - SC_BUFS >= 0:
                for cp in write(nxt - SC_BUFS):
                    cp.wait()
                reclaimed.add(nxt - SC_BUFS)
            for cp in read(nxt):
                cp.start()
    for c in range(n_chunks):
        if c not in reclaimed:
            for cp in write(c):
                cp.wait()


def _sc_dispatch(h2p, dest, cap):
    T = h2p.shape[0]
    nc, ns = _sc_workers()
    per_w = T // (nc * ns)
    n_ch = per_w // SC_CHUNK
    nt, _, tm = dest.shape
    assert nt * tm == T and tm % per_w == 0 and per_w % SC_CHUNK == 0
    idx = dest.reshape(nt, TOP_K, tm // per_w, per_w).transpose(0, 2, 1, 3).reshape(nc * ns, TOP_K * n_ch, SC_CHUNK)
    mesh = plsc.VectorSubcoreMesh(core_axis_name="c", subcore_axis_name="s")

    @functools.partial(
        pl.kernel, mesh=mesh,
        out_type=jax.ShapeDtypeStruct((cap, PACKED), U32),
        scratch_types=[pltpu.VMEM((TOP_K * n_ch, SC_CHUNK), jnp.int32),
                       pltpu.VMEM((SC_BUFS, SC_CHUNK, PACKED), U32),
                       pltpu.SemaphoreType.DMA((SC_BUFS,)),
                       pltpu.SemaphoreType.DMA((SC_BUFS,))])
    def scatter(src_hbm, idx_hbm, out_hbm, idx_v, rows_v, rsem, wsem):
        wid = lax.axis_index("s") * nc + lax.axis_index("c")
        base = pl.multiple_of(wid * per_w, per_w)
        pltpu.sync_copy(idx_hbm.at[wid], idx_v)

        def read(c):
            b = c % SC_BUFS
            return [pltpu.make_async_copy(src_hbm.at[pl.ds(base + c * SC_CHUNK, SC_CHUNK)], rows_v.at[b], rsem.at[b])]

        def write(c):
            b = c % SC_BUFS
            return [pltpu.make_async_copy(rows_v.at[b], out_hbm.at[idx_v.at[k * n_ch + c]], wsem.at[b])
                    for k in range(TOP_K)]
        _sc_ring(n_ch, read, write)

    return scatter(h2p, idx)


def _sc_gather(yb, dest):
    nt, _, tm = dest.shape
    nc, ns = _sc_workers()
    n_rows = nt * TOP_K * tm
    per_w = n_rows // (nc * ns)
    n_ch = per_w // SC_CHUNK
    assert per_w * nc * ns == n_rows and per_w % SC_CHUNK == 0
    mesh = plsc.VectorSubcoreMesh(core_axis_name="c", subcore_axis_name="s")

    @functools.partial(
        pl.kernel, mesh=mesh,
        out_type=jax.ShapeDtypeStruct((n_rows, PACKED), U32),
        scratch_types=[pltpu.VMEM((per_w,), jnp.int32),
                       pltpu.VMEM((SC_BUFS, SC_CHUNK, PACKED), U32),
                       pltpu.SemaphoreType.DMA((SC_BUFS,)),
                       pltpu.SemaphoreType.DMA((SC_BUFS,))])
    def gather(table_hbm, idx_hbm, out_hbm, idx_v, rows_v, gsem, wsem):
        wid = lax.axis_index("s") * nc + lax.axis_index("c")
        base = pl.multiple_of(wid * per_w, per_w)
        pltpu.sync_copy(idx_hbm.at[pl.ds(base, per_w)], idx_v)

        def read(c):
            b = c % SC_BUFS
            return [pltpu.make_async_copy(table_hbm.at[idx_v.at[pl.ds(c * SC_CHUNK, SC_CHUNK)]], rows_v.at[b], gsem.at[b])]

        def write(c):
            b = c % SC_BUFS
            return [pltpu.make_async_copy(rows_v.at[b], out_hbm.at[pl.ds(base + c * SC_CHUNK, SC_CHUNK)], wsem.at[b])]
        _sc_ring(n_ch, read, write)

    return gather(yb, dest.reshape(n_rows)).reshape(nt, TOP_K, tm, PACKED)


def _padfill_kernel(cnt_ref, pstart, xs_in, xs_ref, zeros, zsem):
    del xs_in

    def pad_copies(fn):
        for e in range(N_EXPERTS):
            cnt = cnt_ref[0, e]
            head = (-cnt) & (SUBLANES - 1)
            rest = ((-cnt) & (MOE_BLOCK - 1)) - head
            off = pstart[e] + cnt
            for k in range(SUBLANES - 1):
                @pl.when(k < head)
                def _(off=off, k=k):
                    fn(pltpu.make_async_copy(zeros.at[pl.ds(0, 1), :], xs_ref.at[pl.ds(off + k, 1), :], zsem))
            off = off + head
            for bit in PAD_BITS:
                @pl.when((rest & bit) != 0)
                def _(off=off, bit=bit):
                    fn(pltpu.make_async_copy(zeros.at[pl.ds(0, bit), :],
                                             xs_ref.at[pl.ds(pl.multiple_of(off, SUBLANES), bit), :], zsem))
                off = off + (rest & bit)

    zeros[...] = jnp.zeros_like(zeros)
    pad_copies(lambda cp: cp.start())
    pad_copies(lambda cp: cp.wait())


def _padfill(xs, pstart, cnt):
    smem = pl.BlockSpec(memory_space=pltpu.SMEM)
    hbm = pl.BlockSpec(memory_space=pl.ANY)
    return pl.pallas_call(
        _padfill_kernel,
        in_specs=[smem, smem, hbm],
        out_specs=hbm,
        out_shape=jax.ShapeDtypeStruct(xs.shape, xs.dtype),
        input_output_aliases={2: 0},
        scratch_shapes=[pltpu.VMEM((MOE_BLOCK // 2, PACKED), U32), pltpu.SemaphoreType.DMA(())],
        name="padfill",
    )(cnt, pstart, xs)


W_SLOTS = 3
W_AHEAD = W_SLOTS - 1
EXPERT_GROUP = 16
EXPERT_RUNS = (1, 2, 4, 8)


def _expert_kernel(be_ref, nu_ref, ge_ref, x_ref, wg_hbm, wu_hbm, wd_hbm, o_ref,
                   wgf, wuf, wdf, grp_ref, sems):
    step = pl.program_id(0)

    def weight_copies(e, slot):
        return (pltpu.make_async_copy(wg_hbm.at[e], wgf.at[slot], sems.at[slot, 0]),
                pltpu.make_async_copy(wu_hbm.at[e], wuf.at[slot], sems.at[slot, 1]),
                pltpu.make_async_copy(wd_hbm.at[e], wdf.at[slot], sems.at[slot, 2]))

    @pl.when(step == 0)
    def _():
        grp_ref[0] = 0
        for a in range(W_AHEAD):
            @pl.when(ge_ref[a] >= 0)
            def _(a=a):
                for cp in weight_copies(ge_ref[a], a):
                    cp.start()

    n_blocks = be_ref.shape[0]
    n_used = nu_ref[0]

    def swiglu(s, n, slot):
        rows = pl.ds(pl.multiple_of(s * MOE_BLOCK, MOE_BLOCK), n * MOE_BLOCK)
        lo, hi = _unpack_rows(x_ref[rows, :])
        lo = lo.astype(BF16)
        hi = hi.astype(BF16)
        g = (jnp.dot(lo, wgf[slot, :PACKED, :], preferred_element_type=F32)
             + jnp.dot(hi, wgf[slot, PACKED:, :], preferred_element_type=F32))
        u = (jnp.dot(lo, wuf[slot, :PACKED, :], preferred_element_type=F32)
             + jnp.dot(hi, wuf[slot, PACKED:, :], preferred_element_type=F32))
        h = (g * _sigmoid(g) * u).astype(BF16)
        o_ref[rows, :] = _pack_rows(jnp.dot(h, wdf[slot], preferred_element_type=F32))

    def run(s):
        j = step * EXPERT_GROUP + s
        e = be_ref[j]
        first = jnp.logical_or(j == 0, e != be_ref[jnp.maximum(j - 1, 0)])

        @pl.when(first)
        def _():
            grp = grp_ref[0]
            slot = grp % W_SLOTS
            for cp in weight_copies(e, slot):
                cp.wait()
            nxt = ge_ref[grp + W_AHEAD]

            @pl.when(nxt >= 0)
            def _():
                for cp in weight_copies(nxt, (grp + W_AHEAD) % W_SLOTS):
                    cp.start()
            grp_ref[0] = grp + 1

        def same(k):
            return (s + k < EXPERT_GROUP) & (j + k < n_used) & (be_ref[jnp.minimum(j + k, n_blocks - 1)] == e)
        take = jnp.int32(1)
        for n in EXPERT_RUNS[1:]:
            ok = same(n - 1)
            for k in range(1, n - 1):
                ok = ok & same(k)
            take = jnp.where(ok, n, take)
        slot = (grp_ref[0] + W_SLOTS - 1) % W_SLOTS
        for n in EXPERT_RUNS:
            @pl.when(take == n)
            def _(n=n):
                swiglu(s, n, slot)
        return s + take

    lax.while_loop(lambda s: (s < EXPERT_GROUP) & (step * EXPERT_GROUP + s < n_used), run, jnp.int32(0))


def _experts(xs, block_expert, n_used, group_expert, w_gate, w_up, w_down):
    cap = xs.shape[0]
    n_blocks = cap // MOE_BLOCK
    assert n_blocks % EXPERT_GROUP == 0
    rows = EXPERT_GROUP * MOE_BLOCK
    last = lambda j, be, nu, ge: jnp.minimum(j, (nu[0] - 1) // EXPERT_GROUP)
    hbm = pl.BlockSpec(memory_space=pl.ANY)
    gs = pltpu.PrefetchScalarGridSpec(
        num_scalar_prefetch=3,
        grid=(n_blocks // EXPERT_GROUP,),
        in_specs=[pl.BlockSpec((rows, PACKED), lambda j, be, nu, ge: (last(j, be, nu, ge), 0)), hbm, hbm, hbm],
        out_specs=pl.BlockSpec((rows, PACKED), lambda j, be, nu, ge: (last(j, be, nu, ge), 0)),
        scratch_shapes=[pltpu.VMEM((W_SLOTS, D_MODEL, D_EXPERT), F32),
                        pltpu.VMEM((W_SLOTS, D_MODEL, D_EXPERT), F32),
                        pltpu.VMEM((W_SLOTS, D_EXPERT, D_MODEL), F32),
                        pltpu.SMEM((1,), jnp.int32),
                        pltpu.SemaphoreType.DMA((W_SLOTS, 3))],
    )
    return pl.pallas_call(
        _expert_kernel,
        grid_spec=gs,
        out_shape=jax.ShapeDtypeStruct((cap, PACKED), U32),
        compiler_params=_cparams(1),
        name="experts",
    )(block_expert, n_used, group_expert, xs, w_gate, w_up, w_down)


CB_TM = RT_TM


def _combine_kernel(x1_ref, gt_ref, y2_ref, o_ref):
    g = gt_ref[...]
    lo1, hi1 = _unpack_rows(y2_ref[0, 0])
    lo2, hi2 = _unpack_rows(y2_ref[0, 1])
    o_ref[:, :PACKED] = x1_ref[:, :PACKED] + g[:, 0:1] * lo1 + g[:, 1:2] * lo2
    o_ref[:, PACKED:] = x1_ref[:, PACKED:] + g[:, 0:1] * hi1 + g[:, 1:2] * hi2


def _combine(x1, gates, y2):
    T = x1.shape[0]
    nt = T // CB_TM
    return pl.pallas_call(
        _combine_kernel,
        grid=(nt,),
        in_specs=[pl.BlockSpec((CB_TM, D_MODEL), lambda i: (i, 0)),
                  pl.BlockSpec((CB_TM, RINFO), lambda i: (i, 0)),
                  pl.BlockSpec((1, TOP_K, CB_TM, PACKED), lambda i: (i, 0, 0, 0))],
        out_specs=pl.BlockSpec((CB_TM, D_MODEL), lambda i: (i, 0)),
        out_shape=jax.ShapeDtypeStruct((T, D_MODEL), F32),
        compiler_params=_cparams(1),
        name="combine",
    )(x1, gates, y2)


def _layer(x, rel_bias, ln1, w_in, q_norm, k_norm, attn_sink, conv_w, conv_b, lru_wa, lru_ba, lru_wi, lru_bi,
           lru_lambda, out_norm_attn, out_norm_lru, w_out, ln2, w_group, b_group, w_er, b_er, w_gate, w_up, w_down):
    B, S, D = x.shape
    T = B * S
    x2 = x.reshape(T, D)
    q, kv, xr, gr = _in_proj(x2, ln1, w_in, q_norm, k_norm)
    attn_n = _attention(q.reshape(B, S, ATTN_WIDTH), kv.reshape(B, S, 2 * KV_WIDTH), rel_bias, attn_sink,
                        out_norm_attn)
    lru_n = _rglru(xr.reshape(B, S, LRU_WIDTH), gr.reshape(B, S, LRU_WIDTH), conv_w, conv_b,
                   lru_wa, lru_ba, lru_wi, lru_bi, lru_lambda, out_norm_lru)
    x1, h2, gates, ei, cnt = _out_route(attn_n.reshape(T, ATTN_WIDTH), lru_n.reshape(T, LRU_WIDTH), x2, w_out, ln2,
                                        w_group, b_group, w_er, b_er)
    cap = _moe_cap(T)
    dest, pstart, block_expert, n_used, group_expert = _layout(ei, cnt, cap // MOE_BLOCK)
    xs = _padfill(_sc_dispatch(h2, dest, cap), pstart, cnt)
    yb = _experts(xs, block_expert, n_used, group_expert, w_gate, w_up, w_down)
    out = _combine(x1, gates, _sc_gather(yb, dest))
    return out.reshape(B, S, D)


def kernel(x, rel_bias, ln1, w_in, q_norm, k_norm, attn_sink, conv_w, conv_b, lru_wa, lru_ba, lru_wi, lru_bi,
           lru_lambda, out_norm_attn, out_norm_lru, w_out, ln2, w_group, b_group, w_expert_router, b_expert_router,
           w_gate, w_up, w_down):
    depth = ln1.shape[0]
    for l in range(depth):
        x = _layer(x, rel_bias, ln1[l], w_in[l], q_norm[l], k_norm[l], attn_sink[l], conv_w[l], conv_b[l],
                   lru_wa[l], lru_ba[l], lru_wi[l], lru_bi[l], lru_lambda[l], out_norm_attn[l], out_norm_lru[l],
                   w_out[l], ln2[l], w_group[l], b_group[l], w_expert_router[l], b_expert_router[l],
                   w_gate[l], w_up[l], w_down[l])
    return x
```

```python
import functools
import math

import jax
import jax.numpy as jnp
from jax import lax
from jax.experimental import pallas as pl
from jax.experimental.pallas import tpu as pltpu
from jax.experimental.pallas import tpu_sc as plsc

D_MODEL = 1024
N_HEADS = 8
N_KV_HEADS = 2
HEAD_DIM = 64
Q_PER_KV = N_HEADS // N_KV_HEADS
ATTN_WIDTH = N_HEADS * HEAD_DIM
KV_WIDTH = N_KV_HEADS * HEAD_DIM
WINDOW = 128
BLOCK = 128
NUM_BUCKETS = 32
MAX_DISTANCE = 128
LRU_WIDTH = D_MODEL - ATTN_WIDTH
LRU_BLOCKS = 8
LRU_BLOCK_DIM = LRU_WIDTH // LRU_BLOCKS
LRU_C = 8.0
CONV_W = 4
CONV_LEFT = 2
N_GROUPS = 4
EXPERTS_PER_GROUP = 8
N_EXPERTS = N_GROUPS * EXPERTS_PER_GROUP
TOP_K = 2
D_EXPERT = 512
MOE_BLOCK = 128
EPS = 1e-6
NEG_INF = -1e30

LANES = 128
SUBLANES = 8
VMEM_LIMIT = 56 * 1024 * 1024
LRU_VMEM_LIMIT = 62 * 1024 * 1024

F32 = jnp.float32
BF16 = jnp.bfloat16
LOG2E = math.log2(math.e)


def _cparams(n_axes, vmem=VMEM_LIMIT):
    return pltpu.CompilerParams(dimension_semantics=("arbitrary",) * n_axes, vmem_limit_bytes=vmem)


def _rms(x, gain):
    return x * lax.rsqrt(jnp.mean(x * x, axis=-1, keepdims=True) + EPS) * gain


U32 = jnp.uint32
HI_MASK = 0xFFFF0000
PACKED = D_MODEL // 2


def _pack_rows(x):
    return _pack_rounded(x.astype(BF16).astype(F32))


def _pack_rounded(xb):
    h = xb.shape[1] // 2
    lo = lax.bitcast_convert_type(xb[:, :h], U32) >> 16
    hi = lax.bitcast_convert_type(xb[:, h:], U32) & jnp.uint32(HI_MASK)
    return lo | hi


def _unpack_rows(p):
    lo = lax.bitcast_convert_type(p << 16, F32)
    hi = lax.bitcast_convert_type(p & jnp.uint32(HI_MASK), F32)
    return lo, hi


IN_TM = 1024


def _head_rms(x, n_heads, gain):
    head = lax.broadcasted_iota(jnp.int32, (1, n_heads * HEAD_DIM), 1) // HEAD_DIM
    x2 = x * x
    scale = jnp.zeros_like(x)
    for h in range(n_heads):
        ms = jnp.sum(jnp.where(head == h, x2, 0.0), axis=-1, keepdims=True) * (1.0 / HEAD_DIM)
        scale = jnp.where(head == h, lax.rsqrt(ms + EPS), scale)
    return x * scale * gain


def _in_proj_kernel(x_ref, g_ref, w_ref, qn_ref, kn_ref, q_ref, kv_ref, xr_ref, gr_ref, wb_ref, qg_ref, kg_ref):
    @pl.when(pl.program_id(0) == 0)
    def _():
        wb_ref[...] = w_ref[...].astype(BF16)
        qg_ref[...] = jnp.concatenate([qn_ref[...]] * N_HEADS, axis=1) * (HEAD_DIM ** -0.5 * LOG2E)
        kg_ref[...] = jnp.concatenate([kn_ref[...]] * N_KV_HEADS, axis=1)

    h = _rms(x_ref[...], g_ref[...]).astype(BF16)
    c_k = ATTN_WIDTH
    c_v = c_k + KV_WIDTH
    c_x = c_v + KV_WIDTH
    c_g = c_x + LRU_WIDTH
    q = jnp.dot(h, wb_ref[:, :c_k], preferred_element_type=F32)
    q_ref[...] = _head_rms(q, N_HEADS, qg_ref[...]).astype(BF16)
    k = jnp.dot(h, wb_ref[:, c_k:c_v], preferred_element_type=F32)
    kv_ref[:, :KV_WIDTH] = _head_rms(k, N_KV_HEADS, kg_ref[...]).astype(BF16)
    kv_ref[:, KV_WIDTH:] = jnp.dot(h, wb_ref[:, c_v:c_x], preferred_element_type=F32).astype(BF16)
    xr_ref[...] = jnp.dot(h, wb_ref[:, c_x:c_g], preferred_element_type=F32)
    gr_ref[...] = jnp.dot(h, wb_ref[:, c_g:], preferred_element_type=F32)


def _in_proj(x2, ln1, w_in, q_gain, k_gain):
    T = x2.shape[0]
    n_in = w_in.shape[1]
    row = lambda w: pl.BlockSpec((IN_TM, w), lambda i: (i, 0))
    return pl.pallas_call(
        _in_proj_kernel,
        grid=(T // IN_TM,),
        in_specs=[row(D_MODEL),
                  pl.BlockSpec((1, D_MODEL), lambda i: (0, 0)),
                  pl.BlockSpec((D_MODEL, n_in), lambda i: (0, 0)),
                  pl.BlockSpec((1, HEAD_DIM), lambda i: (0, 0)),
                  pl.BlockSpec((1, HEAD_DIM), lambda i: (0, 0))],
        out_specs=[row(ATTN_WIDTH), row(2 * KV_WIDTH), row(LRU_WIDTH), row(LRU_WIDTH)],
        out_shape=[jax.ShapeDtypeStruct((T, ATTN_WIDTH), BF16),
                   jax.ShapeDtypeStruct((T, 2 * KV_WIDTH), BF16),
                   jax.ShapeDtypeStruct((T, LRU_WIDTH), F32),
                   jax.ShapeDtypeStruct((T, LRU_WIDTH), F32)],
        scratch_shapes=[pltpu.VMEM((D_MODEL, n_in), BF16),
                        pltpu.VMEM((1, ATTN_WIDTH), F32),
                        pltpu.VMEM((1, KV_WIDTH), F32)],
        compiler_params=_cparams(1),
        name="in_proj",
    )(x2, ln1.reshape(1, D_MODEL), w_in, q_gain.reshape(1, HEAD_DIM).astype(F32),
      k_gain.reshape(1, HEAD_DIM).astype(F32))


def _t5_bucket(rel):
    half = NUM_BUCKETS // 2
    max_exact = half // 2
    base = jnp.where(rel > 0, half, 0)
    n = jnp.abs(rel)
    nf = jnp.maximum(n, 1).astype(jnp.float32)
    large = max_exact + (jnp.log(nf / max_exact) / math.log(MAX_DISTANCE / max_exact)
                         * (half - max_exact)).astype(jnp.int32)
    large = jnp.minimum(large, half - 1)
    return base + jnp.where(n < max_exact, n, large)


HEAD_PAIRS = Q_PER_KV // 2
EDGE_VARIANTS = 3


def _fill_bias_table(rb_ref, bucket_ref, band_ref, o_ref):
    bucket = bucket_ref[...]
    band = band_ref[...] > 0
    col = lax.broadcasted_iota(jnp.int32, bucket.shape, 1)
    valid = (band & (col >= BLOCK), band, band & (col < 2 * BLOCK))
    for h in range(N_HEADS):
        acc = jnp.zeros(bucket.shape, F32)
        for b in range(NUM_BUCKETS):
            acc = jnp.where(bucket == b, rb_ref[b, h], acc)
        kv, g = divmod(h, Q_PER_KV)
        pair, parity = divmod(g, 2)
        for var in range(EDGE_VARIANTS):
            o_ref[var, kv, parity, pair * BLOCK:(pair + 1) * BLOCK, :] = jnp.where(valid[var], acc * LOG2E, NEG_INF)


def _attn_kernel(sink_ref, rb_ref, q_ref, kp_ref, kc_ref, kn_ref, bucket_ref, band_ref, og_ref, o_ref, bias_ref):
    n = pl.program_id(1)

    @pl.when((pl.program_id(0) == 0) & (n == 0))
    def _():
        _fill_bias_table(rb_ref, bucket_ref, band_ref, bias_ref)

    kv_all = jnp.concatenate([kp_ref[0], kc_ref[0], kn_ref[0]], axis=0)
    for qb in range(ATTN_QB):
        variant = 1
        if qb == 0:
            variant = jnp.where(n == 0, 0, 1)
        if qb == ATTN_QB - 1:
            variant = jnp.where(n == pl.num_programs(1) - 1, 2, variant)
        out = _attn_block(q_ref[0, qb * BLOCK:(qb + 1) * BLOCK, :], kv_all[qb * BLOCK:(qb + 3) * BLOCK, :],
                          lambda kv, parity: bias_ref[variant, kv, parity], sink_ref)
        o_ref[0, qb * BLOCK:(qb + 1) * BLOCK, :] = _rms(out, og_ref[...]).astype(o_ref.dtype)


def _attn_block(q, kvw, bias, sink_ref):
    low = lax.broadcasted_iota(jnp.int32, (1, LANES), 1) < HEAD_DIM
    swap = lambda slab: pltpu.roll(slab.astype(F32), HEAD_DIM, 1).astype(BF16)
    kslab, vslab = kvw[:, :KV_WIDTH], kvw[:, KV_WIDTH:]
    kslab_sw, vslab_sw = swap(kslab), swap(vslab)
    rowi = lax.broadcasted_iota(jnp.int32, (HEAD_PAIRS * BLOCK, 1), 0)
    combos = [(kv, parity) for kv in range(N_KV_HEADS) for parity in range(2)]
    scores, vzs, sinks = [], [], []
    for kv, parity in combos:
        ks, vs = (kslab, vslab) if (kv == 0) == (parity == 0) else (kslab_sw, vslab_sw)
        keep = low if parity == 0 else jnp.logical_not(low)
        kz = jnp.where(keep, ks, jnp.zeros_like(ks))
        vzs.append(jnp.where(keep, vs, jnp.zeros_like(vs)))
        base = kv * Q_PER_KV * HEAD_DIM
        qpair = jnp.concatenate([q[:, base + j * LANES:base + (j + 1) * LANES] for j in range(HEAD_PAIRS)], axis=0)
        s = lax.dot_general(qpair, kz, (((1,), (1,)), ((), ())), preferred_element_type=F32)
        scores.append(s + bias(kv, parity))
        sink = jnp.zeros((HEAD_PAIRS * BLOCK, 1), F32)
        for j in range(HEAD_PAIRS):
            sink = jnp.where(rowi // BLOCK == j, sink_ref[kv * Q_PER_KV + 2 * j + parity], sink)
        sinks.append(sink)
    probs, inv = [], []
    for s, sink in zip(scores, sinks):
        m = jnp.maximum(jnp.max(s, axis=-1, keepdims=True), sink)
        p = jnp.exp2(s - m)
        inv.append(1.0 / (jnp.sum(p, axis=-1, keepdims=True) + jnp.exp2(sink - m)))
        probs.append(p.astype(BF16))
    outs = [jnp.dot(p, vz, preferred_element_type=F32) * r for p, vz, r in zip(probs, vzs, inv)]
    cols = []
    for kv in range(N_KV_HEADS):
        acc = outs[2 * kv] + outs[2 * kv + 1]
        cols += [acc[j * BLOCK:(j + 1) * BLOCK, :] for j in range(HEAD_PAIRS)]
    return jnp.concatenate(cols, axis=1)


ATTN_QB = 8


def _attention(q, kv, rel_bias, sink, out_gain):
    B, S, _ = q.shape
    nb = S // BLOCK
    assert ATTN_QB >= 2 and nb % ATTN_QB == 0, "a step's first and last query blocks must be distinct"
    ns = nb // ATTN_QB
    rows = ATTN_QB * BLOCK
    qi = jnp.arange(BLOCK, dtype=jnp.int32)
    kj = jnp.arange(3 * BLOCK, dtype=jnp.int32)
    rel = kj[None, :] - BLOCK - qi[:, None]
    bucket = _t5_bucket(rel).astype(jnp.int32)
    band = (jnp.abs(rel) <= WINDOW).astype(jnp.int32)
    kvspec = lambda f: pl.BlockSpec((1, BLOCK, 2 * KV_WIDTH), f)
    smem = pl.BlockSpec(memory_space=pltpu.SMEM)
    geom = pl.BlockSpec((BLOCK, 3 * BLOCK), lambda b, n: (0, 0))
    return pl.pallas_call(
        _attn_kernel,
        grid=(B, ns),
        in_specs=[smem, smem,
                  pl.BlockSpec((1, rows, ATTN_WIDTH), lambda b, n: (b, n, 0)),
                  kvspec(lambda b, n: (b, jnp.maximum(n * ATTN_QB - 1, 0), 0)),
                  pl.BlockSpec((1, rows, 2 * KV_WIDTH), lambda b, n: (b, n, 0)),
                  kvspec(lambda b, n: (b, jnp.minimum((n + 1) * ATTN_QB, nb - 1), 0)),
                  geom, geom,
                  pl.BlockSpec((1, ATTN_WIDTH), lambda b, n: (0, 0))],
        out_specs=pl.BlockSpec((1, rows, ATTN_WIDTH), lambda b, n: (b, n, 0)),
        out_shape=jax.ShapeDtypeStruct((B, S, ATTN_WIDTH), BF16),
        scratch_shapes=[pltpu.VMEM((EDGE_VARIANTS, N_KV_HEADS, 2, HEAD_PAIRS * BLOCK, 3 * BLOCK), F32)],
        compiler_params=_cparams(2),
        name="attention",
    )(sink.astype(F32) * LOG2E, rel_bias.astype(F32), q, kv, kv, kv, bucket, band, out_gain.reshape(1, ATTN_WIDTH))


LRU_TC = 128
LRU_PITCH = LRU_TC + SUBLANES // 2
LRU_SLABS = LRU_WIDTH // LANES
LRU_UNROLL = 8
HALO = SUBLANES


def _softplus(x):
    return jnp.maximum(x, 0.0) + jnp.log(1.0 + jnp.exp(-jnp.abs(x)))


def _gelu_tanh(x):
    k = math.sqrt(2.0 / math.pi)
    hx = 0.5 * x
    return hx + hx * jnp.tanh(x * (k + (k * 0.044715) * (x * x)))


def _sigmoid(x):
    return 0.5 + 0.5 * jnp.tanh(0.5 * x)


def _rglru_kernel(xr_ref, xp_ref, xn_ref, gr_ref, cw_ref, cb_ref, wa_ref, wi_ref, ba_ref, bi_ref, lam_ref, og_ref,
                  o_ref, sx_ref, a_ref, u_ref, h_ref, carry_ref, hf_ref, xcs_ref, wg_ref, bg_ref, k_ref):
    p = pl.program_id(0)
    i = pl.program_id(1)
    nc = pl.num_programs(1)
    c = i + p * (nc - 1 - 2 * i)
    B = xr_ref.shape[0]
    TC = LRU_TC

    @pl.when(i == 0)
    def _():
        carry_ref[...] = jnp.zeros_like(carry_ref)
        wg_ref[...] = jnp.zeros_like(wg_ref)
        for sel, w_ref in enumerate((wa_ref, wi_ref)):
            for h in range(LRU_BLOCKS):
                lo = h * LRU_BLOCK_DIM
                wg_ref[lo:lo + LRU_BLOCK_DIM, sel * LRU_WIDTH + lo:sel * LRU_WIDTH + lo + LRU_BLOCK_DIM] = (
                    0.5 * w_ref[0, h]).astype(BF16)
        row = pl.ds(p, 1)
        bg_ref[:, :LRU_WIDTH] = 0.5 * ba_ref[row, :]
        bg_ref[:, LRU_WIDTH:] = 0.5 * bi_ref[row, :]
        k_ref[...] = (-0.5 * LRU_C * math.log2(math.e)) * _softplus(-lam_ref[row, :])

    def gates_and_scan(xc2, backward):
        g = jnp.dot(xc2.astype(BF16), wg_ref[...], preferred_element_type=F32) + bg_ref[...]
        ta = jnp.tanh(g[:, :LRU_WIDTH])
        ig = 0.5 + 0.5 * jnp.tanh(g[:, LRU_WIDTH:])
        a = jnp.exp2((1.0 + ta) * k_ref[...])
        z = 1.0 - a * a
        u = z * lax.rsqrt(jnp.maximum(z, 1e-30)) * ig * xc2
        for b in range(B):
            for s in range(LRU_SLABS):
                a_ref[s, b * LRU_PITCH:b * LRU_PITCH + TC, :] = a[b * TC:(b + 1) * TC, s * LANES:(s + 1) * LANES]
                u_ref[s, b * LRU_PITCH:b * LRU_PITCH + TC, :] = u[b * TC:(b + 1) * TC, s * LANES:(s + 1) * LANES]

        def trip(i, hs):
            t0 = pl.multiple_of((TC // LRU_UNROLL - 1 - i if backward else i) * LRU_UNROLL, LRU_UNROLL)
            for j in range(LRU_UNROLL):
                t = t0 + (LRU_UNROLL - 1 - j if backward else j)
                out = []
                for s in range(LRU_SLABS):
                    idx = pl.ds(t, B, stride=LRU_PITCH)
                    hn = a_ref[s, idx, :] * hs[s] + u_ref[s, idx, :]
                    h_ref[s, idx, :] = hn
                    out.append(hn)
                hs = tuple(out)
            return hs

        hs = lax.fori_loop(0, TC // LRU_UNROLL, trip, tuple(carry_ref[s] for s in range(LRU_SLABS)))
        for s in range(LRU_SLABS):
            carry_ref[s] = hs[s]

    @pl.when(p == 0)
    def _():
        sx_ref[:, HALO:HALO + TC, :] = xr_ref[...]
        sx_ref[:, 0:HALO, :] = jnp.where(c > 0, xp_ref[...], 0.0)
        sx_ref[:, HALO + TC:, :] = jnp.where(c < nc - 1, xn_ref[...], 0.0)
        xc = cb_ref[...][None]
        for j in range(CONV_W):
            off = HALO + j - CONV_LEFT
            xc = xc + cw_ref[j:j + 1, :][None] * sx_ref[:, off:off + TC, :]
        xc2 = xc.reshape(B * TC, LRU_WIDTH)
        xcs_ref[c] = xc2.astype(xcs_ref.dtype)
        gates_and_scan(xc2, backward=False)
        for b in range(B):
            for s in range(LRU_SLABS):
                hf_ref[c, s, b * TC:(b + 1) * TC, :] = h_ref[s, b * LRU_PITCH:b * LRU_PITCH + TC, :].astype(hf_ref.dtype)

    @pl.when(p == 1)
    def _():
        gates_and_scan(xcs_ref[c].astype(F32), backward=True)
        for b in range(B):
            hsum = jnp.concatenate(
                [h_ref[s, b * LRU_PITCH:b * LRU_PITCH + TC, :] + hf_ref[c, s, b * TC:(b + 1) * TC, :].astype(F32)
                 for s in range(LRU_SLABS)], axis=1)
            y = hsum * _gelu_tanh(gr_ref[b])
            o_ref[b] = _rms(y, og_ref[...]).astype(o_ref.dtype)


def _rglru(xr, gr, conv_w, conv_b, w_a, b_a, w_i, b_i, lam, out_gain):
    B, S, W = xr.shape
    nc = S // LRU_TC
    hb = LRU_TC // HALO
    fwd = lambda p, i: jnp.where(p == 0, i, nc - 1)
    bwd = lambda p, i: nc - 1 - p * i
    full2 = lambda shape: pl.BlockSpec(shape, lambda p, i: (0,) * len(shape))
    wblock = pl.BlockSpec((1, LRU_BLOCKS, LRU_BLOCK_DIM, LRU_BLOCK_DIM), lambda p, i: (p, 0, 0, 0))
    return pl.pallas_call(
        _rglru_kernel,
        grid=(2, nc),
        in_specs=[pl.BlockSpec((B, LRU_TC, W), lambda p, i: (0, fwd(p, i), 0)),
                  pl.BlockSpec((B, HALO, W), lambda p, i: (0, jnp.maximum(fwd(p, i) * hb - 1, 0), 0)),
                  pl.BlockSpec((B, HALO, W), lambda p, i: (0, jnp.minimum((fwd(p, i) + 1) * hb, S // HALO - 1), 0)),
                  pl.BlockSpec((B, LRU_TC, W), lambda p, i: (0, bwd(p, i), 0)),
                  full2((CONV_W, W)),
                  full2((1, W)),
                  wblock, wblock,
                  full2((2, W)), full2((2, W)), full2((2, W)),
                  full2((1, W))],
        out_specs=pl.BlockSpec((B, LRU_TC, W), lambda p, i: (0, bwd(p, i), 0)),
        out_shape=jax.ShapeDtypeStruct((B, S, W), BF16),
        scratch_shapes=[pltpu.VMEM((B, LRU_TC + 2 * HALO, W), F32),
                        pltpu.VMEM((LRU_SLABS, B * LRU_PITCH, LANES), F32),
                        pltpu.VMEM((LRU_SLABS, B * LRU_PITCH, LANES), F32),
                        pltpu.VMEM((LRU_SLABS, B * LRU_PITCH, LANES), F32),
                        pltpu.VMEM((LRU_SLABS, B, LANES), F32),
                        pltpu.VMEM((nc, LRU_SLABS, B * LRU_TC, LANES), BF16),
                        pltpu.VMEM((nc, B * LRU_TC, W), BF16),
                        pltpu.VMEM((W, 2 * W), BF16),
                        pltpu.VMEM((1, 2 * W), F32),
                        pltpu.VMEM((1, W), F32)],
        compiler_params=_cparams(2, LRU_VMEM_LIMIT),
        name="rglru",
    )(xr, xr, xr, gr, conv_w.astype(F32), conv_b.reshape(1, W).astype(F32), w_a.astype(F32), w_i.astype(F32),
      b_a.astype(F32), b_i.astype(F32), lam.astype(F32), out_gain.reshape(1, W).astype(F32))


RT_TM = 1024
RT_PARTS = 4
RT_COLS = LANES
RT_ROWS = 48
RINFO = SUBLANES


def _split_bf16(x):
    hi = x.astype(BF16)
    lo = (x - hi.astype(F32)).astype(BF16)
    return hi, lo


def _route_kernel(an_ref, ln_ref, x_ref, wo_ref, g2_ref, wr_ref, br_ref,
                  x1_ref, h2_ref, gt_ref, ei_ref, cnt_ref, wob_ref, wrb_ref, tri_ref, run_ref, runc_ref):
    @pl.when(pl.program_id(0) == 0)
    def _():
        wob_ref[...] = wo_ref[...].astype(BF16)
        hi, lo = _split_bf16(wr_ref[...])
        wrb_ref[:RT_ROWS, :] = hi
        wrb_ref[RT_ROWS:, :] = lo
        r = lax.broadcasted_iota(jnp.int32, (RT_TM, RT_TM), 0)
        cidx = lax.broadcasted_iota(jnp.int32, (RT_TM, RT_TM), 1)
        tri_ref[...] = (r < cidx).astype(BF16)
        run_ref[...] = jnp.zeros_like(run_ref)
        runc_ref[...] = jnp.zeros_like(runc_ref)

    nt_dims = (((1,), (1,)), ((), ()))
    part = RT_TM // RT_PARTS
    x1s = []
    for r in range(RT_PARTS):
        rows = slice(r * part, (r + 1) * part)
        x1 = (x_ref[rows, :]
              + jnp.dot(an_ref[rows, :], wob_ref[:ATTN_WIDTH, :], preferred_element_type=F32)
              + jnp.dot(ln_ref[rows, :], wob_ref[ATTN_WIDTH:, :], preferred_element_type=F32))
        x1_ref[rows, :] = x1
        x1s.append(x1)
    splits = []
    for r, x1 in enumerate(x1s):
        h2 = _rms(x1, g2_ref[...])
        hi = h2.astype(BF16)
        hi_f = hi.astype(F32)
        h2_ref[r * part:(r + 1) * part, :] = _pack_rounded(hi_f)
        splits.append((hi, (h2 - hi_f).astype(BF16)))
    logits = []
    for hi, lo in splits:
        t1 = lax.dot_general(wrb_ref[...], hi, nt_dims, preferred_element_type=F32)
        t2 = lax.dot_general(wrb_ref[:RT_ROWS, :], lo, nt_dims, preferred_element_type=F32)
        logits.append(t1[:RT_ROWS] + t1[RT_ROWS:] + t2)
    logit = jnp.concatenate(logits, axis=1) + br_ref[...]

    sub = lax.broadcasted_iota(jnp.int32, (SUBLANES, RT_TM), 0)
    first_min = lambda hit: jnp.min(jnp.where(hit, sub, SUBLANES), axis=0, keepdims=True)
    is_g = sub < N_GROUPS
    gl = jnp.where(is_g, logit[:SUBLANES], -jnp.inf)
    gm = jnp.max(gl, axis=0, keepdims=True)
    gidx = first_min(gl == gm)
    g_p = 1.0 / jnp.sum(jnp.where(is_g, jnp.exp(logit[:SUBLANES] - gm), 0.0), axis=0, keepdims=True)
    el = logit[SUBLANES:2 * SUBLANES]
    for g in range(1, N_GROUPS):
        el = jnp.where(gidx == g, logit[(g + 1) * SUBLANES:(g + 2) * SUBLANES], el)
    m1 = jnp.max(el, axis=0, keepdims=True)
    i1 = first_min(el == m1)
    el2 = jnp.where(sub == i1, -jnp.inf, el)
    m2 = jnp.max(el2, axis=0, keepdims=True)
    i2 = first_min(el2 == m2)
    t = jnp.exp(m2 - m1)
    gate1 = g_p / (1.0 + t)
    gate2 = g_p * t / (1.0 + t)
    e1 = gidx * EXPERTS_PER_GROUP + i1
    e2 = gidx * EXPERTS_PER_GROUP + i2

    erow = lax.broadcasted_iota(jnp.int32, (N_EXPERTS, RT_TM), 0)
    oh1 = erow == e1
    oh2 = erow == e2
    oh = (oh1 | oh2).astype(F32)
    ohb = oh.astype(BF16)
    cum = jnp.dot(ohb, tri_ref[...], preferred_element_type=F32) + runc_ref[...]
    rank1 = jnp.sum(jnp.where(oh1, cum, 0.0), axis=0, keepdims=True)
    rank2 = jnp.sum(jnp.where(oh2, cum, 0.0), axis=0, keepdims=True)
    runc_ref[...] = runc_ref[...] + jnp.sum(oh, axis=1, keepdims=True)
    tile_cnt = lax.dot_general(jnp.ones((SUBLANES, RT_TM), BF16), ohb, nt_dims, preferred_element_type=F32)
    run_ref[:, :N_EXPERTS] = run_ref[:, :N_EXPERTS] + tile_cnt[0:1]
    cnt_ref[...] = run_ref[...].astype(jnp.int32)

    rows = [e1, e2, rank1.astype(jnp.int32), rank2.astype(jnp.int32)]
    ei = jnp.zeros((RINFO, RT_TM), jnp.int32)
    for k, v in enumerate(rows):
        ei = jnp.where(sub == k, v, ei)
    ei_ref[0] = ei
    gt_ref[...] = jnp.where(sub == 0, gate1, jnp.where(sub == 1, gate2, 0.0)).T


def _out_route(attn_n, lru_n, x2, w_out, ln2, w_group, b_group, w_er, b_er):
    T = x2.shape[0]
    pad_g = SUBLANES - N_GROUPS
    wr = jnp.concatenate([jnp.pad(w_group.T, ((0, pad_g), (0, 0))),
                          jnp.transpose(w_er, (0, 2, 1)).reshape(N_EXPERTS, D_MODEL)], axis=0)
    wr = jnp.pad(wr, ((0, RT_ROWS - wr.shape[0]), (0, 0))).astype(F32)
    br = jnp.concatenate([jnp.pad(b_group, (0, pad_g)), b_er.reshape(-1)])
    br = jnp.pad(br, (0, RT_ROWS - br.shape[0])).reshape(RT_ROWS, 1).astype(F32)
    row = lambda w: pl.BlockSpec((RT_TM, w), lambda i: (i, 0))
    const = lambda shape: pl.BlockSpec(shape, lambda i: (0, 0))
    return pl.pallas_call(
        _route_kernel,
        grid=(T // RT_TM,),
        in_specs=[row(ATTN_WIDTH), row(LRU_WIDTH), row(D_MODEL), const((D_MODEL, D_MODEL)), const((1, D_MODEL)),
                  const((RT_ROWS, D_MODEL)), const((RT_ROWS, 1))],
        out_specs=[row(D_MODEL), row(PACKED), row(RINFO),
                   pl.BlockSpec((1, RINFO, RT_TM), lambda i: (i, 0, 0)), const((1, RT_COLS))],
        out_shape=[jax.ShapeDtypeStruct((T, D_MODEL), F32),
                   jax.ShapeDtypeStruct((T, PACKED), U32),
                   jax.ShapeDtypeStruct((T, RINFO), F32),
                   jax.ShapeDtypeStruct((T // RT_TM, RINFO, RT_TM), jnp.int32),
                   jax.ShapeDtypeStruct((1, RT_COLS), jnp.int32)],
        scratch_shapes=[pltpu.VMEM((D_MODEL, D_MODEL), BF16),
                        pltpu.VMEM((2 * RT_ROWS, D_MODEL), BF16),
                        pltpu.VMEM((RT_TM, RT_TM), BF16),
                        pltpu.VMEM((1, RT_COLS), F32),
                        pltpu.VMEM((N_EXPERTS, 1), F32)],
        compiler_params=_cparams(1),
        name="out_route",
    )(attn_n, lru_n, x2, w_out, ln2.reshape(1, D_MODEL).astype(F32), wr, br)


def _moe_cap(T):
    A = T * TOP_K
    return ((A + MOE_BLOCK - 1) // MOE_BLOCK) * MOE_BLOCK + N_EXPERTS * MOE_BLOCK


PAD_BITS = tuple(1 << b for b in reversed(range(3, MOE_BLOCK.bit_length() - 1)))


def _layout_kernel(cnt_ref, ei_ref, dest_ref, pstart, be_ref, nu_ref, ge_ref):
    n_blocks = be_ref.shape[0]

    def lay(e, carry):
        start, blk, grp = carry
        pstart[e] = start
        nb = (cnt_ref[0, e] + MOE_BLOCK - 1) // MOE_BLOCK
        ge_ref[grp] = e

        def fill(k, c):
            be_ref[blk + k] = e
            return c
        lax.fori_loop(0, nb, fill, 0)
        return start + nb * MOE_BLOCK, blk + nb, grp + (nb > 0).astype(jnp.int32)
    _, used, groups = lax.fori_loop(0, N_EXPERTS, lay, (jnp.int32(0), jnp.int32(0), jnp.int32(0)))
    nu_ref[0] = used

    def tail(k, c):
        be_ref[k] = N_EXPERTS - 1
        return c
    lax.fori_loop(used, n_blocks, tail, 0)

    def no_group(k, c):
        ge_ref[k] = -1
        return c
    lax.fori_loop(groups, ge_ref.shape[0], no_group, 0)

    expert = ei_ref[:, 0:TOP_K, :]
    dest = ei_ref[:, TOP_K:2 * TOP_K, :]
    for e in range(N_EXPERTS):
        dest = dest + jnp.where(expert == e, pstart[e], 0)
    dest_ref[...] = dest


def _layout(ei, cnt, n_blocks):
    nt = ei.shape[0]
    smem = pl.BlockSpec(memory_space=pltpu.SMEM)
    vmem = pl.BlockSpec(memory_space=pltpu.VMEM)
    return pl.pallas_call(
        _layout_kernel,
        in_specs=[smem, vmem],
        out_specs=[vmem, smem, smem, smem, smem],
        out_shape=[jax.ShapeDtypeStruct((nt, TOP_K, RT_TM), jnp.int32),
                   jax.ShapeDtypeStruct((N_EXPERTS,), jnp.int32),
                   jax.ShapeDtypeStruct((n_blocks,), jnp.int32),
                   jax.ShapeDtypeStruct((1,), jnp.int32),
                   jax.ShapeDtypeStruct((N_EXPERTS + W_AHEAD,), jnp.int32)],
        name="layout",
    )(cnt, ei)


SC_CHUNK = 64
SC_BUFS = 3
SC_LEAD = SC_BUFS - 1


def _sc_workers():
    info = plsc.get_sparse_core_info()
    return info.num_cores, info.num_subcores


def _sc_ring(n_chunks, read, write):
    for c in range(min(SC_LEAD, n_chunks)):
        for cp in read(c):
            cp.start()
    reclaimed = set()
    for c in range(n_chunks):
        for cp in read(c):
            cp.wait()
        for cp in write(c):
            cp.start()
        nxt = c + SC_LEAD
        if nxt < n_chunks:
            if nxt - SC_BUFS >= 0:
                for cp in write(nxt - SC_BUFS):
                    cp.wait()
                reclaimed.add(nxt - SC_BUFS)
            for cp in read(nxt):
                cp.start()
    for c in range(n_chunks):
        if c not in reclaimed:
            for cp in write(c):
                cp.wait()


def _sc_dispatch(h2p, dest, cap):
    T = h2p.shape[0]
    nc, ns = _sc_workers()
    per_w = T // (nc * ns)
    n_ch = per_w // SC_CHUNK
    nt, _, tm = dest.shape
    assert nt * tm == T and tm % per_w == 0 and per_w % SC_CHUNK == 0
    idx = dest.reshape(nt, TOP_K, tm // per_w, per_w).transpose(0, 2, 1, 3).reshape(nc * ns, TOP_K * n_ch, SC_CHUNK)
    mesh = plsc.VectorSubcoreMesh(core_axis_name="c", subcore_axis_name="s")

    @functools.partial(
        pl.kernel, mesh=mesh,
        out_type=jax.ShapeDtypeStruct((cap, PACKED), U32),
        scratch_types=[pltpu.VMEM((TOP_K * n_ch, SC_CHUNK), jnp.int32),
                       pltpu.VMEM((SC_BUFS, SC_CHUNK, PACKED), U32),
                       pltpu.SemaphoreType.DMA((SC_BUFS,)),
                       pltpu.SemaphoreType.DMA((SC_BUFS,))])
    def scatter(src_hbm, idx_hbm, out_hbm, idx_v, rows_v, rsem, wsem):
        wid = lax.axis_index("s") * nc + lax.axis_index("c")
        base = pl.multiple_of(wid * per_w, per_w)
        pltpu.sync_copy(idx_hbm.at[wid], idx_v)

        def read(c):
            b = c % SC_BUFS
            return [pltpu.make_async_copy(src_hbm.at[pl.ds(base + c * SC_CHUNK, SC_CHUNK)], rows_v.at[b], rsem.at[b])]

        def write(c):
            b = c % SC_BUFS
            return [pltpu.make_async_copy(rows_v.at[b], out_hbm.at[idx_v.at[k * n_ch + c]], wsem.at[b])
                    for k in range(TOP_K)]
        _sc_ring(n_ch, read, write)

    return scatter(h2p, idx)


def _sc_gather(yb, dest):
    nt, _, tm = dest.shape
    nc, ns = _sc_workers()
    n_rows = nt * TOP_K * tm
    per_w = n_rows // (nc * ns)
    n_ch = per_w // SC_CHUNK
    assert per_w * nc * ns == n_rows and per_w % SC_CHUNK == 0
    mesh = plsc.VectorSubcoreMesh(core_axis_name="c", subcore_axis_name="s")

    @functools.partial(
        pl.kernel, mesh=mesh,
        out_type=jax.ShapeDtypeStruct((n_rows, PACKED), U32),
        scratch_types=[pltpu.VMEM((per_w,), jnp.int32),
                       pltpu.VMEM((SC_BUFS, SC_CHUNK, PACKED), U32),
                       pltpu.SemaphoreType.DMA((SC_BUFS,)),
                       pltpu.SemaphoreType.DMA((SC_BUFS,))])
    def gather(table_hbm, idx_hbm, out_hbm, idx_v, rows_v, gsem, wsem):
        wid = lax.axis_index("s") * nc + lax.axis_index("c")
        base = pl.multiple_of(wid * per_w, per_w)
        pltpu.sync_copy(idx_hbm.at[pl.ds(base, per_w)], idx_v)

        def read(c):
            b = c % SC_BUFS
            return [pltpu.make_async_copy(table_hbm.at[idx_v.at[pl.ds(c * SC_CHUNK, SC_CHUNK)]], rows_v.at[b], gsem.at[b])]

        def write(c):
            b = c % SC_BUFS
            return [pltpu.make_async_copy(rows_v.at[b], out_hbm.at[pl.ds(base + c * SC_CHUNK, SC_CHUNK)], wsem.at[b])]
        _sc_ring(n_ch, read, write)

    return gather(yb, dest.reshape(n_rows)).reshape(nt, TOP_K, tm, PACKED)


def _padfill_kernel(cnt_ref, pstart, xs_in, xs_ref, zeros, zsem):
    del xs_in

    def pad_copies(fn):
        for e in range(N_EXPERTS):
            cnt = cnt_ref[0, e]
            head = (-cnt) & (SUBLANES - 1)
            rest = ((-cnt) & (MOE_BLOCK - 1)) - head
            off = pstart[e] + cnt
            for k in range(SUBLANES - 1):
                @pl.when(k < head)
                def _(off=off, k=k):
                    fn(pltpu.make_async_copy(zeros.at[pl.ds(0, 1), :], xs_ref.at[pl.ds(off + k, 1), :], zsem))
            off = off + head
            for bit in PAD_BITS:
                @pl.when((rest & bit) != 0)
                def _(off=off, bit=bit):
                    fn(pltpu.make_async_copy(zeros.at[pl.ds(0, bit), :],
                                             xs_ref.at[pl.ds(pl.multiple_of(off, SUBLANES), bit), :], zsem))
                off = off + (rest & bit)

    zeros[...] = jnp.zeros_like(zeros)
    pad_copies(lambda cp: cp.start())
    pad_copies(lambda cp: cp.wait())


def _padfill(xs, pstart, cnt):
    smem = pl.BlockSpec(memory_space=pltpu.SMEM)
    hbm = pl.BlockSpec(memory_space=pl.ANY)
    return pl.pallas_call(
        _padfill_kernel,
        in_specs=[smem, smem, hbm],
        out_specs=hbm,
        out_shape=jax.ShapeDtypeStruct(xs.shape, xs.dtype),
        input_output_aliases={2: 0},
        scratch_shapes=[pltpu.VMEM((MOE_BLOCK // 2, PACKED), U32), pltpu.SemaphoreType.DMA(())],
        name="padfill",
    )(cnt, pstart, xs)


W_SLOTS = 3
W_AHEAD = W_SLOTS - 1
EXPERT_GROUP = 16
EXPERT_RUNS = (1, 2, 4, 8)


def _expert_kernel(be_ref, nu_ref, ge_ref, x_ref, wg_hbm, wu_hbm, wd_hbm, o_ref,
                   wgf, wuf, wdf, grp_ref, sems):
    step = pl.program_id(0)

    def weight_copies(e, slot):
        return (pltpu.make_async_copy(wg_hbm.at[e], wgf.at[slot], sems.at[slot, 0]),
                pltpu.make_async_copy(wu_hbm.at[e], wuf.at[slot], sems.at[slot, 1]),
                pltpu.make_async_copy(wd_hbm.at[e], wdf.at[slot], sems.at[slot, 2]))

    @pl.when(step == 0)
    def _():
        grp_ref[0] = 0
        for a in range(W_AHEAD):
            @pl.when(ge_ref[a] >= 0)
            def _(a=a):
                for cp in weight_copies(ge_ref[a], a):
                    cp.start()

    n_blocks = be_ref.shape[0]
    n_used = nu_ref[0]

    def swiglu(s, n, slot):
        rows = pl.ds(pl.multiple_of(s * MOE_BLOCK, MOE_BLOCK), n * MOE_BLOCK)
        lo, hi = _unpack_rows(x_ref[rows, :])
        lo = lo.astype(BF16)
        hi = hi.astype(BF16)
        g = (jnp.dot(lo, wgf[slot, :PACKED, :], preferred_element_type=F32)
             + jnp.dot(hi, wgf[slot, PACKED:, :], preferred_element_type=F32))
        u = (jnp.dot(lo, wuf[slot, :PACKED, :], preferred_element_type=F32)
             + jnp.dot(hi, wuf[slot, PACKED:, :], preferred_element_type=F32))
        h = (g * _sigmoid(g) * u).astype(BF16)
        o_ref[rows, :] = _pack_rows(jnp.dot(h, wdf[slot], preferred_element_type=F32))

    def run(s):
        j = step * EXPERT_GROUP + s
        e = be_ref[j]
        first = jnp.logical_or(j == 0, e != be_ref[jnp.maximum(j - 1, 0)])

        @pl.when(first)
        def _():
            grp = grp_ref[0]
            slot = grp % W_SLOTS
            for cp in weight_copies(e, slot):
                cp.wait()
            nxt = ge_ref[grp + W_AHEAD]

            @pl.when(nxt >= 0)
            def _():
                for cp in weight_copies(nxt, (grp + W_AHEAD) % W_SLOTS):
                    cp.start()
            grp_ref[0] = grp + 1

        def same(k):
            return (s + k < EXPERT_GROUP) & (j + k < n_used) & (be_ref[jnp.minimum(j + k, n_blocks - 1)] == e)
        take = jnp.int32(1)
        for n in EXPERT_RUNS[1:]:
            ok = same(n - 1)
            for k in range(1, n - 1):
                ok = ok & same(k)
            take = jnp.where(ok, n, take)
        slot = (grp_ref[0] + W_SLOTS - 1) % W_SLOTS
        for n in EXPERT_RUNS:
            @pl.when(take == n)
            def _(n=n):
                swiglu(s, n, slot)
        return s + take

    lax.while_loop(lambda s: (s < EXPERT_GROUP) & (step * EXPERT_GROUP + s < n_used), run, jnp.int32(0))


def _experts(xs, block_expert, n_used, group_expert, w_gate, w_up, w_down):
    cap = xs.shape[0]
    n_blocks = cap // MOE_BLOCK
    assert n_blocks % EXPERT_GROUP == 0
    rows = EXPERT_GROUP * MOE_BLOCK
    last = lambda j, be, nu, ge: jnp.minimum(j, (nu[0] - 1) // EXPERT_GROUP)
    hbm = pl.BlockSpec(memory_space=pl.ANY)
    gs = pltpu.PrefetchScalarGridSpec(
        num_scalar_prefetch=3,
        grid=(n_blocks // EXPERT_GROUP,),
        in_specs=[pl.BlockSpec((rows, PACKED), lambda j, be, nu, ge: (last(j, be, nu, ge), 0)), hbm, hbm, hbm],
        out_specs=pl.BlockSpec((rows, PACKED), lambda j, be, nu, ge: (last(j, be, nu, ge), 0)),
        scratch_shapes=[pltpu.VMEM((W_SLOTS, D_MODEL, D_EXPERT), F32),
                        pltpu.VMEM((W_SLOTS, D_MODEL, D_EXPERT), F32),
                        pltpu.VMEM((W_SLOTS, D_EXPERT, D_MODEL), F32),
                        pltpu.SMEM((1,), jnp.int32),
                        pltpu.SemaphoreType.DMA((W_SLOTS, 3))],
    )
    return pl.pallas_call(
        _expert_kernel,
        grid_spec=gs,
        out_shape=jax.ShapeDtypeStruct((cap, PACKED), U32),
        compiler_params=_cparams(1),
        name="experts",
    )(block_expert, n_used, group_expert, xs, w_gate, w_up, w_down)


CB_TM = RT_TM


def _combine_kernel(x1_ref, gt_ref, y2_ref, o_ref):
    g = gt_ref[...]
    lo1, hi1 = _unpack_rows(y2_ref[0, 0])
    lo2, hi2 = _unpack_rows(y2_ref[0, 1])
    o_ref[:, :PACKED] = x1_ref[:, :PACKED] + g[:, 0:1] * lo1 + g[:, 1:2] * lo2
    o_ref[:, PACKED:] = x1_ref[:, PACKED:] + g[:, 0:1] * hi1 + g[:, 1:2] * hi2


def _combine(x1, gates, y2):
    T = x1.shape[0]
    nt = T // CB_TM
    return pl.pallas_call(
        _combine_kernel,
        grid=(nt,),
        in_specs=[pl.BlockSpec((CB_TM, D_MODEL), lambda i: (i, 0)),
                  pl.BlockSpec((CB_TM, RINFO), lambda i: (i, 0)),
                  pl.BlockSpec((1, TOP_K, CB_TM, PACKED), lambda i: (i, 0, 0, 0))],
        out_specs=pl.BlockSpec((CB_TM, D_MODEL), lambda i: (i, 0)),
        out_shape=jax.ShapeDtypeStruct((T, D_MODEL), F32),
        compiler_params=_cparams(1),
        name="combine",
    )(x1, gates, y2)


def _layer(x, rel_bias, ln1, w_in, q_norm, k_norm, attn_sink, conv_w, conv_b, lru_wa, lru_ba, lru_wi, lru_bi,
           lru_lambda, out_norm_attn, out_norm_lru, w_out, ln2, w_group, b_group, w_er, b_er, w_gate, w_up, w_down):
    B, S, D = x.shape
    T = B * S
    x2 = x.reshape(T, D)
    q, kv, xr, gr = _in_proj(x2, ln1, w_in, q_norm, k_norm)
    attn_n = _attention(q.reshape(B, S, ATTN_WIDTH), kv.reshape(B, S, 2 * KV_WIDTH), rel_bias, attn_sink,
                        out_norm_attn)
    lru_n = _rglru(xr.reshape(B, S, LRU_WIDTH), gr.reshape(B, S, LRU_WIDTH), conv_w, conv_b,
                   lru_wa, lru_ba, lru_wi, lru_bi, lru_lambda, out_norm_lru)
    x1, h2, gates, ei, cnt = _out_route(attn_n.reshape(T, ATTN_WIDTH), lru_n.reshape(T, LRU_WIDTH), x2, w_out, ln2,
                                        w_group, b_group, w_er, b_er)
    cap = _moe_cap(T)
    dest, pstart, block_expert, n_used, group_expert = _layout(ei, cnt, cap // MOE_BLOCK)
    xs = _padfill(_sc_dispatch(h2, dest, cap), pstart, cnt)
    yb = _experts(xs, block_expert, n_used, group_expert, w_gate, w_up, w_down)
    out = _combine(x1, gates, _sc_gather(yb, dest))
    return out.reshape(B, S, D)


def kernel(x, rel_bias, ln1, w_in, q_norm, k_norm, attn_sink, conv_w, conv_b, lru_wa, lru_ba, lru_wi, lru_bi,
           lru_lambda, out_norm_attn, out_norm_lru, w_out, ln2, w_group, b_group, w_expert_router, b_expert_router,
           w_gate, w_up, w_down):
    depth = ln1.shape[0]
    for l in range(depth):
        x = _layer(x, rel_bias, ln1[l], w_in[l], q_norm[l], k_norm[l], attn_sink[l], conv_w[l], conv_b[l],
                   lru_wa[l], lru_ba[l], lru_wi[l], lru_bi[l], lru_lambda[l], out_norm_attn[l], out_norm_lru[l],
                   w_out[l], ln2[l], w_group[l], b_group[l], w_expert_router[l], b_expert_router[l],
                   w_gate[l], w_up[l], w_down[l])
    return x
```

```python
import functools
import math

import jax
import jax.numpy as jnp
from jax import lax
from jax.experimental import pallas as pl
from jax.experimental.pallas import tpu as pltpu
from jax.experimental.pallas import tpu_sc as plsc

D_MODEL = 1024
N_HEADS = 8
N_KV_HEADS = 2
HEAD_DIM = 64
Q_PER_KV = N_HEADS // N_KV_HEADS
ATTN_WIDTH = N_HEADS * HEAD_DIM
KV_WIDTH = N_KV_HEADS * HEAD_DIM
WINDOW = 128
BLOCK = 128
NUM_BUCKETS = 32
MAX_DISTANCE = 128
LRU_WIDTH = D_MODEL - ATTN_WIDTH
LRU_BLOCKS = 8
LRU_BLOCK_DIM = LRU_WIDTH // LRU_BLOCKS
LRU_C = 8.0
CONV_W = 4
CONV_LEFT = 2
N_GROUPS = 4
EXPERTS_PER_GROUP = 8
N_EXPERTS = N_GROUPS * EXPERTS_PER_GROUP
TOP_K = 2
D_EXPERT = 512
MOE_BLOCK = 128
EPS = 1e-6
NEG_INF = -1e30

LANES = 128
SUBLANES = 8
VMEM_LIMIT = 56 * 1024 * 1024
LRU_VMEM_LIMIT = 62 * 1024 * 1024

F32 = jnp.float32
BF16 = jnp.bfloat16
LOG2E = math.log2(math.e)


def _cparams(n_axes, vmem=VMEM_LIMIT):
    return pltpu.CompilerParams(dimension_semantics=("arbitrary",) * n_axes, vmem_limit_bytes=vmem)


def _rms(x, gain):
    return x * lax.rsqrt(jnp.mean(x * x, axis=-1, keepdims=True) + EPS) * gain


U32 = jnp.uint32
HI_MASK = 0xFFFF0000
PACKED = D_MODEL // 2


def _pack_rows(x):
    return _pack_rounded(x.astype(BF16).astype(F32))


def _pack_rounded(xb):
    h = xb.shape[1] // 2
    lo = lax.bitcast_convert_type(xb[:, :h], U32) >> 16
    hi = lax.bitcast_convert_type(xb[:, h:], U32) & jnp.uint32(HI_MASK)
    return lo | hi


def _unpack_rows(p):
    lo = lax.bitcast_convert_type(p << 16, F32)
    hi = lax.bitcast_convert_type(p & jnp.uint32(HI_MASK), F32)
    return lo, hi


IN_TM = 1024


def _head_rms(x, n_heads, gain):
    head = lax.broadcasted_iota(jnp.int32, (1, n_heads * HEAD_DIM), 1) // HEAD_DIM
    x2 = x * x
    scale = jnp.zeros_like(x)
    for h in range(n_heads):
        ms = jnp.sum(jnp.where(head == h, x2, 0.0), axis=-1, keepdims=True) * (1.0 / HEAD_DIM)
        scale = jnp.where(head == h, lax.rsqrt(ms + EPS), scale)
    return x * scale * gain


def _in_proj_kernel(x_ref, g_ref, w_ref, qn_ref, kn_ref, q_ref, kv_ref, xr_ref, gr_ref, wb_ref, qg_ref, kg_ref):
    @pl.when(pl.program_id(0) == 0)
    def _():
        wb_ref[...] = w_ref[...].astype(BF16)
        qg_ref[...] = jnp.concatenate([qn_ref[...]] * N_HEADS, axis=1) * (HEAD_DIM ** -0.5 * LOG2E)
        kg_ref[...] = jnp.concatenate([kn_ref[...]] * N_KV_HEADS, axis=1)

    h = _rms(x_ref[...], g_ref[...]).astype(BF16)
    c_k = ATTN_WIDTH
    c_v = c_k + KV_WIDTH
    c_x = c_v + KV_WIDTH
    c_g = c_x + LRU_WIDTH
    q = jnp.dot(h, wb_ref[:, :c_k], preferred_element_type=F32)
    q_ref[...] = _head_rms(q, N_HEADS, qg_ref[...]).astype(BF16)
    k = jnp.dot(h, wb_ref[:, c_k:c_v], preferred_element_type=F32)
    kv_ref[:, :KV_WIDTH] = _head_rms(k, N_KV_HEADS, kg_ref[...]).astype(BF16)
    kv_ref[:, KV_WIDTH:] = jnp.dot(h, wb_ref[:, c_v:c_x], preferred_element_type=F32).astype(BF16)
    xr_ref[...] = jnp.dot(h, wb_ref[:, c_x:c_g], preferred_element_type=F32)
    gr_ref[...] = jnp.dot(h, wb_ref[:, c_g:], preferred_element_type=F32)


def _in_proj(x2, ln1, w_in, q_gain, k_gain):
    T = x2.shape[0]
    n_in = w_in.shape[1]
    row = lambda w: pl.BlockSpec((IN_TM, w), lambda i: (i, 0))
    return pl.pallas_call(
        _in_proj_kernel,
        grid=(T // IN_TM,),
        in_specs=[row(D_MODEL),
                  pl.BlockSpec((1, D_MODEL), lambda i: (0, 0)),
                  pl.BlockSpec((D_MODEL, n_in), lambda i: (0, 0)),
                  pl.BlockSpec((1, HEAD_DIM), lambda i: (0, 0)),
                  pl.BlockSpec((1, HEAD_DIM), lambda i: (0, 0))],
        out_specs=[row(ATTN_WIDTH), row(2 * KV_WIDTH), row(LRU_WIDTH), row(LRU_WIDTH)],
        out_shape=[jax.ShapeDtypeStruct((T, ATTN_WIDTH), BF16),
                   jax.ShapeDtypeStruct((T, 2 * KV_WIDTH), BF16),
                   jax.ShapeDtypeStruct((T, LRU_WIDTH), F32),
                   jax.ShapeDtypeStruct((T, LRU_WIDTH), F32)],
        scratch_shapes=[pltpu.VMEM((D_MODEL, n_in), BF16),
                        pltpu.VMEM((1, ATTN_WIDTH), F32),
                        pltpu.VMEM((1, KV_WIDTH), F32)],
        compiler_params=_cparams(1),
        name="in_proj",
    )(x2, ln1.reshape(1, D_MODEL), w_in, q_gain.reshape(1, HEAD_DIM).astype(F32),
      k_gain.reshape(1, HEAD_DIM).astype(F32))


def _t5_bucket(rel):
    half = NUM_BUCKETS // 2
    max_exact = half // 2
    base = jnp.where(rel > 0, half, 0)
    n = jnp.abs(rel)
    nf = jnp.maximum(n, 1).astype(jnp.float32)
    large = max_exact + (jnp.log(nf / max_exact) / math.log(MAX_DISTANCE / max_exact)
                         * (half - max_exact)).astype(jnp.int32)
    large = jnp.minimum(large, half - 1)
    return base + jnp.where(n < max_exact, n, large)


HEAD_PAIRS = Q_PER_KV // 2
EDGE_VARIANTS = 3


def _fill_bias_table(rb_ref, bucket_ref, band_ref, o_ref):
    bucket = bucket_ref[...]
    band = band_ref[...] > 0
    col = lax.broadcasted_iota(jnp.int32, bucket.shape, 1)
    valid = (band & (col >= BLOCK), band, band & (col < 2 * BLOCK))
    for h in range(N_HEADS):
        acc = jnp.zeros(bucket.shape, F32)
        for b in range(NUM_BUCKETS):
            acc = jnp.where(bucket == b, rb_ref[b, h], acc)
        kv, g = divmod(h, Q_PER_KV)
        pair, parity = divmod(g, 2)
        for var in range(EDGE_VARIANTS):
            o_ref[var, kv, parity, pair * BLOCK:(pair + 1) * BLOCK, :] = jnp.where(valid[var], acc * LOG2E, NEG_INF)


def _attn_kernel(sink_ref, rb_ref, q_ref, kp_ref, kc_ref, kn_ref, bucket_ref, band_ref, og_ref, o_ref, bias_ref):
    n = pl.program_id(1)

    @pl.when((pl.program_id(0) == 0) & (n == 0))
    def _():
        _fill_bias_table(rb_ref, bucket_ref, band_ref, bias_ref)

    kv_all = jnp.concatenate([kp_ref[0], kc_ref[0], kn_ref[0]], axis=0)
    for qb in range(ATTN_QB):
        variant = 1
        if qb == 0:
            variant = jnp.where(n == 0, 0, 1)
        if qb == ATTN_QB - 1:
            variant = jnp.where(n == pl.num_programs(1) - 1, 2, variant)
        out = _attn_block(q_ref[0, qb * BLOCK:(qb + 1) * BLOCK, :], kv_all[qb * BLOCK:(qb + 3) * BLOCK, :],
                          lambda kv, parity: bias_ref[variant, kv, parity], sink_ref)
        o_ref[0, qb * BLOCK:(qb + 1) * BLOCK, :] = _rms(out, og_ref[...]).astype(o_ref.dtype)


def _attn_block(q, kvw, bias, sink_ref):
    low = lax.broadcasted_iota(jnp.int32, (1, LANES), 1) < HEAD_DIM
    swap = lambda slab: pltpu.roll(slab.astype(F32), HEAD_DIM, 1).astype(BF16)
    kslab, vslab = kvw[:, :KV_WIDTH], kvw[:, KV_WIDTH:]
    kslab_sw, vslab_sw = swap(kslab), swap(vslab)
    rowi = lax.broadcasted_iota(jnp.int32, (HEAD_PAIRS * BLOCK, 1), 0)
    combos = [(kv, parity) for kv in range(N_KV_HEADS) for parity in range(2)]
    scores, vzs, sinks = [], [], []
    for kv, parity in combos:
        ks, vs = (kslab, vslab) if (kv == 0) == (parity == 0) else (kslab_sw, vslab_sw)
        keep = low if parity == 0 else jnp.logical_not(low)
        kz = jnp.where(keep, ks, jnp.zeros_like(ks))
        vzs.append(jnp.where(keep, vs, jnp.zeros_like(vs)))
        base = kv * Q_PER_KV * HEAD_DIM
        qpair = jnp.concatenate([q[:, base + j * LANES:base + (j + 1) * LANES] for j in range(HEAD_PAIRS)], axis=0)
        s = lax.dot_general(qpair, kz, (((1,), (1,)), ((), ())), preferred_element_type=F32)
        scores.append(s + bias(kv, parity))
        sink = jnp.zeros((HEAD_PAIRS * BLOCK, 1), F32)
        for j in range(HEAD_PAIRS):
            sink = jnp.where(rowi // BLOCK == j, sink_ref[kv * Q_PER_KV + 2 * j + parity], sink)
        sinks.append(sink)
    probs, inv = [], []
    for s, sink in zip(scores, sinks):
        m = jnp.maximum(jnp.max(s, axis=-1, keepdims=True), sink)
        p = jnp.exp2(s - m)
        inv.append(1.0 / (jnp.sum(p, axis=-1, keepdims=True) + jnp.exp2(sink - m)))
        probs.append(p.astype(BF16))
    outs = [jnp.dot(p, vz, preferred_element_type=F32) * r for p, vz, r in zip(probs, vzs, inv)]
    cols = []
    for kv in range(N_KV_HEADS):
        acc = outs[2 * kv] + outs[2 * kv + 1]
        cols += [acc[j * BLOCK:(j + 1) * BLOCK, :] for j in range(HEAD_PAIRS)]
    return jnp.concatenate(cols, axis=1)


ATTN_QB = 8


def _attention(q, kv, rel_bias, sink, out_gain):
    B, S, _ = q.shape
    nb = S // BLOCK
    assert ATTN_QB >= 2 and nb % ATTN_QB == 0, "a step's first and last query blocks must be distinct"
    ns = nb // ATTN_QB
    rows = ATTN_QB * BLOCK
    qi = jnp.arange(BLOCK, dtype=jnp.int32)
    kj = jnp.arange(3 * BLOCK, dtype=jnp.int32)
    rel = kj[None, :] - BLOCK - qi[:, None]
    bucket = _t5_bucket(rel).astype(jnp.int32)
    band = (jnp.abs(rel) <= WINDOW).astype(jnp.int32)
    kvspec = lambda f: pl.BlockSpec((1, BLOCK, 2 * KV_WIDTH), f)
    smem = pl.BlockSpec(memory_space=pltpu.SMEM)
    geom = pl.BlockSpec((BLOCK, 3 * BLOCK), lambda b, n: (0, 0))
    return pl.pallas_call(
        _attn_kernel,
        grid=(B, ns),
        in_specs=[smem, smem,
                  pl.BlockSpec((1, rows, ATTN_WIDTH), lambda b, n: (b, n, 0)),
                  kvspec(lambda b, n: (b, jnp.maximum(n * ATTN_QB - 1, 0), 0)),
                  pl.BlockSpec((1, rows, 2 * KV_WIDTH), lambda b, n: (b, n, 0)),
                  kvspec(lambda b, n: (b, jnp.minimum((n + 1) * ATTN_QB, nb - 1), 0)),
                  geom, geom,
                  pl.BlockSpec((1, ATTN_WIDTH), lambda b, n: (0, 0))],
        out_specs=pl.BlockSpec((1, rows, ATTN_WIDTH), lambda b, n: (b, n, 0)),
        out_shape=jax.ShapeDtypeStruct((B, S, ATTN_WIDTH), BF16),
        scratch_shapes=[pltpu.VMEM((EDGE_VARIANTS, N_KV_HEADS, 2, HEAD_PAIRS * BLOCK, 3 * BLOCK), F32)],
        compiler_params=_cparams(2),
        name="attention",
    )(sink.astype(F32) * LOG2E, rel_bias.astype(F32), q, kv, kv, kv, bucket, band, out_gain.reshape(1, ATTN_WIDTH))


LRU_TC = 128
LRU_PITCH = LRU_TC + SUBLANES // 2
LRU_SLABS = LRU_WIDTH // LANES
LRU_UNROLL = 8
HALO = SUBLANES


def _softplus(x):
    return jnp.maximum(x, 0.0) + jnp.log(1.0 + jnp.exp(-jnp.abs(x)))


def _gelu_tanh(x):
    k = math.sqrt(2.0 / math.pi)
    hx = 0.5 * x
    return hx + hx * jnp.tanh(x * (k + (k * 0.044715) * (x * x)))


def _sigmoid(x):
    return 0.5 + 0.5 * jnp.tanh(0.5 * x)


def _rglru_kernel(xr_ref, xp_ref, xn_ref, gr_ref, cw_ref, cb_ref, wa_ref, wi_ref, ba_ref, bi_ref, lam_ref, og_ref,
                  o_ref, sx_ref, a_ref, u_ref, h_ref, carry_ref, hf_ref, xcs_ref, wg_ref, bg_ref, k_ref):
    p = pl.program_id(0)
    i = pl.program_id(1)
    nc = pl.num_programs(1)
    c = i + p * (nc - 1 - 2 * i)
    B = xr_ref.shape[0]
    TC = LRU_TC

    @pl.when(i == 0)
    def _():
        carry_ref[...] = jnp.zeros_like(carry_ref)
        wg_ref[...] = jnp.zeros_like(wg_ref)
        for sel, w_ref in enumerate((wa_ref, wi_ref)):
            for h in range(LRU_BLOCKS):
                lo = h * LRU_BLOCK_DIM
                wg_ref[lo:lo + LRU_BLOCK_DIM, sel * LRU_WIDTH + lo:sel * LRU_WIDTH + lo + LRU_BLOCK_DIM] = (
                    0.5 * w_ref[0, h]).astype(BF16)
        row = pl.ds(p, 1)
        bg_ref[:, :LRU_WIDTH] = 0.5 * ba_ref[row, :]
        bg_ref[:, LRU_WIDTH:] = 0.5 * bi_ref[row, :]
        k_ref[...] = (-0.5 * LRU_C * math.log2(math.e)) * _softplus(-lam_ref[row, :])

    def gates_and_scan(xc2, backward):
        g = jnp.dot(xc2.astype(BF16), wg_ref[...], preferred_element_type=F32) + bg_ref[...]
        ta = jnp.tanh(g[:, :LRU_WIDTH])
        ig = 0.5 + 0.5 * jnp.tanh(g[:, LRU_WIDTH:])
        a = jnp.exp2((1.0 + ta) * k_ref[...])
        z = 1.0 - a * a
        u = z * lax.rsqrt(jnp.maximum(z, 1e-30)) * ig * xc2
        for b in range(B):
            for s in range(LRU_SLABS):
                a_ref[s, b * LRU_PITCH:b * LRU_PITCH + TC, :] = a[b * TC:(b + 1) * TC, s * LANES:(s + 1) * LANES]
                u_ref[s, b * LRU_PITCH:b * LRU_PITCH + TC, :] = u[b * TC:(b + 1) * TC, s * LANES:(s + 1) * LANES]

        def trip(i, hs):
            t0 = pl.multiple_of((TC // LRU_UNROLL - 1 - i if backward else i) * LRU_UNROLL, LRU_UNROLL)
            for j in range(LRU_UNROLL):
                t = t0 + (LRU_UNROLL - 1 - j if backward else j)
                out = []
                for s in range(LRU_SLABS):
                    idx = pl.ds(t, B, stride=LRU_PITCH)
                    hn = a_ref[s, idx, :] * hs[s] + u_ref[s, idx, :]
                    h_ref[s, idx, :] = hn
                    out.append(hn)
                hs = tuple(out)
            return hs

        hs = lax.fori_loop(0, TC // LRU_UNROLL, trip, tuple(carry_ref[s] for s in range(LRU_SLABS)))
        for s in range(LRU_SLABS):
            carry_ref[s] = hs[s]

    @pl.when(p == 0)
    def _():
        sx_ref[:, HALO:HALO + TC, :] = xr_ref[...]
        sx_ref[:, 0:HALO, :] = jnp.where(c > 0, xp_ref[...], 0.0)
        sx_ref[:, HALO + TC:, :] = jnp.where(c < nc - 1, xn_ref[...], 0.0)
        xc = cb_ref[...][None]
        for j in range(CONV_W):
            off = HALO + j - CONV_LEFT
            xc = xc + cw_ref[j:j + 1, :][None] * sx_ref[:, off:off + TC, :]
        xc2 = xc.reshape(B * TC, LRU_WIDTH)
        xcs_ref[c] = xc2.astype(xcs_ref.dtype)
        gates_and_scan(xc2, backward=False)
        for b in range(B):
            for s in range(LRU_SLABS):
                hf_ref[c, s, b * TC:(b + 1) * TC, :] = h_ref[s, b * LRU_PITCH:b * LRU_PITCH + TC, :].astype(hf_ref.dtype)

    @pl.when(p == 1)
    def _():
        gates_and_scan(xcs_ref[c].astype(F32), backward=True)
        for b in range(B):
            hsum = jnp.concatenate(
                [h_ref[s, b * LRU_PITCH:b * LRU_PITCH + TC, :] + hf_ref[c, s, b * TC:(b + 1) * TC, :].astype(F32)
                 for s in range(LRU_SLABS)], axis=1)
            y = hsum * _gelu_tanh(gr_ref[b])
            o_ref[b] = _rms(y, og_ref[...]).astype(o_ref.dtype)


def _rglru(xr, gr, conv_w, conv_b, w_a, b_a, w_i, b_i, lam, out_gain):
    B, S, W = xr.shape
    nc = S // LRU_TC
    hb = LRU_TC // HALO
    fwd = lambda p, i: jnp.where(p == 0, i, nc - 1)
    bwd = lambda p, i: nc - 1 - p * i
    full2 = lambda shape: pl.BlockSpec(shape, lambda p, i: (0,) * len(shape))
    wblock = pl.BlockSpec((1, LRU_BLOCKS, LRU_BLOCK_DIM, LRU_BLOCK_DIM), lambda p, i: (p, 0, 0, 0))
    return pl.pallas_call(
        _rglru_kernel,
        grid=(2, nc),
        in_specs=[pl.BlockSpec((B, LRU_TC, W), lambda p, i: (0, fwd(p, i), 0)),
                  pl.BlockSpec((B, HALO, W), lambda p, i: (0, jnp.maximum(fwd(p, i) * hb - 1, 0), 0)),
                  pl.BlockSpec((B, HALO, W), lambda p, i: (0, jnp.minimum((fwd(p, i) + 1) * hb, S // HALO - 1), 0)),
                  pl.BlockSpec((B, LRU_TC, W), lambda p, i: (0, bwd(p, i), 0)),
                  full2((CONV_W, W)),
                  full2((1, W)),
                  wblock, wblock,
                  full2((2, W)), full2((2, W)), full2((2, W)),
                  full2((1, W))],
        out_specs=pl.BlockSpec((B, LRU_TC, W), lambda p, i: (0, bwd(p, i), 0)),
        out_shape=jax.ShapeDtypeStruct((B, S, W), BF16),
        scratch_shapes=[pltpu.VMEM((B, LRU_TC + 2 * HALO, W), F32),
                        pltpu.VMEM((LRU_SLABS, B * LRU_PITCH, LANES), F32),
                        pltpu.VMEM((LRU_SLABS, B * LRU_PITCH, LANES), F32),
                        pltpu.VMEM((LRU_SLABS, B * LRU_PITCH, LANES), F32),
                        pltpu.VMEM((LRU_SLABS, B, LANES), F32),
                        pltpu.VMEM((nc, LRU_SLABS, B * LRU_TC, LANES), BF16),
                        pltpu.VMEM((nc, B * LRU_TC, W), BF16),
                        pltpu.VMEM((W, 2 * W), BF16),
                        pltpu.VMEM((1, 2 * W), F32),
                        pltpu.VMEM((1, W), F32)],
        compiler_params=_cparams(2, LRU_VMEM_LIMIT),
        name="rglru",
    )(xr, xr, xr, gr, conv_w.astype(F32), conv_b.reshape(1, W).astype(F32), w_a.astype(F32), w_i.astype(F32),
      b_a.astype(F32), b_i.astype(F32), lam.astype(F32), out_gain.reshape(1, W).astype(F32))


RT_TM = 1024
RT_PARTS = 4
RT_COLS = LANES
RT_ROWS = 48
RINFO = SUBLANES


def _split_bf16(x):
    hi = x.astype(BF16)
    lo = (x - hi.astype(F32)).astype(BF16)
    return hi, lo


def _route_kernel(an_ref, ln_ref, x_ref, wo_ref, g2_ref, wr_ref, br_ref,
                  x1_ref, h2_ref, gt_ref, ei_ref, cnt_ref, wob_ref, wrb_ref, tri_ref, runc_ref):
    @pl.when(pl.program_id(0) == 0)
    def _():
        wob_ref[...] = wo_ref[...].astype(BF16)
        hi, lo = _split_bf16(wr_ref[...])
        wrb_ref[:RT_ROWS, :] = hi
        wrb_ref[RT_ROWS:, :] = lo
        r = lax.broadcasted_iota(jnp.int32, (LANES, LANES), 0)
        cidx = lax.broadcasted_iota(jnp.int32, (LANES, LANES), 1)
        tri_ref[...] = (r < cidx).astype(BF16)
        runc_ref[...] = jnp.zeros_like(runc_ref)

    nt_dims = (((1,), (1,)), ((), ()))
    part = RT_TM // RT_PARTS
    x1s = []
    for r in range(RT_PARTS):
        rows = slice(r * part, (r + 1) * part)
        x1 = (x_ref[rows, :]
              + jnp.dot(an_ref[rows, :], wob_ref[:ATTN_WIDTH, :], preferred_element_type=F32)
              + jnp.dot(ln_ref[rows, :], wob_ref[ATTN_WIDTH:, :], preferred_element_type=F32))
        x1_ref[rows, :] = x1
        x1s.append(x1)
    splits = []
    for r, x1 in enumerate(x1s):
        h2 = _rms(x1, g2_ref[...])
        hi = h2.astype(BF16)
        hi_f = hi.astype(F32)
        h2_ref[r * part:(r + 1) * part, :] = _pack_rounded(hi_f)
        splits.append((hi, (h2 - hi_f).astype(BF16)))
    logits = []
    for hi, lo in splits:
        t1 = lax.dot_general(wrb_ref[...], hi, nt_dims, preferred_element_type=F32)
        t2 = lax.dot_general(wrb_ref[:RT_ROWS, :], lo, nt_dims, preferred_element_type=F32)
        logits.append(t1[:RT_ROWS] + t1[RT_ROWS:] + t2)
    sub = lax.broadcasted_iota(jnp.int32, (SUBLANES, part), 0)
    first_min = lambda hit: jnp.min(jnp.where(hit, sub, SUBLANES), axis=0, keepdims=True)
    is_g = sub < N_GROUPS

    def top2(logit):
        gl = jnp.where(is_g, logit[:SUBLANES], -jnp.inf)
        gm = jnp.max(gl, axis=0, keepdims=True)
        gidx = first_min(gl == gm)
        g_p = 1.0 / jnp.sum(jnp.where(is_g, jnp.exp(logit[:SUBLANES] - gm), 0.0), axis=0, keepdims=True)
        el = logit[SUBLANES:2 * SUBLANES]
        for g in range(1, N_GROUPS):
            el = jnp.where(gidx == g, logit[(g + 1) * SUBLANES:(g + 2) * SUBLANES], el)
        m1 = jnp.max(el, axis=0, keepdims=True)
        i1 = first_min(el == m1)
        el2 = jnp.where(sub == i1, -jnp.inf, el)
        m2 = jnp.max(el2, axis=0, keepdims=True)
        i2 = first_min(el2 == m2)
        t = jnp.exp(m2 - m1)
        return (gidx * EXPERTS_PER_GROUP + i1, gidx * EXPERTS_PER_GROUP + i2,
                g_p / (1.0 + t), g_p * t / (1.0 + t))

    picks = [top2(logit + br_ref[...]) for logit in logits]

    erow = lax.broadcasted_iota(jnp.int32, (N_EXPERTS, part), 0)
    tri = tri_ref[...]
    count = runc_ref[...]
    ranks = []
    for e1, e2, _, _ in picks:
        oh1 = erow == e1
        oh2 = erow == e2
        oh = (oh1 | oh2).astype(F32)
        ohb = oh.astype(BF16)
        cums = []
        for g in range(part // LANES):
            grp = slice(g * LANES, (g + 1) * LANES)
            cums.append(jnp.dot(ohb[:, grp], tri, preferred_element_type=F32) + count)
            count = count + jnp.sum(oh[:, grp], axis=1, keepdims=True)
        cum = jnp.concatenate(cums, axis=1)
        ranks.append((jnp.sum(jnp.where(oh1, cum, 0.0), axis=0, keepdims=True),
                      jnp.sum(jnp.where(oh2, cum, 0.0), axis=0, keepdims=True)))
    runc_ref[...] = count
    square = jnp.concatenate([jnp.broadcast_to(count, (N_EXPERTS, RT_COLS)),
                              jnp.zeros((RT_COLS - N_EXPERTS, RT_COLS), F32)], axis=0)
    cnt_ref[...] = square.T[0:1, :].astype(jnp.int32)

    for r, ((e1, e2, gate1, gate2), (rank1, rank2)) in enumerate(zip(picks, ranks)):
        ei = jnp.zeros((RINFO, part), jnp.int32)
        for k, v in enumerate([e1, e2, rank1.astype(jnp.int32), rank2.astype(jnp.int32)]):
            ei = jnp.where(sub == k, v, ei)
        ei_ref[0, :, r * part:(r + 1) * part] = ei
        gt_ref[r * part:(r + 1) * part, :] = jnp.where(sub == 0, gate1, jnp.where(sub == 1, gate2, 0.0)).T


def _out_route(attn_n, lru_n, x2, w_out, ln2, w_group, b_group, w_er, b_er):
    T = x2.shape[0]
    pad_g = SUBLANES - N_GROUPS
    wr = jnp.concatenate([jnp.pad(w_group.T, ((0, pad_g), (0, 0))),
                          jnp.transpose(w_er, (0, 2, 1)).reshape(N_EXPERTS, D_MODEL)], axis=0)
    wr = jnp.pad(wr, ((0, RT_ROWS - wr.shape[0]), (0, 0))).astype(F32)
    br = jnp.concatenate([jnp.pad(b_group, (0, pad_g)), b_er.reshape(-1)])
    br = jnp.pad(br, (0, RT_ROWS - br.shape[0])).reshape(RT_ROWS, 1).astype(F32)
    row = lambda w: pl.BlockSpec((RT_TM, w), lambda i: (i, 0))
    const = lambda shape: pl.BlockSpec(shape, lambda i: (0, 0))
    return pl.pallas_call(
        _route_kernel,
        grid=(T // RT_TM,),
        in_specs=[row(ATTN_WIDTH), row(LRU_WIDTH), row(D_MODEL), const((D_MODEL, D_MODEL)), const((1, D_MODEL)),
                  const((RT_ROWS, D_MODEL)), const((RT_ROWS, 1))],
        out_specs=[row(D_MODEL), row(PACKED), row(RINFO),
                   pl.BlockSpec((1, RINFO, RT_TM), lambda i: (i, 0, 0)), const((1, RT_COLS))],
        out_shape=[jax.ShapeDtypeStruct((T, D_MODEL), F32),
                   jax.ShapeDtypeStruct((T, PACKED), U32),
                   jax.ShapeDtypeStruct((T, RINFO), F32),
                   jax.ShapeDtypeStruct((T // RT_TM, RINFO, RT_TM), jnp.int32),
                   jax.ShapeDtypeStruct((1, RT_COLS), jnp.int32)],
        scratch_shapes=[pltpu.VMEM((D_MODEL, D_MODEL), BF16),
                        pltpu.VMEM((2 * RT_ROWS, D_MODEL), BF16),
                        pltpu.VMEM((LANES, LANES), BF16),
                        pltpu.VMEM((N_EXPERTS, 1), F32)],
        compiler_params=_cparams(1),
        name="out_route",
    )(attn_n, lru_n, x2, w_out, ln2.reshape(1, D_MODEL).astype(F32), wr, br)


def _moe_cap(T):
    A = T * TOP_K
    return ((A + MOE_BLOCK - 1) // MOE_BLOCK) * MOE_BLOCK + N_EXPERTS * MOE_BLOCK


PAD_BITS = tuple(1 << b for b in reversed(range(3, MOE_BLOCK.bit_length() - 1)))


def _layout_kernel(cnt_ref, ei_ref, dest_ref, pstart, be_ref, nu_ref, ge_ref):
    n_blocks = be_ref.shape[0]

    def lay(e, carry):
        start, blk, grp = carry
        pstart[e] = start
        nb = (cnt_ref[0, e] + MOE_BLOCK - 1) // MOE_BLOCK
        ge_ref[grp] = e

        def fill(k, c):
            be_ref[blk + k] = e
            return c
        lax.fori_loop(0, nb, fill, 0)
        return start + nb * MOE_BLOCK, blk + nb, grp + (nb > 0).astype(jnp.int32)
    _, used, groups = lax.fori_loop(0, N_EXPERTS, lay, (jnp.int32(0), jnp.int32(0), jnp.int32(0)))
    nu_ref[0] = used

    def tail(k, c):
        be_ref[k] = N_EXPERTS - 1
        return c
    lax.fori_loop(used, n_blocks, tail, 0)

    def no_group(k, c):
        ge_ref[k] = -1
        return c
    lax.fori_loop(groups, ge_ref.shape[0], no_group, 0)

    expert = ei_ref[:, 0:TOP_K, :]
    dest = ei_ref[:, TOP_K:2 * TOP_K, :]
    for e in range(N_EXPERTS):
        dest = dest + jnp.where(expert == e, pstart[e], 0)
    dest_ref[...] = dest


def _layout(ei, cnt, n_blocks):
    nt = ei.shape[0]
    smem = pl.BlockSpec(memory_space=pltpu.SMEM)
    vmem = pl.BlockSpec(memory_space=pltpu.VMEM)
    return pl.pallas_call(
        _layout_kernel,
        in_specs=[smem, vmem],
        out_specs=[vmem, smem, smem, smem, smem],
        out_shape=[jax.ShapeDtypeStruct((nt, TOP_K, RT_TM), jnp.int32),
                   jax.ShapeDtypeStruct((N_EXPERTS,), jnp.int32),
                   jax.ShapeDtypeStruct((n_blocks,), jnp.int32),
                   jax.ShapeDtypeStruct((1,), jnp.int32),
                   jax.ShapeDtypeStruct((N_EXPERTS + W_AHEAD,), jnp.int32)],
        name="layout",
    )(cnt, ei)


SC_CHUNK = 64
SC_BUFS = 3
SC_LEAD = SC_BUFS - 1


def _sc_workers():
    info = plsc.get_sparse_core_info()
    return info.num_cores, info.num_subcores


def _sc_ring(n_chunks, read, write):
    for c in range(min(SC_LEAD, n_chunks)):
        for cp in read(c):
            cp.start()
    reclaimed = set()
    for c in range(n_chunks):
        for cp in read(c):
            cp.wait()
        for cp in write(c):
            cp.start()
        nxt = c + SC_LEAD
        if nxt < n_chunks:
            if nxt - SC_BUFS >= 0:
                for cp in write(nxt - SC_BUFS):
                    cp.wait()
                reclaimed.add(nxt - SC_BUFS)
            for cp in read(nxt):
                cp.start()
    for c in range(n_chunks):
        if c not in reclaimed:
            for cp in write(c):
                cp.wait()


def _sc_dispatch(h2p, dest, cap):
    T = h2p.shape[0]
    nc, ns = _sc_workers()
    per_w = T // (nc * ns)
    n_ch = per_w // SC_CHUNK
    nt, _, tm = dest.shape
    assert nt * tm == T and tm % per_w == 0 and per_w % SC_CHUNK == 0
    idx = dest.reshape(nt, TOP_K, tm // per_w, per_w).transpose(0, 2, 1, 3).reshape(nc * ns, TOP_K * n_ch, SC_CHUNK)
    mesh = plsc.VectorSubcoreMesh(core_axis_name="c", subcore_axis_name="s")

    @functools.partial(
        pl.kernel, mesh=mesh,
        out_type=jax.ShapeDtypeStruct((cap, PACKED), U32),
        scratch_types=[pltpu.VMEM((TOP_K * n_ch, SC_CHUNK), jnp.int32),
                       pltpu.VMEM((SC_BUFS, SC_CHUNK, PACKED), U32),
                       pltpu.SemaphoreType.DMA((SC_BUFS,)),
                       pltpu.SemaphoreType.DMA((SC_BUFS,))])
    def scatter(src_hbm, idx_hbm, out_hbm, idx_v, rows_v, rsem, wsem):
        wid = lax.axis_index("s") * nc + lax.axis_index("c")
        base = pl.multiple_of(wid * per_w, per_w)
        pltpu.sync_copy(idx_hbm.at[wid], idx_v)

        def read(c):
            b = c % SC_BUFS
            return [pltpu.make_async_copy(src_hbm.at[pl.ds(base + c * SC_CHUNK, SC_CHUNK)], rows_v.at[b], rsem.at[b])]

        def write(c):
            b = c % SC_BUFS
            return [pltpu.make_async_copy(rows_v.at[b], out_hbm.at[idx_v.at[k * n_ch + c]], wsem.at[b])
                    for k in range(TOP_K)]
        _sc_ring(n_ch, read, write)

    return scatter(h2p, idx)


def _sc_gather(yb, dest):
    nt, _, tm = dest.shape
    nc, ns = _sc_workers()
    n_rows = nt * TOP_K * tm
    per_w = n_rows // (nc * ns)
    n_ch = per_w // SC_CHUNK
    assert per_w * nc * ns == n_rows and per_w % SC_CHUNK == 0
    mesh = plsc.VectorSubcoreMesh(core_axis_name="c", subcore_axis_name="s")

    @functools.partial(
        pl.kernel, mesh=mesh,
        out_type=jax.ShapeDtypeStruct((n_rows, PACKED), U32),
        scratch_types=[pltpu.VMEM((per_w,), jnp.int32),
                       pltpu.VMEM((SC_BUFS, SC_CHUNK, PACKED), U32),
                       pltpu.SemaphoreType.DMA((SC_BUFS,)),
                       pltpu.SemaphoreType.DMA((SC_BUFS,))])
    def gather(table_hbm, idx_hbm, out_hbm, idx_v, rows_v, gsem, wsem):
        wid = lax.axis_index("s") * nc + lax.axis_index("c")
        base = pl.multiple_of(wid * per_w, per_w)
        pltpu.sync_copy(idx_hbm.at[pl.ds(base, per_w)], idx_v)

        def read(c):
            b = c % SC_BUFS
            return [pltpu.make_async_copy(table_hbm.at[idx_v.at[pl.ds(c * SC_CHUNK, SC_CHUNK)]], rows_v.at[b], gsem.at[b])]

        def write(c):
            b = c % SC_BUFS
            return [pltpu.make_async_copy(rows_v.at[b], out_hbm.at[pl.ds(base + c * SC_CHUNK, SC_CHUNK)], wsem.at[b])]
        _sc_ring(n_ch, read, write)

    return gather(yb, dest.reshape(n_rows)).reshape(nt, TOP_K, tm, PACKED)


def _padfill_kernel(cnt_ref, pstart, xs_in, xs_ref, zeros, zsem):
    del xs_in

    def pad_copies(fn):
        for e in range(N_EXPERTS):
            cnt = cnt_ref[0, e]
            head = (-cnt) & (SUBLANES - 1)
            rest = ((-cnt) & (MOE_BLOCK - 1)) - head
            off = pstart[e] + cnt
            for k in range(SUBLANES - 1):
                @pl.when(k < head)
                def _(off=off, k=k):
                    fn(pltpu.make_async_copy(zeros.at[pl.ds(0, 1), :], xs_ref.at[pl.ds(off + k, 1), :], zsem))
            off = off + head
            for bit in PAD_BITS:
                @pl.when((rest & bit) != 0)
                def _(off=off, bit=bit):
                    fn(pltpu.make_async_copy(zeros.at[pl.ds(0, bit), :],
                                             xs_ref.at[pl.ds(pl.multiple_of(off, SUBLANES), bit), :], zsem))
                off = off + (rest & bit)

    zeros[...] = jnp.zeros_like(zeros)
    pad_copies(lambda cp: cp.start())
    pad_copies(lambda cp: cp.wait())


def _padfill(xs, pstart, cnt):
    smem = pl.BlockSpec(memory_space=pltpu.SMEM)
    hbm = pl.BlockSpec(memory_space=pl.ANY)
    return pl.pallas_call(
        _padfill_kernel,
        in_specs=[smem, smem, hbm],
        out_specs=hbm,
        out_shape=jax.ShapeDtypeStruct(xs.shape, xs.dtype),
        input_output_aliases={2: 0},
        scratch_shapes=[pltpu.VMEM((MOE_BLOCK // 2, PACKED), U32), pltpu.SemaphoreType.DMA(())],
        name="padfill",
    )(cnt, pstart, xs)


W_SLOTS = 3
W_AHEAD = W_SLOTS - 1
EXPERT_GROUP = 16
EXPERT_RUNS = (1, 2, 4, 8)


def _expert_kernel(be_ref, nu_ref, ge_ref, x_ref, wg_hbm, wu_hbm, wd_hbm, o_ref,
                   wgf, wuf, wdf, grp_ref, sems):
    step = pl.program_id(0)

    def weight_copies(e, slot):
        return (pltpu.make_async_copy(wg_hbm.at[e], wgf.at[slot], sems.at[slot, 0]),
                pltpu.make_async_copy(wu_hbm.at[e], wuf.at[slot], sems.at[slot, 1]),
                pltpu.make_async_copy(wd_hbm.at[e], wdf.at[slot], sems.at[slot, 2]))

    @pl.when(step == 0)
    def _():
        grp_ref[0] = 0
        for a in range(W_AHEAD):
            @pl.when(ge_ref[a] >= 0)
            def _(a=a):
                for cp in weight_copies(ge_ref[a], a):
                    cp.start()

    n_blocks = be_ref.shape[0]
    n_used = nu_ref[0]

    def swiglu(s, n, slot):
        rows = pl.ds(pl.multiple_of(s * MOE_BLOCK, MOE_BLOCK), n * MOE_BLOCK)
        lo, hi = _unpack_rows(x_ref[rows, :])
        lo = lo.astype(BF16)
        hi = hi.astype(BF16)
        g = (jnp.dot(lo, wgf[slot, :PACKED, :], preferred_element_type=F32)
             + jnp.dot(hi, wgf[slot, PACKED:, :], preferred_element_type=F32))
        u = (jnp.dot(lo, wuf[slot, :PACKED, :], preferred_element_type=F32)
             + jnp.dot(hi, wuf[slot, PACKED:, :], preferred_element_type=F32))
        h = (g * _sigmoid(g) * u).astype(BF16)
        o_ref[rows, :] = _pack_rows(jnp.dot(h, wdf[slot], preferred_element_type=F32))

    def run(s):
        j = step * EXPERT_GROUP + s
        e = be_ref[j]
        first = jnp.logical_or(j == 0, e != be_ref[jnp.maximum(j - 1, 0)])

        @pl.when(first)
        def _():
            grp = grp_ref[0]
            slot = grp % W_SLOTS
            for cp in weight_copies(e, slot):
                cp.wait()
            nxt = ge_ref[grp + W_AHEAD]

            @pl.when(nxt >= 0)
            def _():
                for cp in weight_copies(nxt, (grp + W_AHEAD) % W_SLOTS):
                    cp.start()
            grp_ref[0] = grp + 1

        def same(k):
            return (s + k < EXPERT_GROUP) & (j + k < n_used) & (be_ref[jnp.minimum(j + k, n_blocks - 1)] == e)
        take = jnp.int32(1)
        for n in EXPERT_RUNS[1:]:
            ok = same(n - 1)
            for k in range(1, n - 1):
                ok = ok & same(k)
            take = jnp.where(ok, n, take)
        slot = (grp_ref[0] + W_SLOTS - 1) % W_SLOTS
        for n in EXPERT_RUNS:
            @pl.when(take == n)
            def _(n=n):
                swiglu(s, n, slot)
        return s + take

    lax.while_loop(lambda s: (s < EXPERT_GROUP) & (step * EXPERT_GROUP + s < n_used), run, jnp.int32(0))


def _experts(xs, block_expert, n_used, group_expert, w_gate, w_up, w_down):
    cap = xs.shape[0]
    n_blocks = cap // MOE_BLOCK
    assert n_blocks % EXPERT_GROUP == 0
    rows = EXPERT_GROUP * MOE_BLOCK
    last = lambda j, be, nu, ge: jnp.minimum(j, (nu[0] - 1) // EXPERT_GROUP)
    hbm = pl.BlockSpec(memory_space=pl.ANY)
    gs = pltpu.PrefetchScalarGridSpec(
        num_scalar_prefetch=3,
        grid=(n_blocks // EXPERT_GROUP,),
        in_specs=[pl.BlockSpec((rows, PACKED), lambda j, be, nu, ge: (last(j, be, nu, ge), 0)), hbm, hbm, hbm],
        out_specs=pl.BlockSpec((rows, PACKED), lambda j, be, nu, ge: (last(j, be, nu, ge), 0)),
        scratch_shapes=[pltpu.VMEM((W_SLOTS, D_MODEL, D_EXPERT), F32),
                        pltpu.VMEM((W_SLOTS, D_MODEL, D_EXPERT), F32),
                        pltpu.VMEM((W_SLOTS, D_EXPERT, D_MODEL), F32),
                        pltpu.SMEM((1,), jnp.int32),
                        pltpu.SemaphoreType.DMA((W_SLOTS, 3))],
    )
    return pl.pallas_call(
        _expert_kernel,
        grid_spec=gs,
        out_shape=jax.ShapeDtypeStruct((cap, PACKED), U32),
        compiler_params=_cparams(1),
        name="experts",
    )(block_expert, n_used, group_expert, xs, w_gate, w_up, w_down)


CB_TM = RT_TM


def _combine_kernel(x1_ref, gt_ref, y2_ref, o_ref):
    g = gt_ref[...]
    lo1, hi1 = _unpack_rows(y2_ref[0, 0])
    lo2, hi2 = _unpack_rows(y2_ref[0, 1])
    o_ref[:, :PACKED] = x1_ref[:, :PACKED] + g[:, 0:1] * lo1 + g[:, 1:2] * lo2
    o_ref[:, PACKED:] = x1_ref[:, PACKED:] + g[:, 0:1] * hi1 + g[:, 1:2] * hi2


def _combine(x1, gates, y2):
    T = x1.shape[0]
    nt = T // CB_TM
    return pl.pallas_call(
        _combine_kernel,
        grid=(nt,),
        in_specs=[pl.BlockSpec((CB_TM, D_MODEL), lambda i: (i, 0)),
                  pl.BlockSpec((CB_TM, RINFO), lambda i: (i, 0)),
                  pl.BlockSpec((1, TOP_K, CB_TM, PACKED), lambda i: (i, 0, 0, 0))],
        out_specs=pl.BlockSpec((CB_TM, D_MODEL), lambda i: (i, 0)),
        out_shape=jax.ShapeDtypeStruct((T, D_MODEL), F32),
        compiler_params=_cparams(1),
        name="combine",
    )(x1, gates, y2)


def _layer(x, rel_bias, ln1, w_in, q_norm, k_norm, attn_sink, conv_w, conv_b, lru_wa, lru_ba, lru_wi, lru_bi,
           lru_lambda, out_norm_attn, out_norm_lru, w_out, ln2, w_group, b_group, w_er, b_er, w_gate, w_up, w_down):
    B, S, D = x.shape
    T = B * S
    x2 = x.reshape(T, D)
    q, kv, xr, gr = _in_proj(x2, ln1, w_in, q_norm, k_norm)
    attn_n = _attention(q.reshape(B, S, ATTN_WIDTH), kv.reshape(B, S, 2 * KV_WIDTH), rel_bias, attn_sink,
                        out_norm_attn)
    lru_n = _rglru(xr.reshape(B, S, LRU_WIDTH), gr.reshape(B, S, LRU_WIDTH), conv_w, conv_b,
                   lru_wa, lru_ba, lru_wi, lru_bi, lru_lambda, out_norm_lru)
    x1, h2, gates, ei, cnt = _out_route(attn_n.reshape(T, ATTN_WIDTH), lru_n.reshape(T, LRU_WIDTH), x2, w_out, ln2,
                                        w_group, b_group, w_er, b_er)
    cap = _moe_cap(T)
    dest, pstart, block_expert, n_used, group_expert = _layout(ei, cnt, cap // MOE_BLOCK)
    xs = _padfill(_sc_dispatch(h2, dest, cap), pstart, cnt)
    yb = _experts(xs, block_expert, n_used, group_expert, w_gate, w_up, w_down)
    out = _combine(x1, gates, _sc_gather(yb, dest))
    return out.reshape(B, S, D)


def kernel(x, rel_bias, ln1, w_in, q_norm, k_norm, attn_sink, conv_w, conv_b, lru_wa, lru_ba, lru_wi, lru_bi,
           lru_lambda, out_norm_attn, out_norm_lru, w_out, ln2, w_group, b_group, w_expert_router, b_expert_router,
           w_gate, w_up, w_down):
    depth = ln1.shape[0]
    for l in range(depth):
        x = _layer(x, rel_bias, ln1[l], w_in[l], q_norm[l], k_norm[l], attn_sink[l], conv_w[l], conv_b[l],
                   lru_wa[l], lru_ba[l], lru_wi[l], lru_bi[l], lru_lambda[l], out_norm_attn[l], out_norm_lru[l],
                   w_out[l], ln2[l], w_group[l], b_group[l], w_expert_router[l], b_expert_router[l],
                   w_gate[l], w_up[l], w_down[l])
    return x
```

```python
import functools
import math

import jax
import jax.numpy as jnp
from jax import lax
from jax.experimental import pallas as pl
from jax.experimental.pallas import tpu as pltpu
from jax.experimental.pallas import tpu_sc as plsc

D_MODEL = 1024
N_HEADS = 8
N_KV_HEADS = 2
HEAD_DIM = 64
Q_PER_KV = N_HEADS // N_KV_HEADS
ATTN_WIDTH = N_HEADS * HEAD_DIM
KV_WIDTH = N_KV_HEADS * HEAD_DIM
WINDOW = 128
BLOCK = 128
NUM_BUCKETS = 32
MAX_DISTANCE = 128
LRU_WIDTH = D_MODEL - ATTN_WIDTH
LRU_BLOCKS = 8
LRU_BLOCK_DIM = LRU_WIDTH // LRU_BLOCKS
LRU_C = 8.0
CONV_W = 4
CONV_LEFT = 2
N_GROUPS = 4
EXPERTS_PER_GROUP = 8
N_EXPERTS = N_GROUPS * EXPERTS_PER_GROUP
TOP_K = 2
D_EXPERT = 512
MOE_BLOCK = 128
EPS = 1e-6
NEG_INF = -1e30

LANES = 128
SUBLANES = 8
VMEM_LIMIT = 56 * 1024 * 1024
LRU_VMEM_LIMIT = 62 * 1024 * 1024

F32 = jnp.float32
BF16 = jnp.bfloat16
LOG2E = math.log2(math.e)


def _cparams(n_axes, vmem=VMEM_LIMIT):
    return pltpu.CompilerParams(dimension_semantics=("arbitrary",) * n_axes, vmem_limit_bytes=vmem)


def _rms(x, gain):
    return x * lax.rsqrt(jnp.mean(x * x, axis=-1, keepdims=True) + EPS) * gain


U32 = jnp.uint32
HI_MASK = 0xFFFF0000
PACKED = D_MODEL // 2


def _pack_rows(x):
    return _pack_rounded(x.astype(BF16).astype(F32))


def _pack_rounded(xb):
    h = xb.shape[1] // 2
    lo = lax.bitcast_convert_type(xb[:, :h], U32) >> 16
    hi = lax.bitcast_convert_type(xb[:, h:], U32) & jnp.uint32(HI_MASK)
    return lo | hi


def _unpack_rows(p):
    lo = lax.bitcast_convert_type(p << 16, F32)
    hi = lax.bitcast_convert_type(p & jnp.uint32(HI_MASK), F32)
    return lo, hi


IN_TM = 1024


def _head_rms(x, n_heads, gain):
    head = lax.broadcasted_iota(jnp.int32, (1, n_heads * HEAD_DIM), 1) // HEAD_DIM
    x2 = x * x
    scale = jnp.zeros_like(x)
    for h in range(n_heads):
        ms = jnp.sum(jnp.where(head == h, x2, 0.0), axis=-1, keepdims=True) * (1.0 / HEAD_DIM)
        scale = jnp.where(head == h, lax.rsqrt(ms + EPS), scale)
    return x * scale * gain


def _in_proj_kernel(x_ref, g_ref, w_ref, qn_ref, kn_ref, q_ref, kv_ref, xr_ref, gr_ref, wb_ref, qg_ref, kg_ref):
    @pl.when(pl.program_id(0) == 0)
    def _():
        wb_ref[...] = w_ref[...].astype(BF16)
        qg_ref[...] = jnp.concatenate([qn_ref[...]] * N_HEADS, axis=1) * (HEAD_DIM ** -0.5 * LOG2E)
        kg_ref[...] = jnp.concatenate([kn_ref[...]] * N_KV_HEADS, axis=1)

    h = _rms(x_ref[...], g_ref[...]).astype(BF16)
    c_k = ATTN_WIDTH
    c_v = c_k + KV_WIDTH
    c_x = c_v + KV_WIDTH
    c_g = c_x + LRU_WIDTH
    q = jnp.dot(h, wb_ref[:, :c_k], preferred_element_type=F32)
    q_ref[...] = _head_rms(q, N_HEADS, qg_ref[...]).astype(BF16)
    k = jnp.dot(h, wb_ref[:, c_k:c_v], preferred_element_type=F32)
    kv_ref[:, :KV_WIDTH] = _head_rms(k, N_KV_HEADS, kg_ref[...]).astype(BF16)
    kv_ref[:, KV_WIDTH:] = jnp.dot(h, wb_ref[:, c_v:c_x], preferred_element_type=F32).astype(BF16)
    xr_ref[...] = jnp.dot(h, wb_ref[:, c_x:c_g], preferred_element_type=F32)
    gr_ref[...] = jnp.dot(h, wb_ref[:, c_g:], preferred_element_type=F32)


def _in_proj(x2, ln1, w_in, q_gain, k_gain):
    T = x2.shape[0]
    n_in = w_in.shape[1]
    row = lambda w: pl.BlockSpec((IN_TM, w), lambda i: (i, 0))
    return pl.pallas_call(
        _in_proj_kernel,
        grid=(T // IN_TM,),
        in_specs=[row(D_MODEL),
                  pl.BlockSpec((1, D_MODEL), lambda i: (0, 0)),
                  pl.BlockSpec((D_MODEL, n_in), lambda i: (0, 0)),
                  pl.BlockSpec((1, HEAD_DIM), lambda i: (0, 0)),
                  pl.BlockSpec((1, HEAD_DIM), lambda i: (0, 0))],
        out_specs=[row(ATTN_WIDTH), row(2 * KV_WIDTH), row(LRU_WIDTH), row(LRU_WIDTH)],
        out_shape=[jax.ShapeDtypeStruct((T, ATTN_WIDTH), BF16),
                   jax.ShapeDtypeStruct((T, 2 * KV_WIDTH), BF16),
                   jax.ShapeDtypeStruct((T, LRU_WIDTH), F32),
                   jax.ShapeDtypeStruct((T, LRU_WIDTH), F32)],
        scratch_shapes=[pltpu.VMEM((D_MODEL, n_in), BF16),
                        pltpu.VMEM((1, ATTN_WIDTH), F32),
                        pltpu.VMEM((1, KV_WIDTH), F32)],
        compiler_params=_cparams(1),
        name="in_proj",
    )(x2, ln1.reshape(1, D_MODEL), w_in, q_gain.reshape(1, HEAD_DIM).astype(F32),
      k_gain.reshape(1, HEAD_DIM).astype(F32))


def _t5_bucket(rel):
    half = NUM_BUCKETS // 2
    max_exact = half // 2
    base = jnp.where(rel > 0, half, 0)
    n = jnp.abs(rel)
    nf = jnp.maximum(n, 1).astype(jnp.float32)
    large = max_exact + (jnp.log(nf / max_exact) / math.log(MAX_DISTANCE / max_exact)
                         * (half - max_exact)).astype(jnp.int32)
    large = jnp.minimum(large, half - 1)
    return base + jnp.where(n < max_exact, n, large)


HEAD_PAIRS = Q_PER_KV // 2
EDGE_VARIANTS = 3


def _fill_bias_table(rb_ref, bucket_ref, band_ref, o_ref):
    bucket = bucket_ref[...]
    band = band_ref[...] > 0
    col = lax.broadcasted_iota(jnp.int32, bucket.shape, 1)
    valid = (band & (col >= BLOCK), band, band & (col < 2 * BLOCK))
    for h in range(N_HEADS):
        acc = jnp.zeros(bucket.shape, F32)
        for b in range(NUM_BUCKETS):
            acc = jnp.where(bucket == b, rb_ref[b, h], acc)
        kv, g = divmod(h, Q_PER_KV)
        pair, parity = divmod(g, 2)
        for var in range(EDGE_VARIANTS):
            o_ref[var, kv, parity, pair * BLOCK:(pair + 1) * BLOCK, :] = jnp.where(valid[var], acc * LOG2E, NEG_INF)


def _attn_kernel(sink_ref, rb_ref, q_ref, kp_ref, kc_ref, kn_ref, bucket_ref, band_ref, og_ref, o_ref, bias_ref):
    n = pl.program_id(1)

    @pl.when((pl.program_id(0) == 0) & (n == 0))
    def _():
        _fill_bias_table(rb_ref, bucket_ref, band_ref, bias_ref)

    kv_all = jnp.concatenate([kp_ref[0], kc_ref[0], kn_ref[0]], axis=0)
    for qb in range(ATTN_QB):
        variant = 1
        if qb == 0:
            variant = jnp.where(n == 0, 0, 1)
        if qb == ATTN_QB - 1:
            variant = jnp.where(n == pl.num_programs(1) - 1, 2, variant)
        out = _attn_block(q_ref[0, qb * BLOCK:(qb + 1) * BLOCK, :], kv_all[qb * BLOCK:(qb + 3) * BLOCK, :],
                          lambda kv, parity: bias_ref[variant, kv, parity], sink_ref)
        o_ref[0, qb * BLOCK:(qb + 1) * BLOCK, :] = _rms(out, og_ref[...]).astype(o_ref.dtype)


def _attn_block(q, kvw, bias, sink_ref):
    low = lax.broadcasted_iota(jnp.int32, (1, LANES), 1) < HEAD_DIM
    swap = lambda slab: pltpu.roll(slab.astype(F32), HEAD_DIM, 1).astype(BF16)
    kslab, vslab = kvw[:, :KV_WIDTH], kvw[:, KV_WIDTH:]
    kslab_sw, vslab_sw = swap(kslab), swap(vslab)
    rowi = lax.broadcasted_iota(jnp.int32, (HEAD_PAIRS * BLOCK, 1), 0)
    combos = [(kv, parity) for kv in range(N_KV_HEADS) for parity in range(2)]
    scores, vzs, sinks = [], [], []
    for kv, parity in combos:
        ks, vs = (kslab, vslab) if (kv == 0) == (parity == 0) else (kslab_sw, vslab_sw)
        keep = low if parity == 0 else jnp.logical_not(low)
        kz = jnp.where(keep, ks, jnp.zeros_like(ks))
        vzs.append(jnp.where(keep, vs, jnp.zeros_like(vs)))
        base = kv * Q_PER_KV * HEAD_DIM
        qpair = jnp.concatenate([q[:, base + j * LANES:base + (j + 1) * LANES] for j in range(HEAD_PAIRS)], axis=0)
        s = lax.dot_general(qpair, kz, (((1,), (1,)), ((), ())), preferred_element_type=F32)
        scores.append(s + bias(kv, parity))
        sink = jnp.zeros((HEAD_PAIRS * BLOCK, 1), F32)
        for j in range(HEAD_PAIRS):
            sink = jnp.where(rowi // BLOCK == j, sink_ref[kv * Q_PER_KV + 2 * j + parity], sink)
        sinks.append(sink)
    probs, inv = [], []
    for s, sink in zip(scores, sinks):
        m = jnp.maximum(jnp.max(s, axis=-1, keepdims=True), sink)
        p = jnp.exp2(s - m)
        inv.append(1.0 / (jnp.sum(p, axis=-1, keepdims=True) + jnp.exp2(sink - m)))
        probs.append(p.astype(BF16))
    outs = [jnp.dot(p, vz, preferred_element_type=F32) * r for p, vz, r in zip(probs, vzs, inv)]
    cols = []
    for kv in range(N_KV_HEADS):
        acc = outs[2 * kv] + outs[2 * kv + 1]
        cols += [acc[j * BLOCK:(j + 1) * BLOCK, :] for j in range(HEAD_PAIRS)]
    return jnp.concatenate(cols, axis=1)


ATTN_QB = 8


def _attention(q, kv, rel_bias, sink, out_gain):
    B, S, _ = q.shape
    nb = S // BLOCK
    assert ATTN_QB >= 2 and nb % ATTN_QB == 0, "a step's first and last query blocks must be distinct"
    ns = nb // ATTN_QB
    rows = ATTN_QB * BLOCK
    qi = jnp.arange(BLOCK, dtype=jnp.int32)
    kj = jnp.arange(3 * BLOCK, dtype=jnp.int32)
    rel = kj[None, :] - BLOCK - qi[:, None]
    bucket = _t5_bucket(rel).astype(jnp.int32)
    band = (jnp.abs(rel) <= WINDOW).astype(jnp.int32)
    kvspec = lambda f: pl.BlockSpec((1, BLOCK, 2 * KV_WIDTH), f)
    smem = pl.BlockSpec(memory_space=pltpu.SMEM)
    geom = pl.BlockSpec((BLOCK, 3 * BLOCK), lambda b, n: (0, 0))
    return pl.pallas_call(
        _attn_kernel,
        grid=(B, ns),
        in_specs=[smem, smem,
                  pl.BlockSpec((1, rows, ATTN_WIDTH), lambda b, n: (b, n, 0)),
                  kvspec(lambda b, n: (b, jnp.maximum(n * ATTN_QB - 1, 0), 0)),
                  pl.BlockSpec((1, rows, 2 * KV_WIDTH), lambda b, n: (b, n, 0)),
                  kvspec(lambda b, n: (b, jnp.minimum((n + 1) * ATTN_QB, nb - 1), 0)),
                  geom, geom,
                  pl.BlockSpec((1, ATTN_WIDTH), lambda b, n: (0, 0))],
        out_specs=pl.BlockSpec((1, rows, ATTN_WIDTH), lambda b, n: (b, n, 0)),
        out_shape=jax.ShapeDtypeStruct((B, S, ATTN_WIDTH), BF16),
        scratch_shapes=[pltpu.VMEM((EDGE_VARIANTS, N_KV_HEADS, 2, HEAD_PAIRS * BLOCK, 3 * BLOCK), F32)],
        compiler_params=_cparams(2),
        name="attention",
    )(sink.astype(F32) * LOG2E, rel_bias.astype(F32), q, kv, kv, kv, bucket, band, out_gain.reshape(1, ATTN_WIDTH))


LRU_TC = 128
LRU_PITCH = LRU_TC + SUBLANES // 2
LRU_SLABS = LRU_WIDTH // LANES
LRU_UNROLL = 8
HALO = SUBLANES


def _softplus(x):
    return jnp.maximum(x, 0.0) + jnp.log(1.0 + jnp.exp(-jnp.abs(x)))


def _gelu_tanh(x):
    k = math.sqrt(2.0 / math.pi)
    hx = 0.5 * x
    return hx + hx * jnp.tanh(x * (k + (k * 0.044715) * (x * x)))


def _sigmoid(x):
    return 0.5 + 0.5 * jnp.tanh(0.5 * x)


def _rglru_kernel(xr_ref, xp_ref, xn_ref, gr_ref, cw_ref, cb_ref, wa_ref, wi_ref, ba_ref, bi_ref, lam_ref, og_ref,
                  o_ref, sx_ref, a_ref, u_ref, h_ref, carry_ref, hf_ref, xcs_ref, wg_ref, bg_ref, k_ref):
    p = pl.program_id(0)
    i = pl.program_id(1)
    nc = pl.num_programs(1)
    c = i + p * (nc - 1 - 2 * i)
    B = xr_ref.shape[0]
    TC = LRU_TC

    @pl.when(i == 0)
    def _():
        carry_ref[...] = jnp.zeros_like(carry_ref)
        wg_ref[...] = jnp.zeros_like(wg_ref)
        for sel, w_ref in enumerate((wa_ref, wi_ref)):
            for h in range(LRU_BLOCKS):
                lo = h * LRU_BLOCK_DIM
                wg_ref[lo:lo + LRU_BLOCK_DIM, sel * LRU_WIDTH + lo:sel * LRU_WIDTH + lo + LRU_BLOCK_DIM] = (
                    0.5 * w_ref[0, h]).astype(BF16)
        row = pl.ds(p, 1)
        bg_ref[:, :LRU_WIDTH] = 0.5 * ba_ref[row, :]
        bg_ref[:, LRU_WIDTH:] = 0.5 * bi_ref[row, :]
        k_ref[...] = (-0.5 * LRU_C * math.log2(math.e)) * _softplus(-lam_ref[row, :])

    def gates_and_scan(xc2, backward):
        g = jnp.dot(xc2.astype(BF16), wg_ref[...], preferred_element_type=F32) + bg_ref[...]
        ta = jnp.tanh(g[:, :LRU_WIDTH])
        ig = 0.5 + 0.5 * jnp.tanh(g[:, LRU_WIDTH:])
        a = jnp.exp2((1.0 + ta) * k_ref[...])
        z = 1.0 - a * a
        u = z * lax.rsqrt(jnp.maximum(z, 1e-30)) * ig * xc2
        for b in range(B):
            for s in range(LRU_SLABS):
                a_ref[s, b * LRU_PITCH:b * LRU_PITCH + TC, :] = a[b * TC:(b + 1) * TC, s * LANES:(s + 1) * LANES]
                u_ref[s, b * LRU_PITCH:b * LRU_PITCH + TC, :] = u[b * TC:(b + 1) * TC, s * LANES:(s + 1) * LANES]

        def trip(i, hs):
            t0 = pl.multiple_of((TC // LRU_UNROLL - 1 - i if backward else i) * LRU_UNROLL, LRU_UNROLL)
            for j in range(LRU_UNROLL):
                t = t0 + (LRU_UNROLL - 1 - j if backward else j)
                out = []
                for s in range(LRU_SLABS):
                    idx = pl.ds(t, B, stride=LRU_PITCH)
                    hn = a_ref[s, idx, :] * hs[s] + u_ref[s, idx, :]
                    h_ref[s, idx, :] = hn
                    out.append(hn)
                hs = tuple(out)
            return hs

        hs = lax.fori_loop(0, TC // LRU_UNROLL, trip, tuple(carry_ref[s] for s in range(LRU_SLABS)))
        for s in range(LRU_SLABS):
            carry_ref[s] = hs[s]

    @pl.when(p == 0)
    def _():
        sx_ref[:, HALO:HALO + TC, :] = xr_ref[...]
        sx_ref[:, 0:HALO, :] = jnp.where(c > 0, xp_ref[...], 0.0)
        sx_ref[:, HALO + TC:, :] = jnp.where(c < nc - 1, xn_ref[...], 0.0)
        xc = cb_ref[...][None]
        for j in range(CONV_W):
            off = HALO + j - CONV_LEFT
            xc = xc + cw_ref[j:j + 1, :][None] * sx_ref[:, off:off + TC, :]
        xc2 = xc.reshape(B * TC, LRU_WIDTH)
        xcs_ref[c] = xc2.astype(xcs_ref.dtype)
        gates_and_scan(xc2, backward=False)
        for b in range(B):
            for s in range(LRU_SLABS):
                hf_ref[c, s, b * TC:(b + 1) * TC, :] = h_ref[s, b * LRU_PITCH:b * LRU_PITCH + TC, :].astype(hf_ref.dtype)

    @pl.when(p == 1)
    def _():
        gates_and_scan(xcs_ref[c].astype(F32), backward=True)
        for b in range(B):
            hsum = jnp.concatenate(
                [h_ref[s, b * LRU_PITCH:b * LRU_PITCH + TC, :] + hf_ref[c, s, b * TC:(b + 1) * TC, :].astype(F32)
                 for s in range(LRU_SLABS)], axis=1)
            y = hsum * _gelu_tanh(gr_ref[b])
            o_ref[b] = _rms(y, og_ref[...]).astype(o_ref.dtype)


def _rglru(xr, gr, conv_w, conv_b, w_a, b_a, w_i, b_i, lam, out_gain):
    B, S, W = xr.shape
    nc = S // LRU_TC
    hb = LRU_TC // HALO
    fwd = lambda p, i: jnp.where(p == 0, i, nc - 1)
    bwd = lambda p, i: nc - 1 - p * i
    full2 = lambda shape: pl.BlockSpec(shape, lambda p, i: (0,) * len(shape))
    wblock = pl.BlockSpec((1, LRU_BLOCKS, LRU_BLOCK_DIM, LRU_BLOCK_DIM), lambda p, i: (p, 0, 0, 0))
    return pl.pallas_call(
        _rglru_kernel,
        grid=(2, nc),
        in_specs=[pl.BlockSpec((B, LRU_TC, W), lambda p, i: (0, fwd(p, i), 0)),
                  pl.BlockSpec((B, HALO, W), lambda p, i: (0, jnp.maximum(fwd(p, i) * hb - 1, 0), 0)),
                  pl.BlockSpec((B, HALO, W), lambda p, i: (0, jnp.minimum((fwd(p, i) + 1) * hb, S // HALO - 1), 0)),
                  pl.BlockSpec((B, LRU_TC, W), lambda p, i: (0, bwd(p, i), 0)),
                  full2((CONV_W, W)),
                  full2((1, W)),
                  wblock, wblock,
                  full2((2, W)), full2((2, W)), full2((2, W)),
                  full2((1, W))],
        out_specs=pl.BlockSpec((B, LRU_TC, W), lambda p, i: (0, bwd(p, i), 0)),
        out_shape=jax.ShapeDtypeStruct((B, S, W), BF16),
        scratch_shapes=[pltpu.VMEM((B, LRU_TC + 2 * HALO, W), F32),
                        pltpu.VMEM((LRU_SLABS, B * LRU_PITCH, LANES), F32),
                        pltpu.VMEM((LRU_SLABS, B * LRU_PITCH, LANES), F32),
                        pltpu.VMEM((LRU_SLABS, B * LRU_PITCH, LANES), F32),
                        pltpu.VMEM((LRU_SLABS, B, LANES), F32),
                        pltpu.VMEM((nc, LRU_SLABS, B * LRU_TC, LANES), BF16),
                        pltpu.VMEM((nc, B * LRU_TC, W), BF16),
                        pltpu.VMEM((W, 2 * W), BF16),
                        pltpu.VMEM((1, 2 * W), F32),
                        pltpu.VMEM((1, W), F32)],
        compiler_params=_cparams(2, LRU_VMEM_LIMIT),
        name="rglru",
    )(xr, xr, xr, gr, conv_w.astype(F32), conv_b.reshape(1, W).astype(F32), w_a.astype(F32), w_i.astype(F32),
      b_a.astype(F32), b_i.astype(F32), lam.astype(F32), out_gain.reshape(1, W).astype(F32))


RT_TM = 1024
RT_PARTS = 4
RT_COLS = LANES
RT_ROWS = 48
RINFO = SUBLANES


def _split_bf16(x):
    hi = x.astype(BF16)
    lo = (x - hi.astype(F32)).astype(BF16)
    return hi, lo


def _route_kernel(an_ref, ln_ref, x_ref, wo_ref, g2_ref, wr_ref, br_ref,
                  x1_ref, h2_ref, gt_ref, ei_ref, cnt_ref, wob_ref, wrb_ref, tri_ref, run_ref, runc_ref):
    @pl.when(pl.program_id(0) == 0)
    def _():
        wob_ref[...] = wo_ref[...].astype(BF16)
        hi, lo = _split_bf16(wr_ref[...])
        wrb_ref[:RT_ROWS, :] = hi
        wrb_ref[RT_ROWS:, :] = lo
        r = lax.broadcasted_iota(jnp.int32, (RT_TM, RT_TM), 0)
        cidx = lax.broadcasted_iota(jnp.int32, (RT_TM, RT_TM), 1)
        tri_ref[...] = (r < cidx).astype(BF16)
        run_ref[...] = jnp.zeros_like(run_ref)
        runc_ref[...] = jnp.zeros_like(runc_ref)

    nt_dims = (((1,), (1,)), ((), ()))
    part = RT_TM // RT_PARTS
    x1s = []
    for r in range(RT_PARTS):
        rows = slice(r * part, (r + 1) * part)
        x1 = (x_ref[rows, :]
              + jnp.dot(an_ref[rows, :], wob_ref[:ATTN_WIDTH, :], preferred_element_type=F32)
              + jnp.dot(ln_ref[rows, :], wob_ref[ATTN_WIDTH:, :], preferred_element_type=F32))
        x1_ref[rows, :] = x1
        x1s.append(x1)
    splits = []
    for r, x1 in enumerate(x1s):
        h2 = _rms(x1, g2_ref[...])
        hi = h2.astype(BF16)
        hi_f = hi.astype(F32)
        h2_ref[r * part:(r + 1) * part, :] = _pack_rounded(hi_f)
        splits.append((hi, (h2 - hi_f).astype(BF16)))
    logits = []
    for hi, lo in splits:
        t1 = lax.dot_general(wrb_ref[...], hi, nt_dims, preferred_element_type=F32)
        t2 = lax.dot_general(wrb_ref[:RT_ROWS, :], lo, nt_dims, preferred_element_type=F32)
        logits.append(t1[:RT_ROWS] + t1[RT_ROWS:] + t2)
    logit = jnp.concatenate(logits, axis=1) + br_ref[...]

    sub = lax.broadcasted_iota(jnp.int32, (SUBLANES, RT_TM), 0)
    first_min = lambda hit: jnp.min(jnp.where(hit, sub, SUBLANES), axis=0, keepdims=True)
    is_g = sub < N_GROUPS
    gl = jnp.where(is_g, logit[:SUBLANES], -jnp.inf)
    gm = jnp.max(gl, axis=0, keepdims=True)
    gidx = first_min(gl == gm)
    g_p = 1.0 / jnp.sum(jnp.where(is_g, jnp.exp(logit[:SUBLANES] - gm), 0.0), axis=0, keepdims=True)
    el = logit[SUBLANES:2 * SUBLANES]
    for g in range(1, N_GROUPS):
        el = jnp.where(gidx == g, logit[(g + 1) * SUBLANES:(g + 2) * SUBLANES], el)
    m1 = jnp.max(el, axis=0, keepdims=True)
    i1 = first_min(el == m1)
    el2 = jnp.where(sub == i1, -jnp.inf, el)
    m2 = jnp.max(el2, axis=0, keepdims=True)
    i2 = first_min(el2 == m2)
    t = jnp.exp(m2 - m1)
    gate1 = g_p / (1.0 + t)
    gate2 = g_p * t / (1.0 + t)
    e1 = gidx * EXPERTS_PER_GROUP + i1
    e2 = gidx * EXPERTS_PER_GROUP + i2

    erow = lax.broadcasted_iota(jnp.int32, (N_EXPERTS, RT_TM), 0)
    oh1 = erow == e1
    oh2 = erow == e2
    oh = (oh1 | oh2).astype(F32)
    ohb = oh.astype(BF16)
    cum = jnp.dot(ohb, tri_ref[...], preferred_element_type=F32) + runc_ref[...]
    rank1 = jnp.sum(jnp.where(oh1, cum, 0.0), axis=0, keepdims=True)
    rank2 = jnp.sum(jnp.where(oh2, cum, 0.0), axis=0, keepdims=True)
    runc_ref[...] = runc_ref[...] + jnp.sum(oh, axis=1, keepdims=True)
    tile_cnt = lax.dot_general(jnp.ones((SUBLANES, RT_TM), BF16), ohb, nt_dims, preferred_element_type=F32)
    run_ref[:, :N_EXPERTS] = run_ref[:, :N_EXPERTS] + tile_cnt[0:1]
    cnt_ref[...] = run_ref[...].astype(jnp.int32)

    rows = [e1, e2, rank1.astype(jnp.int32), rank2.astype(jnp.int32)]
    ei = jnp.zeros((RINFO, RT_TM), jnp.int32)
    for k, v in enumerate(rows):
        ei = jnp.where(sub == k, v, ei)
    ei_ref[0] = ei
    gt_ref[...] = jnp.where(sub == 0, gate1, jnp.where(sub == 1, gate2, 0.0)).T


def _out_route(attn_n, lru_n, x2, w_out, ln2, w_group, b_group, w_er, b_er):
    T = x2.shape[0]
    pad_g = SUBLANES - N_GROUPS
    wr = jnp.concatenate([jnp.pad(w_group.T, ((0, pad_g), (0, 0))),
                          jnp.transpose(w_er, (0, 2, 1)).reshape(N_EXPERTS, D_MODEL)], axis=0)
    wr = jnp.pad(wr, ((0, RT_ROWS - wr.shape[0]), (0, 0))).astype(F32)
    br = jnp.concatenate([jnp.pad(b_group, (0, pad_g)), b_er.reshape(-1)])
    br = jnp.pad(br, (0, RT_ROWS - br.shape[0])).reshape(RT_ROWS, 1).astype(F32)
    row = lambda w: pl.BlockSpec((RT_TM, w), lambda i: (i, 0))
    const = lambda shape: pl.BlockSpec(shape, lambda i: (0, 0))
    return pl.pallas_call(
        _route_kernel,
        grid=(T // RT_TM,),
        in_specs=[row(ATTN_WIDTH), row(LRU_WIDTH), row(D_MODEL), const((D_MODEL, D_MODEL)), const((1, D_MODEL)),
                  const((RT_ROWS, D_MODEL)), const((RT_ROWS, 1))],
        out_specs=[row(D_MODEL), row(PACKED), row(RINFO),
                   pl.BlockSpec((1, RINFO, RT_TM), lambda i: (i, 0, 0)), const((1, RT_COLS))],
        out_shape=[jax.ShapeDtypeStruct((T, D_MODEL), F32),
                   jax.ShapeDtypeStruct((T, PACKED), U32),
                   jax.ShapeDtypeStruct((T, RINFO), F32),
                   jax.ShapeDtypeStruct((T // RT_TM, RINFO, RT_TM), jnp.int32),
                   jax.ShapeDtypeStruct((1, RT_COLS), jnp.int32)],
        scratch_shapes=[pltpu.VMEM((D_MODEL, D_MODEL), BF16),
                        pltpu.VMEM((2 * RT_ROWS, D_MODEL), BF16),
                        pltpu.VMEM((RT_TM, RT_TM), BF16),
                        pltpu.VMEM((1, RT_COLS), F32),
                        pltpu.VMEM((N_EXPERTS, 1), F32)],
        compiler_params=_cparams(1),
        name="out_route",
    )(attn_n, lru_n, x2, w_out, ln2.reshape(1, D_MODEL).astype(F32), wr, br)


def _moe_cap(T):
    A = T * TOP_K
    return ((A + MOE_BLOCK - 1) // MOE_BLOCK) * MOE_BLOCK + N_EXPERTS * MOE_BLOCK


PAD_BITS = tuple(1 << b for b in reversed(range(3, MOE_BLOCK.bit_length() - 1)))


def _layout_kernel(cnt_ref, ei_ref, dest_ref, pstart, be_ref, nu_ref, ge_ref):
    n_blocks = be_ref.shape[0]

    def lay(e, carry):
        start, blk, grp = carry
        pstart[e] = start
        nb = (cnt_ref[0, e] + MOE_BLOCK - 1) // MOE_BLOCK
        ge_ref[grp] = e

        def fill(k, c):
            be_ref[blk + k] = e
            return c
        lax.fori_loop(0, nb, fill, 0)
        return start + nb * MOE_BLOCK, blk + nb, grp + (nb > 0).astype(jnp.int32)
    _, used, groups = lax.fori_loop(0, N_EXPERTS, lay, (jnp.int32(0), jnp.int32(0), jnp.int32(0)))
    nu_ref[0] = used

    def tail(k, c):
        be_ref[k] = N_EXPERTS - 1
        return c
    lax.fori_loop(used, n_blocks, tail, 0)

    def no_group(k, c):
        ge_ref[k] = -1
        return c
    lax.fori_loop(groups, ge_ref.shape[0], no_group, 0)

    expert = ei_ref[:, 0:TOP_K, :]
    dest = ei_ref[:, TOP_K:2 * TOP_K, :]
    for e in range(N_EXPERTS):
        dest = dest + jnp.where(expert == e, pstart[e], 0)
    dest_ref[...] = dest


def _layout(ei, cnt, n_blocks):
    nt = ei.shape[0]
    smem = pl.BlockSpec(memory_space=pltpu.SMEM)
    vmem = pl.BlockSpec(memory_space=pltpu.VMEM)
    return pl.pallas_call(
        _layout_kernel,
        in_specs=[smem, vmem],
        out_specs=[vmem, smem, smem, smem, smem],
        out_shape=[jax.ShapeDtypeStruct((nt, TOP_K, RT_TM), jnp.int32),
                   jax.ShapeDtypeStruct((N_EXPERTS,), jnp.int32),
                   jax.ShapeDtypeStruct((n_blocks,), jnp.int32),
                   jax.ShapeDtypeStruct((1,), jnp.int32),
                   jax.ShapeDtypeStruct((N_EXPERTS + W_AHEAD,), jnp.int32)],
        name="layout",
    )(cnt, ei)


SC_CHUNK = 32
SC_BUFS = 6
SC_LEAD = SC_BUFS - 1


def _sc_workers():
    info = plsc.get_sparse_core_info()
    return info.num_cores, info.num_subcores


def _sc_ring(n_chunks, read, write):
    for c in range(min(SC_LEAD, n_chunks)):
        for cp in read(c):
            cp.start()
    reclaimed = set()
    for c in range(n_chunks):
        for cp in read(c):
            cp.wait()
        for cp in write(c):
            cp.start()
        nxt = c + SC_LEAD
        if nxt < n_chunks:
            if nxt - SC_BUFS >= 0:
                for cp in write(nxt - SC_BUFS):
                    cp.wait()
                reclaimed.add(nxt - SC_BUFS)
            for cp in read(nxt):
                cp.start()
    for c in range(n_chunks):
        if c not in reclaimed:
            for cp in write(c):
                cp.wait()


def _sc_dispatch(h2p, dest, cap):
    T = h2p.shape[0]
    nc, ns = _sc_workers()
    per_w = T // (nc * ns)
    n_ch = per_w // SC_CHUNK
    nt, _, tm = dest.shape
    assert nt * tm == T and tm % per_w == 0 and per_w % SC_CHUNK == 0
    idx = dest.reshape(nt, TOP_K, tm // per_w, per_w).transpose(0, 2, 1, 3).reshape(nc * ns, TOP_K * n_ch, SC_CHUNK)
    mesh = plsc.VectorSubcoreMesh(core_axis_name="c", subcore_axis_name="s")

    @functools.partial(
        pl.kernel, mesh=mesh,
        out_type=jax.ShapeDtypeStruct((cap, PACKED), U32),
        scratch_types=[pltpu.VMEM((TOP_K * n_ch, SC_CHUNK), jnp.int32),
                       pltpu.VMEM((SC_BUFS, SC_CHUNK, PACKED), U32),
                       pltpu.SemaphoreType.DMA((SC_BUFS,)),
                       pltpu.SemaphoreType.DMA((SC_BUFS,))])
    def scatter(src_hbm, idx_hbm, out_hbm, idx_v, rows_v, rsem, wsem):
        wid = lax.axis_index("s") * nc + lax.axis_index("c")
        base = pl.multiple_of(wid * per_w, per_w)
        pltpu.sync_copy(idx_hbm.at[wid], idx_v)

        def read(c):
            b = c % SC_BUFS
            return [pltpu.make_async_copy(src_hbm.at[pl.ds(base + c * SC_CHUNK, SC_CHUNK)], rows_v.at[b], rsem.at[b])]

        def write(c):
            b = c % SC_BUFS
            return [pltpu.make_async_copy(rows_v.at[b], out_hbm.at[idx_v.at[k * n_ch + c]], wsem.at[b])
                    for k in range(TOP_K)]
        _sc_ring(n_ch, read, write)

    return scatter(h2p, idx)


def _sc_gather(yb, dest):
    nt, _, tm = dest.shape
    nc, ns = _sc_workers()
    n_rows = nt * TOP_K * tm
    per_w = n_rows // (nc * ns)
    n_ch = per_w // SC_CHUNK
    assert per_w * nc * ns == n_rows and per_w % SC_CHUNK == 0
    mesh = plsc.VectorSubcoreMesh(core_axis_name="c", subcore_axis_name="s")

    @functools.partial(
        pl.kernel, mesh=mesh,
        out_type=jax.ShapeDtypeStruct((n_rows, PACKED), U32),
        scratch_types=[pltpu.VMEM((per_w,), jnp.int32),
                       pltpu.VMEM((SC_BUFS, SC_CHUNK, PACKED), U32),
                       pltpu.SemaphoreType.DMA((SC_BUFS,)),
                       pltpu.SemaphoreType.DMA((SC_BUFS,))])
    def gather(table_hbm, idx_hbm, out_hbm, idx_v, rows_v, gsem, wsem):
        wid = lax.axis_index("s") * nc + lax.axis_index("c")
        base = pl.multiple_of(wid * per_w, per_w)
        pltpu.sync_copy(idx_hbm.at[pl.ds(base, per_w)], idx_v)

        def read(c):
            b = c % SC_BUFS
            return [pltpu.make_async_copy(table_hbm.at[idx_v.at[pl.ds(c * SC_CHUNK, SC_CHUNK)]], rows_v.at[b], gsem.at[b])]

        def write(c):
            b = c % SC_BUFS
            return [pltpu.make_async_copy(rows_v.at[b], out_hbm.at[pl.ds(base + c * SC_CHUNK, SC_CHUNK)], wsem.at[b])]
        _sc_ring(n_ch, read, write)

    return gather(yb, dest.reshape(n_rows)).reshape(nt, TOP_K, tm, PACKED)


def _padfill_kernel(cnt_ref, pstart, xs_in, xs_ref, zeros, zsem):
    del xs_in

    def pad_copies(fn):
        for e in range(N_EXPERTS):
            cnt = cnt_ref[0, e]
            head = (-cnt) & (SUBLANES - 1)
            rest = ((-cnt) & (MOE_BLOCK - 1)) - head
            off = pstart[e] + cnt
            for k in range(SUBLANES - 1):
                @pl.when(k < head)
                def _(off=off, k=k):
                    fn(pltpu.make_async_copy(zeros.at[pl.ds(0, 1), :], xs_ref.at[pl.ds(off + k, 1), :], zsem))
            off = off + head
            for bit in PAD_BITS:
                @pl.when((rest & bit) != 0)
                def _(off=off, bit=bit):
                    fn(pltpu.make_async_copy(zeros.at[pl.ds(0, bit), :],
                                             xs_ref.at[pl.ds(pl.multiple_of(off, SUBLANES), bit), :], zsem))
                off = off + (rest & bit)

    zeros[...] = jnp.zeros_like(zeros)
    pad_copies(lambda cp: cp.start())
    pad_copies(lambda cp: cp.wait())


def _padfill(xs, pstart, cnt):
    smem = pl.BlockSpec(memory_space=pltpu.SMEM)
    hbm = pl.BlockSpec(memory_space=pl.ANY)
    return pl.pallas_call(
        _padfill_kernel,
        in_specs=[smem, smem, hbm],
        out_specs=hbm,
        out_shape=jax.ShapeDtypeStruct(xs.shape, xs.dtype),
        input_output_aliases={2: 0},
        scratch_shapes=[pltpu.VMEM((MOE_BLOCK // 2, PACKED), U32), pltpu.SemaphoreType.DMA(())],
        name="padfill",
    )(cnt, pstart, xs)


W_SLOTS = 3
W_AHEAD = W_SLOTS - 1
EXPERT_GROUP = 16
EXPERT_RUNS = (1, 2, 4, 8)


def _expert_kernel(be_ref, nu_ref, ge_ref, x_ref, wg_hbm, wu_hbm, wd_hbm, o_ref,
                   wgf, wuf, wdf, grp_ref, sems):
    step = pl.program_id(0)

    def weight_copies(e, slot):
        return (pltpu.make_async_copy(wg_hbm.at[e], wgf.at[slot], sems.at[slot, 0]),
                pltpu.make_async_copy(wu_hbm.at[e], wuf.at[slot], sems.at[slot, 1]),
                pltpu.make_async_copy(wd_hbm.at[e], wdf.at[slot], sems.at[slot, 2]))

    @pl.when(step == 0)
    def _():
        grp_ref[0] = 0
        for a in range(W_AHEAD):
            @pl.when(ge_ref[a] >= 0)
            def _(a=a):
                for cp in weight_copies(ge_ref[a], a):
                    cp.start()

    n_blocks = be_ref.shape[0]
    n_used = nu_ref[0]

    def swiglu(s, n, slot):
        rows = pl.ds(pl.multiple_of(s * MOE_BLOCK, MOE_BLOCK), n * MOE_BLOCK)
        lo, hi = _unpack_rows(x_ref[rows, :])
        lo = lo.astype(BF16)
        hi = hi.astype(BF16)
        g = (jnp.dot(lo, wgf[slot, :PACKED, :], preferred_element_type=F32)
             + jnp.dot(hi, wgf[slot, PACKED:, :], preferred_element_type=F32))
        u = (jnp.dot(lo, wuf[slot, :PACKED, :], preferred_element_type=F32)
             + jnp.dot(hi, wuf[slot, PACKED:, :], preferred_element_type=F32))
        h = (g * _sigmoid(g) * u).astype(BF16)
        o_ref[rows, :] = _pack_rows(jnp.dot(h, wdf[slot], preferred_element_type=F32))

    def run(s):
        j = step * EXPERT_GROUP + s
        e = be_ref[j]
        first = jnp.logical_or(j == 0, e != be_ref[jnp.maximum(j - 1, 0)])

        @pl.when(first)
        def _():
            grp = grp_ref[0]
            slot = grp % W_SLOTS
            for cp in weight_copies(e, slot):
                cp.wait()
            nxt = ge_ref[grp + W_AHEAD]

            @pl.when(nxt >= 0)
            def _():
                for cp in weight_copies(nxt, (grp + W_AHEAD) % W_SLOTS):
                    cp.start()
            grp_ref[0] = grp + 1

        def same(k):
            return (s + k < EXPERT_GROUP) & (j + k < n_used) & (be_ref[jnp.minimum(j + k, n_blocks - 1)] == e)
        take = jnp.int32(1)
        for n in EXPERT_RUNS[1:]:
            ok = same(n - 1)
            for k in range(1, n - 1):
                ok = ok & same(k)
            take = jnp.where(ok, n, take)
        slot = (grp_ref[0] + W_SLOTS - 1) % W_SLOTS
        for n in EXPERT_RUNS:
            @pl.when(take == n)
            def _(n=n):
                swiglu(s, n, slot)
        return s + take

    lax.while_loop(lambda s: (s < EXPERT_GROUP) & (step * EXPERT_GROUP + s < n_used), run, jnp.int32(0))


def _experts(xs, block_expert, n_used, group_expert, w_gate, w_up, w_down):
    cap = xs.shape[0]
    n_blocks = cap // MOE_BLOCK
    assert n_blocks % EXPERT_GROUP == 0
    rows = EXPERT_GROUP * MOE_BLOCK
    last = lambda j, be, nu, ge: jnp.minimum(j, (nu[0] - 1) // EXPERT_GROUP)
    hbm = pl.BlockSpec(memory_space=pl.ANY)
    gs = pltpu.PrefetchScalarGridSpec(
        num_scalar_prefetch=3,
        grid=(n_blocks // EXPERT_GROUP,),
        in_specs=[pl.BlockSpec((rows, PACKED), lambda j, be, nu, ge: (last(j, be, nu, ge), 0)), hbm, hbm, hbm],
        out_specs=pl.BlockSpec((rows, PACKED), lambda j, be, nu, ge: (last(j, be, nu, ge), 0)),
        scratch_shapes=[pltpu.VMEM((W_SLOTS, D_MODEL, D_EXPERT), F32),
                        pltpu.VMEM((W_SLOTS, D_MODEL, D_EXPERT), F32),
                        pltpu.VMEM((W_SLOTS, D_EXPERT, D_MODEL), F32),
                        pltpu.SMEM((1,), jnp.int32),
                        pltpu.SemaphoreType.DMA((W_SLOTS, 3))],
    )
    return pl.pallas_call(
        _expert_kernel,
        grid_spec=gs,
        out_shape=jax.ShapeDtypeStruct((cap, PACKED), U32),
        compiler_params=_cparams(1),
        name="experts",
    )(block_expert, n_used, group_expert, xs, w_gate, w_up, w_down)


CB_TM = RT_TM


def _combine_kernel(x1_ref, gt_ref, y2_ref, o_ref):
    g = gt_ref[...]
    lo1, hi1 = _unpack_rows(y2_ref[0, 0])
    lo2, hi2 = _unpack_rows(y2_ref[0, 1])
    o_ref[:, :PACKED] = x1_ref[:, :PACKED] + g[:, 0:1] * lo1 + g[:, 1:2] * lo2
    o_ref[:, PACKED:] = x1_ref[:, PACKED:] + g[:, 0:1] * hi1 + g[:, 1:2] * hi2


def _combine(x1, gates, y2):
    T = x1.shape[0]
    nt = T // CB_TM
    return pl.pallas_call(
        _combine_kernel,
        grid=(nt,),
        in_specs=[pl.BlockSpec((CB_TM, D_MODEL), lambda i: (i, 0)),
                  pl.BlockSpec((CB_TM, RINFO), lambda i: (i, 0)),
                  pl.BlockSpec((1, TOP_K, CB_TM, PACKED), lambda i: (i, 0, 0, 0))],
        out_specs=pl.BlockSpec((CB_TM, D_MODEL), lambda i: (i, 0)),
        out_shape=jax.ShapeDtypeStruct((T, D_MODEL), F32),
        compiler_params=_cparams(1),
        name="combine",
    )(x1, gates, y2)


def _layer(x, rel_bias, ln1, w_in, q_norm, k_norm, attn_sink, conv_w, conv_b, lru_wa, lru_ba, lru_wi, lru_bi,
           lru_lambda, out_norm_attn, out_norm_lru, w_out, ln2, w_group, b_group, w_er, b_er, w_gate, w_up, w_down):
    B, S, D = x.shape
    T = B * S
    x2 = x.reshape(T, D)
    q, kv, xr, gr = _in_proj(x2, ln1, w_in, q_norm, k_norm)
    attn_n = _attention(q.reshape(B, S, ATTN_WIDTH), kv.reshape(B, S, 2 * KV_WIDTH), rel_bias, attn_sink,
                        out_norm_attn)
    lru_n = _rglru(xr.reshape(B, S, LRU_WIDTH), gr.reshape(B, S, LRU_WIDTH), conv_w, conv_b,
                   lru_wa, lru_ba, lru_wi, lru_bi, lru_lambda, out_norm_lru)
    x1, h2, gates, ei, cnt = _out_route(attn_n.reshape(T, ATTN_WIDTH), lru_n.reshape(T, LRU_WIDTH), x2, w_out, ln2,
                                        w_group, b_group, w_er, b_er)
    cap = _moe_cap(T)
    dest, pstart, block_expert, n_used, group_expert = _layout(ei, cnt, cap // MOE_BLOCK)
    xs = _padfill(_sc_dispatch(h2, dest, cap), pstart, cnt)
    yb = _experts(xs, block_expert, n_used, group_expert, w_gate, w_up, w_down)
    out = _combine(x1, gates, _sc_gather(yb, dest))
    return out.reshape(B, S, D)


def kernel(x, rel_bias, ln1, w_in, q_norm, k_norm, attn_sink, conv_w, conv_b, lru_wa, lru_ba, lru_wi, lru_bi,
           lru_lambda, out_norm_attn, out_norm_lru, w_out, ln2, w_group, b_group, w_expert_router, b_expert_router,
           w_gate, w_up, w_down):
    depth = ln1.shape[0]
    for l in range(depth):
        x = _layer(x, rel_bias, ln1[l], w_in[l], q_norm[l], k_norm[l], attn_sink[l], conv_w[l], conv_b[l],
                   lru_wa[l], lru_ba[l], lru_wi[l], lru_bi[l], lru_lambda[l], out_norm_attn[l], out_norm_lru[l],
                   w_out[l], ln2[l], w_group[l], b_group[l], w_expert_router[l], b_expert_router[l],
                   w_gate[l], w_up[l], w_down[l])
    return x
```

```python
import functools
import math

import jax
import jax.numpy as jnp
from jax import lax
from jax.experimental import pallas as pl
from jax.experimental.pallas import tpu as pltpu
from jax.experimental.pallas import tpu_sc as plsc

D_MODEL = 1024
N_HEADS = 8
N_KV_HEADS = 2
HEAD_DIM = 64
Q_PER_KV = N_HEADS // N_KV_HEADS
ATTN_WIDTH = N_HEADS * HEAD_DIM
KV_WIDTH = N_KV_HEADS * HEAD_DIM
WINDOW = 128
BLOCK = 128
NUM_BUCKETS = 32
MAX_DISTANCE = 128
LRU_WIDTH = D_MODEL - ATTN_WIDTH
LRU_BLOCKS = 8
LRU_BLOCK_DIM = LRU_WIDTH // LRU_BLOCKS
LRU_C = 8.0
CONV_W = 4
CONV_LEFT = 2
N_GROUPS = 4
EXPERTS_PER_GROUP = 8
N_EXPERTS = N_GROUPS * EXPERTS_PER_GROUP
TOP_K = 2
D_EXPERT = 512
MOE_BLOCK = 128
EPS = 1e-6
NEG_INF = -1e30

LANES = 128
SUBLANES = 8
VMEM_LIMIT = 56 * 1024 * 1024
LRU_VMEM_LIMIT = 62 * 1024 * 1024

F32 = jnp.float32
BF16 = jnp.bfloat16
LOG2E = math.log2(math.e)


def _cparams(n_axes, vmem=VMEM_LIMIT):
    return pltpu.CompilerParams(dimension_semantics=("arbitrary",) * n_axes, vmem_limit_bytes=vmem)


def _rms(x, gain):
    return x * lax.rsqrt(jnp.mean(x * x, axis=-1, keepdims=True) + EPS) * gain


U32 = jnp.uint32
HI_MASK = 0xFFFF0000
PACKED = D_MODEL // 2


def _pack_rows(x):
    return _pack_rounded(x.astype(BF16).astype(F32))


def _pack_rounded(xb):
    h = xb.shape[1] // 2
    lo = lax.bitcast_convert_type(xb[:, :h], U32) >> 16
    hi = lax.bitcast_convert_type(xb[:, h:], U32) & jnp.uint32(HI_MASK)
    return lo | hi


def _unpack_rows(p):
    lo = lax.bitcast_convert_type(p << 16, F32)
    hi = lax.bitcast_convert_type(p & jnp.uint32(HI_MASK), F32)
    return lo, hi


IN_TM = 1024


def _head_rms(x, n_heads, gain):
    head = lax.broadcasted_iota(jnp.int32, (1, n_heads * HEAD_DIM), 1) // HEAD_DIM
    x2 = x * x
    scale = jnp.zeros_like(x)
    for h in range(n_heads):
        ms = jnp.sum(jnp.where(head == h, x2, 0.0), axis=-1, keepdims=True) * (1.0 / HEAD_DIM)
        scale = jnp.where(head == h, lax.rsqrt(ms + EPS), scale)
    return x * scale * gain


def _in_proj_kernel(x_ref, g_ref, w_ref, qn_ref, kn_ref, q_ref, kv_ref, xr_ref, gr_ref, wb_ref, qg_ref, kg_ref):
    @pl.when(pl.program_id(0) == 0)
    def _():
        wb_ref[...] = w_ref[...].astype(BF16)
        qg_ref[...] = jnp.concatenate([qn_ref[...]] * N_HEADS, axis=1) * (HEAD_DIM ** -0.5 * LOG2E)
        kg_ref[...] = jnp.concatenate([kn_ref[...]] * N_KV_HEADS, axis=1)

    h = _rms(x_ref[...], g_ref[...]).astype(BF16)
    c_k = ATTN_WIDTH
    c_v = c_k + KV_WIDTH
    c_x = c_v + KV_WIDTH
    c_g = c_x + LRU_WIDTH
    q = jnp.dot(h, wb_ref[:, :c_k], preferred_element_type=F32)
    q_ref[...] = _head_rms(q, N_HEADS, qg_ref[...]).astype(BF16)
    k = jnp.dot(h, wb_ref[:, c_k:c_v], preferred_element_type=F32)
    kv_ref[:, :KV_WIDTH] = _head_rms(k, N_KV_HEADS, kg_ref[...]).astype(BF16)
    kv_ref[:, KV_WIDTH:] = jnp.dot(h, wb_ref[:, c_v:c_x], preferred_element_type=F32).astype(BF16)
    xr_ref[...] = jnp.dot(h, wb_ref[:, c_x:c_g], preferred_element_type=F32)
    gr_ref[...] = jnp.dot(h, wb_ref[:, c_g:], preferred_element_type=F32)


def _in_proj(x2, ln1, w_in, q_gain, k_gain):
    T = x2.shape[0]
    n_in = w_in.shape[1]
    row = lambda w: pl.BlockSpec((IN_TM, w), lambda i: (i, 0))
    return pl.pallas_call(
        _in_proj_kernel,
        grid=(T // IN_TM,),
        in_specs=[row(D_MODEL),
                  pl.BlockSpec((1, D_MODEL), lambda i: (0, 0)),
                  pl.BlockSpec((D_MODEL, n_in), lambda i: (0, 0)),
                  pl.BlockSpec((1, HEAD_DIM), lambda i: (0, 0)),
                  pl.BlockSpec((1, HEAD_DIM), lambda i: (0, 0))],
        out_specs=[row(ATTN_WIDTH), row(2 * KV_WIDTH), row(LRU_WIDTH), row(LRU_WIDTH)],
        out_shape=[jax.ShapeDtypeStruct((T, ATTN_WIDTH), BF16),
                   jax.ShapeDtypeStruct((T, 2 * KV_WIDTH), BF16),
                   jax.ShapeDtypeStruct((T, LRU_WIDTH), F32),
                   jax.ShapeDtypeStruct((T, LRU_WIDTH), F32)],
        scratch_shapes=[pltpu.VMEM((D_MODEL, n_in), BF16),
                        pltpu.VMEM((1, ATTN_WIDTH), F32),
                        pltpu.VMEM((1, KV_WIDTH), F32)],
        compiler_params=_cparams(1),
        name="in_proj",
    )(x2, ln1.reshape(1, D_MODEL), w_in, q_gain.reshape(1, HEAD_DIM).astype(F32),
      k_gain.reshape(1, HEAD_DIM).astype(F32))


def _t5_bucket(rel):
    half = NUM_BUCKETS // 2
    max_exact = half // 2
    base = jnp.where(rel > 0, half, 0)
    n = jnp.abs(rel)
    nf = jnp.maximum(n, 1).astype(jnp.float32)
    large = max_exact + (jnp.log(nf / max_exact) / math.log(MAX_DISTANCE / max_exact)
                         * (half - max_exact)).astype(jnp.int32)
    large = jnp.minimum(large, half - 1)
    return base + jnp.where(n < max_exact, n, large)


HEAD_PAIRS = Q_PER_KV // 2
EDGE_VARIANTS = 3


def _fill_bias_table(rb_ref, bucket_ref, band_ref, o_ref):
    bucket = bucket_ref[...]
    band = band_ref[...] > 0
    col = lax.broadcasted_iota(jnp.int32, bucket.shape, 1)
    valid = (band & (col >= BLOCK), band, band & (col < 2 * BLOCK))
    for h in range(N_HEADS):
        acc = jnp.zeros(bucket.shape, F32)
        for b in range(NUM_BUCKETS):
            acc = jnp.where(bucket == b, rb_ref[b, h], acc)
        kv, g = divmod(h, Q_PER_KV)
        pair, parity = divmod(g, 2)
        for var in range(EDGE_VARIANTS):
            o_ref[var, kv, parity, pair * BLOCK:(pair + 1) * BLOCK, :] = jnp.where(valid[var], acc * LOG2E, NEG_INF)


def _attn_kernel(sink_ref, rb_ref, q_ref, kp_ref, kc_ref, kn_ref, bucket_ref, band_ref, og_ref, o_ref, bias_ref):
    n = pl.program_id(1)

    @pl.when((pl.program_id(0) == 0) & (n == 0))
    def _():
        _fill_bias_table(rb_ref, bucket_ref, band_ref, bias_ref)

    kv_all = jnp.concatenate([kp_ref[0], kc_ref[0], kn_ref[0]], axis=0)
    for qb in range(ATTN_QB):
        variant = 1
        if qb == 0:
            variant = jnp.where(n == 0, 0, 1)
        if qb == ATTN_QB - 1:
            variant = jnp.where(n == pl.num_programs(1) - 1, 2, variant)
        out = _attn_block(q_ref[0, qb * BLOCK:(qb + 1) * BLOCK, :], kv_all[qb * BLOCK:(qb + 3) * BLOCK, :],
                          lambda kv, parity: bias_ref[variant, kv, parity], sink_ref)
        o_ref[0, qb * BLOCK:(qb + 1) * BLOCK, :] = _rms(out, og_ref[...]).astype(o_ref.dtype)


def _attn_block(q, kvw, bias, sink_ref):
    low = lax.broadcasted_iota(jnp.int32, (1, LANES), 1) < HEAD_DIM
    swap = lambda slab: pltpu.roll(slab.astype(F32), HEAD_DIM, 1).astype(BF16)
    kslab, vslab = kvw[:, :KV_WIDTH], kvw[:, KV_WIDTH:]
    kslab_sw, vslab_sw = swap(kslab), swap(vslab)
    rowi = lax.broadcasted_iota(jnp.int32, (HEAD_PAIRS * BLOCK, 1), 0)
    combos = [(kv, parity) for kv in range(N_KV_HEADS) for parity in range(2)]
    scores, vzs, sinks = [], [], []
    for kv, parity in combos:
        ks, vs = (kslab, vslab) if (kv == 0) == (parity == 0) else (kslab_sw, vslab_sw)
        keep = low if parity == 0 else jnp.logical_not(low)
        kz = jnp.where(keep, ks, jnp.zeros_like(ks))
        vzs.append(jnp.where(keep, vs, jnp.zeros_like(vs)))
        base = kv * Q_PER_KV * HEAD_DIM
        qpair = jnp.concatenate([q[:, base + j * LANES:base + (j + 1) * LANES] for j in range(HEAD_PAIRS)], axis=0)
        s = lax.dot_general(qpair, kz, (((1,), (1,)), ((), ())), preferred_element_type=F32)
        scores.append(s + bias(kv, parity))
        sink = jnp.zeros((HEAD_PAIRS * BLOCK, 1), F32)
        for j in range(HEAD_PAIRS):
            sink = jnp.where(rowi // BLOCK == j, sink_ref[kv * Q_PER_KV + 2 * j + parity], sink)
        sinks.append(sink)
    probs, inv = [], []
    for s, sink in zip(scores, sinks):
        m = jnp.maximum(jnp.max(s, axis=-1, keepdims=True), sink)
        p = jnp.exp2(s - m)
        inv.append(1.0 / (jnp.sum(p, axis=-1, keepdims=True) + jnp.exp2(sink - m)))
        probs.append(p.astype(BF16))
    outs = [jnp.dot(p, vz, preferred_element_type=F32) * r for p, vz, r in zip(probs, vzs, inv)]
    cols = []
    for kv in range(N_KV_HEADS):
        acc = outs[2 * kv] + outs[2 * kv + 1]
        cols += [acc[j * BLOCK:(j + 1) * BLOCK, :] for j in range(HEAD_PAIRS)]
    return jnp.concatenate(cols, axis=1)


ATTN_QB = 8


def _attention(q, kv, rel_bias, sink, out_gain):
    B, S, _ = q.shape
    nb = S // BLOCK
    assert ATTN_QB >= 2 and nb % ATTN_QB == 0, "a step's first and last query blocks must be distinct"
    ns = nb // ATTN_QB
    rows = ATTN_QB * BLOCK
    qi = jnp.arange(BLOCK, dtype=jnp.int32)
    kj = jnp.arange(3 * BLOCK, dtype=jnp.int32)
    rel = kj[None, :] - BLOCK - qi[:, None]
    bucket = _t5_bucket(rel).astype(jnp.int32)
    band = (jnp.abs(rel) <= WINDOW).astype(jnp.int32)
    kvspec = lambda f: pl.BlockSpec((1, BLOCK, 2 * KV_WIDTH), f)
    smem = pl.BlockSpec(memory_space=pltpu.SMEM)
    geom = pl.BlockSpec((BLOCK, 3 * BLOCK), lambda b, n: (0, 0))
    return pl.pallas_call(
        _attn_kernel,
        grid=(B, ns),
        in_specs=[smem, smem,
                  pl.BlockSpec((1, rows, ATTN_WIDTH), lambda b, n: (b, n, 0)),
                  kvspec(lambda b, n: (b, jnp.maximum(n * ATTN_QB - 1, 0), 0)),
                  pl.BlockSpec((1, rows, 2 * KV_WIDTH), lambda b, n: (b, n, 0)),
                  kvspec(lambda b, n: (b, jnp.minimum((n + 1) * ATTN_QB, nb - 1), 0)),
                  geom, geom,
                  pl.BlockSpec((1, ATTN_WIDTH), lambda b, n: (0, 0))],
        out_specs=pl.BlockSpec((1, rows, ATTN_WIDTH), lambda b, n: (b, n, 0)),
        out_shape=jax.ShapeDtypeStruct((B, S, ATTN_WIDTH), BF16),
        scratch_shapes=[pltpu.VMEM((EDGE_VARIANTS, N_KV_HEADS, 2, HEAD_PAIRS * BLOCK, 3 * BLOCK), F32)],
        compiler_params=_cparams(2),
        name="attention",
    )(sink.astype(F32) * LOG2E, rel_bias.astype(F32), q, kv, kv, kv, bucket, band, out_gain.reshape(1, ATTN_WIDTH))


LRU_TC = 128
LRU_PITCH = LRU_TC + SUBLANES // 2
LRU_SLABS = LRU_WIDTH // LANES
LRU_UNROLL = 32
HALO = SUBLANES


def _softplus(x):
    return jnp.maximum(x, 0.0) + jnp.log(1.0 + jnp.exp(-jnp.abs(x)))


def _gelu_tanh(x):
    k = math.sqrt(2.0 / math.pi)
    hx = 0.5 * x
    return hx + hx * jnp.tanh(x * (k + (k * 0.044715) * (x * x)))


def _sigmoid(x):
    return 0.5 + 0.5 * jnp.tanh(0.5 * x)


def _rglru_kernel(xr_ref, xp_ref, xn_ref, gr_ref, cw_ref, cb_ref, wa_ref, wi_ref, ba_ref, bi_ref, lam_ref, og_ref,
                  o_ref, sx_ref, a_ref, u_ref, h_ref, carry_ref, hf_ref, xcs_ref, wg_ref, bg_ref, k_ref):
    p = pl.program_id(0)
    i = pl.program_id(1)
    nc = pl.num_programs(1)
    c = i + p * (nc - 1 - 2 * i)
    B = xr_ref.shape[0]
    TC = LRU_TC

    @pl.when(i == 0)
    def _():
        carry_ref[...] = jnp.zeros_like(carry_ref)
        wg_ref[...] = jnp.zeros_like(wg_ref)
        for sel, w_ref in enumerate((wa_ref, wi_ref)):
            for h in range(LRU_BLOCKS):
                lo = h * LRU_BLOCK_DIM
                wg_ref[lo:lo + LRU_BLOCK_DIM, sel * LRU_WIDTH + lo:sel * LRU_WIDTH + lo + LRU_BLOCK_DIM] = (
                    0.5 * w_ref[0, h]).astype(BF16)
        row = pl.ds(p, 1)
        bg_ref[:, :LRU_WIDTH] = 0.5 * ba_ref[row, :]
        bg_ref[:, LRU_WIDTH:] = 0.5 * bi_ref[row, :]
        k_ref[...] = (-0.5 * LRU_C * math.log2(math.e)) * _softplus(-lam_ref[row, :])

    def gates_and_scan(xc2, backward):
        g = jnp.dot(xc2.astype(BF16), wg_ref[...], preferred_element_type=F32) + bg_ref[...]
        ta = jnp.tanh(g[:, :LRU_WIDTH])
        ig = 0.5 + 0.5 * jnp.tanh(g[:, LRU_WIDTH:])
        a = jnp.exp2((1.0 + ta) * k_ref[...])
        z = 1.0 - a * a
        u = z * lax.rsqrt(jnp.maximum(z, 1e-30)) * ig * xc2
        for b in range(B):
            for s in range(LRU_SLABS):
                a_ref[s, b * LRU_PITCH:b * LRU_PITCH + TC, :] = a[b * TC:(b + 1) * TC, s * LANES:(s + 1) * LANES]
                u_ref[s, b * LRU_PITCH:b * LRU_PITCH + TC, :] = u[b * TC:(b + 1) * TC, s * LANES:(s + 1) * LANES]

        def trip(i, hs):
            t0 = pl.multiple_of((TC // LRU_UNROLL - 1 - i if backward else i) * LRU_UNROLL, LRU_UNROLL)
            for j in range(LRU_UNROLL):
                t = t0 + (LRU_UNROLL - 1 - j if backward else j)
                out = []
                for s in range(LRU_SLABS):
                    idx = pl.ds(t, B, stride=LRU_PITCH)
                    hn = a_ref[s, idx, :] * hs[s] + u_ref[s, idx, :]
                    h_ref[s, idx, :] = hn
                    out.append(hn)
                hs = tuple(out)
            return hs

        hs = lax.fori_loop(0, TC // LRU_UNROLL, trip, tuple(carry_ref[s] for s in range(LRU_SLABS)))
        for s in range(LRU_SLABS):
            carry_ref[s] = hs[s]

    @pl.when(p == 0)
    def _():
        sx_ref[:, HALO:HALO + TC, :] = xr_ref[...]
        sx_ref[:, 0:HALO, :] = jnp.where(c > 0, xp_ref[...], 0.0)
        sx_ref[:, HALO + TC:, :] = jnp.where(c < nc - 1, xn_ref[...], 0.0)
        xc = cb_ref[...][None]
        for j in range(CONV_W):
            off = HALO + j - CONV_LEFT
            xc = xc + cw_ref[j:j + 1, :][None] * sx_ref[:, off:off + TC, :]
        xc2 = xc.reshape(B * TC, LRU_WIDTH)
        xcs_ref[c] = xc2.astype(xcs_ref.dtype)
        gates_and_scan(xc2, backward=False)
        for b in range(B):
            for s in range(LRU_SLABS):
                hf_ref[c, s, b * TC:(b + 1) * TC, :] = h_ref[s, b * LRU_PITCH:b * LRU_PITCH + TC, :].astype(hf_ref.dtype)

    @pl.when(p == 1)
    def _():
        gates_and_scan(xcs_ref[c].astype(F32), backward=True)
        for b in range(B):
            hsum = jnp.concatenate(
                [h_ref[s, b * LRU_PITCH:b * LRU_PITCH + TC, :] + hf_ref[c, s, b * TC:(b + 1) * TC, :].astype(F32)
                 for s in range(LRU_SLABS)], axis=1)
            y = hsum * _gelu_tanh(gr_ref[b])
            o_ref[b] = _rms(y, og_ref[...]).astype(o_ref.dtype)


def _rglru(xr, gr, conv_w, conv_b, w_a, b_a, w_i, b_i, lam, out_gain):
    B, S, W = xr.shape
    nc = S // LRU_TC
    hb = LRU_TC // HALO
    fwd = lambda p, i: jnp.where(p == 0, i, nc - 1)
    bwd = lambda p, i: nc - 1 - p * i
    full2 = lambda shape: pl.BlockSpec(shape, lambda p, i: (0,) * len(shape))
    wblock = pl.BlockSpec((1, LRU_BLOCKS, LRU_BLOCK_DIM, LRU_BLOCK_DIM), lambda p, i: (p, 0, 0, 0))
    return pl.pallas_call(
        _rglru_kernel,
        grid=(2, nc),
        in_specs=[pl.BlockSpec((B, LRU_TC, W), lambda p, i: (0, fwd(p, i), 0)),
                  pl.BlockSpec((B, HALO, W), lambda p, i: (0, jnp.maximum(fwd(p, i) * hb - 1, 0), 0)),
                  pl.BlockSpec((B, HALO, W), lambda p, i: (0, jnp.minimum((fwd(p, i) + 1) * hb, S // HALO - 1), 0)),
                  pl.BlockSpec((B, LRU_TC, W), lambda p, i: (0, bwd(p, i), 0)),
                  full2((CONV_W, W)),
                  full2((1, W)),
                  wblock, wblock,
                  full2((2, W)), full2((2, W)), full2((2, W)),
                  full2((1, W))],
        out_specs=pl.BlockSpec((B, LRU_TC, W), lambda p, i: (0, bwd(p, i), 0)),
        out_shape=jax.ShapeDtypeStruct((B, S, W), BF16),
        scratch_shapes=[pltpu.VMEM((B, LRU_TC + 2 * HALO, W), F32),
                        pltpu.VMEM((LRU_SLABS, B * LRU_PITCH, LANES), F32),
                        pltpu.VMEM((LRU_SLABS, B * LRU_PITCH, LANES), F32),
                        pltpu.VMEM((LRU_SLABS, B * LRU_PITCH, LANES), F32),
                        pltpu.VMEM((LRU_SLABS, B, LANES), F32),
                        pltpu.VMEM((nc, LRU_SLABS, B * LRU_TC, LANES), BF16),
                        pltpu.VMEM((nc, B * LRU_TC, W), BF16),
                        pltpu.VMEM((W, 2 * W), BF16),
                        pltpu.VMEM((1, 2 * W), F32),
                        pltpu.VMEM((1, W), F32)],
        compiler_params=_cparams(2, LRU_VMEM_LIMIT),
        name="rglru",
    )(xr, xr, xr, gr, conv_w.astype(F32), conv_b.reshape(1, W).astype(F32), w_a.astype(F32), w_i.astype(F32),
      b_a.astype(F32), b_i.astype(F32), lam.astype(F32), out_gain.reshape(1, W).astype(F32))


RT_TM = 1024
RT_PARTS = 4
RT_COLS = LANES
RT_ROWS = 48
RINFO = SUBLANES


def _split_bf16(x):
    hi = x.astype(BF16)
    lo = (x - hi.astype(F32)).astype(BF16)
    return hi, lo


def _route_kernel(an_ref, ln_ref, x_ref, wo_ref, g2_ref, wr_ref, br_ref,
                  x1_ref, h2_ref, gt_ref, ei_ref, cnt_ref, wob_ref, wrb_ref, tri_ref, run_ref, runc_ref):
    @pl.when(pl.program_id(0) == 0)
    def _():
        wob_ref[...] = wo_ref[...].astype(BF16)
        hi, lo = _split_bf16(wr_ref[...])
        wrb_ref[:RT_ROWS, :] = hi
        wrb_ref[RT_ROWS:, :] = lo
        r = lax.broadcasted_iota(jnp.int32, (RT_TM, RT_TM), 0)
        cidx = lax.broadcasted_iota(jnp.int32, (RT_TM, RT_TM), 1)
        tri_ref[...] = (r < cidx).astype(BF16)
        run_ref[...] = jnp.zeros_like(run_ref)
        runc_ref[...] = jnp.zeros_like(runc_ref)

    nt_dims = (((1,), (1,)), ((), ()))
    part = RT_TM // RT_PARTS
    x1s = []
    for r in range(RT_PARTS):
        rows = slice(r * part, (r + 1) * part)
        x1 = (x_ref[rows, :]
              + jnp.dot(an_ref[rows, :], wob_ref[:ATTN_WIDTH, :], preferred_element_type=F32)
              + jnp.dot(ln_ref[rows, :], wob_ref[ATTN_WIDTH:, :], preferred_element_type=F32))
        x1_ref[rows, :] = x1
        x1s.append(x1)
    splits = []
    for r, x1 in enumerate(x1s):
        h2 = _rms(x1, g2_ref[...])
        hi = h2.astype(BF16)
        hi_f = hi.astype(F32)
        h2_ref[r * part:(r + 1) * part, :] = _pack_rounded(hi_f)
        splits.append((hi, (h2 - hi_f).astype(BF16)))
    logits = []
    for hi, lo in splits:
        t1 = lax.dot_general(wrb_ref[...], hi, nt_dims, preferred_element_type=F32)
        t2 = lax.dot_general(wrb_ref[:RT_ROWS, :], lo, nt_dims, preferred_element_type=F32)
        logits.append(t1[:RT_ROWS] + t1[RT_ROWS:] + t2)
    logit = jnp.concatenate(logits, axis=1) + br_ref[...]

    sub = lax.broadcasted_iota(jnp.int32, (SUBLANES, RT_TM), 0)
    first_min = lambda hit: jnp.min(jnp.where(hit, sub, SUBLANES), axis=0, keepdims=True)
    is_g = sub < N_GROUPS
    gl = jnp.where(is_g, logit[:SUBLANES], -jnp.inf)
    gm = jnp.max(gl, axis=0, keepdims=True)
    gidx = first_min(gl == gm)
    g_p = 1.0 / jnp.sum(jnp.where(is_g, jnp.exp(logit[:SUBLANES] - gm), 0.0), axis=0, keepdims=True)
    el = logit[SUBLANES:2 * SUBLANES]
    for g in range(1, N_GROUPS):
        el = jnp.where(gidx == g, logit[(g + 1) * SUBLANES:(g + 2) * SUBLANES], el)
    m1 = jnp.max(el, axis=0, keepdims=True)
    i1 = first_min(el == m1)
    el2 = jnp.where(sub == i1, -jnp.inf, el)
    m2 = jnp.max(el2, axis=0, keepdims=True)
    i2 = first_min(el2 == m2)
    t = jnp.exp(m2 - m1)
    gate1 = g_p / (1.0 + t)
    gate2 = g_p * t / (1.0 + t)
    e1 = gidx * EXPERTS_PER_GROUP + i1
    e2 = gidx * EXPERTS_PER_GROUP + i2

    erow = lax.broadcasted_iota(jnp.int32, (N_EXPERTS, RT_TM), 0)
    oh1 = erow == e1
    oh2 = erow == e2
    oh = (oh1 | oh2).astype(F32)
    ohb = oh.astype(BF16)
    cum = jnp.dot(ohb, tri_ref[...], preferred_element_type=F32) + runc_ref[...]
    rank1 = jnp.sum(jnp.where(oh1, cum, 0.0), axis=0, keepdims=True)
    rank2 = jnp.sum(jnp.where(oh2, cum, 0.0), axis=0, keepdims=True)
    runc_ref[...] = runc_ref[...] + jnp.sum(oh, axis=1, keepdims=True)
    tile_cnt = lax.dot_general(jnp.ones((SUBLANES, RT_TM), BF16), ohb, nt_dims, preferred_element_type=F32)
    run_ref[:, :N_EXPERTS] = run_ref[:, :N_EXPERTS] + tile_cnt[0:1]
    cnt_ref[...] = run_ref[...].astype(jnp.int32)

    rows = [e1, e2, rank1.astype(jnp.int32), rank2.astype(jnp.int32)]
    ei = jnp.zeros((RINFO, RT_TM), jnp.int32)
    for k, v in enumerate(rows):
        ei = jnp.where(sub == k, v, ei)
    ei_ref[0] = ei
    gt_ref[...] = jnp.where(sub == 0, gate1, jnp.where(sub == 1, gate2, 0.0)).T


def _out_route(attn_n, lru_n, x2, w_out, ln2, w_group, b_group, w_er, b_er):
    T = x2.shape[0]
    pad_g = SUBLANES - N_GROUPS
    wr = jnp.concatenate([jnp.pad(w_group.T, ((0, pad_g), (0, 0))),
                          jnp.transpose(w_er, (0, 2, 1)).reshape(N_EXPERTS, D_MODEL)], axis=0)
    wr = jnp.pad(wr, ((0, RT_ROWS - wr.shape[0]), (0, 0))).astype(F32)
    br = jnp.concatenate([jnp.pad(b_group, (0, pad_g)), b_er.reshape(-1)])
    br = jnp.pad(br, (0, RT_ROWS - br.shape[0])).reshape(RT_ROWS, 1).astype(F32)
    row = lambda w: pl.BlockSpec((RT_TM, w), lambda i: (i, 0))
    const = lambda shape: pl.BlockSpec(shape, lambda i: (0, 0))
    return pl.pallas_call(
        _route_kernel,
        grid=(T // RT_TM,),
        in_specs=[row(ATTN_WIDTH), row(LRU_WIDTH), row(D_MODEL), const((D_MODEL, D_MODEL)), const((1, D_MODEL)),
                  const((RT_ROWS, D_MODEL)), const((RT_ROWS, 1))],
        out_specs=[row(D_MODEL), row(PACKED), row(RINFO),
                   pl.BlockSpec((1, RINFO, RT_TM), lambda i: (i, 0, 0)), const((1, RT_COLS))],
        out_shape=[jax.ShapeDtypeStruct((T, D_MODEL), F32),
                   jax.ShapeDtypeStruct((T, PACKED), U32),
                   jax.ShapeDtypeStruct((T, RINFO), F32),
                   jax.ShapeDtypeStruct((T // RT_TM, RINFO, RT_TM), jnp.int32),
                   jax.ShapeDtypeStruct((1, RT_COLS), jnp.int32)],
        scratch_shapes=[pltpu.VMEM((D_MODEL, D_MODEL), BF16),
                        pltpu.VMEM((2 * RT_ROWS, D_MODEL), BF16),
                        pltpu.VMEM((RT_TM, RT_TM), BF16),
                        pltpu.VMEM((1, RT_COLS), F32),
                        pltpu.VMEM((N_EXPERTS, 1), F32)],
        compiler_params=_cparams(1),
        name="out_route",
    )(attn_n, lru_n, x2, w_out, ln2.reshape(1, D_MODEL).astype(F32), wr, br)


def _moe_cap(T):
    A = T * TOP_K
    return ((A + MOE_BLOCK - 1) // MOE_BLOCK) * MOE_BLOCK + N_EXPERTS * MOE_BLOCK


PAD_BITS = tuple(1 << b for b in reversed(range(3, MOE_BLOCK.bit_length() - 1)))


def _layout_kernel(cnt_ref, ei_ref, dest_ref, pstart, be_ref, nu_ref, ge_ref):
    n_blocks = be_ref.shape[0]

    def lay(e, carry):
        start, blk, grp = carry
        pstart[e] = start
        nb = (cnt_ref[0, e] + MOE_BLOCK - 1) // MOE_BLOCK
        ge_ref[grp] = e

        def fill(k, c):
            be_ref[blk + k] = e
            return c
        lax.fori_loop(0, nb, fill, 0)
        return start + nb * MOE_BLOCK, blk + nb, grp + (nb > 0).astype(jnp.int32)
    _, used, groups = lax.fori_loop(0, N_EXPERTS, lay, (jnp.int32(0), jnp.int32(0), jnp.int32(0)))
    nu_ref[0] = used

    def tail(k, c):
        be_ref[k] = N_EXPERTS - 1
        return c
    lax.fori_loop(used, n_blocks, tail, 0)

    def no_group(k, c):
        ge_ref[k] = -1
        return c
    lax.fori_loop(groups, ge_ref.shape[0], no_group, 0)

    expert = ei_ref[:, 0:TOP_K, :]
    dest = ei_ref[:, TOP_K:2 * TOP_K, :]
    for e in range(N_EXPERTS):
        dest = dest + jnp.where(expert == e, pstart[e], 0)
    dest_ref[...] = dest


def _layout(ei, cnt, n_blocks):
    nt = ei.shape[0]
    smem = pl.BlockSpec(memory_space=pltpu.SMEM)
    vmem = pl.BlockSpec(memory_space=pltpu.VMEM)
    return pl.pallas_call(
        _layout_kernel,
        in_specs=[smem, vmem],
        out_specs=[vmem, smem, smem, smem, smem],
        out_shape=[jax.ShapeDtypeStruct((nt, TOP_K, RT_TM), jnp.int32),
                   jax.ShapeDtypeStruct((N_EXPERTS,), jnp.int32),
                   jax.ShapeDtypeStruct((n_blocks,), jnp.int32),
                   jax.ShapeDtypeStruct((1,), jnp.int32),
                   jax.ShapeDtypeStruct((N_EXPERTS + W_AHEAD,), jnp.int32)],
        name="layout",
    )(cnt, ei)


SC_CHUNK = 64
SC_BUFS = 3
SC_LEAD = SC_BUFS - 1


def _sc_workers():
    info = plsc.get_sparse_core_info()
    return info.num_cores, info.num_subcores


def _sc_ring(n_chunks, read, write):
    for c in range(min(SC_LEAD, n_chunks)):
        for cp in read(c):
            cp.start()
    reclaimed = set()
    for c in range(n_chunks):
        for cp in read(c):
            cp.wait()
        for cp in write(c):
            cp.start()
        nxt = c + SC_LEAD
        if nxt < n_chunks:
            if nxt - SC_BUFS >= 0:
                for cp in write(nxt - SC_BUFS):
                    cp.wait()
                reclaimed.add(nxt - SC_BUFS)
            for cp in read(nxt):
                cp.start()
    for c in range(n_chunks):
        if c not in reclaimed:
            for cp in write(c):
                cp.wait()


def _sc_dispatch(h2p, dest, cap):
    T = h2p.shape[0]
    nc, ns = _sc_workers()
    per_w = T // (nc * ns)
    n_ch = per_w // SC_CHUNK
    nt, _, tm = dest.shape
    assert nt * tm == T and tm % per_w == 0 and per_w % SC_CHUNK == 0
    idx = dest.reshape(nt, TOP_K, tm // per_w, per_w).transpose(0, 2, 1, 3).reshape(nc * ns, TOP_K * n_ch, SC_CHUNK)
    mesh = plsc.VectorSubcoreMesh(core_axis_name="c", subcore_axis_name="s")

    @functools.partial(
        pl.kernel, mesh=mesh,
        out_type=jax.ShapeDtypeStruct((cap, PACKED), U32),
        scratch_types=[pltpu.VMEM((TOP_K * n_ch, SC_CHUNK), jnp.int32),
                       pltpu.VMEM((SC_BUFS, SC_CHUNK, PACKED), U32),
                       pltpu.SemaphoreType.DMA((SC_BUFS,)),
                       pltpu.SemaphoreType.DMA((SC_BUFS,))])
    def scatter(src_hbm, idx_hbm, out_hbm, idx_v, rows_v, rsem, wsem):
        wid = lax.axis_index("s") * nc + lax.axis_index("c")
        base = pl.multiple_of(wid * per_w, per_w)
        pltpu.sync_copy(idx_hbm.at[wid], idx_v)

        def read(c):
            b = c % SC_BUFS
            return [pltpu.make_async_copy(src_hbm.at[pl.ds(base + c * SC_CHUNK, SC_CHUNK)], rows_v.at[b], rsem.at[b])]

        def write(c):
            b = c % SC_BUFS
            return [pltpu.make_async_copy(rows_v.at[b], out_hbm.at[idx_v.at[k * n_ch + c]], wsem.at[b])
                    for k in range(TOP_K)]
        _sc_ring(n_ch, read, write)

    return scatter(h2p, idx)


def _sc_gather(yb, dest):
    nt, _, tm = dest.shape
    nc, ns = _sc_workers()
    n_rows = nt * TOP_K * tm
    per_w = n_rows // (nc * ns)
    n_ch = per_w // SC_CHUNK
    assert per_w * nc * ns == n_rows and per_w % SC_CHUNK == 0
    mesh = plsc.VectorSubcoreMesh(core_axis_name="c", subcore_axis_name="s")

    @functools.partial(
        pl.kernel, mesh=mesh,
        out_type=jax.ShapeDtypeStruct((n_rows, PACKED), U32),
        scratch_types=[pltpu.VMEM((per_w,), jnp.int32),
                       pltpu.VMEM((SC_BUFS, SC_CHUNK, PACKED), U32),
                       pltpu.SemaphoreType.DMA((SC_BUFS,)),
                       pltpu.SemaphoreType.DMA((SC_BUFS,))])
    def gather(table_hbm, idx_hbm, out_hbm, idx_v, rows_v, gsem, wsem):
        wid = lax.axis_index("s") * nc + lax.axis_index("c")
        base = pl.multiple_of(wid * per_w, per_w)
        pltpu.sync_copy(idx_hbm.at[pl.ds(base, per_w)], idx_v)

        def read(c):
            b = c % SC_BUFS
            return [pltpu.make_async_copy(table_hbm.at[idx_v.at[pl.ds(c * SC_CHUNK, SC_CHUNK)]], rows_v.at[b], gsem.at[b])]

        def write(c):
            b = c % SC_BUFS
            return [pltpu.make_async_copy(rows_v.at[b], out_hbm.at[pl.ds(base + c * SC_CHUNK, SC_CHUNK)], wsem.at[b])]
        _sc_ring(n_ch, read, write)

    return gather(yb, dest.reshape(n_rows)).reshape(nt, TOP_K, tm, PACKED)


def _padfill_kernel(cnt_ref, pstart, xs_in, xs_ref, zeros, zsem):
    del xs_in

    def pad_copies(fn):
        for e in range(N_EXPERTS):
            cnt = cnt_ref[0, e]
            head = (-cnt) & (SUBLANES - 1)
            rest = ((-cnt) & (MOE_BLOCK - 1)) - head
            off = pstart[e] + cnt
            for k in range(SUBLANES - 1):
                @pl.when(k < head)
                def _(off=off, k=k):
                    fn(pltpu.make_async_copy(zeros.at[pl.ds(0, 1), :], xs_ref.at[pl.ds(off + k, 1), :], zsem))
            off = off + head
            for bit in PAD_BITS:
                @pl.when((rest & bit) != 0)
                def _(off=off, bit=bit):
                    fn(pltpu.make_async_copy(zeros.at[pl.ds(0, bit), :],
                                             xs_ref.at[pl.ds(pl.multiple_of(off, SUBLANES), bit), :], zsem))
                off = off + (rest & bit)

    zeros[...] = jnp.zeros_like(zeros)
    pad_copies(lambda cp: cp.start())
    pad_copies(lambda cp: cp.wait())


def _padfill(xs, pstart, cnt):
    smem = pl.BlockSpec(memory_space=pltpu.SMEM)
    hbm = pl.BlockSpec(memory_space=pl.ANY)
    return pl.pallas_call(
        _padfill_kernel,
        in_specs=[smem, smem, hbm],
        out_specs=hbm,
        out_shape=jax.ShapeDtypeStruct(xs.shape, xs.dtype),
        input_output_aliases={2: 0},
        scratch_shapes=[pltpu.VMEM((MOE_BLOCK // 2, PACKED), U32), pltpu.SemaphoreType.DMA(())],
        name="padfill",
    )(cnt, pstart, xs)


W_SLOTS = 3
W_AHEAD = W_SLOTS - 1
EXPERT_GROUP = 16
EXPERT_RUNS = (1, 2, 4, 8)


def _expert_kernel(be_ref, nu_ref, ge_ref, x_ref, wg_hbm, wu_hbm, wd_hbm, o_ref,
                   wgf, wuf, wdf, grp_ref, sems):
    step = pl.program_id(0)

    def weight_copies(e, slot):
        return (pltpu.make_async_copy(wg_hbm.at[e], wgf.at[slot], sems.at[slot, 0]),
                pltpu.make_async_copy(wu_hbm.at[e], wuf.at[slot], sems.at[slot, 1]),
                pltpu.make_async_copy(wd_hbm.at[e], wdf.at[slot], sems.at[slot, 2]))

    @pl.when(step == 0)
    def _():
        grp_ref[0] = 0
        for a in range(W_AHEAD):
            @pl.when(ge_ref[a] >= 0)
            def _(a=a):
                for cp in weight_copies(ge_ref[a], a):
                    cp.start()

    n_blocks = be_ref.shape[0]
    n_used = nu_ref[0]

    def swiglu(s, n, slot):
        rows = pl.ds(pl.multiple_of(s * MOE_BLOCK, MOE_BLOCK), n * MOE_BLOCK)
        lo, hi = _unpack_rows(x_ref[rows, :])
        lo = lo.astype(BF16)
        hi = hi.astype(BF16)
        g = (jnp.dot(lo, wgf[slot, :PACKED, :], preferred_element_type=F32)
             + jnp.dot(hi, wgf[slot, PACKED:, :], preferred_element_type=F32))
        u = (jnp.dot(lo, wuf[slot, :PACKED, :], preferred_element_type=F32)
             + jnp.dot(hi, wuf[slot, PACKED:, :], preferred_element_type=F32))
        h = (g * _sigmoid(g) * u).astype(BF16)
        o_ref[rows, :] = _pack_rows(jnp.dot(h, wdf[slot], preferred_element_type=F32))

    def run(s):
        j = step * EXPERT_GROUP + s
        e = be_ref[j]
        first = jnp.logical_or(j == 0, e != be_ref[jnp.maximum(j - 1, 0)])

        @pl.when(first)
        def _():
            grp = grp_ref[0]
            slot = grp % W_SLOTS
            for cp in weight_copies(e, slot):
                cp.wait()
            nxt = ge_ref[grp + W_AHEAD]

            @pl.when(nxt >= 0)
            def _():
                for cp in weight_copies(nxt, (grp + W_AHEAD) % W_SLOTS):
                    cp.start()
            grp_ref[0] = grp + 1

        def same(k):
            return (s + k < EXPERT_GROUP) & (j + k < n_used) & (be_ref[jnp.minimum(j + k, n_blocks - 1)] == e)
        take = jnp.int32(1)
        for n in EXPERT_RUNS[1:]:
            ok = same(n - 1)
            for k in range(1, n - 1):
                ok = ok & same(k)
            take = jnp.where(ok, n, take)
        slot = (grp_ref[0] + W_SLOTS - 1) % W_SLOTS
        for n in EXPERT_RUNS:
            @pl.when(take == n)
            def _(n=n):
                swiglu(s, n, slot)
        return s + take

    lax.while_loop(lambda s: (s < EXPERT_GROUP) & (step * EXPERT_GROUP + s < n_used), run, jnp.int32(0))


def _experts(xs, block_expert, n_used, group_expert, w_gate, w_up, w_down):
    cap = xs.shape[0]
    n_blocks = cap // MOE_BLOCK
    assert n_blocks % EXPERT_GROUP == 0
    rows = EXPERT_GROUP * MOE_BLOCK
    last = lambda j, be, nu, ge: jnp.minimum(j, (nu[0] - 1) // EXPERT_GROUP)
    hbm = pl.BlockSpec(memory_space=pl.ANY)
    gs = pltpu.PrefetchScalarGridSpec(
        num_scalar_prefetch=3,
        grid=(n_blocks // EXPERT_GROUP,),
        in_specs=[pl.BlockSpec((rows, PACKED), lambda j, be, nu, ge: (last(j, be, nu, ge), 0)), hbm, hbm, hbm],
        out_specs=pl.BlockSpec((rows, PACKED), lambda j, be, nu, ge: (last(j, be, nu, ge), 0)),
        scratch_shapes=[pltpu.VMEM((W_SLOTS, D_MODEL, D_EXPERT), F32),
                        pltpu.VMEM((W_SLOTS, D_MODEL, D_EXPERT), F32),
                        pltpu.VMEM((W_SLOTS, D_EXPERT, D_MODEL), F32),
                        pltpu.SMEM((1,), jnp.int32),
                        pltpu.SemaphoreType.DMA((W_SLOTS, 3))],
    )
    return pl.pallas_call(
        _expert_kernel,
        grid_spec=gs,
        out_shape=jax.ShapeDtypeStruct((cap, PACKED), U32),
        compiler_params=_cparams(1),
        name="experts",
    )(block_expert, n_used, group_expert, xs, w_gate, w_up, w_down)


CB_TM = RT_TM


def _combine_kernel(x1_ref, gt_ref, y2_ref, o_ref):
    g = gt_ref[...]
    lo1, hi1 = _unpack_rows(y2_ref[0, 0])
    lo2, hi2 = _unpack_rows(y2_ref[0, 1])
    o_ref[:, :PACKED] = x1_ref[:, :PACKED] + g[:, 0:1] * lo1 + g[:, 1:2] * lo2
    o_ref[:, PACKED:] = x1_ref[:, PACKED:] + g[:, 0:1] * hi1 + g[:, 1:2] * hi2


def _combine(x1, gates, y2):
    T = x1.shape[0]
    nt = T // CB_TM
    return pl.pallas_call(
        _combine_kernel,
        grid=(nt,),
        in_specs=[pl.BlockSpec((CB_TM, D_MODEL), lambda i: (i, 0)),
                  pl.BlockSpec((CB_TM, RINFO), lambda i: (i, 0)),
                  pl.BlockSpec((1, TOP_K, CB_TM, PACKED), lambda i: (i, 0, 0, 0))],
        out_specs=pl.BlockSpec((CB_TM, D_MODEL), lambda i: (i, 0)),
        out_shape=jax.ShapeDtypeStruct((T, D_MODEL), F32),
        compiler_params=_cparams(1),
        name="combine",
    )(x1, gates, y2)


def _layer(x, rel_bias, ln1, w_in, q_norm, k_norm, attn_sink, conv_w, conv_b, lru_wa, lru_ba, lru_wi, lru_bi,
           lru_lambda, out_norm_attn, out_norm_lru, w_out, ln2, w_group, b_group, w_er, b_er, w_gate, w_up, w_down):
    B, S, D = x.shape
    T = B * S
    x2 = x.reshape(T, D)
    q, kv, xr, gr = _in_proj(x2, ln1, w_in, q_norm, k_norm)
    attn_n = _attention(q.reshape(B, S, ATTN_WIDTH), kv.reshape(B, S, 2 * KV_WIDTH), rel_bias, attn_sink,
                        out_norm_attn)
    lru_n = _rglru(xr.reshape(B, S, LRU_WIDTH), gr.reshape(B, S, LRU_WIDTH), conv_w, conv_b,
                   lru_wa, lru_ba, lru_wi, lru_bi, lru_lambda, out_norm_lru)
    x1, h2, gates, ei, cnt = _out_route(attn_n.reshape(T, ATTN_WIDTH), lru_n.reshape(T, LRU_WIDTH), x2, w_out, ln2,
                                        w_group, b_group, w_er, b_er)
    cap = _moe_cap(T)
    dest, pstart, block_expert, n_used, group_expert = _layout(ei, cnt, cap // MOE_BLOCK)
    xs = _padfill(_sc_dispatch(h2, dest, cap), pstart, cnt)
    yb = _experts(xs, block_expert, n_used, group_expert, w_gate, w_up, w_down)
    out = _combine(x1, gates, _sc_gather(yb, dest))
    return out.reshape(B, S, D)


def kernel(x, rel_bias, ln1, w_in, q_norm, k_norm, attn_sink, conv_w, conv_b, lru_wa, lru_ba, lru_wi, lru_bi,
           lru_lambda, out_norm_attn, out_norm_lru, w_out, ln2, w_group, b_group, w_expert_router, b_expert_router,
           w_gate, w_up, w_down):
    depth = ln1.shape[0]
    for l in range(depth):
        x = _layer(x, rel_bias, ln1[l], w_in[l], q_norm[l], k_norm[l], attn_sink[l], conv_w[l], conv_b[l],
                   lru_wa[l], lru_ba[l], lru_wi[l], lru_bi[l], lru_lambda[l], out_norm_attn[l], out_norm_lru[l],
                   w_out[l], ln2[l], w_group[l], b_group[l], w_expert_router[l], b_expert_router[l],
                   w_gate[l], w_up[l], w_down[l])
    return x
```

```python
import functools
import math

import jax
import jax.numpy as jnp
from jax import lax
from jax.experimental import pallas as pl
from jax.experimental.pallas import tpu as pltpu
from jax.experimental.pallas import tpu_sc as plsc

D_MODEL = 1024
N_HEADS = 8
N_KV_HEADS = 2
HEAD_DIM = 64
Q_PER_KV = N_HEADS // N_KV_HEADS
ATTN_WIDTH = N_HEADS * HEAD_DIM
KV_WIDTH = N_KV_HEADS * HEAD_DIM
WINDOW = 128
BLOCK = 128
NUM_BUCKETS = 32
MAX_DISTANCE = 128
LRU_WIDTH = D_MODEL - ATTN_WIDTH
LRU_BLOCKS = 8
LRU_BLOCK_DIM = LRU_WIDTH // LRU_BLOCKS
LRU_C = 8.0
CONV_W = 4
CONV_LEFT = 2
N_GROUPS = 4
EXPERTS_PER_GROUP = 8
N_EXPERTS = N_GROUPS * EXPERTS_PER_GROUP
TOP_K = 2
D_EXPERT = 512
MOE_BLOCK = 128
EPS = 1e-6
NEG_INF = -1e30

LANES = 128
SUBLANES = 8
VMEM_LIMIT = 56 * 1024 * 1024
LRU_VMEM_LIMIT = 62 * 1024 * 1024

F32 = jnp.float32
BF16 = jnp.bfloat16
LOG2E = math.log2(math.e)


def _cparams(n_axes, vmem=VMEM_LIMIT):
    return pltpu.CompilerParams(dimension_semantics=("arbitrary",) * n_axes, vmem_limit_bytes=vmem)


def _rms(x, gain):
    return x * lax.rsqrt(jnp.mean(x * x, axis=-1, keepdims=True) + EPS) * gain


U32 = jnp.uint32
HI_MASK = 0xFFFF0000
PACKED = D_MODEL // 2


def _pack_rows(x):
    return _pack_rounded(x.astype(BF16).astype(F32))


def _pack_rounded(xb):
    h = xb.shape[1] // 2
    lo = lax.bitcast_convert_type(xb[:, :h], U32) >> 16
    hi = lax.bitcast_convert_type(xb[:, h:], U32) & jnp.uint32(HI_MASK)
    return lo | hi


def _unpack_rows(p):
    lo = lax.bitcast_convert_type(p << 16, F32)
    hi = lax.bitcast_convert_type(p & jnp.uint32(HI_MASK), F32)
    return lo, hi


IN_TM = 1024


def _head_rms(x, n_heads, gain):
    head = lax.broadcasted_iota(jnp.int32, (1, n_heads * HEAD_DIM), 1) // HEAD_DIM
    x2 = x * x
    scale = jnp.zeros_like(x)
    for h in range(n_heads):
        ms = jnp.sum(jnp.where(head == h, x2, 0.0), axis=-1, keepdims=True) * (1.0 / HEAD_DIM)
        scale = jnp.where(head == h, lax.rsqrt(ms + EPS), scale)
    return x * scale * gain


def _in_proj_kernel(x_ref, g_ref, w_ref, qn_ref, kn_ref, q_ref, kv_ref, xr_ref, gr_ref, wb_ref, qg_ref, kg_ref):
    @pl.when(pl.program_id(0) == 0)
    def _():
        wb_ref[...] = w_ref[...].astype(BF16)
        qg_ref[...] = jnp.concatenate([qn_ref[...]] * N_HEADS, axis=1) * (HEAD_DIM ** -0.5 * LOG2E)
        kg_ref[...] = jnp.concatenate([kn_ref[...]] * N_KV_HEADS, axis=1)

    h = _rms(x_ref[...], g_ref[...]).astype(BF16)
    c_k = ATTN_WIDTH
    c_v = c_k + KV_WIDTH
    c_x = c_v + KV_WIDTH
    c_g = c_x + LRU_WIDTH
    q = jnp.dot(h, wb_ref[:, :c_k], preferred_element_type=F32)
    q_ref[...] = _head_rms(q, N_HEADS, qg_ref[...]).astype(BF16)
    k = jnp.dot(h, wb_ref[:, c_k:c_v], preferred_element_type=F32)
    kv_ref[:, :KV_WIDTH] = _head_rms(k, N_KV_HEADS, kg_ref[...]).astype(BF16)
    kv_ref[:, KV_WIDTH:] = jnp.dot(h, wb_ref[:, c_v:c_x], preferred_element_type=F32).astype(BF16)
    xr_ref[...] = jnp.dot(h, wb_ref[:, c_x:c_g], preferred_element_type=F32)
    gr_ref[...] = jnp.dot(h, wb_ref[:, c_g:], preferred_element_type=F32)


def _in_proj(x2, ln1, w_in, q_gain, k_gain):
    T = x2.shape[0]
    n_in = w_in.shape[1]
    row = lambda w: pl.BlockSpec((IN_TM, w), lambda i: (i, 0))
    return pl.pallas_call(
        _in_proj_kernel,
        grid=(T // IN_TM,),
        in_specs=[row(D_MODEL),
                  pl.BlockSpec((1, D_MODEL), lambda i: (0, 0)),
                  pl.BlockSpec((D_MODEL, n_in), lambda i: (0, 0)),
                  pl.BlockSpec((1, HEAD_DIM), lambda i: (0, 0)),
                  pl.BlockSpec((1, HEAD_DIM), lambda i: (0, 0))],
        out_specs=[row(ATTN_WIDTH), row(2 * KV_WIDTH), row(LRU_WIDTH), row(LRU_WIDTH)],
        out_shape=[jax.ShapeDtypeStruct((T, ATTN_WIDTH), BF16),
                   jax.ShapeDtypeStruct((T, 2 * KV_WIDTH), BF16),
                   jax.ShapeDtypeStruct((T, LRU_WIDTH), F32),
                   jax.ShapeDtypeStruct((T, LRU_WIDTH), F32)],
        scratch_shapes=[pltpu.VMEM((D_MODEL, n_in), BF16),
                        pltpu.VMEM((1, ATTN_WIDTH), F32),
                        pltpu.VMEM((1, KV_WIDTH), F32)],
        compiler_params=_cparams(1),
        name="in_proj",
    )(x2, ln1.reshape(1, D_MODEL), w_in, q_gain.reshape(1, HEAD_DIM).astype(F32),
      k_gain.reshape(1, HEAD_DIM).astype(F32))


def _t5_bucket(rel):
    half = NUM_BUCKETS // 2
    max_exact = half // 2
    base = jnp.where(rel > 0, half, 0)
    n = jnp.abs(rel)
    nf = jnp.maximum(n, 1).astype(jnp.float32)
    large = max_exact + (jnp.log(nf / max_exact) / math.log(MAX_DISTANCE / max_exact)
                         * (half - max_exact)).astype(jnp.int32)
    large = jnp.minimum(large, half - 1)
    return base + jnp.where(n < max_exact, n, large)


HEAD_PAIRS = Q_PER_KV // 2
EDGE_VARIANTS = 3


def _fill_bias_table(rb_ref, bucket_ref, band_ref, o_ref):
    bucket = bucket_ref[...]
    band = band_ref[...] > 0
    col = lax.broadcasted_iota(jnp.int32, bucket.shape, 1)
    valid = (band & (col >= BLOCK), band, band & (col < 2 * BLOCK))
    for h in range(N_HEADS):
        acc = jnp.zeros(bucket.shape, F32)
        for b in range(NUM_BUCKETS):
            acc = jnp.where(bucket == b, rb_ref[b, h], acc)
        kv, g = divmod(h, Q_PER_KV)
        pair, parity = divmod(g, 2)
        for var in range(EDGE_VARIANTS):
            o_ref[var, kv, parity, pair * BLOCK:(pair + 1) * BLOCK, :] = jnp.where(valid[var], acc * LOG2E, NEG_INF)


def _attn_kernel(sink_ref, rb_ref, q_ref, kp_ref, kc_ref, kn_ref, bucket_ref, band_ref, og_ref, o_ref, bias_ref):
    n = pl.program_id(1)

    @pl.when((pl.program_id(0) == 0) & (n == 0))
    def _():
        _fill_bias_table(rb_ref, bucket_ref, band_ref, bias_ref)

    kv_all = jnp.concatenate([kp_ref[0], kc_ref[0], kn_ref[0]], axis=0)
    for qb in range(ATTN_QB):
        variant = 1
        if qb == 0:
            variant = jnp.where(n == 0, 0, 1)
        if qb == ATTN_QB - 1:
            variant = jnp.where(n == pl.num_programs(1) - 1, 2, variant)
        out = _attn_block(q_ref[0, qb * BLOCK:(qb + 1) * BLOCK, :], kv_all[qb * BLOCK:(qb + 3) * BLOCK, :],
                          lambda kv, parity: bias_ref[variant, kv, parity], sink_ref)
        o_ref[0, qb * BLOCK:(qb + 1) * BLOCK, :] = _rms(out, og_ref[...]).astype(o_ref.dtype)


def _attn_block(q, kvw, bias, sink_ref):
    low = lax.broadcasted_iota(jnp.int32, (1, LANES), 1) < HEAD_DIM
    swap = lambda slab: pltpu.roll(slab.astype(F32), HEAD_DIM, 1).astype(BF16)
    kslab, vslab = kvw[:, :KV_WIDTH], kvw[:, KV_WIDTH:]
    kslab_sw, vslab_sw = swap(kslab), swap(vslab)
    rowi = lax.broadcasted_iota(jnp.int32, (HEAD_PAIRS * BLOCK, 1), 0)
    combos = [(kv, parity) for kv in range(N_KV_HEADS) for parity in range(2)]
    scores, vzs, sinks = [], [], []
    for kv, parity in combos:
        ks, vs = (kslab, vslab) if (kv == 0) == (parity == 0) else (kslab_sw, vslab_sw)
        keep = low if parity == 0 else jnp.logical_not(low)
        kz = jnp.where(keep, ks, jnp.zeros_like(ks))
        vzs.append(jnp.where(keep, vs, jnp.zeros_like(vs)))
        base = kv * Q_PER_KV * HEAD_DIM
        qpair = jnp.concatenate([q[:, base + j * LANES:base + (j + 1) * LANES] for j in range(HEAD_PAIRS)], axis=0)
        s = lax.dot_general(qpair, kz, (((1,), (1,)), ((), ())), preferred_element_type=F32)
        scores.append(s + bias(kv, parity))
        sink = jnp.zeros((HEAD_PAIRS * BLOCK, 1), F32)
        for j in range(HEAD_PAIRS):
            sink = jnp.where(rowi // BLOCK == j, sink_ref[kv * Q_PER_KV + 2 * j + parity], sink)
        sinks.append(sink)
    probs, inv = [], []
    for s, sink in zip(scores, sinks):
        m = jnp.maximum(jnp.max(s, axis=-1, keepdims=True), sink)
        p = jnp.exp2(s - m)
        inv.append(1.0 / (jnp.sum(p, axis=-1, keepdims=True) + jnp.exp2(sink - m)))
        probs.append(p.astype(BF16))
    outs = [jnp.dot(p, vz, preferred_element_type=F32) * r for p, vz, r in zip(probs, vzs, inv)]
    cols = []
    for kv in range(N_KV_HEADS):
        acc = outs[2 * kv] + outs[2 * kv + 1]
        cols += [acc[j * BLOCK:(j + 1) * BLOCK, :] for j in range(HEAD_PAIRS)]
    return jnp.concatenate(cols, axis=1)


ATTN_QB = 8


def _attention(q, kv, rel_bias, sink, out_gain):
    B, S, _ = q.shape
    nb = S // BLOCK
    assert ATTN_QB >= 2 and nb % ATTN_QB == 0, "a step's first and last query blocks must be distinct"
    ns = nb // ATTN_QB
    rows = ATTN_QB * BLOCK
    qi = jnp.arange(BLOCK, dtype=jnp.int32)
    kj = jnp.arange(3 * BLOCK, dtype=jnp.int32)
    rel = kj[None, :] - BLOCK - qi[:, None]
    bucket = _t5_bucket(rel).astype(jnp.int32)
    band = (jnp.abs(rel) <= WINDOW).astype(jnp.int32)
    kvspec = lambda f: pl.BlockSpec((1, BLOCK, 2 * KV_WIDTH), f)
    smem = pl.BlockSpec(memory_space=pltpu.SMEM)
    geom = pl.BlockSpec((BLOCK, 3 * BLOCK), lambda b, n: (0, 0))
    return pl.pallas_call(
        _attn_kernel,
        grid=(B, ns),
        in_specs=[smem, smem,
                  pl.BlockSpec((1, rows, ATTN_WIDTH), lambda b, n: (b, n, 0)),
                  kvspec(lambda b, n: (b, jnp.maximum(n * ATTN_QB - 1, 0), 0)),
                  pl.BlockSpec((1, rows, 2 * KV_WIDTH), lambda b, n: (b, n, 0)),
                  kvspec(lambda b, n: (b, jnp.minimum((n + 1) * ATTN_QB, nb - 1), 0)),
                  geom, geom,
                  pl.BlockSpec((1, ATTN_WIDTH), lambda b, n: (0, 0))],
        out_specs=pl.BlockSpec((1, rows, ATTN_WIDTH), lambda b, n: (b, n, 0)),
        out_shape=jax.ShapeDtypeStruct((B, S, ATTN_WIDTH), BF16),
        scratch_shapes=[pltpu.VMEM((EDGE_VARIANTS, N_KV_HEADS, 2, HEAD_PAIRS * BLOCK, 3 * BLOCK), F32)],
        compiler_params=_cparams(2),
        name="attention",
    )(sink.astype(F32) * LOG2E, rel_bias.astype(F32), q, kv, kv, kv, bucket, band, out_gain.reshape(1, ATTN_WIDTH))


LRU_TC = 128
LRU_PITCH = LRU_TC + SUBLANES // 2
LRU_SLABS = LRU_WIDTH // LANES
LRU_UNROLL = 32
HALO = SUBLANES


def _softplus(x):
    return jnp.maximum(x, 0.0) + jnp.log(1.0 + jnp.exp(-jnp.abs(x)))


def _gelu_tanh(x):
    k = math.sqrt(2.0 / math.pi)
    hx = 0.5 * x
    return hx + hx * jnp.tanh(x * (k + (k * 0.044715) * (x * x)))


def _sigmoid(x):
    return 0.5 + 0.5 * jnp.tanh(0.5 * x)


def _rglru_kernel(xr_ref, xp_ref, xn_ref, gr_ref, cw_ref, cb_ref, wa_ref, wi_ref, ba_ref, bi_ref, lam_ref, og_ref,
                  o_ref, sx_ref, a_ref, u_ref, h_ref, carry_ref, hf_ref, xcs_ref, wg_ref, bg_ref, k_ref):
    p = pl.program_id(0)
    i = pl.program_id(1)
    nc = pl.num_programs(1)
    c = i + p * (nc - 1 - 2 * i)
    B = xr_ref.shape[0]
    TC = LRU_TC

    @pl.when(i == 0)
    def _():
        carry_ref[...] = jnp.zeros_like(carry_ref)
        wg_ref[...] = jnp.zeros_like(wg_ref)
        for sel, w_ref in enumerate((wa_ref, wi_ref)):
            for h in range(LRU_BLOCKS):
                lo = h * LRU_BLOCK_DIM
                wg_ref[lo:lo + LRU_BLOCK_DIM, sel * LRU_WIDTH + lo:sel * LRU_WIDTH + lo + LRU_BLOCK_DIM] = (
                    0.5 * w_ref[0, h]).astype(BF16)
        row = pl.ds(p, 1)
        bg_ref[:, :LRU_WIDTH] = 0.5 * ba_ref[row, :]
        bg_ref[:, LRU_WIDTH:] = 0.5 * bi_ref[row, :]
        k_ref[...] = (-0.5 * LRU_C * math.log2(math.e)) * _softplus(-lam_ref[row, :])

    def gates_and_scan(xc2, backward):
        g = jnp.dot(xc2.astype(BF16), wg_ref[...], preferred_element_type=F32) + bg_ref[...]
        ta = jnp.tanh(g[:, :LRU_WIDTH])
        ig = 0.5 + 0.5 * jnp.tanh(g[:, LRU_WIDTH:])
        a = jnp.exp2((1.0 + ta) * k_ref[...])
        z = 1.0 - a * a
        u = z * lax.rsqrt(jnp.maximum(z, 1e-30)) * ig * xc2
        for b in range(B):
            for s in range(LRU_SLABS):
                a_ref[s, b * LRU_PITCH:b * LRU_PITCH + TC, :] = a[b * TC:(b + 1) * TC, s * LANES:(s + 1) * LANES]
                u_ref[s, b * LRU_PITCH:b * LRU_PITCH + TC, :] = u[b * TC:(b + 1) * TC, s * LANES:(s + 1) * LANES]

        def trip(i, hs):
            t0 = pl.multiple_of((TC // LRU_UNROLL - 1 - i if backward else i) * LRU_UNROLL, LRU_UNROLL)
            for j in range(LRU_UNROLL):
                t = t0 + (LRU_UNROLL - 1 - j if backward else j)
                out = []
                for s in range(LRU_SLABS):
                    idx = pl.ds(t, B, stride=LRU_PITCH)
                    hn = a_ref[s, idx, :] * hs[s] + u_ref[s, idx, :]
                    h_ref[s, idx, :] = hn
                    out.append(hn)
                hs = tuple(out)
            return hs

        hs = lax.fori_loop(0, TC // LRU_UNROLL, trip, tuple(carry_ref[s] for s in range(LRU_SLABS)))
        for s in range(LRU_SLABS):
            carry_ref[s] = hs[s]

    @pl.when(p == 0)
    def _():
        sx_ref[:, HALO:HALO + TC, :] = xr_ref[...]
        sx_ref[:, 0:HALO, :] = jnp.where(c > 0, xp_ref[...], 0.0)
        sx_ref[:, HALO + TC:, :] = jnp.where(c < nc - 1, xn_ref[...], 0.0)
        xc = cb_ref[...][None]
        for j in range(CONV_W):
            off = HALO + j - CONV_LEFT
            xc = xc + cw_ref[j:j + 1, :][None] * sx_ref[:, off:off + TC, :]
        xc2 = xc.reshape(B * TC, LRU_WIDTH)
        xcs_ref[c] = xc2.astype(xcs_ref.dtype)
        gates_and_scan(xc2, backward=False)
        for b in range(B):
            for s in range(LRU_SLABS):
                hf_ref[c, s, b * TC:(b + 1) * TC, :] = h_ref[s, b * LRU_PITCH:b * LRU_PITCH + TC, :].astype(hf_ref.dtype)

    @pl.when(p == 1)
    def _():
        gates_and_scan(xcs_ref[c].astype(F32), backward=True)
        for b in range(B):
            hsum = jnp.concatenate(
                [h_ref[s, b * LRU_PITCH:b * LRU_PITCH + TC, :] + hf_ref[c, s, b * TC:(b + 1) * TC, :].astype(F32)
                 for s in range(LRU_SLABS)], axis=1)
            y = hsum * _gelu_tanh(gr_ref[b])
            o_ref[b] = _rms(y, og_ref[...]).astype(o_ref.dtype)


def _rglru(xr, gr, conv_w, conv_b, w_a, b_a, w_i, b_i, lam, out_gain):
    B, S, W = xr.shape
    nc = S // LRU_TC
    hb = LRU_TC // HALO
    fwd = lambda p, i: jnp.where(p == 0, i, nc - 1)
    bwd = lambda p, i: nc - 1 - p * i
    full2 = lambda shape: pl.BlockSpec(shape, lambda p, i: (0,) * len(shape))
    wblock = pl.BlockSpec((1, LRU_BLOCKS, LRU_BLOCK_DIM, LRU_BLOCK_DIM), lambda p, i: (p, 0, 0, 0))
    return pl.pallas_call(
        _rglru_kernel,
        grid=(2, nc),
        in_specs=[pl.BlockSpec((B, LRU_TC, W), lambda p, i: (0, fwd(p, i), 0)),
                  pl.BlockSpec((B, HALO, W), lambda p, i: (0, jnp.maximum(fwd(p, i) * hb - 1, 0), 0)),
                  pl.BlockSpec((B, HALO, W), lambda p, i: (0, jnp.minimum((fwd(p, i) + 1) * hb, S // HALO - 1), 0)),
                  pl.BlockSpec((B, LRU_TC, W), lambda p, i: (0, bwd(p, i), 0)),
                  full2((CONV_W, W)),
                  full2((1, W)),
                  wblock, wblock,
                  full2((2, W)), full2((2, W)), full2((2, W)),
                  full2((1, W))],
        out_specs=pl.BlockSpec((B, LRU_TC, W), lambda p, i: (0, bwd(p, i), 0)),
        out_shape=jax.ShapeDtypeStruct((B, S, W), BF16),
        scratch_shapes=[pltpu.VMEM((B, LRU_TC + 2 * HALO, W), F32),
                        pltpu.VMEM((LRU_SLABS, B * LRU_PITCH, LANES), F32),
                        pltpu.VMEM((LRU_SLABS, B * LRU_PITCH, LANES), F32),
                        pltpu.VMEM((LRU_SLABS, B * LRU_PITCH, LANES), F32),
                        pltpu.VMEM((LRU_SLABS, B, LANES), F32),
                        pltpu.VMEM((nc, LRU_SLABS, B * LRU_TC, LANES), BF16),
                        pltpu.VMEM((nc, B * LRU_TC, W), BF16),
                        pltpu.VMEM((W, 2 * W), BF16),
                        pltpu.VMEM((1, 2 * W), F32),
                        pltpu.VMEM((1, W), F32)],
        compiler_params=_cparams(2, LRU_VMEM_LIMIT),
        name="rglru",
    )(xr, xr, xr, gr, conv_w.astype(F32), conv_b.reshape(1, W).astype(F32), w_a.astype(F32), w_i.astype(F32),
      b_a.astype(F32), b_i.astype(F32), lam.astype(F32), out_gain.reshape(1, W).astype(F32))


RT_TM = 1024
RT_PARTS = 4
RT_COLS = LANES
RT_ROWS = 48
RINFO = SUBLANES


def _split_bf16(x):
    hi = x.astype(BF16)
    lo = (x - hi.astype(F32)).astype(BF16)
    return hi, lo


def _route_kernel(an_ref, ln_ref, x_ref, wo_ref, g2_ref, wr_ref, br_ref,
                  x1_ref, h2_ref, gt_ref, ei_ref, cnt_ref, wob_ref, wrb_ref, tri_ref, run_ref, runc_ref):
    @pl.when(pl.program_id(0) == 0)
    def _():
        wob_ref[...] = wo_ref[...].astype(BF16)
        hi, lo = _split_bf16(wr_ref[...])
        wrb_ref[:RT_ROWS, :] = hi
        wrb_ref[RT_ROWS:, :] = lo
        r = lax.broadcasted_iota(jnp.int32, (RT_TM, RT_TM), 0)
        cidx = lax.broadcasted_iota(jnp.int32, (RT_TM, RT_TM), 1)
        tri_ref[...] = (r < cidx).astype(BF16)
        run_ref[...] = jnp.zeros_like(run_ref)
        runc_ref[...] = jnp.zeros_like(runc_ref)

    nt_dims = (((1,), (1,)), ((), ()))
    part = RT_TM // RT_PARTS
    x1s = []
    for r in range(RT_PARTS):
        rows = slice(r * part, (r + 1) * part)
        x1 = (x_ref[rows, :]
              + jnp.dot(an_ref[rows, :], wob_ref[:ATTN_WIDTH, :], preferred_element_type=F32)
              + jnp.dot(ln_ref[rows, :], wob_ref[ATTN_WIDTH:, :], preferred_element_type=F32))
        x1_ref[rows, :] = x1
        x1s.append(x1)
    splits = []
    for r, x1 in enumerate(x1s):
        h2 = _rms(x1, g2_ref[...])
        hi = h2.astype(BF16)
        hi_f = hi.astype(F32)
        h2_ref[r * part:(r + 1) * part, :] = _pack_rounded(hi_f)
        splits.append((hi, (h2 - hi_f).astype(BF16)))
    logits = []
    for hi, lo in splits:
        t1 = lax.dot_general(wrb_ref[...], hi, nt_dims, preferred_element_type=F32)
        t2 = lax.dot_general(wrb_ref[:RT_ROWS, :], lo, nt_dims, preferred_element_type=F32)
        logits.append(t1[:RT_ROWS] + t1[RT_ROWS:] + t2)
    logit = jnp.concatenate(logits, axis=1) + br_ref[...]

    sub = lax.broadcasted_iota(jnp.int32, (SUBLANES, RT_TM), 0)
    first_min = lambda hit: jnp.min(jnp.where(hit, sub, SUBLANES), axis=0, keepdims=True)
    is_g = sub < N_GROUPS
    gl = jnp.where(is_g, logit[:SUBLANES], -jnp.inf)
    gm = jnp.max(gl, axis=0, keepdims=True)
    gidx = first_min(gl == gm)
    g_p = 1.0 / jnp.sum(jnp.where(is_g, jnp.exp(logit[:SUBLANES] - gm), 0.0), axis=0, keepdims=True)
    el = logit[SUBLANES:2 * SUBLANES]
    for g in range(1, N_GROUPS):
        el = jnp.where(gidx == g, logit[(g + 1) * SUBLANES:(g + 2) * SUBLANES], el)
    m1 = jnp.max(el, axis=0, keepdims=True)
    i1 = first_min(el == m1)
    el2 = jnp.where(sub == i1, -jnp.inf, el)
    m2 = jnp.max(el2, axis=0, keepdims=True)
    i2 = first_min(el2 == m2)
    t = jnp.exp(m2 - m1)
    gate1 = g_p / (1.0 + t)
    gate2 = g_p * t / (1.0 + t)
    e1 = gidx * EXPERTS_PER_GROUP + i1
    e2 = gidx * EXPERTS_PER_GROUP + i2

    erow = lax.broadcasted_iota(jnp.int32, (N_EXPERTS, RT_TM), 0)
    oh1 = erow == e1
    oh2 = erow == e2
    oh = (oh1 | oh2).astype(F32)
    ohb = oh.astype(BF16)
    cum = jnp.dot(ohb, tri_ref[...], preferred_element_type=F32) + runc_ref[...]
    rank1 = jnp.sum(jnp.where(oh1, cum, 0.0), axis=0, keepdims=True)
    rank2 = jnp.sum(jnp.where(oh2, cum, 0.0), axis=0, keepdims=True)
    runc_ref[...] = runc_ref[...] + jnp.sum(oh, axis=1, keepdims=True)
    tile_cnt = lax.dot_general(jnp.ones((SUBLANES, RT_TM), BF16), ohb, nt_dims, preferred_element_type=F32)
    run_ref[:, :N_EXPERTS] = run_ref[:, :N_EXPERTS] + tile_cnt[0:1]
    cnt_ref[...] = run_ref[...].astype(jnp.int32)

    rows = [e1, e2, rank1.astype(jnp.int32), rank2.astype(jnp.int32)]
    ei = jnp.zeros((RINFO, RT_TM), jnp.int32)
    for k, v in enumerate(rows):
        ei = jnp.where(sub == k, v, ei)
    ei_ref[0] = ei
    gt_ref[...] = jnp.where(sub == 0, gate1, jnp.where(sub == 1, gate2, 0.0)).T


def _out_route(attn_n, lru_n, x2, w_out, ln2, w_group, b_group, w_er, b_er):
    T = x2.shape[0]
    pad_g = SUBLANES - N_GROUPS
    wr = jnp.concatenate([jnp.pad(w_group.T, ((0, pad_g), (0, 0))),
                          jnp.transpose(w_er, (0, 2, 1)).reshape(N_EXPERTS, D_MODEL)], axis=0)
    wr = jnp.pad(wr, ((0, RT_ROWS - wr.shape[0]), (0, 0))).astype(F32)
    br = jnp.concatenate([jnp.pad(b_group, (0, pad_g)), b_er.reshape(-1)])
    br = jnp.pad(br, (0, RT_ROWS - br.shape[0])).reshape(RT_ROWS, 1).astype(F32)
    row = lambda w: pl.BlockSpec((RT_TM, w), lambda i: (i, 0))
    const = lambda shape: pl.BlockSpec(shape, lambda i: (0, 0))
    return pl.pallas_call(
        _route_kernel,
        grid=(T // RT_TM,),
        in_specs=[row(ATTN_WIDTH), row(LRU_WIDTH), row(D_MODEL), const((D_MODEL, D_MODEL)), const((1, D_MODEL)),
                  const((RT_ROWS, D_MODEL)), const((RT_ROWS, 1))],
        out_specs=[row(D_MODEL), row(PACKED), row(RINFO),
                   pl.BlockSpec((1, RINFO, RT_TM), lambda i: (i, 0, 0)), const((1, RT_COLS))],
        out_shape=[jax.ShapeDtypeStruct((T, D_MODEL), F32),
                   jax.ShapeDtypeStruct((T, PACKED), U32),
                   jax.ShapeDtypeStruct((T, RINFO), F32),
                   jax.ShapeDtypeStruct((T // RT_TM, RINFO, RT_TM), jnp.int32),
                   jax.ShapeDtypeStruct((1, RT_COLS), jnp.int32)],
        scratch_shapes=[pltpu.VMEM((D_MODEL, D_MODEL), BF16),
                        pltpu.VMEM((2 * RT_ROWS, D_MODEL), BF16),
                        pltpu.VMEM((RT_TM, RT_TM), BF16),
                        pltpu.VMEM((1, RT_COLS), F32),
                        pltpu.VMEM((N_EXPERTS, 1), F32)],
        compiler_params=_cparams(1),
        name="out_route",
    )(attn_n, lru_n, x2, w_out, ln2.reshape(1, D_MODEL).astype(F32), wr, br)


def _moe_cap(T):
    A = T * TOP_K
    return ((A + MOE_BLOCK - 1) // MOE_BLOCK) * MOE_BLOCK + N_EXPERTS * MOE_BLOCK


PAD_BITS = tuple(1 << b for b in reversed(range(3, MOE_BLOCK.bit_length() - 1)))


def _layout_kernel(cnt_ref, ei_ref, dest_ref, pstart, be_ref, nu_ref, ge_ref):
    n_blocks = be_ref.shape[0]

    def lay(e, carry):
        start, blk, grp = carry
        pstart[e] = start
        nb = (cnt_ref[0, e] + MOE_BLOCK - 1) // MOE_BLOCK
        ge_ref[grp] = e

        def fill(k, c):
            be_ref[blk + k] = e
            return c
        lax.fori_loop(0, nb, fill, 0)
        return start + nb * MOE_BLOCK, blk + nb, grp + (nb > 0).astype(jnp.int32)
    _, used, groups = lax.fori_loop(0, N_EXPERTS, lay, (jnp.int32(0), jnp.int32(0), jnp.int32(0)))
    nu_ref[0] = used

    def tail(k, c):
        be_ref[k] = N_EXPERTS - 1
        return c
    lax.fori_loop(used, n_blocks, tail, 0)

    def no_group(k, c):
        ge_ref[k] = -1
        return c
    lax.fori_loop(groups, ge_ref.shape[0], no_group, 0)

    expert = ei_ref[:, 0:TOP_K, :]
    dest = ei_ref[:, TOP_K:2 * TOP_K, :]
    for e in range(N_EXPERTS):
        dest = dest + jnp.where(expert == e, pstart[e], 0)
    dest_ref[...] = dest


def _layout(ei, cnt, n_blocks):
    nt = ei.shape[0]
    smem = pl.BlockSpec(memory_space=pltpu.SMEM)
    vmem = pl.BlockSpec(memory_space=pltpu.VMEM)
    return pl.pallas_call(
        _layout_kernel,
        in_specs=[smem, vmem],
        out_specs=[vmem, smem, smem, smem, smem],
        out_shape=[jax.ShapeDtypeStruct((nt, TOP_K, RT_TM), jnp.int32),
                   jax.ShapeDtypeStruct((N_EXPERTS,), jnp.int32),
                   jax.ShapeDtypeStruct((n_blocks,), jnp.int32),
                   jax.ShapeDtypeStruct((1,), jnp.int32),
                   jax.ShapeDtypeStruct((N_EXPERTS + W_AHEAD,), jnp.int32)],
        name="layout",
    )(cnt, ei)


SC_CHUNK = 64
SC_BUFS = 3
SC_LEAD = SC_BUFS - 1


def _sc_workers():
    info = plsc.get_sparse_core_info()
    return info.num_cores, info.num_subcores


def _sc_ring(n_chunks, read, write):
    for c in range(min(SC_LEAD, n_chunks)):
        for cp in read(c):
            cp.start()
    reclaimed = set()
    for c in range(n_chunks):
        for cp in read(c):
            cp.wait()
        for cp in write(c):
            cp.start()
        nxt = c + SC_LEAD
        if nxt < n_chunks:
            if nxt - SC_BUFS >= 0:
                for cp in write(nxt - SC_BUFS):
                    cp.wait()
                reclaimed.add(nxt - SC_BUFS)
            for cp in read(nxt):
                cp.start()
    for c in range(n_chunks):
        if c not in reclaimed:
            for cp in write(c):
                cp.wait()


def _sc_dispatch(h2p, dest, cap):
    T = h2p.shape[0]
    nc, ns = _sc_workers()
    per_w = T // (nc * ns)
    n_ch = per_w // SC_CHUNK
    nt, _, tm = dest.shape
    assert nt * tm == T and tm % per_w == 0 and per_w % SC_CHUNK == 0
    idx = dest.reshape(nt, TOP_K, tm // per_w, per_w).transpose(0, 2, 1, 3).reshape(nc * ns, TOP_K * n_ch, SC_CHUNK)
    mesh = plsc.VectorSubcoreMesh(core_axis_name="c", subcore_axis_name="s")

    @functools.partial(
        pl.kernel, mesh=mesh,
        out_type=jax.ShapeDtypeStruct((cap, PACKED), U32),
        scratch_types=[pltpu.VMEM((TOP_K * n_ch, SC_CHUNK), jnp.int32),
                       pltpu.VMEM((SC_BUFS, SC_CHUNK, PACKED), U32),
                       pltpu.SemaphoreType.DMA((SC_BUFS,)),
                       pltpu.SemaphoreType.DMA((SC_BUFS,))])
    def scatter(src_hbm, idx_hbm, out_hbm, idx_v, rows_v, rsem, wsem):
        wid = lax.axis_index("s") * nc + lax.axis_index("c")
        base = pl.multiple_of(wid * per_w, per_w)
        pltpu.sync_copy(idx_hbm.at[wid], idx_v)

        def read(c):
            b = c % SC_BUFS
            return [pltpu.make_async_copy(src_hbm.at[pl.ds(base + c * SC_CHUNK, SC_CHUNK)], rows_v.at[b], rsem.at[b])]

        def write(c):
            b = c % SC_BUFS
            return [pltpu.make_async_copy(rows_v.at[b], out_hbm.at[idx_v.at[k * n_ch + c]], wsem.at[b])
                    for k in range(TOP_K)]
        _sc_ring(n_ch, read, write)

    return scatter(h2p, idx)


def _sc_gather(yb, dest):
    nt, _, tm = dest.shape
    nc, ns = _sc_workers()
    n_rows = nt * TOP_K * tm
    per_w = n_rows // (nc * ns)
    n_ch = per_w // SC_CHUNK
    assert per_w * nc * ns == n_rows and per_w % SC_CHUNK == 0
    mesh = plsc.VectorSubcoreMesh(core_axis_name="c", subcore_axis_name="s")

    @functools.partial(
        pl.kernel, mesh=mesh,
        out_type=jax.ShapeDtypeStruct((n_rows, PACKED), U32),
        scratch_types=[pltpu.VMEM((per_w,), jnp.int32),
                       pltpu.VMEM((SC_BUFS, SC_CHUNK, PACKED), U32),
                       pltpu.SemaphoreType.DMA((SC_BUFS,)),
                       pltpu.SemaphoreType.DMA((SC_BUFS,))])
    def gather(table_hbm, idx_hbm, out_hbm, idx_v, rows_v, gsem, wsem):
        wid = lax.axis_index("s") * nc + lax.axis_index("c")
        base = pl.multiple_of(wid * per_w, per_w)
        pltpu.sync_copy(idx_hbm.at[pl.ds(base, per_w)], idx_v)

        def read(c):
            b = c % SC_BUFS
            return [pltpu.make_async_copy(table_hbm.at[idx_v.at[pl.ds(c * SC_CHUNK, SC_CHUNK)]], rows_v.at[b], gsem.at[b])]

        def write(c):
            b = c % SC_BUFS
            return [pltpu.make_async_copy(rows_v.at[b], out_hbm.at[pl.ds(base + c * SC_CHUNK, SC_CHUNK)], wsem.at[b])]
        _sc_ring(n_ch, read, write)

    return gather(yb, dest.reshape(n_rows)).reshape(nt, TOP_K, tm, PACKED)


def _padfill_kernel(cnt_ref, pstart, xs_in, xs_ref, zeros, zsem):
    del xs_in

    def pad_copies(fn):
        for e in range(N_EXPERTS):
            cnt = cnt_ref[0, e]
            head = (-cnt) & (SUBLANES - 1)
            rest = ((-cnt) & (MOE_BLOCK - 1)) - head
            off = pstart[e] + cnt
            for k in range(SUBLANES - 1):
                @pl.when(k < head)
                def _(off=off, k=k):
                    fn(pltpu.make_async_copy(zeros.at[pl.ds(0, 1), :], xs_ref.at[pl.ds(off + k, 1), :], zsem))
            off = off + head
            for bit in PAD_BITS:
                @pl.when((rest & bit) != 0)
                def _(off=off, bit=bit):
                    fn(pltpu.make_async_copy(zeros.at[pl.ds(0, bit), :],
                                             xs_ref.at[pl.ds(pl.multiple_of(off, SUBLANES), bit), :], zsem))
                off = off + (rest & bit)

    zeros[...] = jnp.zeros_like(zeros)
    pad_copies(lambda cp: cp.start())
    pad_copies(lambda cp: cp.wait())


def _padfill(xs, pstart, cnt):
    smem = pl.BlockSpec(memory_space=pltpu.SMEM)
    hbm = pl.BlockSpec(memory_space=pl.ANY)
    return pl.pallas_call(
        _padfill_kernel,
        in_specs=[smem, smem, hbm],
        out_specs=hbm,
        out_shape=jax.ShapeDtypeStruct(xs.shape, xs.dtype),
        input_output_aliases={2: 0},
        scratch_shapes=[pltpu.VMEM((MOE_BLOCK // 2, PACKED), U32), pltpu.SemaphoreType.DMA(())],
        name="padfill",
    )(cnt, pstart, xs)


W_SLOTS = 2
W_AHEAD = W_SLOTS - 1
EXPERT_GROUP = 16
EXPERT_RUNS = (1, 2, 4, 8)


def _expert_kernel(be_ref, nu_ref, ge_ref, x_ref, wg_hbm, wu_hbm, wd_hbm, o_ref,
                   wgf, wuf, wdf, grp_ref, sems):
    step = pl.program_id(0)

    def weight_copies(e, slot):
        return (pltpu.make_async_copy(wg_hbm.at[e], wgf.at[slot], sems.at[slot, 0]),
                pltpu.make_async_copy(wu_hbm.at[e], wuf.at[slot], sems.at[slot, 1]),
                pltpu.make_async_copy(wd_hbm.at[e], wdf.at[slot], sems.at[slot, 2]))

    @pl.when(step == 0)
    def _():
        grp_ref[0] = 0
        for a in range(W_AHEAD):
            @pl.when(ge_ref[a] >= 0)
            def _(a=a):
                for cp in weight_copies(ge_ref[a], a):
                    cp.start()

    n_blocks = be_ref.shape[0]
    n_used = nu_ref[0]

    def swiglu(s, n, slot):
        rows = pl.ds(pl.multiple_of(s * MOE_BLOCK, MOE_BLOCK), n * MOE_BLOCK)
        lo, hi = _unpack_rows(x_ref[rows, :])
        lo = lo.astype(BF16)
        hi = hi.astype(BF16)
        g = (jnp.dot(lo, wgf[slot, :PACKED, :], preferred_element_type=F32)
             + jnp.dot(hi, wgf[slot, PACKED:, :], preferred_element_type=F32))
        u = (jnp.dot(lo, wuf[slot, :PACKED, :], preferred_element_type=F32)
             + jnp.dot(hi, wuf[slot, PACKED:, :], preferred_element_type=F32))
        h = (g * _sigmoid(g) * u).astype(BF16)
        o_ref[rows, :] = _pack_rows(jnp.dot(h, wdf[slot], preferred_element_type=F32))

    def run(s):
        j = step * EXPERT_GROUP + s
        e = be_ref[j]
        first = jnp.logical_or(j == 0, e != be_ref[jnp.maximum(j - 1, 0)])

        @pl.when(first)
        def _():
            grp = grp_ref[0]
            slot = grp % W_SLOTS
            for cp in weight_copies(e, slot):
                cp.wait()
            nxt = ge_ref[grp + W_AHEAD]

            @pl.when(nxt >= 0)
            def _():
                for cp in weight_copies(nxt, (grp + W_AHEAD) % W_SLOTS):
                    cp.start()
            grp_ref[0] = grp + 1

        def same(k):
            return (s + k < EXPERT_GROUP) & (j + k < n_used) & (be_ref[jnp.minimum(j + k, n_blocks - 1)] == e)
        take = jnp.int32(1)
        for n in EXPERT_RUNS[1:]:
            ok = same(n - 1)
            for k in range(1, n - 1):
                ok = ok & same(k)
            take = jnp.where(ok, n, take)
        slot = (grp_ref[0] + W_SLOTS - 1) % W_SLOTS
        for n in EXPERT_RUNS:
            @pl.when(take == n)
            def _(n=n):
                swiglu(s, n, slot)
        return s + take

    lax.while_loop(lambda s: (s < EXPERT_GROUP) & (step * EXPERT_GROUP + s < n_used), run, jnp.int32(0))


def _experts(xs, block_expert, n_used, group_expert, w_gate, w_up, w_down):
    cap = xs.shape[0]
    n_blocks = cap // MOE_BLOCK
    assert n_blocks % EXPERT_GROUP == 0
    rows = EXPERT_GROUP * MOE_BLOCK
    last = lambda j, be, nu, ge: jnp.minimum(j, (nu[0] - 1) // EXPERT_GROUP)
    hbm = pl.BlockSpec(memory_space=pl.ANY)
    gs = pltpu.PrefetchScalarGridSpec(
        num_scalar_prefetch=3,
        grid=(n_blocks // EXPERT_GROUP,),
        in_specs=[pl.BlockSpec((rows, PACKED), lambda j, be, nu, ge: (last(j, be, nu, ge), 0)), hbm, hbm, hbm],
        out_specs=pl.BlockSpec((rows, PACKED), lambda j, be, nu, ge: (last(j, be, nu, ge), 0)),
        scratch_shapes=[pltpu.VMEM((W_SLOTS, D_MODEL, D_EXPERT), F32),
                        pltpu.VMEM((W_SLOTS, D_MODEL, D_EXPERT), F32),
                        pltpu.VMEM((W_SLOTS, D_EXPERT, D_MODEL), F32),
                        pltpu.SMEM((1,), jnp.int32),
                        pltpu.SemaphoreType.DMA((W_SLOTS, 3))],
    )
    return pl.pallas_call(
        _expert_kernel,
        grid_spec=gs,
        out_shape=jax.ShapeDtypeStruct((cap, PACKED), U32),
        compiler_params=_cparams(1),
        name="experts",
    )(block_expert, n_used, group_expert, xs, w_gate, w_up, w_down)


CB_TM = RT_TM


def _combine_kernel(x1_ref, gt_ref, y2_ref, o_ref):
    g = gt_ref[...]
    lo1, hi1 = _unpack_rows(y2_ref[0, 0])
    lo2, hi2 = _unpack_rows(y2_ref[0, 1])
    o_ref[:, :PACKED] = x1_ref[:, :PACKED] + g[:, 0:1] * lo1 + g[:, 1:2] * lo2
    o_ref[:, PACKED:] = x1_ref[:, PACKED:] + g[:, 0:1] * hi1 + g[:, 1:2] * hi2


def _combine(x1, gates, y2):
    T = x1.shape[0]
    nt = T // CB_TM
    return pl.pallas_call(
        _combine_kernel,
        grid=(nt,),
        in_specs=[pl.BlockSpec((CB_TM, D_MODEL), lambda i: (i, 0)),
                  pl.BlockSpec((CB_TM, RINFO), lambda i: (i, 0)),
                  pl.BlockSpec((1, TOP_K, CB_TM, PACKED), lambda i: (i, 0, 0, 0))],
        out_specs=pl.BlockSpec((CB_TM, D_MODEL), lambda i: (i, 0)),
        out_shape=jax.ShapeDtypeStruct((T, D_MODEL), F32),
        compiler_params=_cparams(1),
        name="combine",
    )(x1, gates, y2)


def _layer(x, rel_bias, ln1, w_in, q_norm, k_norm, attn_sink, conv_w, conv_b, lru_wa, lru_ba, lru_wi, lru_bi,
           lru_lambda, out_norm_attn, out_norm_lru, w_out, ln2, w_group, b_group, w_er, b_er, w_gate, w_up, w_down):
    B, S, D = x.shape
    T = B * S
    x2 = x.reshape(T, D)
    q, kv, xr, gr = _in_proj(x2, ln1, w_in, q_norm, k_norm)
    attn_n = _attention(q.reshape(B, S, ATTN_WIDTH), kv.reshape(B, S, 2 * KV_WIDTH), rel_bias, attn_sink,
                        out_norm_attn)
    lru_n = _rglru(xr.reshape(B, S, LRU_WIDTH), gr.reshape(B, S, LRU_WIDTH), conv_w, conv_b,
                   lru_wa, lru_ba, lru_wi, lru_bi, lru_lambda, out_norm_lru)
    x1, h2, gates, ei, cnt = _out_route(attn_n.reshape(T, ATTN_WIDTH), lru_n.reshape(T, LRU_WIDTH), x2, w_out, ln2,
                                        w_group, b_group, w_er, b_er)
    cap = _moe_cap(T)
    dest, pstart, block_expert, n_used, group_expert = _layout(ei, cnt, cap // MOE_BLOCK)
    xs = _padfill(_sc_dispatch(h2, dest, cap), pstart, cnt)
    yb = _experts(xs, block_expert, n_used, group_expert, w_gate, w_up, w_down)
    out = _combine(x1, gates, _sc_gather(yb, dest))
    return out.reshape(B, S, D)


def kernel(x, rel_bias, ln1, w_in, q_norm, k_norm, attn_sink, conv_w, conv_b, lru_wa, lru_ba, lru_wi, lru_bi,
           lru_lambda, out_norm_attn, out_norm_lru, w_out, ln2, w_group, b_group, w_expert_router, b_expert_router,
           w_gate, w_up, w_down):
    depth = ln1.shape[0]
    for l in range(depth):
        x = _layer(x, rel_bias, ln1[l], w_in[l], q_norm[l], k_norm[l], attn_sink[l], conv_w[l], conv_b[l],
                   lru_wa[l], lru_ba[l], lru_wi[l], lru_bi[l], lru_lambda[l], out_norm_attn[l], out_norm_lru[l],
                   w_out[l], ln2[l], w_group[l], b_group[l], w_expert_router[l], b_expert_router[l],
                   w_gate[l], w_up[l], w_down[l])
    return x
```

```python
import functools
import math

import jax
import jax.numpy as jnp
from jax import lax
from jax.experimental import pallas as pl
from jax.experimental.pallas import tpu as pltpu
from jax.experimental.pallas import tpu_sc as plsc

D_MODEL = 1024
N_HEADS = 8
N_KV_HEADS = 2
HEAD_DIM = 64
Q_PER_KV = N_HEADS // N_KV_HEADS
ATTN_WIDTH = N_HEADS * HEAD_DIM
KV_WIDTH = N_KV_HEADS * HEAD_DIM
WINDOW = 128
BLOCK = 128
NUM_BUCKETS = 32
MAX_DISTANCE = 128
LRU_WIDTH = D_MODEL - ATTN_WIDTH
LRU_BLOCKS = 8
LRU_BLOCK_DIM = LRU_WIDTH // LRU_BLOCKS
LRU_C = 8.0
CONV_W = 4
CONV_LEFT = 2
N_GROUPS = 4
EXPERTS_PER_GROUP = 8
N_EXPERTS = N_GROUPS * EXPERTS_PER_GROUP
TOP_K = 2
D_EXPERT = 512
MOE_BLOCK = 128
EPS = 1e-6
NEG_INF = -1e30

LANES = 128
SUBLANES = 8
VMEM_LIMIT = 56 * 1024 * 1024
LRU_VMEM_LIMIT = 62 * 1024 * 1024

F32 = jnp.float32
BF16 = jnp.bfloat16
LOG2E = math.log2(math.e)


def _cparams(n_axes, vmem=VMEM_LIMIT):
    return pltpu.CompilerParams(dimension_semantics=("arbitrary",) * n_axes, vmem_limit_bytes=vmem)


def _rms(x, gain):
    return x * lax.rsqrt(jnp.mean(x * x, axis=-1, keepdims=True) + EPS) * gain


U32 = jnp.uint32
HI_MASK = 0xFFFF0000
PACKED = D_MODEL // 2


def _pack_rows(x):
    return _pack_rounded(x.astype(BF16).astype(F32))


def _pack_rounded(xb):
    h = xb.shape[1] // 2
    lo = lax.bitcast_convert_type(xb[:, :h], U32) >> 16
    hi = lax.bitcast_convert_type(xb[:, h:], U32) & jnp.uint32(HI_MASK)
    return lo | hi


def _unpack_rows(p):
    lo = lax.bitcast_convert_type(p << 16, F32)
    hi = lax.bitcast_convert_type(p & jnp.uint32(HI_MASK), F32)
    return lo, hi


IN_TM = 1024


def _head_rms(x, n_heads, gain):
    head = lax.broadcasted_iota(jnp.int32, (1, n_heads * HEAD_DIM), 1) // HEAD_DIM
    x2 = x * x
    scale = jnp.zeros_like(x)
    for h in range(n_heads):
        ms = jnp.sum(jnp.where(head == h, x2, 0.0), axis=-1, keepdims=True) * (1.0 / HEAD_DIM)
        scale = jnp.where(head == h, lax.rsqrt(ms + EPS), scale)
    return x * scale * gain


def _in_proj_kernel(x_ref, g_ref, w_ref, qn_ref, kn_ref, q_ref, kv_ref, xr_ref, gr_ref, wb_ref, qg_ref, kg_ref):
    @pl.when(pl.program_id(0) == 0)
    def _():
        wb_ref[...] = w_ref[...].astype(BF16)
        qg_ref[...] = jnp.concatenate([qn_ref[...]] * N_HEADS, axis=1) * (HEAD_DIM ** -0.5 * LOG2E)
        kg_ref[...] = jnp.concatenate([kn_ref[...]] * N_KV_HEADS, axis=1)

    h = _rms(x_ref[...], g_ref[...]).astype(BF16)
    c_k = ATTN_WIDTH
    c_v = c_k + KV_WIDTH
    c_x = c_v + KV_WIDTH
    c_g = c_x + LRU_WIDTH
    q = jnp.dot(h, wb_ref[:, :c_k], preferred_element_type=F32)
    q_ref[...] = _head_rms(q, N_HEADS, qg_ref[...]).astype(BF16)
    k = jnp.dot(h, wb_ref[:, c_k:c_v], preferred_element_type=F32)
    kv_ref[:, :KV_WIDTH] = _head_rms(k, N_KV_HEADS, kg_ref[...]).astype(BF16)
    kv_ref[:, KV_WIDTH:] = jnp.dot(h, wb_ref[:, c_v:c_x], preferred_element_type=F32).astype(BF16)
    xr_ref[...] = jnp.dot(h, wb_ref[:, c_x:c_g], preferred_element_type=F32)
    gr_ref[...] = jnp.dot(h, wb_ref[:, c_g:], preferred_element_type=F32)


def _in_proj(x2, ln1, w_in, q_gain, k_gain):
    T = x2.shape[0]
    n_in = w_in.shape[1]
    row = lambda w: pl.BlockSpec((IN_TM, w), lambda i: (i, 0))
    return pl.pallas_call(
        _in_proj_kernel,
        grid=(T // IN_TM,),
        in_specs=[row(D_MODEL),
                  pl.BlockSpec((1, D_MODEL), lambda i: (0, 0)),
                  pl.BlockSpec((D_MODEL, n_in), lambda i: (0, 0)),
                  pl.BlockSpec((1, HEAD_DIM), lambda i: (0, 0)),
                  pl.BlockSpec((1, HEAD_DIM), lambda i: (0, 0))],
        out_specs=[row(ATTN_WIDTH), row(2 * KV_WIDTH), row(LRU_WIDTH), row(LRU_WIDTH)],
        out_shape=[jax.ShapeDtypeStruct((T, ATTN_WIDTH), BF16),
                   jax.ShapeDtypeStruct((T, 2 * KV_WIDTH), BF16),
                   jax.ShapeDtypeStruct((T, LRU_WIDTH), F32),
                   jax.ShapeDtypeStruct((T, LRU_WIDTH), F32)],
        scratch_shapes=[pltpu.VMEM((D_MODEL, n_in), BF16),
                        pltpu.VMEM((1, ATTN_WIDTH), F32),
                        pltpu.VMEM((1, KV_WIDTH), F32)],
        compiler_params=_cparams(1),
        name="in_proj",
    )(x2, ln1.reshape(1, D_MODEL), w_in, q_gain.reshape(1, HEAD_DIM).astype(F32),
      k_gain.reshape(1, HEAD_DIM).astype(F32))


def _t5_bucket(rel):
    half = NUM_BUCKETS // 2
    max_exact = half // 2
    base = jnp.where(rel > 0, half, 0)
    n = jnp.abs(rel)
    nf = jnp.maximum(n, 1).astype(jnp.float32)
    large = max_exact + (jnp.log(nf / max_exact) / math.log(MAX_DISTANCE / max_exact)
                         * (half - max_exact)).astype(jnp.int32)
    large = jnp.minimum(large, half - 1)
    return base + jnp.where(n < max_exact, n, large)


HEAD_PAIRS = Q_PER_KV // 2
EDGE_VARIANTS = 3


def _fill_bias_table(rb_ref, bucket_ref, band_ref, o_ref):
    bucket = bucket_ref[...]
    band = band_ref[...] > 0
    col = lax.broadcasted_iota(jnp.int32, bucket.shape, 1)
    valid = (band & (col >= BLOCK), band, band & (col < 2 * BLOCK))
    for h in range(N_HEADS):
        acc = jnp.zeros(bucket.shape, F32)
        for b in range(NUM_BUCKETS):
            acc = jnp.where(bucket == b, rb_ref[b, h], acc)
        kv, g = divmod(h, Q_PER_KV)
        pair, parity = divmod(g, 2)
        for var in range(EDGE_VARIANTS):
            o_ref[var, kv, parity, pair * BLOCK:(pair + 1) * BLOCK, :] = jnp.where(valid[var], acc * LOG2E, NEG_INF)


def _attn_kernel(sink_ref, rb_ref, q_ref, kp_ref, kc_ref, kn_ref, bucket_ref, band_ref, og_ref, o_ref, bias_ref):
    n = pl.program_id(1)

    @pl.when((pl.program_id(0) == 0) & (n == 0))
    def _():
        _fill_bias_table(rb_ref, bucket_ref, band_ref, bias_ref)

    kv_all = jnp.concatenate([kp_ref[0], kc_ref[0], kn_ref[0]], axis=0)
    for qb in range(ATTN_QB):
        variant = 1
        if qb == 0:
            variant = jnp.where(n == 0, 0, 1)
        if qb == ATTN_QB - 1:
            variant = jnp.where(n == pl.num_programs(1) - 1, 2, variant)
        out = _attn_block(q_ref[0, qb * BLOCK:(qb + 1) * BLOCK, :], kv_all[qb * BLOCK:(qb + 3) * BLOCK, :],
                          lambda kv, parity: bias_ref[variant, kv, parity], sink_ref)
        o_ref[0, qb * BLOCK:(qb + 1) * BLOCK, :] = _rms(out, og_ref[...]).astype(o_ref.dtype)


def _attn_block(q, kvw, bias, sink_ref):
    low = lax.broadcasted_iota(jnp.int32, (1, LANES), 1) < HEAD_DIM
    swap = lambda slab: pltpu.roll(slab.astype(F32), HEAD_DIM, 1).astype(BF16)
    kslab, vslab = kvw[:, :KV_WIDTH], kvw[:, KV_WIDTH:]
    kslab_sw, vslab_sw = swap(kslab), swap(vslab)
    rowi = lax.broadcasted_iota(jnp.int32, (HEAD_PAIRS * BLOCK, 1), 0)
    combos = [(kv, parity) for kv in range(N_KV_HEADS) for parity in range(2)]
    scores, vzs, sinks = [], [], []
    for kv, parity in combos:
        ks, vs = (kslab, vslab) if (kv == 0) == (parity == 0) else (kslab_sw, vslab_sw)
        keep = low if parity == 0 else jnp.logical_not(low)
        kz = jnp.where(keep, ks, jnp.zeros_like(ks))
        vzs.append(jnp.where(keep, vs, jnp.zeros_like(vs)))
        base = kv * Q_PER_KV * HEAD_DIM
        qpair = jnp.concatenate([q[:, base + j * LANES:base + (j + 1) * LANES] for j in range(HEAD_PAIRS)], axis=0)
        s = lax.dot_general(qpair, kz, (((1,), (1,)), ((), ())), preferred_element_type=F32)
        scores.append(s + bias(kv, parity))
        sink = jnp.zeros((HEAD_PAIRS * BLOCK, 1), F32)
        for j in range(HEAD_PAIRS):
            sink = jnp.where(rowi // BLOCK == j, sink_ref[kv * Q_PER_KV + 2 * j + parity], sink)
        sinks.append(sink)
    probs, inv = [], []
    for s, sink in zip(scores, sinks):
        m = jnp.maximum(jnp.max(s, axis=-1, keepdims=True), sink)
        p = jnp.exp2(s - m)
        inv.append(1.0 / (jnp.sum(p, axis=-1, keepdims=True) + jnp.exp2(sink - m)))
        probs.append(p.astype(BF16))
    outs = [jnp.dot(p, vz, preferred_element_type=F32) * r for p, vz, r in zip(probs, vzs, inv)]
    cols = []
    for kv in range(N_KV_HEADS):
        acc = outs[2 * kv] + outs[2 * kv + 1]
        cols += [acc[j * BLOCK:(j + 1) * BLOCK, :] for j in range(HEAD_PAIRS)]
    return jnp.concatenate(cols, axis=1)


ATTN_QB = 8


def _attention(q, kv, rel_bias, sink, out_gain):
    B, S, _ = q.shape
    nb = S // BLOCK
    assert ATTN_QB >= 2 and nb % ATTN_QB == 0, "a step's first and last query blocks must be distinct"
    ns = nb // ATTN_QB
    rows = ATTN_QB * BLOCK
    qi = jnp.arange(BLOCK, dtype=jnp.int32)
    kj = jnp.arange(3 * BLOCK, dtype=jnp.int32)
    rel = kj[None, :] - BLOCK - qi[:, None]
    bucket = _t5_bucket(rel).astype(jnp.int32)
    band = (jnp.abs(rel) <= WINDOW).astype(jnp.int32)
    kvspec = lambda f: pl.BlockSpec((1, BLOCK, 2 * KV_WIDTH), f)
    smem = pl.BlockSpec(memory_space=pltpu.SMEM)
    geom = pl.BlockSpec((BLOCK, 3 * BLOCK), lambda b, n: (0, 0))
    return pl.pallas_call(
        _attn_kernel,
        grid=(B, ns),
        in_specs=[smem, smem,
                  pl.BlockSpec((1, rows, ATTN_WIDTH), lambda b, n: (b, n, 0)),
                  kvspec(lambda b, n: (b, jnp.maximum(n * ATTN_QB - 1, 0), 0)),
                  pl.BlockSpec((1, rows, 2 * KV_WIDTH), lambda b, n: (b, n, 0)),
                  kvspec(lambda b, n: (b, jnp.minimum((n + 1) * ATTN_QB, nb - 1), 0)),
                  geom, geom,
                  pl.BlockSpec((1, ATTN_WIDTH), lambda b, n: (0, 0))],
        out_specs=pl.BlockSpec((1, rows, ATTN_WIDTH), lambda b, n: (b, n, 0)),
        out_shape=jax.ShapeDtypeStruct((B, S, ATTN_WIDTH), BF16),
        scratch_shapes=[pltpu.VMEM((EDGE_VARIANTS, N_KV_HEADS, 2, HEAD_PAIRS * BLOCK, 3 * BLOCK), F32)],
        compiler_params=_cparams(2),
        name="attention",
    )(sink.astype(F32) * LOG2E, rel_bias.astype(F32), q, kv, kv, kv, bucket, band, out_gain.reshape(1, ATTN_WIDTH))


LRU_TC = 128
LRU_PITCH = LRU_TC + SUBLANES // 2
LRU_SLABS = LRU_WIDTH // LANES
LRU_UNROLL = 32
HALO = SUBLANES


def _softplus(x):
    return jnp.maximum(x, 0.0) + jnp.log(1.0 + jnp.exp(-jnp.abs(x)))


def _gelu_tanh(x):
    k = math.sqrt(2.0 / math.pi)
    hx = 0.5 * x
    return hx + hx * jnp.tanh(x * (k + (k * 0.044715) * (x * x)))


def _sigmoid(x):
    return 0.5 + 0.5 * jnp.tanh(0.5 * x)


def _rglru_kernel(xr_ref, xp_ref, xn_ref, gr_ref, cw_ref, cb_ref, wa_ref, wi_ref, ba_ref, bi_ref, lam_ref, og_ref,
                  o_ref, sx_ref, a_ref, u_ref, h_ref, carry_ref, hf_ref, xcs_ref, wg_ref, bg_ref, k_ref):
    p = pl.program_id(0)
    i = pl.program_id(1)
    nc = pl.num_programs(1)
    c = i + p * (nc - 1 - 2 * i)
    B = xr_ref.shape[0]
    TC = LRU_TC

    @pl.when(i == 0)
    def _():
        carry_ref[...] = jnp.zeros_like(carry_ref)
        wg_ref[...] = jnp.zeros_like(wg_ref)
        for sel, w_ref in enumerate((wa_ref, wi_ref)):
            for h in range(LRU_BLOCKS):
                lo = h * LRU_BLOCK_DIM
                wg_ref[lo:lo + LRU_BLOCK_DIM, sel * LRU_WIDTH + lo:sel * LRU_WIDTH + lo + LRU_BLOCK_DIM] = (
                    0.5 * w_ref[0, h]).astype(BF16)
        row = pl.ds(p, 1)
        bg_ref[:, :LRU_WIDTH] = 0.5 * ba_ref[row, :]
        bg_ref[:, LRU_WIDTH:] = 0.5 * bi_ref[row, :]
        k_ref[...] = (-0.5 * LRU_C * math.log2(math.e)) * _softplus(-lam_ref[row, :])

    def gates_and_scan(xc2, backward):
        g = jnp.dot(xc2.astype(BF16), wg_ref[...], preferred_element_type=F32) + bg_ref[...]
        ta = jnp.tanh(g[:, :LRU_WIDTH])
        ig = 0.5 + 0.5 * jnp.tanh(g[:, LRU_WIDTH:])
        a = jnp.exp2((1.0 + ta) * k_ref[...])
        z = 1.0 - a * a
        u = z * lax.rsqrt(jnp.maximum(z, 1e-30)) * ig * xc2
        for b in range(B):
            for s in range(LRU_SLABS):
                a_ref[s, b * LRU_PITCH:b * LRU_PITCH + TC, :] = a[b * TC:(b + 1) * TC, s * LANES:(s + 1) * LANES]
                u_ref[s, b * LRU_PITCH:b * LRU_PITCH + TC, :] = u[b * TC:(b + 1) * TC, s * LANES:(s + 1) * LANES]

        def trip(i, hs):
            t0 = pl.multiple_of((TC // LRU_UNROLL - 1 - i if backward else i) * LRU_UNROLL, LRU_UNROLL)
            for j in range(LRU_UNROLL):
                t = t0 + (LRU_UNROLL - 1 - j if backward else j)
                out = []
                for s in range(LRU_SLABS):
                    idx = pl.ds(t, B, stride=LRU_PITCH)
                    hn = a_ref[s, idx, :] * hs[s] + u_ref[s, idx, :]
                    h_ref[s, idx, :] = hn
                    out.append(hn)
                hs = tuple(out)
            return hs

        hs = lax.fori_loop(0, TC // LRU_UNROLL, trip, tuple(carry_ref[s] for s in range(LRU_SLABS)))
        for s in range(LRU_SLABS):
            carry_ref[s] = hs[s]

    @pl.when(p == 0)
    def _():
        sx_ref[:, HALO:HALO + TC, :] = xr_ref[...]
        sx_ref[:, 0:HALO, :] = jnp.where(c > 0, xp_ref[...], 0.0)
        sx_ref[:, HALO + TC:, :] = jnp.where(c < nc - 1, xn_ref[...], 0.0)
        xc = cb_ref[...][None]
        for j in range(CONV_W):
            off = HALO + j - CONV_LEFT
            xc = xc + cw_ref[j:j + 1, :][None] * sx_ref[:, off:off + TC, :]
        xc2 = xc.reshape(B * TC, LRU_WIDTH)
        xcs_ref[c] = xc2.astype(xcs_ref.dtype)
        gates_and_scan(xc2, backward=False)
        for b in range(B):
            for s in range(LRU_SLABS):
                hf_ref[c, s, b * TC:(b + 1) * TC, :] = h_ref[s, b * LRU_PITCH:b * LRU_PITCH + TC, :].astype(hf_ref.dtype)

    @pl.when(p == 1)
    def _():
        gates_and_scan(xcs_ref[c].astype(F32), backward=True)
        for b in range(B):
            hsum = jnp.concatenate(
                [h_ref[s, b * LRU_PITCH:b * LRU_PITCH + TC, :] + hf_ref[c, s, b * TC:(b + 1) * TC, :].astype(F32)
                 for s in range(LRU_SLABS)], axis=1)
            y = hsum * _gelu_tanh(gr_ref[b])
            o_ref[b] = _rms(y, og_ref[...]).astype(o_ref.dtype)


def _rglru(xr, gr, conv_w, conv_b, w_a, b_a, w_i, b_i, lam, out_gain):
    B, S, W = xr.shape
    nc = S // LRU_TC
    hb = LRU_TC // HALO
    fwd = lambda p, i: jnp.where(p == 0, i, nc - 1)
    bwd = lambda p, i: nc - 1 - p * i
    full2 = lambda shape: pl.BlockSpec(shape, lambda p, i: (0,) * len(shape))
    wblock = pl.BlockSpec((1, LRU_BLOCKS, LRU_BLOCK_DIM, LRU_BLOCK_DIM), lambda p, i: (p, 0, 0, 0))
    return pl.pallas_call(
        _rglru_kernel,
        grid=(2, nc),
        in_specs=[pl.BlockSpec((B, LRU_TC, W), lambda p, i: (0, fwd(p, i), 0)),
                  pl.BlockSpec((B, HALO, W), lambda p, i: (0, jnp.maximum(fwd(p, i) * hb - 1, 0), 0)),
                  pl.BlockSpec((B, HALO, W), lambda p, i: (0, jnp.minimum((fwd(p, i) + 1) * hb, S // HALO - 1), 0)),
                  pl.BlockSpec((B, LRU_TC, W), lambda p, i: (0, bwd(p, i), 0)),
                  full2((CONV_W, W)),
                  full2((1, W)),
                  wblock, wblock,
                  full2((2, W)), full2((2, W)), full2((2, W)),
                  full2((1, W))],
        out_specs=pl.BlockSpec((B, LRU_TC, W), lambda p, i: (0, bwd(p, i), 0)),
        out_shape=jax.ShapeDtypeStruct((B, S, W), BF16),
        scratch_shapes=[pltpu.VMEM((B, LRU_TC + 2 * HALO, W), F32),
                        pltpu.VMEM((LRU_SLABS, B * LRU_PITCH, LANES), F32),
                        pltpu.VMEM((LRU_SLABS, B * LRU_PITCH, LANES), F32),
                        pltpu.VMEM((LRU_SLABS, B * LRU_PITCH, LANES), F32),
                        pltpu.VMEM((LRU_SLABS, B, LANES), F32),
                        pltpu.VMEM((nc, LRU_SLABS, B * LRU_TC, LANES), BF16),
                        pltpu.VMEM((nc, B * LRU_TC, W), BF16),
                        pltpu.VMEM((W, 2 * W), BF16),
                        pltpu.VMEM((1, 2 * W), F32),
                        pltpu.VMEM((1, W), F32)],
        compiler_params=_cparams(2, LRU_VMEM_LIMIT),
        name="rglru",
    )(xr, xr, xr, gr, conv_w.astype(F32), conv_b.reshape(1, W).astype(F32), w_a.astype(F32), w_i.astype(F32),
      b_a.astype(F32), b_i.astype(F32), lam.astype(F32), out_gain.reshape(1, W).astype(F32))


RT_TM = 1024
RT_PARTS = 4
RT_COLS = LANES
RT_ROWS = 48
RINFO = SUBLANES


def _split_bf16(x):
    hi = x.astype(BF16)
    lo = (x - hi.astype(F32)).astype(BF16)
    return hi, lo


def _route_kernel(an_ref, ln_ref, x_ref, wo_ref, g2_ref, wr_ref, br_ref,
                  x1_ref, h2_ref, gt_ref, ei_ref, cnt_ref, wob_ref, wrb_ref, tri_ref, run_ref, runc_ref):
    @pl.when(pl.program_id(0) == 0)
    def _():
        wob_ref[...] = wo_ref[...].astype(BF16)
        hi, lo = _split_bf16(wr_ref[...])
        wrb_ref[:RT_ROWS, :] = hi
        wrb_ref[RT_ROWS:, :] = lo
        r = lax.broadcasted_iota(jnp.int32, (RT_TM, RT_TM), 0)
        cidx = lax.broadcasted_iota(jnp.int32, (RT_TM, RT_TM), 1)
        tri_ref[...] = (r < cidx).astype(BF16)
        run_ref[...] = jnp.zeros_like(run_ref)
        runc_ref[...] = jnp.zeros_like(runc_ref)

    nt_dims = (((1,), (1,)), ((), ()))
    part = RT_TM // RT_PARTS
    x1s = []
    for r in range(RT_PARTS):
        rows = slice(r * part, (r + 1) * part)
        x1 = (x_ref[rows, :]
              + jnp.dot(an_ref[rows, :], wob_ref[:ATTN_WIDTH, :], preferred_element_type=F32)
              + jnp.dot(ln_ref[rows, :], wob_ref[ATTN_WIDTH:, :], preferred_element_type=F32))
        x1_ref[rows, :] = x1
        x1s.append(x1)
    splits = []
    for r, x1 in enumerate(x1s):
        h2 = _rms(x1, g2_ref[...])
        hi = h2.astype(BF16)
        hi_f = hi.astype(F32)
        h2_ref[r * part:(r + 1) * part, :] = _pack_rounded(hi_f)
        splits.append((hi, (h2 - hi_f).astype(BF16)))
    logits = []
    for hi, lo in splits:
        t1 = lax.dot_general(wrb_ref[...], hi, nt_dims, preferred_element_type=F32)
        t2 = lax.dot_general(wrb_ref[:RT_ROWS, :], lo, nt_dims, preferred_element_type=F32)
        logits.append(t1[:RT_ROWS] + t1[RT_ROWS:] + t2)
    logit = jnp.concatenate(logits, axis=1) + br_ref[...]

    sub = lax.broadcasted_iota(jnp.int32, (SUBLANES, RT_TM), 0)
    first_min = lambda hit: jnp.min(jnp.where(hit, sub, SUBLANES), axis=0, keepdims=True)
    is_g = sub < N_GROUPS
    gl = jnp.where(is_g, logit[:SUBLANES], -jnp.inf)
    gm = jnp.max(gl, axis=0, keepdims=True)
    gidx = first_min(gl == gm)
    g_p = 1.0 / jnp.sum(jnp.where(is_g, jnp.exp(logit[:SUBLANES] - gm), 0.0), axis=0, keepdims=True)
    el = logit[SUBLANES:2 * SUBLANES]
    for g in range(1, N_GROUPS):
        el = jnp.where(gidx == g, logit[(g + 1) * SUBLANES:(g + 2) * SUBLANES], el)
    m1 = jnp.max(el, axis=0, keepdims=True)
    i1 = first_min(el == m1)
    el2 = jnp.where(sub == i1, -jnp.inf, el)
    m2 = jnp.max(el2, axis=0, keepdims=True)
    i2 = first_min(el2 == m2)
    t = jnp.exp(m2 - m1)
    gate1 = g_p / (1.0 + t)
    gate2 = g_p * t / (1.0 + t)
    e1 = gidx * EXPERTS_PER_GROUP + i1
    e2 = gidx * EXPERTS_PER_GROUP + i2

    erow = lax.broadcasted_iota(jnp.int32, (N_EXPERTS, RT_TM), 0)
    oh1 = erow == e1
    oh2 = erow == e2
    oh = (oh1 | oh2).astype(F32)
    ohb = oh.astype(BF16)
    cum = jnp.dot(ohb, tri_ref[...], preferred_element_type=F32) + runc_ref[...]
    rank1 = jnp.sum(jnp.where(oh1, cum, 0.0), axis=0, keepdims=True)
    rank2 = jnp.sum(jnp.where(oh2, cum, 0.0), axis=0, keepdims=True)
    runc_ref[...] = runc_ref[...] + jnp.sum(oh, axis=1, keepdims=True)
    tile_cnt = lax.dot_general(jnp.ones((SUBLANES, RT_TM), BF16), ohb, nt_dims, preferred_element_type=F32)
    run_ref[:, :N_EXPERTS] = run_ref[:, :N_EXPERTS] + tile_cnt[0:1]
    cnt_ref[...] = run_ref[...].astype(jnp.int32)

    rows = [e1, e2, rank1.astype(jnp.int32), rank2.astype(jnp.int32)]
    ei = jnp.zeros((RINFO, RT_TM), jnp.int32)
    for k, v in enumerate(rows):
        ei = jnp.where(sub == k, v, ei)
    ei_ref[0] = ei
    gt_ref[...] = jnp.where(sub == 0, gate1, jnp.where(sub == 1, gate2, 0.0)).T


def _out_route(attn_n, lru_n, x2, w_out, ln2, w_group, b_group, w_er, b_er):
    T = x2.shape[0]
    pad_g = SUBLANES - N_GROUPS
    wr = jnp.concatenate([jnp.pad(w_group.T, ((0, pad_g), (0, 0))),
                          jnp.transpose(w_er, (0, 2, 1)).reshape(N_EXPERTS, D_MODEL)], axis=0)
    wr = jnp.pad(wr, ((0, RT_ROWS - wr.shape[0]), (0, 0))).astype(F32)
    br = jnp.concatenate([jnp.pad(b_group, (0, pad_g)), b_er.reshape(-1)])
    br = jnp.pad(br, (0, RT_ROWS - br.shape[0])).reshape(RT_ROWS, 1).astype(F32)
    row = lambda w: pl.BlockSpec((RT_TM, w), lambda i: (i, 0))
    const = lambda shape: pl.BlockSpec(shape, lambda i: (0, 0))
    return pl.pallas_call(
        _route_kernel,
        grid=(T // RT_TM,),
        in_specs=[row(ATTN_WIDTH), row(LRU_WIDTH), row(D_MODEL), const((D_MODEL, D_MODEL)), const((1, D_MODEL)),
                  const((RT_ROWS, D_MODEL)), const((RT_ROWS, 1))],
        out_specs=[row(D_MODEL), row(PACKED), row(RINFO),
                   pl.BlockSpec((1, RINFO, RT_TM), lambda i: (i, 0, 0)), const((1, RT_COLS))],
        out_shape=[jax.ShapeDtypeStruct((T, D_MODEL), F32),
                   jax.ShapeDtypeStruct((T, PACKED), U32),
                   jax.ShapeDtypeStruct((T, RINFO), F32),
                   jax.ShapeDtypeStruct((T // RT_TM, RINFO, RT_TM), jnp.int32),
                   jax.ShapeDtypeStruct((1, RT_COLS), jnp.int32)],
        scratch_shapes=[pltpu.VMEM((D_MODEL, D_MODEL), BF16),
                        pltpu.VMEM((2 * RT_ROWS, D_MODEL), BF16),
                        pltpu.VMEM((RT_TM, RT_TM), BF16),
                        pltpu.VMEM((1, RT_COLS), F32),
                        pltpu.VMEM((N_EXPERTS, 1), F32)],
        compiler_params=_cparams(1),
        name="out_route",
    )(attn_n, lru_n, x2, w_out, ln2.reshape(1, D_MODEL).astype(F32), wr, br)


def _moe_cap(T):
    A = T * TOP_K
    return ((A + MOE_BLOCK - 1) // MOE_BLOCK) * MOE_BLOCK + N_EXPERTS * MOE_BLOCK


PAD_BITS = tuple(1 << b for b in reversed(range(3, MOE_BLOCK.bit_length() - 1)))


def _layout_kernel(cnt_ref, ei_ref, dest_ref, pstart, be_ref, nu_ref, ge_ref):
    n_blocks = be_ref.shape[0]

    def lay(e, carry):
        start, blk, grp = carry
        pstart[e] = start
        nb = (cnt_ref[0, e] + MOE_BLOCK - 1) // MOE_BLOCK
        ge_ref[grp] = e

        def fill(k, c):
            be_ref[blk + k] = e
            return c
        lax.fori_loop(0, nb, fill, 0)
        return start + nb * MOE_BLOCK, blk + nb, grp + (nb > 0).astype(jnp.int32)
    _, used, groups = lax.fori_loop(0, N_EXPERTS, lay, (jnp.int32(0), jnp.int32(0), jnp.int32(0)))
    nu_ref[0] = used

    def tail(k, c):
        be_ref[k] = N_EXPERTS - 1
        return c
    lax.fori_loop(used, n_blocks, tail, 0)

    def no_group(k, c):
        ge_ref[k] = -1
        return c
    lax.fori_loop(groups, ge_ref.shape[0], no_group, 0)

    expert = ei_ref[:, 0:TOP_K, :]
    dest = ei_ref[:, TOP_K:2 * TOP_K, :]
    for e in range(N_EXPERTS):
        dest = dest + jnp.where(expert == e, pstart[e], 0)
    dest_ref[...] = dest


def _layout(ei, cnt, n_blocks):
    nt = ei.shape[0]
    smem = pl.BlockSpec(memory_space=pltpu.SMEM)
    vmem = pl.BlockSpec(memory_space=pltpu.VMEM)
    return pl.pallas_call(
        _layout_kernel,
        in_specs=[smem, vmem],
        out_specs=[vmem, smem, smem, smem, smem],
        out_shape=[jax.ShapeDtypeStruct((nt, TOP_K, RT_TM), jnp.int32),
                   jax.ShapeDtypeStruct((N_EXPERTS,), jnp.int32),
                   jax.ShapeDtypeStruct((n_blocks,), jnp.int32),
                   jax.ShapeDtypeStruct((1,), jnp.int32),
                   jax.ShapeDtypeStruct((N_EXPERTS + W_AHEAD,), jnp.int32)],
        name="layout",
    )(cnt, ei)


SC_CHUNK = 64
SC_BUFS = 3
SC_LEAD = SC_BUFS - 1


def _sc_workers():
    info = plsc.get_sparse_core_info()
    return info.num_cores, info.num_subcores


def _sc_ring(n_chunks, read, write):
    for c in range(min(SC_LEAD, n_chunks)):
        for cp in read(c):
            cp.start()
    reclaimed = set()
    for c in range(n_chunks):
        for cp in read(c):
            cp.wait()
        for cp in write(c):
            cp.start()
        nxt = c + SC_LEAD
        if nxt < n_chunks:
            if nxt - SC_BUFS >= 0:
                for cp in write(nxt - SC_BUFS):
                    cp.wait()
                reclaimed.add(nxt - SC_BUFS)
            for cp in read(nxt):
                cp.start()
    for c in range(n_chunks):
        if c not in reclaimed:
            for cp in write(c):
                cp.wait()


def _sc_dispatch(h2p, dest, cap):
    T = h2p.shape[0]
    nc, ns = _sc_workers()
    per_w = T // (nc * ns)
    n_ch = per_w // SC_CHUNK
    nt, _, tm = dest.shape
    assert nt * tm == T and tm % per_w == 0 and per_w % SC_CHUNK == 0
    idx = dest.reshape(nt, TOP_K, tm // per_w, per_w).transpose(0, 2, 1, 3).reshape(nc * ns, TOP_K * n_ch, SC_CHUNK)
    mesh = plsc.VectorSubcoreMesh(core_axis_name="c", subcore_axis_name="s")

    @functools.partial(
        pl.kernel, mesh=mesh,
        out_type=jax.ShapeDtypeStruct((cap, PACKED), U32),
        scratch_types=[pltpu.VMEM((TOP_K * n_ch, SC_CHUNK), jnp.int32),
                       pltpu.VMEM((SC_BUFS, SC_CHUNK, PACKED), U32),
                       pltpu.SemaphoreType.DMA((SC_BUFS,)),
                       pltpu.SemaphoreType.DMA((SC_BUFS,))])
    def scatter(src_hbm, idx_hbm, out_hbm, idx_v, rows_v, rsem, wsem):
        wid = lax.axis_index("s") * nc + lax.axis_index("c")
        base = pl.multiple_of(wid * per_w, per_w)
        pltpu.sync_copy(idx_hbm.at[wid], idx_v)

        def read(c):
            b = c % SC_BUFS
            return [pltpu.make_async_copy(src_hbm.at[pl.ds(base + c * SC_CHUNK, SC_CHUNK)], rows_v.at[b], rsem.at[b])]

        def write(c):
            b = c % SC_BUFS
            return [pltpu.make_async_copy(rows_v.at[b], out_hbm.at[idx_v.at[k * n_ch + c]], wsem.at[b])
                    for k in range(TOP_K)]
        _sc_ring(n_ch, read, write)

    return scatter(h2p, idx)


def _sc_gather(yb, dest):
    nt, _, tm = dest.shape
    nc, ns = _sc_workers()
    n_rows = nt * TOP_K * tm
    per_w = n_rows // (nc * ns)
    n_ch = per_w // SC_CHUNK
    assert per_w * nc * ns == n_rows and per_w % SC_CHUNK == 0
    mesh = plsc.VectorSubcoreMesh(core_axis_name="c", subcore_axis_name="s")

    @functools.partial(
        pl.kernel, mesh=mesh,
        out_type=jax.ShapeDtypeStruct((n_rows, PACKED), U32),
        scratch_types=[pltpu.VMEM((per_w,), jnp.int32),
                       pltpu.VMEM((SC_BUFS, SC_CHUNK, PACKED), U32),
                       pltpu.SemaphoreType.DMA((SC_BUFS,)),
                       pltpu.SemaphoreType.DMA((SC_BUFS,))])
    def gather(table_hbm, idx_hbm, out_hbm, idx_v, rows_v, gsem, wsem):
        wid = lax.axis_index("s") * nc + lax.axis_index("c")
        base = pl.multiple_of(wid * per_w, per_w)
        pltpu.sync_copy(idx_hbm.at[pl.ds(base, per_w)], idx_v)

        def read(c):
            b = c % SC_BUFS
            return [pltpu.make_async_copy(table_hbm.at[idx_v.at[pl.ds(c * SC_CHUNK, SC_CHUNK)]], rows_v.at[b], gsem.at[b])]

        def write(c):
            b = c % SC_BUFS
            return [pltpu.make_async_copy(rows_v.at[b], out_hbm.at[pl.ds(base + c * SC_CHUNK, SC_CHUNK)], wsem.at[b])]
        _sc_ring(n_ch, read, write)

    return gather(yb, dest.reshape(n_rows)).reshape(nt, TOP_K, tm, PACKED)


def _padfill_kernel(cnt_ref, pstart, xs_in, xs_ref, zeros, zsem):
    del xs_in

    def pad_copies(fn):
        for e in range(N_EXPERTS):
            cnt = cnt_ref[0, e]
            head = (-cnt) & (SUBLANES - 1)
            rest = ((-cnt) & (MOE_BLOCK - 1)) - head
            off = pstart[e] + cnt
            for k in range(SUBLANES - 1):
                @pl.when(k < head)
                def _(off=off, k=k):
                    fn(pltpu.make_async_copy(zeros.at[pl.ds(0, 1), :], xs_ref.at[pl.ds(off + k, 1), :], zsem))
            off = off + head
            for bit in PAD_BITS:
                @pl.when((rest & bit) != 0)
                def _(off=off, bit=bit):
                    fn(pltpu.make_async_copy(zeros.at[pl.ds(0, bit), :],
                                             xs_ref.at[pl.ds(pl.multiple_of(off, SUBLANES), bit), :], zsem))
                off = off + (rest & bit)

    zeros[...] = jnp.zeros_like(zeros)
    pad_copies(lambda cp: cp.start())
    pad_copies(lambda cp: cp.wait())


def _padfill(xs, pstart, cnt):
    smem = pl.BlockSpec(memory_space=pltpu.SMEM)
    hbm = pl.BlockSpec(memory_space=pl.ANY)
    return pl.pallas_call(
        _padfill_kernel,
        in_specs=[smem, smem, hbm],
        out_specs=hbm,
        out_shape=jax.ShapeDtypeStruct(xs.shape, xs.dtype),
        input_output_aliases={2: 0},
        scratch_shapes=[pltpu.VMEM((MOE_BLOCK // 2, PACKED), U32), pltpu.SemaphoreType.DMA(())],
        name="padfill",
    )(cnt, pstart, xs)


W_SLOTS = 4
W_AHEAD = W_SLOTS - 1
EXPERT_GROUP = 16
EXPERT_RUNS = (1, 2, 4, 8)


def _expert_kernel(be_ref, nu_ref, ge_ref, x_ref, wg_hbm, wu_hbm, wd_hbm, o_ref,
                   wgf, wuf, wdf, grp_ref, sems):
    step = pl.program_id(0)

    def weight_copies(e, slot):
        return (pltpu.make_async_copy(wg_hbm.at[e], wgf.at[slot], sems.at[slot, 0]),
                pltpu.make_async_copy(wu_hbm.at[e], wuf.at[slot], sems.at[slot, 1]),
                pltpu.make_async_copy(wd_hbm.at[e], wdf.at[slot], sems.at[slot, 2]))

    @pl.when(step == 0)
    def _():
        grp_ref[0] = 0
        for a in range(W_AHEAD):
            @pl.when(ge_ref[a] >= 0)
            def _(a=a):
                for cp in weight_copies(ge_ref[a], a):
                    cp.start()

    n_blocks = be_ref.shape[0]
    n_used = nu_ref[0]

    def swiglu(s, n, slot):
        rows = pl.ds(pl.multiple_of(s * MOE_BLOCK, MOE_BLOCK), n * MOE_BLOCK)
        lo, hi = _unpack_rows(x_ref[rows, :])
        lo = lo.astype(BF16)
        hi = hi.astype(BF16)
        g = (jnp.dot(lo, wgf[slot, :PACKED, :], preferred_element_type=F32)
             + jnp.dot(hi, wgf[slot, PACKED:, :], preferred_element_type=F32))
        u = (jnp.dot(lo, wuf[slot, :PACKED, :], preferred_element_type=F32)
             + jnp.dot(hi, wuf[slot, PACKED:, :], preferred_element_type=F32))
        h = (g * _sigmoid(g) * u).astype(BF16)
        o_ref[rows, :] = _pack_rows(jnp.dot(h, wdf[slot], preferred_element_type=F32))

    def run(s):
        j = step * EXPERT_GROUP + s
        e = be_ref[j]
        first = jnp.logical_or(j == 0, e != be_ref[jnp.maximum(j - 1, 0)])

        @pl.when(first)
        def _():
            grp = grp_ref[0]
            slot = grp % W_SLOTS
            for cp in weight_copies(e, slot):
                cp.wait()
            nxt = ge_ref[grp + W_AHEAD]

            @pl.when(nxt >= 0)
            def _():
                for cp in weight_copies(nxt, (grp + W_AHEAD) % W_SLOTS):
                    cp.start()
            grp_ref[0] = grp + 1

        def same(k):
            return (s + k < EXPERT_GROUP) & (j + k < n_used) & (be_ref[jnp.minimum(j + k, n_blocks - 1)] == e)
        take = jnp.int32(1)
        for n in EXPERT_RUNS[1:]:
            ok = same(n - 1)
            for k in range(1, n - 1):
                ok = ok & same(k)
            take = jnp.where(ok, n, take)
        slot = (grp_ref[0] + W_SLOTS - 1) % W_SLOTS
        for n in EXPERT_RUNS:
            @pl.when(take == n)
            def _(n=n):
                swiglu(s, n, slot)
        return s + take

    lax.while_loop(lambda s: (s < EXPERT_GROUP) & (step * EXPERT_GROUP + s < n_used), run, jnp.int32(0))


def _experts(xs, block_expert, n_used, group_expert, w_gate, w_up, w_down):
    cap = xs.shape[0]
    n_blocks = cap // MOE_BLOCK
    assert n_blocks % EXPERT_GROUP == 0
    rows = EXPERT_GROUP * MOE_BLOCK
    last = lambda j, be, nu, ge: jnp.minimum(j, (nu[0] - 1) // EXPERT_GROUP)
    hbm = pl.BlockSpec(memory_space=pl.ANY)
    gs = pltpu.PrefetchScalarGridSpec(
        num_scalar_prefetch=3,
        grid=(n_blocks // EXPERT_GROUP,),
        in_specs=[pl.BlockSpec((rows, PACKED), lambda j, be, nu, ge: (last(j, be, nu, ge), 0)), hbm, hbm, hbm],
        out_specs=pl.BlockSpec((rows, PACKED), lambda j, be, nu, ge: (last(j, be, nu, ge), 0)),
        scratch_shapes=[pltpu.VMEM((W_SLOTS, D_MODEL, D_EXPERT), F32),
                        pltpu.VMEM((W_SLOTS, D_MODEL, D_EXPERT), F32),
                        pltpu.VMEM((W_SLOTS, D_EXPERT, D_MODEL), F32),
                        pltpu.SMEM((1,), jnp.int32),
                        pltpu.SemaphoreType.DMA((W_SLOTS, 3))],
    )
    return pl.pallas_call(
        _expert_kernel,
        grid_spec=gs,
        out_shape=jax.ShapeDtypeStruct((cap, PACKED), U32),
        compiler_params=_cparams(1),
        name="experts",
    )(block_expert, n_used, group_expert, xs, w_gate, w_up, w_down)


CB_TM = RT_TM


def _combine_kernel(x1_ref, gt_ref, y2_ref, o_ref):
    g = gt_ref[...]
    lo1, hi1 = _unpack_rows(y2_ref[0, 0])
    lo2, hi2 = _unpack_rows(y2_ref[0, 1])
    o_ref[:, :PACKED] = x1_ref[:, :PACKED] + g[:, 0:1] * lo1 + g[:, 1:2] * lo2
    o_ref[:, PACKED:] = x1_ref[:, PACKED:] + g[:, 0:1] * hi1 + g[:, 1:2] * hi2


def _combine(x1, gates, y2):
    T = x1.shape[0]
    nt = T // CB_TM
    return pl.pallas_call(
        _combine_kernel,
        grid=(nt,),
        in_specs=[pl.BlockSpec((CB_TM, D_MODEL), lambda i: (i, 0)),
                  pl.BlockSpec((CB_TM, RINFO), lambda i: (i, 0)),
                  pl.BlockSpec((1, TOP_K, CB_TM, PACKED), lambda i: (i, 0, 0, 0))],
        out_specs=pl.BlockSpec((CB_TM, D_MODEL), lambda i: (i, 0)),
        out_shape=jax.ShapeDtypeStruct((T, D_MODEL), F32),
        compiler_params=_cparams(1),
        name="combine",
    )(x1, gates, y2)


def _layer(x, rel_bias, ln1, w_in, q_norm, k_norm, attn_sink, conv_w, conv_b, lru_wa, lru_ba, lru_wi, lru_bi,
           lru_lambda, out_norm_attn, out_norm_lru, w_out, ln2, w_group, b_group, w_er, b_er, w_gate, w_up, w_down):
    B, S, D = x.shape
    T = B * S
    x2 = x.reshape(T, D)
    q, kv, xr, gr = _in_proj(x2, ln1, w_in, q_norm, k_norm)
    attn_n = _attention(q.reshape(B, S, ATTN_WIDTH), kv.reshape(B, S, 2 * KV_WIDTH), rel_bias, attn_sink,
                        out_norm_attn)
    lru_n = _rglru(xr.reshape(B, S, LRU_WIDTH), gr.reshape(B, S, LRU_WIDTH), conv_w, conv_b,
                   lru_wa, lru_ba, lru_wi, lru_bi, lru_lambda, out_norm_lru)
    x1, h2, gates, ei, cnt = _out_route(attn_n.reshape(T, ATTN_WIDTH), lru_n.reshape(T, LRU_WIDTH), x2, w_out, ln2,
                                        w_group, b_group, w_er, b_er)
    cap = _moe_cap(T)
    dest, pstart, block_expert, n_used, group_expert = _layout(ei, cnt, cap // MOE_BLOCK)
    xs = _padfill(_sc_dispatch(h2, dest, cap), pstart, cnt)
    yb = _experts(xs, block_expert, n_used, group_expert, w_gate, w_up, w_down)
    out = _combine(x1, gates, _sc_gather(yb, dest))
    return out.reshape(B, S, D)


def kernel(x, rel_bias, ln1, w_in, q_norm, k_norm, attn_sink, conv_w, conv_b, lru_wa, lru_ba, lru_wi, lru_bi,
           lru_lambda, out_norm_attn, out_norm_lru, w_out, ln2, w_group, b_group, w_expert_router, b_expert_router,
           w_gate, w_up, w_down):
    depth = ln1.shape[0]
    for l in range(depth):
        x = _layer(x, rel_bias, ln1[l], w_in[l], q_norm[l], k_norm[l], attn_sink[l], conv_w[l], conv_b[l],
                   lru_wa[l], lru_ba[l], lru_wi[l], lru_bi[l], lru_lambda[l], out_norm_attn[l], out_norm_lru[l],
                   w_out[l], ln2[l], w_group[l], b_group[l], w_expert_router[l], b_expert_router[l],
                   w_gate[l], w_up[l], w_down[l])
    return x
```

```python
import functools
import math

import jax
import jax.numpy as jnp
from jax import lax
from jax.experimental import pallas as pl
from jax.experimental.pallas import tpu as pltpu
from jax.experimental.pallas import tpu_sc as plsc

D_MODEL = 1024
N_HEADS = 8
N_KV_HEADS = 2
HEAD_DIM = 64
Q_PER_KV = N_HEADS // N_KV_HEADS
ATTN_WIDTH = N_HEADS * HEAD_DIM
KV_WIDTH = N_KV_HEADS * HEAD_DIM
WINDOW = 128
BLOCK = 128
NUM_BUCKETS = 32
MAX_DISTANCE = 128
LRU_WIDTH = D_MODEL - ATTN_WIDTH
LRU_BLOCKS = 8
LRU_BLOCK_DIM = LRU_WIDTH // LRU_BLOCKS
LRU_C = 8.0
CONV_W = 4
CONV_LEFT = 2
N_GROUPS = 4
EXPERTS_PER_GROUP = 8
N_EXPERTS = N_GROUPS * EXPERTS_PER_GROUP
TOP_K = 2
D_EXPERT = 512
MOE_BLOCK = 256
EPS = 1e-6
NEG_INF = -1e30

LANES = 128
SUBLANES = 8
VMEM_LIMIT = 56 * 1024 * 1024
LRU_VMEM_LIMIT = 62 * 1024 * 1024

F32 = jnp.float32
BF16 = jnp.bfloat16
LOG2E = math.log2(math.e)


def _cparams(n_axes, vmem=VMEM_LIMIT):
    return pltpu.CompilerParams(dimension_semantics=("arbitrary",) * n_axes, vmem_limit_bytes=vmem)


def _rms(x, gain):
    return x * lax.rsqrt(jnp.mean(x * x, axis=-1, keepdims=True) + EPS) * gain


U32 = jnp.uint32
HI_MASK = 0xFFFF0000
PACKED = D_MODEL // 2


def _pack_rows(x):
    return _pack_rounded(x.astype(BF16).astype(F32))


def _pack_rounded(xb):
    h = xb.shape[1] // 2
    lo = lax.bitcast_convert_type(xb[:, :h], U32) >> 16
    hi = lax.bitcast_convert_type(xb[:, h:], U32) & jnp.uint32(HI_MASK)
    return lo | hi


def _unpack_rows(p):
    lo = lax.bitcast_convert_type(p << 16, F32)
    hi = lax.bitcast_convert_type(p & jnp.uint32(HI_MASK), F32)
    return lo, hi


IN_TM = 1024


def _head_rms(x, n_heads, gain):
    head = lax.broadcasted_iota(jnp.int32, (1, n_heads * HEAD_DIM), 1) // HEAD_DIM
    x2 = x * x
    scale = jnp.zeros_like(x)
    for h in range(n_heads):
        ms = jnp.sum(jnp.where(head == h, x2, 0.0), axis=-1, keepdims=True) * (1.0 / HEAD_DIM)
        scale = jnp.where(head == h, lax.rsqrt(ms + EPS), scale)
    return x * scale * gain


def _in_proj_kernel(x_ref, g_ref, w_ref, qn_ref, kn_ref, q_ref, kv_ref, xr_ref, gr_ref, wb_ref, qg_ref, kg_ref):
    @pl.when(pl.program_id(0) == 0)
    def _():
        wb_ref[...] = w_ref[...].astype(BF16)
        qg_ref[...] = jnp.concatenate([qn_ref[...]] * N_HEADS, axis=1) * (HEAD_DIM ** -0.5 * LOG2E)
        kg_ref[...] = jnp.concatenate([kn_ref[...]] * N_KV_HEADS, axis=1)

    h = _rms(x_ref[...], g_ref[...]).astype(BF16)
    c_k = ATTN_WIDTH
    c_v = c_k + KV_WIDTH
    c_x = c_v + KV_WIDTH
    c_g = c_x + LRU_WIDTH
    q = jnp.dot(h, wb_ref[:, :c_k], preferred_element_type=F32)
    q_ref[...] = _head_rms(q, N_HEADS, qg_ref[...]).astype(BF16)
    k = jnp.dot(h, wb_ref[:, c_k:c_v], preferred_element_type=F32)
    kv_ref[:, :KV_WIDTH] = _head_rms(k, N_KV_HEADS, kg_ref[...]).astype(BF16)
    kv_ref[:, KV_WIDTH:] = jnp.dot(h, wb_ref[:, c_v:c_x], preferred_element_type=F32).astype(BF16)
    xr_ref[...] = jnp.dot(h, wb_ref[:, c_x:c_g], preferred_element_type=F32)
    gr_ref[...] = jnp.dot(h, wb_ref[:, c_g:], preferred_element_type=F32)


def _in_proj(x2, ln1, w_in, q_gain, k_gain):
    T = x2.shape[0]
    n_in = w_in.shape[1]
    row = lambda w: pl.BlockSpec((IN_TM, w), lambda i: (i, 0))
    return pl.pallas_call(
        _in_proj_kernel,
        grid=(T // IN_TM,),
        in_specs=[row(D_MODEL),
                  pl.BlockSpec((1, D_MODEL), lambda i: (0, 0)),
                  pl.BlockSpec((D_MODEL, n_in), lambda i: (0, 0)),
                  pl.BlockSpec((1, HEAD_DIM), lambda i: (0, 0)),
                  pl.BlockSpec((1, HEAD_DIM), lambda i: (0, 0))],
        out_specs=[row(ATTN_WIDTH), row(2 * KV_WIDTH), row(LRU_WIDTH), row(LRU_WIDTH)],
        out_shape=[jax.ShapeDtypeStruct((T, ATTN_WIDTH), BF16),
                   jax.ShapeDtypeStruct((T, 2 * KV_WIDTH), BF16),
                   jax.ShapeDtypeStruct((T, LRU_WIDTH), F32),
                   jax.ShapeDtypeStruct((T, LRU_WIDTH), F32)],
        scratch_shapes=[pltpu.VMEM((D_MODEL, n_in), BF16),
                        pltpu.VMEM((1, ATTN_WIDTH), F32),
                        pltpu.VMEM((1, KV_WIDTH), F32)],
        compiler_params=_cparams(1),
        name="in_proj",
    )(x2, ln1.reshape(1, D_MODEL), w_in, q_gain.reshape(1, HEAD_DIM).astype(F32),
      k_gain.reshape(1, HEAD_DIM).astype(F32))


def _t5_bucket(rel):
    half = NUM_BUCKETS // 2
    max_exact = half // 2
    base = jnp.where(rel > 0, half, 0)
    n = jnp.abs(rel)
    nf = jnp.maximum(n, 1).astype(jnp.float32)
    large = max_exact + (jnp.log(nf / max_exact) / math.log(MAX_DISTANCE / max_exact)
                         * (half - max_exact)).astype(jnp.int32)
    large = jnp.minimum(large, half - 1)
    return base + jnp.where(n < max_exact, n, large)


HEAD_PAIRS = Q_PER_KV // 2
EDGE_VARIANTS = 3


def _fill_bias_table(rb_ref, bucket_ref, band_ref, o_ref):
    bucket = bucket_ref[...]
    band = band_ref[...] > 0
    col = lax.broadcasted_iota(jnp.int32, bucket.shape, 1)
    valid = (band & (col >= BLOCK), band, band & (col < 2 * BLOCK))
    for h in range(N_HEADS):
        acc = jnp.zeros(bucket.shape, F32)
        for b in range(NUM_BUCKETS):
            acc = jnp.where(bucket == b, rb_ref[b, h], acc)
        kv, g = divmod(h, Q_PER_KV)
        pair, parity = divmod(g, 2)
        for var in range(EDGE_VARIANTS):
            o_ref[var, kv, parity, pair * BLOCK:(pair + 1) * BLOCK, :] = jnp.where(valid[var], acc * LOG2E, NEG_INF)


def _attn_kernel(sink_ref, rb_ref, q_ref, kp_ref, kc_ref, kn_ref, bucket_ref, band_ref, og_ref, o_ref, bias_ref):
    n = pl.program_id(1)

    @pl.when((pl.program_id(0) == 0) & (n == 0))
    def _():
        _fill_bias_table(rb_ref, bucket_ref, band_ref, bias_ref)

    kv_all = jnp.concatenate([kp_ref[0], kc_ref[0], kn_ref[0]], axis=0)
    for qb in range(ATTN_QB):
        variant = 1
        if qb == 0:
            variant = jnp.where(n == 0, 0, 1)
        if qb == ATTN_QB - 1:
            variant = jnp.where(n == pl.num_programs(1) - 1, 2, variant)
        out = _attn_block(q_ref[0, qb * BLOCK:(qb + 1) * BLOCK, :], kv_all[qb * BLOCK:(qb + 3) * BLOCK, :],
                          lambda kv, parity: bias_ref[variant, kv, parity], sink_ref)
        o_ref[0, qb * BLOCK:(qb + 1) * BLOCK, :] = _rms(out, og_ref[...]).astype(o_ref.dtype)


def _attn_block(q, kvw, bias, sink_ref):
    low = lax.broadcasted_iota(jnp.int32, (1, LANES), 1) < HEAD_DIM
    swap = lambda slab: pltpu.roll(slab.astype(F32), HEAD_DIM, 1).astype(BF16)
    kslab, vslab = kvw[:, :KV_WIDTH], kvw[:, KV_WIDTH:]
    kslab_sw, vslab_sw = swap(kslab), swap(vslab)
    rowi = lax.broadcasted_iota(jnp.int32, (HEAD_PAIRS * BLOCK, 1), 0)
    combos = [(kv, parity) for kv in range(N_KV_HEADS) for parity in range(2)]
    scores, vzs, sinks = [], [], []
    for kv, parity in combos:
        ks, vs = (kslab, vslab) if (kv == 0) == (parity == 0) else (kslab_sw, vslab_sw)
        keep = low if parity == 0 else jnp.logical_not(low)
        kz = jnp.where(keep, ks, jnp.zeros_like(ks))
        vzs.append(jnp.where(keep, vs, jnp.zeros_like(vs)))
        base = kv * Q_PER_KV * HEAD_DIM
        qpair = jnp.concatenate([q[:, base + j * LANES:base + (j + 1) * LANES] for j in range(HEAD_PAIRS)], axis=0)
        s = lax.dot_general(qpair, kz, (((1,), (1,)), ((), ())), preferred_element_type=F32)
        scores.append(s + bias(kv, parity))
        sink = jnp.zeros((HEAD_PAIRS * BLOCK, 1), F32)
        for j in range(HEAD_PAIRS):
            sink = jnp.where(rowi // BLOCK == j, sink_ref[kv * Q_PER_KV + 2 * j + parity], sink)
        sinks.append(sink)
    probs, inv = [], []
    for s, sink in zip(scores, sinks):
        m = jnp.maximum(jnp.max(s, axis=-1, keepdims=True), sink)
        p = jnp.exp2(s - m)
        inv.append(1.0 / (jnp.sum(p, axis=-1, keepdims=True) + jnp.exp2(sink - m)))
        probs.append(p.astype(BF16))
    outs = [jnp.dot(p, vz, preferred_element_type=F32) * r for p, vz, r in zip(probs, vzs, inv)]
    cols = []
    for kv in range(N_KV_HEADS):
        acc = outs[2 * kv] + outs[2 * kv + 1]
        cols += [acc[j * BLOCK:(j + 1) * BLOCK, :] for j in range(HEAD_PAIRS)]
    return jnp.concatenate(cols, axis=1)


ATTN_QB = 8


def _attention(q, kv, rel_bias, sink, out_gain):
    B, S, _ = q.shape
    nb = S // BLOCK
    assert ATTN_QB >= 2 and nb % ATTN_QB == 0, "a step's first and last query blocks must be distinct"
    ns = nb // ATTN_QB
    rows = ATTN_QB * BLOCK
    qi = jnp.arange(BLOCK, dtype=jnp.int32)
    kj = jnp.arange(3 * BLOCK, dtype=jnp.int32)
    rel = kj[None, :] - BLOCK - qi[:, None]
    bucket = _t5_bucket(rel).astype(jnp.int32)
    band = (jnp.abs(rel) <= WINDOW).astype(jnp.int32)
    kvspec = lambda f: pl.BlockSpec((1, BLOCK, 2 * KV_WIDTH), f)
    smem = pl.BlockSpec(memory_space=pltpu.SMEM)
    geom = pl.BlockSpec((BLOCK, 3 * BLOCK), lambda b, n: (0, 0))
    return pl.pallas_call(
        _attn_kernel,
        grid=(B, ns),
        in_specs=[smem, smem,
                  pl.BlockSpec((1, rows, ATTN_WIDTH), lambda b, n: (b, n, 0)),
                  kvspec(lambda b, n: (b, jnp.maximum(n * ATTN_QB - 1, 0), 0)),
                  pl.BlockSpec((1, rows, 2 * KV_WIDTH), lambda b, n: (b, n, 0)),
                  kvspec(lambda b, n: (b, jnp.minimum((n + 1) * ATTN_QB, nb - 1), 0)),
                  geom, geom,
                  pl.BlockSpec((1, ATTN_WIDTH), lambda b, n: (0, 0))],
        out_specs=pl.BlockSpec((1, rows, ATTN_WIDTH), lambda b, n: (b, n, 0)),
        out_shape=jax.ShapeDtypeStruct((B, S, ATTN_WIDTH), BF16),
        scratch_shapes=[pltpu.VMEM((EDGE_VARIANTS, N_KV_HEADS, 2, HEAD_PAIRS * BLOCK, 3 * BLOCK), F32)],
        compiler_params=_cparams(2),
        name="attention",
    )(sink.astype(F32) * LOG2E, rel_bias.astype(F32), q, kv, kv, kv, bucket, band, out_gain.reshape(1, ATTN_WIDTH))


LRU_TC = 128
LRU_PITCH = LRU_TC + SUBLANES // 2
LRU_SLABS = LRU_WIDTH // LANES
LRU_UNROLL = 32
HALO = SUBLANES


def _softplus(x):
    return jnp.maximum(x, 0.0) + jnp.log(1.0 + jnp.exp(-jnp.abs(x)))


def _gelu_tanh(x):
    k = math.sqrt(2.0 / math.pi)
    hx = 0.5 * x
    return hx + hx * jnp.tanh(x * (k + (k * 0.044715) * (x * x)))


def _sigmoid(x):
    return 0.5 + 0.5 * jnp.tanh(0.5 * x)


def _rglru_kernel(xr_ref, xp_ref, xn_ref, gr_ref, cw_ref, cb_ref, wa_ref, wi_ref, ba_ref, bi_ref, lam_ref, og_ref,
                  o_ref, sx_ref, a_ref, u_ref, h_ref, carry_ref, hf_ref, xcs_ref, wg_ref, bg_ref, k_ref):
    p = pl.program_id(0)
    i = pl.program_id(1)
    nc = pl.num_programs(1)
    c = i + p * (nc - 1 - 2 * i)
    B = xr_ref.shape[0]
    TC = LRU_TC

    @pl.when(i == 0)
    def _():
        carry_ref[...] = jnp.zeros_like(carry_ref)
        wg_ref[...] = jnp.zeros_like(wg_ref)
        for sel, w_ref in enumerate((wa_ref, wi_ref)):
            for h in range(LRU_BLOCKS):
                lo = h * LRU_BLOCK_DIM
                wg_ref[lo:lo + LRU_BLOCK_DIM, sel * LRU_WIDTH + lo:sel * LRU_WIDTH + lo + LRU_BLOCK_DIM] = (
                    0.5 * w_ref[0, h]).astype(BF16)
        row = pl.ds(p, 1)
        bg_ref[:, :LRU_WIDTH] = 0.5 * ba_ref[row, :]
        bg_ref[:, LRU_WIDTH:] = 0.5 * bi_ref[row, :]
        k_ref[...] = (-0.5 * LRU_C * math.log2(math.e)) * _softplus(-lam_ref[row, :])

    def gates_and_scan(xc2, backward):
        g = jnp.dot(xc2.astype(BF16), wg_ref[...], preferred_element_type=F32) + bg_ref[...]
        ta = jnp.tanh(g[:, :LRU_WIDTH])
        ig = 0.5 + 0.5 * jnp.tanh(g[:, LRU_WIDTH:])
        a = jnp.exp2((1.0 + ta) * k_ref[...])
        z = 1.0 - a * a
        u = z * lax.rsqrt(jnp.maximum(z, 1e-30)) * ig * xc2
        for b in range(B):
            for s in range(LRU_SLABS):
                a_ref[s, b * LRU_PITCH:b * LRU_PITCH + TC, :] = a[b * TC:(b + 1) * TC, s * LANES:(s + 1) * LANES]
                u_ref[s, b * LRU_PITCH:b * LRU_PITCH + TC, :] = u[b * TC:(b + 1) * TC, s * LANES:(s + 1) * LANES]

        def trip(i, hs):
            t0 = pl.multiple_of((TC // LRU_UNROLL - 1 - i if backward else i) * LRU_UNROLL, LRU_UNROLL)
            for j in range(LRU_UNROLL):
                t = t0 + (LRU_UNROLL - 1 - j if backward else j)
                out = []
                for s in range(LRU_SLABS):
                    idx = pl.ds(t, B, stride=LRU_PITCH)
                    hn = a_ref[s, idx, :] * hs[s] + u_ref[s, idx, :]
                    h_ref[s, idx, :] = hn
                    out.append(hn)
                hs = tuple(out)
            return hs

        hs = lax.fori_loop(0, TC // LRU_UNROLL, trip, tuple(carry_ref[s] for s in range(LRU_SLABS)))
        for s in range(LRU_SLABS):
            carry_ref[s] = hs[s]

    @pl.when(p == 0)
    def _():
        sx_ref[:, HALO:HALO + TC, :] = xr_ref[...]
        sx_ref[:, 0:HALO, :] = jnp.where(c > 0, xp_ref[...], 0.0)
        sx_ref[:, HALO + TC:, :] = jnp.where(c < nc - 1, xn_ref[...], 0.0)
        xc = cb_ref[...][None]
        for j in range(CONV_W):
            off = HALO + j - CONV_LEFT
            xc = xc + cw_ref[j:j + 1, :][None] * sx_ref[:, off:off + TC, :]
        xc2 = xc.reshape(B * TC, LRU_WIDTH)
        xcs_ref[c] = xc2.astype(xcs_ref.dtype)
        gates_and_scan(xc2, backward=False)
        for b in range(B):
            for s in range(LRU_SLABS):
                hf_ref[c, s, b * TC:(b + 1) * TC, :] = h_ref[s, b * LRU_PITCH:b * LRU_PITCH + TC, :].astype(hf_ref.dtype)

    @pl.when(p == 1)
    def _():
        gates_and_scan(xcs_ref[c].astype(F32), backward=True)
        for b in range(B):
            hsum = jnp.concatenate(
                [h_ref[s, b * LRU_PITCH:b * LRU_PITCH + TC, :] + hf_ref[c, s, b * TC:(b + 1) * TC, :].astype(F32)
                 for s in range(LRU_SLABS)], axis=1)
            y = hsum * _gelu_tanh(gr_ref[b])
            o_ref[b] = _rms(y, og_ref[...]).astype(o_ref.dtype)


def _rglru(xr, gr, conv_w, conv_b, w_a, b_a, w_i, b_i, lam, out_gain):
    B, S, W = xr.shape
    nc = S // LRU_TC
    hb = LRU_TC // HALO
    fwd = lambda p, i: jnp.where(p == 0, i, nc - 1)
    bwd = lambda p, i: nc - 1 - p * i
    full2 = lambda shape: pl.BlockSpec(shape, lambda p, i: (0,) * len(shape))
    wblock = pl.BlockSpec((1, LRU_BLOCKS, LRU_BLOCK_DIM, LRU_BLOCK_DIM), lambda p, i: (p, 0, 0, 0))
    return pl.pallas_call(
        _rglru_kernel,
        grid=(2, nc),
        in_specs=[pl.BlockSpec((B, LRU_TC, W), lambda p, i: (0, fwd(p, i), 0)),
                  pl.BlockSpec((B, HALO, W), lambda p, i: (0, jnp.maximum(fwd(p, i) * hb - 1, 0), 0)),
                  pl.BlockSpec((B, HALO, W), lambda p, i: (0, jnp.minimum((fwd(p, i) + 1) * hb, S // HALO - 1), 0)),
                  pl.BlockSpec((B, LRU_TC, W), lambda p, i: (0, bwd(p, i), 0)),
                  full2((CONV_W, W)),
                  full2((1, W)),
                  wblock, wblock,
                  full2((2, W)), full2((2, W)), full2((2, W)),
                  full2((1, W))],
        out_specs=pl.BlockSpec((B, LRU_TC, W), lambda p, i: (0, bwd(p, i), 0)),
        out_shape=jax.ShapeDtypeStruct((B, S, W), BF16),
        scratch_shapes=[pltpu.VMEM((B, LRU_TC + 2 * HALO, W), F32),
                        pltpu.VMEM((LRU_SLABS, B * LRU_PITCH, LANES), F32),
                        pltpu.VMEM((LRU_SLABS, B * LRU_PITCH, LANES), F32),
                        pltpu.VMEM((LRU_SLABS, B * LRU_PITCH, LANES), F32),
                        pltpu.VMEM((LRU_SLABS, B, LANES), F32),
                        pltpu.VMEM((nc, LRU_SLABS, B * LRU_TC, LANES), BF16),
                        pltpu.VMEM((nc, B * LRU_TC, W), BF16),
                        pltpu.VMEM((W, 2 * W), BF16),
                        pltpu.VMEM((1, 2 * W), F32),
                        pltpu.VMEM((1, W), F32)],
        compiler_params=_cparams(2, LRU_VMEM_LIMIT),
        name="rglru",
    )(xr, xr, xr, gr, conv_w.astype(F32), conv_b.reshape(1, W).astype(F32), w_a.astype(F32), w_i.astype(F32),
      b_a.astype(F32), b_i.astype(F32), lam.astype(F32), out_gain.reshape(1, W).astype(F32))


RT_TM = 1024
RT_PARTS = 4
RT_COLS = LANES
RT_ROWS = 48
RINFO = SUBLANES


def _split_bf16(x):
    hi = x.astype(BF16)
    lo = (x - hi.astype(F32)).astype(BF16)
    return hi, lo


def _route_kernel(an_ref, ln_ref, x_ref, wo_ref, g2_ref, wr_ref, br_ref,
                  x1_ref, h2_ref, gt_ref, ei_ref, cnt_ref, wob_ref, wrb_ref, tri_ref, run_ref, runc_ref):
    @pl.when(pl.program_id(0) == 0)
    def _():
        wob_ref[...] = wo_ref[...].astype(BF16)
        hi, lo = _split_bf16(wr_ref[...])
        wrb_ref[:RT_ROWS, :] = hi
        wrb_ref[RT_ROWS:, :] = lo
        r = lax.broadcasted_iota(jnp.int32, (RT_TM, RT_TM), 0)
        cidx = lax.broadcasted_iota(jnp.int32, (RT_TM, RT_TM), 1)
        tri_ref[...] = (r < cidx).astype(BF16)
        run_ref[...] = jnp.zeros_like(run_ref)
        runc_ref[...] = jnp.zeros_like(runc_ref)

    nt_dims = (((1,), (1,)), ((), ()))
    part = RT_TM // RT_PARTS
    x1s = []
    for r in range(RT_PARTS):
        rows = slice(r * part, (r + 1) * part)
        x1 = (x_ref[rows, :]
              + jnp.dot(an_ref[rows, :], wob_ref[:ATTN_WIDTH, :], preferred_element_type=F32)
              + jnp.dot(ln_ref[rows, :], wob_ref[ATTN_WIDTH:, :], preferred_element_type=F32))
        x1_ref[rows, :] = x1
        x1s.append(x1)
    splits = []
    for r, x1 in enumerate(x1s):
        h2 = _rms(x1, g2_ref[...])
        hi = h2.astype(BF16)
        hi_f = hi.astype(F32)
        h2_ref[r * part:(r + 1) * part, :] = _pack_rounded(hi_f)
        splits.append((hi, (h2 - hi_f).astype(BF16)))
    logits = []
    for hi, lo in splits:
        t1 = lax.dot_general(wrb_ref[...], hi, nt_dims, preferred_element_type=F32)
        t2 = lax.dot_general(wrb_ref[:RT_ROWS, :], lo, nt_dims, preferred_element_type=F32)
        logits.append(t1[:RT_ROWS] + t1[RT_ROWS:] + t2)
    logit = jnp.concatenate(logits, axis=1) + br_ref[...]

    sub = lax.broadcasted_iota(jnp.int32, (SUBLANES, RT_TM), 0)
    first_min = lambda hit: jnp.min(jnp.where(hit, sub, SUBLANES), axis=0, keepdims=True)
    is_g = sub < N_GROUPS
    gl = jnp.where(is_g, logit[:SUBLANES], -jnp.inf)
    gm = jnp.max(gl, axis=0, keepdims=True)
    gidx = first_min(gl == gm)
    g_p = 1.0 / jnp.sum(jnp.where(is_g, jnp.exp(logit[:SUBLANES] - gm), 0.0), axis=0, keepdims=True)
    el = logit[SUBLANES:2 * SUBLANES]
    for g in range(1, N_GROUPS):
        el = jnp.where(gidx == g, logit[(g + 1) * SUBLANES:(g + 2) * SUBLANES], el)
    m1 = jnp.max(el, axis=0, keepdims=True)
    i1 = first_min(el == m1)
    el2 = jnp.where(sub == i1, -jnp.inf, el)
    m2 = jnp.max(el2, axis=0, keepdims=True)
    i2 = first_min(el2 == m2)
    t = jnp.exp(m2 - m1)
    gate1 = g_p / (1.0 + t)
    gate2 = g_p * t / (1.0 + t)
    e1 = gidx * EXPERTS_PER_GROUP + i1
    e2 = gidx * EXPERTS_PER_GROUP + i2

    erow = lax.broadcasted_iota(jnp.int32, (N_EXPERTS, RT_TM), 0)
    oh1 = erow == e1
    oh2 = erow == e2
    oh = (oh1 | oh2).astype(F32)
    ohb = oh.astype(BF16)
    cum = jnp.dot(ohb, tri_ref[...], preferred_element_type=F32) + runc_ref[...]
    rank1 = jnp.sum(jnp.where(oh1, cum, 0.0), axis=0, keepdims=True)
    rank2 = jnp.sum(jnp.where(oh2, cum, 0.0), axis=0, keepdims=True)
    runc_ref[...] = runc_ref[...] + jnp.sum(oh, axis=1, keepdims=True)
    tile_cnt = lax.dot_general(jnp.ones((SUBLANES, RT_TM), BF16), ohb, nt_dims, preferred_element_type=F32)
    run_ref[:, :N_EXPERTS] = run_ref[:, :N_EXPERTS] + tile_cnt[0:1]
    cnt_ref[...] = run_ref[...].astype(jnp.int32)

    rows = [e1, e2, rank1.astype(jnp.int32), rank2.astype(jnp.int32)]
    ei = jnp.zeros((RINFO, RT_TM), jnp.int32)
    for k, v in enumerate(rows):
        ei = jnp.where(sub == k, v, ei)
    ei_ref[0] = ei
    gt_ref[...] = jnp.where(sub == 0, gate1, jnp.where(sub == 1, gate2, 0.0)).T


def _out_route(attn_n, lru_n, x2, w_out, ln2, w_group, b_group, w_er, b_er):
    T = x2.shape[0]
    pad_g = SUBLANES - N_GROUPS
    wr = jnp.concatenate([jnp.pad(w_group.T, ((0, pad_g), (0, 0))),
                          jnp.transpose(w_er, (0, 2, 1)).reshape(N_EXPERTS, D_MODEL)], axis=0)
    wr = jnp.pad(wr, ((0, RT_ROWS - wr.shape[0]), (0, 0))).astype(F32)
    br = jnp.concatenate([jnp.pad(b_group, (0, pad_g)), b_er.reshape(-1)])
    br = jnp.pad(br, (0, RT_ROWS - br.shape[0])).reshape(RT_ROWS, 1).astype(F32)
    row = lambda w: pl.BlockSpec((RT_TM, w), lambda i: (i, 0))
    const = lambda shape: pl.BlockSpec(shape, lambda i: (0, 0))
    return pl.pallas_call(
        _route_kernel,
        grid=(T // RT_TM,),
        in_specs=[row(ATTN_WIDTH), row(LRU_WIDTH), row(D_MODEL), const((D_MODEL, D_MODEL)), const((1, D_MODEL)),
                  const((RT_ROWS, D_MODEL)), const((RT_ROWS, 1))],
        out_specs=[row(D_MODEL), row(PACKED), row(RINFO),
                   pl.BlockSpec((1, RINFO, RT_TM), lambda i: (i, 0, 0)), const((1, RT_COLS))],
        out_shape=[jax.ShapeDtypeStruct((T, D_MODEL), F32),
                   jax.ShapeDtypeStruct((T, PACKED), U32),
                   jax.ShapeDtypeStruct((T, RINFO), F32),
                   jax.ShapeDtypeStruct((T // RT_TM, RINFO, RT_TM), jnp.int32),
                   jax.ShapeDtypeStruct((1, RT_COLS), jnp.int32)],
        scratch_shapes=[pltpu.VMEM((D_MODEL, D_MODEL), BF16),
                        pltpu.VMEM((2 * RT_ROWS, D_MODEL), BF16),
                        pltpu.VMEM((RT_TM, RT_TM), BF16),
                        pltpu.VMEM((1, RT_COLS), F32),
                        pltpu.VMEM((N_EXPERTS, 1), F32)],
        compiler_params=_cparams(1),
        name="out_route",
    )(attn_n, lru_n, x2, w_out, ln2.reshape(1, D_MODEL).astype(F32), wr, br)


def _moe_cap(T):
    A = T * TOP_K
    return ((A + MOE_BLOCK - 1) // MOE_BLOCK) * MOE_BLOCK + N_EXPERTS * MOE_BLOCK


PAD_BITS = tuple(1 << b for b in reversed(range(3, MOE_BLOCK.bit_length() - 1)))


def _layout_kernel(cnt_ref, ei_ref, dest_ref, pstart, be_ref, nu_ref, ge_ref):
    n_blocks = be_ref.shape[0]

    def lay(e, carry):
        start, blk, grp = carry
        pstart[e] = start
        nb = (cnt_ref[0, e] + MOE_BLOCK - 1) // MOE_BLOCK
        ge_ref[grp] = e

        def fill(k, c):
            be_ref[blk + k] = e
            return c
        lax.fori_loop(0, nb, fill, 0)
        return start + nb * MOE_BLOCK, blk + nb, grp + (nb > 0).astype(jnp.int32)
    _, used, groups = lax.fori_loop(0, N_EXPERTS, lay, (jnp.int32(0), jnp.int32(0), jnp.int32(0)))
    nu_ref[0] = used

    def tail(k, c):
        be_ref[k] = N_EXPERTS - 1
        return c
    lax.fori_loop(used, n_blocks, tail, 0)

    def no_group(k, c):
        ge_ref[k] = -1
        return c
    lax.fori_loop(groups, ge_ref.shape[0], no_group, 0)

    expert = ei_ref[:, 0:TOP_K, :]
    dest = ei_ref[:, TOP_K:2 * TOP_K, :]
    for e in range(N_EXPERTS):
        dest = dest + jnp.where(expert == e, pstart[e], 0)
    dest_ref[...] = dest


def _layout(ei, cnt, n_blocks):
    nt = ei.shape[0]
    smem = pl.BlockSpec(memory_space=pltpu.SMEM)
    vmem = pl.BlockSpec(memory_space=pltpu.VMEM)
    return pl.pallas_call(
        _layout_kernel,
        in_specs=[smem, vmem],
        out_specs=[vmem, smem, smem, smem, smem],
        out_shape=[jax.ShapeDtypeStruct((nt, TOP_K, RT_TM), jnp.int32),
                   jax.ShapeDtypeStruct((N_EXPERTS,), jnp.int32),
                   jax.ShapeDtypeStruct((n_blocks,), jnp.int32),
                   jax.ShapeDtypeStruct((1,), jnp.int32),
                   jax.ShapeDtypeStruct((N_EXPERTS + W_AHEAD,), jnp.int32)],
        name="layout",
    )(cnt, ei)


SC_CHUNK = 64
SC_BUFS = 3
SC_LEAD = SC_BUFS - 1


def _sc_workers():
    info = plsc.get_sparse_core_info()
    return info.num_cores, info.num_subcores


def _sc_ring(n_chunks, read, write):
    for c in range(min(SC_LEAD, n_chunks)):
        for cp in read(c):
            cp.start()
    reclaimed = set()
    for c in range(n_chunks):
        for cp in read(c):
            cp.wait()
        for cp in write(c):
            cp.start()
        nxt = c + SC_LEAD
        if nxt < n_chunks:
            if nxt - SC_BUFS >= 0:
                for cp in write(nxt - SC_BUFS):
                    cp.wait()
                reclaimed.add(nxt - SC_BUFS)
            for cp in read(nxt):
                cp.start()
    for c in range(n_chunks):
        if c not in reclaimed:
            for cp in write(c):
                cp.wait()


def _sc_dispatch(h2p, dest, cap):
    T = h2p.shape[0]
    nc, ns = _sc_workers()
    per_w = T // (nc * ns)
    n_ch = per_w // SC_CHUNK
    nt, _, tm = dest.shape
    assert nt * tm == T and tm % per_w == 0 and per_w % SC_CHUNK == 0
    idx = dest.reshape(nt, TOP_K, tm // per_w, per_w).transpose(0, 2, 1, 3).reshape(nc * ns, TOP_K * n_ch, SC_CHUNK)
    mesh = plsc.VectorSubcoreMesh(core_axis_name="c", subcore_axis_name="s")

    @functools.partial(
        pl.kernel, mesh=mesh,
        out_type=jax.ShapeDtypeStruct((cap, PACKED), U32),
        scratch_types=[pltpu.VMEM((TOP_K * n_ch, SC_CHUNK), jnp.int32),
                       pltpu.VMEM((SC_BUFS, SC_CHUNK, PACKED), U32),
                       pltpu.SemaphoreType.DMA((SC_BUFS,)),
                       pltpu.SemaphoreType.DMA((SC_BUFS,))])
    def scatter(src_hbm, idx_hbm, out_hbm, idx_v, rows_v, rsem, wsem):
        wid = lax.axis_index("s") * nc + lax.axis_index("c")
        base = pl.multiple_of(wid * per_w, per_w)
        pltpu.sync_copy(idx_hbm.at[wid], idx_v)

        def read(c):
            b = c % SC_BUFS
            return [pltpu.make_async_copy(src_hbm.at[pl.ds(base + c * SC_CHUNK, SC_CHUNK)], rows_v.at[b], rsem.at[b])]

        def write(c):
            b = c % SC_BUFS
            return [pltpu.make_async_copy(rows_v.at[b], out_hbm.at[idx_v.at[k * n_ch + c]], wsem.at[b])
                    for k in range(TOP_K)]
        _sc_ring(n_ch, read, write)

    return scatter(h2p, idx)


def _sc_gather(yb, dest):
    nt, _, tm = dest.shape
    nc, ns = _sc_workers()
    n_rows = nt * TOP_K * tm
    per_w = n_rows // (nc * ns)
    n_ch = per_w // SC_CHUNK
    assert per_w * nc * ns == n_rows and per_w % SC_CHUNK == 0
    mesh = plsc.VectorSubcoreMesh(core_axis_name="c", subcore_axis_name="s")

    @functools.partial(
        pl.kernel, mesh=mesh,
        out_type=jax.ShapeDtypeStruct((n_rows, PACKED), U32),
        scratch_types=[pltpu.VMEM((per_w,), jnp.int32),
                       pltpu.VMEM((SC_BUFS, SC_CHUNK, PACKED), U32),
                       pltpu.SemaphoreType.DMA((SC_BUFS,)),
                       pltpu.SemaphoreType.DMA((SC_BUFS,))])
    def gather(table_hbm, idx_hbm, out_hbm, idx_v, rows_v, gsem, wsem):
        wid = lax.axis_index("s") * nc + lax.axis_index("c")
        base = pl.multiple_of(wid * per_w, per_w)
        pltpu.sync_copy(idx_hbm.at[pl.ds(base, per_w)], idx_v)

        def read(c):
            b = c % SC_BUFS
            return [pltpu.make_async_copy(table_hbm.at[idx_v.at[pl.ds(c * SC_CHUNK, SC_CHUNK)]], rows_v.at[b], gsem.at[b])]

        def write(c):
            b = c % SC_BUFS
            return [pltpu.make_async_copy(rows_v.at[b], out_hbm.at[pl.ds(base + c * SC_CHUNK, SC_CHUNK)], wsem.at[b])]
        _sc_ring(n_ch, read, write)

    return gather(yb, dest.reshape(n_rows)).reshape(nt, TOP_K, tm, PACKED)


def _padfill_kernel(cnt_ref, pstart, xs_in, xs_ref, zeros, zsem):
    del xs_in

    def pad_copies(fn):
        for e in range(N_EXPERTS):
            cnt = cnt_ref[0, e]
            head = (-cnt) & (SUBLANES - 1)
            rest = ((-cnt) & (MOE_BLOCK - 1)) - head
            off = pstart[e] + cnt
            for k in range(SUBLANES - 1):
                @pl.when(k < head)
                def _(off=off, k=k):
                    fn(pltpu.make_async_copy(zeros.at[pl.ds(0, 1), :], xs_ref.at[pl.ds(off + k, 1), :], zsem))
            off = off + head
            for bit in PAD_BITS:
                @pl.when((rest & bit) != 0)
                def _(off=off, bit=bit):
                    fn(pltpu.make_async_copy(zeros.at[pl.ds(0, bit), :],
                                             xs_ref.at[pl.ds(pl.multiple_of(off, SUBLANES), bit), :], zsem))
                off = off + (rest & bit)

    zeros[...] = jnp.zeros_like(zeros)
    pad_copies(lambda cp: cp.start())
    pad_copies(lambda cp: cp.wait())


def _padfill(xs, pstart, cnt):
    smem = pl.BlockSpec(memory_space=pltpu.SMEM)
    hbm = pl.BlockSpec(memory_space=pl.ANY)
    return pl.pallas_call(
        _padfill_kernel,
        in_specs=[smem, smem, hbm],
        out_specs=hbm,
        out_shape=jax.ShapeDtypeStruct(xs.shape, xs.dtype),
        input_output_aliases={2: 0},
        scratch_shapes=[pltpu.VMEM((MOE_BLOCK // 2, PACKED), U32), pltpu.SemaphoreType.DMA(())],
        name="padfill",
    )(cnt, pstart, xs)


W_SLOTS = 3
W_AHEAD = W_SLOTS - 1
EXPERT_GROUP = 8
EXPERT_RUNS = (1, 2, 4)


def _expert_kernel(be_ref, nu_ref, ge_ref, x_ref, wg_hbm, wu_hbm, wd_hbm, o_ref,
                   wgf, wuf, wdf, grp_ref, sems):
    step = pl.program_id(0)

    def weight_copies(e, slot):
        return (pltpu.make_async_copy(wg_hbm.at[e], wgf.at[slot], sems.at[slot, 0]),
                pltpu.make_async_copy(wu_hbm.at[e], wuf.at[slot], sems.at[slot, 1]),
                pltpu.make_async_copy(wd_hbm.at[e], wdf.at[slot], sems.at[slot, 2]))

    @pl.when(step == 0)
    def _():
        grp_ref[0] = 0
        for a in range(W_AHEAD):
            @pl.when(ge_ref[a] >= 0)
            def _(a=a):
                for cp in weight_copies(ge_ref[a], a):
                    cp.start()

    n_blocks = be_ref.shape[0]
    n_used = nu_ref[0]

    def swiglu(s, n, slot):
        rows = pl.ds(pl.multiple_of(s * MOE_BLOCK, MOE_BLOCK), n * MOE_BLOCK)
        lo, hi = _unpack_rows(x_ref[rows, :])
        lo = lo.astype(BF16)
        hi = hi.astype(BF16)
        g = (jnp.dot(lo, wgf[slot, :PACKED, :], preferred_element_type=F32)
             + jnp.dot(hi, wgf[slot, PACKED:, :], preferred_element_type=F32))
        u = (jnp.dot(lo, wuf[slot, :PACKED, :], preferred_element_type=F32)
             + jnp.dot(hi, wuf[slot, PACKED:, :], preferred_element_type=F32))
        h = (g * _sigmoid(g) * u).astype(BF16)
        o_ref[rows, :] = _pack_rows(jnp.dot(h, wdf[slot], preferred_element_type=F32))

    def run(s):
        j = step * EXPERT_GROUP + s
        e = be_ref[j]
        first = jnp.logical_or(j == 0, e != be_ref[jnp.maximum(j - 1, 0)])

        @pl.when(first)
        def _():
            grp = grp_ref[0]
            slot = grp % W_SLOTS
            for cp in weight_copies(e, slot):
                cp.wait()
            nxt = ge_ref[grp + W_AHEAD]

            @pl.when(nxt >= 0)
            def _():
                for cp in weight_copies(nxt, (grp + W_AHEAD) % W_SLOTS):
                    cp.start()
            grp_ref[0] = grp + 1

        def same(k):
            return (s + k < EXPERT_GROUP) & (j + k < n_used) & (be_ref[jnp.minimum(j + k, n_blocks - 1)] == e)
        take = jnp.int32(1)
        for n in EXPERT_RUNS[1:]:
            ok = same(n - 1)
            for k in range(1, n - 1):
                ok = ok & same(k)
            take = jnp.where(ok, n, take)
        slot = (grp_ref[0] + W_SLOTS - 1) % W_SLOTS
        for n in EXPERT_RUNS:
            @pl.when(take == n)
            def _(n=n):
                swiglu(s, n, slot)
        return s + take

    lax.while_loop(lambda s: (s < EXPERT_GROUP) & (step * EXPERT_GROUP + s < n_used), run, jnp.int32(0))


def _experts(xs, block_expert, n_used, group_expert, w_gate, w_up, w_down):
    cap = xs.shape[0]
    n_blocks = cap // MOE_BLOCK
    assert n_blocks % EXPERT_GROUP == 0
    rows = EXPERT_GROUP * MOE_BLOCK
    last = lambda j, be, nu, ge: jnp.minimum(j, (nu[0] - 1) // EXPERT_GROUP)
    hbm = pl.BlockSpec(memory_space=pl.ANY)
    gs = pltpu.PrefetchScalarGridSpec(
        num_scalar_prefetch=3,
        grid=(n_blocks // EXPERT_GROUP,),
        in_specs=[pl.BlockSpec((rows, PACKED), lambda j, be, nu, ge: (last(j, be, nu, ge), 0)), hbm, hbm, hbm],
        out_specs=pl.BlockSpec((rows, PACKED), lambda j, be, nu, ge: (last(j, be, nu, ge), 0)),
        scratch_shapes=[pltpu.VMEM((W_SLOTS, D_MODEL, D_EXPERT), F32),
                        pltpu.VMEM((W_SLOTS, D_MODEL, D_EXPERT), F32),
                        pltpu.VMEM((W_SLOTS, D_EXPERT, D_MODEL), F32),
                        pltpu.SMEM((1,), jnp.int32),
                        pltpu.SemaphoreType.DMA((W_SLOTS, 3))],
    )
    return pl.pallas_call(
        _expert_kernel,
        grid_spec=gs,
        out_shape=jax.ShapeDtypeStruct((cap, PACKED), U32),
        compiler_params=_cparams(1),
        name="experts",
    )(block_expert, n_used, group_expert, xs, w_gate, w_up, w_down)


CB_TM = RT_TM


def _combine_kernel(x1_ref, gt_ref, y2_ref, o_ref):
    g = gt_ref[...]
    lo1, hi1 = _unpack_rows(y2_ref[0, 0])
    lo2, hi2 = _unpack_rows(y2_ref[0, 1])
    o_ref[:, :PACKED] = x1_ref[:, :PACKED] + g[:, 0:1] * lo1 + g[:, 1:2] * lo2
    o_ref[:, PACKED:] = x1_ref[:, PACKED:] + g[:, 0:1] * hi1 + g[:, 1:2] * hi2


def _combine(x1, gates, y2):
    T = x1.shape[0]
    nt = T // CB_TM
    return pl.pallas_call(
        _combine_kernel,
        grid=(nt,),
        in_specs=[pl.BlockSpec((CB_TM, D_MODEL), lambda i: (i, 0)),
                  pl.BlockSpec((CB_TM, RINFO), lambda i: (i, 0)),
                  pl.BlockSpec((1, TOP_K, CB_TM, PACKED), lambda i: (i, 0, 0, 0))],
        out_specs=pl.BlockSpec((CB_TM, D_MODEL), lambda i: (i, 0)),
        out_shape=jax.ShapeDtypeStruct((T, D_MODEL), F32),
        compiler_params=_cparams(1),
        name="combine",
    )(x1, gates, y2)


def _layer(x, rel_bias, ln1, w_in, q_norm, k_norm, attn_sink, conv_w, conv_b, lru_wa, lru_ba, lru_wi, lru_bi,
           lru_lambda, out_norm_attn, out_norm_lru, w_out, ln2, w_group, b_group, w_er, b_er, w_gate, w_up, w_down):
    B, S, D = x.shape
    T = B * S
    x2 = x.reshape(T, D)
    q, kv, xr, gr = _in_proj(x2, ln1, w_in, q_norm, k_norm)
    attn_n = _attention(q.reshape(B, S, ATTN_WIDTH), kv.reshape(B, S, 2 * KV_WIDTH), rel_bias, attn_sink,
                        out_norm_attn)
    lru_n = _rglru(xr.reshape(B, S, LRU_WIDTH), gr.reshape(B, S, LRU_WIDTH), conv_w, conv_b,
                   lru_wa, lru_ba, lru_wi, lru_bi, lru_lambda, out_norm_lru)
    x1, h2, gates, ei, cnt = _out_route(attn_n.reshape(T, ATTN_WIDTH), lru_n.reshape(T, LRU_WIDTH), x2, w_out, ln2,
                                        w_group, b_group, w_er, b_er)
    cap = _moe_cap(T)
    dest, pstart, block_expert, n_used, group_expert = _layout(ei, cnt, cap // MOE_BLOCK)
    xs = _padfill(_sc_dispatch(h2, dest, cap), pstart, cnt)
    yb = _experts(xs, block_expert, n_used, group_expert, w_gate, w_up, w_down)
    out = _combine(x1, gates, _sc_gather(yb, dest))
    return out.reshape(B, S, D)


def kernel(x, rel_bias, ln1, w_in, q_norm, k_norm, attn_sink, conv_w, conv_b, lru_wa, lru_ba, lru_wi, lru_bi,
           lru_lambda, out_norm_attn, out_norm_lru, w_out, ln2, w_group, b_group, w_expert_router, b_expert_router,
           w_gate, w_up, w_down):
    depth = ln1.shape[0]
    for l in range(depth):
        x = _layer(x, rel_bias, ln1[l], w_in[l], q_norm[l], k_norm[l], attn_sink[l], conv_w[l], conv_b[l],
                   lru_wa[l], lru_ba[l], lru_wi[l], lru_bi[l], lru_lambda[l], out_norm_attn[l], out_norm_lru[l],
                   w_out[l], ln2[l], w_group[l], b_group[l], w_expert_router[l], b_expert_router[l],
                   w_gate[l], w_up[l], w_down[l])
    return x
```

```python
import functools
import math

import jax
import jax.numpy as jnp
from jax import lax
from jax.experimental import pallas as pl
from jax.experimental.pallas import tpu as pltpu
from jax.experimental.pallas import tpu_sc as plsc

D_MODEL = 1024
N_HEADS = 8
N_KV_HEADS = 2
HEAD_DIM = 64
Q_PER_KV = N_HEADS // N_KV_HEADS
ATTN_WIDTH = N_HEADS * HEAD_DIM
KV_WIDTH = N_KV_HEADS * HEAD_DIM
WINDOW = 128
BLOCK = 128
NUM_BUCKETS = 32
MAX_DISTANCE = 128
LRU_WIDTH = D_MODEL - ATTN_WIDTH
LRU_BLOCKS = 8
LRU_BLOCK_DIM = LRU_WIDTH // LRU_BLOCKS
LRU_C = 8.0
CONV_W = 4
CONV_LEFT = 2
N_GROUPS = 4
EXPERTS_PER_GROUP = 8
N_EXPERTS = N_GROUPS * EXPERTS_PER_GROUP
TOP_K = 2
D_EXPERT = 512
MOE_BLOCK = 256
EPS = 1e-6
NEG_INF = -1e30

LANES = 128
SUBLANES = 8
VMEM_LIMIT = 56 * 1024 * 1024
LRU_VMEM_LIMIT = 62 * 1024 * 1024

F32 = jnp.float32
BF16 = jnp.bfloat16
LOG2E = math.log2(math.e)


def _cparams(n_axes, vmem=VMEM_LIMIT):
    return pltpu.CompilerParams(dimension_semantics=("arbitrary",) * n_axes, vmem_limit_bytes=vmem)


def _rms(x, gain):
    return x * lax.rsqrt(jnp.mean(x * x, axis=-1, keepdims=True) + EPS) * gain


U32 = jnp.uint32
HI_MASK = 0xFFFF0000
PACKED = D_MODEL // 2


def _pack_rows(x):
    return _pack_rounded(x.astype(BF16).astype(F32))


def _pack_rounded(xb):
    h = xb.shape[1] // 2
    lo = lax.bitcast_convert_type(xb[:, :h], U32) >> 16
    hi = lax.bitcast_convert_type(xb[:, h:], U32) & jnp.uint32(HI_MASK)
    return lo | hi


def _unpack_rows(p):
    lo = lax.bitcast_convert_type(p << 16, F32)
    hi = lax.bitcast_convert_type(p & jnp.uint32(HI_MASK), F32)
    return lo, hi


IN_TM = 1024


def _head_rms(x, n_heads, gain):
    head = lax.broadcasted_iota(jnp.int32, (1, n_heads * HEAD_DIM), 1) // HEAD_DIM
    x2 = x * x
    scale = jnp.zeros_like(x)
    for h in range(n_heads):
        ms = jnp.sum(jnp.where(head == h, x2, 0.0), axis=-1, keepdims=True) * (1.0 / HEAD_DIM)
        scale = jnp.where(head == h, lax.rsqrt(ms + EPS), scale)
    return x * scale * gain


def _in_proj_kernel(x_ref, g_ref, w_ref, qn_ref, kn_ref, q_ref, kv_ref, xr_ref, gr_ref, wb_ref, qg_ref, kg_ref):
    @pl.when(pl.program_id(0) == 0)
    def _():
        wb_ref[...] = w_ref[...].astype(BF16)
        qg_ref[...] = jnp.concatenate([qn_ref[...]] * N_HEADS, axis=1) * (HEAD_DIM ** -0.5 * LOG2E)
        kg_ref[...] = jnp.concatenate([kn_ref[...]] * N_KV_HEADS, axis=1)

    h = _rms(x_ref[...], g_ref[...]).astype(BF16)
    c_k = ATTN_WIDTH
    c_v = c_k + KV_WIDTH
    c_x = c_v + KV_WIDTH
    c_g = c_x + LRU_WIDTH
    q = jnp.dot(h, wb_ref[:, :c_k], preferred_element_type=F32)
    q_ref[...] = _head_rms(q, N_HEADS, qg_ref[...]).astype(BF16)
    k = jnp.dot(h, wb_ref[:, c_k:c_v], preferred_element_type=F32)
    kv_ref[:, :KV_WIDTH] = _head_rms(k, N_KV_HEADS, kg_ref[...]).astype(BF16)
    kv_ref[:, KV_WIDTH:] = jnp.dot(h, wb_ref[:, c_v:c_x], preferred_element_type=F32).astype(BF16)
    xr_ref[...] = jnp.dot(h, wb_ref[:, c_x:c_g], preferred_element_type=F32)
    gr_ref[...] = jnp.dot(h, wb_ref[:, c_g:], preferred_element_type=F32)


def _in_proj(x2, ln1, w_in, q_gain, k_gain):
    T = x2.shape[0]
    n_in = w_in.shape[1]
    row = lambda w: pl.BlockSpec((IN_TM, w), lambda i: (i, 0))
    return pl.pallas_call(
        _in_proj_kernel,
        grid=(T // IN_TM,),
        in_specs=[row(D_MODEL),
                  pl.BlockSpec((1, D_MODEL), lambda i: (0, 0)),
                  pl.BlockSpec((D_MODEL, n_in), lambda i: (0, 0)),
                  pl.BlockSpec((1, HEAD_DIM), lambda i: (0, 0)),
                  pl.BlockSpec((1, HEAD_DIM), lambda i: (0, 0))],
        out_specs=[row(ATTN_WIDTH), row(2 * KV_WIDTH), row(LRU_WIDTH), row(LRU_WIDTH)],
        out_shape=[jax.ShapeDtypeStruct((T, ATTN_WIDTH), BF16),
                   jax.ShapeDtypeStruct((T, 2 * KV_WIDTH), BF16),
                   jax.ShapeDtypeStruct((T, LRU_WIDTH), F32),
                   jax.ShapeDtypeStruct((T, LRU_WIDTH), F32)],
        scratch_shapes=[pltpu.VMEM((D_MODEL, n_in), BF16),
                        pltpu.VMEM((1, ATTN_WIDTH), F32),
                        pltpu.VMEM((1, KV_WIDTH), F32)],
        compiler_params=_cparams(1),
        name="in_proj",
    )(x2, ln1.reshape(1, D_MODEL), w_in, q_gain.reshape(1, HEAD_DIM).astype(F32),
      k_gain.reshape(1, HEAD_DIM).astype(F32))


def _t5_bucket(rel):
    half = NUM_BUCKETS // 2
    max_exact = half // 2
    base = jnp.where(rel > 0, half, 0)
    n = jnp.abs(rel)
    nf = jnp.maximum(n, 1).astype(jnp.float32)
    large = max_exact + (jnp.log(nf / max_exact) / math.log(MAX_DISTANCE / max_exact)
                         * (half - max_exact)).astype(jnp.int32)
    large = jnp.minimum(large, half - 1)
    return base + jnp.where(n < max_exact, n, large)


HEAD_PAIRS = Q_PER_KV // 2
EDGE_VARIANTS = 3


def _fill_bias_table(rb_ref, bucket_ref, band_ref, o_ref):
    bucket = bucket_ref[...]
    band = band_ref[...] > 0
    col = lax.broadcasted_iota(jnp.int32, bucket.shape, 1)
    valid = (band & (col >= BLOCK), band, band & (col < 2 * BLOCK))
    for h in range(N_HEADS):
        acc = jnp.zeros(bucket.shape, F32)
        for b in range(NUM_BUCKETS):
            acc = jnp.where(bucket == b, rb_ref[b, h], acc)
        kv, g = divmod(h, Q_PER_KV)
        pair, parity = divmod(g, 2)
        for var in range(EDGE_VARIANTS):
            o_ref[var, kv, parity, pair * BLOCK:(pair + 1) * BLOCK, :] = jnp.where(valid[var], acc * LOG2E, NEG_INF)


def _attn_kernel(sink_ref, rb_ref, q_ref, kp_ref, kc_ref, kn_ref, bucket_ref, band_ref, og_ref, o_ref, bias_ref):
    n = pl.program_id(1)

    @pl.when((pl.program_id(0) == 0) & (n == 0))
    def _():
        _fill_bias_table(rb_ref, bucket_ref, band_ref, bias_ref)

    kv_all = jnp.concatenate([kp_ref[0], kc_ref[0], kn_ref[0]], axis=0)
    for qb in range(ATTN_QB):
        variant = 1
        if qb == 0:
            variant = jnp.where(n == 0, 0, 1)
        if qb == ATTN_QB - 1:
            variant = jnp.where(n == pl.num_programs(1) - 1, 2, variant)
        out = _attn_block(q_ref[0, qb * BLOCK:(qb + 1) * BLOCK, :], kv_all[qb * BLOCK:(qb + 3) * BLOCK, :],
                          lambda kv, parity: bias_ref[variant, kv, parity], sink_ref)
        o_ref[0, qb * BLOCK:(qb + 1) * BLOCK, :] = _rms(out, og_ref[...]).astype(o_ref.dtype)


def _attn_block(q, kvw, bias, sink_ref):
    low = lax.broadcasted_iota(jnp.int32, (1, LANES), 1) < HEAD_DIM
    swap = lambda slab: pltpu.roll(slab.astype(F32), HEAD_DIM, 1).astype(BF16)
    kslab, vslab = kvw[:, :KV_WIDTH], kvw[:, KV_WIDTH:]
    kslab_sw, vslab_sw = swap(kslab), swap(vslab)
    rowi = lax.broadcasted_iota(jnp.int32, (HEAD_PAIRS * BLOCK, 1), 0)
    combos = [(kv, parity) for kv in range(N_KV_HEADS) for parity in range(2)]
    scores, vzs, sinks = [], [], []
    for kv, parity in combos:
        ks, vs = (kslab, vslab) if (kv == 0) == (parity == 0) else (kslab_sw, vslab_sw)
        keep = low if parity == 0 else jnp.logical_not(low)
        kz = jnp.where(keep, ks, jnp.zeros_like(ks))
        vzs.append(jnp.where(keep, vs, jnp.zeros_like(vs)))
        base = kv * Q_PER_KV * HEAD_DIM
        qpair = jnp.concatenate([q[:, base + j * LANES:base + (j + 1) * LANES] for j in range(HEAD_PAIRS)], axis=0)
        s = lax.dot_general(qpair, kz, (((1,), (1,)), ((), ())), preferred_element_type=F32)
        scores.append(s + bias(kv, parity))
        sink = jnp.zeros((HEAD_PAIRS * BLOCK, 1), F32)
        for j in range(HEAD_PAIRS):
            sink = jnp.where(rowi // BLOCK == j, sink_ref[kv * Q_PER_KV + 2 * j + parity], sink)
        sinks.append(sink)
    probs, inv = [], []
    for s, sink in zip(scores, sinks):
        m = jnp.maximum(jnp.max(s, axis=-1, keepdims=True), sink)
        p = jnp.exp2(s - m)
        inv.append(1.0 / (jnp.sum(p, axis=-1, keepdims=True) + jnp.exp2(sink - m)))
        probs.append(p.astype(BF16))
    outs = [jnp.dot(p, vz, preferred_element_type=F32) * r for p, vz, r in zip(probs, vzs, inv)]
    cols = []
    for kv in range(N_KV_HEADS):
        acc = outs[2 * kv] + outs[2 * kv + 1]
        cols += [acc[j * BLOCK:(j + 1) * BLOCK, :] for j in range(HEAD_PAIRS)]
    return jnp.concatenate(cols, axis=1)


ATTN_QB = 8


def _attention(q, kv, rel_bias, sink, out_gain):
    B, S, _ = q.shape
    nb = S // BLOCK
    assert ATTN_QB >= 2 and nb % ATTN_QB == 0, "a step's first and last query blocks must be distinct"
    ns = nb // ATTN_QB
    rows = ATTN_QB * BLOCK
    qi = jnp.arange(BLOCK, dtype=jnp.int32)
    kj = jnp.arange(3 * BLOCK, dtype=jnp.int32)
    rel = kj[None, :] - BLOCK - qi[:, None]
    bucket = _t5_bucket(rel).astype(jnp.int32)
    band = (jnp.abs(rel) <= WINDOW).astype(jnp.int32)
    kvspec = lambda f: pl.BlockSpec((1, BLOCK, 2 * KV_WIDTH), f)
    smem = pl.BlockSpec(memory_space=pltpu.SMEM)
    geom = pl.BlockSpec((BLOCK, 3 * BLOCK), lambda b, n: (0, 0))
    return pl.pallas_call(
        _attn_kernel,
        grid=(B, ns),
        in_specs=[smem, smem,
                  pl.BlockSpec((1, rows, ATTN_WIDTH), lambda b, n: (b, n, 0)),
                  kvspec(lambda b, n: (b, jnp.maximum(n * ATTN_QB - 1, 0), 0)),
                  pl.BlockSpec((1, rows, 2 * KV_WIDTH), lambda b, n: (b, n, 0)),
                  kvspec(lambda b, n: (b, jnp.minimum((n + 1) * ATTN_QB, nb - 1), 0)),
                  geom, geom,
                  pl.BlockSpec((1, ATTN_WIDTH), lambda b, n: (0, 0))],
        out_specs=pl.BlockSpec((1, rows, ATTN_WIDTH), lambda b, n: (b, n, 0)),
        out_shape=jax.ShapeDtypeStruct((B, S, ATTN_WIDTH), BF16),
        scratch_shapes=[pltpu.VMEM((EDGE_VARIANTS, N_KV_HEADS, 2, HEAD_PAIRS * BLOCK, 3 * BLOCK), F32)],
        compiler_params=_cparams(2),
        name="attention",
    )(sink.astype(F32) * LOG2E, rel_bias.astype(F32), q, kv, kv, kv, bucket, band, out_gain.reshape(1, ATTN_WIDTH))


LRU_TC = 128
LRU_PITCH = LRU_TC + SUBLANES // 2
LRU_SLABS = LRU_WIDTH // LANES
LRU_UNROLL = 32
HALO = SUBLANES


def _softplus(x):
    return jnp.maximum(x, 0.0) + jnp.log(1.0 + jnp.exp(-jnp.abs(x)))


def _gelu_tanh(x):
    k = math.sqrt(2.0 / math.pi)
    hx = 0.5 * x
    return hx + hx * jnp.tanh(x * (k + (k * 0.044715) * (x * x)))


def _sigmoid(x):
    return 0.5 + 0.5 * jnp.tanh(0.5 * x)


def _rglru_kernel(xr_ref, xp_ref, xn_ref, gr_ref, cw_ref, cb_ref, wa_ref, wi_ref, ba_ref, bi_ref, lam_ref, og_ref,
                  o_ref, sx_ref, a_ref, u_ref, h_ref, carry_ref, hf_ref, xcs_ref, wg_ref, bg_ref, k_ref):
    p = pl.program_id(0)
    i = pl.program_id(1)
    nc = pl.num_programs(1)
    c = i + p * (nc - 1 - 2 * i)
    B = xr_ref.shape[0]
    TC = LRU_TC

    @pl.when(i == 0)
    def _():
        carry_ref[...] = jnp.zeros_like(carry_ref)
        wg_ref[...] = jnp.zeros_like(wg_ref)
        for sel, w_ref in enumerate((wa_ref, wi_ref)):
            for h in range(LRU_BLOCKS):
                lo = h * LRU_BLOCK_DIM
                wg_ref[lo:lo + LRU_BLOCK_DIM, sel * LRU_WIDTH + lo:sel * LRU_WIDTH + lo + LRU_BLOCK_DIM] = (
                    0.5 * w_ref[0, h]).astype(BF16)
        row = pl.ds(p, 1)
        bg_ref[:, :LRU_WIDTH] = 0.5 * ba_ref[row, :]
        bg_ref[:, LRU_WIDTH:] = 0.5 * bi_ref[row, :]
        k_ref[...] = (-0.5 * LRU_C * math.log2(math.e)) * _softplus(-lam_ref[row, :])

    def gates_and_scan(xc2, backward):
        g = jnp.dot(xc2.astype(BF16), wg_ref[...], preferred_element_type=F32) + bg_ref[...]
        ta = jnp.tanh(g[:, :LRU_WIDTH])
        ig = 0.5 + 0.5 * jnp.tanh(g[:, LRU_WIDTH:])
        a = jnp.exp2((1.0 + ta) * k_ref[...])
        z = 1.0 - a * a
        u = z * lax.rsqrt(jnp.maximum(z, 1e-30)) * ig * xc2
        for b in range(B):
            for s in range(LRU_SLABS):
                a_ref[s, b * LRU_PITCH:b * LRU_PITCH + TC, :] = a[b * TC:(b + 1) * TC, s * LANES:(s + 1) * LANES]
                u_ref[s, b * LRU_PITCH:b * LRU_PITCH + TC, :] = u[b * TC:(b + 1) * TC, s * LANES:(s + 1) * LANES]

        def trip(i, hs):
            t0 = pl.multiple_of((TC // LRU_UNROLL - 1 - i if backward else i) * LRU_UNROLL, LRU_UNROLL)
            for j in range(LRU_UNROLL):
                t = t0 + (LRU_UNROLL - 1 - j if backward else j)
                out = []
                for s in range(LRU_SLABS):
                    idx = pl.ds(t, B, stride=LRU_PITCH)
                    hn = a_ref[s, idx, :] * hs[s] + u_ref[s, idx, :]
                    h_ref[s, idx, :] = hn
                    out.append(hn)
                hs = tuple(out)
            return hs

        hs = lax.fori_loop(0, TC // LRU_UNROLL, trip, tuple(carry_ref[s] for s in range(LRU_SLABS)))
        for s in range(LRU_SLABS):
            carry_ref[s] = hs[s]

    @pl.when(p == 0)
    def _():
        sx_ref[:, HALO:HALO + TC, :] = xr_ref[...]
        sx_ref[:, 0:HALO, :] = jnp.where(c > 0, xp_ref[...], 0.0)
        sx_ref[:, HALO + TC:, :] = jnp.where(c < nc - 1, xn_ref[...], 0.0)
        xc = cb_ref[...][None]
        for j in range(CONV_W):
            off = HALO + j - CONV_LEFT
            xc = xc + cw_ref[j:j + 1, :][None] * sx_ref[:, off:off + TC, :]
        xc2 = xc.reshape(B * TC, LRU_WIDTH)
        xcs_ref[c] = xc2.astype(xcs_ref.dtype)
        gates_and_scan(xc2, backward=False)
        for b in range(B):
            for s in range(LRU_SLABS):
                hf_ref[c, s, b * TC:(b + 1) * TC, :] = h_ref[s, b * LRU_PITCH:b * LRU_PITCH + TC, :].astype(hf_ref.dtype)

    @pl.when(p == 1)
    def _():
        gates_and_scan(xcs_ref[c].astype(F32), backward=True)
        for b in range(B):
            hsum = jnp.concatenate(
                [h_ref[s, b * LRU_PITCH:b * LRU_PITCH + TC, :] + hf_ref[c, s, b * TC:(b + 1) * TC, :].astype(F32)
                 for s in range(LRU_SLABS)], axis=1)
            y = hsum * _gelu_tanh(gr_ref[b])
            o_ref[b] = _rms(y, og_ref[...]).astype(o_ref.dtype)


def _rglru(xr, gr, conv_w, conv_b, w_a, b_a, w_i, b_i, lam, out_gain):
    B, S, W = xr.shape
    nc = S // LRU_TC
    hb = LRU_TC // HALO
    fwd = lambda p, i: jnp.where(p == 0, i, nc - 1)
    bwd = lambda p, i: nc - 1 - p * i
    full2 = lambda shape: pl.BlockSpec(shape, lambda p, i: (0,) * len(shape))
    wblock = pl.BlockSpec((1, LRU_BLOCKS, LRU_BLOCK_DIM, LRU_BLOCK_DIM), lambda p, i: (p, 0, 0, 0))
    return pl.pallas_call(
        _rglru_kernel,
        grid=(2, nc),
        in_specs=[pl.BlockSpec((B, LRU_TC, W), lambda p, i: (0, fwd(p, i), 0)),
                  pl.BlockSpec((B, HALO, W), lambda p, i: (0, jnp.maximum(fwd(p, i) * hb - 1, 0), 0)),
                  pl.BlockSpec((B, HALO, W), lambda p, i: (0, jnp.minimum((fwd(p, i) + 1) * hb, S // HALO - 1), 0)),
                  pl.BlockSpec((B, LRU_TC, W), lambda p, i: (0, bwd(p, i), 0)),
                  full2((CONV_W, W)),
                  full2((1, W)),
                  wblock, wblock,
                  full2((2, W)), full2((2, W)), full2((2, W)),
                  full2((1, W))],
        out_specs=pl.BlockSpec((B, LRU_TC, W), lambda p, i: (0, bwd(p, i), 0)),
        out_shape=jax.ShapeDtypeStruct((B, S, W), BF16),
        scratch_shapes=[pltpu.VMEM((B, LRU_TC + 2 * HALO, W), F32),
                        pltpu.VMEM((LRU_SLABS, B * LRU_PITCH, LANES), F32),
                        pltpu.VMEM((LRU_SLABS, B * LRU_PITCH, LANES), F32),
                        pltpu.VMEM((LRU_SLABS, B * LRU_PITCH, LANES), F32),
                        pltpu.VMEM((LRU_SLABS, B, LANES), F32),
                        pltpu.VMEM((nc, LRU_SLABS, B * LRU_TC, LANES), BF16),
                        pltpu.VMEM((nc, B * LRU_TC, W), BF16),
                        pltpu.VMEM((W, 2 * W), BF16),
                        pltpu.VMEM((1, 2 * W), F32),
                        pltpu.VMEM((1, W), F32)],
        compiler_params=_cparams(2, LRU_VMEM_LIMIT),
        name="rglru",
    )(xr, xr, xr, gr, conv_w.astype(F32), conv_b.reshape(1, W).astype(F32), w_a.astype(F32), w_i.astype(F32),
      b_a.astype(F32), b_i.astype(F32), lam.astype(F32), out_gain.reshape(1, W).astype(F32))


RT_TM = 1024
RT_PARTS = 2
RT_COLS = LANES
RT_ROWS = 48
RINFO = SUBLANES


def _split_bf16(x):
    hi = x.astype(BF16)
    lo = (x - hi.astype(F32)).astype(BF16)
    return hi, lo


def _route_kernel(an_ref, ln_ref, x_ref, wo_ref, g2_ref, wr_ref, br_ref,
                  x1_ref, h2_ref, gt_ref, ei_ref, cnt_ref, wob_ref, wrb_ref, tri_ref, run_ref, runc_ref):
    @pl.when(pl.program_id(0) == 0)
    def _():
        wob_ref[...] = wo_ref[...].astype(BF16)
        hi, lo = _split_bf16(wr_ref[...])
        wrb_ref[:RT_ROWS, :] = hi
        wrb_ref[RT_ROWS:, :] = lo
        r = lax.broadcasted_iota(jnp.int32, (RT_TM, RT_TM), 0)
        cidx = lax.broadcasted_iota(jnp.int32, (RT_TM, RT_TM), 1)
        tri_ref[...] = (r < cidx).astype(BF16)
        run_ref[...] = jnp.zeros_like(run_ref)
        runc_ref[...] = jnp.zeros_like(runc_ref)

    nt_dims = (((1,), (1,)), ((), ()))
    part = RT_TM // RT_PARTS
    x1s = []
    for r in range(RT_PARTS):
        rows = slice(r * part, (r + 1) * part)
        x1 = (x_ref[rows, :]
              + jnp.dot(an_ref[rows, :], wob_ref[:ATTN_WIDTH, :], preferred_element_type=F32)
              + jnp.dot(ln_ref[rows, :], wob_ref[ATTN_WIDTH:, :], preferred_element_type=F32))
        x1_ref[rows, :] = x1
        x1s.append(x1)
    splits = []
    for r, x1 in enumerate(x1s):
        h2 = _rms(x1, g2_ref[...])
        hi = h2.astype(BF16)
        hi_f = hi.astype(F32)
        h2_ref[r * part:(r + 1) * part, :] = _pack_rounded(hi_f)
        splits.append((hi, (h2 - hi_f).astype(BF16)))
    logits = []
    for hi, lo in splits:
        t1 = lax.dot_general(wrb_ref[...], hi, nt_dims, preferred_element_type=F32)
        t2 = lax.dot_general(wrb_ref[:RT_ROWS, :], lo, nt_dims, preferred_element_type=F32)
        logits.append(t1[:RT_ROWS] + t1[RT_ROWS:] + t2)
    logit = jnp.concatenate(logits, axis=1) + br_ref[...]

    sub = lax.broadcasted_iota(jnp.int32, (SUBLANES, RT_TM), 0)
    first_min = lambda hit: jnp.min(jnp.where(hit, sub, SUBLANES), axis=0, keepdims=True)
    is_g = sub < N_GROUPS
    gl = jnp.where(is_g, logit[:SUBLANES], -jnp.inf)
    gm = jnp.max(gl, axis=0, keepdims=True)
    gidx = first_min(gl == gm)
    g_p = 1.0 / jnp.sum(jnp.where(is_g, jnp.exp(logit[:SUBLANES] - gm), 0.0), axis=0, keepdims=True)
    el = logit[SUBLANES:2 * SUBLANES]
    for g in range(1, N_GROUPS):
        el = jnp.where(gidx == g, logit[(g + 1) * SUBLANES:(g + 2) * SUBLANES], el)
    m1 = jnp.max(el, axis=0, keepdims=True)
    i1 = first_min(el == m1)
    el2 = jnp.where(sub == i1, -jnp.inf, el)
    m2 = jnp.max(el2, axis=0, keepdims=True)
    i2 = first_min(el2 == m2)
    t = jnp.exp(m2 - m1)
    gate1 = g_p / (1.0 + t)
    gate2 = g_p * t / (1.0 + t)
    e1 = gidx * EXPERTS_PER_GROUP + i1
    e2 = gidx * EXPERTS_PER_GROUP + i2

    erow = lax.broadcasted_iota(jnp.int32, (N_EXPERTS, RT_TM), 0)
    oh1 = erow == e1
    oh2 = erow == e2
    oh = (oh1 | oh2).astype(F32)
    ohb = oh.astype(BF16)
    cum = jnp.dot(ohb, tri_ref[...], preferred_element_type=F32) + runc_ref[...]
    rank1 = jnp.sum(jnp.where(oh1, cum, 0.0), axis=0, keepdims=True)
    rank2 = jnp.sum(jnp.where(oh2, cum, 0.0), axis=0, keepdims=True)
    runc_ref[...] = runc_ref[...] + jnp.sum(oh, axis=1, keepdims=True)
    tile_cnt = lax.dot_general(jnp.ones((SUBLANES, RT_TM), BF16), ohb, nt_dims, preferred_element_type=F32)
    run_ref[:, :N_EXPERTS] = run_ref[:, :N_EXPERTS] + tile_cnt[0:1]
    cnt_ref[...] = run_ref[...].astype(jnp.int32)

    rows = [e1, e2, rank1.astype(jnp.int32), rank2.astype(jnp.int32)]
    ei = jnp.zeros((RINFO, RT_TM), jnp.int32)
    for k, v in enumerate(rows):
        ei = jnp.where(sub == k, v, ei)
    ei_ref[0] = ei
    gt_ref[...] = jnp.where(sub == 0, gate1, jnp.where(sub == 1, gate2, 0.0)).T


def _out_route(attn_n, lru_n, x2, w_out, ln2, w_group, b_group, w_er, b_er):
    T = x2.shape[0]
    pad_g = SUBLANES - N_GROUPS
    wr = jnp.concatenate([jnp.pad(w_group.T, ((0, pad_g), (0, 0))),
                          jnp.transpose(w_er, (0, 2, 1)).reshape(N_EXPERTS, D_MODEL)], axis=0)
    wr = jnp.pad(wr, ((0, RT_ROWS - wr.shape[0]), (0, 0))).astype(F32)
    br = jnp.concatenate([jnp.pad(b_group, (0, pad_g)), b_er.reshape(-1)])
    br = jnp.pad(br, (0, RT_ROWS - br.shape[0])).reshape(RT_ROWS, 1).astype(F32)
    row = lambda w: pl.BlockSpec((RT_TM, w), lambda i: (i, 0))
    const = lambda shape: pl.BlockSpec(shape, lambda i: (0, 0))
    return pl.pallas_call(
        _route_kernel,
        grid=(T // RT_TM,),
        in_specs=[row(ATTN_WIDTH), row(LRU_WIDTH), row(D_MODEL), const((D_MODEL, D_MODEL)), const((1, D_MODEL)),
                  const((RT_ROWS, D_MODEL)), const((RT_ROWS, 1))],
        out_specs=[row(D_MODEL), row(PACKED), row(RINFO),
                   pl.BlockSpec((1, RINFO, RT_TM), lambda i: (i, 0, 0)), const((1, RT_COLS))],
        out_shape=[jax.ShapeDtypeStruct((T, D_MODEL), F32),
                   jax.ShapeDtypeStruct((T, PACKED), U32),
                   jax.ShapeDtypeStruct((T, RINFO), F32),
                   jax.ShapeDtypeStruct((T // RT_TM, RINFO, RT_TM), jnp.int32),
                   jax.ShapeDtypeStruct((1, RT_COLS), jnp.int32)],
        scratch_shapes=[pltpu.VMEM((D_MODEL, D_MODEL), BF16),
                        pltpu.VMEM((2 * RT_ROWS, D_MODEL), BF16),
                        pltpu.VMEM((RT_TM, RT_TM), BF16),
                        pltpu.VMEM((1, RT_COLS), F32),
                        pltpu.VMEM((N_EXPERTS, 1), F32)],
        compiler_params=_cparams(1),
        name="out_route",
    )(attn_n, lru_n, x2, w_out, ln2.reshape(1, D_MODEL).astype(F32), wr, br)


def _moe_cap(T):
    A = T * TOP_K
    return ((A + MOE_BLOCK - 1) // MOE_BLOCK) * MOE_BLOCK + N_EXPERTS * MOE_BLOCK


PAD_BITS = tuple(1 << b for b in reversed(range(3, MOE_BLOCK.bit_length() - 1)))


def _layout_kernel(cnt_ref, ei_ref, dest_ref, pstart, be_ref, nu_ref, ge_ref):
    n_blocks = be_ref.shape[0]

    def lay(e, carry):
        start, blk, grp = carry
        pstart[e] = start
        nb = (cnt_ref[0, e] + MOE_BLOCK - 1) // MOE_BLOCK
        ge_ref[grp] = e

        def fill(k, c):
            be_ref[blk + k] = e
            return c
        lax.fori_loop(0, nb, fill, 0)
        return start + nb * MOE_BLOCK, blk + nb, grp + (nb > 0).astype(jnp.int32)
    _, used, groups = lax.fori_loop(0, N_EXPERTS, lay, (jnp.int32(0), jnp.int32(0), jnp.int32(0)))
    nu_ref[0] = used

    def tail(k, c):
        be_ref[k] = N_EXPERTS - 1
        return c
    lax.fori_loop(used, n_blocks, tail, 0)

    def no_group(k, c):
        ge_ref[k] = -1
        return c
    lax.fori_loop(groups, ge_ref.shape[0], no_group, 0)

    expert = ei_ref[:, 0:TOP_K, :]
    dest = ei_ref[:, TOP_K:2 * TOP_K, :]
    for e in range(N_EXPERTS):
        dest = dest + jnp.where(expert == e, pstart[e], 0)
    dest_ref[...] = dest


def _layout(ei, cnt, n_blocks):
    nt = ei.shape[0]
    smem = pl.BlockSpec(memory_space=pltpu.SMEM)
    vmem = pl.BlockSpec(memory_space=pltpu.VMEM)
    return pl.pallas_call(
        _layout_kernel,
        in_specs=[smem, vmem],
        out_specs=[vmem, smem, smem, smem, smem],
        out_shape=[jax.ShapeDtypeStruct((nt, TOP_K, RT_TM), jnp.int32),
                   jax.ShapeDtypeStruct((N_EXPERTS,), jnp.int32),
                   jax.ShapeDtypeStruct((n_blocks,), jnp.int32),
                   jax.ShapeDtypeStruct((1,), jnp.int32),
                   jax.ShapeDtypeStruct((N_EXPERTS + W_AHEAD,), jnp.int32)],
        name="layout",
    )(cnt, ei)


SC_CHUNK = 64
SC_BUFS = 3
SC_LEAD = SC_BUFS - 1


def _sc_workers():
    info = plsc.get_sparse_core_info()
    return info.num_cores, info.num_subcores


def _sc_ring(n_chunks, read, write):
    for c in range(min(SC_LEAD, n_chunks)):
        for cp in read(c):
            cp.start()
    reclaimed = set()
    for c in range(n_chunks):
        for cp in read(c):
            cp.wait()
        for cp in write(c):
            cp.start()
        nxt = c + SC_LEAD
        if nxt < n_chunks:
            if nxt - SC_BUFS >= 0:
                for cp in write(nxt - SC_BUFS):
                    cp.wait()
                reclaimed.add(nxt - SC_BUFS)
            for cp in read(nxt):
                cp.start()
    for c in range(n_chunks):
        if c not in reclaimed:
            for cp in write(c):
                cp.wait()


def _sc_dispatch(h2p, dest, cap):
    T = h2p.shape[0]
    nc, ns = _sc_workers()
    per_w = T // (nc * ns)
    n_ch = per_w // SC_CHUNK
    nt, _, tm = dest.shape
    assert nt * tm == T and tm % per_w == 0 and per_w % SC_CHUNK == 0
    idx = dest.reshape(nt, TOP_K, tm // per_w, per_w).transpose(0, 2, 1, 3).reshape(nc * ns, TOP_K * n_ch, SC_CHUNK)
    mesh = plsc.VectorSubcoreMesh(core_axis_name="c", subcore_axis_name="s")

    @functools.partial(
        pl.kernel, mesh=mesh,
        out_type=jax.ShapeDtypeStruct((cap, PACKED), U32),
        scratch_types=[pltpu.VMEM((TOP_K * n_ch, SC_CHUNK), jnp.int32),
                       pltpu.VMEM((SC_BUFS, SC_CHUNK, PACKED), U32),
                       pltpu.SemaphoreType.DMA((SC_BUFS,)),
                       pltpu.SemaphoreType.DMA((SC_BUFS,))])
    def scatter(src_hbm, idx_hbm, out_hbm, idx_v, rows_v, rsem, wsem):
        wid = lax.axis_index("s") * nc + lax.axis_index("c")
        base = pl.multiple_of(wid * per_w, per_w)
        pltpu.sync_copy(idx_hbm.at[wid], idx_v)

        def read(c):
            b = c % SC_BUFS
            return [pltpu.make_async_copy(src_hbm.at[pl.ds(base + c * SC_CHUNK, SC_CHUNK)], rows_v.at[b], rsem.at[b])]

        def write(c):
            b = c % SC_BUFS
            return [pltpu.make_async_copy(rows_v.at[b], out_hbm.at[idx_v.at[k * n_ch + c]], wsem.at[b])
                    for k in range(TOP_K)]
        _sc_ring(n_ch, read, write)

    return scatter(h2p, idx)


def _sc_gather(yb, dest):
    nt, _, tm = dest.shape
    nc, ns = _sc_workers()
    n_rows = nt * TOP_K * tm
    per_w = n_rows // (nc * ns)
    n_ch = per_w // SC_CHUNK
    assert per_w * nc * ns == n_rows and per_w % SC_CHUNK == 0
    mesh = plsc.VectorSubcoreMesh(core_axis_name="c", subcore_axis_name="s")

    @functools.partial(
        pl.kernel, mesh=mesh,
        out_type=jax.ShapeDtypeStruct((n_rows, PACKED), U32),
        scratch_types=[pltpu.VMEM((per_w,), jnp.int32),
                       pltpu.VMEM((SC_BUFS, SC_CHUNK, PACKED), U32),
                       pltpu.SemaphoreType.DMA((SC_BUFS,)),
                       pltpu.SemaphoreType.DMA((SC_BUFS,))])
    def gather(table_hbm, idx_hbm, out_hbm, idx_v, rows_v, gsem, wsem):
        wid = lax.axis_index("s") * nc + lax.axis_index("c")
        base = pl.multiple_of(wid * per_w, per_w)
        pltpu.sync_copy(idx_hbm.at[pl.ds(base, per_w)], idx_v)

        def read(c):
            b = c % SC_BUFS
            return [pltpu.make_async_copy(table_hbm.at[idx_v.at[pl.ds(c * SC_CHUNK, SC_CHUNK)]], rows_v.at[b], gsem.at[b])]

        def write(c):
            b = c % SC_BUFS
            return [pltpu.make_async_copy(rows_v.at[b], out_hbm.at[pl.ds(base + c * SC_CHUNK, SC_CHUNK)], wsem.at[b])]
        _sc_ring(n_ch, read, write)

    return gather(yb, dest.reshape(n_rows)).reshape(nt, TOP_K, tm, PACKED)


def _padfill_kernel(cnt_ref, pstart, xs_in, xs_ref, zeros, zsem):
    del xs_in

    def pad_copies(fn):
        for e in range(N_EXPERTS):
            cnt = cnt_ref[0, e]
            head = (-cnt) & (SUBLANES - 1)
            rest = ((-cnt) & (MOE_BLOCK - 1)) - head
            off = pstart[e] + cnt
            for k in range(SUBLANES - 1):
                @pl.when(k < head)
                def _(off=off, k=k):
                    fn(pltpu.make_async_copy(zeros.at[pl.ds(0, 1), :], xs_ref.at[pl.ds(off + k, 1), :], zsem))
            off = off + head
            for bit in PAD_BITS:
                @pl.when((rest & bit) != 0)
                def _(off=off, bit=bit):
                    fn(pltpu.make_async_copy(zeros.at[pl.ds(0, bit), :],
                                             xs_ref.at[pl.ds(pl.multiple_of(off, SUBLANES), bit), :], zsem))
                off = off + (rest & bit)

    zeros[...] = jnp.zeros_like(zeros)
    pad_copies(lambda cp: cp.start())
    pad_copies(lambda cp: cp.wait())


def _padfill(xs, pstart, cnt):
    smem = pl.BlockSpec(memory_space=pltpu.SMEM)
    hbm = pl.BlockSpec(memory_space=pl.ANY)
    return pl.pallas_call(
        _padfill_kernel,
        in_specs=[smem, smem, hbm],
        out_specs=hbm,
        out_shape=jax.ShapeDtypeStruct(xs.shape, xs.dtype),
        input_output_aliases={2: 0},
        scratch_shapes=[pltpu.VMEM((MOE_BLOCK // 2, PACKED), U32), pltpu.SemaphoreType.DMA(())],
        name="padfill",
    )(cnt, pstart, xs)


W_SLOTS = 3
W_AHEAD = W_SLOTS - 1
EXPERT_GROUP = 8
EXPERT_RUNS = (1, 2, 4)


def _expert_kernel(be_ref, nu_ref, ge_ref, x_ref, wg_hbm, wu_hbm, wd_hbm, o_ref,
                   wgf, wuf, wdf, grp_ref, sems):
    step = pl.program_id(0)

    def weight_copies(e, slot):
        return (pltpu.make_async_copy(wg_hbm.at[e], wgf.at[slot], sems.at[slot, 0]),
                pltpu.make_async_copy(wu_hbm.at[e], wuf.at[slot], sems.at[slot, 1]),
                pltpu.make_async_copy(wd_hbm.at[e], wdf.at[slot], sems.at[slot, 2]))

    @pl.when(step == 0)
    def _():
        grp_ref[0] = 0
        for a in range(W_AHEAD):
            @pl.when(ge_ref[a] >= 0)
            def _(a=a):
                for cp in weight_copies(ge_ref[a], a):
                    cp.start()

    n_blocks = be_ref.shape[0]
    n_used = nu_ref[0]

    def swiglu(s, n, slot):
        rows = pl.ds(pl.multiple_of(s * MOE_BLOCK, MOE_BLOCK), n * MOE_BLOCK)
        lo, hi = _unpack_rows(x_ref[rows, :])
        lo = lo.astype(BF16)
        hi = hi.astype(BF16)
        g = (jnp.dot(lo, wgf[slot, :PACKED, :], preferred_element_type=F32)
             + jnp.dot(hi, wgf[slot, PACKED:, :], preferred_element_type=F32))
        u = (jnp.dot(lo, wuf[slot, :PACKED, :], preferred_element_type=F32)
             + jnp.dot(hi, wuf[slot, PACKED:, :], preferred_element_type=F32))
        h = (g * _sigmoid(g) * u).astype(BF16)
        o_ref[rows, :] = _pack_rows(jnp.dot(h, wdf[slot], preferred_element_type=F32))

    def run(s):
        j = step * EXPERT_GROUP + s
        e = be_ref[j]
        first = jnp.logical_or(j == 0, e != be_ref[jnp.maximum(j - 1, 0)])

        @pl.when(first)
        def _():
            grp = grp_ref[0]
            slot = grp % W_SLOTS
            for cp in weight_copies(e, slot):
                cp.wait()
            nxt = ge_ref[grp + W_AHEAD]

            @pl.when(nxt >= 0)
            def _():
                for cp in weight_copies(nxt, (grp + W_AHEAD) % W_SLOTS):
                    cp.start()
            grp_ref[0] = grp + 1

        def same(k):
            return (s + k < EXPERT_GROUP) & (j + k < n_used) & (be_ref[jnp.minimum(j + k, n_blocks - 1)] == e)
        take = jnp.int32(1)
        for n in EXPERT_RUNS[1:]:
            ok = same(n - 1)
            for k in range(1, n - 1):
                ok = ok & same(k)
            take = jnp.where(ok, n, take)
        slot = (grp_ref[0] + W_SLOTS - 1) % W_SLOTS
        for n in EXPERT_RUNS:
            @pl.when(take == n)
            def _(n=n):
                swiglu(s, n, slot)
        return s + take

    lax.while_loop(lambda s: (s < EXPERT_GROUP) & (step * EXPERT_GROUP + s < n_used), run, jnp.int32(0))


def _experts(xs, block_expert, n_used, group_expert, w_gate, w_up, w_down):
    cap = xs.shape[0]
    n_blocks = cap // MOE_BLOCK
    assert n_blocks % EXPERT_GROUP == 0
    rows = EXPERT_GROUP * MOE_BLOCK
    last = lambda j, be, nu, ge: jnp.minimum(j, (nu[0] - 1) // EXPERT_GROUP)
    hbm = pl.BlockSpec(memory_space=pl.ANY)
    gs = pltpu.PrefetchScalarGridSpec(
        num_scalar_prefetch=3,
        grid=(n_blocks // EXPERT_GROUP,),
        in_specs=[pl.BlockSpec((rows, PACKED), lambda j, be, nu, ge: (last(j, be, nu, ge), 0)), hbm, hbm, hbm],
        out_specs=pl.BlockSpec((rows, PACKED), lambda j, be, nu, ge: (last(j, be, nu, ge), 0)),
        scratch_shapes=[pltpu.VMEM((W_SLOTS, D_MODEL, D_EXPERT), F32),
                        pltpu.VMEM((W_SLOTS, D_MODEL, D_EXPERT), F32),
                        pltpu.VMEM((W_SLOTS, D_EXPERT, D_MODEL), F32),
                        pltpu.SMEM((1,), jnp.int32),
                        pltpu.SemaphoreType.DMA((W_SLOTS, 3))],
    )
    return pl.pallas_call(
        _expert_kernel,
        grid_spec=gs,
        out_shape=jax.ShapeDtypeStruct((cap, PACKED), U32),
        compiler_params=_cparams(1),
        name="experts",
    )(block_expert, n_used, group_expert, xs, w_gate, w_up, w_down)


CB_TM = RT_TM


def _combine_kernel(x1_ref, gt_ref, y2_ref, o_ref):
    g = gt_ref[...]
    lo1, hi1 = _unpack_rows(y2_ref[0, 0])
    lo2, hi2 = _unpack_rows(y2_ref[0, 1])
    o_ref[:, :PACKED] = x1_ref[:, :PACKED] + g[:, 0:1] * lo1 + g[:, 1:2] * lo2
    o_ref[:, PACKED:] = x1_ref[:, PACKED:] + g[:, 0:1] * hi1 + g[:, 1:2] * hi2


def _combine(x1, gates, y2):
    T = x1.shape[0]
    nt = T // CB_TM
    return pl.pallas_call(
        _combine_kernel,
        grid=(nt,),
        in_specs=[pl.BlockSpec((CB_TM, D_MODEL), lambda i: (i, 0)),
                  pl.BlockSpec((CB_TM, RINFO), lambda i: (i, 0)),
                  pl.BlockSpec((1, TOP_K, CB_TM, PACKED), lambda i: (i, 0, 0, 0))],
        out_specs=pl.BlockSpec((CB_TM, D_MODEL), lambda i: (i, 0)),
        out_shape=jax.ShapeDtypeStruct((T, D_MODEL), F32),
        compiler_params=_cparams(1),
        name="combine",
    )(x1, gates, y2)


def _layer(x, rel_bias, ln1, w_in, q_norm, k_norm, attn_sink, conv_w, conv_b, lru_wa, lru_ba, lru_wi, lru_bi,
           lru_lambda, out_norm_attn, out_norm_lru, w_out, ln2, w_group, b_group, w_er, b_er, w_gate, w_up, w_down):
    B, S, D = x.shape
    T = B * S
    x2 = x.reshape(T, D)
    q, kv, xr, gr = _in_proj(x2, ln1, w_in, q_norm, k_norm)
    attn_n = _attention(q.reshape(B, S, ATTN_WIDTH), kv.reshape(B, S, 2 * KV_WIDTH), rel_bias, attn_sink,
                        out_norm_attn)
    lru_n = _rglru(xr.reshape(B, S, LRU_WIDTH), gr.reshape(B, S, LRU_WIDTH), conv_w, conv_b,
                   lru_wa, lru_ba, lru_wi, lru_bi, lru_lambda, out_norm_lru)
    x1, h2, gates, ei, cnt = _out_route(attn_n.reshape(T, ATTN_WIDTH), lru_n.reshape(T, LRU_WIDTH), x2, w_out, ln2,
                                        w_group, b_group, w_er, b_er)
    cap = _moe_cap(T)
    dest, pstart, block_expert, n_used, group_expert = _layout(ei, cnt, cap // MOE_BLOCK)
    xs = _padfill(_sc_dispatch(h2, dest, cap), pstart, cnt)
    yb = _experts(xs, block_expert, n_used, group_expert, w_gate, w_up, w_down)
    out = _combine(x1, gates, _sc_gather(yb, dest))
    return out.reshape(B, S, D)


def kernel(x, rel_bias, ln1, w_in, q_norm, k_norm, attn_sink, conv_w, conv_b, lru_wa, lru_ba, lru_wi, lru_bi,
           lru_lambda, out_norm_attn, out_norm_lru, w_out, ln2, w_group, b_group, w_expert_router, b_expert_router,
           w_gate, w_up, w_down):
    depth = ln1.shape[0]
    for l in range(depth):
        x = _layer(x, rel_bias, ln1[l], w_in[l], q_norm[l], k_norm[l], attn_sink[l], conv_w[l], conv_b[l],
                   lru_wa[l], lru_ba[l], lru_wi[l], lru_bi[l], lru_lambda[l], out_norm_attn[l], out_norm_lru[l],
                   w_out[l], ln2[l], w_group[l], b_group[l], w_expert_router[l], b_expert_router[l],
                   w_gate[l], w_up[l], w_down[l])
    return x
```

```python
import functools
import math

import jax
import jax.numpy as jnp
from jax import lax
from jax.experimental import pallas as pl
from jax.experimental.pallas import tpu as pltpu
from jax.experimental.pallas import tpu_sc as plsc

D_MODEL = 1024
N_HEADS = 8
N_KV_HEADS = 2
HEAD_DIM = 64
Q_PER_KV = N_HEADS // N_KV_HEADS
ATTN_WIDTH = N_HEADS * HEAD_DIM
KV_WIDTH = N_KV_HEADS * HEAD_DIM
WINDOW = 128
BLOCK = 128
NUM_BUCKETS = 32
MAX_DISTANCE = 128
LRU_WIDTH = D_MODEL - ATTN_WIDTH
LRU_BLOCKS = 8
LRU_BLOCK_DIM = LRU_WIDTH // LRU_BLOCKS
LRU_C = 8.0
CONV_W = 4
CONV_LEFT = 2
N_GROUPS = 4
EXPERTS_PER_GROUP = 8
N_EXPERTS = N_GROUPS * EXPERTS_PER_GROUP
TOP_K = 2
D_EXPERT = 512
MOE_BLOCK = 256
EPS = 1e-6
NEG_INF = -1e30

LANES = 128
SUBLANES = 8
VMEM_LIMIT = 56 * 1024 * 1024
LRU_VMEM_LIMIT = 62 * 1024 * 1024

F32 = jnp.float32
BF16 = jnp.bfloat16
LOG2E = math.log2(math.e)


def _cparams(n_axes, vmem=VMEM_LIMIT):
    return pltpu.CompilerParams(dimension_semantics=("arbitrary",) * n_axes, vmem_limit_bytes=vmem)


def _rms(x, gain):
    return x * lax.rsqrt(jnp.mean(x * x, axis=-1, keepdims=True) + EPS) * gain


U32 = jnp.uint32
HI_MASK = 0xFFFF0000
PACKED = D_MODEL // 2


def _pack_rows(x):
    return _pack_rounded(x.astype(BF16).astype(F32))


def _pack_rounded(xb):
    h = xb.shape[1] // 2
    lo = lax.bitcast_convert_type(xb[:, :h], U32) >> 16
    hi = lax.bitcast_convert_type(xb[:, h:], U32) & jnp.uint32(HI_MASK)
    return lo | hi


def _unpack_rows(p):
    lo = lax.bitcast_convert_type(p << 16, F32)
    hi = lax.bitcast_convert_type(p & jnp.uint32(HI_MASK), F32)
    return lo, hi


IN_TM = 1024


def _head_rms(x, n_heads, gain):
    head = lax.broadcasted_iota(jnp.int32, (1, n_heads * HEAD_DIM), 1) // HEAD_DIM
    x2 = x * x
    scale = jnp.zeros_like(x)
    for h in range(n_heads):
        ms = jnp.sum(jnp.where(head == h, x2, 0.0), axis=-1, keepdims=True) * (1.0 / HEAD_DIM)
        scale = jnp.where(head == h, lax.rsqrt(ms + EPS), scale)
    return x * scale * gain


def _in_proj_kernel(x_ref, g_ref, w_ref, qn_ref, kn_ref, q_ref, kv_ref, xr_ref, gr_ref, wb_ref, qg_ref, kg_ref):
    @pl.when(pl.program_id(0) == 0)
    def _():
        wb_ref[...] = w_ref[...].astype(BF16)
        qg_ref[...] = jnp.concatenate([qn_ref[...]] * N_HEADS, axis=1) * (HEAD_DIM ** -0.5 * LOG2E)
        kg_ref[...] = jnp.concatenate([kn_ref[...]] * N_KV_HEADS, axis=1)

    h = _rms(x_ref[...], g_ref[...]).astype(BF16)
    c_k = ATTN_WIDTH
    c_v = c_k + KV_WIDTH
    c_x = c_v + KV_WIDTH
    c_g = c_x + LRU_WIDTH
    q = jnp.dot(h, wb_ref[:, :c_k], preferred_element_type=F32)
    q_ref[...] = _head_rms(q, N_HEADS, qg_ref[...]).astype(BF16)
    k = jnp.dot(h, wb_ref[:, c_k:c_v], preferred_element_type=F32)
    kv_ref[:, :KV_WIDTH] = _head_rms(k, N_KV_HEADS, kg_ref[...]).astype(BF16)
    kv_ref[:, KV_WIDTH:] = jnp.dot(h, wb_ref[:, c_v:c_x], preferred_element_type=F32).astype(BF16)
    xr_ref[...] = jnp.dot(h, wb_ref[:, c_x:c_g], preferred_element_type=F32)
    gr_ref[...] = jnp.dot(h, wb_ref[:, c_g:], preferred_element_type=F32)


def _in_proj(x2, ln1, w_in, q_gain, k_gain):
    T = x2.shape[0]
    n_in = w_in.shape[1]
    row = lambda w: pl.BlockSpec((IN_TM, w), lambda i: (i, 0))
    return pl.pallas_call(
        _in_proj_kernel,
        grid=(T // IN_TM,),
        in_specs=[row(D_MODEL),
                  pl.BlockSpec((1, D_MODEL), lambda i: (0, 0)),
                  pl.BlockSpec((D_MODEL, n_in), lambda i: (0, 0)),
                  pl.BlockSpec((1, HEAD_DIM), lambda i: (0, 0)),
                  pl.BlockSpec((1, HEAD_DIM), lambda i: (0, 0))],
        out_specs=[row(ATTN_WIDTH), row(2 * KV_WIDTH), row(LRU_WIDTH), row(LRU_WIDTH)],
        out_shape=[jax.ShapeDtypeStruct((T, ATTN_WIDTH), BF16),
                   jax.ShapeDtypeStruct((T, 2 * KV_WIDTH), BF16),
                   jax.ShapeDtypeStruct((T, LRU_WIDTH), F32),
                   jax.ShapeDtypeStruct((T, LRU_WIDTH), F32)],
        scratch_shapes=[pltpu.VMEM((D_MODEL, n_in), BF16),
                        pltpu.VMEM((1, ATTN_WIDTH), F32),
                        pltpu.VMEM((1, KV_WIDTH), F32)],
        compiler_params=_cparams(1),
        name="in_proj",
    )(x2, ln1.reshape(1, D_MODEL), w_in, q_gain.reshape(1, HEAD_DIM).astype(F32),
      k_gain.reshape(1, HEAD_DIM).astype(F32))


def _t5_bucket(rel):
    half = NUM_BUCKETS // 2
    max_exact = half // 2
    base = jnp.where(rel > 0, half, 0)
    n = jnp.abs(rel)
    nf = jnp.maximum(n, 1).astype(jnp.float32)
    large = max_exact + (jnp.log(nf / max_exact) / math.log(MAX_DISTANCE / max_exact)
                         * (half - max_exact)).astype(jnp.int32)
    large = jnp.minimum(large, half - 1)
    return base + jnp.where(n < max_exact, n, large)


HEAD_PAIRS = Q_PER_KV // 2
EDGE_VARIANTS = 3


def _fill_bias_table(rb_ref, bucket_ref, band_ref, o_ref):
    bucket = bucket_ref[...]
    band = band_ref[...] > 0
    col = lax.broadcasted_iota(jnp.int32, bucket.shape, 1)
    valid = (band & (col >= BLOCK), band, band & (col < 2 * BLOCK))
    for h in range(N_HEADS):
        acc = jnp.zeros(bucket.shape, F32)
        for b in range(NUM_BUCKETS):
            acc = jnp.where(bucket == b, rb_ref[b, h], acc)
        kv, g = divmod(h, Q_PER_KV)
        pair, parity = divmod(g, 2)
        for var in range(EDGE_VARIANTS):
            o_ref[var, kv, parity, pair * BLOCK:(pair + 1) * BLOCK, :] = jnp.where(valid[var], acc * LOG2E, NEG_INF)


def _attn_kernel(sink_ref, rb_ref, q_ref, kp_ref, kc_ref, kn_ref, bucket_ref, band_ref, og_ref, o_ref, bias_ref):
    n = pl.program_id(1)

    @pl.when((pl.program_id(0) == 0) & (n == 0))
    def _():
        _fill_bias_table(rb_ref, bucket_ref, band_ref, bias_ref)

    kv_all = jnp.concatenate([kp_ref[0], kc_ref[0], kn_ref[0]], axis=0)
    for qb in range(ATTN_QB):
        variant = 1
        if qb == 0:
            variant = jnp.where(n == 0, 0, 1)
        if qb == ATTN_QB - 1:
            variant = jnp.where(n == pl.num_programs(1) - 1, 2, variant)
        out = _attn_block(q_ref[0, qb * BLOCK:(qb + 1) * BLOCK, :], kv_all[qb * BLOCK:(qb + 3) * BLOCK, :],
                          lambda kv, parity: bias_ref[variant, kv, parity], sink_ref)
        o_ref[0, qb * BLOCK:(qb + 1) * BLOCK, :] = _rms(out, og_ref[...]).astype(o_ref.dtype)


def _attn_block(q, kvw, bias, sink_ref):
    low = lax.broadcasted_iota(jnp.int32, (1, LANES), 1) < HEAD_DIM
    swap = lambda slab: pltpu.roll(slab.astype(F32), HEAD_DIM, 1).astype(BF16)
    kslab, vslab = kvw[:, :KV_WIDTH], kvw[:, KV_WIDTH:]
    kslab_sw, vslab_sw = swap(kslab), swap(vslab)
    rowi = lax.broadcasted_iota(jnp.int32, (HEAD_PAIRS * BLOCK, 1), 0)
    combos = [(kv, parity) for kv in range(N_KV_HEADS) for parity in range(2)]
    scores, vzs, sinks = [], [], []
    for kv, parity in combos:
        ks, vs = (kslab, vslab) if (kv == 0) == (parity == 0) else (kslab_sw, vslab_sw)
        keep = low if parity == 0 else jnp.logical_not(low)
        kz = jnp.where(keep, ks, jnp.zeros_like(ks))
        vzs.append(jnp.where(keep, vs, jnp.zeros_like(vs)))
        base = kv * Q_PER_KV * HEAD_DIM
        qpair = jnp.concatenate([q[:, base + j * LANES:base + (j + 1) * LANES] for j in range(HEAD_PAIRS)], axis=0)
        s = lax.dot_general(qpair, kz, (((1,), (1,)), ((), ())), preferred_element_type=F32)
        scores.append(s + bias(kv, parity))
        sink = jnp.zeros((HEAD_PAIRS * BLOCK, 1), F32)
        for j in range(HEAD_PAIRS):
            sink = jnp.where(rowi // BLOCK == j, sink_ref[kv * Q_PER_KV + 2 * j + parity], sink)
        sinks.append(sink)
    probs, inv = [], []
    for s, sink in zip(scores, sinks):
        m = jnp.maximum(jnp.max(s, axis=-1, keepdims=True), sink)
        p = jnp.exp2(s - m)
        inv.append(1.0 / (jnp.sum(p, axis=-1, keepdims=True) + jnp.exp2(sink - m)))
        probs.append(p.astype(BF16))
    outs = [jnp.dot(p, vz, preferred_element_type=F32) * r for p, vz, r in zip(probs, vzs, inv)]
    cols = []
    for kv in range(N_KV_HEADS):
        acc = outs[2 * kv] + outs[2 * kv + 1]
        cols += [acc[j * BLOCK:(j + 1) * BLOCK, :] for j in range(HEAD_PAIRS)]
    return jnp.concatenate(cols, axis=1)


ATTN_QB = 16


def _attention(q, kv, rel_bias, sink, out_gain):
    B, S, _ = q.shape
    nb = S // BLOCK
    assert ATTN_QB >= 2 and nb % ATTN_QB == 0, "a step's first and last query blocks must be distinct"
    ns = nb // ATTN_QB
    rows = ATTN_QB * BLOCK
    qi = jnp.arange(BLOCK, dtype=jnp.int32)
    kj = jnp.arange(3 * BLOCK, dtype=jnp.int32)
    rel = kj[None, :] - BLOCK - qi[:, None]
    bucket = _t5_bucket(rel).astype(jnp.int32)
    band = (jnp.abs(rel) <= WINDOW).astype(jnp.int32)
    kvspec = lambda f: pl.BlockSpec((1, BLOCK, 2 * KV_WIDTH), f)
    smem = pl.BlockSpec(memory_space=pltpu.SMEM)
    geom = pl.BlockSpec((BLOCK, 3 * BLOCK), lambda b, n: (0, 0))
    return pl.pallas_call(
        _attn_kernel,
        grid=(B, ns),
        in_specs=[smem, smem,
                  pl.BlockSpec((1, rows, ATTN_WIDTH), lambda b, n: (b, n, 0)),
                  kvspec(lambda b, n: (b, jnp.maximum(n * ATTN_QB - 1, 0), 0)),
                  pl.BlockSpec((1, rows, 2 * KV_WIDTH), lambda b, n: (b, n, 0)),
                  kvspec(lambda b, n: (b, jnp.minimum((n + 1) * ATTN_QB, nb - 1), 0)),
                  geom, geom,
                  pl.BlockSpec((1, ATTN_WIDTH), lambda b, n: (0, 0))],
        out_specs=pl.BlockSpec((1, rows, ATTN_WIDTH), lambda b, n: (b, n, 0)),
        out_shape=jax.ShapeDtypeStruct((B, S, ATTN_WIDTH), BF16),
        scratch_shapes=[pltpu.VMEM((EDGE_VARIANTS, N_KV_HEADS, 2, HEAD_PAIRS * BLOCK, 3 * BLOCK), F32)],
        compiler_params=_cparams(2),
        name="attention",
    )(sink.astype(F32) * LOG2E, rel_bias.astype(F32), q, kv, kv, kv, bucket, band, out_gain.reshape(1, ATTN_WIDTH))


LRU_TC = 128
LRU_PITCH = LRU_TC + SUBLANES // 2
LRU_SLABS = LRU_WIDTH // LANES
LRU_UNROLL = 32
HALO = SUBLANES


def _softplus(x):
    return jnp.maximum(x, 0.0) + jnp.log(1.0 + jnp.exp(-jnp.abs(x)))


def _gelu_tanh(x):
    k = math.sqrt(2.0 / math.pi)
    hx = 0.5 * x
    return hx + hx * jnp.tanh(x * (k + (k * 0.044715) * (x * x)))


def _sigmoid(x):
    return 0.5 + 0.5 * jnp.tanh(0.5 * x)


def _rglru_kernel(xr_ref, xp_ref, xn_ref, gr_ref, cw_ref, cb_ref, wa_ref, wi_ref, ba_ref, bi_ref, lam_ref, og_ref,
                  o_ref, sx_ref, a_ref, u_ref, h_ref, carry_ref, hf_ref, xcs_ref, wg_ref, bg_ref, k_ref):
    p = pl.program_id(0)
    i = pl.program_id(1)
    nc = pl.num_programs(1)
    c = i + p * (nc - 1 - 2 * i)
    B = xr_ref.shape[0]
    TC = LRU_TC

    @pl.when(i == 0)
    def _():
        carry_ref[...] = jnp.zeros_like(carry_ref)
        wg_ref[...] = jnp.zeros_like(wg_ref)
        for sel, w_ref in enumerate((wa_ref, wi_ref)):
            for h in range(LRU_BLOCKS):
                lo = h * LRU_BLOCK_DIM
                wg_ref[lo:lo + LRU_BLOCK_DIM, sel * LRU_WIDTH + lo:sel * LRU_WIDTH + lo + LRU_BLOCK_DIM] = (
                    0.5 * w_ref[0, h]).astype(BF16)
        row = pl.ds(p, 1)
        bg_ref[:, :LRU_WIDTH] = 0.5 * ba_ref[row, :]
        bg_ref[:, LRU_WIDTH:] = 0.5 * bi_ref[row, :]
        k_ref[...] = (-0.5 * LRU_C * math.log2(math.e)) * _softplus(-lam_ref[row, :])

    def gates_and_scan(xc2, backward):
        g = jnp.dot(xc2.astype(BF16), wg_ref[...], preferred_element_type=F32) + bg_ref[...]
        ta = jnp.tanh(g[:, :LRU_WIDTH])
        ig = 0.5 + 0.5 * jnp.tanh(g[:, LRU_WIDTH:])
        a = jnp.exp2((1.0 + ta) * k_ref[...])
        z = 1.0 - a * a
        u = z * lax.rsqrt(jnp.maximum(z, 1e-30)) * ig * xc2
        for b in range(B):
            for s in range(LRU_SLABS):
                a_ref[s, b * LRU_PITCH:b * LRU_PITCH + TC, :] = a[b * TC:(b + 1) * TC, s * LANES:(s + 1) * LANES]
                u_ref[s, b * LRU_PITCH:b * LRU_PITCH + TC, :] = u[b * TC:(b + 1) * TC, s * LANES:(s + 1) * LANES]

        def trip(i, hs):
            t0 = pl.multiple_of((TC // LRU_UNROLL - 1 - i if backward else i) * LRU_UNROLL, LRU_UNROLL)
            for j in range(LRU_UNROLL):
                t = t0 + (LRU_UNROLL - 1 - j if backward else j)
                out = []
                for s in range(LRU_SLABS):
                    idx = pl.ds(t, B, stride=LRU_PITCH)
                    hn = a_ref[s, idx, :] * hs[s] + u_ref[s, idx, :]
                    h_ref[s, idx, :] = hn
                    out.append(hn)
                hs = tuple(out)
            return hs

        hs = lax.fori_loop(0, TC // LRU_UNROLL, trip, tuple(carry_ref[s] for s in range(LRU_SLABS)))
        for s in range(LRU_SLABS):
            carry_ref[s] = hs[s]

    @pl.when(p == 0)
    def _():
        sx_ref[:, HALO:HALO + TC, :] = xr_ref[...]
        sx_ref[:, 0:HALO, :] = jnp.where(c > 0, xp_ref[...], 0.0)
        sx_ref[:, HALO + TC:, :] = jnp.where(c < nc - 1, xn_ref[...], 0.0)
        xc = cb_ref[...][None]
        for j in range(CONV_W):
            off = HALO + j - CONV_LEFT
            xc = xc + cw_ref[j:j + 1, :][None] * sx_ref[:, off:off + TC, :]
        xc2 = xc.reshape(B * TC, LRU_WIDTH)
        xcs_ref[c] = xc2.astype(xcs_ref.dtype)
        gates_and_scan(xc2, backward=False)
        for b in range(B):
            for s in range(LRU_SLABS):
                hf_ref[c, s, b * TC:(b + 1) * TC, :] = h_ref[s, b * LRU_PITCH:b * LRU_PITCH + TC, :].astype(hf_ref.dtype)

    @pl.when(p == 1)
    def _():
        gates_and_scan(xcs_ref[c].astype(F32), backward=True)
        for b in range(B):
            hsum = jnp.concatenate(
                [h_ref[s, b * LRU_PITCH:b * LRU_PITCH + TC, :] + hf_ref[c, s, b * TC:(b + 1) * TC, :].astype(F32)
                 for s in range(LRU_SLABS)], axis=1)
            y = hsum * _gelu_tanh(gr_ref[b])
            o_ref[b] = _rms(y, og_ref[...]).astype(o_ref.dtype)


def _rglru(xr, gr, conv_w, conv_b, w_a, b_a, w_i, b_i, lam, out_gain):
    B, S, W = xr.shape
    nc = S // LRU_TC
    hb = LRU_TC // HALO
    fwd = lambda p, i: jnp.where(p == 0, i, nc - 1)
    bwd = lambda p, i: nc - 1 - p * i
    full2 = lambda shape: pl.BlockSpec(shape, lambda p, i: (0,) * len(shape))
    wblock = pl.BlockSpec((1, LRU_BLOCKS, LRU_BLOCK_DIM, LRU_BLOCK_DIM), lambda p, i: (p, 0, 0, 0))
    return pl.pallas_call(
        _rglru_kernel,
        grid=(2, nc),
        in_specs=[pl.BlockSpec((B, LRU_TC, W), lambda p, i: (0, fwd(p, i), 0)),
                  pl.BlockSpec((B, HALO, W), lambda p, i: (0, jnp.maximum(fwd(p, i) * hb - 1, 0), 0)),
                  pl.BlockSpec((B, HALO, W), lambda p, i: (0, jnp.minimum((fwd(p, i) + 1) * hb, S // HALO - 1), 0)),
                  pl.BlockSpec((B, LRU_TC, W), lambda p, i: (0, bwd(p, i), 0)),
                  full2((CONV_W, W)),
                  full2((1, W)),
                  wblock, wblock,
                  full2((2, W)), full2((2, W)), full2((2, W)),
                  full2((1, W))],
        out_specs=pl.BlockSpec((B, LRU_TC, W), lambda p, i: (0, bwd(p, i), 0)),
        out_shape=jax.ShapeDtypeStruct((B, S, W), BF16),
        scratch_shapes=[pltpu.VMEM((B, LRU_TC + 2 * HALO, W), F32),
                        pltpu.VMEM((LRU_SLABS, B * LRU_PITCH, LANES), F32),
                        pltpu.VMEM((LRU_SLABS, B * LRU_PITCH, LANES), F32),
                        pltpu.VMEM((LRU_SLABS, B * LRU_PITCH, LANES), F32),
                        pltpu.VMEM((LRU_SLABS, B, LANES), F32),
                        pltpu.VMEM((nc, LRU_SLABS, B * LRU_TC, LANES), BF16),
                        pltpu.VMEM((nc, B * LRU_TC, W), BF16),
                        pltpu.VMEM((W, 2 * W), BF16),
                        pltpu.VMEM((1, 2 * W), F32),
                        pltpu.VMEM((1, W), F32)],
        compiler_params=_cparams(2, LRU_VMEM_LIMIT),
        name="rglru",
    )(xr, xr, xr, gr, conv_w.astype(F32), conv_b.reshape(1, W).astype(F32), w_a.astype(F32), w_i.astype(F32),
      b_a.astype(F32), b_i.astype(F32), lam.astype(F32), out_gain.reshape(1, W).astype(F32))


RT_TM = 1024
RT_PARTS = 4
RT_COLS = LANES
RT_ROWS = 48
RINFO = SUBLANES


def _split_bf16(x):
    hi = x.astype(BF16)
    lo = (x - hi.astype(F32)).astype(BF16)
    return hi, lo


def _route_kernel(an_ref, ln_ref, x_ref, wo_ref, g2_ref, wr_ref, br_ref,
                  x1_ref, h2_ref, gt_ref, ei_ref, cnt_ref, wob_ref, wrb_ref, tri_ref, run_ref, runc_ref):
    @pl.when(pl.program_id(0) == 0)
    def _():
        wob_ref[...] = wo_ref[...].astype(BF16)
        hi, lo = _split_bf16(wr_ref[...])
        wrb_ref[:RT_ROWS, :] = hi
        wrb_ref[RT_ROWS:, :] = lo
        r = lax.broadcasted_iota(jnp.int32, (RT_TM, RT_TM), 0)
        cidx = lax.broadcasted_iota(jnp.int32, (RT_TM, RT_TM), 1)
        tri_ref[...] = (r < cidx).astype(BF16)
        run_ref[...] = jnp.zeros_like(run_ref)
        runc_ref[...] = jnp.zeros_like(runc_ref)

    nt_dims = (((1,), (1,)), ((), ()))
    part = RT_TM // RT_PARTS
    x1s = []
    for r in range(RT_PARTS):
        rows = slice(r * part, (r + 1) * part)
        x1 = (x_ref[rows, :]
              + jnp.dot(an_ref[rows, :], wob_ref[:ATTN_WIDTH, :], preferred_element_type=F32)
              + jnp.dot(ln_ref[rows, :], wob_ref[ATTN_WIDTH:, :], preferred_element_type=F32))
        x1_ref[rows, :] = x1
        x1s.append(x1)
    splits = []
    for r, x1 in enumerate(x1s):
        h2 = _rms(x1, g2_ref[...])
        hi = h2.astype(BF16)
        hi_f = hi.astype(F32)
        h2_ref[r * part:(r + 1) * part, :] = _pack_rounded(hi_f)
        splits.append((hi, (h2 - hi_f).astype(BF16)))
    logits = []
    for hi, lo in splits:
        t1 = lax.dot_general(wrb_ref[...], hi, nt_dims, preferred_element_type=F32)
        t2 = lax.dot_general(wrb_ref[:RT_ROWS, :], lo, nt_dims, preferred_element_type=F32)
        logits.append(t1[:RT_ROWS] + t1[RT_ROWS:] + t2)
    logit = jnp.concatenate(logits, axis=1) + br_ref[...]

    sub = lax.broadcasted_iota(jnp.int32, (SUBLANES, RT_TM), 0)
    first_min = lambda hit: jnp.min(jnp.where(hit, sub, SUBLANES), axis=0, keepdims=True)
    is_g = sub < N_GROUPS
    gl = jnp.where(is_g, logit[:SUBLANES], -jnp.inf)
    gm = jnp.max(gl, axis=0, keepdims=True)
    gidx = first_min(gl == gm)
    g_p = 1.0 / jnp.sum(jnp.where(is_g, jnp.exp(logit[:SUBLANES] - gm), 0.0), axis=0, keepdims=True)
    el = logit[SUBLANES:2 * SUBLANES]
    for g in range(1, N_GROUPS):
        el = jnp.where(gidx == g, logit[(g + 1) * SUBLANES:(g + 2) * SUBLANES], el)
    m1 = jnp.max(el, axis=0, keepdims=True)
    i1 = first_min(el == m1)
    el2 = jnp.where(sub == i1, -jnp.inf, el)
    m2 = jnp.max(el2, axis=0, keepdims=True)
    i2 = first_min(el2 == m2)
    t = jnp.exp(m2 - m1)
    gate1 = g_p / (1.0 + t)
    gate2 = g_p * t / (1.0 + t)
    e1 = gidx * EXPERTS_PER_GROUP + i1
    e2 = gidx * EXPERTS_PER_GROUP + i2

    erow = lax.broadcasted_iota(jnp.int32, (N_EXPERTS, RT_TM), 0)
    oh1 = erow == e1
    oh2 = erow == e2
    oh = (oh1 | oh2).astype(F32)
    ohb = oh.astype(BF16)
    cum = jnp.dot(ohb, tri_ref[...], preferred_element_type=F32) + runc_ref[...]
    rank1 = jnp.sum(jnp.where(oh1, cum, 0.0), axis=0, keepdims=True)
    rank2 = jnp.sum(jnp.where(oh2, cum, 0.0), axis=0, keepdims=True)
    runc_ref[...] = runc_ref[...] + jnp.sum(oh, axis=1, keepdims=True)
    tile_cnt = lax.dot_general(jnp.ones((SUBLANES, RT_TM), BF16), ohb, nt_dims, preferred_element_type=F32)
    run_ref[:, :N_EXPERTS] = run_ref[:, :N_EXPERTS] + tile_cnt[0:1]
    cnt_ref[...] = run_ref[...].astype(jnp.int32)

    rows = [e1, e2, rank1.astype(jnp.int32), rank2.astype(jnp.int32)]
    ei = jnp.zeros((RINFO, RT_TM), jnp.int32)
    for k, v in enumerate(rows):
        ei = jnp.where(sub == k, v, ei)
    ei_ref[0] = ei
    gt_ref[...] = jnp.where(sub == 0, gate1, jnp.where(sub == 1, gate2, 0.0)).T


def _out_route(attn_n, lru_n, x2, w_out, ln2, w_group, b_group, w_er, b_er):
    T = x2.shape[0]
    pad_g = SUBLANES - N_GROUPS
    wr = jnp.concatenate([jnp.pad(w_group.T, ((0, pad_g), (0, 0))),
                          jnp.transpose(w_er, (0, 2, 1)).reshape(N_EXPERTS, D_MODEL)], axis=0)
    wr = jnp.pad(wr, ((0, RT_ROWS - wr.shape[0]), (0, 0))).astype(F32)
    br = jnp.concatenate([jnp.pad(b_group, (0, pad_g)), b_er.reshape(-1)])
    br = jnp.pad(br, (0, RT_ROWS - br.shape[0])).reshape(RT_ROWS, 1).astype(F32)
    row = lambda w: pl.BlockSpec((RT_TM, w), lambda i: (i, 0))
    const = lambda shape: pl.BlockSpec(shape, lambda i: (0, 0))
    return pl.pallas_call(
        _route_kernel,
        grid=(T // RT_TM,),
        in_specs=[row(ATTN_WIDTH), row(LRU_WIDTH), row(D_MODEL), const((D_MODEL, D_MODEL)), const((1, D_MODEL)),
                  const((RT_ROWS, D_MODEL)), const((RT_ROWS, 1))],
        out_specs=[row(D_MODEL), row(PACKED), row(RINFO),
                   pl.BlockSpec((1, RINFO, RT_TM), lambda i: (i, 0, 0)), const((1, RT_COLS))],
        out_shape=[jax.ShapeDtypeStruct((T, D_MODEL), F32),
                   jax.ShapeDtypeStruct((T, PACKED), U32),
                   jax.ShapeDtypeStruct((T, RINFO), F32),
                   jax.ShapeDtypeStruct((T // RT_TM, RINFO, RT_TM), jnp.int32),
                   jax.ShapeDtypeStruct((1, RT_COLS), jnp.int32)],
        scratch_shapes=[pltpu.VMEM((D_MODEL, D_MODEL), BF16),
                        pltpu.VMEM((2 * RT_ROWS, D_MODEL), BF16),
                        pltpu.VMEM((RT_TM, RT_TM), BF16),
                        pltpu.VMEM((1, RT_COLS), F32),
                        pltpu.VMEM((N_EXPERTS, 1), F32)],
        compiler_params=_cparams(1),
        name="out_route",
    )(attn_n, lru_n, x2, w_out, ln2.reshape(1, D_MODEL).astype(F32), wr, br)


def _moe_cap(T):
    A = T * TOP_K
    return ((A + MOE_BLOCK - 1) // MOE_BLOCK) * MOE_BLOCK + N_EXPERTS * MOE_BLOCK


PAD_BITS = tuple(1 << b for b in reversed(range(3, MOE_BLOCK.bit_length() - 1)))


def _layout_kernel(cnt_ref, ei_ref, dest_ref, pstart, be_ref, nu_ref, ge_ref):
    n_blocks = be_ref.shape[0]

    def lay(e, carry):
        start, blk, grp = carry
        pstart[e] = start
        nb = (cnt_ref[0, e] + MOE_BLOCK - 1) // MOE_BLOCK
        ge_ref[grp] = e

        def fill(k, c):
            be_ref[blk + k] = e
            return c
        lax.fori_loop(0, nb, fill, 0)
        return start + nb * MOE_BLOCK, blk + nb, grp + (nb > 0).astype(jnp.int32)
    _, used, groups = lax.fori_loop(0, N_EXPERTS, lay, (jnp.int32(0), jnp.int32(0), jnp.int32(0)))
    nu_ref[0] = used

    def tail(k, c):
        be_ref[k] = N_EXPERTS - 1
        return c
    lax.fori_loop(used, n_blocks, tail, 0)

    def no_group(k, c):
        ge_ref[k] = -1
        return c
    lax.fori_loop(groups, ge_ref.shape[0], no_group, 0)

    expert = ei_ref[:, 0:TOP_K, :]
    dest = ei_ref[:, TOP_K:2 * TOP_K, :]
    for e in range(N_EXPERTS):
        dest = dest + jnp.where(expert == e, pstart[e], 0)
    dest_ref[...] = dest


def _layout(ei, cnt, n_blocks):
    nt = ei.shape[0]
    smem = pl.BlockSpec(memory_space=pltpu.SMEM)
    vmem = pl.BlockSpec(memory_space=pltpu.VMEM)
    return pl.pallas_call(
        _layout_kernel,
        in_specs=[smem, vmem],
        out_specs=[vmem, smem, smem, smem, smem],
        out_shape=[jax.ShapeDtypeStruct((nt, TOP_K, RT_TM), jnp.int32),
                   jax.ShapeDtypeStruct((N_EXPERTS,), jnp.int32),
                   jax.ShapeDtypeStruct((n_blocks,), jnp.int32),
                   jax.ShapeDtypeStruct((1,), jnp.int32),
                   jax.ShapeDtypeStruct((N_EXPERTS + W_AHEAD,), jnp.int32)],
        name="layout",
    )(cnt, ei)


SC_CHUNK = 64
SC_BUFS = 3
SC_LEAD = SC_BUFS - 1


def _sc_workers():
    info = plsc.get_sparse_core_info()
    return info.num_cores, info.num_subcores


def _sc_ring(n_chunks, read, write):
    for c in range(min(SC_LEAD, n_chunks)):
        for cp in read(c):
            cp.start()
    reclaimed = set()
    for c in range(n_chunks):
        for cp in read(c):
            cp.wait()
        for cp in write(c):
            cp.start()
        nxt = c + SC_LEAD
        if nxt < n_chunks:
            if nxt - SC_BUFS >= 0:
                for cp in write(nxt - SC_BUFS):
                    cp.wait()
                reclaimed.add(nxt - SC_BUFS)
            for cp in read(nxt):
                cp.start()
    for c in range(n_chunks):
        if c not in reclaimed:
            for cp in write(c):
                cp.wait()


def _sc_dispatch(h2p, dest, cap):
    T = h2p.shape[0]
    nc, ns = _sc_workers()
    per_w = T // (nc * ns)
    n_ch = per_w // SC_CHUNK
    nt, _, tm = dest.shape
    assert nt * tm == T and tm % per_w == 0 and per_w % SC_CHUNK == 0
    idx = dest.reshape(nt, TOP_K, tm // per_w, per_w).transpose(0, 2, 1, 3).reshape(nc * ns, TOP_K * n_ch, SC_CHUNK)
    mesh = plsc.VectorSubcoreMesh(core_axis_name="c", subcore_axis_name="s")

    @functools.partial(
        pl.kernel, mesh=mesh,
        out_type=jax.ShapeDtypeStruct((cap, PACKED), U32),
        scratch_types=[pltpu.VMEM((TOP_K * n_ch, SC_CHUNK), jnp.int32),
                       pltpu.VMEM((SC_BUFS, SC_CHUNK, PACKED), U32),
                       pltpu.SemaphoreType.DMA((SC_BUFS,)),
                       pltpu.SemaphoreType.DMA((SC_BUFS,))])
    def scatter(src_hbm, idx_hbm, out_hbm, idx_v, rows_v, rsem, wsem):
        wid = lax.axis_index("s") * nc + lax.axis_index("c")
        base = pl.multiple_of(wid * per_w, per_w)
        pltpu.sync_copy(idx_hbm.at[wid], idx_v)

        def read(c):
            b = c % SC_BUFS
            return [pltpu.make_async_copy(src_hbm.at[pl.ds(base + c * SC_CHUNK, SC_CHUNK)], rows_v.at[b], rsem.at[b])]

        def write(c):
            b = c % SC_BUFS
            return [pltpu.make_async_copy(rows_v.at[b], out_hbm.at[idx_v.at[k * n_ch + c]], wsem.at[b])
                    for k in range(TOP_K)]
        _sc_ring(n_ch, read, write)

    return scatter(h2p, idx)


def _sc_gather(yb, dest):
    nt, _, tm = dest.shape
    nc, ns = _sc_workers()
    n_rows = nt * TOP_K * tm
    per_w = n_rows // (nc * ns)
    n_ch = per_w // SC_CHUNK
    assert per_w * nc * ns == n_rows and per_w % SC_CHUNK == 0
    mesh = plsc.VectorSubcoreMesh(core_axis_name="c", subcore_axis_name="s")

    @functools.partial(
        pl.kernel, mesh=mesh,
        out_type=jax.ShapeDtypeStruct((n_rows, PACKED), U32),
        scratch_types=[pltpu.VMEM((per_w,), jnp.int32),
                       pltpu.VMEM((SC_BUFS, SC_CHUNK, PACKED), U32),
                       pltpu.SemaphoreType.DMA((SC_BUFS,)),
                       pltpu.SemaphoreType.DMA((SC_BUFS,))])
    def gather(table_hbm, idx_hbm, out_hbm, idx_v, rows_v, gsem, wsem):
        wid = lax.axis_index("s") * nc + lax.axis_index("c")
        base = pl.multiple_of(wid * per_w, per_w)
        pltpu.sync_copy(idx_hbm.at[pl.ds(base, per_w)], idx_v)

        def read(c):
            b = c % SC_BUFS
            return [pltpu.make_async_copy(table_hbm.at[idx_v.at[pl.ds(c * SC_CHUNK, SC_CHUNK)]], rows_v.at[b], gsem.at[b])]

        def write(c):
            b = c % SC_BUFS
            return [pltpu.make_async_copy(rows_v.at[b], out_hbm.at[pl.ds(base + c * SC_CHUNK, SC_CHUNK)], wsem.at[b])]
        _sc_ring(n_ch, read, write)

    return gather(yb, dest.reshape(n_rows)).reshape(nt, TOP_K, tm, PACKED)


def _padfill_kernel(cnt_ref, pstart, xs_in, xs_ref, zeros, zsem):
    del xs_in

    def pad_copies(fn):
        for e in range(N_EXPERTS):
            cnt = cnt_ref[0, e]
            head = (-cnt) & (SUBLANES - 1)
            rest = ((-cnt) & (MOE_BLOCK - 1)) - head
            off = pstart[e] + cnt
            for k in range(SUBLANES - 1):
                @pl.when(k < head)
                def _(off=off, k=k):
                    fn(pltpu.make_async_copy(zeros.at[pl.ds(0, 1), :], xs_ref.at[pl.ds(off + k, 1), :], zsem))
            off = off + head
            for bit in PAD_BITS:
                @pl.when((rest & bit) != 0)
                def _(off=off, bit=bit):
                    fn(pltpu.make_async_copy(zeros.at[pl.ds(0, bit), :],
                                             xs_ref.at[pl.ds(pl.multiple_of(off, SUBLANES), bit), :], zsem))
                off = off + (rest & bit)

    zeros[...] = jnp.zeros_like(zeros)
    pad_copies(lambda cp: cp.start())
    pad_copies(lambda cp: cp.wait())


def _padfill(xs, pstart, cnt):
    smem = pl.BlockSpec(memory_space=pltpu.SMEM)
    hbm = pl.BlockSpec(memory_space=pl.ANY)
    return pl.pallas_call(
        _padfill_kernel,
        in_specs=[smem, smem, hbm],
        out_specs=hbm,
        out_shape=jax.ShapeDtypeStruct(xs.shape, xs.dtype),
        input_output_aliases={2: 0},
        scratch_shapes=[pltpu.VMEM((MOE_BLOCK // 2, PACKED), U32), pltpu.SemaphoreType.DMA(())],
        name="padfill",
    )(cnt, pstart, xs)


W_SLOTS = 3
W_AHEAD = W_SLOTS - 1
EXPERT_GROUP = 8
EXPERT_RUNS = (1, 2, 4)


def _expert_kernel(be_ref, nu_ref, ge_ref, x_ref, wg_hbm, wu_hbm, wd_hbm, o_ref,
                   wgf, wuf, wdf, grp_ref, sems):
    step = pl.program_id(0)

    def weight_copies(e, slot):
        return (pltpu.make_async_copy(wg_hbm.at[e], wgf.at[slot], sems.at[slot, 0]),
                pltpu.make_async_copy(wu_hbm.at[e], wuf.at[slot], sems.at[slot, 1]),
                pltpu.make_async_copy(wd_hbm.at[e], wdf.at[slot], sems.at[slot, 2]))

    @pl.when(step == 0)
    def _():
        grp_ref[0] = 0
        for a in range(W_AHEAD):
            @pl.when(ge_ref[a] >= 0)
            def _(a=a):
                for cp in weight_copies(ge_ref[a], a):
                    cp.start()

    n_blocks = be_ref.shape[0]
    n_used = nu_ref[0]

    def swiglu(s, n, slot):
        rows = pl.ds(pl.multiple_of(s * MOE_BLOCK, MOE_BLOCK), n * MOE_BLOCK)
        lo, hi = _unpack_rows(x_ref[rows, :])
        lo = lo.astype(BF16)
        hi = hi.astype(BF16)
        g = (jnp.dot(lo, wgf[slot, :PACKED, :], preferred_element_type=F32)
             + jnp.dot(hi, wgf[slot, PACKED:, :], preferred_element_type=F32))
        u = (jnp.dot(lo, wuf[slot, :PACKED, :], preferred_element_type=F32)
             + jnp.dot(hi, wuf[slot, PACKED:, :], preferred_element_type=F32))
        h = (g * _sigmoid(g) * u).astype(BF16)
        o_ref[rows, :] = _pack_rows(jnp.dot(h, wdf[slot], preferred_element_type=F32))

    def run(s):
        j = step * EXPERT_GROUP + s
        e = be_ref[j]
        first = jnp.logical_or(j == 0, e != be_ref[jnp.maximum(j - 1, 0)])

        @pl.when(first)
        def _():
            grp = grp_ref[0]
            slot = grp % W_SLOTS
            for cp in weight_copies(e, slot):
                cp.wait()
            nxt = ge_ref[grp + W_AHEAD]

            @pl.when(nxt >= 0)
            def _():
                for cp in weight_copies(nxt, (grp + W_AHEAD) % W_SLOTS):
                    cp.start()
            grp_ref[0] = grp + 1

        def same(k):
            return (s + k < EXPERT_GROUP) & (j + k < n_used) & (be_ref[jnp.minimum(j + k, n_blocks - 1)] == e)
        take = jnp.int32(1)
        for n in EXPERT_RUNS[1:]:
            ok = same(n - 1)
            for k in range(1, n - 1):
                ok = ok & same(k)
            take = jnp.where(ok, n, take)
        slot = (grp_ref[0] + W_SLOTS - 1) % W_SLOTS
        for n in EXPERT_RUNS:
            @pl.when(take == n)
            def _(n=n):
                swiglu(s, n, slot)
        return s + take

    lax.while_loop(lambda s: (s < EXPERT_GROUP) & (step * EXPERT_GROUP + s < n_used), run, jnp.int32(0))


def _experts(xs, block_expert, n_used, group_expert, w_gate, w_up, w_down):
    cap = xs.shape[0]
    n_blocks = cap // MOE_BLOCK
    assert n_blocks % EXPERT_GROUP == 0
    rows = EXPERT_GROUP * MOE_BLOCK
    last = lambda j, be, nu, ge: jnp.minimum(j, (nu[0] - 1) // EXPERT_GROUP)
    hbm = pl.BlockSpec(memory_space=pl.ANY)
    gs = pltpu.PrefetchScalarGridSpec(
        num_scalar_prefetch=3,
        grid=(n_blocks // EXPERT_GROUP,),
        in_specs=[pl.BlockSpec((rows, PACKED), lambda j, be, nu, ge: (last(j, be, nu, ge), 0)), hbm, hbm, hbm],
        out_specs=pl.BlockSpec((rows, PACKED), lambda j, be, nu, ge: (last(j, be, nu, ge), 0)),
        scratch_shapes=[pltpu.VMEM((W_SLOTS, D_MODEL, D_EXPERT), F32),
                        pltpu.VMEM((W_SLOTS, D_MODEL, D_EXPERT), F32),
                        pltpu.VMEM((W_SLOTS, D_EXPERT, D_MODEL), F32),
                        pltpu.SMEM((1,), jnp.int32),
                        pltpu.SemaphoreType.DMA((W_SLOTS, 3))],
    )
    return pl.pallas_call(
        _expert_kernel,
        grid_spec=gs,
        out_shape=jax.ShapeDtypeStruct((cap, PACKED), U32),
        compiler_params=_cparams(1),
        name="experts",
    )(block_expert, n_used, group_expert, xs, w_gate, w_up, w_down)


CB_TM = RT_TM


def _combine_kernel(x1_ref, gt_ref, y2_ref, o_ref):
    g = gt_ref[...]
    lo1, hi1 = _unpack_rows(y2_ref[0, 0])
    lo2, hi2 = _unpack_rows(y2_ref[0, 1])
    o_ref[:, :PACKED] = x1_ref[:, :PACKED] + g[:, 0:1] * lo1 + g[:, 1:2] * lo2
    o_ref[:, PACKED:] = x1_ref[:, PACKED:] + g[:, 0:1] * hi1 + g[:, 1:2] * hi2


def _combine(x1, gates, y2):
    T = x1.shape[0]
    nt = T // CB_TM
    return pl.pallas_call(
        _combine_kernel,
        grid=(nt,),
        in_specs=[pl.BlockSpec((CB_TM, D_MODEL), lambda i: (i, 0)),
                  pl.BlockSpec((CB_TM, RINFO), lambda i: (i, 0)),
                  pl.BlockSpec((1, TOP_K, CB_TM, PACKED), lambda i: (i, 0, 0, 0))],
        out_specs=pl.BlockSpec((CB_TM, D_MODEL), lambda i: (i, 0)),
        out_shape=jax.ShapeDtypeStruct((T, D_MODEL), F32),
        compiler_params=_cparams(1),
        name="combine",
    )(x1, gates, y2)


def _layer(x, rel_bias, ln1, w_in, q_norm, k_norm, attn_sink, conv_w, conv_b, lru_wa, lru_ba, lru_wi, lru_bi,
           lru_lambda, out_norm_attn, out_norm_lru, w_out, ln2, w_group, b_group, w_er, b_er, w_gate, w_up, w_down):
    B, S, D = x.shape
    T = B * S
    x2 = x.reshape(T, D)
    q, kv, xr, gr = _in_proj(x2, ln1, w_in, q_norm, k_norm)
    attn_n = _attention(q.reshape(B, S, ATTN_WIDTH), kv.reshape(B, S, 2 * KV_WIDTH), rel_bias, attn_sink,
                        out_norm_attn)
    lru_n = _rglru(xr.reshape(B, S, LRU_WIDTH), gr.reshape(B, S, LRU_WIDTH), conv_w, conv_b,
                   lru_wa, lru_ba, lru_wi, lru_bi, lru_lambda, out_norm_lru)
    x1, h2, gates, ei, cnt = _out_route(attn_n.reshape(T, ATTN_WIDTH), lru_n.reshape(T, LRU_WIDTH), x2, w_out, ln2,
                                        w_group, b_group, w_er, b_er)
    cap = _moe_cap(T)
    dest, pstart, block_expert, n_used, group_expert = _layout(ei, cnt, cap // MOE_BLOCK)
    xs = _padfill(_sc_dispatch(h2, dest, cap), pstart, cnt)
    yb = _experts(xs, block_expert, n_used, group_expert, w_gate, w_up, w_down)
    out = _combine(x1, gates, _sc_gather(yb, dest))
    return out.reshape(B, S, D)


def kernel(x, rel_bias, ln1, w_in, q_norm, k_norm, attn_sink, conv_w, conv_b, lru_wa, lru_ba, lru_wi, lru_bi,
           lru_lambda, out_norm_attn, out_norm_lru, w_out, ln2, w_group, b_group, w_expert_router, b_expert_router,
           w_gate, w_up, w_down):
    depth = ln1.shape[0]
    for l in range(depth):
        x = _layer(x, rel_bias, ln1[l], w_in[l], q_norm[l], k_norm[l], attn_sink[l], conv_w[l], conv_b[l],
                   lru_wa[l], lru_ba[l], lru_wi[l], lru_bi[l], lru_lambda[l], out_norm_attn[l], out_norm_lru[l],
                   w_out[l], ln2[l], w_group[l], b_group[l], w_expert_router[l], b_expert_router[l],
                   w_gate[l], w_up[l], w_down[l])
    return x
```

```python
import functools
import math

import jax
import jax.numpy as jnp
from jax import lax
from jax.experimental import pallas as pl
from jax.experimental.pallas import tpu as pltpu
from jax.experimental.pallas import tpu_sc as plsc

D_MODEL = 1024
N_HEADS = 8
N_KV_HEADS = 2
HEAD_DIM = 64
Q_PER_KV = N_HEADS // N_KV_HEADS
ATTN_WIDTH = N_HEADS * HEAD_DIM
KV_WIDTH = N_KV_HEADS * HEAD_DIM
WINDOW = 128
BLOCK = 128
NUM_BUCKETS = 32
MAX_DISTANCE = 128
LRU_WIDTH = D_MODEL - ATTN_WIDTH
LRU_BLOCKS = 8
LRU_BLOCK_DIM = LRU_WIDTH // LRU_BLOCKS
LRU_C = 8.0
CONV_W = 4
CONV_LEFT = 2
N_GROUPS = 4
EXPERTS_PER_GROUP = 8
N_EXPERTS = N_GROUPS * EXPERTS_PER_GROUP
TOP_K = 2
D_EXPERT = 512
MOE_BLOCK = 256
EPS = 1e-6
NEG_INF = -1e30

LANES = 128
SUBLANES = 8
VMEM_LIMIT = 56 * 1024 * 1024
LRU_VMEM_LIMIT = 62 * 1024 * 1024

F32 = jnp.float32
BF16 = jnp.bfloat16
LOG2E = math.log2(math.e)


def _cparams(n_axes, vmem=VMEM_LIMIT):
    return pltpu.CompilerParams(dimension_semantics=("arbitrary",) * n_axes, vmem_limit_bytes=vmem)


def _rms(x, gain):
    return x * lax.rsqrt(jnp.mean(x * x, axis=-1, keepdims=True) + EPS) * gain


U32 = jnp.uint32
HI_MASK = 0xFFFF0000
PACKED = D_MODEL // 2


def _pack_rows(x):
    return _pack_rounded(x.astype(BF16).astype(F32))


def _pack_rounded(xb):
    h = xb.shape[1] // 2
    lo = lax.bitcast_convert_type(xb[:, :h], U32) >> 16
    hi = lax.bitcast_convert_type(xb[:, h:], U32) & jnp.uint32(HI_MASK)
    return lo | hi


def _unpack_rows(p):
    lo = lax.bitcast_convert_type(p << 16, F32)
    hi = lax.bitcast_convert_type(p & jnp.uint32(HI_MASK), F32)
    return lo, hi


IN_TM = 1024


def _head_rms(x, n_heads, gain):
    head = lax.broadcasted_iota(jnp.int32, (1, n_heads * HEAD_DIM), 1) // HEAD_DIM
    x2 = x * x
    scale = jnp.zeros_like(x)
    for h in range(n_heads):
        ms = jnp.sum(jnp.where(head == h, x2, 0.0), axis=-1, keepdims=True) * (1.0 / HEAD_DIM)
        scale = jnp.where(head == h, lax.rsqrt(ms + EPS), scale)
    return x * scale * gain


def _in_proj_kernel(x_ref, g_ref, w_ref, qn_ref, kn_ref, q_ref, kv_ref, xr_ref, gr_ref, wb_ref, qg_ref, kg_ref):
    @pl.when(pl.program_id(0) == 0)
    def _():
        wb_ref[...] = w_ref[...].astype(BF16)
        qg_ref[...] = jnp.concatenate([qn_ref[...]] * N_HEADS, axis=1) * (HEAD_DIM ** -0.5 * LOG2E)
        kg_ref[...] = jnp.concatenate([kn_ref[...]] * N_KV_HEADS, axis=1)

    h = _rms(x_ref[...], g_ref[...]).astype(BF16)
    c_k = ATTN_WIDTH
    c_v = c_k + KV_WIDTH
    c_x = c_v + KV_WIDTH
    c_g = c_x + LRU_WIDTH
    q = jnp.dot(h, wb_ref[:, :c_k], preferred_element_type=F32)
    q_ref[...] = _head_rms(q, N_HEADS, qg_ref[...]).astype(BF16)
    k = jnp.dot(h, wb_ref[:, c_k:c_v], preferred_element_type=F32)
    kv_ref[:, :KV_WIDTH] = _head_rms(k, N_KV_HEADS, kg_ref[...]).astype(BF16)
    kv_ref[:, KV_WIDTH:] = jnp.dot(h, wb_ref[:, c_v:c_x], preferred_element_type=F32).astype(BF16)
    xr_ref[...] = jnp.dot(h, wb_ref[:, c_x:c_g], preferred_element_type=F32)
    gr_ref[...] = jnp.dot(h, wb_ref[:, c_g:], preferred_element_type=F32)


def _in_proj(x2, ln1, w_in, q_gain, k_gain):
    T = x2.shape[0]
    n_in = w_in.shape[1]
    row = lambda w: pl.BlockSpec((IN_TM, w), lambda i: (i, 0))
    return pl.pallas_call(
        _in_proj_kernel,
        grid=(T // IN_TM,),
        in_specs=[row(D_MODEL),
                  pl.BlockSpec((1, D_MODEL), lambda i: (0, 0)),
                  pl.BlockSpec((D_MODEL, n_in), lambda i: (0, 0)),
                  pl.BlockSpec((1, HEAD_DIM), lambda i: (0, 0)),
                  pl.BlockSpec((1, HEAD_DIM), lambda i: (0, 0))],
        out_specs=[row(ATTN_WIDTH), row(2 * KV_WIDTH), row(LRU_WIDTH), row(LRU_WIDTH)],
        out_shape=[jax.ShapeDtypeStruct((T, ATTN_WIDTH), BF16),
                   jax.ShapeDtypeStruct((T, 2 * KV_WIDTH), BF16),
                   jax.ShapeDtypeStruct((T, LRU_WIDTH), F32),
                   jax.ShapeDtypeStruct((T, LRU_WIDTH), F32)],
        scratch_shapes=[pltpu.VMEM((D_MODEL, n_in), BF16),
                        pltpu.VMEM((1, ATTN_WIDTH), F32),
                        pltpu.VMEM((1, KV_WIDTH), F32)],
        compiler_params=_cparams(1),
        name="in_proj",
    )(x2, ln1.reshape(1, D_MODEL), w_in, q_gain.reshape(1, HEAD_DIM).astype(F32),
      k_gain.reshape(1, HEAD_DIM).astype(F32))


def _t5_bucket(rel):
    half = NUM_BUCKETS // 2
    max_exact = half // 2
    base = jnp.where(rel > 0, half, 0)
    n = jnp.abs(rel)
    nf = jnp.maximum(n, 1).astype(jnp.float32)
    large = max_exact + (jnp.log(nf / max_exact) / math.log(MAX_DISTANCE / max_exact)
                         * (half - max_exact)).astype(jnp.int32)
    large = jnp.minimum(large, half - 1)
    return base + jnp.where(n < max_exact, n, large)


HEAD_PAIRS = Q_PER_KV // 2
EDGE_VARIANTS = 3


def _fill_bias_table(rb_ref, bucket_ref, band_ref, o_ref):
    bucket = bucket_ref[...]
    band = band_ref[...] > 0
    col = lax.broadcasted_iota(jnp.int32, bucket.shape, 1)
    valid = (band & (col >= BLOCK), band, band & (col < 2 * BLOCK))
    for h in range(N_HEADS):
        acc = jnp.zeros(bucket.shape, F32)
        for b in range(NUM_BUCKETS):
            acc = jnp.where(bucket == b, rb_ref[b, h], acc)
        kv, g = divmod(h, Q_PER_KV)
        pair, parity = divmod(g, 2)
        for var in range(EDGE_VARIANTS):
            o_ref[var, kv, parity, pair * BLOCK:(pair + 1) * BLOCK, :] = jnp.where(valid[var], acc * LOG2E, NEG_INF)


def _attn_kernel(sink_ref, rb_ref, q_ref, kp_ref, kc_ref, kn_ref, bucket_ref, band_ref, og_ref, o_ref, bias_ref):
    n = pl.program_id(1)

    @pl.when((pl.program_id(0) == 0) & (n == 0))
    def _():
        _fill_bias_table(rb_ref, bucket_ref, band_ref, bias_ref)

    kv_all = jnp.concatenate([kp_ref[0], kc_ref[0], kn_ref[0]], axis=0)
    for qb in range(ATTN_QB):
        variant = 1
        if qb == 0:
            variant = jnp.where(n == 0, 0, 1)
        if qb == ATTN_QB - 1:
            variant = jnp.where(n == pl.num_programs(1) - 1, 2, variant)
        out = _attn_block(q_ref[0, qb * BLOCK:(qb + 1) * BLOCK, :], kv_all[qb * BLOCK:(qb + 3) * BLOCK, :],
                          lambda kv, parity: bias_ref[variant, kv, parity], sink_ref)
        o_ref[0, qb * BLOCK:(qb + 1) * BLOCK, :] = _rms(out, og_ref[...]).astype(o_ref.dtype)


def _attn_block(q, kvw, bias, sink_ref):
    low = lax.broadcasted_iota(jnp.int32, (1, LANES), 1) < HEAD_DIM
    swap = lambda slab: pltpu.roll(slab.astype(F32), HEAD_DIM, 1).astype(BF16)
    kslab, vslab = kvw[:, :KV_WIDTH], kvw[:, KV_WIDTH:]
    kslab_sw, vslab_sw = swap(kslab), swap(vslab)
    rowi = lax.broadcasted_iota(jnp.int32, (HEAD_PAIRS * BLOCK, 1), 0)
    combos = [(kv, parity) for kv in range(N_KV_HEADS) for parity in range(2)]
    scores, vzs, sinks = [], [], []
    for kv, parity in combos:
        ks, vs = (kslab, vslab) if (kv == 0) == (parity == 0) else (kslab_sw, vslab_sw)
        keep = low if parity == 0 else jnp.logical_not(low)
        kz = jnp.where(keep, ks, jnp.zeros_like(ks))
        vzs.append(jnp.where(keep, vs, jnp.zeros_like(vs)))
        base = kv * Q_PER_KV * HEAD_DIM
        qpair = jnp.concatenate([q[:, base + j * LANES:base + (j + 1) * LANES] for j in range(HEAD_PAIRS)], axis=0)
        s = lax.dot_general(qpair, kz, (((1,), (1,)), ((), ())), preferred_element_type=F32)
        scores.append(s + bias(kv, parity))
        sink = jnp.zeros((HEAD_PAIRS * BLOCK, 1), F32)
        for j in range(HEAD_PAIRS):
            sink = jnp.where(rowi // BLOCK == j, sink_ref[kv * Q_PER_KV + 2 * j + parity], sink)
        sinks.append(sink)
    probs, inv = [], []
    for s, sink in zip(scores, sinks):
        m = jnp.maximum(jnp.max(s, axis=-1, keepdims=True), sink)
        p = jnp.exp2(s - m)
        inv.append(1.0 / (jnp.sum(p, axis=-1, keepdims=True) + jnp.exp2(sink - m)))
        probs.append(p.astype(BF16))
    outs = [jnp.dot(p, vz, preferred_element_type=F32) * r for p, vz, r in zip(probs, vzs, inv)]
    cols = []
    for kv in range(N_KV_HEADS):
        acc = outs[2 * kv] + outs[2 * kv + 1]
        cols += [acc[j * BLOCK:(j + 1) * BLOCK, :] for j in range(HEAD_PAIRS)]
    return jnp.concatenate(cols, axis=1)


ATTN_QB = 8


def _attention(q, kv, rel_bias, sink, out_gain):
    B, S, _ = q.shape
    nb = S // BLOCK
    assert ATTN_QB >= 2 and nb % ATTN_QB == 0, "a step's first and last query blocks must be distinct"
    ns = nb // ATTN_QB
    rows = ATTN_QB * BLOCK
    qi = jnp.arange(BLOCK, dtype=jnp.int32)
    kj = jnp.arange(3 * BLOCK, dtype=jnp.int32)
    rel = kj[None, :] - BLOCK - qi[:, None]
    bucket = _t5_bucket(rel).astype(jnp.int32)
    band = (jnp.abs(rel) <= WINDOW).astype(jnp.int32)
    kvspec = lambda f: pl.BlockSpec((1, BLOCK, 2 * KV_WIDTH), f)
    smem = pl.BlockSpec(memory_space=pltpu.SMEM)
    geom = pl.BlockSpec((BLOCK, 3 * BLOCK), lambda b, n: (0, 0))
    return pl.pallas_call(
        _attn_kernel,
        grid=(B, ns),
        in_specs=[smem, smem,
                  pl.BlockSpec((1, rows, ATTN_WIDTH), lambda b, n: (b, n, 0)),
                  kvspec(lambda b, n: (b, jnp.maximum(n * ATTN_QB - 1, 0), 0)),
                  pl.BlockSpec((1, rows, 2 * KV_WIDTH), lambda b, n: (b, n, 0)),
                  kvspec(lambda b, n: (b, jnp.minimum((n + 1) * ATTN_QB, nb - 1), 0)),
                  geom, geom,
                  pl.BlockSpec((1, ATTN_WIDTH), lambda b, n: (0, 0))],
        out_specs=pl.BlockSpec((1, rows, ATTN_WIDTH), lambda b, n: (b, n, 0)),
        out_shape=jax.ShapeDtypeStruct((B, S, ATTN_WIDTH), BF16),
        scratch_shapes=[pltpu.VMEM((EDGE_VARIANTS, N_KV_HEADS, 2, HEAD_PAIRS * BLOCK, 3 * BLOCK), F32)],
        compiler_params=_cparams(2),
        name="attention",
    )(sink.astype(F32) * LOG2E, rel_bias.astype(F32), q, kv, kv, kv, bucket, band, out_gain.reshape(1, ATTN_WIDTH))


LRU_TC = 128
LRU_PITCH = LRU_TC + SUBLANES // 2
LRU_SLABS = LRU_WIDTH // LANES
LRU_UNROLL = 32
HALO = SUBLANES


def _softplus(x):
    return jnp.maximum(x, 0.0) + jnp.log(1.0 + jnp.exp(-jnp.abs(x)))


def _gelu_tanh(x):
    k = math.sqrt(2.0 / math.pi)
    hx = 0.5 * x
    return hx + hx * jnp.tanh(x * (k + (k * 0.044715) * (x * x)))


def _sigmoid(x):
    return 0.5 + 0.5 * jnp.tanh(0.5 * x)


def _rglru_kernel(xr_ref, xp_ref, xn_ref, gr_ref, cw_ref, cb_ref, wa_ref, wi_ref, ba_ref, bi_ref, lam_ref, og_ref,
                  o_ref, sx_ref, a_ref, u_ref, h_ref, carry_ref, hf_ref, xcs_ref, wg_ref, bg_ref, k_ref):
    p = pl.program_id(0)
    i = pl.program_id(1)
    nc = pl.num_programs(1)
    c = i + p * (nc - 1 - 2 * i)
    B = xr_ref.shape[0]
    TC = LRU_TC

    @pl.when(i == 0)
    def _():
        carry_ref[...] = jnp.zeros_like(carry_ref)
        wg_ref[...] = jnp.zeros_like(wg_ref)
        for sel, w_ref in enumerate((wa_ref, wi_ref)):
            for h in range(LRU_BLOCKS):
                lo = h * LRU_BLOCK_DIM
                wg_ref[lo:lo + LRU_BLOCK_DIM, sel * LRU_WIDTH + lo:sel * LRU_WIDTH + lo + LRU_BLOCK_DIM] = (
                    0.5 * w_ref[0, h]).astype(BF16)
        row = pl.ds(p, 1)
        bg_ref[:, :LRU_WIDTH] = 0.5 * ba_ref[row, :]
        bg_ref[:, LRU_WIDTH:] = 0.5 * bi_ref[row, :]
        k_ref[...] = (-0.5 * LRU_C * math.log2(math.e)) * _softplus(-lam_ref[row, :])

    def gates_and_scan(xc2, backward):
        g = jnp.dot(xc2.astype(BF16), wg_ref[...], preferred_element_type=F32) + bg_ref[...]
        ta = jnp.tanh(g[:, :LRU_WIDTH])
        ig = 0.5 + 0.5 * jnp.tanh(g[:, LRU_WIDTH:])
        a = jnp.exp2((1.0 + ta) * k_ref[...])
        z = 1.0 - a * a
        u = z * lax.rsqrt(jnp.maximum(z, 1e-30)) * ig * xc2
        for b in range(B):
            for s in range(LRU_SLABS):
                a_ref[s, b * LRU_PITCH:b * LRU_PITCH + TC, :] = a[b * TC:(b + 1) * TC, s * LANES:(s + 1) * LANES]
                u_ref[s, b * LRU_PITCH:b * LRU_PITCH + TC, :] = u[b * TC:(b + 1) * TC, s * LANES:(s + 1) * LANES]

        def trip(i, hs):
            t0 = pl.multiple_of((TC // LRU_UNROLL - 1 - i if backward else i) * LRU_UNROLL, LRU_UNROLL)
            for j in range(LRU_UNROLL):
                t = t0 + (LRU_UNROLL - 1 - j if backward else j)
                out = []
                for s in range(LRU_SLABS):
                    idx = pl.ds(t, B, stride=LRU_PITCH)
                    hn = a_ref[s, idx, :] * hs[s] + u_ref[s, idx, :]
                    h_ref[s, idx, :] = hn
                    out.append(hn)
                hs = tuple(out)
            return hs

        hs = lax.fori_loop(0, TC // LRU_UNROLL, trip, tuple(carry_ref[s] for s in range(LRU_SLABS)))
        for s in range(LRU_SLABS):
            carry_ref[s] = hs[s]

    @pl.when(p == 0)
    def _():
        sx_ref[:, HALO:HALO + TC, :] = xr_ref[...]
        sx_ref[:, 0:HALO, :] = jnp.where(c > 0, xp_ref[...], 0.0)
        sx_ref[:, HALO + TC:, :] = jnp.where(c < nc - 1, xn_ref[...], 0.0)
        xc = cb_ref[...][None]
        for j in range(CONV_W):
            off = HALO + j - CONV_LEFT
            xc = xc + cw_ref[j:j + 1, :][None] * sx_ref[:, off:off + TC, :]
        xc2 = xc.reshape(B * TC, LRU_WIDTH)
        xcs_ref[c] = xc2.astype(xcs_ref.dtype)
        gates_and_scan(xc2, backward=False)
        for b in range(B):
            for s in range(LRU_SLABS):
                hf_ref[c, s, b * TC:(b + 1) * TC, :] = h_ref[s, b * LRU_PITCH:b * LRU_PITCH + TC, :].astype(hf_ref.dtype)

    @pl.when(p == 1)
    def _():
        gates_and_scan(xcs_ref[c].astype(F32), backward=True)
        for b in range(B):
            hsum = jnp.concatenate(
                [h_ref[s, b * LRU_PITCH:b * LRU_PITCH + TC, :] + hf_ref[c, s, b * TC:(b + 1) * TC, :].astype(F32)
                 for s in range(LRU_SLABS)], axis=1)
            y = hsum * _gelu_tanh(gr_ref[b])
            o_ref[b] = _rms(y, og_ref[...]).astype(o_ref.dtype)


def _rglru(xr, gr, conv_w, conv_b, w_a, b_a, w_i, b_i, lam, out_gain):
    B, S, W = xr.shape
    nc = S // LRU_TC
    hb = LRU_TC // HALO
    fwd = lambda p, i: jnp.where(p == 0, i, nc - 1)
    bwd = lambda p, i: nc - 1 - p * i
    full2 = lambda shape: pl.BlockSpec(shape, lambda p, i: (0,) * len(shape))
    wblock = pl.BlockSpec((1, LRU_BLOCKS, LRU_BLOCK_DIM, LRU_BLOCK_DIM), lambda p, i: (p, 0, 0, 0))
    return pl.pallas_call(
        _rglru_kernel,
        grid=(2, nc),
        in_specs=[pl.BlockSpec((B, LRU_TC, W), lambda p, i: (0, fwd(p, i), 0)),
                  pl.BlockSpec((B, HALO, W), lambda p, i: (0, jnp.maximum(fwd(p, i) * hb - 1, 0), 0)),
                  pl.BlockSpec((B, HALO, W), lambda p, i: (0, jnp.minimum((fwd(p, i) + 1) * hb, S // HALO - 1), 0)),
                  pl.BlockSpec((B, LRU_TC, W), lambda p, i: (0, bwd(p, i), 0)),
                  full2((CONV_W, W)),
                  full2((1, W)),
                  wblock, wblock,
                  full2((2, W)), full2((2, W)), full2((2, W)),
                  full2((1, W))],
        out_specs=pl.BlockSpec((B, LRU_TC, W), lambda p, i: (0, bwd(p, i), 0)),
        out_shape=jax.ShapeDtypeStruct((B, S, W), BF16),
        scratch_shapes=[pltpu.VMEM((B, LRU_TC + 2 * HALO, W), F32),
                        pltpu.VMEM((LRU_SLABS, B * LRU_PITCH, LANES), F32),
                        pltpu.VMEM((LRU_SLABS, B * LRU_PITCH, LANES), F32),
                        pltpu.VMEM((LRU_SLABS, B * LRU_PITCH, LANES), F32),
                        pltpu.VMEM((LRU_SLABS, B, LANES), F32),
                        pltpu.VMEM((nc, LRU_SLABS, B * LRU_TC, LANES), BF16),
                        pltpu.VMEM((nc, B * LRU_TC, W), BF16),
                        pltpu.VMEM((W, 2 * W), BF16),
                        pltpu.VMEM((1, 2 * W), F32),
                        pltpu.VMEM((1, W), F32)],
        compiler_params=_cparams(2, LRU_VMEM_LIMIT),
        name="rglru",
    )(xr, xr, xr, gr, conv_w.astype(F32), conv_b.reshape(1, W).astype(F32), w_a.astype(F32), w_i.astype(F32),
      b_a.astype(F32), b_i.astype(F32), lam.astype(F32), out_gain.reshape(1, W).astype(F32))


RT_TM = 1024
RT_PARTS = 4
RT_COLS = LANES
RT_ROWS = 48
RINFO = SUBLANES


def _split_bf16(x):
    hi = x.astype(BF16)
    lo = (x - hi.astype(F32)).astype(BF16)
    return hi, lo


def _route_kernel(an_ref, ln_ref, x_ref, wo_ref, g2_ref, wr_ref, br_ref,
                  x1_ref, h2_ref, gt_ref, ei_ref, cnt_ref, wob_ref, wrb_ref, tri_ref, run_ref, runc_ref):
    @pl.when(pl.program_id(0) == 0)
    def _():
        wob_ref[...] = wo_ref[...].astype(BF16)
        hi, lo = _split_bf16(wr_ref[...])
        wrb_ref[:RT_ROWS, :] = hi
        wrb_ref[RT_ROWS:, :] = lo
        r = lax.broadcasted_iota(jnp.int32, (RT_TM, RT_TM), 0)
        cidx = lax.broadcasted_iota(jnp.int32, (RT_TM, RT_TM), 1)
        tri_ref[...] = (r < cidx).astype(BF16)
        run_ref[...] = jnp.zeros_like(run_ref)
        runc_ref[...] = jnp.zeros_like(runc_ref)

    nt_dims = (((1,), (1,)), ((), ()))
    part = RT_TM // RT_PARTS
    x1s = []
    for r in range(RT_PARTS):
        rows = slice(r * part, (r + 1) * part)
        x1 = (x_ref[rows, :]
              + jnp.dot(an_ref[rows, :], wob_ref[:ATTN_WIDTH, :], preferred_element_type=F32)
              + jnp.dot(ln_ref[rows, :], wob_ref[ATTN_WIDTH:, :], preferred_element_type=F32))
        x1_ref[rows, :] = x1
        x1s.append(x1)
    splits = []
    for r, x1 in enumerate(x1s):
        h2 = _rms(x1, g2_ref[...])
        hi = h2.astype(BF16)
        hi_f = hi.astype(F32)
        h2_ref[r * part:(r + 1) * part, :] = _pack_rounded(hi_f)
        splits.append((hi, (h2 - hi_f).astype(BF16)))
    logits = []
    for hi, lo in splits:
        t1 = lax.dot_general(wrb_ref[...], hi, nt_dims, preferred_element_type=F32)
        t2 = lax.dot_general(wrb_ref[:RT_ROWS, :], lo, nt_dims, preferred_element_type=F32)
        logits.append(t1[:RT_ROWS] + t1[RT_ROWS:] + t2)
    logit = jnp.concatenate(logits, axis=1) + br_ref[...]

    sub = lax.broadcasted_iota(jnp.int32, (SUBLANES, RT_TM), 0)
    first_min = lambda hit: jnp.min(jnp.where(hit, sub, SUBLANES), axis=0, keepdims=True)
    is_g = sub < N_GROUPS
    gl = jnp.where(is_g, logit[:SUBLANES], -jnp.inf)
    gm = jnp.max(gl, axis=0, keepdims=True)
    gidx = first_min(gl == gm)
    g_p = 1.0 / jnp.sum(jnp.where(is_g, jnp.exp(logit[:SUBLANES] - gm), 0.0), axis=0, keepdims=True)
    el = logit[SUBLANES:2 * SUBLANES]
    for g in range(1, N_GROUPS):
        el = jnp.where(gidx == g, logit[(g + 1) * SUBLANES:(g + 2) * SUBLANES], el)
    m1 = jnp.max(el, axis=0, keepdims=True)
    i1 = first_min(el == m1)
    el2 = jnp.where(sub == i1, -jnp.inf, el)
    m2 = jnp.max(el2, axis=0, keepdims=True)
    i2 = first_min(el2 == m2)
    t = jnp.exp(m2 - m1)
    gate1 = g_p / (1.0 + t)
    gate2 = g_p * t / (1.0 + t)
    e1 = gidx * EXPERTS_PER_GROUP + i1
    e2 = gidx * EXPERTS_PER_GROUP + i2

    erow = lax.broadcasted_iota(jnp.int32, (N_EXPERTS, RT_TM), 0)
    oh1 = erow == e1
    oh2 = erow == e2
    oh = (oh1 | oh2).astype(F32)
    ohb = oh.astype(BF16)
    cum = jnp.dot(ohb, tri_ref[...], preferred_element_type=F32) + runc_ref[...]
    rank1 = jnp.sum(jnp.where(oh1, cum, 0.0), axis=0, keepdims=True)
    rank2 = jnp.sum(jnp.where(oh2, cum, 0.0), axis=0, keepdims=True)
    runc_ref[...] = runc_ref[...] + jnp.sum(oh, axis=1, keepdims=True)
    tile_cnt = lax.dot_general(jnp.ones((SUBLANES, RT_TM), BF16), ohb, nt_dims, preferred_element_type=F32)
    run_ref[:, :N_EXPERTS] = run_ref[:, :N_EXPERTS] + tile_cnt[0:1]
    cnt_ref[...] = run_ref[...].astype(jnp.int32)

    rows = [e1, e2, rank1.astype(jnp.int32), rank2.astype(jnp.int32)]
    ei = jnp.zeros((RINFO, RT_TM), jnp.int32)
    for k, v in enumerate(rows):
        ei = jnp.where(sub == k, v, ei)
    ei_ref[0] = ei
    gt_ref[...] = jnp.where(sub == 0, gate1, jnp.where(sub == 1, gate2, 0.0)).T


def _out_route(attn_n, lru_n, x2, w_out, ln2, w_group, b_group, w_er, b_er):
    T = x2.shape[0]
    pad_g = SUBLANES - N_GROUPS
    wr = jnp.concatenate([jnp.pad(w_group.T, ((0, pad_g), (0, 0))),
                          jnp.transpose(w_er, (0, 2, 1)).reshape(N_EXPERTS, D_MODEL)], axis=0)
    wr = jnp.pad(wr, ((0, RT_ROWS - wr.shape[0]), (0, 0))).astype(F32)
    br = jnp.concatenate([jnp.pad(b_group, (0, pad_g)), b_er.reshape(-1)])
    br = jnp.pad(br, (0, RT_ROWS - br.shape[0])).reshape(RT_ROWS, 1).astype(F32)
    row = lambda w: pl.BlockSpec((RT_TM, w), lambda i: (i, 0))
    const = lambda shape: pl.BlockSpec(shape, lambda i: (0, 0))
    return pl.pallas_call(
        _route_kernel,
        grid=(T // RT_TM,),
        in_specs=[row(ATTN_WIDTH), row(LRU_WIDTH), row(D_MODEL), const((D_MODEL, D_MODEL)), const((1, D_MODEL)),
                  const((RT_ROWS, D_MODEL)), const((RT_ROWS, 1))],
        out_specs=[row(D_MODEL), row(PACKED), row(RINFO),
                   pl.BlockSpec((1, RINFO, RT_TM), lambda i: (i, 0, 0)), const((1, RT_COLS))],
        out_shape=[jax.ShapeDtypeStruct((T, D_MODEL), F32),
                   jax.ShapeDtypeStruct((T, PACKED), U32),
                   jax.ShapeDtypeStruct((T, RINFO), F32),
                   jax.ShapeDtypeStruct((T // RT_TM, RINFO, RT_TM), jnp.int32),
                   jax.ShapeDtypeStruct((1, RT_COLS), jnp.int32)],
        scratch_shapes=[pltpu.VMEM((D_MODEL, D_MODEL), BF16),
                        pltpu.VMEM((2 * RT_ROWS, D_MODEL), BF16),
                        pltpu.VMEM((RT_TM, RT_TM), BF16),
                        pltpu.VMEM((1, RT_COLS), F32),
                        pltpu.VMEM((N_EXPERTS, 1), F32)],
        compiler_params=_cparams(1),
        name="out_route",
    )(attn_n, lru_n, x2, w_out, ln2.reshape(1, D_MODEL).astype(F32), wr, br)


def _moe_cap(T):
    A = T * TOP_K
    return ((A + MOE_BLOCK - 1) // MOE_BLOCK) * MOE_BLOCK + N_EXPERTS * MOE_BLOCK


PAD_BITS = tuple(1 << b for b in reversed(range(3, MOE_BLOCK.bit_length() - 1)))


def _layout_kernel(cnt_ref, ei_ref, dest_ref, pstart, be_ref, nu_ref, ge_ref):
    n_blocks = be_ref.shape[0]

    def lay(e, carry):
        start, blk, grp = carry
        pstart[e] = start
        nb = (cnt_ref[0, e] + MOE_BLOCK - 1) // MOE_BLOCK
        ge_ref[grp] = e

        def fill(k, c):
            be_ref[blk + k] = e
            return c
        lax.fori_loop(0, nb, fill, 0)
        return start + nb * MOE_BLOCK, blk + nb, grp + (nb > 0).astype(jnp.int32)
    _, used, groups = lax.fori_loop(0, N_EXPERTS, lay, (jnp.int32(0), jnp.int32(0), jnp.int32(0)))
    nu_ref[0] = used

    def tail(k, c):
        be_ref[k] = N_EXPERTS - 1
        return c
    lax.fori_loop(used, n_blocks, tail, 0)

    def no_group(k, c):
        ge_ref[k] = -1
        return c
    lax.fori_loop(groups, ge_ref.shape[0], no_group, 0)

    expert = ei_ref[:, 0:TOP_K, :]
    dest = ei_ref[:, TOP_K:2 * TOP_K, :]
    for e in range(N_EXPERTS):
        dest = dest + jnp.where(expert == e, pstart[e], 0)
    dest_ref[...] = dest


def _layout(ei, cnt, n_blocks):
    nt = ei.shape[0]
    smem = pl.BlockSpec(memory_space=pltpu.SMEM)
    vmem = pl.BlockSpec(memory_space=pltpu.VMEM)
    return pl.pallas_call(
        _layout_kernel,
        in_specs=[smem, vmem],
        out_specs=[vmem, smem, smem, smem, smem],
        out_shape=[jax.ShapeDtypeStruct((nt, TOP_K, RT_TM), jnp.int32),
                   jax.ShapeDtypeStruct((N_EXPERTS,), jnp.int32),
                   jax.ShapeDtypeStruct((n_blocks,), jnp.int32),
                   jax.ShapeDtypeStruct((1,), jnp.int32),
                   jax.ShapeDtypeStruct((N_EXPERTS + W_AHEAD,), jnp.int32)],
        name="layout",
    )(cnt, ei)


SC_CHUNK = 64
SC_BUFS = 3
SC_LEAD = SC_BUFS - 1


def _sc_workers():
    info = plsc.get_sparse_core_info()
    return info.num_cores, info.num_subcores


def _sc_ring(n_chunks, read, write):
    for c in range(min(SC_LEAD, n_chunks)):
        for cp in read(c):
            cp.start()
    reclaimed = set()
    for c in range(n_chunks):
        for cp in read(c):
            cp.wait()
        for cp in write(c):
            cp.start()
        nxt = c + SC_LEAD
        if nxt < n_chunks:
            if nxt - SC_BUFS >= 0:
                for cp in write(nxt - SC_BUFS):
                    cp.wait()
                reclaimed.add(nxt - SC_BUFS)
            for cp in read(nxt):
                cp.start()
    for c in range(n_chunks):
        if c not in reclaimed:
            for cp in write(c):
                cp.wait()


def _sc_dispatch(h2p, dest, cap):
    T = h2p.shape[0]
    nc, ns = _sc_workers()
    per_w = T // (nc * ns)
    n_ch = per_w // SC_CHUNK
    nt, _, tm = dest.shape
    assert nt * tm == T and tm % per_w == 0 and per_w % SC_CHUNK == 0
    idx = dest.reshape(nt, TOP_K, tm // per_w, per_w).transpose(0, 2, 1, 3).reshape(nc * ns, TOP_K * n_ch, SC_CHUNK)
    mesh = plsc.VectorSubcoreMesh(core_axis_name="c", subcore_axis_name="s")

    @functools.partial(
        pl.kernel, mesh=mesh,
        out_type=jax.ShapeDtypeStruct((cap, PACKED), U32),
        scratch_types=[pltpu.VMEM((TOP_K * n_ch, SC_CHUNK), jnp.int32),
                       pltpu.VMEM((SC_BUFS, SC_CHUNK, PACKED), U32),
                       pltpu.SemaphoreType.DMA((SC_BUFS,)),
                       pltpu.SemaphoreType.DMA((SC_BUFS,))])
    def scatter(src_hbm, idx_hbm, out_hbm, idx_v, rows_v, rsem, wsem):
        wid = lax.axis_index("s") * nc + lax.axis_index("c")
        base = pl.multiple_of(wid * per_w, per_w)
        pltpu.sync_copy(idx_hbm.at[wid], idx_v)

        def read(c):
            b = c % SC_BUFS
            return [pltpu.make_async_copy(src_hbm.at[pl.ds(base + c * SC_CHUNK, SC_CHUNK)], rows_v.at[b], rsem.at[b])]

        def write(c):
            b = c % SC_BUFS
            return [pltpu.make_async_copy(rows_v.at[b], out_hbm.at[idx_v.at[k * n_ch + c]], wsem.at[b])
                    for k in range(TOP_K)]
        _sc_ring(n_ch, read, write)

    return scatter(h2p, idx)


def _sc_gather(yb, dest):
    nt, _, tm = dest.shape
    nc, ns = _sc_workers()
    n_rows = nt * TOP_K * tm
    per_w = n_rows // (nc * ns)
    n_ch = per_w // SC_CHUNK
    assert per_w * nc * ns == n_rows and per_w % SC_CHUNK == 0
    mesh = plsc.VectorSubcoreMesh(core_axis_name="c", subcore_axis_name="s")

    @functools.partial(
        pl.kernel, mesh=mesh,
        out_type=jax.ShapeDtypeStruct((n_rows, PACKED), U32),
        scratch_types=[pltpu.VMEM((per_w,), jnp.int32),
                       pltpu.VMEM((SC_BUFS, SC_CHUNK, PACKED), U32),
                       pltpu.SemaphoreType.DMA((SC_BUFS,)),
                       pltpu.SemaphoreType.DMA((SC_BUFS,))])
    def gather(table_hbm, idx_hbm, out_hbm, idx_v, rows_v, gsem, wsem):
        wid = lax.axis_index("s") * nc + lax.axis_index("c")
        base = pl.multiple_of(wid * per_w, per_w)
        pltpu.sync_copy(idx_hbm.at[pl.ds(base, per_w)], idx_v)

        def read(c):
            b = c % SC_BUFS
            return [pltpu.make_async_copy(table_hbm.at[idx_v.at[pl.ds(c * SC_CHUNK, SC_CHUNK)]], rows_v.at[b], gsem.at[b])]

        def write(c):
            b = c % SC_BUFS
            return [pltpu.make_async_copy(rows_v.at[b], out_hbm.at[pl.ds(base + c * SC_CHUNK, SC_CHUNK)], wsem.at[b])]
        _sc_ring(n_ch, read, write)

    return gather(yb, dest.reshape(n_rows)).reshape(nt, TOP_K, tm, PACKED)


def _padfill_kernel(cnt_ref, pstart, xs_in, xs_ref, zeros, zsem):
    del xs_in

    def pad_copies(fn):
        for e in range(N_EXPERTS):
            cnt = cnt_ref[0, e]
            head = (-cnt) & (SUBLANES - 1)
            rest = ((-cnt) & (MOE_BLOCK - 1)) - head
            off = pstart[e] + cnt
            for k in range(SUBLANES - 1):
                @pl.when(k < head)
                def _(off=off, k=k):
                    fn(pltpu.make_async_copy(zeros.at[pl.ds(0, 1), :], xs_ref.at[pl.ds(off + k, 1), :], zsem))
            off = off + head
            for bit in PAD_BITS:
                @pl.when((rest & bit) != 0)
                def _(off=off, bit=bit):
                    fn(pltpu.make_async_copy(zeros.at[pl.ds(0, bit), :],
                                             xs_ref.at[pl.ds(pl.multiple_of(off, SUBLANES), bit), :], zsem))
                off = off + (rest & bit)

    zeros[...] = jnp.zeros_like(zeros)
    pad_copies(lambda cp: cp.start())
    pad_copies(lambda cp: cp.wait())


def _padfill(xs, pstart, cnt):
    smem = pl.BlockSpec(memory_space=pltpu.SMEM)
    hbm = pl.BlockSpec(memory_space=pl.ANY)
    return pl.pallas_call(
        _padfill_kernel,
        in_specs=[smem, smem, hbm],
        out_specs=hbm,
        out_shape=jax.ShapeDtypeStruct(xs.shape, xs.dtype),
        input_output_aliases={2: 0},
        scratch_shapes=[pltpu.VMEM((MOE_BLOCK // 2, PACKED), U32), pltpu.SemaphoreType.DMA(())],
        name="padfill",
    )(cnt, pstart, xs)


W_SLOTS = 3
W_AHEAD = W_SLOTS - 1
EXPERT_GROUP = 8
EXPERT_RUNS = (1, 2, 4)


def _expert_kernel(be_ref, nu_ref, ge_ref, x_ref, wg_hbm, wu_hbm, wd_hbm, o_ref,
                   wgf, wuf, wdf, grp_ref, sems):
    step = pl.program_id(0)

    def weight_copies(e, slot):
        return (pltpu.make_async_copy(wg_hbm.at[e], wgf.at[slot], sems.at[slot, 0]),
                pltpu.make_async_copy(wu_hbm.at[e], wuf.at[slot], sems.at[slot, 1]),
                pltpu.make_async_copy(wd_hbm.at[e], wdf.at[slot], sems.at[slot, 2]))

    @pl.when(step == 0)
    def _():
        grp_ref[0] = 0
        for a in range(W_AHEAD):
            @pl.when(ge_ref[a] >= 0)
            def _(a=a):
                for cp in weight_copies(ge_ref[a], a):
                    cp.start()

    n_blocks = be_ref.shape[0]
    n_used = nu_ref[0]

    def swiglu(s, n, slot):
        rows = pl.ds(pl.multiple_of(s * MOE_BLOCK, MOE_BLOCK), n * MOE_BLOCK)
        lo, hi = _unpack_rows(x_ref[rows, :])
        lo = lo.astype(BF16)
        hi = hi.astype(BF16)
        g = (jnp.dot(lo, wgf[slot, :PACKED, :], preferred_element_type=F32)
             + jnp.dot(hi, wgf[slot, PACKED:, :], preferred_element_type=F32))
        u = (jnp.dot(lo, wuf[slot, :PACKED, :], preferred_element_type=F32)
             + jnp.dot(hi, wuf[slot, PACKED:, :], preferred_element_type=F32))
        h = (g * _sigmoid(g) * u).astype(BF16)
        o_ref[rows, :] = _pack_rows(jnp.dot(h, wdf[slot], preferred_element_type=F32))

    def run(s):
        j = step * EXPERT_GROUP + s
        e = be_ref[j]
        first = jnp.logical_or(j == 0, e != be_ref[jnp.maximum(j - 1, 0)])

        @pl.when(first)
        def _():
            grp = grp_ref[0]
            slot = grp % W_SLOTS
            for cp in weight_copies(e, slot):
                cp.wait()
            nxt = ge_ref[grp + W_AHEAD]

            @pl.when(nxt >= 0)
            def _():
                for cp in weight_copies(nxt, (grp + W_AHEAD) % W_SLOTS):
                    cp.start()
            grp_ref[0] = grp + 1

        def same(k):
            return (s + k < EXPERT_GROUP) & (j + k < n_used) & (be_ref[jnp.minimum(j + k, n_blocks - 1)] == e)
        take = jnp.int32(1)
        for n in EXPERT_RUNS[1:]:
            ok = same(n - 1)
            for k in range(1, n - 1):
                ok = ok & same(k)
            take = jnp.where(ok, n, take)
        slot = (grp_ref[0] + W_SLOTS - 1) % W_SLOTS
        for n in EXPERT_RUNS:
            @pl.when(take == n)
            def _(n=n):
                swiglu(s, n, slot)
        return s + take

    lax.while_loop(lambda s: (s < EXPERT_GROUP) & (step * EXPERT_GROUP + s < n_used), run, jnp.int32(0))


def _experts(xs, block_expert, n_used, group_expert, w_gate, w_up, w_down):
    cap = xs.shape[0]
    n_blocks = cap // MOE_BLOCK
    assert n_blocks % EXPERT_GROUP == 0
    rows = EXPERT_GROUP * MOE_BLOCK
    last = lambda j, be, nu, ge: jnp.minimum(j, (nu[0] - 1) // EXPERT_GROUP)
    hbm = pl.BlockSpec(memory_space=pl.ANY)
    gs = pltpu.PrefetchScalarGridSpec(
        num_scalar_prefetch=3,
        grid=(n_blocks // EXPERT_GROUP,),
        in_specs=[pl.BlockSpec((rows, PACKED), lambda j, be, nu, ge: (last(j, be, nu, ge), 0)), hbm, hbm, hbm],
        out_specs=pl.BlockSpec((rows, PACKED), lambda j, be, nu, ge: (last(j, be, nu, ge), 0)),
        scratch_shapes=[pltpu.VMEM((W_SLOTS, D_MODEL, D_EXPERT), F32),
                        pltpu.VMEM((W_SLOTS, D_MODEL, D_EXPERT), F32),
                        pltpu.VMEM((W_SLOTS, D_EXPERT, D_MODEL), F32),
                        pltpu.SMEM((1,), jnp.int32),
                        pltpu.SemaphoreType.DMA((W_SLOTS, 3))],
    )
    return pl.pallas_call(
        _expert_kernel,
        grid_spec=gs,
        out_shape=jax.ShapeDtypeStruct((cap, PACKED), U32),
        compiler_params=_cparams(1),
        name="experts",
    )(block_expert, n_used, group_expert, xs, w_gate, w_up, w_down)


CB_TILES = 2
CB_TM = CB_TILES * RT_TM


def _combine_kernel(x1_ref, gt_ref, y2_ref, o_ref):
    for t in range(CB_TILES):
        rows = slice(t * RT_TM, (t + 1) * RT_TM)
        g = gt_ref[rows, :]
        lo1, hi1 = _unpack_rows(y2_ref[t, 0])
        lo2, hi2 = _unpack_rows(y2_ref[t, 1])
        o_ref[rows, :PACKED] = x1_ref[rows, :PACKED] + g[:, 0:1] * lo1 + g[:, 1:2] * lo2
        o_ref[rows, PACKED:] = x1_ref[rows, PACKED:] + g[:, 0:1] * hi1 + g[:, 1:2] * hi2


def _combine(x1, gates, y2):
    T = x1.shape[0]
    nt = T // CB_TM
    return pl.pallas_call(
        _combine_kernel,
        grid=(nt,),
        in_specs=[pl.BlockSpec((CB_TM, D_MODEL), lambda i: (i, 0)),
                  pl.BlockSpec((CB_TM, RINFO), lambda i: (i, 0)),
                  pl.BlockSpec((CB_TILES, TOP_K, RT_TM, PACKED), lambda i: (i, 0, 0, 0))],
        out_specs=pl.BlockSpec((CB_TM, D_MODEL), lambda i: (i, 0)),
        out_shape=jax.ShapeDtypeStruct((T, D_MODEL), F32),
        compiler_params=_cparams(1),
        name="combine",
    )(x1, gates, y2)


def _layer(x, rel_bias, ln1, w_in, q_norm, k_norm, attn_sink, conv_w, conv_b, lru_wa, lru_ba, lru_wi, lru_bi,
           lru_lambda, out_norm_attn, out_norm_lru, w_out, ln2, w_group, b_group, w_er, b_er, w_gate, w_up, w_down):
    B, S, D = x.shape
    T = B * S
    x2 = x.reshape(T, D)
    q, kv, xr, gr = _in_proj(x2, ln1, w_in, q_norm, k_norm)
    attn_n = _attention(q.reshape(B, S, ATTN_WIDTH), kv.reshape(B, S, 2 * KV_WIDTH), rel_bias, attn_sink,
                        out_norm_attn)
    lru_n = _rglru(xr.reshape(B, S, LRU_WIDTH), gr.reshape(B, S, LRU_WIDTH), conv_w, conv_b,
                   lru_wa, lru_ba, lru_wi, lru_bi, lru_lambda, out_norm_lru)
    x1, h2, gates, ei, cnt = _out_route(attn_n.reshape(T, ATTN_WIDTH), lru_n.reshape(T, LRU_WIDTH), x2, w_out, ln2,
                                        w_group, b_group, w_er, b_er)
    cap = _moe_cap(T)
    dest, pstart, block_expert, n_used, group_expert = _layout(ei, cnt, cap // MOE_BLOCK)
    xs = _padfill(_sc_dispatch(h2, dest, cap), pstart, cnt)
    yb = _experts(xs, block_expert, n_used, group_expert, w_gate, w_up, w_down)
    out = _combine(x1, gates, _sc_gather(yb, dest))
    return out.reshape(B, S, D)


def kernel(x, rel_bias, ln1, w_in, q_norm, k_norm, attn_sink, conv_w, conv_b, lru_wa, lru_ba, lru_wi, lru_bi,
           lru_lambda, out_norm_attn, out_norm_lru, w_out, ln2, w_group, b_group, w_expert_router, b_expert_router,
           w_gate, w_up, w_down):
    depth = ln1.shape[0]
    for l in range(depth):
        x = _layer(x, rel_bias, ln1[l], w_in[l], q_norm[l], k_norm[l], attn_sink[l], conv_w[l], conv_b[l],
                   lru_wa[l], lru_ba[l], lru_wi[l], lru_bi[l], lru_lambda[l], out_norm_attn[l], out_norm_lru[l],
                   w_out[l], ln2[l], w_group[l], b_group[l], w_expert_router[l], b_expert_router[l],
                   w_gate[l], w_up[l], w_down[l])
    return x
```

```python
import functools
import math

import jax
import jax.numpy as jnp
from jax import lax
from jax.experimental import pallas as pl
from jax.experimental.pallas import tpu as pltpu
from jax.experimental.pallas import tpu_sc as plsc

D_MODEL = 1024
N_HEADS = 8
N_KV_HEADS = 2
HEAD_DIM = 64
Q_PER_KV = N_HEADS // N_KV_HEADS
ATTN_WIDTH = N_HEADS * HEAD_DIM
KV_WIDTH = N_KV_HEADS * HEAD_DIM
WINDOW = 128
BLOCK = 128
NUM_BUCKETS = 32
MAX_DISTANCE = 128
LRU_WIDTH = D_MODEL - ATTN_WIDTH
LRU_BLOCKS = 8
LRU_BLOCK_DIM = LRU_WIDTH // LRU_BLOCKS
LRU_C = 8.0
CONV_W = 4
CONV_LEFT = 2
N_GROUPS = 4
EXPERTS_PER_GROUP = 8
N_EXPERTS = N_GROUPS * EXPERTS_PER_GROUP
TOP_K = 2
D_EXPERT = 512
MOE_BLOCK = 256
EPS = 1e-6
NEG_INF = -1e30

LANES = 128
SUBLANES = 8
VMEM_LIMIT = 56 * 1024 * 1024
LRU_VMEM_LIMIT = 62 * 1024 * 1024

F32 = jnp.float32
BF16 = jnp.bfloat16
LOG2E = math.log2(math.e)


def _cparams(n_axes, vmem=VMEM_LIMIT):
    return pltpu.CompilerParams(dimension_semantics=("arbitrary",) * n_axes, vmem_limit_bytes=vmem)


def _rms(x, gain):
    return x * lax.rsqrt(jnp.mean(x * x, axis=-1, keepdims=True) + EPS) * gain


U32 = jnp.uint32
HI_MASK = 0xFFFF0000
PACKED = D_MODEL // 2


def _pack_rows(x):
    return _pack_rounded(x.astype(BF16).astype(F32))


def _pack_rounded(xb):
    h = xb.shape[1] // 2
    lo = lax.bitcast_convert_type(xb[:, :h], U32) >> 16
    hi = lax.bitcast_convert_type(xb[:, h:], U32) & jnp.uint32(HI_MASK)
    return lo | hi


def _unpack_rows(p):
    lo = lax.bitcast_convert_type(p << 16, F32)
    hi = lax.bitcast_convert_type(p & jnp.uint32(HI_MASK), F32)
    return lo, hi


IN_TM = 1024


def _head_rms(x, n_heads, gain):
    head = lax.broadcasted_iota(jnp.int32, (1, n_heads * HEAD_DIM), 1) // HEAD_DIM
    x2 = x * x
    scale = jnp.zeros_like(x)
    for h in range(n_heads):
        ms = jnp.sum(jnp.where(head == h, x2, 0.0), axis=-1, keepdims=True) * (1.0 / HEAD_DIM)
        scale = jnp.where(head == h, lax.rsqrt(ms + EPS), scale)
    return x * scale * gain


def _in_proj_kernel(x_ref, g_ref, w_ref, qn_ref, kn_ref, q_ref, kv_ref, xr_ref, gr_ref, wb_ref, qg_ref, kg_ref):
    @pl.when(pl.program_id(0) == 0)
    def _():
        wb_ref[...] = w_ref[...].astype(BF16)
        qg_ref[...] = jnp.concatenate([qn_ref[...]] * N_HEADS, axis=1) * (HEAD_DIM ** -0.5 * LOG2E)
        kg_ref[...] = jnp.concatenate([kn_ref[...]] * N_KV_HEADS, axis=1)

    h = _rms(x_ref[...], g_ref[...]).astype(BF16)
    c_k = ATTN_WIDTH
    c_v = c_k + KV_WIDTH
    c_x = c_v + KV_WIDTH
    c_g = c_x + LRU_WIDTH
    q = jnp.dot(h, wb_ref[:, :c_k], preferred_element_type=F32)
    q_ref[...] = _head_rms(q, N_HEADS, qg_ref[...]).astype(BF16)
    k = jnp.dot(h, wb_ref[:, c_k:c_v], preferred_element_type=F32)
    kv_ref[:, :KV_WIDTH] = _head_rms(k, N_KV_HEADS, kg_ref[...]).astype(BF16)
    kv_ref[:, KV_WIDTH:] = jnp.dot(h, wb_ref[:, c_v:c_x], preferred_element_type=F32).astype(BF16)
    xr_ref[...] = jnp.dot(h, wb_ref[:, c_x:c_g], preferred_element_type=F32)
    gr_ref[...] = jnp.dot(h, wb_ref[:, c_g:], preferred_element_type=F32)


def _in_proj(x2, ln1, w_in, q_gain, k_gain):
    T = x2.shape[0]
    n_in = w_in.shape[1]
    row = lambda w: pl.BlockSpec((IN_TM, w), lambda i: (i, 0))
    return pl.pallas_call(
        _in_proj_kernel,
        grid=(T // IN_TM,),
        in_specs=[row(D_MODEL),
                  pl.BlockSpec((1, D_MODEL), lambda i: (0, 0)),
                  pl.BlockSpec((D_MODEL, n_in), lambda i: (0, 0)),
                  pl.BlockSpec((1, HEAD_DIM), lambda i: (0, 0)),
                  pl.BlockSpec((1, HEAD_DIM), lambda i: (0, 0))],
        out_specs=[row(ATTN_WIDTH), row(2 * KV_WIDTH), row(LRU_WIDTH), row(LRU_WIDTH)],
        out_shape=[jax.ShapeDtypeStruct((T, ATTN_WIDTH), BF16),
                   jax.ShapeDtypeStruct((T, 2 * KV_WIDTH), BF16),
                   jax.ShapeDtypeStruct((T, LRU_WIDTH), F32),
                   jax.ShapeDtypeStruct((T, LRU_WIDTH), F32)],
        scratch_shapes=[pltpu.VMEM((D_MODEL, n_in), BF16),
                        pltpu.VMEM((1, ATTN_WIDTH), F32),
                        pltpu.VMEM((1, KV_WIDTH), F32)],
        compiler_params=_cparams(1),
        name="in_proj",
    )(x2, ln1.reshape(1, D_MODEL), w_in, q_gain.reshape(1, HEAD_DIM).astype(F32),
      k_gain.reshape(1, HEAD_DIM).astype(F32))


def _t5_bucket(rel):
    half = NUM_BUCKETS // 2
    max_exact = half // 2
    base = jnp.where(rel > 0, half, 0)
    n = jnp.abs(rel)
    nf = jnp.maximum(n, 1).astype(jnp.float32)
    large = max_exact + (jnp.log(nf / max_exact) / math.log(MAX_DISTANCE / max_exact)
                         * (half - max_exact)).astype(jnp.int32)
    large = jnp.minimum(large, half - 1)
    return base + jnp.where(n < max_exact, n, large)


HEAD_PAIRS = Q_PER_KV // 2
EDGE_VARIANTS = 3


def _fill_bias_table(rb_ref, bucket_ref, band_ref, o_ref):
    bucket = bucket_ref[...]
    band = band_ref[...] > 0
    col = lax.broadcasted_iota(jnp.int32, bucket.shape, 1)
    valid = (band & (col >= BLOCK), band, band & (col < 2 * BLOCK))
    for h in range(N_HEADS):
        acc = jnp.zeros(bucket.shape, F32)
        for b in range(NUM_BUCKETS):
            acc = jnp.where(bucket == b, rb_ref[b, h], acc)
        kv, g = divmod(h, Q_PER_KV)
        pair, parity = divmod(g, 2)
        for var in range(EDGE_VARIANTS):
            o_ref[var, kv, parity, pair * BLOCK:(pair + 1) * BLOCK, :] = jnp.where(valid[var], acc * LOG2E, NEG_INF)


def _attn_kernel(sink_ref, rb_ref, q_ref, kp_ref, kc_ref, kn_ref, bucket_ref, band_ref, og_ref, o_ref, bias_ref):
    n = pl.program_id(1)

    @pl.when((pl.program_id(0) == 0) & (n == 0))
    def _():
        _fill_bias_table(rb_ref, bucket_ref, band_ref, bias_ref)

    kv_all = jnp.concatenate([kp_ref[0], kc_ref[0], kn_ref[0]], axis=0)
    for qb in range(ATTN_QB):
        variant = 1
        if qb == 0:
            variant = jnp.where(n == 0, 0, 1)
        if qb == ATTN_QB - 1:
            variant = jnp.where(n == pl.num_programs(1) - 1, 2, variant)
        out = _attn_block(q_ref[0, qb * BLOCK:(qb + 1) * BLOCK, :], kv_all[qb * BLOCK:(qb + 3) * BLOCK, :],
                          lambda kv, parity: bias_ref[variant, kv, parity], sink_ref)
        o_ref[0, qb * BLOCK:(qb + 1) * BLOCK, :] = _rms(out, og_ref[...]).astype(o_ref.dtype)


def _attn_block(q, kvw, bias, sink_ref):
    low = lax.broadcasted_iota(jnp.int32, (1, LANES), 1) < HEAD_DIM
    swap = lambda slab: pltpu.roll(slab.astype(F32), HEAD_DIM, 1).astype(BF16)
    kslab, vslab = kvw[:, :KV_WIDTH], kvw[:, KV_WIDTH:]
    kslab_sw, vslab_sw = swap(kslab), swap(vslab)
    scores, vzs, sinks = [], [], []
    for kv in range(N_KV_HEADS):
        for parity in range(2):
            ks, vs = (kslab, vslab) if (kv == 0) == (parity == 0) else (kslab_sw, vslab_sw)
            keep = low if parity == 0 else jnp.logical_not(low)
            kz = jnp.where(keep, ks, jnp.zeros_like(ks))
            vz = jnp.where(keep, vs, jnp.zeros_like(vs))
            b = bias(kv, parity)
            for j in range(HEAD_PAIRS):
                lo_lane = kv * Q_PER_KV * HEAD_DIM + j * LANES
                s = lax.dot_general(q[:, lo_lane:lo_lane + LANES], kz, (((1,), (1,)), ((), ())),
                                    preferred_element_type=F32)
                scores.append(s + b[j * BLOCK:(j + 1) * BLOCK, :])
                vzs.append(vz)
                sinks.append(sink_ref[kv * Q_PER_KV + 2 * j + parity])
    probs, inv = [], []
    for s, sink in zip(scores, sinks):
        m = jnp.maximum(jnp.max(s, axis=-1, keepdims=True), sink)
        p = jnp.exp2(s - m)
        inv.append(1.0 / (jnp.sum(p, axis=-1, keepdims=True) + jnp.exp2(sink - m)))
        probs.append(p.astype(BF16))
    outs = [jnp.dot(p, vz, preferred_element_type=F32) * r for p, vz, r in zip(probs, vzs, inv)]
    cols = []
    for kv in range(N_KV_HEADS):
        for j in range(HEAD_PAIRS):
            first = kv * 2 * HEAD_PAIRS + j
            cols.append(outs[first] + outs[first + HEAD_PAIRS])
    return jnp.concatenate(cols, axis=1)


ATTN_QB = 8


def _attention(q, kv, rel_bias, sink, out_gain):
    B, S, _ = q.shape
    nb = S // BLOCK
    assert ATTN_QB >= 2 and nb % ATTN_QB == 0, "a step's first and last query blocks must be distinct"
    ns = nb // ATTN_QB
    rows = ATTN_QB * BLOCK
    qi = jnp.arange(BLOCK, dtype=jnp.int32)
    kj = jnp.arange(3 * BLOCK, dtype=jnp.int32)
    rel = kj[None, :] - BLOCK - qi[:, None]
    bucket = _t5_bucket(rel).astype(jnp.int32)
    band = (jnp.abs(rel) <= WINDOW).astype(jnp.int32)
    kvspec = lambda f: pl.BlockSpec((1, BLOCK, 2 * KV_WIDTH), f)
    smem = pl.BlockSpec(memory_space=pltpu.SMEM)
    geom = pl.BlockSpec((BLOCK, 3 * BLOCK), lambda b, n: (0, 0))
    return pl.pallas_call(
        _attn_kernel,
        grid=(B, ns),
        in_specs=[smem, smem,
                  pl.BlockSpec((1, rows, ATTN_WIDTH), lambda b, n: (b, n, 0)),
                  kvspec(lambda b, n: (b, jnp.maximum(n * ATTN_QB - 1, 0), 0)),
                  pl.BlockSpec((1, rows, 2 * KV_WIDTH), lambda b, n: (b, n, 0)),
                  kvspec(lambda b, n: (b, jnp.minimum((n + 1) * ATTN_QB, nb - 1), 0)),
                  geom, geom,
                  pl.BlockSpec((1, ATTN_WIDTH), lambda b, n: (0, 0))],
        out_specs=pl.BlockSpec((1, rows, ATTN_WIDTH), lambda b, n: (b, n, 0)),
        out_shape=jax.ShapeDtypeStruct((B, S, ATTN_WIDTH), BF16),
        scratch_shapes=[pltpu.VMEM((EDGE_VARIANTS, N_KV_HEADS, 2, HEAD_PAIRS * BLOCK, 3 * BLOCK), F32)],
        compiler_params=_cparams(2),
        name="attention",
    )(sink.astype(F32) * LOG2E, rel_bias.astype(F32), q, kv, kv, kv, bucket, band, out_gain.reshape(1, ATTN_WIDTH))


LRU_TC = 128
LRU_PITCH = LRU_TC + SUBLANES // 2
LRU_SLABS = LRU_WIDTH // LANES
LRU_UNROLL = 32
HALO = SUBLANES


def _softplus(x):
    return jnp.maximum(x, 0.0) + jnp.log(1.0 + jnp.exp(-jnp.abs(x)))


def _gelu_tanh(x):
    k = math.sqrt(2.0 / math.pi)
    hx = 0.5 * x
    return hx + hx * jnp.tanh(x * (k + (k * 0.044715) * (x * x)))


def _sigmoid(x):
    return 0.5 + 0.5 * jnp.tanh(0.5 * x)


def _rglru_kernel(xr_ref, xp_ref, xn_ref, gr_ref, cw_ref, cb_ref, wa_ref, wi_ref, ba_ref, bi_ref, lam_ref, og_ref,
                  o_ref, sx_ref, a_ref, u_ref, h_ref, carry_ref, hf_ref, xcs_ref, wg_ref, bg_ref, k_ref):
    p = pl.program_id(0)
    i = pl.program_id(1)
    nc = pl.num_programs(1)
    c = i + p * (nc - 1 - 2 * i)
    B = xr_ref.shape[0]
    TC = LRU_TC

    @pl.when(i == 0)
    def _():
        carry_ref[...] = jnp.zeros_like(carry_ref)
        wg_ref[...] = jnp.zeros_like(wg_ref)
        for sel, w_ref in enumerate((wa_ref, wi_ref)):
            for h in range(LRU_BLOCKS):
                lo = h * LRU_BLOCK_DIM
                wg_ref[lo:lo + LRU_BLOCK_DIM, sel * LRU_WIDTH + lo:sel * LRU_WIDTH + lo + LRU_BLOCK_DIM] = (
                    0.5 * w_ref[0, h]).astype(BF16)
        row = pl.ds(p, 1)
        bg_ref[:, :LRU_WIDTH] = 0.5 * ba_ref[row, :]
        bg_ref[:, LRU_WIDTH:] = 0.5 * bi_ref[row, :]
        k_ref[...] = (-0.5 * LRU_C * math.log2(math.e)) * _softplus(-lam_ref[row, :])

    def gates_and_scan(xc2, backward):
        g = jnp.dot(xc2.astype(BF16), wg_ref[...], preferred_element_type=F32) + bg_ref[...]
        ta = jnp.tanh(g[:, :LRU_WIDTH])
        ig = 0.5 + 0.5 * jnp.tanh(g[:, LRU_WIDTH:])
        a = jnp.exp2((1.0 + ta) * k_ref[...])
        z = 1.0 - a * a
        u = z * lax.rsqrt(jnp.maximum(z, 1e-30)) * ig * xc2
        for b in range(B):
            for s in range(LRU_SLABS):
                a_ref[s, b * LRU_PITCH:b * LRU_PITCH + TC, :] = a[b * TC:(b + 1) * TC, s * LANES:(s + 1) * LANES]
                u_ref[s, b * LRU_PITCH:b * LRU_PITCH + TC, :] = u[b * TC:(b + 1) * TC, s * LANES:(s + 1) * LANES]

        def trip(i, hs):
            t0 = pl.multiple_of((TC // LRU_UNROLL - 1 - i if backward else i) * LRU_UNROLL, LRU_UNROLL)
            for j in range(LRU_UNROLL):
                t = t0 + (LRU_UNROLL - 1 - j if backward else j)
                out = []
                for s in range(LRU_SLABS):
                    idx = pl.ds(t, B, stride=LRU_PITCH)
                    hn = a_ref[s, idx, :] * hs[s] + u_ref[s, idx, :]
                    h_ref[s, idx, :] = hn
                    out.append(hn)
                hs = tuple(out)
            return hs

        hs = lax.fori_loop(0, TC // LRU_UNROLL, trip, tuple(carry_ref[s] for s in range(LRU_SLABS)))
        for s in range(LRU_SLABS):
            carry_ref[s] = hs[s]

    @pl.when(p == 0)
    def _():
        sx_ref[:, HALO:HALO + TC, :] = xr_ref[...]
        sx_ref[:, 0:HALO, :] = jnp.where(c > 0, xp_ref[...], 0.0)
        sx_ref[:, HALO + TC:, :] = jnp.where(c < nc - 1, xn_ref[...], 0.0)
        xc = cb_ref[...][None]
        for j in range(CONV_W):
            off = HALO + j - CONV_LEFT
            xc = xc + cw_ref[j:j + 1, :][None] * sx_ref[:, off:off + TC, :]
        xc2 = xc.reshape(B * TC, LRU_WIDTH)
        xcs_ref[c] = xc2.astype(xcs_ref.dtype)
        gates_and_scan(xc2, backward=False)
        for b in range(B):
            for s in range(LRU_SLABS):
                hf_ref[c, s, b * TC:(b + 1) * TC, :] = h_ref[s, b * LRU_PITCH:b * LRU_PITCH + TC, :].astype(hf_ref.dtype)

    @pl.when(p == 1)
    def _():
        gates_and_scan(xcs_ref[c].astype(F32), backward=True)
        for b in range(B):
            hsum = jnp.concatenate(
                [h_ref[s, b * LRU_PITCH:b * LRU_PITCH + TC, :] + hf_ref[c, s, b * TC:(b + 1) * TC, :].astype(F32)
                 for s in range(LRU_SLABS)], axis=1)
            y = hsum * _gelu_tanh(gr_ref[b])
            o_ref[b] = _rms(y, og_ref[...]).astype(o_ref.dtype)


def _rglru(xr, gr, conv_w, conv_b, w_a, b_a, w_i, b_i, lam, out_gain):
    B, S, W = xr.shape
    nc = S // LRU_TC
    hb = LRU_TC // HALO
    fwd = lambda p, i: jnp.where(p == 0, i, nc - 1)
    bwd = lambda p, i: nc - 1 - p * i
    full2 = lambda shape: pl.BlockSpec(shape, lambda p, i: (0,) * len(shape))
    wblock = pl.BlockSpec((1, LRU_BLOCKS, LRU_BLOCK_DIM, LRU_BLOCK_DIM), lambda p, i: (p, 0, 0, 0))
    return pl.pallas_call(
        _rglru_kernel,
        grid=(2, nc),
        in_specs=[pl.BlockSpec((B, LRU_TC, W), lambda p, i: (0, fwd(p, i), 0)),
                  pl.BlockSpec((B, HALO, W), lambda p, i: (0, jnp.maximum(fwd(p, i) * hb - 1, 0), 0)),
                  pl.BlockSpec((B, HALO, W), lambda p, i: (0, jnp.minimum((fwd(p, i) + 1) * hb, S // HALO - 1), 0)),
                  pl.BlockSpec((B, LRU_TC, W), lambda p, i: (0, bwd(p, i), 0)),
                  full2((CONV_W, W)),
                  full2((1, W)),
                  wblock, wblock,
                  full2((2, W)), full2((2, W)), full2((2, W)),
                  full2((1, W))],
        out_specs=pl.BlockSpec((B, LRU_TC, W), lambda p, i: (0, bwd(p, i), 0)),
        out_shape=jax.ShapeDtypeStruct((B, S, W), BF16),
        scratch_shapes=[pltpu.VMEM((B, LRU_TC + 2 * HALO, W), F32),
                        pltpu.VMEM((LRU_SLABS, B * LRU_PITCH, LANES), F32),
                        pltpu.VMEM((LRU_SLABS, B * LRU_PITCH, LANES), F32),
                        pltpu.VMEM((LRU_SLABS, B * LRU_PITCH, LANES), F32),
                        pltpu.VMEM((LRU_SLABS, B, LANES), F32),
                        pltpu.VMEM((nc, LRU_SLABS, B * LRU_TC, LANES), BF16),
                        pltpu.VMEM((nc, B * LRU_TC, W), BF16),
                        pltpu.VMEM((W, 2 * W), BF16),
                        pltpu.VMEM((1, 2 * W), F32),
                        pltpu.VMEM((1, W), F32)],
        compiler_params=_cparams(2, LRU_VMEM_LIMIT),
        name="rglru",
    )(xr, xr, xr, gr, conv_w.astype(F32), conv_b.reshape(1, W).astype(F32), w_a.astype(F32), w_i.astype(F32),
      b_a.astype(F32), b_i.astype(F32), lam.astype(F32), out_gain.reshape(1, W).astype(F32))


RT_TM = 1024
RT_PARTS = 4
RT_COLS = LANES
RT_ROWS = 48
RINFO = SUBLANES


def _split_bf16(x):
    hi = x.astype(BF16)
    lo = (x - hi.astype(F32)).astype(BF16)
    return hi, lo


def _route_kernel(an_ref, ln_ref, x_ref, wo_ref, g2_ref, wr_ref, br_ref,
                  x1_ref, h2_ref, gt_ref, ei_ref, cnt_ref, wob_ref, wrb_ref, tri_ref, run_ref, runc_ref):
    @pl.when(pl.program_id(0) == 0)
    def _():
        wob_ref[...] = wo_ref[...].astype(BF16)
        hi, lo = _split_bf16(wr_ref[...])
        wrb_ref[:RT_ROWS, :] = hi
        wrb_ref[RT_ROWS:, :] = lo
        r = lax.broadcasted_iota(jnp.int32, (RT_TM, RT_TM), 0)
        cidx = lax.broadcasted_iota(jnp.int32, (RT_TM, RT_TM), 1)
        tri_ref[...] = (r < cidx).astype(BF16)
        run_ref[...] = jnp.zeros_like(run_ref)
        runc_ref[...] = jnp.zeros_like(runc_ref)

    nt_dims = (((1,), (1,)), ((), ()))
    part = RT_TM // RT_PARTS
    x1s = []
    for r in range(RT_PARTS):
        rows = slice(r * part, (r + 1) * part)
        x1 = (x_ref[rows, :]
              + jnp.dot(an_ref[rows, :], wob_ref[:ATTN_WIDTH, :], preferred_element_type=F32)
              + jnp.dot(ln_ref[rows, :], wob_ref[ATTN_WIDTH:, :], preferred_element_type=F32))
        x1_ref[rows, :] = x1
        x1s.append(x1)
    splits = []
    for r, x1 in enumerate(x1s):
        h2 = _rms(x1, g2_ref[...])
        hi = h2.astype(BF16)
        hi_f = hi.astype(F32)
        h2_ref[r * part:(r + 1) * part, :] = _pack_rounded(hi_f)
        splits.append((hi, (h2 - hi_f).astype(BF16)))
    logits = []
    for hi, lo in splits:
        t1 = lax.dot_general(wrb_ref[...], hi, nt_dims, preferred_element_type=F32)
        t2 = lax.dot_general(wrb_ref[:RT_ROWS, :], lo, nt_dims, preferred_element_type=F32)
        logits.append(t1[:RT_ROWS] + t1[RT_ROWS:] + t2)
    logit = jnp.concatenate(logits, axis=1) + br_ref[...]

    sub = lax.broadcasted_iota(jnp.int32, (SUBLANES, RT_TM), 0)
    first_min = lambda hit: jnp.min(jnp.where(hit, sub, SUBLANES), axis=0, keepdims=True)
    is_g = sub < N_GROUPS
    gl = jnp.where(is_g, logit[:SUBLANES], -jnp.inf)
    gm = jnp.max(gl, axis=0, keepdims=True)
    gidx = first_min(gl == gm)
    g_p = 1.0 / jnp.sum(jnp.where(is_g, jnp.exp(logit[:SUBLANES] - gm), 0.0), axis=0, keepdims=True)
    el = logit[SUBLANES:2 * SUBLANES]
    for g in range(1, N_GROUPS):
        el = jnp.where(gidx == g, logit[(g + 1) * SUBLANES:(g + 2) * SUBLANES], el)
    m1 = jnp.max(el, axis=0, keepdims=True)
    i1 = first_min(el == m1)
    el2 = jnp.where(sub == i1, -jnp.inf, el)
    m2 = jnp.max(el2, axis=0, keepdims=True)
    i2 = first_min(el2 == m2)
    t = jnp.exp(m2 - m1)
    gate1 = g_p / (1.0 + t)
    gate2 = g_p * t / (1.0 + t)
    e1 = gidx * EXPERTS_PER_GROUP + i1
    e2 = gidx * EXPERTS_PER_GROUP + i2

    erow = lax.broadcasted_iota(jnp.int32, (N_EXPERTS, RT_TM), 0)
    oh1 = erow == e1
    oh2 = erow == e2
    oh = (oh1 | oh2).astype(F32)
    ohb = oh.astype(BF16)
    cum = jnp.dot(ohb, tri_ref[...], preferred_element_type=F32) + runc_ref[...]
    rank1 = jnp.sum(jnp.where(oh1, cum, 0.0), axis=0, keepdims=True)
    rank2 = jnp.sum(jnp.where(oh2, cum, 0.0), axis=0, keepdims=True)
    runc_ref[...] = runc_ref[...] + jnp.sum(oh, axis=1, keepdims=True)
    tile_cnt = lax.dot_general(jnp.ones((SUBLANES, RT_TM), BF16), ohb, nt_dims, preferred_element_type=F32)
    run_ref[:, :N_EXPERTS] = run_ref[:, :N_EXPERTS] + tile_cnt[0:1]
    cnt_ref[...] = run_ref[...].astype(jnp.int32)

    rows = [e1, e2, rank1.astype(jnp.int32), rank2.astype(jnp.int32)]
    ei = jnp.zeros((RINFO, RT_TM), jnp.int32)
    for k, v in enumerate(rows):
        ei = jnp.where(sub == k, v, ei)
    ei_ref[0] = ei
    gt_ref[...] = jnp.where(sub == 0, gate1, jnp.where(sub == 1, gate2, 0.0)).T


def _out_route(attn_n, lru_n, x2, w_out, ln2, w_group, b_group, w_er, b_er):
    T = x2.shape[0]
    pad_g = SUBLANES - N_GROUPS
    wr = jnp.concatenate([jnp.pad(w_group.T, ((0, pad_g), (0, 0))),
                          jnp.transpose(w_er, (0, 2, 1)).reshape(N_EXPERTS, D_MODEL)], axis=0)
    wr = jnp.pad(wr, ((0, RT_ROWS - wr.shape[0]), (0, 0))).astype(F32)
    br = jnp.concatenate([jnp.pad(b_group, (0, pad_g)), b_er.reshape(-1)])
    br = jnp.pad(br, (0, RT_ROWS - br.shape[0])).reshape(RT_ROWS, 1).astype(F32)
    row = lambda w: pl.BlockSpec((RT_TM, w), lambda i: (i, 0))
    const = lambda shape: pl.BlockSpec(shape, lambda i: (0, 0))
    return pl.pallas_call(
        _route_kernel,
        grid=(T // RT_TM,),
        in_specs=[row(ATTN_WIDTH), row(LRU_WIDTH), row(D_MODEL), const((D_MODEL, D_MODEL)), const((1, D_MODEL)),
                  const((RT_ROWS, D_MODEL)), const((RT_ROWS, 1))],
        out_specs=[row(D_MODEL), row(PACKED), row(RINFO),
                   pl.BlockSpec((1, RINFO, RT_TM), lambda i: (i, 0, 0)), const((1, RT_COLS))],
        out_shape=[jax.ShapeDtypeStruct((T, D_MODEL), F32),
                   jax.ShapeDtypeStruct((T, PACKED), U32),
                   jax.ShapeDtypeStruct((T, RINFO), F32),
                   jax.ShapeDtypeStruct((T // RT_TM, RINFO, RT_TM), jnp.int32),
                   jax.ShapeDtypeStruct((1, RT_COLS), jnp.int32)],
        scratch_shapes=[pltpu.VMEM((D_MODEL, D_MODEL), BF16),
                        pltpu.VMEM((2 * RT_ROWS, D_MODEL), BF16),
                        pltpu.VMEM((RT_TM, RT_TM), BF16),
                        pltpu.VMEM((1, RT_COLS), F32),
                        pltpu.VMEM((N_EXPERTS, 1), F32)],
        compiler_params=_cparams(1),
        name="out_route",
    )(attn_n, lru_n, x2, w_out, ln2.reshape(1, D_MODEL).astype(F32), wr, br)


def _moe_cap(T):
    A = T * TOP_K
    return ((A + MOE_BLOCK - 1) // MOE_BLOCK) * MOE_BLOCK + N_EXPERTS * MOE_BLOCK


PAD_BITS = tuple(1 << b for b in reversed(range(3, MOE_BLOCK.bit_length() - 1)))


def _layout_kernel(cnt_ref, ei_ref, dest_ref, pstart, be_ref, nu_ref, ge_ref):
    n_blocks = be_ref.shape[0]

    def lay(e, carry):
        start, blk, grp = carry
        pstart[e] = start
        nb = (cnt_ref[0, e] + MOE_BLOCK - 1) // MOE_BLOCK
        ge_ref[grp] = e

        def fill(k, c):
            be_ref[blk + k] = e
            return c
        lax.fori_loop(0, nb, fill, 0)
        return start + nb * MOE_BLOCK, blk + nb, grp + (nb > 0).astype(jnp.int32)
    _, used, groups = lax.fori_loop(0, N_EXPERTS, lay, (jnp.int32(0), jnp.int32(0), jnp.int32(0)))
    nu_ref[0] = used

    def tail(k, c):
        be_ref[k] = N_EXPERTS - 1
        return c
    lax.fori_loop(used, n_blocks, tail, 0)

    def no_group(k, c):
        ge_ref[k] = -1
        return c
    lax.fori_loop(groups, ge_ref.shape[0], no_group, 0)

    expert = ei_ref[:, 0:TOP_K, :]
    dest = ei_ref[:, TOP_K:2 * TOP_K, :]
    for e in range(N_EXPERTS):
        dest = dest + jnp.where(expert == e, pstart[e], 0)
    dest_ref[...] = dest


def _layout(ei, cnt, n_blocks):
    nt = ei.shape[0]
    smem = pl.BlockSpec(memory_space=pltpu.SMEM)
    vmem = pl.BlockSpec(memory_space=pltpu.VMEM)
    return pl.pallas_call(
        _layout_kernel,
        in_specs=[smem, vmem],
        out_specs=[vmem, smem, smem, smem, smem],
        out_shape=[jax.ShapeDtypeStruct((nt, TOP_K, RT_TM), jnp.int32),
                   jax.ShapeDtypeStruct((N_EXPERTS,), jnp.int32),
                   jax.ShapeDtypeStruct((n_blocks,), jnp.int32),
                   jax.ShapeDtypeStruct((1,), jnp.int32),
                   jax.ShapeDtypeStruct((N_EXPERTS + W_AHEAD,), jnp.int32)],
        name="layout",
    )(cnt, ei)


SC_CHUNK = 64
SC_BUFS = 3
SC_LEAD = SC_BUFS - 1


def _sc_workers():
    info = plsc.get_sparse_core_info()
    return info.num_cores, info.num_subcores


def _sc_ring(n_chunks, read, write):
    for c in range(min(SC_LEAD, n_chunks)):
        for cp in read(c):
            cp.start()
    reclaimed = set()
    for c in range(n_chunks):
        for cp in read(c):
            cp.wait()
        for cp in write(c):
            cp.start()
        nxt = c + SC_LEAD
        if nxt < n_chunks:
            if nxt - SC_BUFS >= 0:
                for cp in write(nxt - SC_BUFS):
                    cp.wait()
                reclaimed.add(nxt - SC_BUFS)
            for cp in read(nxt):
                cp.start()
    for c in range(n_chunks):
        if c not in reclaimed:
            for cp in write(c):
                cp.wait()


def _sc_dispatch(h2p, dest, cap):
    T = h2p.shape[0]
    nc, ns = _sc_workers()
    per_w = T // (nc * ns)
    n_ch = per_w // SC_CHUNK
    nt, _, tm = dest.shape
    assert nt * tm == T and tm % per_w == 0 and per_w % SC_CHUNK == 0
    idx = dest.reshape(nt, TOP_K, tm // per_w, per_w).transpose(0, 2, 1, 3).reshape(nc * ns, TOP_K * n_ch, SC_CHUNK)
    mesh = plsc.VectorSubcoreMesh(core_axis_name="c", subcore_axis_name="s")

    @functools.partial(
        pl.kernel, mesh=mesh,
        out_type=jax.ShapeDtypeStruct((cap, PACKED), U32),
        scratch_types=[pltpu.VMEM((TOP_K * n_ch, SC_CHUNK), jnp.int32),
                       pltpu.VMEM((SC_BUFS, SC_CHUNK, PACKED), U32),
                       pltpu.SemaphoreType.DMA((SC_BUFS,)),
                       pltpu.SemaphoreType.DMA((SC_BUFS,))])
    def scatter(src_hbm, idx_hbm, out_hbm, idx_v, rows_v, rsem, wsem):
        wid = lax.axis_index("s") * nc + lax.axis_index("c")
        base = pl.multiple_of(wid * per_w, per_w)
        pltpu.sync_copy(idx_hbm.at[wid], idx_v)

        def read(c):
            b = c % SC_BUFS
            return [pltpu.make_async_copy(src_hbm.at[pl.ds(base + c * SC_CHUNK, SC_CHUNK)], rows_v.at[b], rsem.at[b])]

        def write(c):
            b = c % SC_BUFS
            return [pltpu.make_async_copy(rows_v.at[b], out_hbm.at[idx_v.at[k * n_ch + c]], wsem.at[b])
                    for k in range(TOP_K)]
        _sc_ring(n_ch, read, write)

    return scatter(h2p, idx)


def _sc_gather(yb, dest):
    nt, _, tm = dest.shape
    nc, ns = _sc_workers()
    n_rows = nt * TOP_K * tm
    per_w = n_rows // (nc * ns)
    n_ch = per_w // SC_CHUNK
    assert per_w * nc * ns == n_rows and per_w % SC_CHUNK == 0
    mesh = plsc.VectorSubcoreMesh(core_axis_name="c", subcore_axis_name="s")

    @functools.partial(
        pl.kernel, mesh=mesh,
        out_type=jax.ShapeDtypeStruct((n_rows, PACKED), U32),
        scratch_types=[pltpu.VMEM((per_w,), jnp.int32),
                       pltpu.VMEM((SC_BUFS, SC_CHUNK, PACKED), U32),
                       pltpu.SemaphoreType.DMA((SC_BUFS,)),
                       pltpu.SemaphoreType.DMA((SC_BUFS,))])
    def gather(table_hbm, idx_hbm, out_hbm, idx_v, rows_v, gsem, wsem):
        wid = lax.axis_index("s") * nc + lax.axis_index("c")
        base = pl.multiple_of(wid * per_w, per_w)
        pltpu.sync_copy(idx_hbm.at[pl.ds(base, per_w)], idx_v)

        def read(c):
            b = c % SC_BUFS
            return [pltpu.make_async_copy(table_hbm.at[idx_v.at[pl.ds(c * SC_CHUNK, SC_CHUNK)]], rows_v.at[b], gsem.at[b])]

        def write(c):
            b = c % SC_BUFS
            return [pltpu.make_async_copy(rows_v.at[b], out_hbm.at[pl.ds(base + c * SC_CHUNK, SC_CHUNK)], wsem.at[b])]
        _sc_ring(n_ch, read, write)

    return gather(yb, dest.reshape(n_rows)).reshape(nt, TOP_K, tm, PACKED)


def _padfill_kernel(cnt_ref, pstart, xs_in, xs_ref, zeros, zsem):
    del xs_in

    def pad_copies(fn):
        for e in range(N_EXPERTS):
            cnt = cnt_ref[0, e]
            head = (-cnt) & (SUBLANES - 1)
            rest = ((-cnt) & (MOE_BLOCK - 1)) - head
            off = pstart[e] + cnt
            for k in range(SUBLANES - 1):
                @pl.when(k < head)
                def _(off=off, k=k):
                    fn(pltpu.make_async_copy(zeros.at[pl.ds(0, 1), :], xs_ref.at[pl.ds(off + k, 1), :], zsem))
            off = off + head
            for bit in PAD_BITS:
                @pl.when((rest & bit) != 0)
                def _(off=off, bit=bit):
                    fn(pltpu.make_async_copy(zeros.at[pl.ds(0, bit), :],
                                             xs_ref.at[pl.ds(pl.multiple_of(off, SUBLANES), bit), :], zsem))
                off = off + (rest & bit)

    zeros[...] = jnp.zeros_like(zeros)
    pad_copies(lambda cp: cp.start())
    pad_copies(lambda cp: cp.wait())


def _padfill(xs, pstart, cnt):
    smem = pl.BlockSpec(memory_space=pltpu.SMEM)
    hbm = pl.BlockSpec(memory_space=pl.ANY)
    return pl.pallas_call(
        _padfill_kernel,
        in_specs=[smem, smem, hbm],
        out_specs=hbm,
        out_shape=jax.ShapeDtypeStruct(xs.shape, xs.dtype),
        input_output_aliases={2: 0},
        scratch_shapes=[pltpu.VMEM((MOE_BLOCK // 2, PACKED), U32), pltpu.SemaphoreType.DMA(())],
        name="padfill",
    )(cnt, pstart, xs)


W_SLOTS = 3
W_AHEAD = W_SLOTS - 1
EXPERT_GROUP = 8
EXPERT_RUNS = (1, 2, 4)


def _expert_kernel(be_ref, nu_ref, ge_ref, x_ref, wg_hbm, wu_hbm, wd_hbm, o_ref,
                   wgf, wuf, wdf, grp_ref, sems):
    step = pl.program_id(0)

    def weight_copies(e, slot):
        return (pltpu.make_async_copy(wg_hbm.at[e], wgf.at[slot], sems.at[slot, 0]),
                pltpu.make_async_copy(wu_hbm.at[e], wuf.at[slot], sems.at[slot, 1]),
                pltpu.make_async_copy(wd_hbm.at[e], wdf.at[slot], sems.at[slot, 2]))

    @pl.when(step == 0)
    def _():
        grp_ref[0] = 0
        for a in range(W_AHEAD):
            @pl.when(ge_ref[a] >= 0)
            def _(a=a):
                for cp in weight_copies(ge_ref[a], a):
                    cp.start()

    n_blocks = be_ref.shape[0]
    n_used = nu_ref[0]

    def swiglu(s, n, slot):
        rows = pl.ds(pl.multiple_of(s * MOE_BLOCK, MOE_BLOCK), n * MOE_BLOCK)
        lo, hi = _unpack_rows(x_ref[rows, :])
        lo = lo.astype(BF16)
        hi = hi.astype(BF16)
        g = (jnp.dot(lo, wgf[slot, :PACKED, :], preferred_element_type=F32)
             + jnp.dot(hi, wgf[slot, PACKED:, :], preferred_element_type=F32))
        u = (jnp.dot(lo, wuf[slot, :PACKED, :], preferred_element_type=F32)
             + jnp.dot(hi, wuf[slot, PACKED:, :], preferred_element_type=F32))
        h = (g * _sigmoid(g) * u).astype(BF16)
        o_ref[rows, :] = _pack_rows(jnp.dot(h, wdf[slot], preferred_element_type=F32))

    def run(s):
        j = step * EXPERT_GROUP + s
        e = be_ref[j]
        first = jnp.logical_or(j == 0, e != be_ref[jnp.maximum(j - 1, 0)])

        @pl.when(first)
        def _():
            grp = grp_ref[0]
            slot = grp % W_SLOTS
            for cp in weight_copies(e, slot):
                cp.wait()
            nxt = ge_ref[grp + W_AHEAD]

            @pl.when(nxt >= 0)
            def _():
                for cp in weight_copies(nxt, (grp + W_AHEAD) % W_SLOTS):
                    cp.start()
            grp_ref[0] = grp + 1

        def same(k):
            return (s + k < EXPERT_GROUP) & (j + k < n_used) & (be_ref[jnp.minimum(j + k, n_blocks - 1)] == e)
        take = jnp.int32(1)
        for n in EXPERT_RUNS[1:]:
            ok = same(n - 1)
            for k in range(1, n - 1):
                ok = ok & same(k)
            take = jnp.where(ok, n, take)
        slot = (grp_ref[0] + W_SLOTS - 1) % W_SLOTS
        for n in EXPERT_RUNS:
            @pl.when(take == n)
            def _(n=n):
                swiglu(s, n, slot)
        return s + take

    lax.while_loop(lambda s: (s < EXPERT_GROUP) & (step * EXPERT_GROUP + s < n_used), run, jnp.int32(0))


def _experts(xs, block_expert, n_used, group_expert, w_gate, w_up, w_down):
    cap = xs.shape[0]
    n_blocks = cap // MOE_BLOCK
    assert n_blocks % EXPERT_GROUP == 0
    rows = EXPERT_GROUP * MOE_BLOCK
    last = lambda j, be, nu, ge: jnp.minimum(j, (nu[0] - 1) // EXPERT_GROUP)
    hbm = pl.BlockSpec(memory_space=pl.ANY)
    gs = pltpu.PrefetchScalarGridSpec(
        num_scalar_prefetch=3,
        grid=(n_blocks // EXPERT_GROUP,),
        in_specs=[pl.BlockSpec((rows, PACKED), lambda j, be, nu, ge: (last(j, be, nu, ge), 0)), hbm, hbm, hbm],
        out_specs=pl.BlockSpec((rows, PACKED), lambda j, be, nu, ge: (last(j, be, nu, ge), 0)),
        scratch_shapes=[pltpu.VMEM((W_SLOTS, D_MODEL, D_EXPERT), F32),
                        pltpu.VMEM((W_SLOTS, D_MODEL, D_EXPERT), F32),
                        pltpu.VMEM((W_SLOTS, D_EXPERT, D_MODEL), F32),
                        pltpu.SMEM((1,), jnp.int32),
                        pltpu.SemaphoreType.DMA((W_SLOTS, 3))],
    )
    return pl.pallas_call(
        _expert_kernel,
        grid_spec=gs,
        out_shape=jax.ShapeDtypeStruct((cap, PACKED), U32),
        compiler_params=_cparams(1),
        name="experts",
    )(block_expert, n_used, group_expert, xs, w_gate, w_up, w_down)


CB_TM = RT_TM


def _combine_kernel(x1_ref, gt_ref, y2_ref, o_ref):
    g = gt_ref[...]
    lo1, hi1 = _unpack_rows(y2_ref[0, 0])
    lo2, hi2 = _unpack_rows(y2_ref[0, 1])
    o_ref[:, :PACKED] = x1_ref[:, :PACKED] + g[:, 0:1] * lo1 + g[:, 1:2] * lo2
    o_ref[:, PACKED:] = x1_ref[:, PACKED:] + g[:, 0:1] * hi1 + g[:, 1:2] * hi2


def _combine(x1, gates, y2):
    T = x1.shape[0]
    nt = T // CB_TM
    return pl.pallas_call(
        _combine_kernel,
        grid=(nt,),
        in_specs=[pl.BlockSpec((CB_TM, D_MODEL), lambda i: (i, 0)),
                  pl.BlockSpec((CB_TM, RINFO), lambda i: (i, 0)),
                  pl.BlockSpec((1, TOP_K, CB_TM, PACKED), lambda i: (i, 0, 0, 0))],
        out_specs=pl.BlockSpec((CB_TM, D_MODEL), lambda i: (i, 0)),
        out_shape=jax.ShapeDtypeStruct((T, D_MODEL), F32),
        compiler_params=_cparams(1),
        name="combine",
    )(x1, gates, y2)


def _layer(x, rel_bias, ln1, w_in, q_norm, k_norm, attn_sink, conv_w, conv_b, lru_wa, lru_ba, lru_wi, lru_bi,
           lru_lambda, out_norm_attn, out_norm_lru, w_out, ln2, w_group, b_group, w_er, b_er, w_gate, w_up, w_down):
    B, S, D = x.shape
    T = B * S
    x2 = x.reshape(T, D)
    q, kv, xr, gr = _in_proj(x2, ln1, w_in, q_norm, k_norm)
    attn_n = _attention(q.reshape(B, S, ATTN_WIDTH), kv.reshape(B, S, 2 * KV_WIDTH), rel_bias, attn_sink,
                        out_norm_attn)
    lru_n = _rglru(xr.reshape(B, S, LRU_WIDTH), gr.reshape(B, S, LRU_WIDTH), conv_w, conv_b,
                   lru_wa, lru_ba, lru_wi, lru_bi, lru_lambda, out_norm_lru)
    x1, h2, gates, ei, cnt = _out_route(attn_n.reshape(T, ATTN_WIDTH), lru_n.reshape(T, LRU_WIDTH), x2, w_out, ln2,
                                        w_group, b_group, w_er, b_er)
    cap = _moe_cap(T)
    dest, pstart, block_expert, n_used, group_expert = _layout(ei, cnt, cap // MOE_BLOCK)
    xs = _padfill(_sc_dispatch(h2, dest, cap), pstart, cnt)
    yb = _experts(xs, block_expert, n_used, group_expert, w_gate, w_up, w_down)
    out = _combine(x1, gates, _sc_gather(yb, dest))
    return out.reshape(B, S, D)


def kernel(x, rel_bias, ln1, w_in, q_norm, k_norm, attn_sink, conv_w, conv_b, lru_wa, lru_ba, lru_wi, lru_bi,
           lru_lambda, out_norm_attn, out_norm_lru, w_out, ln2, w_group, b_group, w_expert_router, b_expert_router,
           w_gate, w_up, w_down):
    depth = ln1.shape[0]
    for l in range(depth):
        x = _layer(x, rel_bias, ln1[l], w_in[l], q_norm[l], k_norm[l], attn_sink[l], conv_w[l], conv_b[l],
                   lru_wa[l], lru_ba[l], lru_wi[l], lru_bi[l], lru_lambda[l], out_norm_attn[l], out_norm_lru[l],
                   w_out[l], ln2[l], w_group[l], b_group[l], w_expert_router[l], b_expert_router[l],
                   w_gate[l], w_up[l], w_down[l])
    return x
```

```python
import functools
import math

import jax
import jax.numpy as jnp
from jax import lax
from jax.experimental import pallas as pl
from jax.experimental.pallas import tpu as pltpu
from jax.experimental.pallas import tpu_sc as plsc

D_MODEL = 1024
N_HEADS = 8
N_KV_HEADS = 2
HEAD_DIM = 64
Q_PER_KV = N_HEADS // N_KV_HEADS
ATTN_WIDTH = N_HEADS * HEAD_DIM
KV_WIDTH = N_KV_HEADS * HEAD_DIM
WINDOW = 128
BLOCK = 128
NUM_BUCKETS = 32
MAX_DISTANCE = 128
LRU_WIDTH = D_MODEL - ATTN_WIDTH
LRU_BLOCKS = 8
LRU_BLOCK_DIM = LRU_WIDTH // LRU_BLOCKS
LRU_C = 8.0
CONV_W = 4
CONV_LEFT = 2
N_GROUPS = 4
EXPERTS_PER_GROUP = 8
N_EXPERTS = N_GROUPS * EXPERTS_PER_GROUP
TOP_K = 2
D_EXPERT = 512
MOE_BLOCK = 256
EPS = 1e-6
NEG_INF = -1e30

LANES = 128
SUBLANES = 8
VMEM_LIMIT = 56 * 1024 * 1024
LRU_VMEM_LIMIT = 62 * 1024 * 1024

F32 = jnp.float32
BF16 = jnp.bfloat16
LOG2E = math.log2(math.e)


def _cparams(n_axes, vmem=VMEM_LIMIT):
    return pltpu.CompilerParams(dimension_semantics=("arbitrary",) * n_axes, vmem_limit_bytes=vmem)


def _rms(x, gain):
    return x * lax.rsqrt(jnp.mean(x * x, axis=-1, keepdims=True) + EPS) * gain


U32 = jnp.uint32
HI_MASK = 0xFFFF0000
PACKED = D_MODEL // 2


def _pack_rows(x):
    return _pack_rounded(x.astype(BF16).astype(F32))


def _pack_rounded(xb):
    h = xb.shape[1] // 2
    lo = lax.bitcast_convert_type(xb[:, :h], U32) >> 16
    hi = lax.bitcast_convert_type(xb[:, h:], U32) & jnp.uint32(HI_MASK)
    return lo | hi


def _unpack_rows(p):
    lo = lax.bitcast_convert_type(p << 16, F32)
    hi = lax.bitcast_convert_type(p & jnp.uint32(HI_MASK), F32)
    return lo, hi


IN_TM = 1024


def _head_rms(x, n_heads, gain):
    head = lax.broadcasted_iota(jnp.int32, (1, n_heads * HEAD_DIM), 1) // HEAD_DIM
    x2 = x * x
    scale = jnp.zeros_like(x)
    for h in range(n_heads):
        ms = jnp.sum(jnp.where(head == h, x2, 0.0), axis=-1, keepdims=True) * (1.0 / HEAD_DIM)
        scale = jnp.where(head == h, lax.rsqrt(ms + EPS), scale)
    return x * scale * gain


def _in_proj_kernel(x_ref, g_ref, w_ref, qn_ref, kn_ref, q_ref, kv_ref, xr_ref, gr_ref, wb_ref, qg_ref, kg_ref):
    @pl.when(pl.program_id(0) == 0)
    def _():
        wb_ref[...] = w_ref[...].astype(BF16)
        qg_ref[...] = jnp.concatenate([qn_ref[...]] * N_HEADS, axis=1) * (HEAD_DIM ** -0.5 * LOG2E)
        kg_ref[...] = jnp.concatenate([kn_ref[...]] * N_KV_HEADS, axis=1)

    h = _rms(x_ref[...], g_ref[...]).astype(BF16)
    c_k = ATTN_WIDTH
    c_v = c_k + KV_WIDTH
    c_x = c_v + KV_WIDTH
    c_g = c_x + LRU_WIDTH
    q = jnp.dot(h, wb_ref[:, :c_k], preferred_element_type=F32)
    q_ref[...] = _head_rms(q, N_HEADS, qg_ref[...]).astype(BF16)
    k = jnp.dot(h, wb_ref[:, c_k:c_v], preferred_element_type=F32)
    kv_ref[:, :KV_WIDTH] = _head_rms(k, N_KV_HEADS, kg_ref[...]).astype(BF16)
    kv_ref[:, KV_WIDTH:] = jnp.dot(h, wb_ref[:, c_v:c_x], preferred_element_type=F32).astype(BF16)
    xr_ref[...] = jnp.dot(h, wb_ref[:, c_x:c_g], preferred_element_type=F32)
    gr_ref[...] = jnp.dot(h, wb_ref[:, c_g:], preferred_element_type=F32)


def _in_proj(x2, ln1, w_in, q_gain, k_gain):
    T = x2.shape[0]
    n_in = w_in.shape[1]
    row = lambda w: pl.BlockSpec((IN_TM, w), lambda i: (i, 0))
    return pl.pallas_call(
        _in_proj_kernel,
        grid=(T // IN_TM,),
        in_specs=[row(D_MODEL),
                  pl.BlockSpec((1, D_MODEL), lambda i: (0, 0)),
                  pl.BlockSpec((D_MODEL, n_in), lambda i: (0, 0)),
                  pl.BlockSpec((1, HEAD_DIM), lambda i: (0, 0)),
                  pl.BlockSpec((1, HEAD_DIM), lambda i: (0, 0))],
        out_specs=[row(ATTN_WIDTH), row(2 * KV_WIDTH), row(LRU_WIDTH), row(LRU_WIDTH)],
        out_shape=[jax.ShapeDtypeStruct((T, ATTN_WIDTH), BF16),
                   jax.ShapeDtypeStruct((T, 2 * KV_WIDTH), BF16),
                   jax.ShapeDtypeStruct((T, LRU_WIDTH), F32),
                   jax.ShapeDtypeStruct((T, LRU_WIDTH), F32)],
        scratch_shapes=[pltpu.VMEM((D_MODEL, n_in), BF16),
                        pltpu.VMEM((1, ATTN_WIDTH), F32),
                        pltpu.VMEM((1, KV_WIDTH), F32)],
        compiler_params=_cparams(1),
        name="in_proj",
    )(x2, ln1.reshape(1, D_MODEL), w_in, q_gain.reshape(1, HEAD_DIM).astype(F32),
      k_gain.reshape(1, HEAD_DIM).astype(F32))


def _t5_bucket(rel):
    half = NUM_BUCKETS // 2
    max_exact = half // 2
    base = jnp.where(rel > 0, half, 0)
    n = jnp.abs(rel)
    nf = jnp.maximum(n, 1).astype(jnp.float32)
    large = max_exact + (jnp.log(nf / max_exact) / math.log(MAX_DISTANCE / max_exact)
                         * (half - max_exact)).astype(jnp.int32)
    large = jnp.minimum(large, half - 1)
    return base + jnp.where(n < max_exact, n, large)


HEAD_PAIRS = Q_PER_KV // 2
EDGE_VARIANTS = 3


def _fill_bias_table(rb_ref, bucket_ref, band_ref, o_ref):
    bucket = bucket_ref[...]
    band = band_ref[...] > 0
    col = lax.broadcasted_iota(jnp.int32, bucket.shape, 1)
    valid = (band & (col >= BLOCK), band, band & (col < 2 * BLOCK))
    for h in range(N_HEADS):
        acc = jnp.zeros(bucket.shape, F32)
        for b in range(NUM_BUCKETS):
            acc = jnp.where(bucket == b, rb_ref[b, h], acc)
        kv, g = divmod(h, Q_PER_KV)
        pair, parity = divmod(g, 2)
        for var in range(EDGE_VARIANTS):
            o_ref[var, kv, parity, pair * BLOCK:(pair + 1) * BLOCK, :] = jnp.where(valid[var], acc * LOG2E, NEG_INF)


def _attn_kernel(sink_ref, rb_ref, q_ref, kp_ref, kc_ref, kn_ref, bucket_ref, band_ref, og_ref, o_ref, bias_ref):
    n = pl.program_id(1)

    @pl.when((pl.program_id(0) == 0) & (n == 0))
    def _():
        _fill_bias_table(rb_ref, bucket_ref, band_ref, bias_ref)

    kv_all = jnp.concatenate([kp_ref[0], kc_ref[0], kn_ref[0]], axis=0)
    for qb in range(ATTN_QB):
        variant = 1
        if qb == 0:
            variant = jnp.where(n == 0, 0, 1)
        if qb == ATTN_QB - 1:
            variant = jnp.where(n == pl.num_programs(1) - 1, 2, variant)
        out = _attn_block(q_ref[0, qb * BLOCK:(qb + 1) * BLOCK, :], kv_all[qb * BLOCK:(qb + 3) * BLOCK, :],
                          lambda kv, parity: bias_ref[variant, kv, parity], sink_ref)
        o_ref[0, qb * BLOCK:(qb + 1) * BLOCK, :] = _rms(out, og_ref[...]).astype(o_ref.dtype)


def _attn_block(q, kvw, bias, sink_ref):
    low = lax.broadcasted_iota(jnp.int32, (1, LANES), 1) < HEAD_DIM
    swap = lambda slab: pltpu.roll(slab.astype(F32), HEAD_DIM, 1).astype(BF16)
    kslab, vslab = kvw[:, :KV_WIDTH], kvw[:, KV_WIDTH:]
    kslab_sw, vslab_sw = swap(kslab), swap(vslab)
    rowi = lax.broadcasted_iota(jnp.int32, (HEAD_PAIRS * BLOCK, 1), 0)
    combos = [(kv, parity) for kv in range(N_KV_HEADS) for parity in range(2)]
    scores, vzs, sinks = [], [], []
    for kv, parity in combos:
        ks, vs = (kslab, vslab) if (kv == 0) == (parity == 0) else (kslab_sw, vslab_sw)
        keep = low if parity == 0 else jnp.logical_not(low)
        kz = jnp.where(keep, ks, jnp.zeros_like(ks))
        vzs.append(jnp.where(keep, vs, jnp.zeros_like(vs)))
        base = kv * Q_PER_KV * HEAD_DIM
        qpair = jnp.concatenate([q[:, base + j * LANES:base + (j + 1) * LANES] for j in range(HEAD_PAIRS)], axis=0)
        s = lax.dot_general(qpair, kz, (((1,), (1,)), ((), ())), preferred_element_type=F32)
        scores.append(s + bias(kv, parity))
        sink = jnp.zeros((HEAD_PAIRS * BLOCK, 1), F32)
        for j in range(HEAD_PAIRS):
            sink = jnp.where(rowi // BLOCK == j, sink_ref[kv * Q_PER_KV + 2 * j + parity], sink)
        sinks.append(sink)
    probs, inv = [], []
    for s, sink in zip(scores, sinks):
        m = jnp.maximum(jnp.max(s, axis=-1, keepdims=True), sink)
        p = jnp.exp2(s - m)
        inv.append(1.0 / (jnp.sum(p, axis=-1, keepdims=True) + jnp.exp2(sink - m)))
        probs.append(p.astype(BF16))
    outs = [jnp.dot(p, vz, preferred_element_type=F32) * r for p, vz, r in zip(probs, vzs, inv)]
    cols = []
    for kv in range(N_KV_HEADS):
        acc = outs[2 * kv] + outs[2 * kv + 1]
        cols += [acc[j * BLOCK:(j + 1) * BLOCK, :] for j in range(HEAD_PAIRS)]
    return jnp.concatenate(cols, axis=1)


ATTN_QB = 8


def _attention(q, kv, rel_bias, sink, out_gain):
    B, S, _ = q.shape
    nb = S // BLOCK
    assert ATTN_QB >= 2 and nb % ATTN_QB == 0, "a step's first and last query blocks must be distinct"
    ns = nb // ATTN_QB
    rows = ATTN_QB * BLOCK
    qi = jnp.arange(BLOCK, dtype=jnp.int32)
    kj = jnp.arange(3 * BLOCK, dtype=jnp.int32)
    rel = kj[None, :] - BLOCK - qi[:, None]
    bucket = _t5_bucket(rel).astype(jnp.int32)
    band = (jnp.abs(rel) <= WINDOW).astype(jnp.int32)
    kvspec = lambda f: pl.BlockSpec((1, BLOCK, 2 * KV_WIDTH), f)
    smem = pl.BlockSpec(memory_space=pltpu.SMEM)
    geom = pl.BlockSpec((BLOCK, 3 * BLOCK), lambda b, n: (0, 0))
    return pl.pallas_call(
        _attn_kernel,
        grid=(B, ns),
        in_specs=[smem, smem,
                  pl.BlockSpec((1, rows, ATTN_WIDTH), lambda b, n: (b, n, 0)),
                  kvspec(lambda b, n: (b, jnp.maximum(n * ATTN_QB - 1, 0), 0)),
                  pl.BlockSpec((1, rows, 2 * KV_WIDTH), lambda b, n: (b, n, 0)),
                  kvspec(lambda b, n: (b, jnp.minimum((n + 1) * ATTN_QB, nb - 1), 0)),
                  geom, geom,
                  pl.BlockSpec((1, ATTN_WIDTH), lambda b, n: (0, 0))],
        out_specs=pl.BlockSpec((1, rows, ATTN_WIDTH), lambda b, n: (b, n, 0)),
        out_shape=jax.ShapeDtypeStruct((B, S, ATTN_WIDTH), BF16),
        scratch_shapes=[pltpu.VMEM((EDGE_VARIANTS, N_KV_HEADS, 2, HEAD_PAIRS * BLOCK, 3 * BLOCK), F32)],
        compiler_params=_cparams(2),
        name="attention",
    )(sink.astype(F32) * LOG2E, rel_bias.astype(F32), q, kv, kv, kv, bucket, band, out_gain.reshape(1, ATTN_WIDTH))


LRU_TC = 128
LRU_PITCH = LRU_TC + SUBLANES // 2
LRU_SLABS = LRU_WIDTH // LANES
LRU_UNROLL = 32
HALO = SUBLANES


def _softplus(x):
    return jnp.maximum(x, 0.0) + jnp.log(1.0 + jnp.exp(-jnp.abs(x)))


def _gelu_tanh(x):
    k = math.sqrt(2.0 / math.pi)
    hx = 0.5 * x
    return hx + hx * jnp.tanh(x * (k + (k * 0.044715) * (x * x)))


def _sigmoid(x):
    return 0.5 + 0.5 * jnp.tanh(0.5 * x)


def _rglru_kernel(xr_ref, xp_ref, xn_ref, gr_ref, cw_ref, cb_ref, wa_ref, wi_ref, ba_ref, bi_ref, lam_ref,
                  o_ref, sx_ref, a_ref, u_ref, h_ref, carry_ref, hf_ref, xcs_ref, wg_ref, bg_ref, k_ref):
    p = pl.program_id(0)
    i = pl.program_id(1)
    nc = pl.num_programs(1)
    c = i + p * (nc - 1 - 2 * i)
    B = xr_ref.shape[0]
    TC = LRU_TC

    @pl.when(i == 0)
    def _():
        carry_ref[...] = jnp.zeros_like(carry_ref)
        wg_ref[...] = jnp.zeros_like(wg_ref)
        for sel, w_ref in enumerate((wa_ref, wi_ref)):
            for h in range(LRU_BLOCKS):
                lo = h * LRU_BLOCK_DIM
                wg_ref[lo:lo + LRU_BLOCK_DIM, sel * LRU_WIDTH + lo:sel * LRU_WIDTH + lo + LRU_BLOCK_DIM] = (
                    0.5 * w_ref[0, h]).astype(BF16)
        row = pl.ds(p, 1)
        bg_ref[:, :LRU_WIDTH] = 0.5 * ba_ref[row, :]
        bg_ref[:, LRU_WIDTH:] = 0.5 * bi_ref[row, :]
        k_ref[...] = (-0.5 * LRU_C * math.log2(math.e)) * _softplus(-lam_ref[row, :])

    def gates_and_scan(xc2, backward):
        g = jnp.dot(xc2.astype(BF16), wg_ref[...], preferred_element_type=F32) + bg_ref[...]
        ta = jnp.tanh(g[:, :LRU_WIDTH])
        ig = 0.5 + 0.5 * jnp.tanh(g[:, LRU_WIDTH:])
        a = jnp.exp2((1.0 + ta) * k_ref[...])
        z = 1.0 - a * a
        u = z * lax.rsqrt(jnp.maximum(z, 1e-30)) * ig * xc2
        for b in range(B):
            for s in range(LRU_SLABS):
                a_ref[s, b * LRU_PITCH:b * LRU_PITCH + TC, :] = a[b * TC:(b + 1) * TC, s * LANES:(s + 1) * LANES]
                u_ref[s, b * LRU_PITCH:b * LRU_PITCH + TC, :] = u[b * TC:(b + 1) * TC, s * LANES:(s + 1) * LANES]

        def trip(i, hs):
            t0 = pl.multiple_of((TC // LRU_UNROLL - 1 - i if backward else i) * LRU_UNROLL, LRU_UNROLL)
            for j in range(LRU_UNROLL):
                t = t0 + (LRU_UNROLL - 1 - j if backward else j)
                out = []
                for s in range(LRU_SLABS):
                    idx = pl.ds(t, B, stride=LRU_PITCH)
                    hn = a_ref[s, idx, :] * hs[s] + u_ref[s, idx, :]
                    h_ref[s, idx, :] = hn
                    out.append(hn)
                hs = tuple(out)
            return hs

        hs = lax.fori_loop(0, TC // LRU_UNROLL, trip, tuple(carry_ref[s] for s in range(LRU_SLABS)))
        for s in range(LRU_SLABS):
            carry_ref[s] = hs[s]

    @pl.when(p == 0)
    def _():
        sx_ref[:, HALO:HALO + TC, :] = xr_ref[...]
        sx_ref[:, 0:HALO, :] = jnp.where(c > 0, xp_ref[...], 0.0)
        sx_ref[:, HALO + TC:, :] = jnp.where(c < nc - 1, xn_ref[...], 0.0)
        xc = cb_ref[...][None]
        for j in range(CONV_W):
            off = HALO + j - CONV_LEFT
            xc = xc + cw_ref[j:j + 1, :][None] * sx_ref[:, off:off + TC, :]
        xc2 = xc.reshape(B * TC, LRU_WIDTH)
        xcs_ref[c] = xc2.astype(xcs_ref.dtype)
        gates_and_scan(xc2, backward=False)
        for b in range(B):
            for s in range(LRU_SLABS):
                hf_ref[c, s, b * TC:(b + 1) * TC, :] = h_ref[s, b * LRU_PITCH:b * LRU_PITCH + TC, :].astype(hf_ref.dtype)

    @pl.when(p == 1)
    def _():
        gates_and_scan(xcs_ref[c].astype(F32), backward=True)
        for b in range(B):
            hsum = jnp.concatenate(
                [h_ref[s, b * LRU_PITCH:b * LRU_PITCH + TC, :] + hf_ref[c, s, b * TC:(b + 1) * TC, :].astype(F32)
                 for s in range(LRU_SLABS)], axis=1)
            y = hsum * _gelu_tanh(gr_ref[b])
            o_ref[b] = y.astype(o_ref.dtype)


def _rglru(xr, gr, conv_w, conv_b, w_a, b_a, w_i, b_i, lam):
    B, S, W = xr.shape
    nc = S // LRU_TC
    hb = LRU_TC // HALO
    fwd = lambda p, i: jnp.where(p == 0, i, nc - 1)
    bwd = lambda p, i: nc - 1 - p * i
    full2 = lambda shape: pl.BlockSpec(shape, lambda p, i: (0,) * len(shape))
    wblock = pl.BlockSpec((1, LRU_BLOCKS, LRU_BLOCK_DIM, LRU_BLOCK_DIM), lambda p, i: (p, 0, 0, 0))
    return pl.pallas_call(
        _rglru_kernel,
        grid=(2, nc),
        in_specs=[pl.BlockSpec((B, LRU_TC, W), lambda p, i: (0, fwd(p, i), 0)),
                  pl.BlockSpec((B, HALO, W), lambda p, i: (0, jnp.maximum(fwd(p, i) * hb - 1, 0), 0)),
                  pl.BlockSpec((B, HALO, W), lambda p, i: (0, jnp.minimum((fwd(p, i) + 1) * hb, S // HALO - 1), 0)),
                  pl.BlockSpec((B, LRU_TC, W), lambda p, i: (0, bwd(p, i), 0)),
                  full2((CONV_W, W)),
                  full2((1, W)),
                  wblock, wblock,
                  full2((2, W)), full2((2, W)), full2((2, W))],
        out_specs=pl.BlockSpec((B, LRU_TC, W), lambda p, i: (0, bwd(p, i), 0)),
        out_shape=jax.ShapeDtypeStruct((B, S, W), BF16),
        scratch_shapes=[pltpu.VMEM((B, LRU_TC + 2 * HALO, W), F32),
                        pltpu.VMEM((LRU_SLABS, B * LRU_PITCH, LANES), F32),
                        pltpu.VMEM((LRU_SLABS, B * LRU_PITCH, LANES), F32),
                        pltpu.VMEM((LRU_SLABS, B * LRU_PITCH, LANES), F32),
                        pltpu.VMEM((LRU_SLABS, B, LANES), F32),
                        pltpu.VMEM((nc, LRU_SLABS, B * LRU_TC, LANES), BF16),
                        pltpu.VMEM((nc, B * LRU_TC, W), BF16),
                        pltpu.VMEM((W, 2 * W), BF16),
                        pltpu.VMEM((1, 2 * W), F32),
                        pltpu.VMEM((1, W), F32)],
        compiler_params=_cparams(2, LRU_VMEM_LIMIT),
        name="rglru",
    )(xr, xr, xr, gr, conv_w.astype(F32), conv_b.reshape(1, W).astype(F32), w_a.astype(F32), w_i.astype(F32),
      b_a.astype(F32), b_i.astype(F32), lam.astype(F32))


RT_TM = 1024
RT_PARTS = 4
RT_COLS = LANES
RT_ROWS = 48
RINFO = SUBLANES


def _split_bf16(x):
    hi = x.astype(BF16)
    lo = (x - hi.astype(F32)).astype(BF16)
    return hi, lo


def _route_kernel(an_ref, ln_ref, x_ref, wo_ref, g2_ref, gl_ref, wr_ref, br_ref,
                  x1_ref, h2_ref, gt_ref, ei_ref, cnt_ref, wob_ref, wrb_ref, tri_ref, run_ref, runc_ref):
    @pl.when(pl.program_id(0) == 0)
    def _():
        wob_ref[:ATTN_WIDTH, :] = wo_ref[:ATTN_WIDTH, :].astype(BF16)
        wob_ref[ATTN_WIDTH:, :] = (wo_ref[ATTN_WIDTH:, :] * gl_ref[...]).astype(BF16)
        hi, lo = _split_bf16(wr_ref[...])
        wrb_ref[:RT_ROWS, :] = hi
        wrb_ref[RT_ROWS:, :] = lo
        r = lax.broadcasted_iota(jnp.int32, (RT_TM, RT_TM), 0)
        cidx = lax.broadcasted_iota(jnp.int32, (RT_TM, RT_TM), 1)
        tri_ref[...] = (r < cidx).astype(BF16)
        run_ref[...] = jnp.zeros_like(run_ref)
        runc_ref[...] = jnp.zeros_like(runc_ref)

    nt_dims = (((1,), (1,)), ((), ()))
    part = RT_TM // RT_PARTS
    x1s = []
    for r in range(RT_PARTS):
        rows = slice(r * part, (r + 1) * part)
        lru = ln_ref[rows, :]
        lru_f = lru.astype(F32)
        scale = lax.rsqrt(jnp.mean(lru_f * lru_f, axis=-1, keepdims=True) + EPS)
        x1 = (x_ref[rows, :]
              + jnp.dot(an_ref[rows, :], wob_ref[:ATTN_WIDTH, :], preferred_element_type=F32)
              + scale * jnp.dot(lru, wob_ref[ATTN_WIDTH:, :], preferred_element_type=F32))
        x1_ref[rows, :] = x1
        x1s.append(x1)
    splits = []
    for r, x1 in enumerate(x1s):
        h2 = _rms(x1, g2_ref[...])
        hi = h2.astype(BF16)
        hi_f = hi.astype(F32)
        h2_ref[r * part:(r + 1) * part, :] = _pack_rounded(hi_f)
        splits.append((hi, (h2 - hi_f).astype(BF16)))
    logits = []
    for hi, lo in splits:
        t1 = lax.dot_general(wrb_ref[...], hi, nt_dims, preferred_element_type=F32)
        t2 = lax.dot_general(wrb_ref[:RT_ROWS, :], lo, nt_dims, preferred_element_type=F32)
        logits.append(t1[:RT_ROWS] + t1[RT_ROWS:] + t2)
    logit = jnp.concatenate(logits, axis=1) + br_ref[...]

    sub = lax.broadcasted_iota(jnp.int32, (SUBLANES, RT_TM), 0)
    first_min = lambda hit: jnp.min(jnp.where(hit, sub, SUBLANES), axis=0, keepdims=True)
    is_g = sub < N_GROUPS
    gl = jnp.where(is_g, logit[:SUBLANES], -jnp.inf)
    gm = jnp.max(gl, axis=0, keepdims=True)
    gidx = first_min(gl == gm)
    g_p = 1.0 / jnp.sum(jnp.where(is_g, jnp.exp(logit[:SUBLANES] - gm), 0.0), axis=0, keepdims=True)
    el = logit[SUBLANES:2 * SUBLANES]
    for g in range(1, N_GROUPS):
        el = jnp.where(gidx == g, logit[(g + 1) * SUBLANES:(g + 2) * SUBLANES], el)
    m1 = jnp.max(el, axis=0, keepdims=True)
    i1 = first_min(el == m1)
    el2 = jnp.where(sub == i1, -jnp.inf, el)
    m2 = jnp.max(el2, axis=0, keepdims=True)
    i2 = first_min(el2 == m2)
    t = jnp.exp(m2 - m1)
    gate1 = g_p / (1.0 + t)
    gate2 = g_p * t / (1.0 + t)
    e1 = gidx * EXPERTS_PER_GROUP + i1
    e2 = gidx * EXPERTS_PER_GROUP + i2

    erow = lax.broadcasted_iota(jnp.int32, (N_EXPERTS, RT_TM), 0)
    oh1 = erow == e1
    oh2 = erow == e2
    oh = (oh1 | oh2).astype(F32)
    ohb = oh.astype(BF16)
    cum = jnp.dot(ohb, tri_ref[...], preferred_element_type=F32) + runc_ref[...]
    rank1 = jnp.sum(jnp.where(oh1, cum, 0.0), axis=0, keepdims=True)
    rank2 = jnp.sum(jnp.where(oh2, cum, 0.0), axis=0, keepdims=True)
    runc_ref[...] = runc_ref[...] + jnp.sum(oh, axis=1, keepdims=True)
    tile_cnt = lax.dot_general(jnp.ones((SUBLANES, RT_TM), BF16), ohb, nt_dims, preferred_element_type=F32)
    run_ref[:, :N_EXPERTS] = run_ref[:, :N_EXPERTS] + tile_cnt[0:1]
    cnt_ref[...] = run_ref[...].astype(jnp.int32)

    rows = [e1, e2, rank1.astype(jnp.int32), rank2.astype(jnp.int32)]
    ei = jnp.zeros((RINFO, RT_TM), jnp.int32)
    for k, v in enumerate(rows):
        ei = jnp.where(sub == k, v, ei)
    ei_ref[0] = ei
    gt_ref[...] = jnp.where(sub == 0, gate1, jnp.where(sub == 1, gate2, 0.0)).T


def _out_route(attn_n, lru_n, x2, w_out, ln2, lru_gain, w_group, b_group, w_er, b_er):
    T = x2.shape[0]
    pad_g = SUBLANES - N_GROUPS
    wr = jnp.concatenate([jnp.pad(w_group.T, ((0, pad_g), (0, 0))),
                          jnp.transpose(w_er, (0, 2, 1)).reshape(N_EXPERTS, D_MODEL)], axis=0)
    wr = jnp.pad(wr, ((0, RT_ROWS - wr.shape[0]), (0, 0))).astype(F32)
    br = jnp.concatenate([jnp.pad(b_group, (0, pad_g)), b_er.reshape(-1)])
    br = jnp.pad(br, (0, RT_ROWS - br.shape[0])).reshape(RT_ROWS, 1).astype(F32)
    row = lambda w: pl.BlockSpec((RT_TM, w), lambda i: (i, 0))
    const = lambda shape: pl.BlockSpec(shape, lambda i: (0, 0))
    return pl.pallas_call(
        _route_kernel,
        grid=(T // RT_TM,),
        in_specs=[row(ATTN_WIDTH), row(LRU_WIDTH), row(D_MODEL), const((D_MODEL, D_MODEL)), const((1, D_MODEL)),
                  const((LRU_WIDTH, 1)), const((RT_ROWS, D_MODEL)), const((RT_ROWS, 1))],
        out_specs=[row(D_MODEL), row(PACKED), row(RINFO),
                   pl.BlockSpec((1, RINFO, RT_TM), lambda i: (i, 0, 0)), const((1, RT_COLS))],
        out_shape=[jax.ShapeDtypeStruct((T, D_MODEL), F32),
                   jax.ShapeDtypeStruct((T, PACKED), U32),
                   jax.ShapeDtypeStruct((T, RINFO), F32),
                   jax.ShapeDtypeStruct((T // RT_TM, RINFO, RT_TM), jnp.int32),
                   jax.ShapeDtypeStruct((1, RT_COLS), jnp.int32)],
        scratch_shapes=[pltpu.VMEM((D_MODEL, D_MODEL), BF16),
                        pltpu.VMEM((2 * RT_ROWS, D_MODEL), BF16),
                        pltpu.VMEM((RT_TM, RT_TM), BF16),
                        pltpu.VMEM((1, RT_COLS), F32),
                        pltpu.VMEM((N_EXPERTS, 1), F32)],
        compiler_params=_cparams(1),
        name="out_route",
    )(attn_n, lru_n, x2, w_out, ln2.reshape(1, D_MODEL).astype(F32), lru_gain.reshape(LRU_WIDTH, 1).astype(F32),
      wr, br)


def _moe_cap(T):
    A = T * TOP_K
    return ((A + MOE_BLOCK - 1) // MOE_BLOCK) * MOE_BLOCK + N_EXPERTS * MOE_BLOCK


PAD_BITS = tuple(1 << b for b in reversed(range(3, MOE_BLOCK.bit_length() - 1)))


def _layout_kernel(cnt_ref, ei_ref, dest_ref, pstart, be_ref, nu_ref, ge_ref):
    n_blocks = be_ref.shape[0]

    def lay(e, carry):
        start, blk, grp = carry
        pstart[e] = start
        nb = (cnt_ref[0, e] + MOE_BLOCK - 1) // MOE_BLOCK
        ge_ref[grp] = e

        def fill(k, c):
            be_ref[blk + k] = e
            return c
        lax.fori_loop(0, nb, fill, 0)
        return start + nb * MOE_BLOCK, blk + nb, grp + (nb > 0).astype(jnp.int32)
    _, used, groups = lax.fori_loop(0, N_EXPERTS, lay, (jnp.int32(0), jnp.int32(0), jnp.int32(0)))
    nu_ref[0] = used

    def tail(k, c):
        be_ref[k] = N_EXPERTS - 1
        return c
    lax.fori_loop(used, n_blocks, tail, 0)

    def no_group(k, c):
        ge_ref[k] = -1
        return c
    lax.fori_loop(groups, ge_ref.shape[0], no_group, 0)

    expert = ei_ref[:, 0:TOP_K, :]
    dest = ei_ref[:, TOP_K:2 * TOP_K, :]
    for e in range(N_EXPERTS):
        dest = dest + jnp.where(expert == e, pstart[e], 0)
    dest_ref[...] = dest


def _layout(ei, cnt, n_blocks):
    nt = ei.shape[0]
    smem = pl.BlockSpec(memory_space=pltpu.SMEM)
    vmem = pl.BlockSpec(memory_space=pltpu.VMEM)
    return pl.pallas_call(
        _layout_kernel,
        in_specs=[smem, vmem],
        out_specs=[vmem, smem, smem, smem, smem],
        out_shape=[jax.ShapeDtypeStruct((nt, TOP_K, RT_TM), jnp.int32),
                   jax.ShapeDtypeStruct((N_EXPERTS,), jnp.int32),
                   jax.ShapeDtypeStruct((n_blocks,), jnp.int32),
                   jax.ShapeDtypeStruct((1,), jnp.int32),
                   jax.ShapeDtypeStruct((N_EXPERTS + W_AHEAD,), jnp.int32)],
        name="layout",
    )(cnt, ei)


SC_CHUNK = 64
SC_BUFS = 3
SC_LEAD = SC_BUFS - 1


def _sc_workers():
    info = plsc.get_sparse_core_info()
    return info.num_cores, info.num_subcores


def _sc_ring(n_chunks, read, write):
    for c in range(min(SC_LEAD, n_chunks)):
        for cp in read(c):
            cp.start()
    reclaimed = set()
    for c in range(n_chunks):
        for cp in read(c):
            cp.wait()
        for cp in write(c):
            cp.start()
        nxt = c + SC_LEAD
        if nxt < n_chunks:
            if nxt - SC_BUFS >= 0:
                for cp in write(nxt - SC_BUFS):
                    cp.wait()
                reclaimed.add(nxt - SC_BUFS)
            for cp in read(nxt):
                cp.start()
    for c in range(n_chunks):
        if c not in reclaimed:
            for cp in write(c):
                cp.wait()


def _sc_dispatch(h2p, dest, cap):
    T = h2p.shape[0]
    nc, ns = _sc_workers()
    per_w = T // (nc * ns)
    n_ch = per_w // SC_CHUNK
    nt, _, tm = dest.shape
    assert nt * tm == T and tm % per_w == 0 and per_w % SC_CHUNK == 0
    idx = dest.reshape(nt, TOP_K, tm // per_w, per_w).transpose(0, 2, 1, 3).reshape(nc * ns, TOP_K * n_ch, SC_CHUNK)
    mesh = plsc.VectorSubcoreMesh(core_axis_name="c", subcore_axis_name="s")

    @functools.partial(
        pl.kernel, mesh=mesh,
        out_type=jax.ShapeDtypeStruct((cap, PACKED), U32),
        scratch_types=[pltpu.VMEM((TOP_K * n_ch, SC_CHUNK), jnp.int32),
                       pltpu.VMEM((SC_BUFS, SC_CHUNK, PACKED), U32),
                       pltpu.SemaphoreType.DMA((SC_BUFS,)),
                       pltpu.SemaphoreType.DMA((SC_BUFS,))])
    def scatter(src_hbm, idx_hbm, out_hbm, idx_v, rows_v, rsem, wsem):
        wid = lax.axis_index("s") * nc + lax.axis_index("c")
        base = pl.multiple_of(wid * per_w, per_w)
        pltpu.sync_copy(idx_hbm.at[wid], idx_v)

        def read(c):
            b = c % SC_BUFS
            return [pltpu.make_async_copy(src_hbm.at[pl.ds(base + c * SC_CHUNK, SC_CHUNK)], rows_v.at[b], rsem.at[b])]

        def write(c):
            b = c % SC_BUFS
            return [pltpu.make_async_copy(rows_v.at[b], out_hbm.at[idx_v.at[k * n_ch + c]], wsem.at[b])
                    for k in range(TOP_K)]
        _sc_ring(n_ch, read, write)

    return scatter(h2p, idx)


def _sc_gather(yb, dest):
    nt, _, tm = dest.shape
    nc, ns = _sc_workers()
    n_rows = nt * TOP_K * tm
    per_w = n_rows // (nc * ns)
    n_ch = per_w // SC_CHUNK
    assert per_w * nc * ns == n_rows and per_w % SC_CHUNK == 0
    mesh = plsc.VectorSubcoreMesh(core_axis_name="c", subcore_axis_name="s")

    @functools.partial(
        pl.kernel, mesh=mesh,
        out_type=jax.ShapeDtypeStruct((n_rows, PACKED), U32),
        scratch_types=[pltpu.VMEM((per_w,), jnp.int32),
                       pltpu.VMEM((SC_BUFS, SC_CHUNK, PACKED), U32),
                       pltpu.SemaphoreType.DMA((SC_BUFS,)),
                       pltpu.SemaphoreType.DMA((SC_BUFS,))])
    def gather(table_hbm, idx_hbm, out_hbm, idx_v, rows_v, gsem, wsem):
        wid = lax.axis_index("s") * nc + lax.axis_index("c")
        base = pl.multiple_of(wid * per_w, per_w)
        pltpu.sync_copy(idx_hbm.at[pl.ds(base, per_w)], idx_v)

        def read(c):
            b = c % SC_BUFS
            return [pltpu.make_async_copy(table_hbm.at[idx_v.at[pl.ds(c * SC_CHUNK, SC_CHUNK)]], rows_v.at[b], gsem.at[b])]

        def write(c):
            b = c % SC_BUFS
            return [pltpu.make_async_copy(rows_v.at[b], out_hbm.at[pl.ds(base + c * SC_CHUNK, SC_CHUNK)], wsem.at[b])]
        _sc_ring(n_ch, read, write)

    return gather(yb, dest.reshape(n_rows)).reshape(nt, TOP_K, tm, PACKED)


def _padfill_kernel(cnt_ref, pstart, xs_in, xs_ref, zeros, zsem):
    del xs_in

    def pad_copies(fn):
        for e in range(N_EXPERTS):
            cnt = cnt_ref[0, e]
            head = (-cnt) & (SUBLANES - 1)
            rest = ((-cnt) & (MOE_BLOCK - 1)) - head
            off = pstart[e] + cnt
            for k in range(SUBLANES - 1):
                @pl.when(k < head)
                def _(off=off, k=k):
                    fn(pltpu.make_async_copy(zeros.at[pl.ds(0, 1), :], xs_ref.at[pl.ds(off + k, 1), :], zsem))
            off = off + head
            for bit in PAD_BITS:
                @pl.when((rest & bit) != 0)
                def _(off=off, bit=bit):
                    fn(pltpu.make_async_copy(zeros.at[pl.ds(0, bit), :],
                                             xs_ref.at[pl.ds(pl.multiple_of(off, SUBLANES), bit), :], zsem))
                off = off + (rest & bit)

    zeros[...] = jnp.zeros_like(zeros)
    pad_copies(lambda cp: cp.start())
    pad_copies(lambda cp: cp.wait())


def _padfill(xs, pstart, cnt):
    smem = pl.BlockSpec(memory_space=pltpu.SMEM)
    hbm = pl.BlockSpec(memory_space=pl.ANY)
    return pl.pallas_call(
        _padfill_kernel,
        in_specs=[smem, smem, hbm],
        out_specs=hbm,
        out_shape=jax.ShapeDtypeStruct(xs.shape, xs.dtype),
        input_output_aliases={2: 0},
        scratch_shapes=[pltpu.VMEM((MOE_BLOCK // 2, PACKED), U32), pltpu.SemaphoreType.DMA(())],
        name="padfill",
    )(cnt, pstart, xs)


W_SLOTS = 3
W_AHEAD = W_SLOTS - 1
EXPERT_GROUP = 8
EXPERT_RUNS = (1, 2, 4)


def _expert_kernel(be_ref, nu_ref, ge_ref, x_ref, wg_hbm, wu_hbm, wd_hbm, o_ref,
                   wgf, wuf, wdf, grp_ref, sems):
    step = pl.program_id(0)

    def weight_copies(e, slot):
        return (pltpu.make_async_copy(wg_hbm.at[e], wgf.at[slot], sems.at[slot, 0]),
                pltpu.make_async_copy(wu_hbm.at[e], wuf.at[slot], sems.at[slot, 1]),
                pltpu.make_async_copy(wd_hbm.at[e], wdf.at[slot], sems.at[slot, 2]))

    @pl.when(step == 0)
    def _():
        grp_ref[0] = 0
        for a in range(W_AHEAD):
            @pl.when(ge_ref[a] >= 0)
            def _(a=a):
                for cp in weight_copies(ge_ref[a], a):
                    cp.start()

    n_blocks = be_ref.shape[0]
    n_used = nu_ref[0]

    def swiglu(s, n, slot):
        rows = pl.ds(pl.multiple_of(s * MOE_BLOCK, MOE_BLOCK), n * MOE_BLOCK)
        lo, hi = _unpack_rows(x_ref[rows, :])
        lo = lo.astype(BF16)
        hi = hi.astype(BF16)
        g = (jnp.dot(lo, wgf[slot, :PACKED, :], preferred_element_type=F32)
             + jnp.dot(hi, wgf[slot, PACKED:, :], preferred_element_type=F32))
        u = (jnp.dot(lo, wuf[slot, :PACKED, :], preferred_element_type=F32)
             + jnp.dot(hi, wuf[slot, PACKED:, :], preferred_element_type=F32))
        h = (g * _sigmoid(g) * u).astype(BF16)
        o_ref[rows, :] = _pack_rows(jnp.dot(h, wdf[slot], preferred_element_type=F32))

    def run(s):
        j = step * EXPERT_GROUP + s
        e = be_ref[j]
        first = jnp.logical_or(j == 0, e != be_ref[jnp.maximum(j - 1, 0)])

        @pl.when(first)
        def _():
            grp = grp_ref[0]
            slot = grp % W_SLOTS
            for cp in weight_copies(e, slot):
                cp.wait()
            nxt = ge_ref[grp + W_AHEAD]

            @pl.when(nxt >= 0)
            def _():
                for cp in weight_copies(nxt, (grp + W_AHEAD) % W_SLOTS):
                    cp.start()
            grp_ref[0] = grp + 1

        def same(k):
            return (s + k < EXPERT_GROUP) & (j + k < n_used) & (be_ref[jnp.minimum(j + k, n_blocks - 1)] == e)
        take = jnp.int32(1)
        for n in EXPERT_RUNS[1:]:
            ok = same(n - 1)
            for k in range(1, n - 1):
                ok = ok & same(k)
            take = jnp.where(ok, n, take)
        slot = (grp_ref[0] + W_SLOTS - 1) % W_SLOTS
        for n in EXPERT_RUNS:
            @pl.when(take == n)
            def _(n=n):
                swiglu(s, n, slot)
        return s + take

    lax.while_loop(lambda s: (s < EXPERT_GROUP) & (step * EXPERT_GROUP + s < n_used), run, jnp.int32(0))


def _experts(xs, block_expert, n_used, group_expert, w_gate, w_up, w_down):
    cap = xs.shape[0]
    n_blocks = cap // MOE_BLOCK
    assert n_blocks % EXPERT_GROUP == 0
    rows = EXPERT_GROUP * MOE_BLOCK
    last = lambda j, be, nu, ge: jnp.minimum(j, (nu[0] - 1) // EXPERT_GROUP)
    hbm = pl.BlockSpec(memory_space=pl.ANY)
    gs = pltpu.PrefetchScalarGridSpec(
        num_scalar_prefetch=3,
        grid=(n_blocks // EXPERT_GROUP,),
        in_specs=[pl.BlockSpec((rows, PACKED), lambda j, be, nu, ge: (last(j, be, nu, ge), 0)), hbm, hbm, hbm],
        out_specs=pl.BlockSpec((rows, PACKED), lambda j, be, nu, ge: (last(j, be, nu, ge), 0)),
        scratch_shapes=[pltpu.VMEM((W_SLOTS, D_MODEL, D_EXPERT), F32),
                        pltpu.VMEM((W_SLOTS, D_MODEL, D_EXPERT), F32),
                        pltpu.VMEM((W_SLOTS, D_EXPERT, D_MODEL), F32),
                        pltpu.SMEM((1,), jnp.int32),
                        pltpu.SemaphoreType.DMA((W_SLOTS, 3))],
    )
    return pl.pallas_call(
        _expert_kernel,
        grid_spec=gs,
        out_shape=jax.ShapeDtypeStruct((cap, PACKED), U32),
        compiler_params=_cparams(1),
        name="experts",
    )(block_expert, n_used, group_expert, xs, w_gate, w_up, w_down)


CB_TM = RT_TM


def _combine_kernel(x1_ref, gt_ref, y2_ref, o_ref):
    g = gt_ref[...]
    lo1, hi1 = _unpack_rows(y2_ref[0, 0])
    lo2, hi2 = _unpack_rows(y2_ref[0, 1])
    o_ref[:, :PACKED] = x1_ref[:, :PACKED] + g[:, 0:1] * lo1 + g[:, 1:2] * lo2
    o_ref[:, PACKED:] = x1_ref[:, PACKED:] + g[:, 0:1] * hi1 + g[:, 1:2] * hi2


def _combine(x1, gates, y2):
    T = x1.shape[0]
    nt = T // CB_TM
    return pl.pallas_call(
        _combine_kernel,
        grid=(nt,),
        in_specs=[pl.BlockSpec((CB_TM, D_MODEL), lambda i: (i, 0)),
                  pl.BlockSpec((CB_TM, RINFO), lambda i: (i, 0)),
                  pl.BlockSpec((1, TOP_K, CB_TM, PACKED), lambda i: (i, 0, 0, 0))],
        out_specs=pl.BlockSpec((CB_TM, D_MODEL), lambda i: (i, 0)),
        out_shape=jax.ShapeDtypeStruct((T, D_MODEL), F32),
        compiler_params=_cparams(1),
        name="combine",
    )(x1, gates, y2)


def _layer(x, rel_bias, ln1, w_in, q_norm, k_norm, attn_sink, conv_w, conv_b, lru_wa, lru_ba, lru_wi, lru_bi,
           lru_lambda, out_norm_attn, out_norm_lru, w_out, ln2, w_group, b_group, w_er, b_er, w_gate, w_up, w_down):
    B, S, D = x.shape
    T = B * S
    x2 = x.reshape(T, D)
    q, kv, xr, gr = _in_proj(x2, ln1, w_in, q_norm, k_norm)
    attn_n = _attention(q.reshape(B, S, ATTN_WIDTH), kv.reshape(B, S, 2 * KV_WIDTH), rel_bias, attn_sink,
                        out_norm_attn)
    lru_n = _rglru(xr.reshape(B, S, LRU_WIDTH), gr.reshape(B, S, LRU_WIDTH), conv_w, conv_b,
                   lru_wa, lru_ba, lru_wi, lru_bi, lru_lambda)
    x1, h2, gates, ei, cnt = _out_route(attn_n.reshape(T, ATTN_WIDTH), lru_n.reshape(T, LRU_WIDTH), x2, w_out, ln2,
                                        out_norm_lru, w_group, b_group, w_er, b_er)
    cap = _moe_cap(T)
    dest, pstart, block_expert, n_used, group_expert = _layout(ei, cnt, cap // MOE_BLOCK)
    xs = _padfill(_sc_dispatch(h2, dest, cap), pstart, cnt)
    yb = _experts(xs, block_expert, n_used, group_expert, w_gate, w_up, w_down)
    out = _combine(x1, gates, _sc_gather(yb, dest))
    return out.reshape(B, S, D)


def kernel(x, rel_bias, ln1, w_in, q_norm, k_norm, attn_sink, conv_w, conv_b, lru_wa, lru_ba, lru_wi, lru_bi,
           lru_lambda, out_norm_attn, out_norm_lru, w_out, ln2, w_group, b_group, w_expert_router, b_expert_router,
           w_gate, w_up, w_down):
    depth = ln1.shape[0]
    for l in range(depth):
        x = _layer(x, rel_bias, ln1[l], w_in[l], q_norm[l], k_norm[l], attn_sink[l], conv_w[l], conv_b[l],
                   lru_wa[l], lru_ba[l], lru_wi[l], lru_bi[l], lru_lambda[l], out_norm_attn[l], out_norm_lru[l],
                   w_out[l], ln2[l], w_group[l], b_group[l], w_expert_router[l], b_expert_router[l],
                   w_gate[l], w_up[l], w_down[l])
    return x
```
